```python
import jax, jax.numpy as jnp
from jax import lax
import numpy as np

D_MODEL = 2048
BATCH = 8
SEQ = 4096
DEPTH = 1

D_MIX = D_MODEL
D_POOL = D_MIX // 2
D_LRU = D_MIX - D_POOL
POOL_WINDOWS = (2, 4, 8, 16)
N_POOL_GROUPS = len(POOL_WINDOWS)
POOL_GROUP = D_POOL // N_POOL_GROUPS
N_LRU_HEADS = 8
LRU_HEAD = D_LRU // N_LRU_HEADS
CONV_WIDTH = 4
CONV_PAD = (1, 2)
RG_C = 8.0
D_FF = 4 * D_MODEL
LN_EPS = 1e-5
DEEPNORM_ALPHA = (2 * DEPTH) ** 0.25
DEEPNORM_BETA = (8 * DEPTH) ** -0.25

kernel_name = "hybrid_pool_rglru_deepnorm_encoder"


def layer_norm(x, g, b):
    xf = x.astype(jnp.float32)
    mu = jnp.mean(xf, axis=-1, keepdims=True)
    var = jnp.mean(jnp.square(xf - mu), axis=-1, keepdims=True)
    y = (xf - mu) * lax.rsqrt(var + LN_EPS) * g.astype(jnp.float32) + b.astype(jnp.float32)
    return y.astype(x.dtype)


def window_mean(u, w):
    S = u.shape[1]
    c = jnp.pad(jnp.cumsum(u.astype(jnp.float32), axis=1), ((0, 0), (1, 0), (0, 0)))
    t = jnp.arange(S)
    lo = jnp.clip(t - w // 2, 0, S)
    hi = jnp.clip(t + w // 2, 0, S)
    s = jnp.take(c, hi, axis=1) - jnp.take(c, lo, axis=1)
    cnt = (hi - lo).astype(jnp.float32)
    return s / cnt[None, :, None]


def multiscale_pool(u, w_pool, pool_scale):
    B, S, _ = u.shape
    ug = u.reshape(B, S, N_POOL_GROUPS, POOL_GROUP)
    means = jnp.stack([window_mean(ug[:, :, g], w) for g, w in enumerate(POOL_WINDOWS)], axis=2)
    d = (means - ug.astype(jnp.float32)).astype(u.dtype)
    out = jnp.einsum('bsgi,gio->bsgo', d, w_pool).reshape(B, S, D_POOL)
    return out * pool_scale


def depthwise_conv(u, conv_w, conv_b):
    y = lax.conv_general_dilated(u, conv_w, window_strides=(1,), padding=[CONV_PAD],
                                 dimension_numbers=('NWC', 'WIO', 'NWC'),
                                 feature_group_count=u.shape[-1])
    return y + conv_b


def _combine(c1, c2):
    a1, b1 = c1
    a2, b2 = c2
    return a1 * a2, a2 * b1 + b2


def linear_scan(a, b, reverse):
    return lax.associative_scan(_combine, (a, b), reverse=reverse, axis=1)[1]


def rg_lru_bidirectional(xc, w_a, b_a, w_i, b_i, lam):
    B, S, C = xc.shape
    xh = xc.reshape(B, S, N_LRU_HEADS, LRU_HEAD)
    pre_r = jnp.einsum('bshi,nhio->nbsho', xh, w_a).reshape(2, B, S, C)
    pre_i = jnp.einsum('bshi,nhio->nbsho', xh, w_i).reshape(2, B, S, C)
    r = jax.nn.sigmoid(pre_r.astype(jnp.float32) + b_a.astype(jnp.float32)[:, None, None, :])
    i = jax.nn.sigmoid(pre_i.astype(jnp.float32) + b_i.astype(jnp.float32)[:, None, None, :])
    log_a = -RG_C * r * jax.nn.softplus(-lam.astype(jnp.float32))[:, None, None, :]
    a = jnp.exp(log_a)
    mult = jnp.sqrt(-jnp.expm1(2.0 * log_a))
    b = mult * i * xc.astype(jnp.float32)[None]
    h_fwd = linear_scan(a[0], b[0], False)
    h_bwd = linear_scan(a[1], b[1], True)
    return (h_fwd + h_bwd).astype(xc.dtype)


def hybrid_mixer(x, w_in, w_pool, pool_scale, conv_w, conv_b,
                 w_rg_a, b_rg_a, w_rg_i, b_rg_i, rg_lambda, w_out):
    proj = jnp.einsum('bsd,de->bse', x, w_in)
    u_pool = proj[..., :D_POOL]
    u_rec = proj[..., D_POOL:D_POOL + D_LRU]
    u_gate = proj[..., D_POOL + D_LRU:]
    y_pool = multiscale_pool(u_pool, w_pool, pool_scale)
    xc = depthwise_conv(u_rec, conv_w, conv_b)
    h = rg_lru_bidirectional(xc, w_rg_a, b_rg_a, w_rg_i, b_rg_i, rg_lambda)
    y_rec = h * jax.nn.gelu(u_gate)
    y = jnp.concatenate([y_pool, y_rec], axis=-1)
    return jnp.einsum('bse,ed->bsd', y, w_out)


def squared_relu_mlp(x, w1, w2):
    h = jnp.square(jax.nn.relu(jnp.einsum('bsd,df->bsf', x, w1)))
    return jnp.einsum('bsf,fd->bsd', h, w2)


def _fwd_setup_inputs(seed: int = 0) -> dict:
    key = jax.random.key(seed)
    ks = jax.random.split(key, 20)
    f32 = jnp.float32

    def nrm(k, shape, scale):
        return jax.random.normal(k, shape, f32) * scale

    x = jax.random.normal(ks[0], (BATCH, SEQ, D_MODEL), f32)
    ln_mix_g = 1.0 + nrm(ks[1], (DEPTH, D_MODEL), 0.02)
    ln_mix_b = nrm(ks[2], (DEPTH, D_MODEL), 0.02)
    w_in = nrm(ks[3], (DEPTH, D_MODEL, D_POOL + 2 * D_LRU), D_MODEL ** -0.5)
    w_pool = nrm(ks[4], (DEPTH, N_POOL_GROUPS, POOL_GROUP, POOL_GROUP), POOL_GROUP ** -0.5)
    pool_scale = 1.0 + nrm(ks[5], (DEPTH, D_POOL), 0.1)
    conv_w = nrm(ks[6], (DEPTH, CONV_WIDTH, 1, D_LRU), CONV_WIDTH ** -0.5)
    conv_b = nrm(ks[7], (DEPTH, D_LRU), 0.02)
    w_rg_a = nrm(ks[8], (DEPTH, 2, N_LRU_HEADS, LRU_HEAD, LRU_HEAD), LRU_HEAD ** -0.5)
    b_rg_a = nrm(ks[9], (DEPTH, 2, D_LRU), 0.02)
    w_rg_i = nrm(ks[10], (DEPTH, 2, N_LRU_HEADS, LRU_HEAD, LRU_HEAD), LRU_HEAD ** -0.5)
    b_rg_i = nrm(ks[11], (DEPTH, 2, D_LRU), 0.02)
    u = jax.random.uniform(ks[12], (DEPTH, 2, D_LRU), f32, minval=0.9, maxval=0.999)
    s = u ** (1.0 / RG_C)
    rg_lambda = jnp.log(s) - jnp.log1p(-s)
    w_out = nrm(ks[13], (DEPTH, D_MIX, D_MODEL), D_MIX ** -0.5 * DEEPNORM_BETA)
    ln_ffn_g = 1.0 + nrm(ks[14], (DEPTH, D_MODEL), 0.02)
    ln_ffn_b = nrm(ks[15], (DEPTH, D_MODEL), 0.02)
    w_mlp_in = nrm(ks[16], (DEPTH, D_MODEL, D_FF), D_MODEL ** -0.5 * DEEPNORM_BETA)
    w_mlp_out = nrm(ks[17], (DEPTH, D_FF, D_MODEL), D_FF ** -0.5 * DEEPNORM_BETA)
    return {"x": x, "ln_mix_g": ln_mix_g, "ln_mix_b": ln_mix_b, "w_in": w_in,
            "w_pool": w_pool, "pool_scale": pool_scale, "conv_w": conv_w, "conv_b": conv_b,
            "w_rg_a": w_rg_a, "b_rg_a": b_rg_a, "w_rg_i": w_rg_i, "b_rg_i": b_rg_i,
            "rg_lambda": rg_lambda, "w_out": w_out, "ln_ffn_g": ln_ffn_g, "ln_ffn_b": ln_ffn_b,
            "w_mlp_in": w_mlp_in, "w_mlp_out": w_mlp_out}


def _fwd_reference(x, ln_mix_g, ln_mix_b, w_in, w_pool, pool_scale, conv_w, conv_b,
              w_rg_a, b_rg_a, w_rg_i, b_rg_i, rg_lambda, w_out, ln_ffn_g, ln_ffn_b,
              w_mlp_in, w_mlp_out):
    for l in range(DEPTH):
        mix = hybrid_mixer(x, w_in[l], w_pool[l], pool_scale[l], conv_w[l], conv_b[l],
                           w_rg_a[l], b_rg_a[l], w_rg_i[l], b_rg_i[l], rg_lambda[l], w_out[l])
        x = layer_norm(DEEPNORM_ALPHA * x + mix, ln_mix_g[l], ln_mix_b[l])
        ffn = squared_relu_mlp(x, w_mlp_in[l], w_mlp_out[l])
        x = layer_norm(DEEPNORM_ALPHA * x + ffn, ln_ffn_g[l], ln_ffn_b[l])
    return x


import jax as _jax
import jax.numpy as _jnp

TWIN_FORMAT = 'train_step'
FWD_PARAMS = ['x', 'ln_mix_g', 'ln_mix_b', 'w_in', 'w_pool', 'pool_scale', 'conv_w', 'conv_b', 'w_rg_a', 'b_rg_a', 'w_rg_i', 'b_rg_i', 'rg_lambda', 'w_out', 'ln_ffn_g', 'ln_ffn_b', 'w_mlp_in', 'w_mlp_out']
TWIN_WEIGHTS = ['ln_mix_g', 'ln_mix_b', 'w_in', 'w_pool', 'pool_scale', 'conv_w', 'conv_b', 'w_rg_a', 'b_rg_a', 'w_rg_i', 'b_rg_i', 'rg_lambda', 'w_out', 'ln_ffn_g', 'ln_ffn_b', 'w_mlp_in', 'w_mlp_out']
TWIN_DIFF_INPUT = 'x'
TWIN_INPUTS = ['x', 'ln_mix_g', 'ln_mix_b', 'w_in', 'w_pool', 'pool_scale', 'conv_w', 'conv_b', 'w_rg_a', 'b_rg_a', 'w_rg_i', 'b_rg_i', 'rg_lambda', 'w_out', 'ln_ffn_g', 'ln_ffn_b', 'w_mlp_in', 'w_mlp_out', 'loss_target', 'm_ln_mix_g', 'm_ln_mix_b', 'm_w_in', 'm_w_pool', 'm_pool_scale', 'm_conv_w', 'm_conv_b', 'm_w_rg_a', 'm_b_rg_a', 'm_w_rg_i', 'm_b_rg_i', 'm_rg_lambda', 'm_w_out', 'm_ln_ffn_g', 'm_ln_ffn_b', 'm_w_mlp_in', 'm_w_mlp_out', 'v_ln_mix_g', 'v_ln_mix_b', 'v_w_in', 'v_w_pool', 'v_pool_scale', 'v_conv_w', 'v_conv_b', 'v_w_rg_a', 'v_b_rg_a', 'v_w_rg_i', 'v_b_rg_i', 'v_rg_lambda', 'v_w_out', 'v_ln_ffn_g', 'v_ln_ffn_b', 'v_w_mlp_in', 'v_w_mlp_out']
TWIN_OUTPUTS = ['loss', 'grad_x', 'grad_ln_mix_g', 'grad_ln_mix_b', 'grad_w_in', 'grad_w_pool', 'grad_pool_scale', 'grad_conv_w', 'grad_conv_b', 'grad_w_rg_a', 'grad_b_rg_a', 'grad_w_rg_i', 'grad_b_rg_i', 'grad_rg_lambda', 'grad_w_out', 'grad_ln_ffn_g', 'grad_ln_ffn_b', 'grad_w_mlp_in', 'grad_w_mlp_out', 'delta_ln_mix_g', 'delta_ln_mix_b', 'delta_w_in', 'delta_w_pool', 'delta_pool_scale', 'delta_conv_w', 'delta_conv_b', 'delta_w_rg_a', 'delta_b_rg_a', 'delta_w_rg_i', 'delta_b_rg_i', 'delta_rg_lambda', 'delta_w_out', 'delta_ln_ffn_g', 'delta_ln_ffn_b', 'delta_w_mlp_in', 'delta_w_mlp_out', 'new_m_ln_mix_g', 'new_m_ln_mix_b', 'new_m_w_in', 'new_m_w_pool', 'new_m_pool_scale', 'new_m_conv_w', 'new_m_conv_b', 'new_m_w_rg_a', 'new_m_b_rg_a', 'new_m_w_rg_i', 'new_m_b_rg_i', 'new_m_rg_lambda', 'new_m_w_out', 'new_m_ln_ffn_g', 'new_m_ln_ffn_b', 'new_m_w_mlp_in', 'new_m_w_mlp_out', 'new_v_ln_mix_g', 'new_v_ln_mix_b', 'new_v_w_in', 'new_v_w_pool', 'new_v_pool_scale', 'new_v_conv_w', 'new_v_conv_b', 'new_v_w_rg_a', 'new_v_b_rg_a', 'new_v_w_rg_i', 'new_v_b_rg_i', 'new_v_rg_lambda', 'new_v_w_out', 'new_v_ln_ffn_g', 'new_v_ln_ffn_b', 'new_v_w_mlp_in', 'new_v_w_mlp_out']
TWIN_LEAF_KINDS = {'loss': 'loss', 'grad_x': 'grad_x', 'grad_ln_mix_g': 'grad_w', 'grad_ln_mix_b': 'grad_w', 'grad_w_in': 'grad_w', 'grad_w_pool': 'grad_w', 'grad_pool_scale': 'grad_w', 'grad_conv_w': 'grad_w', 'grad_conv_b': 'grad_w', 'grad_w_rg_a': 'grad_w', 'grad_b_rg_a': 'grad_w', 'grad_w_rg_i': 'grad_w', 'grad_b_rg_i': 'grad_w', 'grad_rg_lambda': 'grad_w', 'grad_w_out': 'grad_w', 'grad_ln_ffn_g': 'grad_w', 'grad_ln_ffn_b': 'grad_w', 'grad_w_mlp_in': 'grad_w', 'grad_w_mlp_out': 'grad_w', 'delta_ln_mix_g': 'delta_w', 'delta_ln_mix_b': 'delta_w', 'delta_w_in': 'delta_w', 'delta_w_pool': 'delta_w', 'delta_pool_scale': 'delta_w', 'delta_conv_w': 'delta_w', 'delta_conv_b': 'delta_w', 'delta_w_rg_a': 'delta_w', 'delta_b_rg_a': 'delta_w', 'delta_w_rg_i': 'delta_w', 'delta_b_rg_i': 'delta_w', 'delta_rg_lambda': 'delta_w', 'delta_w_out': 'delta_w', 'delta_ln_ffn_g': 'delta_w', 'delta_ln_ffn_b': 'delta_w', 'delta_w_mlp_in': 'delta_w', 'delta_w_mlp_out': 'delta_w', 'new_m_ln_mix_g': 'new_m', 'new_m_ln_mix_b': 'new_m', 'new_m_w_in': 'new_m', 'new_m_w_pool': 'new_m', 'new_m_pool_scale': 'new_m', 'new_m_conv_w': 'new_m', 'new_m_conv_b': 'new_m', 'new_m_w_rg_a': 'new_m', 'new_m_b_rg_a': 'new_m', 'new_m_w_rg_i': 'new_m', 'new_m_b_rg_i': 'new_m', 'new_m_rg_lambda': 'new_m', 'new_m_w_out': 'new_m', 'new_m_ln_ffn_g': 'new_m', 'new_m_ln_ffn_b': 'new_m', 'new_m_w_mlp_in': 'new_m', 'new_m_w_mlp_out': 'new_m', 'new_v_ln_mix_g': 'new_v', 'new_v_ln_mix_b': 'new_v', 'new_v_w_in': 'new_v', 'new_v_w_pool': 'new_v', 'new_v_pool_scale': 'new_v', 'new_v_conv_w': 'new_v', 'new_v_conv_b': 'new_v', 'new_v_w_rg_a': 'new_v', 'new_v_b_rg_a': 'new_v', 'new_v_w_rg_i': 'new_v', 'new_v_b_rg_i': 'new_v', 'new_v_rg_lambda': 'new_v', 'new_v_w_out': 'new_v', 'new_v_ln_ffn_g': 'new_v', 'new_v_ln_ffn_b': 'new_v', 'new_v_w_mlp_in': 'new_v', 'new_v_w_mlp_out': 'new_v'}


def _forward(args):
    return _fwd_reference(*[args[k] for k in FWD_PARAMS])


def _output_shape():
    def fwd():
        inp = _fwd_setup_inputs(0)
        return _fwd_reference(*[inp[k] for k in FWD_PARAMS])
    out = _jax.eval_shape(fwd)
    return out.shape, out.dtype

N_MICROBATCH = 1
ADAM_LR = 0.001
ADAM_B1 = 0.9
ADAM_B2 = 0.999
ADAM_EPS = 1e-08
ADAM_WD = 0.01
ADAM_STEP = 10
PER_EXAMPLE_BATCH_AXIS = {'x': 0, 'loss_target': 0}
SHARED_INPUTS = []
_WEIGHT_DTYPES = {'ln_mix_g': _jnp.float32, 'ln_mix_b': _jnp.float32, 'w_in': _jnp.float32, 'w_pool': _jnp.float32, 'pool_scale': _jnp.float32, 'conv_w': _jnp.float32, 'conv_b': _jnp.float32, 'w_rg_a': _jnp.float32, 'b_rg_a': _jnp.float32, 'w_rg_i': _jnp.float32, 'b_rg_i': _jnp.float32, 'rg_lambda': _jnp.float32, 'w_out': _jnp.float32, 'ln_ffn_g': _jnp.float32, 'ln_ffn_b': _jnp.float32, 'w_mlp_in': _jnp.float32, 'w_mlp_out': _jnp.float32}
MOMENT_SCALE = {'ln_mix_g': 5.734009e-01, 'ln_mix_b': 2.804782e-01, 'w_in': 2.952626e-02, 'w_pool': 3.693059e-02, 'pool_scale': 3.710622e-02, 'conv_w': 2.747398e-02, 'conv_b': 5.346079e-01, 'w_rg_a': 6.854341e-03, 'b_rg_a': 5.136600e-03, 'w_rg_i': 1.241745e-02, 'b_rg_i': 6.432168e-03, 'rg_lambda': 9.425751e-03, 'w_out': 5.543016e-02, 'ln_ffn_g': 1.599607e+01, 'ln_ffn_b': 1.583469e+00, 'w_mlp_in': 1.827158e-02, 'w_mlp_out': 4.646896e-02}


def _to_microbatches(a, axis):
    t = _jnp.moveaxis(a, axis, 0)
    t = t.reshape((N_MICROBATCH, t.shape[0] // N_MICROBATCH) + t.shape[1:])
    return _jnp.moveaxis(t, 1, axis + 1)


def setup_inputs(seed: int = 0) -> dict:
    inp = _fwd_setup_inputs(seed)
    key = _jax.random.fold_in(_jax.random.key(seed), 7919)
    shape, _ = _output_shape()
    out = dict(inp)
    out["loss_target"] = _jax.random.normal(_jax.random.fold_in(key, 0), shape, _jnp.float32)
    for i, name in enumerate(TWIN_WEIGHTS):
        w = inp[name].astype(_jnp.float32)
        if MOMENT_SCALE is None:
            s = _jnp.sqrt(_jnp.mean(_jnp.square(w)) + 1e-30)
        else:
            s = MOMENT_SCALE[name]
        km, kv = _jax.random.split(_jax.random.fold_in(key, i + 1))
        out[name] = w
        out["m_" + name] = s * _jax.random.normal(km, w.shape, _jnp.float32)
        out["v_" + name] = (s * s) * _jax.random.uniform(kv, w.shape, _jnp.float32, 0.5, 1.5)
    if N_MICROBATCH > 1:
        for name, axis in PER_EXAMPLE_BATCH_AXIS.items():
            out[name] = _to_microbatches(out[name], axis)
    return {'x': out['x'], 'ln_mix_g': out['ln_mix_g'], 'ln_mix_b': out['ln_mix_b'], 'w_in': out['w_in'], 'w_pool': out['w_pool'], 'pool_scale': out['pool_scale'], 'conv_w': out['conv_w'], 'conv_b': out['conv_b'], 'w_rg_a': out['w_rg_a'], 'b_rg_a': out['b_rg_a'], 'w_rg_i': out['w_rg_i'], 'b_rg_i': out['b_rg_i'], 'rg_lambda': out['rg_lambda'], 'w_out': out['w_out'], 'ln_ffn_g': out['ln_ffn_g'], 'ln_ffn_b': out['ln_ffn_b'], 'w_mlp_in': out['w_mlp_in'], 'w_mlp_out': out['w_mlp_out'], 'loss_target': out['loss_target'], 'm_ln_mix_g': out['m_ln_mix_g'], 'm_ln_mix_b': out['m_ln_mix_b'], 'm_w_in': out['m_w_in'], 'm_w_pool': out['m_w_pool'], 'm_pool_scale': out['m_pool_scale'], 'm_conv_w': out['m_conv_w'], 'm_conv_b': out['m_conv_b'], 'm_w_rg_a': out['m_w_rg_a'], 'm_b_rg_a': out['m_b_rg_a'], 'm_w_rg_i': out['m_w_rg_i'], 'm_b_rg_i': out['m_b_rg_i'], 'm_rg_lambda': out['m_rg_lambda'], 'm_w_out': out['m_w_out'], 'm_ln_ffn_g': out['m_ln_ffn_g'], 'm_ln_ffn_b': out['m_ln_ffn_b'], 'm_w_mlp_in': out['m_w_mlp_in'], 'm_w_mlp_out': out['m_w_mlp_out'], 'v_ln_mix_g': out['v_ln_mix_g'], 'v_ln_mix_b': out['v_ln_mix_b'], 'v_w_in': out['v_w_in'], 'v_w_pool': out['v_w_pool'], 'v_pool_scale': out['v_pool_scale'], 'v_conv_w': out['v_conv_w'], 'v_conv_b': out['v_conv_b'], 'v_w_rg_a': out['v_w_rg_a'], 'v_b_rg_a': out['v_b_rg_a'], 'v_w_rg_i': out['v_w_rg_i'], 'v_b_rg_i': out['v_b_rg_i'], 'v_rg_lambda': out['v_rg_lambda'], 'v_w_out': out['v_w_out'], 'v_ln_ffn_g': out['v_ln_ffn_g'], 'v_ln_ffn_b': out['v_ln_ffn_b'], 'v_w_mlp_in': out['v_w_mlp_in'], 'v_w_mlp_out': out['v_w_mlp_out']}


def _loss(weights, diff, rest, loss_target):
    with _jax.named_scope("forward"):
        args = {**rest, TWIN_DIFF_INPUT: diff, **{k: w.astype(_WEIGHT_DTYPES[k]) for k, w in weights.items()}}
        y = _forward(args)
    with _jax.named_scope("loss_head"):
        err = _jnp.square(y.astype(_jnp.float32) - loss_target)
        return 0.5 * _jnp.sum(_jnp.mean(err, axis=-1)) if err.ndim else 0.5 * err


def _adamw(w, g, m, v):
    m = ADAM_B1 * m + (1.0 - ADAM_B1) * g
    v = ADAM_B2 * v + (1.0 - ADAM_B2) * _jnp.square(g)
    m_hat = m / (1.0 - ADAM_B1 ** ADAM_STEP)
    v_hat = v / (1.0 - ADAM_B2 ** ADAM_STEP)
    delta = -ADAM_LR * (m_hat / (_jnp.sqrt(v_hat) + ADAM_EPS) + ADAM_WD * w)
    return delta, m, v


def reference(x, ln_mix_g, ln_mix_b, w_in, w_pool, pool_scale, conv_w, conv_b, w_rg_a, b_rg_a, w_rg_i, b_rg_i, rg_lambda, w_out, ln_ffn_g, ln_ffn_b, w_mlp_in, w_mlp_out, loss_target, m_ln_mix_g, m_ln_mix_b, m_w_in, m_w_pool, m_pool_scale, m_conv_w, m_conv_b, m_w_rg_a, m_b_rg_a, m_w_rg_i, m_b_rg_i, m_rg_lambda, m_w_out, m_ln_ffn_g, m_ln_ffn_b, m_w_mlp_in, m_w_mlp_out, v_ln_mix_g, v_ln_mix_b, v_w_in, v_w_pool, v_pool_scale, v_conv_w, v_conv_b, v_w_rg_a, v_b_rg_a, v_w_rg_i, v_b_rg_i, v_rg_lambda, v_w_out, v_ln_ffn_g, v_ln_ffn_b, v_w_mlp_in, v_w_mlp_out):
    given = dict(x=x, ln_mix_g=ln_mix_g, ln_mix_b=ln_mix_b, w_in=w_in, w_pool=w_pool, pool_scale=pool_scale, conv_w=conv_w, conv_b=conv_b, w_rg_a=w_rg_a, b_rg_a=b_rg_a, w_rg_i=w_rg_i, b_rg_i=b_rg_i, rg_lambda=rg_lambda, w_out=w_out, ln_ffn_g=ln_ffn_g, ln_ffn_b=ln_ffn_b, w_mlp_in=w_mlp_in, w_mlp_out=w_mlp_out, loss_target=loss_target, m_ln_mix_g=m_ln_mix_g, m_ln_mix_b=m_ln_mix_b, m_w_in=m_w_in, m_w_pool=m_w_pool, m_pool_scale=m_pool_scale, m_conv_w=m_conv_w, m_conv_b=m_conv_b, m_w_rg_a=m_w_rg_a, m_b_rg_a=m_b_rg_a, m_w_rg_i=m_w_rg_i, m_b_rg_i=m_b_rg_i, m_rg_lambda=m_rg_lambda, m_w_out=m_w_out, m_ln_ffn_g=m_ln_ffn_g, m_ln_ffn_b=m_ln_ffn_b, m_w_mlp_in=m_w_mlp_in, m_w_mlp_out=m_w_mlp_out, v_ln_mix_g=v_ln_mix_g, v_ln_mix_b=v_ln_mix_b, v_w_in=v_w_in, v_w_pool=v_w_pool, v_pool_scale=v_pool_scale, v_conv_w=v_conv_w, v_conv_b=v_conv_b, v_w_rg_a=v_w_rg_a, v_b_rg_a=v_b_rg_a, v_w_rg_i=v_w_rg_i, v_b_rg_i=v_b_rg_i, v_rg_lambda=v_rg_lambda, v_w_out=v_w_out, v_ln_ffn_g=v_ln_ffn_g, v_ln_ffn_b=v_ln_ffn_b, v_w_mlp_in=v_w_mlp_in, v_w_mlp_out=v_w_mlp_out)
    weights = {n: given[n] for n in TWIN_WEIGHTS}
    shared = {n: given[n] for n in SHARED_INPUTS}
    per_example = {n: given[n] for n in ['x']}
    grad_fn = _jax.value_and_grad(_loss, argnums=(0, 1))

    def one_microbatch(ex, loss_target):
        ex = dict(ex)
        diff = ex.pop(TWIN_DIFF_INPUT)
        return grad_fn(weights, diff, {**shared, **ex}, loss_target)

    if N_MICROBATCH == 1:
        loss, (grad_w, grad_x) = one_microbatch(per_example, given["loss_target"])
    else:
        def body(carry, xs):
            loss_sum, grad_sum = carry
            l_k, (gw_k, gx_k) = one_microbatch(xs[0], xs[1])
            with _jax.named_scope("update"):
                return (loss_sum + l_k, _jax.tree.map(_jnp.add, grad_sum, gw_k)), gx_k

        init = (_jnp.zeros((), _jnp.float32), _jax.tree.map(_jnp.zeros_like, weights))
        (loss, grad_w), grad_x = _jax.lax.scan(body, init, (per_example, given["loss_target"]))
    with _jax.named_scope("update"):
        delta_w, new_m, new_v = {}, {}, {}
        for n in TWIN_WEIGHTS:
            delta_w[n], new_m[n], new_v[n] = _adamw(weights[n], grad_w[n], given["m_" + n], given["v_" + n])
    return (loss, grad_x, *[grad_w[n] for n in TWIN_WEIGHTS], *[delta_w[n] for n in TWIN_WEIGHTS],
            *[new_m[n] for n in TWIN_WEIGHTS], *[new_v[n] for n in TWIN_WEIGHTS])
```

```python
import functools

import jax
import jax.numpy as jnp
from jax import lax
from jax.experimental import pallas as pl
from jax.experimental.pallas import tpu as pltpu

F32 = jnp.float32
BF16 = jnp.bfloat16

N_CHIPS = 4
LANES = 128
SUBLANES = 8
LRU_HEAD = 128
N_POOL_GROUPS = 4
POOL_WINDOWS = (2, 4, 8, 16)
RG_C = 8.0
LN_EPS = 1e-5
ALPHA = 2.0 ** 0.25
ADAM_LR, ADAM_B1, ADAM_B2, ADAM_EPS, ADAM_WD, ADAM_STEP = 0.001, 0.9, 0.999, 1e-08, 0.01, 10
VMEM_LIMIT = 56 * 1024 * 1024
SEQ_TILE = 256
MM_TILE = 512
ELT_BLOCK_BYTES = 2 * 1024 * 1024
MESH = pl.DeviceIdType.MESH
ANY = pl.BlockSpec(memory_space=pl.ANY)


def _params(*sem):
    return pltpu.CompilerParams(dimension_semantics=sem, vmem_limit_bytes=VMEM_LIMIT)


def _sigmoid(z):
    return 1.0 / (1.0 + jnp.exp(-z))


def _neg_expm1(z):
    series = -(z * (1.0 + z * (0.5 + z * (1.0 / 6.0 + z * (1.0 / 24.0)))))
    return jnp.where(z > -0.01, series, 1.0 - jnp.exp(z))


def _softplus(z):
    return jnp.maximum(z, 0.0) + jnp.log1p(jnp.exp(-jnp.abs(z)))


_GELU_K = 0.7978845608028654
_GELU_C = 0.044715


def _gelu_and_grad(u):
    t = jnp.tanh(_GELU_K * (u + _GELU_C * (u * u * u)))
    g = 0.5 * u * (1.0 + t)
    dg = 0.5 * (1.0 + t) + 0.5 * u * (1.0 - t * t) * (_GELU_K * (1.0 + 3.0 * _GELU_C * u * u))
    return g, dg


def _shift_rows(prv, cur, nxt, o, rows):
    if o == 0:
        return cur
    if o == SUBLANES:
        return nxt
    if o == -SUBLANES:
        return prv
    if o > 0:
        s = SUBLANES - o
        return jnp.where(rows < s, pltpu.roll(cur, s, 0), pltpu.roll(nxt, s, 0))
    p = -o
    return jnp.where(rows >= p, pltpu.roll(cur, p, 0), pltpu.roll(prv, p, 0))


def _neighbour_chunks(main_ref, prev_ref, next_ref, r0, t_rows, cols, first_tile, last_tile):
    cur = main_ref[pl.ds(r0, SUBLANES), cols]
    before = main_ref[pl.ds(pl.multiple_of(jnp.maximum(r0 - SUBLANES, 0), SUBLANES), SUBLANES), cols]
    after = main_ref[pl.ds(pl.multiple_of(jnp.minimum(r0 + SUBLANES, t_rows - SUBLANES), SUBLANES), SUBLANES), cols]
    halo_prev = jnp.where(first_tile, 0.0, prev_ref[:, cols])
    halo_next = jnp.where(last_tile, 0.0, next_ref[:, cols])
    prv = jnp.where(r0 == 0, halo_prev, before)
    nxt = jnp.where(r0 == t_rows - SUBLANES, halo_next, after)
    return prv, cur, nxt


def _halo_specs(t_rows, n_rows, width, col_block):
    per = t_rows // SUBLANES
    last = n_rows // SUBLANES - 1
    return [
        pl.BlockSpec((t_rows, width), lambda i: (i, col_block)),
        pl.BlockSpec((SUBLANES, width), lambda i: (jnp.maximum(i * per - 1, 0), col_block)),
        pl.BlockSpec((SUBLANES, width), lambda i: (jnp.minimum((i + 1) * per, last), col_block)),
    ]


def _chunk_loop(t_rows, fn, init=None):
    def step(ci, carry):
        return fn(pl.multiple_of(ci * SUBLANES, SUBLANES), carry)
    return lax.fori_loop(0, t_rows // SUBLANES, step, init)


def _proj_call(x, w_in):
    s, d = x.shape
    n, _, e4 = w_in.shape
    tm = min(MM_TILE, s)

    def body(x_ref, w_ref, proj_ref, xb_ref):
        xb = x_ref[...].astype(BF16)

        @pl.when(pl.program_id(1) == 0)
        def _():
            xb_ref[...] = xb

        proj_ref[...] = jnp.dot(xb, w_ref[...], preferred_element_type=F32)

    return pl.pallas_call(
        body, name="proj", grid=(s // tm, n),
        in_specs=[pl.BlockSpec((tm, d), lambda i, j: (i, 0)), pl.BlockSpec((None, d, e4), lambda i, j: (j, 0, 0))],
        out_specs=[pl.BlockSpec((tm, e4), lambda i, j: (i, j)), pl.BlockSpec((tm, d), lambda i, j: (i, 0))],
        out_shape=[jax.ShapeDtypeStruct((s, n * e4), F32), jax.ShapeDtypeStruct((s, d), BF16)],
        compiler_params=_params("arbitrary", "arbitrary"),
    )(x, w_in)


def _conv_call(proj, conv_w, conv_b, c):
    s = proj.shape[0]
    t = min(SEQ_TILE, s)
    n_tiles = s // t

    def body(u_ref, up_ref, un_ref, w_ref, b_ref, xc_ref):
        i = pl.program_id(0)
        rows = lax.broadcasted_iota(jnp.int32, (SUBLANES, c), 0)
        w = w_ref[...]
        b = b_ref[...]

        def chunk(r0, _):
            prv, cur, nxt = _neighbour_chunks(u_ref, up_ref, un_ref, r0, t, slice(None), i == 0, i == n_tiles - 1)
            acc = b + w[1:2] * cur
            acc += w[0:1] * _shift_rows(prv, cur, nxt, -1, rows)
            acc += w[2:3] * _shift_rows(prv, cur, nxt, 1, rows)
            acc += w[3:4] * _shift_rows(prv, cur, nxt, 2, rows)
            xc_ref[pl.ds(r0, SUBLANES), :] = acc

        _chunk_loop(t, chunk)

    return pl.pallas_call(
        body, name="conv_fwd", grid=(n_tiles,),
        in_specs=_halo_specs(t, s, c, 1) + [pl.BlockSpec((4, c), lambda i: (0, 0)), pl.BlockSpec((1, c), lambda i: (0, 0))],
        out_specs=pl.BlockSpec((t, c), lambda i: (i, 0)),
        out_shape=jax.ShapeDtypeStruct((s, c), F32),
        compiler_params=_params("arbitrary"),
    )(proj, proj, proj, conv_w, conv_b)


def _gate_matmuls(xc_ref, wa_ref, wi_ref, pr_s, pi_s, heads):
    for h in range(heads):
        cs = pl.ds(h * LRU_HEAD, LRU_HEAD)
        xb = xc_ref[:, cs].astype(BF16)
        pr_s[:, cs] = jnp.dot(xb, wa_ref[h], preferred_element_type=F32)
        pi_s[:, cs] = jnp.dot(xb, wi_ref[h], preferred_element_type=F32)


def _rg_gates(pr, pi, ba, bi, sp):
    r = _sigmoid(pr + ba)
    ig = _sigmoid(pi + bi)
    log_a = (-RG_C * r) * sp
    a = jnp.exp(log_a)
    mult = jnp.sqrt(_neg_expm1(2.0 * log_a))
    return r, ig, a, mult


def _scan_fwd_call(xc, wa, wi, ba, bi, lam, reverse):
    s, c = xc.shape
    heads = c // LRU_HEAD
    t = min(SEQ_TILE, s)
    n_tiles = s // t
    tile = (lambda i: (n_tiles - 1 - i, 0)) if reverse else (lambda i: (i, 0))
    whole2 = lambda i: (0, 0)
    whole3 = lambda i: (0, 0, 0)

    def body(xc_ref, wa_ref, wi_ref, ba_ref, bi_ref, lam_ref, h_ref, pr_s, pi_s, carry_s):
        @pl.when(pl.program_id(0) == 0)
        def _():
            carry_s[...] = jnp.zeros_like(carry_s)

        _gate_matmuls(xc_ref, wa_ref, wi_ref, pr_s, pi_s, heads)
        ba_v, bi_v = ba_ref[...], bi_ref[...]
        sp = _softplus(-lam_ref[...])

        def chunk(r0, _):
            rs = pl.ds(r0, SUBLANES)
            xcv = xc_ref[rs, :]
            _, ig, a, mult = _rg_gates(pr_s[rs, :], pi_s[rs, :], ba_v, bi_v, sp)
            pr_s[rs, :] = a
            pi_s[rs, :] = mult * ig * xcv

        _chunk_loop(t, chunk)

        def row(j, h):
            r = (t - 1 - j) if reverse else j
            h = pr_s[pl.ds(r, 1), :] * h + pi_s[pl.ds(r, 1), :]
            h_ref[pl.ds(r, 1), :] = h
            return h

        carry_s[...] = lax.fori_loop(0, t, row, carry_s[...], unroll=8)

    return pl.pallas_call(
        body, name="scan_fwd_rev" if reverse else "scan_fwd", grid=(n_tiles,),
        in_specs=[pl.BlockSpec((t, c), tile),
                  pl.BlockSpec((heads, LRU_HEAD, LRU_HEAD), whole3), pl.BlockSpec((heads, LRU_HEAD, LRU_HEAD), whole3),
                  pl.BlockSpec((1, c), whole2), pl.BlockSpec((1, c), whole2), pl.BlockSpec((1, c), whole2)],
        out_specs=pl.BlockSpec((t, c), tile),
        out_shape=jax.ShapeDtypeStruct((s, c), F32),
        scratch_shapes=[pltpu.VMEM((t, c), F32), pltpu.VMEM((t, c), F32), pltpu.VMEM((1, c), F32)],
        compiler_params=_params("arbitrary"),
    )(xc, wa, wi, ba, bi, lam)


def _window_counts(r0, tile_idx, t_rows, n_rows, half, shape):
    pos = tile_idx * t_rows + r0 + lax.broadcasted_iota(jnp.int32, shape, 0)
    hi = jnp.minimum(pos + half, n_rows)
    lo = jnp.maximum(pos - half, 0)
    return (hi - lo).astype(F32)


def _pool_combine_call(proj, h_f, h_b, w_pool, pool_scale, p):
    s = proj.shape[0]
    c = h_f.shape[1]
    pg = p // N_POOL_GROUPS
    t = min(SEQ_TILE, s)
    n_tiles = s // t

    def body(u_ref, up_ref, un_ref, gate_ref, hf_ref, hb_ref, wp_ref, sc_ref, y_ref, d_ref, d_s, yr_s):
        i = pl.program_id(0)
        rows = lax.broadcasted_iota(jnp.int32, (SUBLANES, pg), 0)

        def chunk(r0, _):
            rs = pl.ds(r0, SUBLANES)
            for g, w in enumerate(POOL_WINDOWS):
                cols = pl.ds(g * pg, pg)
                prv, cur, nxt = _neighbour_chunks(u_ref, up_ref, un_ref, r0, t, cols, i == 0, i == n_tiles - 1)
                tot = cur
                for o in range(-(w // 2), w // 2):
                    if o != 0:
                        tot = tot + _shift_rows(prv, cur, nxt, o, rows)
                cnt = _window_counts(r0, i, t, s, w // 2, (SUBLANES, pg))
                d_s[rs, cols] = tot / cnt - cur
            gate, _ = _gelu_and_grad(gate_ref[rs, :])
            yr_s[rs, :] = (hf_ref[rs, :] + hb_ref[rs, :]) * gate

        _chunk_loop(t, chunk)
        y_ref[:, pl.ds(p, c)] = yr_s[...].astype(BF16)
        d_ref[...] = d_s[...].astype(BF16)
        for g in range(N_POOL_GROUPS):
            cols = pl.ds(g * pg, pg)
            out = jnp.dot(d_s[:, cols].astype(BF16), wp_ref[g], preferred_element_type=F32)
            y_ref[:, cols] = (out * sc_ref[:, cols]).astype(BF16)

    return pl.pallas_call(
        body, name="pool_combine", grid=(n_tiles,),
        in_specs=_halo_specs(t, s, p, 0) + [
            pl.BlockSpec((t, c), lambda i: (i, 2)),
            pl.BlockSpec((t, c), lambda i: (i, 0)), pl.BlockSpec((t, c), lambda i: (i, 0)),
            pl.BlockSpec((N_POOL_GROUPS, pg, pg), lambda i: (0, 0, 0)), pl.BlockSpec((1, p), lambda i: (0, 0))],
        out_specs=[pl.BlockSpec((t, p + c), lambda i: (i, 0)), pl.BlockSpec((t, p), lambda i: (i, 0))],
        out_shape=[jax.ShapeDtypeStruct((s, p + c), BF16), jax.ShapeDtypeStruct((s, p), BF16)],
        scratch_shapes=[pltpu.VMEM((t, p), F32), pltpu.VMEM((t, c), F32)],
        compiler_params=_params("arbitrary"),
    )(proj, proj, proj, proj, h_f, h_b, w_pool, pool_scale)


def _layer_norm_rows(z, g, b):
    mu = jnp.mean(z, axis=-1, keepdims=True)
    zc = z - mu
    var = jnp.mean(zc * zc, axis=-1, keepdims=True)
    rstd = lax.rsqrt(var + LN_EPS)
    xh = zc * rstd
    return xh, rstd, xh * g + b


def _layer_norm_bwd_rows(dx, xh, rstd, g):
    dxh = dx * g
    m1 = jnp.mean(dxh, axis=-1, keepdims=True)
    m2 = jnp.mean(dxh * xh, axis=-1, keepdims=True)
    return rstd * (dxh - m1 - xh * m2)


def _out_ln1_call(y, w_out, x, g1, b1):
    s, d = x.shape
    tm = min(SEQ_TILE, s)

    def body(y_ref, w_ref, x_ref, g_ref, b_ref, xh_ref, x1b_ref, rstd_ref, acc_s):
        acc_s[...] = jnp.dot(y_ref[...], w_ref[...], preferred_element_type=F32)
        g, b = g_ref[...], b_ref[...]

        def chunk(r0, _):
            rs = pl.ds(r0, SUBLANES)
            xh, rstd, x1 = _layer_norm_rows(ALPHA * x_ref[rs, :] + acc_s[rs, :], g, b)
            xh_ref[rs, :] = xh
            acc_s[rs, :] = x1
            rstd_ref[rs, :] = rstd

        _chunk_loop(tm, chunk)
        x1b_ref[...] = acc_s[...].astype(BF16)

    return pl.pallas_call(
        body, name="out_ln1", grid=(s // tm,),
        in_specs=[pl.BlockSpec((tm, d), lambda i: (i, 0)), pl.BlockSpec((d, d), lambda i: (0, 0)),
                  pl.BlockSpec((tm, d), lambda i: (i, 0)),
                  pl.BlockSpec((1, d), lambda i: (0, 0)), pl.BlockSpec((1, d), lambda i: (0, 0))],
        out_specs=[pl.BlockSpec((tm, d), lambda i: (i, 0)), pl.BlockSpec((tm, d), lambda i: (i, 0)),
                   pl.BlockSpec((tm, 1), lambda i: (i, 0))],
        out_shape=[jax.ShapeDtypeStruct((s, d), F32), jax.ShapeDtypeStruct((s, d), BF16), jax.ShapeDtypeStruct((s, 1), F32)],
        scratch_shapes=[pltpu.VMEM((tm, d), F32)],
        compiler_params=_params("arbitrary"),
    )(y, w_out, x, g1, b1)


def _mlp_in_call(x1b, w1):
    s, d = x1b.shape
    n, _, f4 = w1.shape
    tm = min(MM_TILE, s)
    tn = min(1024, f4)
    per = f4 // tn

    def body(x_ref, w_ref, r_ref, q_ref):
        r = jnp.maximum(jnp.dot(x_ref[...], w_ref[...], preferred_element_type=F32), 0.0)
        r_ref[...] = r.astype(BF16)
        q_ref[...] = (r * r).astype(BF16)

    return pl.pallas_call(
        body, name="mlp_in", grid=(n * per, s // tm),
        in_specs=[pl.BlockSpec((tm, d), lambda j, i: (i, 0)), pl.BlockSpec((None, d, tn), lambda j, i: (j // per, 0, j % per))],
        out_specs=[pl.BlockSpec((tm, tn), lambda j, i: (i, j)), pl.BlockSpec((tm, tn), lambda j, i: (i, j))],
        out_shape=[jax.ShapeDtypeStruct((s, n * f4), BF16), jax.ShapeDtypeStruct((s, n * f4), BF16)],
        compiler_params=_params("arbitrary", "arbitrary"),
    )(x1b, w1)


def _mlp_out_ln2_call(hsq, w2, xh1, g1, b1, g2, b2, target):
    s, f = hsq.shape
    d = w2.shape[1]
    tm = min(MM_TILE, s)
    tk = min(1024, f)
    nk = f // tk

    def body(h_ref, w_ref, xh1_ref, g1_ref, b1_ref, g2_ref, b2_ref, t_ref,
             dz_ref, dzb_ref, loss_ref, dg_ref, db_ref, acc_s):
        i, k = pl.program_id(0), pl.program_id(1)

        @pl.when((i == 0) & (k == 0))
        def _():
            loss_ref[...] = jnp.zeros_like(loss_ref)
            dg_ref[...] = jnp.zeros_like(dg_ref)
            db_ref[...] = jnp.zeros_like(db_ref)

        @pl.when(k == 0)
        def _():
            acc_s[...] = jnp.zeros_like(acc_s)

        acc_s[...] += jnp.dot(h_ref[...], w_ref[...], preferred_element_type=F32)

        @pl.when(k == nk - 1)
        def _():
            g1, b1, g2, b2 = g1_ref[...], b1_ref[...], g2_ref[...], b2_ref[...]

            def chunk(r0, _):
                rs = pl.ds(r0, SUBLANES)
                x1 = xh1_ref[rs, :] * g1 + b1
                xh2, rstd, x2 = _layer_norm_rows(ALPHA * x1 + acc_s[rs, :], g2, b2)
                diff = x2 - t_ref[rs, :]
                loss_ref[...] += jnp.sum(diff * diff, axis=-1, keepdims=True) * (0.5 / d)
                dx2 = diff * (1.0 / d)
                dg_ref[...] += dx2 * xh2
                db_ref[...] += dx2
                dz = _layer_norm_bwd_rows(dx2, xh2, rstd, g2)
                dz_ref[rs, :] = dz

            _chunk_loop(tm, chunk)
            dzb_ref[...] = dz_ref[...].astype(BF16)

    row = lambda i, k: (i, 0)
    vec = lambda i, k: (0, 0)
    return pl.pallas_call(
        body, name="mlp_out_ln2", grid=(s // tm, nk),
        in_specs=[pl.BlockSpec((tm, tk), lambda i, k: (i, k)), pl.BlockSpec((tk, d), lambda i, k: (k, 0)),
                  pl.BlockSpec((tm, d), row), pl.BlockSpec((1, d), vec), pl.BlockSpec((1, d), vec),
                  pl.BlockSpec((1, d), vec), pl.BlockSpec((1, d), vec), pl.BlockSpec((tm, d), row)],
        out_specs=[pl.BlockSpec((tm, d), row), pl.BlockSpec((tm, d), row),
                   pl.BlockSpec((SUBLANES, 1), vec), pl.BlockSpec((SUBLANES, d), vec), pl.BlockSpec((SUBLANES, d), vec)],
        out_shape=[jax.ShapeDtypeStruct((s, d), F32), jax.ShapeDtypeStruct((s, d), BF16),
                   jax.ShapeDtypeStruct((SUBLANES, 1), F32), jax.ShapeDtypeStruct((SUBLANES, d), F32),
                   jax.ShapeDtypeStruct((SUBLANES, d), F32)],
        scratch_shapes=[pltpu.VMEM((tm, d), F32)],
        compiler_params=_params("arbitrary", "arbitrary"),
    )(hsq, w2, xh1, g1, b1, g2, b2, target)


def _weight_grad_call(a, b, tm, tn, out_shape, out_map, name):
    s, m = a.shape
    n = b.shape[1]
    tk = min(512, s)

    def body(a_ref, b_ref, o_ref):
        @pl.when(pl.program_id(2) == 0)
        def _():
            o_ref[...] = jnp.zeros_like(o_ref)

        o_ref[...] += lax.dot_general(a_ref[...], b_ref[...], (((0,), (0,)), ((), ())), preferred_element_type=F32)

    return pl.pallas_call(
        body, name=name, grid=(m // tm, n // tn, s // tk),
        in_specs=[pl.BlockSpec((tk, tm), lambda i, j, k: (k, i)), pl.BlockSpec((tk, tn), lambda i, j, k: (k, j))],
        out_specs=pl.BlockSpec((None, None, tm, tn), lambda i, j, k: out_map(i, j)),
        out_shape=jax.ShapeDtypeStruct(out_shape, F32),
        compiler_params=_params("arbitrary", "arbitrary", "arbitrary"),
    )(a, b)


def _row_sharded_grad(a, b, name):
    m, n = a.shape[1], b.shape[1]
    half_rows = m // (2 * N_CHIPS)
    tm = min(1024, half_rows)
    per = half_rows // tm
    tn = min(1024, n)
    return _weight_grad_call(a, b, tm, tn, (2, N_CHIPS, half_rows, n),
                             lambda i, j: ((i // per) % 2, i // (2 * per), i % per, j), name)


def _col_sharded_grad(a, b, name):
    m, n = a.shape[1], b.shape[1]
    half_rows, shard_cols = m // 2, n // N_CHIPS
    tm = min(1024, half_rows)
    per_m = half_rows // tm
    tn = shard_cols if shard_cols % 1024 else 1024
    per_n = shard_cols // tn
    return _weight_grad_call(a, b, tm, tn, (2, N_CHIPS, half_rows, shard_cols),
                             lambda i, j: (i // per_m, j // per_n, i % per_m, j % per_n), name)


def _dhsq_call(dzb, w2, r):
    s, d = dzb.shape
    f = w2.shape[0]
    tm = min(MM_TILE, s)
    tn = min(1024, f)

    def body(dz_ref, w_ref, r_ref, o_ref):
        dh = lax.dot_general(dz_ref[...], w_ref[...], (((1,), (1,)), ((), ())), preferred_element_type=F32)
        o_ref[...] = (dh * (2.0 * r_ref[...].astype(F32))).astype(BF16)

    return pl.pallas_call(
        body, name="mlp_dpre", grid=(f // tn, s // tm),
        in_specs=[pl.BlockSpec((tm, d), lambda j, i: (i, 0)), pl.BlockSpec((tn, d), lambda j, i: (j, 0)),
                  pl.BlockSpec((tm, tn), lambda j, i: (i, j))],
        out_specs=pl.BlockSpec((tm, tn), lambda j, i: (i, j)),
        out_shape=jax.ShapeDtypeStruct((s, f), BF16),
        compiler_params=_params("arbitrary", "arbitrary"),
    )(dzb, w2, r)


def _dx1_ln1_bwd_call(dpre, w1, dz2, xh1, rstd1, g1):
    s, f = dpre.shape
    n, d, f4 = w1.shape
    tm = min(MM_TILE, s)
    tk = min(1024, f4)
    per = f4 // tk
    nk = n * per

    def body(dp_ref, w_ref, dz2_ref, xh_ref, rstd_ref, g_ref, dz_ref, dzb_ref, dg_ref, db_ref, acc_s):
        i, k = pl.program_id(0), pl.program_id(1)

        @pl.when((i == 0) & (k == 0))
        def _():
            dg_ref[...] = jnp.zeros_like(dg_ref)
            db_ref[...] = jnp.zeros_like(db_ref)

        @pl.when(k == 0)
        def _():
            acc_s[...] = jnp.zeros_like(acc_s)

        acc_s[...] += lax.dot_general(dp_ref[...], w_ref[...], (((1,), (1,)), ((), ())), preferred_element_type=F32)

        @pl.when(k == nk - 1)
        def _():
            g = g_ref[...]

            def chunk(r0, _):
                rs = pl.ds(r0, SUBLANES)
                dx1 = acc_s[rs, :] + ALPHA * dz2_ref[rs, :]
                xh = xh_ref[rs, :]
                dg_ref[...] += dx1 * xh
                db_ref[...] += dx1
                dz = _layer_norm_bwd_rows(dx1, xh, rstd_ref[rs, :], g)
                dz_ref[rs, :] = dz

            _chunk_loop(tm, chunk)
            dzb_ref[...] = dz_ref[...].astype(BF16)

    row = lambda i, k: (i, 0)
    vec = lambda i, k: (0, 0)
    return pl.pallas_call(
        body, name="dx1_ln1_bwd", grid=(s // tm, nk),
        in_specs=[pl.BlockSpec((tm, tk), lambda i, k: (i, k)),
                  pl.BlockSpec((None, d, tk), lambda i, k: (k // per, 0, k % per)),
                  pl.BlockSpec((tm, d), row), pl.BlockSpec((tm, d), row), pl.BlockSpec((tm, 1), row),
                  pl.BlockSpec((1, d), vec)],
        out_specs=[pl.BlockSpec((tm, d), row), pl.BlockSpec((tm, d), row),
                   pl.BlockSpec((SUBLANES, d), vec), pl.BlockSpec((SUBLANES, d), vec)],
        out_shape=[jax.ShapeDtypeStruct((s, d), F32), jax.ShapeDtypeStruct((s, d), BF16),
                   jax.ShapeDtypeStruct((SUBLANES, d), F32), jax.ShapeDtypeStruct((SUBLANES, d), F32)],
        scratch_shapes=[pltpu.VMEM((tm, d), F32)],
        compiler_params=_params("arbitrary", "arbitrary"),
    )(dpre, w1, dz2, xh1, rstd1, g1)


def _dy_call(dzb, w_out):
    s, d = dzb.shape
    e = w_out.shape[0]
    tm = min(MM_TILE, s)

    def body(dz_ref, w_ref, o_ref):
        o_ref[...] = lax.dot_general(dz_ref[...], w_ref[...], (((1,), (1,)), ((), ())), preferred_element_type=F32)

    return pl.pallas_call(
        body, name="dy", grid=(s // tm,),
        in_specs=[pl.BlockSpec((tm, d), lambda i: (i, 0)), pl.BlockSpec((e, d), lambda i: (0, 0))],
        out_specs=pl.BlockSpec((tm, e), lambda i: (i, 0)),
        out_shape=jax.ShapeDtypeStruct((s, e), F32),
        compiler_params=_params("arbitrary"),
    )(dzb, w_out)


def _mixer_bwd_call(dy, d_pool, proj, h_f, h_b, w_pool, pool_scale, p):
    s = dy.shape[0]
    c = h_f.shape[1]
    pg = p // N_POOL_GROUPS
    t = min(SEQ_TILE, s)
    n_tiles = s // t

    def body(dyp_ref, dyr_ref, d_ref, gate_ref, hf_ref, hb_ref, wp_ref, sc_ref,
             e_ref, dh_ref, dgate_ref, dwp_ref, dsc_ref, dd_s):
        i = pl.program_id(0)

        @pl.when(i == 0)
        def _():
            dwp_ref[...] = jnp.zeros_like(dwp_ref)
            dsc_ref[...] = jnp.zeros_like(dsc_ref)

        for g in range(N_POOL_GROUPS):
            cols = pl.ds(g * pg, pg)
            dg = d_ref[:, cols]
            out = jnp.dot(dg, wp_ref[g], preferred_element_type=F32)
            dyp = dyp_ref[:, cols]
            prod = dyp * out
            dsc_ref[:, cols] += jnp.sum(prod.reshape(t // SUBLANES, SUBLANES, pg), axis=0)
            dout = (dyp * sc_ref[:, cols]).astype(BF16)
            dwp_ref[g] += lax.dot_general(dg, dout, (((0,), (0,)), ((), ())), preferred_element_type=F32)
            dd_s[:, cols] = lax.dot_general(dout, wp_ref[g], (((1,), (1,)), ((), ())), preferred_element_type=F32)

        def chunk(r0, _):
            rs = pl.ds(r0, SUBLANES)
            for g, w in enumerate(POOL_WINDOWS):
                cols = pl.ds(g * pg, pg)
                cnt = _window_counts(r0, i, t, s, w // 2, (SUBLANES, pg))
                e_ref[rs, cols] = dd_s[rs, cols] / cnt
            gate, dgate = _gelu_and_grad(gate_ref[rs, :])
            dyr = dyr_ref[rs, :]
            dh_ref[rs, :] = dyr * gate
            dd_s[rs, :] = dyr * (hf_ref[rs, :] + hb_ref[rs, :]) * dgate

        _chunk_loop(t, chunk)
        dgate_ref[...] = dd_s[...].astype(BF16)

    tile = lambda i: (i, 0)
    return pl.pallas_call(
        body, name="mixer_bwd", grid=(n_tiles,),
        in_specs=[pl.BlockSpec((t, p), tile), pl.BlockSpec((t, c), lambda i: (i, 1)), pl.BlockSpec((t, p), tile),
                  pl.BlockSpec((t, c), lambda i: (i, 2)), pl.BlockSpec((t, c), tile), pl.BlockSpec((t, c), tile),
                  pl.BlockSpec((N_POOL_GROUPS, pg, pg), lambda i: (0, 0, 0)), pl.BlockSpec((1, p), lambda i: (0, 0))],
        out_specs=[pl.BlockSpec((t, p), tile), pl.BlockSpec((t, c), tile), pl.BlockSpec((t, c), tile),
                   pl.BlockSpec((N_POOL_GROUPS, pg, pg), lambda i: (0, 0, 0)), pl.BlockSpec((SUBLANES, p), lambda i: (0, 0))],
        out_shape=[jax.ShapeDtypeStruct((s, p), F32), jax.ShapeDtypeStruct((s, c), F32), jax.ShapeDtypeStruct((s, c), BF16),
                   jax.ShapeDtypeStruct((N_POOL_GROUPS, pg, pg), F32), jax.ShapeDtypeStruct((SUBLANES, p), F32)],
        scratch_shapes=[pltpu.VMEM((t, p), F32)],
        compiler_params=_params("arbitrary"),
    )(dy, dy, d_pool, proj, h_f, h_b, w_pool, pool_scale)


def _scan_bwd_call(xc, dh, h_dir, dxc_prev, wa, wi, ba, bi, lam, reverse):
    s, c = xc.shape
    heads = c // LRU_HEAD
    t = min(SEQ_TILE, s)
    n_tiles = s // t
    per = t // SUBLANES
    last_blk = s // SUBLANES - 1
    tile = (lambda i: (i, 0)) if reverse else (lambda i: (n_tiles - 1 - i, 0))
    if reverse:
        halo = lambda i: (jnp.minimum((i + 1) * per, last_blk), 0)
    else:
        halo = lambda i: (jnp.maximum((n_tiles - 1 - i) * per - 1, 0), 0)
    whole2 = lambda i: (0, 0)
    whole3 = lambda i: (0, 0, 0)
    has_prev = dxc_prev is not None

    def body(*refs):
        (xc_ref, dh_ref, h_ref, hh_ref) = refs[:4]
        k = 4
        prev_ref = None
        if has_prev:
            prev_ref = refs[k]
            k += 1
        wa_ref, wi_ref, ba_ref, bi_ref, lam_ref = refs[k:k + 5]
        dxc_ref, dwa_ref, dwi_ref, dba_ref, dbi_ref, dsp_ref = refs[k + 5:k + 11]
        pr_s, pi_s, g_s, carry_s = refs[k + 11:]
        step = pl.program_id(0)
        tile_idx = step if reverse else n_tiles - 1 - step

        @pl.when(step == 0)
        def _():
            carry_s[...] = jnp.zeros_like(carry_s)
            dwa_ref[...] = jnp.zeros_like(dwa_ref)
            dwi_ref[...] = jnp.zeros_like(dwi_ref)
            dba_ref[...] = jnp.zeros_like(dba_ref)
            dbi_ref[...] = jnp.zeros_like(dbi_ref)
            dsp_ref[...] = jnp.zeros_like(dsp_ref)

        _gate_matmuls(xc_ref, wa_ref, wi_ref, pr_s, pi_s, heads)
        ba_v, bi_v = ba_ref[...], bi_ref[...]
        sp = _softplus(-lam_ref[...])
        rows = lax.broadcasted_iota(jnp.int32, (SUBLANES, c), 0)

        def gates(r0, _):
            rs = pl.ds(r0, SUBLANES)
            _, _, a, _ = _rg_gates(pr_s[rs, :], pi_s[rs, :], ba_v, bi_v, sp)
            g_s[rs, :] = a

        _chunk_loop(t, gates)

        def row(j, carry):
            r = j if reverse else (t - 1 - j)
            gt = dh_ref[pl.ds(r, 1), :] + carry
            carry = g_s[pl.ds(r, 1), :] * gt
            g_s[pl.ds(r, 1), :] = gt
            return carry

        carry_s[...] = lax.fori_loop(0, t, row, carry_s[...], unroll=8)

        def chunk(r0, _):
            rs = pl.ds(r0, SUBLANES)
            xcv = xc_ref[rs, :]
            r, ig, a, mult = _rg_gates(pr_s[rs, :], pi_s[rs, :], ba_v, bi_v, sp)
            gt = g_s[rs, :]
            cur = h_ref[rs, :]
            if reverse:
                after = h_ref[pl.ds(pl.multiple_of(jnp.minimum(r0 + SUBLANES, t - SUBLANES), SUBLANES), SUBLANES), :]
                edge = jnp.where(tile_idx == n_tiles - 1, 0.0, hh_ref[...])
                nxt = jnp.where(r0 == t - SUBLANES, edge, after)
                hs = _shift_rows(cur, cur, nxt, 1, rows)
            else:
                before = h_ref[pl.ds(pl.multiple_of(jnp.maximum(r0 - SUBLANES, 0), SUBLANES), SUBLANES), :]
                edge = jnp.where(tile_idx == 0, 0.0, hh_ref[...])
                prv = jnp.where(r0 == 0, edge, before)
                hs = _shift_rows(prv, cur, cur, -1, rows)
            gx = gt * xcv
            dmult = gx * ig
            di = gx * mult
            dlog_a = (gt * hs) * a - dmult * (a * a) / mult
            dr = dlog_a * (-RG_C * sp)
            dsp_ref[...] += dlog_a * (-RG_C * r)
            dpr = dr * r * (1.0 - r)
            dpi = di * ig * (1.0 - ig)
            dba_ref[...] += dpr
            dbi_ref[...] += dpi
            direct = gt * mult * ig
            if has_prev:
                direct = direct + prev_ref[rs, :]
            dxc_ref[rs, :] = direct
            pr_s[rs, :] = dpr
            pi_s[rs, :] = dpi

        _chunk_loop(t, chunk)

        for h in range(heads):
            cs = pl.ds(h * LRU_HEAD, LRU_HEAD)
            xb = xc_ref[:, cs].astype(BF16)
            dprb = pr_s[:, cs].astype(BF16)
            dpib = pi_s[:, cs].astype(BF16)
            dwa_ref[h] += lax.dot_general(xb, dprb, (((0,), (0,)), ((), ())), preferred_element_type=F32)
            dwi_ref[h] += lax.dot_general(xb, dpib, (((0,), (0,)), ((), ())), preferred_element_type=F32)
            dxc_ref[:, cs] += (
                lax.dot_general(dprb, wa_ref[h], (((1,), (1,)), ((), ())), preferred_element_type=F32)
                + lax.dot_general(dpib, wi_ref[h], (((1,), (1,)), ((), ())), preferred_element_type=F32))

    in_specs = [pl.BlockSpec((t, c), tile), pl.BlockSpec((t, c), tile), pl.BlockSpec((t, c), tile),
                pl.BlockSpec((SUBLANES, c), halo)]
    args = [xc, dh, h_dir, h_dir]
    if has_prev:
        in_specs.append(pl.BlockSpec((t, c), tile))
        args.append(dxc_prev)
    in_specs += [pl.BlockSpec((heads, LRU_HEAD, LRU_HEAD), whole3), pl.BlockSpec((heads, LRU_HEAD, LRU_HEAD), whole3),
                 pl.BlockSpec((1, c), whole2), pl.BlockSpec((1, c), whole2), pl.BlockSpec((1, c), whole2)]
    args += [wa, wi, ba, bi, lam]
    return pl.pallas_call(
        body, name="scan_bwd_rev" if reverse else "scan_bwd", grid=(n_tiles,),
        in_specs=in_specs,
        out_specs=[pl.BlockSpec((t, c), tile),
                   pl.BlockSpec((heads, LRU_HEAD, LRU_HEAD), whole3), pl.BlockSpec((heads, LRU_HEAD, LRU_HEAD), whole3),
                   pl.BlockSpec((SUBLANES, c), whole2), pl.BlockSpec((SUBLANES, c), whole2), pl.BlockSpec((SUBLANES, c), whole2)],
        out_shape=[jax.ShapeDtypeStruct((s, c), F32),
                   jax.ShapeDtypeStruct((heads, LRU_HEAD, LRU_HEAD), F32), jax.ShapeDtypeStruct((heads, LRU_HEAD, LRU_HEAD), F32),
                   jax.ShapeDtypeStruct((SUBLANES, c), F32), jax.ShapeDtypeStruct((SUBLANES, c), F32),
                   jax.ShapeDtypeStruct((SUBLANES, c), F32)],
        scratch_shapes=[pltpu.VMEM((t, c), F32), pltpu.VMEM((t, c), F32), pltpu.VMEM((t, c), F32), pltpu.VMEM((1, c), F32)],
        compiler_params=_params("arbitrary"),
    )(*args)


def _dproj_call(e_pool, dxc, proj, dgate, conv_w, p):
    s, c = dxc.shape
    pg = p // N_POOL_GROUPS
    t = min(SEQ_TILE, s)
    n_tiles = s // t

    def body(e_ref, ep_ref, en_ref, dx_ref, dxp_ref, dxn_ref, u_ref, up_ref, un_ref, dgate_ref, w_ref,
             dproj_ref, dcw_ref, dcb_ref, st_s):
        i = pl.program_id(0)
        first, last = i == 0, i == n_tiles - 1

        @pl.when(first)
        def _():
            dcw_ref[...] = jnp.zeros_like(dcw_ref)
            dcb_ref[...] = jnp.zeros_like(dcb_ref)

        rows_p = lax.broadcasted_iota(jnp.int32, (SUBLANES, pg), 0)
        rows_c = lax.broadcasted_iota(jnp.int32, (SUBLANES, c), 0)
        w = w_ref[...]

        def chunk(r0, _):
            rs = pl.ds(r0, SUBLANES)
            for g, win in enumerate(POOL_WINDOWS):
                cols = pl.ds(g * pg, pg)
                prv, cur, nxt = _neighbour_chunks(e_ref, ep_ref, en_ref, r0, t, cols, first, last)
                tot = cur
                for o in range(-(win // 2) + 1, win // 2 + 1):
                    if o != 0:
                        tot = tot + _shift_rows(prv, cur, nxt, o, rows_p)
                cnt = _window_counts(r0, i, t, s, win // 2, (SUBLANES, pg))
                st_s[rs, cols] = tot - cur * cnt
            prv, cur, nxt = _neighbour_chunks(dx_ref, dxp_ref, dxn_ref, r0, t, slice(None), first, last)
            du = w[1:2] * cur
            du += w[0:1] * _shift_rows(prv, cur, nxt, 1, rows_c)
            du += w[2:3] * _shift_rows(prv, cur, nxt, -1, rows_c)
            du += w[3:4] * _shift_rows(prv, cur, nxt, -2, rows_c)
            st_s[rs, pl.ds(p, c)] = du
            uprv, ucur, unxt = _neighbour_chunks(u_ref, up_ref, un_ref, r0, t, slice(None), first, last)
            dcb_ref[...] += cur
            for j, o in enumerate((-1, 0, 1, 2)):
                dcw_ref[j] += cur * _shift_rows(uprv, ucur, unxt, o, rows_c)

        _chunk_loop(t, chunk)
        dproj_ref[:, pl.ds(0, p + c)] = st_s[...].astype(BF16)
        dproj_ref[:, pl.ds(p + c, c)] = dgate_ref[...]

    return pl.pallas_call(
        body, name="dproj", grid=(n_tiles,),
        in_specs=_halo_specs(t, s, p, 0) + _halo_specs(t, s, c, 0) + _halo_specs(t, s, c, 1) + [
            pl.BlockSpec((t, c), lambda i: (i, 0)), pl.BlockSpec((4, c), lambda i: (0, 0))],
        out_specs=[pl.BlockSpec((t, p + 2 * c), lambda i: (i, 0)),
                   pl.BlockSpec((4, SUBLANES, c), lambda i: (0, 0, 0)), pl.BlockSpec((SUBLANES, c), lambda i: (0, 0))],
        out_shape=[jax.ShapeDtypeStruct((s, p + 2 * c), BF16), jax.ShapeDtypeStruct((4, SUBLANES, c), F32),
                   jax.ShapeDtypeStruct((SUBLANES, c), F32)],
        scratch_shapes=[pltpu.VMEM((t, p + c), F32)],
        compiler_params=_params("arbitrary"),
    )(e_pool, e_pool, e_pool, dxc, dxc, dxc, proj, proj, proj, dgate, conv_w)


def _dx_call(dproj, w_in, dz1):
    s, e = dproj.shape
    n, d, e4 = w_in.shape
    tm = min(MM_TILE, s)

    def body(dp_ref, w_ref, dz_ref, o_ref):
        k = pl.program_id(1)

        @pl.when(k == 0)
        def _():
            o_ref[...] = ALPHA * dz_ref[...]

        o_ref[...] += lax.dot_general(dp_ref[...], w_ref[...], (((1,), (1,)), ((), ())), preferred_element_type=F32)

    return pl.pallas_call(
        body, name="grad_x", grid=(s // tm, n),
        in_specs=[pl.BlockSpec((tm, e4), lambda i, k: (i, k)), pl.BlockSpec((None, d, e4), lambda i, k: (k, 0, 0)),
                  pl.BlockSpec((tm, d), lambda i, k: (i, 0))],
        out_specs=pl.BlockSpec((tm, d), lambda i, k: (i, 0)),
        out_shape=jax.ShapeDtypeStruct((s, d), F32),
        compiler_params=_params("arbitrary", "arbitrary"),
    )(dproj, w_in, dz1)


def _row_tile(rows, cols, n_arrays):
    limit = max(SUBLANES, ELT_BLOCK_BYTES // (4 * cols * max(1, n_arrays // 4)))
    best = SUBLANES
    for cand in range(SUBLANES, min(rows, limit) + 1, SUBLANES):
        if rows % cand == 0:
            best = cand
    return best if rows % SUBLANES == 0 else rows


def _add_half_call(g, recv, core, name):
    _, rows, cols = g.shape
    tr = _row_tile(rows, cols, 3)

    def body(core_ref, g_ref, r_ref, o_ref):
        o_ref[...] = g_ref[...] + r_ref[...]

    return pl.pallas_call(
        body, name=name,
        grid_spec=pltpu.PrefetchScalarGridSpec(
            num_scalar_prefetch=1, grid=(rows // tr,),
            in_specs=[pl.BlockSpec((None, tr, cols), lambda i, core_ref: (core_ref[0], i, 0)),
                      pl.BlockSpec((tr, cols), lambda i, core_ref: (i, 0))],
            out_specs=pl.BlockSpec((tr, cols), lambda i, core_ref: (i, 0))),
        out_shape=jax.ShapeDtypeStruct((rows, cols), F32),
        compiler_params=_params("arbitrary"),
    )(core, g, recv)


def _sum_chips_call(r, name):
    _, rows, cols = r.shape
    tr = _row_tile(rows, cols, 5)

    def body(r_ref, o_ref):
        o_ref[...] = ((r_ref[0] + r_ref[1]) + r_ref[2]) + r_ref[3]

    return pl.pallas_call(
        body, name=name, grid=(rows // tr,),
        in_specs=[pl.BlockSpec((N_CHIPS, tr, cols), lambda i: (0, i, 0))],
        out_specs=pl.BlockSpec((tr, cols), lambda i: (i, 0)),
        out_shape=jax.ShapeDtypeStruct((rows, cols), F32),
        compiler_params=_params("arbitrary"),
    )(r)


def _adamw_call(g, w, m, v, name):
    rows, cols = w.shape
    tr = _row_tile(rows, cols, 7)

    def body(g_ref, w_ref, m_ref, v_ref, d_ref, mo_ref, vo_ref):
        gv = g_ref[...]
        mn = ADAM_B1 * m_ref[...] + (1.0 - ADAM_B1) * gv
        vn = ADAM_B2 * v_ref[...] + (1.0 - ADAM_B2) * (gv * gv)
        m_hat = mn / (1.0 - ADAM_B1 ** ADAM_STEP)
        v_hat = vn / (1.0 - ADAM_B2 ** ADAM_STEP)
        d_ref[...] = -ADAM_LR * (m_hat / (jnp.sqrt(v_hat) + ADAM_EPS) + ADAM_WD * w_ref[...])
        mo_ref[...] = mn
        vo_ref[...] = vn

    spec = pl.BlockSpec((tr, cols), lambda i: (i, 0))
    shape = jax.ShapeDtypeStruct((rows, cols), F32)
    return pl.pallas_call(
        body, name=name, grid=(rows // tr,),
        in_specs=[spec] * 4, out_specs=[spec] * 3, out_shape=[shape] * 3,
        compiler_params=_params("arbitrary"),
    )(g, w, m, v)


def _mesh_place():
    x, y, c = lax.axis_index("x"), lax.axis_index("y"), lax.axis_index("c")
    chips = [(1 - x, y), (x, 1 - y), (1 - x, 1 - y)]
    return x, y, c, chips


def _remote(src, dst, send_sems, recv_sems, idx, device):
    return pltpu.make_async_remote_copy(src_ref=src, dst_ref=dst, send_sem=send_sems.at[idx], recv_sem=recv_sems.at[idx],
                                        device_id=device, device_id_type=MESH)


def _gather_shards_call(shards):
    n = len(shards)

    def body(*refs):
        ins, outs = refs[:n], refs[n:2 * n]
        send_sems, recv_sems, local_sems = refs[2 * n:]
        x, y, c, chips = _mesh_place()
        k = 2 * x + y
        sibling = (x, y, 1 - c)
        started = []
        for a in range(n):
            mine = pltpu.make_async_copy(ins[a], outs[a].at[k], local_sems.at[a])
            mine.start()
            started.append(mine)
        sends = []
        for a in range(n):
            for j, (px, py) in enumerate(chips):
                cp = _remote(ins[a].at[c], outs[a].at[k, c], send_sems, recv_sems, 6 * a + j, (px, py, c))
                cp.start()
                sends.append(cp)
        for a in range(n):
            for j, (px, py) in enumerate(chips):
                kj = 2 * px + py
                _remote(ins[a].at[c], outs[a].at[kj, c], send_sems, recv_sems, 6 * a + j, (px, py, c)).wait_recv()
                fwd = _remote(outs[a].at[kj, c], outs[a].at[kj, c], send_sems, recv_sems, 6 * a + 3 + j, sibling)
                fwd.start()
                sends.append(fwd)
        for a in range(n):
            for j, (px, py) in enumerate(chips):
                kj = 2 * px + py
                _remote(ins[a].at[c], outs[a].at[kj, 1 - c], send_sems, recv_sems, 6 * a + 3 + j, sibling).wait_recv()
        for cp in sends:
            cp.wait_send()
        for cp in started:
            cp.wait()

    return pl.pallas_call(
        body, name="gather_weights",
        in_specs=[ANY] * n, out_specs=[ANY] * n,
        out_shape=[jax.ShapeDtypeStruct((N_CHIPS,) + a.shape, a.dtype) for a in shards],
        scratch_shapes=[pltpu.SemaphoreType.DMA((6 * n,)), pltpu.SemaphoreType.DMA((6 * n,)), pltpu.SemaphoreType.DMA((n,))],
    )(*shards)


def _sibling_halves_call(grads):
    n = len(grads)

    def body(*refs):
        ins, outs = refs[:n], refs[n:2 * n]
        send_sems, recv_sems = refs[2 * n:]
        x, y, c, _ = _mesh_place()
        sibling = (x, y, 1 - c)
        copies = [_remote(ins[a].at[1 - c], outs[a], send_sems, recv_sems, a, sibling) for a in range(n)]
        for cp in copies:
            cp.start()
        for cp in copies:
            cp.wait_recv()
        for cp in copies:
            cp.wait_send()

    return pl.pallas_call(
        body, name="reduce_siblings",
        in_specs=[ANY] * n, out_specs=[ANY] * n,
        out_shape=[jax.ShapeDtypeStruct(a.shape[1:], a.dtype) for a in grads],
        scratch_shapes=[pltpu.SemaphoreType.DMA((n,)), pltpu.SemaphoreType.DMA((n,))],
    )(*grads)


def _chip_exchange_call(parts, sharded):
    n = len(parts)

    def body(*refs):
        ins, outs = refs[:n], refs[n:2 * n]
        send_sems, recv_sems, local_sems = refs[2 * n:]
        x, y, c, chips = _mesh_place()
        k = 2 * x + y
        copies = []
        for a in range(n):
            own = pltpu.make_async_copy(ins[a].at[k] if sharded[a] else ins[a], outs[a].at[k], local_sems.at[a])
            own.start()
            copies.append(own)
        sends = []
        for a in range(n):
            for j, (px, py) in enumerate(chips):
                kj = 2 * px + py
                src = ins[a].at[kj] if sharded[a] else ins[a]
                cp = _remote(src, outs[a].at[k], send_sems, recv_sems, 3 * a + j, (px, py, c))
                cp.start()
                sends.append(cp)
        for a in range(n):
            for j, (px, py) in enumerate(chips):
                kj = 2 * px + py
                src = ins[a].at[kj] if sharded[a] else ins[a]
                _remote(src, outs[a].at[kj], send_sems, recv_sems, 3 * a + j, (px, py, c)).wait_recv()
        for cp in sends:
            cp.wait_send()
        for cp in copies:
            cp.wait()

    def slot_shape(a, is_sharded):
        return (N_CHIPS,) + (a.shape[1:] if is_sharded else a.shape)

    return pl.pallas_call(
        body, name="reduce_chips",
        in_specs=[ANY] * n, out_specs=[ANY] * n,
        out_shape=[jax.ShapeDtypeStruct(slot_shape(a, sh), a.dtype) for a, sh in zip(parts, sharded)],
        scratch_shapes=[pltpu.SemaphoreType.DMA((3 * n,)), pltpu.SemaphoreType.DMA((3 * n,)), pltpu.SemaphoreType.DMA((n,))],
    )(*parts)


def _join_halves_call(halves):
    n = len(halves)

    def body(*refs):
        ins, outs = refs[:n], refs[n:2 * n]
        send_sems, recv_sems, local_sems = refs[2 * n:]
        x, y, c, _ = _mesh_place()
        sibling = (x, y, 1 - c)
        own = [pltpu.make_async_copy(ins[a], outs[a].at[c], local_sems.at[a]) for a in range(n)]
        copies = [_remote(ins[a], outs[a].at[c], send_sems, recv_sems, a, sibling) for a in range(n)]
        for cp in own + copies:
            cp.start()
        for a in range(n):
            _remote(ins[a], outs[a].at[1 - c], send_sems, recv_sems, a, sibling).wait_recv()
        for cp in copies:
            cp.wait_send()
        for cp in own:
            cp.wait()

    return pl.pallas_call(
        body, name="reduce_join",
        in_specs=[ANY] * n, out_specs=[ANY] * n,
        out_shape=[jax.ShapeDtypeStruct((2,) + a.shape, a.dtype) for a in halves],
        scratch_shapes=[pltpu.SemaphoreType.DMA((n,)), pltpu.SemaphoreType.DMA((n,)), pltpu.SemaphoreType.DMA((n,))],
    )(*halves)


def _pack(arrays, rows_multiple):
    flat = jnp.concatenate([a.reshape(-1) for a in arrays])
    per = LANES * rows_multiple
    padded = -(-flat.shape[0] // per) * per
    flat = jnp.pad(flat, (0, padded - flat.shape[0]))
    return flat.reshape(-1, LANES)


def _unpack(packed, shapes):
    flat = packed.reshape(-1)
    out, at = [], 0
    for shp in shapes:
        size = 1
        for dim in shp:
            size *= dim
        out.append(flat[at:at + size].reshape(shp))
        at += size
    return out


def _halves(a):
    return a.reshape((2, a.shape[0] // 2) + a.shape[1:])


def kernel(x, ln_mix_g, ln_mix_b, w_in, w_pool, pool_scale, conv_w, conv_b, w_rg_a, b_rg_a, w_rg_i, b_rg_i, rg_lambda, w_out, ln_ffn_g, ln_ffn_b, w_mlp_in, w_mlp_out, loss_target, m_ln_mix_g, m_ln_mix_b, m_w_in, m_w_pool, m_pool_scale, m_conv_w, m_conv_b, m_w_rg_a, m_b_rg_a, m_w_rg_i, m_b_rg_i, m_rg_lambda, m_w_out, m_ln_ffn_g, m_ln_ffn_b, m_w_mlp_in, m_w_mlp_out, v_ln_mix_g, v_ln_mix_b, v_w_in, v_w_pool, v_pool_scale, v_conv_w, v_conv_b, v_w_rg_a, v_b_rg_a, v_w_rg_i, v_b_rg_i, v_rg_lambda, v_w_out, v_ln_ffn_g, v_ln_ffn_b, v_w_mlp_in, v_w_mlp_out):
    weights = dict(ln_mix_g=ln_mix_g, ln_mix_b=ln_mix_b, w_in=w_in, w_pool=w_pool, pool_scale=pool_scale, conv_w=conv_w,
                   conv_b=conv_b, w_rg_a=w_rg_a, b_rg_a=b_rg_a, w_rg_i=w_rg_i, b_rg_i=b_rg_i, rg_lambda=rg_lambda,
                   w_out=w_out, ln_ffn_g=ln_ffn_g, ln_ffn_b=ln_ffn_b, w_mlp_in=w_mlp_in, w_mlp_out=w_mlp_out)
    m_in = dict(ln_mix_g=m_ln_mix_g, ln_mix_b=m_ln_mix_b, w_in=m_w_in, w_pool=m_w_pool, pool_scale=m_pool_scale,
                conv_w=m_conv_w, conv_b=m_conv_b, w_rg_a=m_w_rg_a, b_rg_a=m_b_rg_a, w_rg_i=m_w_rg_i, b_rg_i=m_b_rg_i,
                rg_lambda=m_rg_lambda, w_out=m_w_out, ln_ffn_g=m_ln_ffn_g, ln_ffn_b=m_ln_ffn_b, w_mlp_in=m_w_mlp_in,
                w_mlp_out=m_w_mlp_out)
    v_in = dict(ln_mix_g=v_ln_mix_g, ln_mix_b=v_ln_mix_b, w_in=v_w_in, w_pool=v_w_pool, pool_scale=v_pool_scale,
                conv_w=v_conv_w, conv_b=v_conv_b, w_rg_a=v_w_rg_a, b_rg_a=v_b_rg_a, w_rg_i=v_w_rg_i, b_rg_i=v_b_rg_i,
                rg_lambda=v_rg_lambda, w_out=v_w_out, ln_ffn_g=v_ln_ffn_g, ln_ffn_b=v_ln_ffn_b, w_mlp_in=v_w_mlp_in,
                w_mlp_out=v_w_mlp_out)
    names = list(weights)
    big = ("w_in", "w_out", "w_mlp_in", "w_mlp_out")

    xs = x[0]
    tgt = loss_target[0]
    s, d = xs.shape
    p = c = d // 2
    pg = p // N_POOL_GROUPS
    heads = c // LRU_HEAD
    core = lax.axis_index("c")
    shard = 2 * lax.axis_index("x") + lax.axis_index("y")

    small_shard = _pack([conv_w[0], b_rg_a[0], b_rg_i[0], rg_lambda[0]], 2 * SUBLANES)
    gathered = _gather_shards_call([
        _halves(w_in[0].astype(BF16)), _halves(w_out[0].astype(BF16)),
        _halves(w_mlp_in[0].astype(BF16)), _halves(w_mlp_out[0].astype(BF16)),
        _halves(w_pool[0].astype(BF16)), _halves(small_shard)])
    w_in_f = gathered[0].reshape((N_CHIPS,) + w_in.shape[1:])
    w_out_f = gathered[1].reshape(d, d)
    w1_f = gathered[2].reshape((N_CHIPS,) + w_mlp_in.shape[1:])
    w2_f = gathered[3].reshape(N_CHIPS * w_mlp_out.shape[1], d)
    w_pool_f = gathered[4].reshape(N_CHIPS, N_POOL_GROUPS, pg // N_CHIPS, pg).transpose(1, 0, 2, 3).reshape(N_POOL_GROUPS, pg, pg)
    c4 = c // N_CHIPS
    small_parts = [_unpack(gathered[5][k].reshape(-1, LANES), [(4, c4), (2, c4), (2, c4), (2, c4)]) for k in range(N_CHIPS)]
    conv_w_f = jnp.concatenate([sp_[0] for sp_ in small_parts], axis=1)
    b_a_f = jnp.concatenate([sp_[1] for sp_ in small_parts], axis=1)
    b_i_f = jnp.concatenate([sp_[2] for sp_ in small_parts], axis=1)
    lam_f = jnp.concatenate([sp_[3] for sp_ in small_parts], axis=1)
    wa_b = w_rg_a[0].astype(BF16)
    wi_b = w_rg_i[0].astype(BF16)

    proj, xb = _proj_call(xs, w_in_f)
    xc = _conv_call(proj, conv_w_f, conv_b, c)
    h_b = _scan_fwd_call(xc, wa_b[1], wi_b[1], b_a_f[1:2], b_i_f[1:2], lam_f[1:2], True)
    h_f = _scan_fwd_call(xc, wa_b[0], wi_b[0], b_a_f[0:1], b_i_f[0:1], lam_f[0:1], False)
    y, d_pool = _pool_combine_call(proj, h_f, h_b, w_pool_f, pool_scale, p)
    xh1, x1b, rstd1 = _out_ln1_call(y, w_out_f, xs, ln_mix_g, ln_mix_b)
    r_act, hsq = _mlp_in_call(x1b, w1_f)
    dz2, dz2b, loss8, dg2, db2 = _mlp_out_ln2_call(hsq, w2_f, xh1, ln_mix_g, ln_mix_b, ln_ffn_g, ln_ffn_b, tgt)

    g_w2 = _row_sharded_grad(hsq, dz2b, "grad_w_mlp_out")
    dpre = _dhsq_call(dz2b, w2_f, r_act)
    g_w1 = _col_sharded_grad(x1b, dpre, "grad_w_mlp_in")
    dz1, dz1b, dg1, db1 = _dx1_ln1_bwd_call(dpre, w1_f, dz2, xh1, rstd1, ln_mix_g)
    g_wout = _row_sharded_grad(y, dz1b, "grad_w_out")
    dy = _dy_call(dz1b, w_out_f)
    e_pool, dh, dgate, g_wpool, g_pscale8 = _mixer_bwd_call(dy, d_pool, proj, h_f, h_b, w_pool_f, pool_scale, p)
    dxc0, g_wa0, g_wi0, g_ba0, g_bi0, g_sp0 = _scan_bwd_call(
        xc, dh, h_f, None, wa_b[0], wi_b[0], b_a_f[0:1], b_i_f[0:1], lam_f[0:1], False)
    dxc, g_wa1, g_wi1, g_ba1, g_bi1, g_sp1 = _scan_bwd_call(
        xc, dh, h_b, dxc0, wa_b[1], wi_b[1], b_a_f[1:2], b_i_f[1:2], lam_f[1:2], True)
    dproj, g_cw8, g_cb8 = _dproj_call(e_pool, dxc, proj, dgate, conv_w_f, p)
    g_win = _col_sharded_grad(xb, dproj, "grad_w_in")
    grad_x = _dx_call(dproj, w_in_f, dz1)

    rowsum = lambda a8: jnp.sum(a8, axis=-2)
    g_lam = jnp.stack([rowsum(g_sp0), rowsum(g_sp1)]) * (-_sigmoid(-lam_f))
    small_grads = {
        "ln_mix_g": rowsum(dg1), "ln_mix_b": rowsum(db1), "ln_ffn_g": rowsum(dg2), "ln_ffn_b": rowsum(db2),
        "pool_scale": rowsum(g_pscale8), "conv_b": rowsum(g_cb8),
        "w_rg_a": jnp.stack([g_wa0, g_wa1]), "w_rg_i": jnp.stack([g_wi0, g_wi1]),
        "w_pool": g_wpool, "conv_w": rowsum(g_cw8),
        "b_rg_a": jnp.stack([rowsum(g_ba0), rowsum(g_ba1)]), "b_rg_i": jnp.stack([rowsum(g_bi0), rowsum(g_bi1)]),
        "rg_lambda": g_lam,
    }
    small_names = list(small_grads)
    small_shapes = [small_grads[nm].shape for nm in small_names]
    g_small = _halves(_pack([small_grads[nm] for nm in small_names], 2 * SUBLANES))

    big_grads = [g_win, g_wout, g_w1, g_w2]
    grads = big_grads + [g_small]
    from_sibling = _sibling_halves_call(grads)
    core_arr = jnp.reshape(core, (1,)).astype(jnp.int32)
    chip_sums = []
    for a, (g, rv) in enumerate(zip(grads, from_sibling)):
        cols = g.shape[-1]
        part = _add_half_call(g.reshape(2, -1, cols), rv.reshape(-1, cols), core_arr, f"reduce_add_{a}")
        chip_sums.append(part.reshape(rv.shape))
    slots = _chip_exchange_call(chip_sums, [True] * len(big_grads) + [False])
    reduced_halves = []
    for a, sl in enumerate(slots):
        cols = sl.shape[-1]
        total = _sum_chips_call(sl.reshape(N_CHIPS, -1, cols), f"reduce_sum_{a}")
        reduced_halves.append(total.reshape(sl.shape[1:]))
    joined = _join_halves_call(reduced_halves)

    grad_w, delta_w, new_m, new_v = {}, {}, {}, {}
    for nm, full in zip(big, joined[:len(big)]):
        w2d = weights[nm][0]
        g2d = full.reshape(w2d.shape)
        dl, mn, vn = _adamw_call(g2d, w2d, m_in[nm][0], v_in[nm][0], f"adamw_{nm}")
        grad_w[nm], delta_w[nm], new_m[nm], new_v[nm] = g2d[None], dl[None], mn[None], vn[None]

    small_full = dict(zip(small_names, _unpack(joined[len(big)].reshape(-1, LANES), small_shapes)))
    local = dict(small_full)
    local["w_pool"] = lax.dynamic_slice_in_dim(small_full["w_pool"], shard * (pg // N_CHIPS), pg // N_CHIPS, axis=1)
    for nm in ("conv_w", "b_rg_a", "b_rg_i", "rg_lambda"):
        local[nm] = lax.dynamic_slice_in_dim(small_full[nm], shard * c4, c4, axis=1)
    small_w_shapes = [weights[nm].shape for nm in small_names]
    g_pack = _pack([local[nm] for nm in small_names], SUBLANES)
    w_pack = _pack([weights[nm] for nm in small_names], SUBLANES)
    m_pack = _pack([m_in[nm] for nm in small_names], SUBLANES)
    v_pack = _pack([v_in[nm] for nm in small_names], SUBLANES)
    dl_p, mn_p, vn_p = _adamw_call(g_pack, w_pack, m_pack, v_pack, "adamw_small")
    for nm, gl, dl, mn, vn in zip(small_names, _unpack(g_pack, small_w_shapes), _unpack(dl_p, small_w_shapes),
                                  _unpack(mn_p, small_w_shapes), _unpack(vn_p, small_w_shapes)):
        grad_w[nm], delta_w[nm], new_m[nm], new_v[nm] = gl, dl, mn, vn

    loss = lax.psum(jnp.sum(loss8), ("x", "y", "c"))
    return (loss, grad_x[None], *[grad_w[nm] for nm in names], *[delta_w[nm] for nm in names],
            *[new_m[nm] for nm in names], *[new_v[nm] for nm in names])
```

```python
import functools

import jax
import jax.numpy as jnp
from jax import lax
from jax.experimental import pallas as pl
from jax.experimental.pallas import tpu as pltpu

F32 = jnp.float32
BF16 = jnp.bfloat16

N_CHIPS = 4
LANES = 128
SUBLANES = 8
LRU_HEAD = 128
N_POOL_GROUPS = 4
POOL_WINDOWS = (2, 4, 8, 16)
RG_C = 8.0
LN_EPS = 1e-5
ALPHA = 2.0 ** 0.25
ADAM_LR, ADAM_B1, ADAM_B2, ADAM_EPS, ADAM_WD, ADAM_STEP = 0.001, 0.9, 0.999, 1e-08, 0.01, 10
VMEM_LIMIT = 56 * 1024 * 1024
SEQ_TILE = 256
MM_TILE = 512
LN_UNROLL = 4
ELT_BLOCK_BYTES = 2 * 1024 * 1024
MESH = pl.DeviceIdType.MESH
ANY = pl.BlockSpec(memory_space=pl.ANY)


def _params(*sem):
    return pltpu.CompilerParams(dimension_semantics=sem, vmem_limit_bytes=VMEM_LIMIT)


def _sigmoid(z):
    return 1.0 / (1.0 + jnp.exp(-z))


def _neg_expm1(z):
    series = -(z * (1.0 + z * (0.5 + z * (1.0 / 6.0 + z * (1.0 / 24.0)))))
    return jnp.where(z > -0.01, series, 1.0 - jnp.exp(z))


def _softplus(z):
    return jnp.maximum(z, 0.0) + jnp.log1p(jnp.exp(-jnp.abs(z)))


_GELU_K = 0.7978845608028654
_GELU_C = 0.044715


def _gelu_and_grad(u):
    t = jnp.tanh(_GELU_K * (u + _GELU_C * (u * u * u)))
    g = 0.5 * u * (1.0 + t)
    dg = 0.5 * (1.0 + t) + 0.5 * u * (1.0 - t * t) * (_GELU_K * (1.0 + 3.0 * _GELU_C * u * u))
    return g, dg


def _shift_rows(prv, cur, nxt, o, rows):
    if o == 0:
        return cur
    if o == SUBLANES:
        return nxt
    if o == -SUBLANES:
        return prv
    if o > 0:
        s = SUBLANES - o
        return jnp.where(rows < s, pltpu.roll(cur, s, 0), pltpu.roll(nxt, s, 0))
    p = -o
    return jnp.where(rows >= p, pltpu.roll(cur, p, 0), pltpu.roll(prv, p, 0))


def _neighbour_chunks(main_ref, prev_ref, next_ref, r0, t_rows, cols, first_tile, last_tile):
    cur = main_ref[pl.ds(r0, SUBLANES), cols]
    before = main_ref[pl.ds(pl.multiple_of(jnp.maximum(r0 - SUBLANES, 0), SUBLANES), SUBLANES), cols]
    after = main_ref[pl.ds(pl.multiple_of(jnp.minimum(r0 + SUBLANES, t_rows - SUBLANES), SUBLANES), SUBLANES), cols]
    halo_prev = jnp.where(first_tile, 0.0, prev_ref[:, cols])
    halo_next = jnp.where(last_tile, 0.0, next_ref[:, cols])
    prv = jnp.where(r0 == 0, halo_prev, before)
    nxt = jnp.where(r0 == t_rows - SUBLANES, halo_next, after)
    return prv, cur, nxt


def _halo_specs(t_rows, n_rows, width, col_block):
    per = t_rows // SUBLANES
    last = n_rows // SUBLANES - 1
    return [
        pl.BlockSpec((t_rows, width), lambda i: (i, col_block)),
        pl.BlockSpec((SUBLANES, width), lambda i: (jnp.maximum(i * per - 1, 0), col_block)),
        pl.BlockSpec((SUBLANES, width), lambda i: (jnp.minimum((i + 1) * per, last), col_block)),
    ]


def _chunk_loop(t_rows, fn, init=None, unroll=1):
    span = SUBLANES * unroll

    def step(ci, carry):
        base = pl.multiple_of(ci * span, span)
        for u in range(unroll):
            carry = fn(base + u * SUBLANES, carry)
        return carry
    return lax.fori_loop(0, t_rows // span, step, init)


def _proj_call(x, w_in):
    s, d = x.shape
    n, _, e4 = w_in.shape
    tm = min(MM_TILE, s)

    def body(x_ref, w_ref, proj_ref, xb_ref):
        @pl.when(pl.program_id(1) == 0)
        def _():
            xb_ref[...] = x_ref[...].astype(BF16)

        proj_ref[...] = jnp.dot(xb_ref[...], w_ref[...], preferred_element_type=F32)

    return pl.pallas_call(
        body, name="proj", grid=(s // tm, n),
        in_specs=[pl.BlockSpec((tm, d), lambda i, j: (i, 0)), pl.BlockSpec((None, d, e4), lambda i, j: (j, 0, 0))],
        out_specs=[pl.BlockSpec((tm, e4), lambda i, j: (i, j)), pl.BlockSpec((tm, d), lambda i, j: (i, 0))],
        out_shape=[jax.ShapeDtypeStruct((s, n * e4), F32), jax.ShapeDtypeStruct((s, d), BF16)],
        compiler_params=_params("arbitrary", "arbitrary"),
    )(x, w_in)


def _conv_call(proj, conv_w, conv_b, c):
    s = proj.shape[0]
    t = min(SEQ_TILE, s)
    n_tiles = s // t

    def body(u_ref, up_ref, un_ref, w_ref, b_ref, xc_ref):
        i = pl.program_id(0)
        rows = lax.broadcasted_iota(jnp.int32, (SUBLANES, c), 0)
        w = w_ref[...]
        b = b_ref[...]

        def chunk(r0, _):
            prv, cur, nxt = _neighbour_chunks(u_ref, up_ref, un_ref, r0, t, slice(None), i == 0, i == n_tiles - 1)
            acc = b + w[1:2] * cur
            acc += w[0:1] * _shift_rows(prv, cur, nxt, -1, rows)
            acc += w[2:3] * _shift_rows(prv, cur, nxt, 1, rows)
            acc += w[3:4] * _shift_rows(prv, cur, nxt, 2, rows)
            xc_ref[pl.ds(r0, SUBLANES), :] = acc

        _chunk_loop(t, chunk)

    return pl.pallas_call(
        body, name="conv_fwd", grid=(n_tiles,),
        in_specs=_halo_specs(t, s, c, 1) + [pl.BlockSpec((4, c), lambda i: (0, 0)), pl.BlockSpec((1, c), lambda i: (0, 0))],
        out_specs=pl.BlockSpec((t, c), lambda i: (i, 0)),
        out_shape=jax.ShapeDtypeStruct((s, c), F32),
        compiler_params=_params("arbitrary"),
    )(proj, proj, proj, conv_w, conv_b)


def _gate_matmuls(xc_ref, wa_ref, wi_ref, pr_s, pi_s, heads):
    for h in range(heads):
        cs = pl.ds(h * LRU_HEAD, LRU_HEAD)
        xb = xc_ref[:, cs].astype(BF16)
        pr_s[:, cs] = jnp.dot(xb, wa_ref[h], preferred_element_type=F32)
        pi_s[:, cs] = jnp.dot(xb, wi_ref[h], preferred_element_type=F32)


def _rg_gates(pr, pi, ba, bi, sp):
    r = _sigmoid(pr + ba)
    ig = _sigmoid(pi + bi)
    log_a = (-RG_C * r) * sp
    a = jnp.exp(log_a)
    mult = jnp.sqrt(_neg_expm1(2.0 * log_a))
    return r, ig, a, mult


def _scan_fwd_call(xc, wa, wi, ba, bi, lam, reverse):
    s, c = xc.shape
    heads = c // LRU_HEAD
    t = min(SEQ_TILE, s)
    n_tiles = s // t
    tile = (lambda i: (n_tiles - 1 - i, 0)) if reverse else (lambda i: (i, 0))
    whole2 = lambda i: (0, 0)
    whole3 = lambda i: (0, 0, 0)

    def body(xc_ref, wa_ref, wi_ref, ba_ref, bi_ref, lam_ref, h_ref, pr_s, pi_s, carry_s):
        @pl.when(pl.program_id(0) == 0)
        def _():
            carry_s[...] = jnp.zeros_like(carry_s)

        _gate_matmuls(xc_ref, wa_ref, wi_ref, pr_s, pi_s, heads)
        ba_v, bi_v = ba_ref[...], bi_ref[...]
        sp = _softplus(-lam_ref[...])

        def chunk(r0, _):
            rs = pl.ds(r0, SUBLANES)
            xcv = xc_ref[rs, :]
            _, ig, a, mult = _rg_gates(pr_s[rs, :], pi_s[rs, :], ba_v, bi_v, sp)
            pr_s[rs, :] = a
            pi_s[rs, :] = mult * ig * xcv

        _chunk_loop(t, chunk)

        def row(j, h):
            r = (t - 1 - j) if reverse else j
            h = pr_s[pl.ds(r, 1), :] * h + pi_s[pl.ds(r, 1), :]
            h_ref[pl.ds(r, 1), :] = h
            return h

        carry_s[...] = lax.fori_loop(0, t, row, carry_s[...], unroll=8)

    return pl.pallas_call(
        body, name="scan_fwd_rev" if reverse else "scan_fwd", grid=(n_tiles,),
        in_specs=[pl.BlockSpec((t, c), tile),
                  pl.BlockSpec((heads, LRU_HEAD, LRU_HEAD), whole3), pl.BlockSpec((heads, LRU_HEAD, LRU_HEAD), whole3),
                  pl.BlockSpec((1, c), whole2), pl.BlockSpec((1, c), whole2), pl.BlockSpec((1, c), whole2)],
        out_specs=pl.BlockSpec((t, c), tile),
        out_shape=jax.ShapeDtypeStruct((s, c), F32),
        scratch_shapes=[pltpu.VMEM((t, c), F32), pltpu.VMEM((t, c), F32), pltpu.VMEM((1, c), F32)],
        compiler_params=_params("arbitrary"),
    )(xc, wa, wi, ba, bi, lam)


def _window_counts(r0, tile_idx, t_rows, n_rows, half, shape):
    pos = tile_idx * t_rows + r0 + lax.broadcasted_iota(jnp.int32, shape, 0)
    hi = jnp.minimum(pos + half, n_rows)
    lo = jnp.maximum(pos - half, 0)
    return (hi - lo).astype(F32)


def _pool_combine_call(proj, h_f, h_b, w_pool, pool_scale, p):
    s = proj.shape[0]
    c = h_f.shape[1]
    pg = p // N_POOL_GROUPS
    t = min(SEQ_TILE, s)
    n_tiles = s // t

    def body(u_ref, up_ref, un_ref, gate_ref, hf_ref, hb_ref, wp_ref, sc_ref, y_ref, d_ref, d_s, yr_s):
        i = pl.program_id(0)
        rows = lax.broadcasted_iota(jnp.int32, (SUBLANES, pg), 0)

        def chunk(r0, _):
            rs = pl.ds(r0, SUBLANES)
            for g, w in enumerate(POOL_WINDOWS):
                cols = pl.ds(g * pg, pg)
                prv, cur, nxt = _neighbour_chunks(u_ref, up_ref, un_ref, r0, t, cols, i == 0, i == n_tiles - 1)
                tot = cur
                for o in range(-(w // 2), w // 2):
                    if o != 0:
                        tot = tot + _shift_rows(prv, cur, nxt, o, rows)
                cnt = _window_counts(r0, i, t, s, w // 2, (SUBLANES, pg))
                d_s[rs, cols] = tot / cnt - cur
            gate, _ = _gelu_and_grad(gate_ref[rs, :])
            yr_s[rs, :] = (hf_ref[rs, :] + hb_ref[rs, :]) * gate

        _chunk_loop(t, chunk)
        y_ref[:, pl.ds(p, c)] = yr_s[...].astype(BF16)
        d_ref[...] = d_s[...].astype(BF16)
        for g in range(N_POOL_GROUPS):
            cols = pl.ds(g * pg, pg)
            out = jnp.dot(d_s[:, cols].astype(BF16), wp_ref[g], preferred_element_type=F32)
            y_ref[:, cols] = (out * sc_ref[:, cols]).astype(BF16)

    return pl.pallas_call(
        body, name="pool_combine", grid=(n_tiles,),
        in_specs=_halo_specs(t, s, p, 0) + [
            pl.BlockSpec((t, c), lambda i: (i, 2)),
            pl.BlockSpec((t, c), lambda i: (i, 0)), pl.BlockSpec((t, c), lambda i: (i, 0)),
            pl.BlockSpec((N_POOL_GROUPS, pg, pg), lambda i: (0, 0, 0)), pl.BlockSpec((1, p), lambda i: (0, 0))],
        out_specs=[pl.BlockSpec((t, p + c), lambda i: (i, 0)), pl.BlockSpec((t, p), lambda i: (i, 0))],
        out_shape=[jax.ShapeDtypeStruct((s, p + c), BF16), jax.ShapeDtypeStruct((s, p), BF16)],
        scratch_shapes=[pltpu.VMEM((t, p), F32), pltpu.VMEM((t, c), F32)],
        compiler_params=_params("arbitrary"),
    )(proj, proj, proj, proj, h_f, h_b, w_pool, pool_scale)


def _layer_norm_rows(z, g, b):
    mu = jnp.mean(z, axis=-1, keepdims=True)
    zc = z - mu
    var = jnp.mean(zc * zc, axis=-1, keepdims=True)
    rstd = lax.rsqrt(var + LN_EPS)
    xh = zc * rstd
    return xh, rstd, xh * g + b


def _layer_norm_bwd_rows(dx, xh, rstd, g):
    dxh = dx * g
    m1 = jnp.mean(dxh, axis=-1, keepdims=True)
    m2 = jnp.mean(dxh * xh, axis=-1, keepdims=True)
    return rstd * (dxh - m1 - xh * m2)


def _out_ln1_call(y, w_out, x, g1, b1):
    s, d = x.shape
    tm = min(SEQ_TILE, s)

    def body(y_ref, w_ref, x_ref, g_ref, b_ref, xh_ref, x1b_ref, rstd_ref, acc_s, x1_s):
        acc_s[...] = jnp.dot(y_ref[...], w_ref[...], preferred_element_type=F32)
        g, b = g_ref[...], b_ref[...]

        def chunk(r0, _):
            rs = pl.ds(r0, SUBLANES)
            xh, rstd, x1 = _layer_norm_rows(ALPHA * x_ref[rs, :] + acc_s[rs, :], g, b)
            xh_ref[rs, :] = xh
            x1_s[rs, :] = x1
            rstd_ref[rs, :] = rstd

        _chunk_loop(tm, chunk, unroll=LN_UNROLL)
        x1b_ref[...] = x1_s[...].astype(BF16)

    return pl.pallas_call(
        body, name="out_ln1", grid=(s // tm,),
        in_specs=[pl.BlockSpec((tm, d), lambda i: (i, 0)), pl.BlockSpec((d, d), lambda i: (0, 0)),
                  pl.BlockSpec((tm, d), lambda i: (i, 0)),
                  pl.BlockSpec((1, d), lambda i: (0, 0)), pl.BlockSpec((1, d), lambda i: (0, 0))],
        out_specs=[pl.BlockSpec((tm, d), lambda i: (i, 0)), pl.BlockSpec((tm, d), lambda i: (i, 0)),
                   pl.BlockSpec((tm, 1), lambda i: (i, 0))],
        out_shape=[jax.ShapeDtypeStruct((s, d), F32), jax.ShapeDtypeStruct((s, d), BF16), jax.ShapeDtypeStruct((s, 1), F32)],
        scratch_shapes=[pltpu.VMEM((tm, d), F32), pltpu.VMEM((tm, d), F32)],
        compiler_params=_params("arbitrary"),
    )(y, w_out, x, g1, b1)


def _mlp_in_call(x1b, w1):
    s, d = x1b.shape
    n, _, f4 = w1.shape
    tm = min(MM_TILE, s)
    tn = min(1024, f4)
    per = f4 // tn

    def body(x_ref, w_ref, r_ref, q_ref):
        r = jnp.maximum(jnp.dot(x_ref[...], w_ref[...], preferred_element_type=F32), 0.0)
        r_ref[...] = r.astype(BF16)
        q_ref[...] = (r * r).astype(BF16)

    return pl.pallas_call(
        body, name="mlp_in", grid=(n * per, s // tm),
        in_specs=[pl.BlockSpec((tm, d), lambda j, i: (i, 0)), pl.BlockSpec((None, d, tn), lambda j, i: (j // per, 0, j % per))],
        out_specs=[pl.BlockSpec((tm, tn), lambda j, i: (i, j)), pl.BlockSpec((tm, tn), lambda j, i: (i, j))],
        out_shape=[jax.ShapeDtypeStruct((s, n * f4), BF16), jax.ShapeDtypeStruct((s, n * f4), BF16)],
        compiler_params=_params("arbitrary", "arbitrary"),
    )(x1b, w1)


def _mlp_out_ln2_call(hsq, w2, xh1, g1, b1, g2, b2, target):
    s, f = hsq.shape
    d = w2.shape[1]
    tm = min(MM_TILE, s)
    tk = min(1024, f)
    nk = f // tk

    def body(h_ref, w_ref, xh1_ref, g1_ref, b1_ref, g2_ref, b2_ref, t_ref,
             dz_ref, dzb_ref, loss_ref, dg_ref, db_ref, acc_s):
        i, k = pl.program_id(0), pl.program_id(1)

        @pl.when((i == 0) & (k == 0))
        def _():
            loss_ref[...] = jnp.zeros_like(loss_ref)
            dg_ref[...] = jnp.zeros_like(dg_ref)
            db_ref[...] = jnp.zeros_like(db_ref)

        @pl.when(k == 0)
        def _():
            acc_s[...] = jnp.zeros_like(acc_s)

        acc_s[...] += jnp.dot(h_ref[...], w_ref[...], preferred_element_type=F32)

        @pl.when(k == nk - 1)
        def _():
            g1, b1, g2, b2 = g1_ref[...], b1_ref[...], g2_ref[...], b2_ref[...]

            def chunk(r0, _):
                rs = pl.ds(r0, SUBLANES)
                x1 = xh1_ref[rs, :] * g1 + b1
                xh2, rstd, x2 = _layer_norm_rows(ALPHA * x1 + acc_s[rs, :], g2, b2)
                diff = x2 - t_ref[rs, :]
                loss_ref[...] += diff * diff
                dx2 = diff * (1.0 / d)
                dg_ref[...] += dx2 * xh2
                db_ref[...] += dx2
                dz = _layer_norm_bwd_rows(dx2, xh2, rstd, g2)
                dz_ref[rs, :] = dz

            _chunk_loop(tm, chunk, unroll=LN_UNROLL)
            dzb_ref[...] = dz_ref[...].astype(BF16)

    row = lambda i, k: (i, 0)
    vec = lambda i, k: (0, 0)
    return pl.pallas_call(
        body, name="mlp_out_ln2", grid=(s // tm, nk),
        in_specs=[pl.BlockSpec((tm, tk), lambda i, k: (i, k)), pl.BlockSpec((tk, d), lambda i, k: (k, 0)),
                  pl.BlockSpec((tm, d), row), pl.BlockSpec((1, d), vec), pl.BlockSpec((1, d), vec),
                  pl.BlockSpec((1, d), vec), pl.BlockSpec((1, d), vec), pl.BlockSpec((tm, d), row)],
        out_specs=[pl.BlockSpec((tm, d), row), pl.BlockSpec((tm, d), row),
                   pl.BlockSpec((SUBLANES, d), vec), pl.BlockSpec((SUBLANES, d), vec), pl.BlockSpec((SUBLANES, d), vec)],
        out_shape=[jax.ShapeDtypeStruct((s, d), F32), jax.ShapeDtypeStruct((s, d), BF16),
                   jax.ShapeDtypeStruct((SUBLANES, d), F32), jax.ShapeDtypeStruct((SUBLANES, d), F32),
                   jax.ShapeDtypeStruct((SUBLANES, d), F32)],
        scratch_shapes=[pltpu.VMEM((tm, d), F32)],
        compiler_params=_params("arbitrary", "arbitrary"),
    )(hsq, w2, xh1, g1, b1, g2, b2, target)


def _weight_grad_call(a, b, tm, tn, out_shape, out_map, name):
    s, m = a.shape
    n = b.shape[1]
    tk = min(1024, s)

    def body(a_ref, b_ref, o_ref):
        @pl.when(pl.program_id(2) == 0)
        def _():
            o_ref[...] = jnp.zeros_like(o_ref)

        o_ref[...] += lax.dot_general(a_ref[...], b_ref[...], (((0,), (0,)), ((), ())), preferred_element_type=F32)

    return pl.pallas_call(
        body, name=name, grid=(m // tm, n // tn, s // tk),
        in_specs=[pl.BlockSpec((tk, tm), lambda i, j, k: (k, i)), pl.BlockSpec((tk, tn), lambda i, j, k: (k, j))],
        out_specs=pl.BlockSpec((None, None, tm, tn), lambda i, j, k: out_map(i, j)),
        out_shape=jax.ShapeDtypeStruct(out_shape, F32),
        compiler_params=_params("arbitrary", "arbitrary", "arbitrary"),
    )(a, b)


def _row_sharded_grad(a, b, name):
    m, n = a.shape[1], b.shape[1]
    half_rows = m // (2 * N_CHIPS)
    tm = min(1024, half_rows)
    per = half_rows // tm
    tn = min(1024, n)
    return _weight_grad_call(a, b, tm, tn, (2, N_CHIPS, half_rows, n),
                             lambda i, j: ((i // per) % 2, i // (2 * per), i % per, j), name)


def _col_sharded_grad(a, b, name):
    m, n = a.shape[1], b.shape[1]
    half_rows, shard_cols = m // 2, n // N_CHIPS
    tm = min(1024, half_rows)
    per_m = half_rows // tm
    tn = shard_cols if shard_cols % 1024 else 1024
    per_n = shard_cols // tn
    return _weight_grad_call(a, b, tm, tn, (2, N_CHIPS, half_rows, shard_cols),
                             lambda i, j: (i // per_m, j // per_n, i % per_m, j % per_n), name)


def _dhsq_call(dzb, w2, r):
    s, d = dzb.shape
    f = w2.shape[0]
    tm = min(MM_TILE, s)
    tn = min(1024, f)

    def body(dz_ref, w_ref, r_ref, o_ref):
        dh = lax.dot_general(dz_ref[...], w_ref[...], (((1,), (1,)), ((), ())), preferred_element_type=F32)
        o_ref[...] = (dh * (2.0 * r_ref[...].astype(F32))).astype(BF16)

    return pl.pallas_call(
        body, name="mlp_dpre", grid=(f // tn, s // tm),
        in_specs=[pl.BlockSpec((tm, d), lambda j, i: (i, 0)), pl.BlockSpec((tn, d), lambda j, i: (j, 0)),
                  pl.BlockSpec((tm, tn), lambda j, i: (i, j))],
        out_specs=pl.BlockSpec((tm, tn), lambda j, i: (i, j)),
        out_shape=jax.ShapeDtypeStruct((s, f), BF16),
        compiler_params=_params("arbitrary", "arbitrary"),
    )(dzb, w2, r)


def _dx1_ln1_bwd_call(dpre, w1, dz2, xh1, rstd1, g1):
    s, f = dpre.shape
    n, d, f4 = w1.shape
    tm = min(MM_TILE, s)
    tk = min(1024, f4)
    per = f4 // tk
    nk = n * per

    def body(dp_ref, w_ref, dz2_ref, xh_ref, rstd_ref, g_ref, dz_ref, dzb_ref, dg_ref, db_ref, acc_s):
        i, k = pl.program_id(0), pl.program_id(1)

        @pl.when((i == 0) & (k == 0))
        def _():
            dg_ref[...] = jnp.zeros_like(dg_ref)
            db_ref[...] = jnp.zeros_like(db_ref)

        @pl.when(k == 0)
        def _():
            acc_s[...] = jnp.zeros_like(acc_s)

        acc_s[...] += lax.dot_general(dp_ref[...], w_ref[...], (((1,), (1,)), ((), ())), preferred_element_type=F32)

        @pl.when(k == nk - 1)
        def _():
            g = g_ref[...]

            def chunk(r0, _):
                rs = pl.ds(r0, SUBLANES)
                dx1 = acc_s[rs, :] + ALPHA * dz2_ref[rs, :]
                xh = xh_ref[rs, :]
                dg_ref[...] += dx1 * xh
                db_ref[...] += dx1
                dz = _layer_norm_bwd_rows(dx1, xh, rstd_ref[rs, :], g)
                dz_ref[rs, :] = dz

            _chunk_loop(tm, chunk, unroll=LN_UNROLL)
            dzb_ref[...] = dz_ref[...].astype(BF16)

    row = lambda i, k: (i, 0)
    vec = lambda i, k: (0, 0)
    return pl.pallas_call(
        body, name="dx1_ln1_bwd", grid=(s // tm, nk),
        in_specs=[pl.BlockSpec((tm, tk), lambda i, k: (i, k)),
                  pl.BlockSpec((None, d, tk), lambda i, k: (k // per, 0, k % per)),
                  pl.BlockSpec((tm, d), row), pl.BlockSpec((tm, d), row), pl.BlockSpec((tm, 1), row),
                  pl.BlockSpec((1, d), vec)],
        out_specs=[pl.BlockSpec((tm, d), row), pl.BlockSpec((tm, d), row),
                   pl.BlockSpec((SUBLANES, d), vec), pl.BlockSpec((SUBLANES, d), vec)],
        out_shape=[jax.ShapeDtypeStruct((s, d), F32), jax.ShapeDtypeStruct((s, d), BF16),
                   jax.ShapeDtypeStruct((SUBLANES, d), F32), jax.ShapeDtypeStruct((SUBLANES, d), F32)],
        scratch_shapes=[pltpu.VMEM((tm, d), F32)],
        compiler_params=_params("arbitrary", "arbitrary"),
    )(dpre, w1, dz2, xh1, rstd1, g1)


def _dy_call(dzb, w_out):
    s, d = dzb.shape
    e = w_out.shape[0]
    tm = min(MM_TILE, s)

    def body(dz_ref, w_ref, o_ref):
        o_ref[...] = lax.dot_general(dz_ref[...], w_ref[...], (((1,), (1,)), ((), ())), preferred_element_type=F32)

    return pl.pallas_call(
        body, name="dy", grid=(s // tm,),
        in_specs=[pl.BlockSpec((tm, d), lambda i: (i, 0)), pl.BlockSpec((e, d), lambda i: (0, 0))],
        out_specs=pl.BlockSpec((tm, e), lambda i: (i, 0)),
        out_shape=jax.ShapeDtypeStruct((s, e), F32),
        compiler_params=_params("arbitrary"),
    )(dzb, w_out)


def _mixer_bwd_call(dy, d_pool, proj, h_f, h_b, w_pool, pool_scale, p):
    s = dy.shape[0]
    c = h_f.shape[1]
    pg = p // N_POOL_GROUPS
    t = min(SEQ_TILE, s)
    n_tiles = s // t

    def body(dyp_ref, dyr_ref, d_ref, gate_ref, hf_ref, hb_ref, wp_ref, sc_ref,
             e_ref, dh_ref, dgate_ref, dwp_ref, dsc_ref, dd_s):
        i = pl.program_id(0)

        @pl.when(i == 0)
        def _():
            dwp_ref[...] = jnp.zeros_like(dwp_ref)
            dsc_ref[...] = jnp.zeros_like(dsc_ref)

        for g in range(N_POOL_GROUPS):
            cols = pl.ds(g * pg, pg)
            dg = d_ref[:, cols]
            out = jnp.dot(dg, wp_ref[g], preferred_element_type=F32)
            dyp = dyp_ref[:, cols]
            prod = dyp * out
            dsc_ref[:, cols] += jnp.sum(prod.reshape(t // SUBLANES, SUBLANES, pg), axis=0)
            dout = (dyp * sc_ref[:, cols]).astype(BF16)
            dwp_ref[g] += lax.dot_general(dg, dout, (((0,), (0,)), ((), ())), preferred_element_type=F32)
            dd_s[:, cols] = lax.dot_general(dout, wp_ref[g], (((1,), (1,)), ((), ())), preferred_element_type=F32)

        def chunk(r0, _):
            rs = pl.ds(r0, SUBLANES)
            for g, w in enumerate(POOL_WINDOWS):
                cols = pl.ds(g * pg, pg)
                cnt = _window_counts(r0, i, t, s, w // 2, (SUBLANES, pg))
                e_ref[rs, cols] = dd_s[rs, cols] / cnt
            gate, dgate = _gelu_and_grad(gate_ref[rs, :])
            dyr = dyr_ref[rs, :]
            dh_ref[rs, :] = dyr * gate
            dd_s[rs, :] = dyr * (hf_ref[rs, :] + hb_ref[rs, :]) * dgate

        _chunk_loop(t, chunk)
        dgate_ref[...] = dd_s[...].astype(BF16)

    tile = lambda i: (i, 0)
    return pl.pallas_call(
        body, name="mixer_bwd", grid=(n_tiles,),
        in_specs=[pl.BlockSpec((t, p), tile), pl.BlockSpec((t, c), lambda i: (i, 1)), pl.BlockSpec((t, p), tile),
                  pl.BlockSpec((t, c), lambda i: (i, 2)), pl.BlockSpec((t, c), tile), pl.BlockSpec((t, c), tile),
                  pl.BlockSpec((N_POOL_GROUPS, pg, pg), lambda i: (0, 0, 0)), pl.BlockSpec((1, p), lambda i: (0, 0))],
        out_specs=[pl.BlockSpec((t, p), tile), pl.BlockSpec((t, c), tile), pl.BlockSpec((t, c), tile),
                   pl.BlockSpec((N_POOL_GROUPS, pg, pg), lambda i: (0, 0, 0)), pl.BlockSpec((SUBLANES, p), lambda i: (0, 0))],
        out_shape=[jax.ShapeDtypeStruct((s, p), F32), jax.ShapeDtypeStruct((s, c), F32), jax.ShapeDtypeStruct((s, c), BF16),
                   jax.ShapeDtypeStruct((N_POOL_GROUPS, pg, pg), F32), jax.ShapeDtypeStruct((SUBLANES, p), F32)],
        scratch_shapes=[pltpu.VMEM((t, p), F32)],
        compiler_params=_params("arbitrary"),
    )(dy, dy, d_pool, proj, h_f, h_b, w_pool, pool_scale)


def _scan_bwd_call(xc, dh, h_dir, dxc_prev, wa, wi, ba, bi, lam, reverse):
    s, c = xc.shape
    heads = c // LRU_HEAD
    t = min(SEQ_TILE, s)
    n_tiles = s // t
    per = t // SUBLANES
    last_blk = s // SUBLANES - 1
    tile = (lambda i: (i, 0)) if reverse else (lambda i: (n_tiles - 1 - i, 0))
    if reverse:
        halo = lambda i: (jnp.minimum((i + 1) * per, last_blk), 0)
    else:
        halo = lambda i: (jnp.maximum((n_tiles - 1 - i) * per - 1, 0), 0)
    whole2 = lambda i: (0, 0)
    whole3 = lambda i: (0, 0, 0)
    has_prev = dxc_prev is not None

    def body(*refs):
        (xc_ref, dh_ref, h_ref, hh_ref) = refs[:4]
        k = 4
        prev_ref = None
        if has_prev:
            prev_ref = refs[k]
            k += 1
        wa_ref, wi_ref, ba_ref, bi_ref, lam_ref = refs[k:k + 5]
        dxc_ref, dwa_ref, dwi_ref, dba_ref, dbi_ref, dsp_ref = refs[k + 5:k + 11]
        pr_s, pi_s, g_s, carry_s = refs[k + 11:]
        step = pl.program_id(0)
        tile_idx = step if reverse else n_tiles - 1 - step

        @pl.when(step == 0)
        def _():
            carry_s[...] = jnp.zeros_like(carry_s)
            dwa_ref[...] = jnp.zeros_like(dwa_ref)
            dwi_ref[...] = jnp.zeros_like(dwi_ref)
            dba_ref[...] = jnp.zeros_like(dba_ref)
            dbi_ref[...] = jnp.zeros_like(dbi_ref)
            dsp_ref[...] = jnp.zeros_like(dsp_ref)

        _gate_matmuls(xc_ref, wa_ref, wi_ref, pr_s, pi_s, heads)
        ba_v, bi_v = ba_ref[...], bi_ref[...]
        sp = _softplus(-lam_ref[...])
        rows = lax.broadcasted_iota(jnp.int32, (SUBLANES, c), 0)

        def gates(r0, _):
            rs = pl.ds(r0, SUBLANES)
            _, _, a, _ = _rg_gates(pr_s[rs, :], pi_s[rs, :], ba_v, bi_v, sp)
            g_s[rs, :] = a

        _chunk_loop(t, gates)

        def row(j, carry):
            r = j if reverse else (t - 1 - j)
            gt = dh_ref[pl.ds(r, 1), :] + carry
            carry = g_s[pl.ds(r, 1), :] * gt
            g_s[pl.ds(r, 1), :] = gt
            return carry

        carry_s[...] = lax.fori_loop(0, t, row, carry_s[...], unroll=8)

        def chunk(r0, _):
            rs = pl.ds(r0, SUBLANES)
            xcv = xc_ref[rs, :]
            r, ig, a, mult = _rg_gates(pr_s[rs, :], pi_s[rs, :], ba_v, bi_v, sp)
            gt = g_s[rs, :]
            cur = h_ref[rs, :]
            if reverse:
                after = h_ref[pl.ds(pl.multiple_of(jnp.minimum(r0 + SUBLANES, t - SUBLANES), SUBLANES), SUBLANES), :]
                edge = jnp.where(tile_idx == n_tiles - 1, 0.0, hh_ref[...])
                nxt = jnp.where(r0 == t - SUBLANES, edge, after)
                hs = _shift_rows(cur, cur, nxt, 1, rows)
            else:
                before = h_ref[pl.ds(pl.multiple_of(jnp.maximum(r0 - SUBLANES, 0), SUBLANES), SUBLANES), :]
                edge = jnp.where(tile_idx == 0, 0.0, hh_ref[...])
                prv = jnp.where(r0 == 0, edge, before)
                hs = _shift_rows(prv, cur, cur, -1, rows)
            gx = gt * xcv
            dmult = gx * ig
            di = gx * mult
            dlog_a = (gt * hs) * a - dmult * (a * a) / mult
            dr = dlog_a * (-RG_C * sp)
            dsp_ref[...] += dlog_a * (-RG_C * r)
            dpr = dr * r * (1.0 - r)
            dpi = di * ig * (1.0 - ig)
            dba_ref[...] += dpr
            dbi_ref[...] += dpi
            direct = gt * mult * ig
            if has_prev:
                direct = direct + prev_ref[rs, :]
            dxc_ref[rs, :] = direct
            pr_s[rs, :] = dpr
            pi_s[rs, :] = dpi

        _chunk_loop(t, chunk)

        for h in range(heads):
            cs = pl.ds(h * LRU_HEAD, LRU_HEAD)
            xb = xc_ref[:, cs].astype(BF16)
            dprb = pr_s[:, cs].astype(BF16)
            dpib = pi_s[:, cs].astype(BF16)
            dwa_ref[h] += lax.dot_general(xb, dprb, (((0,), (0,)), ((), ())), preferred_element_type=F32)
            dwi_ref[h] += lax.dot_general(xb, dpib, (((0,), (0,)), ((), ())), preferred_element_type=F32)
            dxc_ref[:, cs] += (
                lax.dot_general(dprb, wa_ref[h], (((1,), (1,)), ((), ())), preferred_element_type=F32)
                + lax.dot_general(dpib, wi_ref[h], (((1,), (1,)), ((), ())), preferred_element_type=F32))

    in_specs = [pl.BlockSpec((t, c), tile), pl.BlockSpec((t, c), tile), pl.BlockSpec((t, c), tile),
                pl.BlockSpec((SUBLANES, c), halo)]
    args = [xc, dh, h_dir, h_dir]
    if has_prev:
        in_specs.append(pl.BlockSpec((t, c), tile))
        args.append(dxc_prev)
    in_specs += [pl.BlockSpec((heads, LRU_HEAD, LRU_HEAD), whole3), pl.BlockSpec((heads, LRU_HEAD, LRU_HEAD), whole3),
                 pl.BlockSpec((1, c), whole2), pl.BlockSpec((1, c), whole2), pl.BlockSpec((1, c), whole2)]
    args += [wa, wi, ba, bi, lam]
    return pl.pallas_call(
        body, name="scan_bwd_rev" if reverse else "scan_bwd", grid=(n_tiles,),
        in_specs=in_specs,
        out_specs=[pl.BlockSpec((t, c), tile),
                   pl.BlockSpec((heads, LRU_HEAD, LRU_HEAD), whole3), pl.BlockSpec((heads, LRU_HEAD, LRU_HEAD), whole3),
                   pl.BlockSpec((SUBLANES, c), whole2), pl.BlockSpec((SUBLANES, c), whole2), pl.BlockSpec((SUBLANES, c), whole2)],
        out_shape=[jax.ShapeDtypeStruct((s, c), F32),
                   jax.ShapeDtypeStruct((heads, LRU_HEAD, LRU_HEAD), F32), jax.ShapeDtypeStruct((heads, LRU_HEAD, LRU_HEAD), F32),
                   jax.ShapeDtypeStruct((SUBLANES, c), F32), jax.ShapeDtypeStruct((SUBLANES, c), F32),
                   jax.ShapeDtypeStruct((SUBLANES, c), F32)],
        scratch_shapes=[pltpu.VMEM((t, c), F32), pltpu.VMEM((t, c), F32), pltpu.VMEM((t, c), F32), pltpu.VMEM((1, c), F32)],
        compiler_params=_params("arbitrary"),
    )(*args)


def _dproj_call(e_pool, dxc, proj, dgate, conv_w, p):
    s, c = dxc.shape
    pg = p // N_POOL_GROUPS
    t = min(SEQ_TILE, s)
    n_tiles = s // t

    def body(e_ref, ep_ref, en_ref, dx_ref, dxp_ref, dxn_ref, u_ref, up_ref, un_ref, dgate_ref, w_ref,
             dproj_ref, dcw_ref, dcb_ref, st_s):
        i = pl.program_id(0)
        first, last = i == 0, i == n_tiles - 1

        @pl.when(first)
        def _():
            dcw_ref[...] = jnp.zeros_like(dcw_ref)
            dcb_ref[...] = jnp.zeros_like(dcb_ref)

        rows_p = lax.broadcasted_iota(jnp.int32, (SUBLANES, pg), 0)
        rows_c = lax.broadcasted_iota(jnp.int32, (SUBLANES, c), 0)
        w = w_ref[...]

        def chunk(r0, _):
            rs = pl.ds(r0, SUBLANES)
            for g, win in enumerate(POOL_WINDOWS):
                cols = pl.ds(g * pg, pg)
                prv, cur, nxt = _neighbour_chunks(e_ref, ep_ref, en_ref, r0, t, cols, first, last)
                tot = cur
                for o in range(-(win // 2) + 1, win // 2 + 1):
                    if o != 0:
                        tot = tot + _shift_rows(prv, cur, nxt, o, rows_p)
                cnt = _window_counts(r0, i, t, s, win // 2, (SUBLANES, pg))
                st_s[rs, cols] = tot - cur * cnt
            prv, cur, nxt = _neighbour_chunks(dx_ref, dxp_ref, dxn_ref, r0, t, slice(None), first, last)
            du = w[1:2] * cur
            du += w[0:1] * _shift_rows(prv, cur, nxt, 1, rows_c)
            du += w[2:3] * _shift_rows(prv, cur, nxt, -1, rows_c)
            du += w[3:4] * _shift_rows(prv, cur, nxt, -2, rows_c)
            st_s[rs, pl.ds(p, c)] = du
            uprv, ucur, unxt = _neighbour_chunks(u_ref, up_ref, un_ref, r0, t, slice(None), first, last)
            dcb_ref[...] += cur
            for j, o in enumerate((-1, 0, 1, 2)):
                dcw_ref[j] += cur * _shift_rows(uprv, ucur, unxt, o, rows_c)

        _chunk_loop(t, chunk)
        dproj_ref[:, pl.ds(0, p + c)] = st_s[...].astype(BF16)
        dproj_ref[:, pl.ds(p + c, c)] = dgate_ref[...]

    return pl.pallas_call(
        body, name="dproj", grid=(n_tiles,),
        in_specs=_halo_specs(t, s, p, 0) + _halo_specs(t, s, c, 0) + _halo_specs(t, s, c, 1) + [
            pl.BlockSpec((t, c), lambda i: (i, 0)), pl.BlockSpec((4, c), lambda i: (0, 0))],
        out_specs=[pl.BlockSpec((t, p + 2 * c), lambda i: (i, 0)),
                   pl.BlockSpec((4, SUBLANES, c), lambda i: (0, 0, 0)), pl.BlockSpec((SUBLANES, c), lambda i: (0, 0))],
        out_shape=[jax.ShapeDtypeStruct((s, p + 2 * c), BF16), jax.ShapeDtypeStruct((4, SUBLANES, c), F32),
                   jax.ShapeDtypeStruct((SUBLANES, c), F32)],
        scratch_shapes=[pltpu.VMEM((t, p + c), F32)],
        compiler_params=_params("arbitrary"),
    )(e_pool, e_pool, e_pool, dxc, dxc, dxc, proj, proj, proj, dgate, conv_w)


def _dx_call(dproj, w_in, dz1):
    s, e = dproj.shape
    n, d, e4 = w_in.shape
    tm = min(MM_TILE, s)

    def body(dp_ref, w_ref, dz_ref, o_ref):
        k = pl.program_id(1)

        @pl.when(k == 0)
        def _():
            o_ref[...] = ALPHA * dz_ref[...]

        o_ref[...] += lax.dot_general(dp_ref[...], w_ref[...], (((1,), (1,)), ((), ())), preferred_element_type=F32)

    return pl.pallas_call(
        body, name="grad_x", grid=(s // tm, n),
        in_specs=[pl.BlockSpec((tm, e4), lambda i, k: (i, k)), pl.BlockSpec((None, d, e4), lambda i, k: (k, 0, 0)),
                  pl.BlockSpec((tm, d), lambda i, k: (i, 0))],
        out_specs=pl.BlockSpec((tm, d), lambda i, k: (i, 0)),
        out_shape=jax.ShapeDtypeStruct((s, d), F32),
        compiler_params=_params("arbitrary", "arbitrary"),
    )(dproj, w_in, dz1)


def _row_tile(rows, cols, n_arrays):
    limit = max(SUBLANES, ELT_BLOCK_BYTES // (4 * cols * max(1, n_arrays // 4)))
    best = SUBLANES
    for cand in range(SUBLANES, min(rows, limit) + 1, SUBLANES):
        if rows % cand == 0:
            best = cand
    return best if rows % SUBLANES == 0 else rows


def _cast_to_slot_call(a, idx, dtype, name):
    rows, cols = a.shape
    tr = _row_tile(rows, cols, 2)

    def body(idx_ref, a_ref, o_ref):
        o_ref[...] = a_ref[...].astype(dtype)

    return pl.pallas_call(
        body, name=name,
        grid_spec=pltpu.PrefetchScalarGridSpec(
            num_scalar_prefetch=1, grid=(rows // tr,),
            in_specs=[pl.BlockSpec((tr, cols), lambda i, idx_ref: (i, 0))],
            out_specs=pl.BlockSpec((None, tr, cols), lambda i, idx_ref: (idx_ref[1], i, 0))),
        out_shape=jax.ShapeDtypeStruct((N_CHIPS, rows, cols), dtype),
        compiler_params=_params("arbitrary"),
    )(idx, a)


def _add_half_call(g, recv, idx, to_slot, name):
    _, rows, cols = g.shape
    tr = _row_tile(rows, cols, 3)

    def body(idx_ref, g_ref, r_ref, o_ref):
        o_ref[...] = g_ref[...] + r_ref[...]

    if to_slot:
        out_spec = pl.BlockSpec((None, tr, cols), lambda i, idx_ref: (idx_ref[1], i, 0))
        out_shape = jax.ShapeDtypeStruct((N_CHIPS, rows, cols), F32)
    else:
        out_spec = pl.BlockSpec((tr, cols), lambda i, idx_ref: (i, 0))
        out_shape = jax.ShapeDtypeStruct((rows, cols), F32)
    return pl.pallas_call(
        body, name=name,
        grid_spec=pltpu.PrefetchScalarGridSpec(
            num_scalar_prefetch=1, grid=(rows // tr,),
            in_specs=[pl.BlockSpec((None, tr, cols), lambda i, idx_ref: (idx_ref[0], i, 0)),
                      pl.BlockSpec((tr, cols), lambda i, idx_ref: (i, 0))],
            out_specs=out_spec),
        out_shape=out_shape,
        compiler_params=_params("arbitrary"),
    )(idx, g, recv)


def _sum_chips_call(own, recv, idx, name):
    _, rows, cols = recv.shape
    tr = _row_tile(rows, cols, 5)
    out_spec = pl.BlockSpec((None, tr, cols), lambda i, idx_ref: (idx_ref[0], i, 0))
    if own is None:
        def body(idx_ref, r_ref, o_ref):
            o_ref[...] = ((r_ref[0] + r_ref[1]) + r_ref[2]) + r_ref[3]
        in_specs = [pl.BlockSpec((N_CHIPS, tr, cols), lambda i, idx_ref: (0, i, 0))]
        args = (recv,)
    else:
        def body(idx_ref, p_ref, r_ref, o_ref):
            o_ref[...] = ((p_ref[...] + r_ref[0]) + r_ref[1]) + r_ref[2]
        in_specs = [pl.BlockSpec((None, tr, cols), lambda i, idx_ref: (idx_ref[1], i, 0)),
                    pl.BlockSpec((N_CHIPS - 1, tr, cols), lambda i, idx_ref: (0, i, 0))]
        args = (own, recv)
    return pl.pallas_call(
        body, name=name,
        grid_spec=pltpu.PrefetchScalarGridSpec(num_scalar_prefetch=1, grid=(rows // tr,), in_specs=in_specs, out_specs=out_spec),
        out_shape=jax.ShapeDtypeStruct((2, rows, cols), F32),
        compiler_params=_params("arbitrary"),
    )(idx, *args)


def _adamw_call(g, w, m, v, name):
    rows, cols = w.shape
    tr = _row_tile(rows, cols, 7)

    def body(g_ref, w_ref, m_ref, v_ref, d_ref, mo_ref, vo_ref):
        gv = g_ref[...]
        mn = ADAM_B1 * m_ref[...] + (1.0 - ADAM_B1) * gv
        vn = ADAM_B2 * v_ref[...] + (1.0 - ADAM_B2) * (gv * gv)
        m_hat = mn / (1.0 - ADAM_B1 ** ADAM_STEP)
        v_hat = vn / (1.0 - ADAM_B2 ** ADAM_STEP)
        d_ref[...] = -ADAM_LR * (m_hat / (jnp.sqrt(v_hat) + ADAM_EPS) + ADAM_WD * w_ref[...])
        mo_ref[...] = mn
        vo_ref[...] = vn

    spec = pl.BlockSpec((tr, cols), lambda i: (i, 0))
    shape = jax.ShapeDtypeStruct((rows, cols), F32)
    return pl.pallas_call(
        body, name=name, grid=(rows // tr,),
        in_specs=[spec] * 4, out_specs=[spec] * 3, out_shape=[shape] * 3,
        compiler_params=_params("arbitrary"),
    )(g, w, m, v)


def _mesh_place():
    x, y, c = lax.axis_index("x"), lax.axis_index("y"), lax.axis_index("c")
    chips = [(1 - x, y), (x, 1 - y), (1 - x, 1 - y)]
    return x, y, c, chips


def _remote(src, dst, send_sems, recv_sems, idx, device):
    return pltpu.make_async_remote_copy(src_ref=src, dst_ref=dst, send_sem=send_sems.at[idx], recv_sem=recv_sems.at[idx],
                                        device_id=device, device_id_type=MESH)


def _gather_shards_call(bufs):
    n = len(bufs)

    def body(*refs):
        ins, outs = refs[:n], refs[n:2 * n]
        send_sems, recv_sems = refs[2 * n:]
        x, y, c, chips = _mesh_place()
        k = 2 * x + y
        sibling = (x, y, 1 - c)
        sends = []
        for a in range(n):
            for j, (px, py) in enumerate(chips):
                cp = _remote(ins[a].at[k, c], outs[a].at[k, c], send_sems, recv_sems, 6 * a + j, (px, py, c))
                cp.start()
                sends.append(cp)
        for a in range(n):
            for j, (px, py) in enumerate(chips):
                kj = 2 * px + py
                _remote(ins[a].at[k, c], outs[a].at[kj, c], send_sems, recv_sems, 6 * a + j, (px, py, c)).wait_recv()
                fwd = _remote(outs[a].at[kj, c], outs[a].at[kj, c], send_sems, recv_sems, 6 * a + 3 + j, sibling)
                fwd.start()
                sends.append(fwd)
        for a in range(n):
            for j, (px, py) in enumerate(chips):
                kj = 2 * px + py
                _remote(ins[a].at[k, c], outs[a].at[kj, 1 - c], send_sems, recv_sems, 6 * a + 3 + j, sibling).wait_recv()
        for cp in sends:
            cp.wait_send()

    return pl.pallas_call(
        body, name="gather_weights",
        in_specs=[ANY] * n, out_specs=[ANY] * n,
        out_shape=[jax.ShapeDtypeStruct(a.shape, a.dtype) for a in bufs],
        input_output_aliases={a: a for a in range(n)},
        scratch_shapes=[pltpu.SemaphoreType.DMA((6 * n,)), pltpu.SemaphoreType.DMA((6 * n,))],
    )(*bufs)


def _sibling_halves_call(grads):
    n = len(grads)

    def body(*refs):
        ins, outs = refs[:n], refs[n:2 * n]
        send_sems, recv_sems = refs[2 * n:]
        x, y, c, _ = _mesh_place()
        sibling = (x, y, 1 - c)
        copies = [_remote(ins[a].at[1 - c], outs[a], send_sems, recv_sems, a, sibling) for a in range(n)]
        for cp in copies:
            cp.start()
        for cp in copies:
            cp.wait_recv()
        for cp in copies:
            cp.wait_send()

    return pl.pallas_call(
        body, name="reduce_siblings",
        in_specs=[ANY] * n, out_specs=[ANY] * n,
        out_shape=[jax.ShapeDtypeStruct(a.shape[1:], a.dtype) for a in grads],
        scratch_shapes=[pltpu.SemaphoreType.DMA((n,)), pltpu.SemaphoreType.DMA((n,))],
    )(*grads)


def _chip_exchange_call(sharded, replicated):
    ns, nr = len(sharded), len(replicated)

    def body(*refs):
        ins_s, ins_r = refs[:ns], refs[ns:ns + nr]
        outs_s, outs_r = refs[ns + nr:2 * ns + nr], refs[2 * ns + nr:2 * (ns + nr)]
        send_sems, recv_sems = refs[2 * (ns + nr):]
        x, y, c, chips = _mesh_place()
        k = 2 * x + y
        sends = []
        for j, (px, py) in enumerate(chips):
            kj = 2 * px + py
            for a in range(ns):
                sends.append(_remote(ins_s[a].at[kj], outs_s[a].at[j], send_sems, recv_sems, 3 * a + j, (px, py, c)))
            for a in range(nr):
                sends.append(_remote(ins_r[a].at[k], outs_r[a].at[k], send_sems, recv_sems, 3 * (ns + a) + j, (px, py, c)))
        for cp in sends:
            cp.start()
        for j, (px, py) in enumerate(chips):
            kj = 2 * px + py
            for a in range(ns):
                _remote(ins_s[a].at[kj], outs_s[a].at[j], send_sems, recv_sems, 3 * a + j, (px, py, c)).wait_recv()
            for a in range(nr):
                _remote(ins_r[a].at[k], outs_r[a].at[kj], send_sems, recv_sems, 3 * (ns + a) + j, (px, py, c)).wait_recv()
        for cp in sends:
            cp.wait_send()

    n = ns + nr
    return pl.pallas_call(
        body, name="reduce_chips",
        in_specs=[ANY] * n, out_specs=[ANY] * n,
        out_shape=[jax.ShapeDtypeStruct((N_CHIPS - 1,) + a.shape[1:], a.dtype) for a in sharded]
        + [jax.ShapeDtypeStruct(a.shape, a.dtype) for a in replicated],
        input_output_aliases={ns + a: ns + a for a in range(nr)},
        scratch_shapes=[pltpu.SemaphoreType.DMA((3 * n,)), pltpu.SemaphoreType.DMA((3 * n,))],
    )(*sharded, *replicated)


def _join_halves_call(bufs):
    n = len(bufs)

    def body(*refs):
        ins, outs = refs[:n], refs[n:2 * n]
        send_sems, recv_sems = refs[2 * n:]
        x, y, c, _ = _mesh_place()
        sibling = (x, y, 1 - c)
        copies = [_remote(ins[a].at[c], outs[a].at[c], send_sems, recv_sems, a, sibling) for a in range(n)]
        for cp in copies:
            cp.start()
        for a in range(n):
            _remote(ins[a].at[c], outs[a].at[1 - c], send_sems, recv_sems, a, sibling).wait_recv()
        for cp in copies:
            cp.wait_send()

    return pl.pallas_call(
        body, name="reduce_join",
        in_specs=[ANY] * n, out_specs=[ANY] * n,
        out_shape=[jax.ShapeDtypeStruct(a.shape, a.dtype) for a in bufs],
        input_output_aliases={a: a for a in range(n)},
        scratch_shapes=[pltpu.SemaphoreType.DMA((n,)), pltpu.SemaphoreType.DMA((n,))],
    )(*bufs)


def _pack(arrays, rows_multiple):
    flat = jnp.concatenate([a.reshape(-1) for a in arrays])
    per = LANES * rows_multiple
    padded = -(-flat.shape[0] // per) * per
    flat = jnp.pad(flat, (0, padded - flat.shape[0]))
    return flat.reshape(-1, LANES)


def _unpack(packed, shapes):
    flat = packed.reshape(-1)
    out, at = [], 0
    for shp in shapes:
        size = 1
        for dim in shp:
            size *= dim
        out.append(flat[at:at + size].reshape(shp))
        at += size
    return out


def _halves(a):
    return a.reshape((2, a.shape[0] // 2) + a.shape[1:])


def kernel(x, ln_mix_g, ln_mix_b, w_in, w_pool, pool_scale, conv_w, conv_b, w_rg_a, b_rg_a, w_rg_i, b_rg_i, rg_lambda, w_out, ln_ffn_g, ln_ffn_b, w_mlp_in, w_mlp_out, loss_target, m_ln_mix_g, m_ln_mix_b, m_w_in, m_w_pool, m_pool_scale, m_conv_w, m_conv_b, m_w_rg_a, m_b_rg_a, m_w_rg_i, m_b_rg_i, m_rg_lambda, m_w_out, m_ln_ffn_g, m_ln_ffn_b, m_w_mlp_in, m_w_mlp_out, v_ln_mix_g, v_ln_mix_b, v_w_in, v_w_pool, v_pool_scale, v_conv_w, v_conv_b, v_w_rg_a, v_b_rg_a, v_w_rg_i, v_b_rg_i, v_rg_lambda, v_w_out, v_ln_ffn_g, v_ln_ffn_b, v_w_mlp_in, v_w_mlp_out):
    weights = dict(ln_mix_g=ln_mix_g, ln_mix_b=ln_mix_b, w_in=w_in, w_pool=w_pool, pool_scale=pool_scale, conv_w=conv_w,
                   conv_b=conv_b, w_rg_a=w_rg_a, b_rg_a=b_rg_a, w_rg_i=w_rg_i, b_rg_i=b_rg_i, rg_lambda=rg_lambda,
                   w_out=w_out, ln_ffn_g=ln_ffn_g, ln_ffn_b=ln_ffn_b, w_mlp_in=w_mlp_in, w_mlp_out=w_mlp_out)
    m_in = dict(ln_mix_g=m_ln_mix_g, ln_mix_b=m_ln_mix_b, w_in=m_w_in, w_pool=m_w_pool, pool_scale=m_pool_scale,
                conv_w=m_conv_w, conv_b=m_conv_b, w_rg_a=m_w_rg_a, b_rg_a=m_b_rg_a, w_rg_i=m_w_rg_i, b_rg_i=m_b_rg_i,
                rg_lambda=m_rg_lambda, w_out=m_w_out, ln_ffn_g=m_ln_ffn_g, ln_ffn_b=m_ln_ffn_b, w_mlp_in=m_w_mlp_in,
                w_mlp_out=m_w_mlp_out)
    v_in = dict(ln_mix_g=v_ln_mix_g, ln_mix_b=v_ln_mix_b, w_in=v_w_in, w_pool=v_w_pool, pool_scale=v_pool_scale,
                conv_w=v_conv_w, conv_b=v_conv_b, w_rg_a=v_w_rg_a, b_rg_a=v_b_rg_a, w_rg_i=v_w_rg_i, b_rg_i=v_b_rg_i,
                rg_lambda=v_rg_lambda, w_out=v_w_out, ln_ffn_g=v_ln_ffn_g, ln_ffn_b=v_ln_ffn_b, w_mlp_in=v_w_mlp_in,
                w_mlp_out=v_w_mlp_out)
    names = list(weights)
    big = ("w_in", "w_out", "w_mlp_in", "w_mlp_out")

    xs = x[0]
    tgt = loss_target[0]
    s, d = xs.shape
    p = c = d // 2
    pg = p // N_POOL_GROUPS
    heads = c // LRU_HEAD
    core = lax.axis_index("c")
    shard = 2 * lax.axis_index("x") + lax.axis_index("y")

    idx = jnp.stack([core, shard]).astype(jnp.int32)
    small_shard = _pack([conv_w[0], b_rg_a[0], b_rg_i[0], rg_lambda[0]], 2 * SUBLANES)
    to_gather = [(w_in[0], BF16), (w_out[0], BF16), (w_mlp_in[0], BF16), (w_mlp_out[0], BF16),
                 (w_pool[0].reshape(-1, pg), BF16), (small_shard, F32)]
    slots = [_cast_to_slot_call(a, idx, dt, f"gather_slot_{i}") for i, (a, dt) in enumerate(to_gather)]
    gathered = _gather_shards_call([sl.reshape(N_CHIPS, 2, sl.shape[1] // 2, sl.shape[2]) for sl in slots])
    w_in_f = gathered[0].reshape((N_CHIPS,) + w_in.shape[1:])
    w_out_f = gathered[1].reshape(d, d)
    w1_f = gathered[2].reshape((N_CHIPS,) + w_mlp_in.shape[1:])
    w2_f = gathered[3].reshape(N_CHIPS * w_mlp_out.shape[1], d)
    w_pool_f = gathered[4].reshape(N_CHIPS, N_POOL_GROUPS, pg // N_CHIPS, pg).transpose(1, 0, 2, 3).reshape(N_POOL_GROUPS, pg, pg)
    c4 = c // N_CHIPS
    small_parts = [_unpack(gathered[5][k].reshape(-1, LANES), [(4, c4), (2, c4), (2, c4), (2, c4)]) for k in range(N_CHIPS)]
    conv_w_f = jnp.concatenate([sp_[0] for sp_ in small_parts], axis=1)
    b_a_f = jnp.concatenate([sp_[1] for sp_ in small_parts], axis=1)
    b_i_f = jnp.concatenate([sp_[2] for sp_ in small_parts], axis=1)
    lam_f = jnp.concatenate([sp_[3] for sp_ in small_parts], axis=1)
    wa_b = w_rg_a[0].astype(BF16)
    wi_b = w_rg_i[0].astype(BF16)

    proj, xb = _proj_call(xs, w_in_f)
    xc = _conv_call(proj, conv_w_f, conv_b, c)
    h_b = _scan_fwd_call(xc, wa_b[1], wi_b[1], b_a_f[1:2], b_i_f[1:2], lam_f[1:2], True)
    h_f = _scan_fwd_call(xc, wa_b[0], wi_b[0], b_a_f[0:1], b_i_f[0:1], lam_f[0:1], False)
    y, d_pool = _pool_combine_call(proj, h_f, h_b, w_pool_f, pool_scale, p)
    xh1, x1b, rstd1 = _out_ln1_call(y, w_out_f, xs, ln_mix_g, ln_mix_b)
    r_act, hsq = _mlp_in_call(x1b, w1_f)
    dz2, dz2b, loss8, dg2, db2 = _mlp_out_ln2_call(hsq, w2_f, xh1, ln_mix_g, ln_mix_b, ln_ffn_g, ln_ffn_b, tgt)

    g_w2 = _row_sharded_grad(hsq, dz2b, "grad_w_mlp_out")
    dpre = _dhsq_call(dz2b, w2_f, r_act)
    g_w1 = _col_sharded_grad(x1b, dpre, "grad_w_mlp_in")
    dz1, dz1b, dg1, db1 = _dx1_ln1_bwd_call(dpre, w1_f, dz2, xh1, rstd1, ln_mix_g)
    g_wout = _row_sharded_grad(y, dz1b, "grad_w_out")
    dy = _dy_call(dz1b, w_out_f)
    e_pool, dh, dgate, g_wpool, g_pscale8 = _mixer_bwd_call(dy, d_pool, proj, h_f, h_b, w_pool_f, pool_scale, p)
    dxc0, g_wa0, g_wi0, g_ba0, g_bi0, g_sp0 = _scan_bwd_call(
        xc, dh, h_f, None, wa_b[0], wi_b[0], b_a_f[0:1], b_i_f[0:1], lam_f[0:1], False)
    dxc, g_wa1, g_wi1, g_ba1, g_bi1, g_sp1 = _scan_bwd_call(
        xc, dh, h_b, dxc0, wa_b[1], wi_b[1], b_a_f[1:2], b_i_f[1:2], lam_f[1:2], True)
    dproj, g_cw8, g_cb8 = _dproj_call(e_pool, dxc, proj, dgate, conv_w_f, p)
    g_win = _col_sharded_grad(xb, dproj, "grad_w_in")
    grad_x = _dx_call(dproj, w_in_f, dz1)

    rowsum = lambda a8: jnp.sum(a8, axis=-2)
    g_lam = jnp.stack([rowsum(g_sp0), rowsum(g_sp1)]) * (-_sigmoid(-lam_f))
    small_grads = {
        "ln_mix_g": rowsum(dg1), "ln_mix_b": rowsum(db1), "ln_ffn_g": rowsum(dg2), "ln_ffn_b": rowsum(db2),
        "pool_scale": rowsum(g_pscale8), "conv_b": rowsum(g_cb8),
        "w_rg_a": jnp.stack([g_wa0, g_wa1]), "w_rg_i": jnp.stack([g_wi0, g_wi1]),
        "w_pool": g_wpool, "conv_w": rowsum(g_cw8),
        "b_rg_a": jnp.stack([rowsum(g_ba0), rowsum(g_ba1)]), "b_rg_i": jnp.stack([rowsum(g_bi0), rowsum(g_bi1)]),
        "rg_lambda": g_lam,
    }
    small_names = list(small_grads)
    small_shapes = [small_grads[nm].shape for nm in small_names]
    g_small = _halves(_pack([small_grads[nm] for nm in small_names], 2 * SUBLANES))

    big_grads = [g_win, g_wout, g_w1, g_w2]
    grads = big_grads + [g_small]
    from_sibling = _sibling_halves_call(grads)
    chip_sums = []
    for a, (g, rv) in enumerate(zip(grads, from_sibling)):
        cols = g.shape[-1]
        part = _add_half_call(g.reshape(2, -1, cols), rv.reshape(-1, cols), idx, a == len(big_grads), f"reduce_add_{a}")
        chip_sums.append(part.reshape((N_CHIPS,) + rv.shape[1:]) if a < len(big_grads) else part)
    received = _chip_exchange_call(chip_sums[:len(big_grads)], chip_sums[len(big_grads):])
    reduced = []
    for a, rc in enumerate(received):
        cols = rc.shape[-1]
        if a < len(big_grads):
            total = _sum_chips_call(chip_sums[a].reshape(N_CHIPS, -1, cols), rc.reshape(N_CHIPS - 1, -1, cols), idx, f"reduce_sum_{a}")
            reduced.append(total.reshape((2,) + rc.shape[1:]))
        else:
            reduced.append(_sum_chips_call(None, rc, idx, f"reduce_sum_{a}"))
    joined = _join_halves_call(reduced)

    grad_w, delta_w, new_m, new_v = {}, {}, {}, {}
    for nm, full in zip(big, joined[:len(big)]):
        w2d = weights[nm][0]
        g2d = full.reshape(w2d.shape)
        dl, mn, vn = _adamw_call(g2d, w2d, m_in[nm][0], v_in[nm][0], f"adamw_{nm}")
        grad_w[nm], delta_w[nm], new_m[nm], new_v[nm] = g2d[None], dl[None], mn[None], vn[None]

    small_full = dict(zip(small_names, _unpack(joined[len(big)].reshape(-1, LANES), small_shapes)))
    local = dict(small_full)
    local["w_pool"] = lax.dynamic_slice_in_dim(small_full["w_pool"], shard * (pg // N_CHIPS), pg // N_CHIPS, axis=1)
    for nm in ("conv_w", "b_rg_a", "b_rg_i", "rg_lambda"):
        local[nm] = lax.dynamic_slice_in_dim(small_full[nm], shard * c4, c4, axis=1)
    small_w_shapes = [weights[nm].shape for nm in small_names]
    g_pack = _pack([local[nm] for nm in small_names], SUBLANES)
    w_pack = _pack([weights[nm] for nm in small_names], SUBLANES)
    m_pack = _pack([m_in[nm] for nm in small_names], SUBLANES)
    v_pack = _pack([v_in[nm] for nm in small_names], SUBLANES)
    dl_p, mn_p, vn_p = _adamw_call(g_pack, w_pack, m_pack, v_pack, "adamw_small")
    for nm, gl, dl, mn, vn in zip(small_names, _unpack(g_pack, small_w_shapes), _unpack(dl_p, small_w_shapes),
                                  _unpack(mn_p, small_w_shapes), _unpack(vn_p, small_w_shapes)):
        grad_w[nm], delta_w[nm], new_m[nm], new_v[nm] = gl, dl, mn, vn

    loss = lax.psum(jnp.sum(loss8) * (0.5 / d), ("x", "y", "c"))
    return (loss, grad_x[None], *[grad_w[nm] for nm in names], *[delta_w[nm] for nm in names],
            *[new_m[nm] for nm in names], *[new_v[nm] for nm in names])
```

```python
import functools

import jax
import jax.numpy as jnp
from jax import lax
from jax.experimental import pallas as pl
from jax.experimental.pallas import tpu as pltpu

F32 = jnp.float32
BF16 = jnp.bfloat16

N_CHIPS = 4
LANES = 128
SUBLANES = 8
LRU_HEAD = 128
N_POOL_GROUPS = 4
POOL_WINDOWS = (2, 4, 8, 16)
RG_C = 8.0
LN_EPS = 1e-5
ALPHA = 2.0 ** 0.25
ADAM_LR, ADAM_B1, ADAM_B2, ADAM_EPS, ADAM_WD, ADAM_STEP = 0.001, 0.9, 0.999, 1e-08, 0.01, 10
VMEM_LIMIT = 56 * 1024 * 1024
SEQ_TILE = 256
MM_TILE = 512
LN_UNROLL = 4
ELT_BLOCK_BYTES = 2 * 1024 * 1024
MESH = pl.DeviceIdType.MESH
ANY = pl.BlockSpec(memory_space=pl.ANY)


def _params(*sem):
    return pltpu.CompilerParams(dimension_semantics=sem, vmem_limit_bytes=VMEM_LIMIT)


def _sigmoid(z):
    return 1.0 / (1.0 + jnp.exp(-z))


def _neg_expm1(z):
    series = -(z * (1.0 + z * (0.5 + z * (1.0 / 6.0 + z * (1.0 / 24.0)))))
    return jnp.where(z > -0.01, series, 1.0 - jnp.exp(z))


def _softplus(z):
    return jnp.maximum(z, 0.0) + jnp.log1p(jnp.exp(-jnp.abs(z)))


_GELU_K = 0.7978845608028654
_GELU_C = 0.044715


def _gelu_and_grad(u):
    t = jnp.tanh(_GELU_K * (u + _GELU_C * (u * u * u)))
    g = 0.5 * u * (1.0 + t)
    dg = 0.5 * (1.0 + t) + 0.5 * u * (1.0 - t * t) * (_GELU_K * (1.0 + 3.0 * _GELU_C * u * u))
    return g, dg


def _shift_rows(prv, cur, nxt, o, rows):
    if o == 0:
        return cur
    if o == SUBLANES:
        return nxt
    if o == -SUBLANES:
        return prv
    if o > 0:
        s = SUBLANES - o
        return jnp.where(rows < s, pltpu.roll(cur, s, 0), pltpu.roll(nxt, s, 0))
    p = -o
    return jnp.where(rows >= p, pltpu.roll(cur, p, 0), pltpu.roll(prv, p, 0))


def _neighbour_chunks(main_ref, prev_ref, next_ref, r0, t_rows, cols, first_tile, last_tile):
    cur = main_ref[pl.ds(r0, SUBLANES), cols]
    before = main_ref[pl.ds(pl.multiple_of(jnp.maximum(r0 - SUBLANES, 0), SUBLANES), SUBLANES), cols]
    after = main_ref[pl.ds(pl.multiple_of(jnp.minimum(r0 + SUBLANES, t_rows - SUBLANES), SUBLANES), SUBLANES), cols]
    halo_prev = jnp.where(first_tile, 0.0, prev_ref[:, cols])
    halo_next = jnp.where(last_tile, 0.0, next_ref[:, cols])
    prv = jnp.where(r0 == 0, halo_prev, before)
    nxt = jnp.where(r0 == t_rows - SUBLANES, halo_next, after)
    return prv, cur, nxt


def _halo_specs(t_rows, n_rows, width, col_block):
    per = t_rows // SUBLANES
    last = n_rows // SUBLANES - 1
    return [
        pl.BlockSpec((t_rows, width), lambda i: (i, col_block)),
        pl.BlockSpec((SUBLANES, width), lambda i: (jnp.maximum(i * per - 1, 0), col_block)),
        pl.BlockSpec((SUBLANES, width), lambda i: (jnp.minimum((i + 1) * per, last), col_block)),
    ]


def _chunk_loop(t_rows, fn, init=None, unroll=1):
    span = SUBLANES * unroll

    def step(ci, carry):
        base = pl.multiple_of(ci * span, span)
        for u in range(unroll):
            carry = fn(base + u * SUBLANES, carry)
        return carry
    return lax.fori_loop(0, t_rows // span, step, init)


def _proj_call(x, w_in):
    s, d = x.shape
    n, _, e4 = w_in.shape
    tm = min(MM_TILE, s)

    def body(x_ref, w_ref, proj_ref, xb_ref):
        @pl.when(pl.program_id(1) == 0)
        def _():
            xb_ref[...] = x_ref[...].astype(BF16)

        proj_ref[...] = jnp.dot(xb_ref[...], w_ref[...], preferred_element_type=F32)

    return pl.pallas_call(
        body, name="proj", grid=(s // tm, n),
        in_specs=[pl.BlockSpec((tm, d), lambda i, j: (i, 0)), pl.BlockSpec((None, d, e4), lambda i, j: (j, 0, 0))],
        out_specs=[pl.BlockSpec((tm, e4), lambda i, j: (i, j)), pl.BlockSpec((tm, d), lambda i, j: (i, 0))],
        out_shape=[jax.ShapeDtypeStruct((s, n * e4), F32), jax.ShapeDtypeStruct((s, d), BF16)],
        compiler_params=_params("arbitrary", "arbitrary"),
    )(x, w_in)


def _conv_call(proj, conv_w, conv_b, c):
    s = proj.shape[0]
    t = min(SEQ_TILE, s)
    n_tiles = s // t

    def body(u_ref, up_ref, un_ref, w_ref, b_ref, xc_ref):
        i = pl.program_id(0)
        rows = lax.broadcasted_iota(jnp.int32, (SUBLANES, c), 0)
        w = w_ref[...]
        b = b_ref[...]

        def chunk(r0, _):
            prv, cur, nxt = _neighbour_chunks(u_ref, up_ref, un_ref, r0, t, slice(None), i == 0, i == n_tiles - 1)
            acc = b + w[1:2] * cur
            acc += w[0:1] * _shift_rows(prv, cur, nxt, -1, rows)
            acc += w[2:3] * _shift_rows(prv, cur, nxt, 1, rows)
            acc += w[3:4] * _shift_rows(prv, cur, nxt, 2, rows)
            xc_ref[pl.ds(r0, SUBLANES), :] = acc

        _chunk_loop(t, chunk)

    return pl.pallas_call(
        body, name="conv_fwd", grid=(n_tiles,),
        in_specs=_halo_specs(t, s, c, 1) + [pl.BlockSpec((4, c), lambda i: (0, 0)), pl.BlockSpec((1, c), lambda i: (0, 0))],
        out_specs=pl.BlockSpec((t, c), lambda i: (i, 0)),
        out_shape=jax.ShapeDtypeStruct((s, c), F32),
        compiler_params=_params("arbitrary"),
    )(proj, proj, proj, conv_w, conv_b)


def _gate_matmuls(xc_ref, wa_ref, wi_ref, pr_s, pi_s, heads):
    for h in range(heads):
        cs = pl.ds(h * LRU_HEAD, LRU_HEAD)
        xb = xc_ref[:, cs].astype(BF16)
        pr_s[:, cs] = jnp.dot(xb, wa_ref[h], preferred_element_type=F32)
        pi_s[:, cs] = jnp.dot(xb, wi_ref[h], preferred_element_type=F32)


def _rg_gates(pr, pi, ba, bi, sp):
    r = _sigmoid(pr + ba)
    ig = _sigmoid(pi + bi)
    log_a = (-RG_C * r) * sp
    a = jnp.exp(log_a)
    mult = jnp.sqrt(_neg_expm1(2.0 * log_a))
    return r, ig, a, mult


def _scan_fwd_call(xc, wa, wi, ba, bi, lam, reverse):
    s, c = xc.shape
    heads = c // LRU_HEAD
    t = min(SEQ_TILE, s)
    n_tiles = s // t
    tile = (lambda i: (n_tiles - 1 - i, 0)) if reverse else (lambda i: (i, 0))
    whole2 = lambda i: (0, 0)
    whole3 = lambda i: (0, 0, 0)

    def body(xc_ref, wa_ref, wi_ref, ba_ref, bi_ref, lam_ref, h_ref, pr_s, pi_s, carry_s):
        @pl.when(pl.program_id(0) == 0)
        def _():
            carry_s[...] = jnp.zeros_like(carry_s)

        _gate_matmuls(xc_ref, wa_ref, wi_ref, pr_s, pi_s, heads)
        ba_v, bi_v = ba_ref[...], bi_ref[...]
        sp = _softplus(-lam_ref[...])

        def chunk(r0, _):
            rs = pl.ds(r0, SUBLANES)
            xcv = xc_ref[rs, :]
            _, ig, a, mult = _rg_gates(pr_s[rs, :], pi_s[rs, :], ba_v, bi_v, sp)
            pr_s[rs, :] = a
            pi_s[rs, :] = mult * ig * xcv

        _chunk_loop(t, chunk)

        def row(j, h):
            r = (t - 1 - j) if reverse else j
            h = pr_s[pl.ds(r, 1), :] * h + pi_s[pl.ds(r, 1), :]
            h_ref[pl.ds(r, 1), :] = h
            return h

        carry_s[...] = lax.fori_loop(0, t, row, carry_s[...], unroll=8)

    return pl.pallas_call(
        body, name="scan_fwd_rev" if reverse else "scan_fwd", grid=(n_tiles,),
        in_specs=[pl.BlockSpec((t, c), tile),
                  pl.BlockSpec((heads, LRU_HEAD, LRU_HEAD), whole3), pl.BlockSpec((heads, LRU_HEAD, LRU_HEAD), whole3),
                  pl.BlockSpec((1, c), whole2), pl.BlockSpec((1, c), whole2), pl.BlockSpec((1, c), whole2)],
        out_specs=pl.BlockSpec((t, c), tile),
        out_shape=jax.ShapeDtypeStruct((s, c), F32),
        scratch_shapes=[pltpu.VMEM((t, c), F32), pltpu.VMEM((t, c), F32), pltpu.VMEM((1, c), F32)],
        compiler_params=_params("arbitrary"),
    )(xc, wa, wi, ba, bi, lam)


def _window_counts(r0, tile_idx, t_rows, n_rows, half, shape):
    pos = tile_idx * t_rows + r0 + lax.broadcasted_iota(jnp.int32, shape, 0)
    hi = jnp.minimum(pos + half, n_rows)
    lo = jnp.maximum(pos - half, 0)
    return (hi - lo).astype(F32)


def _pool_combine_call(proj, h_f, h_b, w_pool, pool_scale, p):
    s = proj.shape[0]
    c = h_f.shape[1]
    pg = p // N_POOL_GROUPS
    t = min(SEQ_TILE, s)
    n_tiles = s // t

    def body(u_ref, up_ref, un_ref, gate_ref, hf_ref, hb_ref, wp_ref, sc_ref, y_ref, d_ref, d_s, yr_s):
        i = pl.program_id(0)
        rows = lax.broadcasted_iota(jnp.int32, (SUBLANES, pg), 0)

        def chunk(r0, _):
            rs = pl.ds(r0, SUBLANES)
            for g, w in enumerate(POOL_WINDOWS):
                cols = pl.ds(g * pg, pg)
                prv, cur, nxt = _neighbour_chunks(u_ref, up_ref, un_ref, r0, t, cols, i == 0, i == n_tiles - 1)
                tot = cur
                for o in range(-(w // 2), w // 2):
                    if o != 0:
                        tot = tot + _shift_rows(prv, cur, nxt, o, rows)
                cnt = _window_counts(r0, i, t, s, w // 2, (SUBLANES, pg))
                d_s[rs, cols] = tot / cnt - cur
            gate, _ = _gelu_and_grad(gate_ref[rs, :])
            yr_s[rs, :] = (hf_ref[rs, :] + hb_ref[rs, :]) * gate

        _chunk_loop(t, chunk)
        y_ref[:, pl.ds(p, c)] = yr_s[...].astype(BF16)
        d_ref[...] = d_s[...].astype(BF16)
        for g in range(N_POOL_GROUPS):
            cols = pl.ds(g * pg, pg)
            out = jnp.dot(d_s[:, cols].astype(BF16), wp_ref[g], preferred_element_type=F32)
            y_ref[:, cols] = (out * sc_ref[:, cols]).astype(BF16)

    return pl.pallas_call(
        body, name="pool_combine", grid=(n_tiles,),
        in_specs=_halo_specs(t, s, p, 0) + [
            pl.BlockSpec((t, c), lambda i: (i, 2)),
            pl.BlockSpec((t, c), lambda i: (i, 0)), pl.BlockSpec((t, c), lambda i: (i, 0)),
            pl.BlockSpec((N_POOL_GROUPS, pg, pg), lambda i: (0, 0, 0)), pl.BlockSpec((1, p), lambda i: (0, 0))],
        out_specs=[pl.BlockSpec((t, p + c), lambda i: (i, 0)), pl.BlockSpec((t, p), lambda i: (i, 0))],
        out_shape=[jax.ShapeDtypeStruct((s, p + c), BF16), jax.ShapeDtypeStruct((s, p), BF16)],
        scratch_shapes=[pltpu.VMEM((t, p), F32), pltpu.VMEM((t, c), F32)],
        compiler_params=_params("arbitrary"),
    )(proj, proj, proj, proj, h_f, h_b, w_pool, pool_scale)


def _layer_norm_rows(z, g, b):
    mu = jnp.mean(z, axis=-1, keepdims=True)
    zc = z - mu
    var = jnp.mean(zc * zc, axis=-1, keepdims=True)
    rstd = lax.rsqrt(var + LN_EPS)
    xh = zc * rstd
    return xh, rstd, xh * g + b


def _layer_norm_bwd_rows(dx, xh, rstd, g):
    dxh = dx * g
    m1 = jnp.mean(dxh, axis=-1, keepdims=True)
    m2 = jnp.mean(dxh * xh, axis=-1, keepdims=True)
    return rstd * (dxh - m1 - xh * m2)


def _out_ln1_call(y, w_out, x, g1, b1):
    s, d = x.shape
    tm = min(SEQ_TILE, s)

    def body(y_ref, w_ref, x_ref, g_ref, b_ref, xh_ref, x1b_ref, rstd_ref, acc_s, x1_s):
        acc_s[...] = jnp.dot(y_ref[...], w_ref[...], preferred_element_type=F32)
        g, b = g_ref[...], b_ref[...]

        def chunk(r0, _):
            rs = pl.ds(r0, SUBLANES)
            xh, rstd, x1 = _layer_norm_rows(ALPHA * x_ref[rs, :] + acc_s[rs, :], g, b)
            xh_ref[rs, :] = xh
            x1_s[rs, :] = x1
            rstd_ref[rs, :] = rstd

        _chunk_loop(tm, chunk, unroll=LN_UNROLL)
        x1b_ref[...] = x1_s[...].astype(BF16)

    return pl.pallas_call(
        body, name="out_ln1", grid=(s // tm,),
        in_specs=[pl.BlockSpec((tm, d), lambda i: (i, 0)), pl.BlockSpec((d, d), lambda i: (0, 0)),
                  pl.BlockSpec((tm, d), lambda i: (i, 0)),
                  pl.BlockSpec((1, d), lambda i: (0, 0)), pl.BlockSpec((1, d), lambda i: (0, 0))],
        out_specs=[pl.BlockSpec((tm, d), lambda i: (i, 0)), pl.BlockSpec((tm, d), lambda i: (i, 0)),
                   pl.BlockSpec((tm, 1), lambda i: (i, 0))],
        out_shape=[jax.ShapeDtypeStruct((s, d), F32), jax.ShapeDtypeStruct((s, d), BF16), jax.ShapeDtypeStruct((s, 1), F32)],
        scratch_shapes=[pltpu.VMEM((tm, d), F32), pltpu.VMEM((tm, d), F32)],
        compiler_params=_params("arbitrary"),
    )(y, w_out, x, g1, b1)


def _mlp_in_call(x1b, w1):
    s, d = x1b.shape
    n, _, f4 = w1.shape
    tm = min(MM_TILE, s)
    tn = min(1024, f4)
    per = f4 // tn

    def body(x_ref, w_ref, r_ref, q_ref):
        r = jnp.maximum(jnp.dot(x_ref[...], w_ref[...], preferred_element_type=F32), 0.0)
        r_ref[...] = r.astype(BF16)
        q_ref[...] = (r * r).astype(BF16)

    return pl.pallas_call(
        body, name="mlp_in", grid=(n * per, s // tm),
        in_specs=[pl.BlockSpec((tm, d), lambda j, i: (i, 0)), pl.BlockSpec((None, d, tn), lambda j, i: (j // per, 0, j % per))],
        out_specs=[pl.BlockSpec((tm, tn), lambda j, i: (i, j)), pl.BlockSpec((tm, tn), lambda j, i: (i, j))],
        out_shape=[jax.ShapeDtypeStruct((s, n * f4), BF16), jax.ShapeDtypeStruct((s, n * f4), BF16)],
        compiler_params=_params("arbitrary", "arbitrary"),
    )(x1b, w1)


def _mlp_out_ln2_call(hsq, w2, xh1, g1, b1, g2, b2, target):
    s, f = hsq.shape
    d = w2.shape[1]
    tm = min(MM_TILE, s)
    tk = min(1024, f)
    nk = f // tk

    def body(h_ref, w_ref, xh1_ref, g1_ref, b1_ref, g2_ref, b2_ref, t_ref,
             dz_ref, dzb_ref, loss_ref, dg_ref, db_ref, acc_s):
        i, k = pl.program_id(0), pl.program_id(1)

        @pl.when((i == 0) & (k == 0))
        def _():
            loss_ref[...] = jnp.zeros_like(loss_ref)
            dg_ref[...] = jnp.zeros_like(dg_ref)
            db_ref[...] = jnp.zeros_like(db_ref)

        @pl.when(k == 0)
        def _():
            acc_s[...] = jnp.zeros_like(acc_s)

        acc_s[...] += jnp.dot(h_ref[...], w_ref[...], preferred_element_type=F32)

        @pl.when(k == nk - 1)
        def _():
            g1, b1, g2, b2 = g1_ref[...], b1_ref[...], g2_ref[...], b2_ref[...]

            def chunk(r0, _):
                rs = pl.ds(r0, SUBLANES)
                x1 = xh1_ref[rs, :] * g1 + b1
                xh2, rstd, x2 = _layer_norm_rows(ALPHA * x1 + acc_s[rs, :], g2, b2)
                diff = x2 - t_ref[rs, :]
                loss_ref[...] += diff * diff
                dx2 = diff * (1.0 / d)
                dg_ref[...] += dx2 * xh2
                db_ref[...] += dx2
                dz = _layer_norm_bwd_rows(dx2, xh2, rstd, g2)
                dz_ref[rs, :] = dz

            _chunk_loop(tm, chunk, unroll=LN_UNROLL)
            dzb_ref[...] = dz_ref[...].astype(BF16)

    row = lambda i, k: (i, 0)
    vec = lambda i, k: (0, 0)
    return pl.pallas_call(
        body, name="mlp_out_ln2", grid=(s // tm, nk),
        in_specs=[pl.BlockSpec((tm, tk), lambda i, k: (i, k)), pl.BlockSpec((tk, d), lambda i, k: (k, 0)),
                  pl.BlockSpec((tm, d), row), pl.BlockSpec((1, d), vec), pl.BlockSpec((1, d), vec),
                  pl.BlockSpec((1, d), vec), pl.BlockSpec((1, d), vec), pl.BlockSpec((tm, d), row)],
        out_specs=[pl.BlockSpec((tm, d), row), pl.BlockSpec((tm, d), row),
                   pl.BlockSpec((SUBLANES, d), vec), pl.BlockSpec((SUBLANES, d), vec), pl.BlockSpec((SUBLANES, d), vec)],
        out_shape=[jax.ShapeDtypeStruct((s, d), F32), jax.ShapeDtypeStruct((s, d), BF16),
                   jax.ShapeDtypeStruct((SUBLANES, d), F32), jax.ShapeDtypeStruct((SUBLANES, d), F32),
                   jax.ShapeDtypeStruct((SUBLANES, d), F32)],
        scratch_shapes=[pltpu.VMEM((tm, d), F32)],
        compiler_params=_params("arbitrary", "arbitrary"),
    )(hsq, w2, xh1, g1, b1, g2, b2, target)


def _weight_grad_call(a, b, tm, tn, out_shape, out_map, name):
    s, m = a.shape
    n = b.shape[1]
    tk = min(1024, s)

    def body(a_ref, b_ref, o_ref):
        @pl.when(pl.program_id(2) == 0)
        def _():
            o_ref[...] = jnp.zeros_like(o_ref)

        o_ref[...] += lax.dot_general(a_ref[...], b_ref[...], (((0,), (0,)), ((), ())), preferred_element_type=F32)

    return pl.pallas_call(
        body, name=name, grid=(m // tm, n // tn, s // tk),
        in_specs=[pl.BlockSpec((tk, tm), lambda i, j, k: (k, i)), pl.BlockSpec((tk, tn), lambda i, j, k: (k, j))],
        out_specs=pl.BlockSpec((None, None, tm, tn), lambda i, j, k: out_map(i, j)),
        out_shape=jax.ShapeDtypeStruct(out_shape, F32),
        compiler_params=_params("arbitrary", "arbitrary", "arbitrary"),
    )(a, b)


def _row_sharded_grad(a, b, name):
    m, n = a.shape[1], b.shape[1]
    half_rows = m // (2 * N_CHIPS)
    tm = min(1024, half_rows)
    per = half_rows // tm
    tn = min(1024, n)
    return _weight_grad_call(a, b, tm, tn, (2, N_CHIPS, half_rows, n),
                             lambda i, j: ((i // per) % 2, i // (2 * per), i % per, j), name)


def _col_sharded_grad(a, b, name):
    m, n = a.shape[1], b.shape[1]
    half_rows, shard_cols = m // 2, n // N_CHIPS
    tm = min(1024, half_rows)
    per_m = half_rows // tm
    tn = shard_cols if shard_cols % 1024 else 1024
    per_n = shard_cols // tn
    return _weight_grad_call(a, b, tm, tn, (2, N_CHIPS, half_rows, shard_cols),
                             lambda i, j: (i // per_m, j // per_n, i % per_m, j % per_n), name)


def _dhsq_call(dzb, w2, r, dep):
    s, d = dzb.shape
    f = w2.shape[0]
    tm = min(MM_TILE, s)
    tn = min(1024, f)

    def body(dz_ref, w_ref, r_ref, dep_ref, o_ref):
        dh = lax.dot_general(dz_ref[...], w_ref[...], (((1,), (1,)), ((), ())), preferred_element_type=F32)
        o_ref[...] = (dh * (2.0 * r_ref[...].astype(F32))).astype(BF16)

    return pl.pallas_call(
        body, name="mlp_dpre", grid=(f // tn, s // tm),
        in_specs=[pl.BlockSpec((tm, d), lambda j, i: (i, 0)), pl.BlockSpec((tn, d), lambda j, i: (j, 0)),
                  pl.BlockSpec((tm, tn), lambda j, i: (i, j)), ANY],
        out_specs=pl.BlockSpec((tm, tn), lambda j, i: (i, j)),
        out_shape=jax.ShapeDtypeStruct((s, f), BF16),
        compiler_params=_params("arbitrary", "arbitrary"),
    )(dzb, w2, r, dep)


def _dx1_ln1_bwd_call(dpre, w1, dz2, xh1, rstd1, g1, dep):
    s, f = dpre.shape
    n, d, f4 = w1.shape
    tm = min(MM_TILE, s)
    tk = min(1024, f4)
    per = f4 // tk
    nk = n * per

    def body(dp_ref, w_ref, dz2_ref, xh_ref, rstd_ref, g_ref, dep_ref, dz_ref, dzb_ref, dg_ref, db_ref, acc_s):
        i, k = pl.program_id(0), pl.program_id(1)

        @pl.when((i == 0) & (k == 0))
        def _():
            dg_ref[...] = jnp.zeros_like(dg_ref)
            db_ref[...] = jnp.zeros_like(db_ref)

        @pl.when(k == 0)
        def _():
            acc_s[...] = jnp.zeros_like(acc_s)

        acc_s[...] += lax.dot_general(dp_ref[...], w_ref[...], (((1,), (1,)), ((), ())), preferred_element_type=F32)

        @pl.when(k == nk - 1)
        def _():
            g = g_ref[...]

            def chunk(r0, _):
                rs = pl.ds(r0, SUBLANES)
                dx1 = acc_s[rs, :] + ALPHA * dz2_ref[rs, :]
                xh = xh_ref[rs, :]
                dg_ref[...] += dx1 * xh
                db_ref[...] += dx1
                dz = _layer_norm_bwd_rows(dx1, xh, rstd_ref[rs, :], g)
                dz_ref[rs, :] = dz

            _chunk_loop(tm, chunk, unroll=LN_UNROLL)
            dzb_ref[...] = dz_ref[...].astype(BF16)

    row = lambda i, k: (i, 0)
    vec = lambda i, k: (0, 0)
    return pl.pallas_call(
        body, name="dx1_ln1_bwd", grid=(s // tm, nk),
        in_specs=[pl.BlockSpec((tm, tk), lambda i, k: (i, k)),
                  pl.BlockSpec((None, d, tk), lambda i, k: (k // per, 0, k % per)),
                  pl.BlockSpec((tm, d), row), pl.BlockSpec((tm, d), row), pl.BlockSpec((tm, 1), row),
                  pl.BlockSpec((1, d), vec), ANY],
        out_specs=[pl.BlockSpec((tm, d), row), pl.BlockSpec((tm, d), row),
                   pl.BlockSpec((SUBLANES, d), vec), pl.BlockSpec((SUBLANES, d), vec)],
        out_shape=[jax.ShapeDtypeStruct((s, d), F32), jax.ShapeDtypeStruct((s, d), BF16),
                   jax.ShapeDtypeStruct((SUBLANES, d), F32), jax.ShapeDtypeStruct((SUBLANES, d), F32)],
        scratch_shapes=[pltpu.VMEM((tm, d), F32)],
        compiler_params=_params("arbitrary", "arbitrary"),
    )(dpre, w1, dz2, xh1, rstd1, g1, dep)


def _dy_call(dzb, w_out):
    s, d = dzb.shape
    e = w_out.shape[0]
    tm = min(MM_TILE, s)

    def body(dz_ref, w_ref, o_ref):
        o_ref[...] = lax.dot_general(dz_ref[...], w_ref[...], (((1,), (1,)), ((), ())), preferred_element_type=F32)

    return pl.pallas_call(
        body, name="dy", grid=(s // tm,),
        in_specs=[pl.BlockSpec((tm, d), lambda i: (i, 0)), pl.BlockSpec((e, d), lambda i: (0, 0))],
        out_specs=pl.BlockSpec((tm, e), lambda i: (i, 0)),
        out_shape=jax.ShapeDtypeStruct((s, e), F32),
        compiler_params=_params("arbitrary"),
    )(dzb, w_out)


def _mixer_bwd_call(dy, d_pool, proj, h_f, h_b, w_pool, pool_scale, p):
    s = dy.shape[0]
    c = h_f.shape[1]
    pg = p // N_POOL_GROUPS
    t = min(SEQ_TILE, s)
    n_tiles = s // t

    def body(dyp_ref, dyr_ref, d_ref, gate_ref, hf_ref, hb_ref, wp_ref, sc_ref,
             e_ref, dh_ref, dgate_ref, dwp_ref, dsc_ref, dd_s):
        i = pl.program_id(0)

        @pl.when(i == 0)
        def _():
            dwp_ref[...] = jnp.zeros_like(dwp_ref)
            dsc_ref[...] = jnp.zeros_like(dsc_ref)

        for g in range(N_POOL_GROUPS):
            cols = pl.ds(g * pg, pg)
            dg = d_ref[:, cols]
            out = jnp.dot(dg, wp_ref[g], preferred_element_type=F32)
            dyp = dyp_ref[:, cols]
            prod = dyp * out
            dsc_ref[:, cols] += jnp.sum(prod.reshape(t // SUBLANES, SUBLANES, pg), axis=0)
            dout = (dyp * sc_ref[:, cols]).astype(BF16)
            dwp_ref[g] += lax.dot_general(dg, dout, (((0,), (0,)), ((), ())), preferred_element_type=F32)
            dd_s[:, cols] = lax.dot_general(dout, wp_ref[g], (((1,), (1,)), ((), ())), preferred_element_type=F32)

        def chunk(r0, _):
            rs = pl.ds(r0, SUBLANES)
            for g, w in enumerate(POOL_WINDOWS):
                cols = pl.ds(g * pg, pg)
                cnt = _window_counts(r0, i, t, s, w // 2, (SUBLANES, pg))
                e_ref[rs, cols] = dd_s[rs, cols] / cnt
            gate, dgate = _gelu_and_grad(gate_ref[rs, :])
            dyr = dyr_ref[rs, :]
            dh_ref[rs, :] = dyr * gate
            dd_s[rs, :] = dyr * (hf_ref[rs, :] + hb_ref[rs, :]) * dgate

        _chunk_loop(t, chunk)
        dgate_ref[...] = dd_s[...].astype(BF16)

    tile = lambda i: (i, 0)
    return pl.pallas_call(
        body, name="mixer_bwd", grid=(n_tiles,),
        in_specs=[pl.BlockSpec((t, p), tile), pl.BlockSpec((t, c), lambda i: (i, 1)), pl.BlockSpec((t, p), tile),
                  pl.BlockSpec((t, c), lambda i: (i, 2)), pl.BlockSpec((t, c), tile), pl.BlockSpec((t, c), tile),
                  pl.BlockSpec((N_POOL_GROUPS, pg, pg), lambda i: (0, 0, 0)), pl.BlockSpec((1, p), lambda i: (0, 0))],
        out_specs=[pl.BlockSpec((t, p), tile), pl.BlockSpec((t, c), tile), pl.BlockSpec((t, c), tile),
                   pl.BlockSpec((N_POOL_GROUPS, pg, pg), lambda i: (0, 0, 0)), pl.BlockSpec((SUBLANES, p), lambda i: (0, 0))],
        out_shape=[jax.ShapeDtypeStruct((s, p), F32), jax.ShapeDtypeStruct((s, c), F32), jax.ShapeDtypeStruct((s, c), BF16),
                   jax.ShapeDtypeStruct((N_POOL_GROUPS, pg, pg), F32), jax.ShapeDtypeStruct((SUBLANES, p), F32)],
        scratch_shapes=[pltpu.VMEM((t, p), F32)],
        compiler_params=_params("arbitrary"),
    )(dy, dy, d_pool, proj, h_f, h_b, w_pool, pool_scale)


def _scan_bwd_call(xc, dh, h_dir, dxc_prev, wa, wi, ba, bi, lam, reverse):
    s, c = xc.shape
    heads = c // LRU_HEAD
    t = min(SEQ_TILE, s)
    n_tiles = s // t
    per = t // SUBLANES
    last_blk = s // SUBLANES - 1
    tile = (lambda i: (i, 0)) if reverse else (lambda i: (n_tiles - 1 - i, 0))
    if reverse:
        halo = lambda i: (jnp.minimum((i + 1) * per, last_blk), 0)
    else:
        halo = lambda i: (jnp.maximum((n_tiles - 1 - i) * per - 1, 0), 0)
    whole2 = lambda i: (0, 0)
    whole3 = lambda i: (0, 0, 0)
    has_prev = dxc_prev is not None

    def body(*refs):
        (xc_ref, dh_ref, h_ref, hh_ref) = refs[:4]
        k = 4
        prev_ref = None
        if has_prev:
            prev_ref = refs[k]
            k += 1
        wa_ref, wi_ref, ba_ref, bi_ref, lam_ref = refs[k:k + 5]
        dxc_ref, dwa_ref, dwi_ref, dba_ref, dbi_ref, dsp_ref = refs[k + 5:k + 11]
        pr_s, pi_s, g_s, carry_s = refs[k + 11:]
        step = pl.program_id(0)
        tile_idx = step if reverse else n_tiles - 1 - step

        @pl.when(step == 0)
        def _():
            carry_s[...] = jnp.zeros_like(carry_s)
            dwa_ref[...] = jnp.zeros_like(dwa_ref)
            dwi_ref[...] = jnp.zeros_like(dwi_ref)
            dba_ref[...] = jnp.zeros_like(dba_ref)
            dbi_ref[...] = jnp.zeros_like(dbi_ref)
            dsp_ref[...] = jnp.zeros_like(dsp_ref)

        _gate_matmuls(xc_ref, wa_ref, wi_ref, pr_s, pi_s, heads)
        ba_v, bi_v = ba_ref[...], bi_ref[...]
        sp = _softplus(-lam_ref[...])
        rows = lax.broadcasted_iota(jnp.int32, (SUBLANES, c), 0)

        def gates(r0, _):
            rs = pl.ds(r0, SUBLANES)
            _, _, a, _ = _rg_gates(pr_s[rs, :], pi_s[rs, :], ba_v, bi_v, sp)
            g_s[rs, :] = a

        _chunk_loop(t, gates)

        def row(j, carry):
            r = j if reverse else (t - 1 - j)
            gt = dh_ref[pl.ds(r, 1), :] + carry
            carry = g_s[pl.ds(r, 1), :] * gt
            g_s[pl.ds(r, 1), :] = gt
            return carry

        carry_s[...] = lax.fori_loop(0, t, row, carry_s[...], unroll=8)

        def chunk(r0, _):
            rs = pl.ds(r0, SUBLANES)
            xcv = xc_ref[rs, :]
            r, ig, a, mult = _rg_gates(pr_s[rs, :], pi_s[rs, :], ba_v, bi_v, sp)
            gt = g_s[rs, :]
            cur = h_ref[rs, :]
            if reverse:
                after = h_ref[pl.ds(pl.multiple_of(jnp.minimum(r0 + SUBLANES, t - SUBLANES), SUBLANES), SUBLANES), :]
                edge = jnp.where(tile_idx == n_tiles - 1, 0.0, hh_ref[...])
                nxt = jnp.where(r0 == t - SUBLANES, edge, after)
                hs = _shift_rows(cur, cur, nxt, 1, rows)
            else:
                before = h_ref[pl.ds(pl.multiple_of(jnp.maximum(r0 - SUBLANES, 0), SUBLANES), SUBLANES), :]
                edge = jnp.where(tile_idx == 0, 0.0, hh_ref[...])
                prv = jnp.where(r0 == 0, edge, before)
                hs = _shift_rows(prv, cur, cur, -1, rows)
            gx = gt * xcv
            dmult = gx * ig
            di = gx * mult
            dlog_a = (gt * hs) * a - dmult * (a * a) / mult
            dr = dlog_a * (-RG_C * sp)
            dsp_ref[...] += dlog_a * (-RG_C * r)
            dpr = dr * r * (1.0 - r)
            dpi = di * ig * (1.0 - ig)
            dba_ref[...] += dpr
            dbi_ref[...] += dpi
            direct = gt * mult * ig
            if has_prev:
                direct = direct + prev_ref[rs, :]
            dxc_ref[rs, :] = direct
            pr_s[rs, :] = dpr
            pi_s[rs, :] = dpi

        _chunk_loop(t, chunk)

        for h in range(heads):
            cs = pl.ds(h * LRU_HEAD, LRU_HEAD)
            xb = xc_ref[:, cs].astype(BF16)
            dprb = pr_s[:, cs].astype(BF16)
            dpib = pi_s[:, cs].astype(BF16)
            dwa_ref[h] += lax.dot_general(xb, dprb, (((0,), (0,)), ((), ())), preferred_element_type=F32)
            dwi_ref[h] += lax.dot_general(xb, dpib, (((0,), (0,)), ((), ())), preferred_element_type=F32)
            dxc_ref[:, cs] += (
                lax.dot_general(dprb, wa_ref[h], (((1,), (1,)), ((), ())), preferred_element_type=F32)
                + lax.dot_general(dpib, wi_ref[h], (((1,), (1,)), ((), ())), preferred_element_type=F32))

    in_specs = [pl.BlockSpec((t, c), tile), pl.BlockSpec((t, c), tile), pl.BlockSpec((t, c), tile),
                pl.BlockSpec((SUBLANES, c), halo)]
    args = [xc, dh, h_dir, h_dir]
    if has_prev:
        in_specs.append(pl.BlockSpec((t, c), tile))
        args.append(dxc_prev)
    in_specs += [pl.BlockSpec((heads, LRU_HEAD, LRU_HEAD), whole3), pl.BlockSpec((heads, LRU_HEAD, LRU_HEAD), whole3),
                 pl.BlockSpec((1, c), whole2), pl.BlockSpec((1, c), whole2), pl.BlockSpec((1, c), whole2)]
    args += [wa, wi, ba, bi, lam]
    return pl.pallas_call(
        body, name="scan_bwd_rev" if reverse else "scan_bwd", grid=(n_tiles,),
        in_specs=in_specs,
        out_specs=[pl.BlockSpec((t, c), tile),
                   pl.BlockSpec((heads, LRU_HEAD, LRU_HEAD), whole3), pl.BlockSpec((heads, LRU_HEAD, LRU_HEAD), whole3),
                   pl.BlockSpec((SUBLANES, c), whole2), pl.BlockSpec((SUBLANES, c), whole2), pl.BlockSpec((SUBLANES, c), whole2)],
        out_shape=[jax.ShapeDtypeStruct((s, c), F32),
                   jax.ShapeDtypeStruct((heads, LRU_HEAD, LRU_HEAD), F32), jax.ShapeDtypeStruct((heads, LRU_HEAD, LRU_HEAD), F32),
                   jax.ShapeDtypeStruct((SUBLANES, c), F32), jax.ShapeDtypeStruct((SUBLANES, c), F32),
                   jax.ShapeDtypeStruct((SUBLANES, c), F32)],
        scratch_shapes=[pltpu.VMEM((t, c), F32), pltpu.VMEM((t, c), F32), pltpu.VMEM((t, c), F32), pltpu.VMEM((1, c), F32)],
        compiler_params=_params("arbitrary"),
    )(*args)


def _dproj_call(e_pool, dxc, proj, dgate, conv_w, p):
    s, c = dxc.shape
    pg = p // N_POOL_GROUPS
    t = min(SEQ_TILE, s)
    n_tiles = s // t

    def body(e_ref, ep_ref, en_ref, dx_ref, dxp_ref, dxn_ref, u_ref, up_ref, un_ref, dgate_ref, w_ref,
             dproj_ref, dcw_ref, dcb_ref, st_s):
        i = pl.program_id(0)
        first, last = i == 0, i == n_tiles - 1

        @pl.when(first)
        def _():
            dcw_ref[...] = jnp.zeros_like(dcw_ref)
            dcb_ref[...] = jnp.zeros_like(dcb_ref)

        rows_p = lax.broadcasted_iota(jnp.int32, (SUBLANES, pg), 0)
        rows_c = lax.broadcasted_iota(jnp.int32, (SUBLANES, c), 0)
        w = w_ref[...]

        def chunk(r0, _):
            rs = pl.ds(r0, SUBLANES)
            for g, win in enumerate(POOL_WINDOWS):
                cols = pl.ds(g * pg, pg)
                prv, cur, nxt = _neighbour_chunks(e_ref, ep_ref, en_ref, r0, t, cols, first, last)
                tot = cur
                for o in range(-(win // 2) + 1, win // 2 + 1):
                    if o != 0:
                        tot = tot + _shift_rows(prv, cur, nxt, o, rows_p)
                cnt = _window_counts(r0, i, t, s, win // 2, (SUBLANES, pg))
                st_s[rs, cols] = tot - cur * cnt
            prv, cur, nxt = _neighbour_chunks(dx_ref, dxp_ref, dxn_ref, r0, t, slice(None), first, last)
            du = w[1:2] * cur
            du += w[0:1] * _shift_rows(prv, cur, nxt, 1, rows_c)
            du += w[2:3] * _shift_rows(prv, cur, nxt, -1, rows_c)
            du += w[3:4] * _shift_rows(prv, cur, nxt, -2, rows_c)
            st_s[rs, pl.ds(p, c)] = du
            uprv, ucur, unxt = _neighbour_chunks(u_ref, up_ref, un_ref, r0, t, slice(None), first, last)
            dcb_ref[...] += cur
            for j, o in enumerate((-1, 0, 1, 2)):
                dcw_ref[j] += cur * _shift_rows(uprv, ucur, unxt, o, rows_c)

        _chunk_loop(t, chunk)
        dproj_ref[:, pl.ds(0, p + c)] = st_s[...].astype(BF16)
        dproj_ref[:, pl.ds(p + c, c)] = dgate_ref[...]

    return pl.pallas_call(
        body, name="dproj", grid=(n_tiles,),
        in_specs=_halo_specs(t, s, p, 0) + _halo_specs(t, s, c, 0) + _halo_specs(t, s, c, 1) + [
            pl.BlockSpec((t, c), lambda i: (i, 0)), pl.BlockSpec((4, c), lambda i: (0, 0))],
        out_specs=[pl.BlockSpec((t, p + 2 * c), lambda i: (i, 0)),
                   pl.BlockSpec((4, SUBLANES, c), lambda i: (0, 0, 0)), pl.BlockSpec((SUBLANES, c), lambda i: (0, 0))],
        out_shape=[jax.ShapeDtypeStruct((s, p + 2 * c), BF16), jax.ShapeDtypeStruct((4, SUBLANES, c), F32),
                   jax.ShapeDtypeStruct((SUBLANES, c), F32)],
        scratch_shapes=[pltpu.VMEM((t, p + c), F32)],
        compiler_params=_params("arbitrary"),
    )(e_pool, e_pool, e_pool, dxc, dxc, dxc, proj, proj, proj, dgate, conv_w)


def _dx_call(dproj, w_in, dz1, dep):
    s, e = dproj.shape
    n, d, e4 = w_in.shape
    tm = min(MM_TILE, s)

    def body(dp_ref, w_ref, dz_ref, dep_ref, o_ref):
        k = pl.program_id(1)

        @pl.when(k == 0)
        def _():
            o_ref[...] = ALPHA * dz_ref[...]

        o_ref[...] += lax.dot_general(dp_ref[...], w_ref[...], (((1,), (1,)), ((), ())), preferred_element_type=F32)

    return pl.pallas_call(
        body, name="grad_x", grid=(s // tm, n),
        in_specs=[pl.BlockSpec((tm, e4), lambda i, k: (i, k)), pl.BlockSpec((None, d, e4), lambda i, k: (k, 0, 0)),
                  pl.BlockSpec((tm, d), lambda i, k: (i, 0)), ANY],
        out_specs=pl.BlockSpec((tm, d), lambda i, k: (i, 0)),
        out_shape=jax.ShapeDtypeStruct((s, d), F32),
        compiler_params=_params("arbitrary", "arbitrary"),
    )(dproj, w_in, dz1, dep)


def _row_tile(rows, cols, n_arrays):
    limit = max(SUBLANES, ELT_BLOCK_BYTES // (4 * cols * max(1, n_arrays // 4)))
    best = SUBLANES
    for cand in range(SUBLANES, min(rows, limit) + 1, SUBLANES):
        if rows % cand == 0:
            best = cand
    return best if rows % SUBLANES == 0 else rows


def _cast_to_slot_call(a, idx, dtype, name):
    rows, cols = a.shape
    tr = _row_tile(rows, cols, 2)

    def body(idx_ref, a_ref, o_ref):
        o_ref[...] = a_ref[...].astype(dtype)

    return pl.pallas_call(
        body, name=name,
        grid_spec=pltpu.PrefetchScalarGridSpec(
            num_scalar_prefetch=1, grid=(rows // tr,),
            in_specs=[pl.BlockSpec((tr, cols), lambda i, idx_ref: (i, 0))],
            out_specs=pl.BlockSpec((None, tr, cols), lambda i, idx_ref: (idx_ref[1], i, 0))),
        out_shape=jax.ShapeDtypeStruct((N_CHIPS, rows, cols), dtype),
        compiler_params=_params("arbitrary"),
    )(idx, a)


def _add_half_call(g, recv, idx, to_slot, name):
    _, rows, cols = g.shape
    tr = _row_tile(rows, cols, 3)

    def body(idx_ref, g_ref, r_ref, o_ref):
        o_ref[...] = g_ref[...] + r_ref[...]

    if to_slot:
        out_spec = pl.BlockSpec((None, tr, cols), lambda i, idx_ref: (idx_ref[1], i, 0))
        out_shape = jax.ShapeDtypeStruct((N_CHIPS, rows, cols), F32)
    else:
        out_spec = pl.BlockSpec((tr, cols), lambda i, idx_ref: (i, 0))
        out_shape = jax.ShapeDtypeStruct((rows, cols), F32)
    return pl.pallas_call(
        body, name=name,
        grid_spec=pltpu.PrefetchScalarGridSpec(
            num_scalar_prefetch=1, grid=(rows // tr,),
            in_specs=[pl.BlockSpec((None, tr, cols), lambda i, idx_ref: (idx_ref[0], i, 0)),
                      pl.BlockSpec((tr, cols), lambda i, idx_ref: (i, 0))],
            out_specs=out_spec),
        out_shape=out_shape,
        compiler_params=_params("arbitrary"),
    )(idx, g, recv)


def _sum_chips_call(own, recv, idx, name):
    _, rows, cols = recv.shape
    tr = _row_tile(rows, cols, 5)
    out_spec = pl.BlockSpec((None, tr, cols), lambda i, idx_ref: (idx_ref[0], i, 0))
    if own is None:
        def body(idx_ref, r_ref, o_ref):
            o_ref[...] = ((r_ref[0] + r_ref[1]) + r_ref[2]) + r_ref[3]
        in_specs = [pl.BlockSpec((N_CHIPS, tr, cols), lambda i, idx_ref: (0, i, 0))]
        args = (recv,)
    else:
        def body(idx_ref, p_ref, r_ref, o_ref):
            o_ref[...] = ((p_ref[...] + r_ref[0]) + r_ref[1]) + r_ref[2]
        in_specs = [pl.BlockSpec((None, tr, cols), lambda i, idx_ref: (idx_ref[1], i, 0)),
                    pl.BlockSpec((N_CHIPS - 1, tr, cols), lambda i, idx_ref: (0, i, 0))]
        args = (own, recv)
    return pl.pallas_call(
        body, name=name,
        grid_spec=pltpu.PrefetchScalarGridSpec(num_scalar_prefetch=1, grid=(rows // tr,), in_specs=in_specs, out_specs=out_spec),
        out_shape=jax.ShapeDtypeStruct((2, rows, cols), F32),
        compiler_params=_params("arbitrary"),
    )(idx, *args)


def _adamw_call(g, w, m, v, name):
    rows, cols = w.shape
    tr = _row_tile(rows, cols, 7)

    def body(g_ref, w_ref, m_ref, v_ref, d_ref, mo_ref, vo_ref):
        gv = g_ref[...]
        mn = ADAM_B1 * m_ref[...] + (1.0 - ADAM_B1) * gv
        vn = ADAM_B2 * v_ref[...] + (1.0 - ADAM_B2) * (gv * gv)
        m_hat = mn / (1.0 - ADAM_B1 ** ADAM_STEP)
        v_hat = vn / (1.0 - ADAM_B2 ** ADAM_STEP)
        d_ref[...] = -ADAM_LR * (m_hat / (jnp.sqrt(v_hat) + ADAM_EPS) + ADAM_WD * w_ref[...])
        mo_ref[...] = mn
        vo_ref[...] = vn

    spec = pl.BlockSpec((tr, cols), lambda i: (i, 0))
    shape = jax.ShapeDtypeStruct((rows, cols), F32)
    return pl.pallas_call(
        body, name=name, grid=(rows // tr,),
        in_specs=[spec] * 4, out_specs=[spec] * 3, out_shape=[shape] * 3,
        compiler_params=_params("arbitrary"),
    )(g, w, m, v)


def _mesh_place():
    x, y, c = lax.axis_index("x"), lax.axis_index("y"), lax.axis_index("c")
    chips = [(1 - x, y), (x, 1 - y), (1 - x, 1 - y)]
    return x, y, c, chips


def _remote(src, dst, send_sems, recv_sems, idx, device):
    return pltpu.make_async_remote_copy(src_ref=src, dst_ref=dst, send_sem=send_sems.at[idx], recv_sem=recv_sems.at[idx],
                                        device_id=device, device_id_type=MESH)


HBM_SPEC = pl.BlockSpec(memory_space=pltpu.HBM)
SEM_SPEC = pl.BlockSpec(memory_space=pltpu.SEMAPHORE)
ORDERED_EFFECT = pltpu.SideEffectType.DATAFLOW_SIDE_EFFECTING


def _in_hbm(a):
    return pltpu.with_memory_space_constraint(a, pltpu.HBM)


def _start_copies_call(name, bufs, groups):
    n, g = len(bufs), len(groups)

    def body(*refs):
        outs = refs[n:2 * n]
        sems = refs[2 * n:2 * n + 2 * g]
        token = refs[2 * n + 2 * g]
        for i, (which, copies_fn, _) in enumerate(groups):
            for mine, _ in copies_fn([outs[w] for w in which], sems[2 * i], sems[2 * i + 1]):
                mine.start()
        token[...] = jnp.zeros_like(token)

    sem_shapes = [pltpu.SemaphoreType.DMA((cnt,)) for _, _, cnt in groups for _ in range(2)]
    res = pl.pallas_call(
        body, name=name,
        in_specs=[HBM_SPEC] * n,
        out_specs=[HBM_SPEC] * n + [SEM_SPEC] * (2 * g) + [pl.BlockSpec(memory_space=pltpu.VMEM)],
        out_shape=[pltpu.HBM(a.shape, a.dtype) for a in bufs] + sem_shapes + [jax.ShapeDtypeStruct((SUBLANES, LANES), F32)],
        input_output_aliases={a: a for a in range(n)},
        compiler_params=pltpu.CompilerParams(has_side_effects=ORDERED_EFFECT),
    )(*[_in_hbm(a) for a in bufs])
    sems = res[n:n + 2 * g]
    return list(res[:n]), [(sems[2 * i], sems[2 * i + 1]) for i in range(g)], res[n + 2 * g]


def _wait_copies_call(name, bufs, sems, copies_fn, after):
    n = len(bufs)

    def body(*refs):
        ins = refs[:n]
        send_sems, recv_sems = refs[n], refs[n + 1]
        for mine, arriving in copies_fn(list(ins), send_sems, recv_sems):
            arriving.wait_recv()
            mine.wait_send()

    res = pl.pallas_call(
        body, name=name,
        in_specs=[HBM_SPEC] * n + [SEM_SPEC, SEM_SPEC, ANY],
        out_specs=[HBM_SPEC] * n,
        out_shape=[pltpu.HBM(a.shape, a.dtype) for a in bufs],
        input_output_aliases={a: a for a in range(n)},
        compiler_params=pltpu.CompilerParams(has_side_effects=ORDERED_EFFECT),
    )(*bufs, sems[0], sems[1], after)
    return list(res)


def _gather_copies(bufs, send_sems, recv_sems):
    x, y, c, chips = _mesh_place()
    k = 2 * x + y
    out = []
    for a, buf in enumerate(bufs):
        for j, (px, py) in enumerate(chips):
            kj = 2 * px + py
            mine = _remote(buf.at[k, c], buf.at[k, c], send_sems, recv_sems, 3 * a + j, (px, py, c))
            arriving = _remote(buf.at[k, c], buf.at[kj, c], send_sems, recv_sems, 3 * a + j, (px, py, c))
            out.append((mine, arriving))
    return out


def _exchange_copies(n_sharded, n_replicated):
    def copies(bufs, send_sems, recv_sems):
        x, y, c, chips = _mesh_place()
        k = 2 * x + y
        sums, lands = bufs[:n_sharded], bufs[n_sharded:2 * n_sharded]
        repl = bufs[2 * n_sharded:]
        out = []
        for j, (px, py) in enumerate(chips):
            kj = 2 * px + py
            for a in range(n_sharded):
                cp = _remote(sums[a].at[kj], lands[a].at[j], send_sems, recv_sems, 3 * a + j, (px, py, c))
                out.append((cp, cp))
            for a in range(n_replicated):
                idx = 3 * (n_sharded + a) + j
                mine = _remote(repl[a].at[k], repl[a].at[k], send_sems, recv_sems, idx, (px, py, c))
                arriving = _remote(repl[a].at[k], repl[a].at[kj], send_sems, recv_sems, idx, (px, py, c))
                out.append((mine, arriving))
        return out
    return copies


def _forward_to_sibling_call(bufs, name):
    n = len(bufs)

    def body(*refs):
        ins, outs = refs[:n], refs[n:2 * n]
        send_sems, recv_sems = refs[2 * n:]
        x, y, c, chips = _mesh_place()
        sibling = (x, y, 1 - c)
        sends = []
        for a in range(n):
            for j, (px, py) in enumerate(chips):
                kj = 2 * px + py
                sends.append(_remote(ins[a].at[kj, c], outs[a].at[kj, c], send_sems, recv_sems, 3 * a + j, sibling))
        for cp in sends:
            cp.start()
        for a in range(n):
            for j, (px, py) in enumerate(chips):
                kj = 2 * px + py
                _remote(ins[a].at[kj, c], outs[a].at[kj, 1 - c], send_sems, recv_sems, 3 * a + j, sibling).wait_recv()
        for cp in sends:
            cp.wait_send()

    return pl.pallas_call(
        body, name=name,
        in_specs=[ANY] * n, out_specs=[ANY] * n,
        out_shape=[jax.ShapeDtypeStruct(a.shape, a.dtype) for a in bufs],
        input_output_aliases={a: a for a in range(n)},
        scratch_shapes=[pltpu.SemaphoreType.DMA((3 * n,)), pltpu.SemaphoreType.DMA((3 * n,))],
    )(*bufs)


def _sibling_halves_call(grads, name):
    n = len(grads)

    def body(*refs):
        ins, outs = refs[:n], refs[n:2 * n]
        send_sems, recv_sems = refs[2 * n:]
        x, y, c, _ = _mesh_place()
        sibling = (x, y, 1 - c)
        copies = [_remote(ins[a].at[1 - c], outs[a], send_sems, recv_sems, a, sibling) for a in range(n)]
        for cp in copies:
            cp.start()
        for cp in copies:
            cp.wait_recv()
        for cp in copies:
            cp.wait_send()

    return pl.pallas_call(
        body, name=name,
        in_specs=[ANY] * n, out_specs=[ANY] * n,
        out_shape=[jax.ShapeDtypeStruct(a.shape[1:], a.dtype) for a in grads],
        scratch_shapes=[pltpu.SemaphoreType.DMA((n,)), pltpu.SemaphoreType.DMA((n,))],
    )(*grads)


def _join_halves_call(bufs, name):
    n = len(bufs)

    def body(*refs):
        ins, outs = refs[:n], refs[n:2 * n]
        send_sems, recv_sems = refs[2 * n:]
        x, y, c, _ = _mesh_place()
        sibling = (x, y, 1 - c)
        copies = [_remote(ins[a].at[c], outs[a].at[c], send_sems, recv_sems, a, sibling) for a in range(n)]
        for cp in copies:
            cp.start()
        for a in range(n):
            _remote(ins[a].at[c], outs[a].at[1 - c], send_sems, recv_sems, a, sibling).wait_recv()
        for cp in copies:
            cp.wait_send()

    return pl.pallas_call(
        body, name=name,
        in_specs=[ANY] * n, out_specs=[ANY] * n,
        out_shape=[jax.ShapeDtypeStruct(a.shape, a.dtype) for a in bufs],
        input_output_aliases={a: a for a in range(n)},
        scratch_shapes=[pltpu.SemaphoreType.DMA((n,)), pltpu.SemaphoreType.DMA((n,))],
    )(*bufs)


def _pack(arrays, rows_multiple):
    flat = jnp.concatenate([a.reshape(-1) for a in arrays])
    per = LANES * rows_multiple
    padded = -(-flat.shape[0] // per) * per
    flat = jnp.pad(flat, (0, padded - flat.shape[0]))
    return flat.reshape(-1, LANES)


def _unpack(packed, shapes):
    flat = packed.reshape(-1)
    out, at = [], 0
    for shp in shapes:
        size = 1
        for dim in shp:
            size *= dim
        out.append(flat[at:at + size].reshape(shp))
        at += size
    return out


def _halves(a):
    return a.reshape((2, a.shape[0] // 2) + a.shape[1:])


def kernel(x, ln_mix_g, ln_mix_b, w_in, w_pool, pool_scale, conv_w, conv_b, w_rg_a, b_rg_a, w_rg_i, b_rg_i, rg_lambda, w_out, ln_ffn_g, ln_ffn_b, w_mlp_in, w_mlp_out, loss_target, m_ln_mix_g, m_ln_mix_b, m_w_in, m_w_pool, m_pool_scale, m_conv_w, m_conv_b, m_w_rg_a, m_b_rg_a, m_w_rg_i, m_b_rg_i, m_rg_lambda, m_w_out, m_ln_ffn_g, m_ln_ffn_b, m_w_mlp_in, m_w_mlp_out, v_ln_mix_g, v_ln_mix_b, v_w_in, v_w_pool, v_pool_scale, v_conv_w, v_conv_b, v_w_rg_a, v_b_rg_a, v_w_rg_i, v_b_rg_i, v_rg_lambda, v_w_out, v_ln_ffn_g, v_ln_ffn_b, v_w_mlp_in, v_w_mlp_out):
    weights = dict(ln_mix_g=ln_mix_g, ln_mix_b=ln_mix_b, w_in=w_in, w_pool=w_pool, pool_scale=pool_scale, conv_w=conv_w,
                   conv_b=conv_b, w_rg_a=w_rg_a, b_rg_a=b_rg_a, w_rg_i=w_rg_i, b_rg_i=b_rg_i, rg_lambda=rg_lambda,
                   w_out=w_out, ln_ffn_g=ln_ffn_g, ln_ffn_b=ln_ffn_b, w_mlp_in=w_mlp_in, w_mlp_out=w_mlp_out)
    m_in = dict(ln_mix_g=m_ln_mix_g, ln_mix_b=m_ln_mix_b, w_in=m_w_in, w_pool=m_w_pool, pool_scale=m_pool_scale,
                conv_w=m_conv_w, conv_b=m_conv_b, w_rg_a=m_w_rg_a, b_rg_a=m_b_rg_a, w_rg_i=m_w_rg_i, b_rg_i=m_b_rg_i,
                rg_lambda=m_rg_lambda, w_out=m_w_out, ln_ffn_g=m_ln_ffn_g, ln_ffn_b=m_ln_ffn_b, w_mlp_in=m_w_mlp_in,
                w_mlp_out=m_w_mlp_out)
    v_in = dict(ln_mix_g=v_ln_mix_g, ln_mix_b=v_ln_mix_b, w_in=v_w_in, w_pool=v_w_pool, pool_scale=v_pool_scale,
                conv_w=v_conv_w, conv_b=v_conv_b, w_rg_a=v_w_rg_a, b_rg_a=v_b_rg_a, w_rg_i=v_w_rg_i, b_rg_i=v_b_rg_i,
                rg_lambda=v_rg_lambda, w_out=v_w_out, ln_ffn_g=v_ln_ffn_g, ln_ffn_b=v_ln_ffn_b, w_mlp_in=v_w_mlp_in,
                w_mlp_out=v_w_mlp_out)
    names = list(weights)
    big = ("w_in", "w_out", "w_mlp_in", "w_mlp_out")

    xs = x[0]
    tgt = loss_target[0]
    s, d = xs.shape
    p = c = d // 2
    pg = p // N_POOL_GROUPS
    heads = c // LRU_HEAD
    core = lax.axis_index("c")
    shard = 2 * lax.axis_index("x") + lax.axis_index("y")

    idx = jnp.stack([core, shard]).astype(jnp.int32)
    small_shard = _pack([conv_w[0], b_rg_a[0], b_rg_i[0], rg_lambda[0]], 2 * SUBLANES)
    to_gather = [(w_in[0], BF16), (w_out[0], BF16), (w_mlp_in[0], BF16), (w_mlp_out[0], BF16),
                 (w_pool[0].reshape(-1, pg), BF16), (small_shard, F32)]
    slots = [_cast_to_slot_call(a, idx, dt, f"gather_slot_{i}") for i, (a, dt) in enumerate(to_gather)]
    views = [sl.reshape(N_CHIPS, 2, sl.shape[1] // 2, sl.shape[2]) for sl in slots]
    first = (0, 4, 5)
    in_flight, g_sems, g_token = _start_copies_call(
        "gather_start", views,
        [(first, _gather_copies, 3 * len(first)), ((1,), _gather_copies, 3), ((2,), _gather_copies, 3), ((3,), _gather_copies, 3)])

    def arrive(which, group, after, tag):
        got = _wait_copies_call(f"gather_wait_{tag}", [in_flight[w] for w in which], g_sems[group], _gather_copies, after)
        return _forward_to_sibling_call(got, f"gather_forward_{tag}")

    gathered = [None] * len(views)
    gathered[0], gathered[4], gathered[5] = arrive(first, 0, g_token, "w_in")
    w_in_f = gathered[0].reshape((N_CHIPS,) + w_in.shape[1:])
    w_pool_f = gathered[4].reshape(N_CHIPS, N_POOL_GROUPS, pg // N_CHIPS, pg).transpose(1, 0, 2, 3).reshape(N_POOL_GROUPS, pg, pg)
    c4 = c // N_CHIPS
    small_parts = [_unpack(gathered[5][k].reshape(-1, LANES), [(4, c4), (2, c4), (2, c4), (2, c4)]) for k in range(N_CHIPS)]
    conv_w_f = jnp.concatenate([sp_[0] for sp_ in small_parts], axis=1)
    b_a_f = jnp.concatenate([sp_[1] for sp_ in small_parts], axis=1)
    b_i_f = jnp.concatenate([sp_[2] for sp_ in small_parts], axis=1)
    lam_f = jnp.concatenate([sp_[3] for sp_ in small_parts], axis=1)
    wa_b = w_rg_a[0].astype(BF16)
    wi_b = w_rg_i[0].astype(BF16)

    proj, xb = _proj_call(xs, w_in_f)
    xc = _conv_call(proj, conv_w_f, conv_b, c)
    h_b = _scan_fwd_call(xc, wa_b[1], wi_b[1], b_a_f[1:2], b_i_f[1:2], lam_f[1:2], True)
    h_f = _scan_fwd_call(xc, wa_b[0], wi_b[0], b_a_f[0:1], b_i_f[0:1], lam_f[0:1], False)
    y, d_pool = _pool_combine_call(proj, h_f, h_b, w_pool_f, pool_scale, p)
    w_out_f = arrive((1,), 1, y, "w_out")[0].reshape(d, d)
    xh1, x1b, rstd1 = _out_ln1_call(y, w_out_f, xs, ln_mix_g, ln_mix_b)
    w1_f = arrive((2,), 2, x1b, "w_mlp_in")[0].reshape((N_CHIPS,) + w_mlp_in.shape[1:])
    r_act, hsq = _mlp_in_call(x1b, w1_f)
    w2_f = arrive((3,), 3, hsq, "w_mlp_out")[0].reshape(N_CHIPS * w_mlp_out.shape[1], d)
    dz2, dz2b, loss8, dg2, db2 = _mlp_out_ln2_call(hsq, w2_f, xh1, ln_mix_g, ln_mix_b, ln_ffn_g, ln_ffn_b, tgt)

    def chip_sums_of(grads, tag, small_at=None):
        from_sibling = _sibling_halves_call(grads, f"reduce_siblings_{tag}")
        out = []
        for a, (g, rv) in enumerate(zip(grads, from_sibling)):
            cols = g.shape[-1]
            part = _add_half_call(g.reshape(2, -1, cols), rv.reshape(-1, cols), idx, a == small_at, f"reduce_add_{tag}_{a}")
            out.append(part if a == small_at else part.reshape((N_CHIPS,) + rv.shape[1:]))
        return out

    def start_exchange(sums, n_repl, tag):
        n_sh = len(sums) - n_repl
        lands = [lax.empty((N_CHIPS - 1,) + a.shape[1:], a.dtype) for a in sums[:n_sh]]
        bufs = sums[:n_sh] + lands + sums[n_sh:]
        copies = _exchange_copies(n_sh, n_repl)
        flying, sems, token = _start_copies_call(
            f"reduce_start_{tag}", bufs, [(tuple(range(len(bufs))), copies, 3 * len(sums))])
        return (flying, sems[0], copies, n_sh, tag), token

    def finish_exchange(state, after):
        flying, sems, copies, n_sh, tag = state
        got = _wait_copies_call(f"reduce_wait_{tag}", flying, sems, copies, after)
        halves = []
        for a in range(n_sh):
            own, land = got[a], got[n_sh + a]
            cols = own.shape[-1]
            total = _sum_chips_call(own.reshape(N_CHIPS, -1, cols), land.reshape(N_CHIPS - 1, -1, cols), idx,
                                    f"reduce_sum_{tag}_{a}")
            halves.append(total.reshape((2,) + own.shape[1:]))
        for a, rp in enumerate(got[2 * n_sh:]):
            halves.append(_sum_chips_call(None, rp, idx, f"reduce_sum_{tag}_r{a}"))
        return _join_halves_call(halves, f"reduce_join_{tag}")

    g_w2 = _row_sharded_grad(hsq, dz2b, "grad_w_mlp_out")
    flying_w2, token = start_exchange(chip_sums_of([g_w2], "w2"), 0, "w2")
    dpre = _dhsq_call(dz2b, w2_f, r_act, token)
    g_w1 = _col_sharded_grad(x1b, dpre, "grad_w_mlp_in")
    flying_w1, token = start_exchange(chip_sums_of([g_w1], "w1"), 0, "w1")
    dz1, dz1b, dg1, db1 = _dx1_ln1_bwd_call(dpre, w1_f, dz2, xh1, rstd1, ln_mix_g, token)
    g_wout = _row_sharded_grad(y, dz1b, "grad_w_out")
    dy = _dy_call(dz1b, w_out_f)
    e_pool, dh, dgate, g_wpool, g_pscale8 = _mixer_bwd_call(dy, d_pool, proj, h_f, h_b, w_pool_f, pool_scale, p)
    dxc0, g_wa0, g_wi0, g_ba0, g_bi0, g_sp0 = _scan_bwd_call(
        xc, dh, h_f, None, wa_b[0], wi_b[0], b_a_f[0:1], b_i_f[0:1], lam_f[0:1], False)
    dxc, g_wa1, g_wi1, g_ba1, g_bi1, g_sp1 = _scan_bwd_call(
        xc, dh, h_b, dxc0, wa_b[1], wi_b[1], b_a_f[1:2], b_i_f[1:2], lam_f[1:2], True)
    dproj, g_cw8, g_cb8 = _dproj_call(e_pool, dxc, proj, dgate, conv_w_f, p)
    g_win = _col_sharded_grad(xb, dproj, "grad_w_in")

    rowsum = lambda a8: jnp.sum(a8, axis=-2)
    g_lam = jnp.stack([rowsum(g_sp0), rowsum(g_sp1)]) * (-_sigmoid(-lam_f))
    small_grads = {
        "ln_mix_g": rowsum(dg1), "ln_mix_b": rowsum(db1), "ln_ffn_g": rowsum(dg2), "ln_ffn_b": rowsum(db2),
        "pool_scale": rowsum(g_pscale8), "conv_b": rowsum(g_cb8),
        "w_rg_a": jnp.stack([g_wa0, g_wa1]), "w_rg_i": jnp.stack([g_wi0, g_wi1]),
        "w_pool": g_wpool, "conv_w": rowsum(g_cw8),
        "b_rg_a": jnp.stack([rowsum(g_ba0), rowsum(g_ba1)]), "b_rg_i": jnp.stack([rowsum(g_bi0), rowsum(g_bi1)]),
        "rg_lambda": g_lam,
    }
    small_names = list(small_grads)
    small_shapes = [small_grads[nm].shape for nm in small_names]
    g_small = _halves(_pack([small_grads[nm] for nm in small_names], 2 * SUBLANES))
    flying_rest, token = start_exchange(chip_sums_of([g_win, g_wout, g_small], "rest", small_at=2), 1, "rest")
    grad_x = _dx_call(dproj, w_in_f, dz1, token)

    grad_w, delta_w, new_m, new_v = {}, {}, {}, {}

    def adamw(nm, full):
        w2d = weights[nm][0]
        g2d = full.reshape(w2d.shape)
        dl, mn, vn = _adamw_call(g2d, w2d, m_in[nm][0], v_in[nm][0], f"adamw_{nm}")
        grad_w[nm], delta_w[nm], new_m[nm], new_v[nm] = g2d[None], dl[None], mn[None], vn[None]
        return vn

    adamw("w_mlp_out", finish_exchange(flying_w2, grad_x)[0])
    last = adamw("w_mlp_in", finish_exchange(flying_w1, grad_x)[0])
    joined = finish_exchange(flying_rest, last)
    adamw("w_in", joined[0])
    adamw("w_out", joined[1])

    small_full = dict(zip(small_names, _unpack(joined[2].reshape(-1, LANES), small_shapes)))
    local = dict(small_full)
    local["w_pool"] = lax.dynamic_slice_in_dim(small_full["w_pool"], shard * (pg // N_CHIPS), pg // N_CHIPS, axis=1)
    for nm in ("conv_w", "b_rg_a", "b_rg_i", "rg_lambda"):
        local[nm] = lax.dynamic_slice_in_dim(small_full[nm], shard * c4, c4, axis=1)
    small_w_shapes = [weights[nm].shape for nm in small_names]
    g_pack = _pack([local[nm] for nm in small_names], SUBLANES)
    w_pack = _pack([weights[nm] for nm in small_names], SUBLANES)
    m_pack = _pack([m_in[nm] for nm in small_names], SUBLANES)
    v_pack = _pack([v_in[nm] for nm in small_names], SUBLANES)
    dl_p, mn_p, vn_p = _adamw_call(g_pack, w_pack, m_pack, v_pack, "adamw_small")
    for nm, gl, dl, mn, vn in zip(small_names, _unpack(g_pack, small_w_shapes), _unpack(dl_p, small_w_shapes),
                                  _unpack(mn_p, small_w_shapes), _unpack(vn_p, small_w_shapes)):
        grad_w[nm], delta_w[nm], new_m[nm], new_v[nm] = gl, dl, mn, vn

    loss = lax.psum(jnp.sum(loss8) * (0.5 / d), ("x", "y", "c"))
    return (loss, grad_x[None], *[grad_w[nm] for nm in names], *[delta_w[nm] for nm in names],
            *[new_m[nm] for nm in names], *[new_v[nm] for nm in names])
```

```python
import functools

import jax
import jax.numpy as jnp
from jax import lax
from jax.experimental import pallas as pl
from jax.experimental.pallas import tpu as pltpu

F32 = jnp.float32
BF16 = jnp.bfloat16

N_CHIPS = 4
LANES = 128
SUBLANES = 8
LRU_HEAD = 128
N_POOL_GROUPS = 4
POOL_WINDOWS = (2, 4, 8, 16)
RG_C = 8.0
LN_EPS = 1e-5
ALPHA = 2.0 ** 0.25
ADAM_LR, ADAM_B1, ADAM_B2, ADAM_EPS, ADAM_WD, ADAM_STEP = 0.001, 0.9, 0.999, 1e-08, 0.01, 10
VMEM_LIMIT = 56 * 1024 * 1024
SEQ_TILE = 256
MM_TILE = 512
LN_UNROLL = 4
ELT_BLOCK_BYTES = 2 * 1024 * 1024
RESIDENT_OPERAND_BYTES = 16 * 1024 * 1024
MESH = pl.DeviceIdType.MESH
ANY = pl.BlockSpec(memory_space=pl.ANY)


def _params(*sem):
    return pltpu.CompilerParams(dimension_semantics=sem, vmem_limit_bytes=VMEM_LIMIT)


def _sigmoid(z):
    return 1.0 / (1.0 + jnp.exp(-z))


def _neg_expm1(z):
    series = -(z * (1.0 + z * (0.5 + z * (1.0 / 6.0 + z * (1.0 / 24.0)))))
    return jnp.where(z > -0.01, series, 1.0 - jnp.exp(z))


def _softplus(z):
    return jnp.maximum(z, 0.0) + jnp.log1p(jnp.exp(-jnp.abs(z)))


_GELU_K = 0.7978845608028654
_GELU_C = 0.044715


def _gelu_and_grad(u):
    t = jnp.tanh(_GELU_K * (u + _GELU_C * (u * u * u)))
    g = 0.5 * u * (1.0 + t)
    dg = 0.5 * (1.0 + t) + 0.5 * u * (1.0 - t * t) * (_GELU_K * (1.0 + 3.0 * _GELU_C * u * u))
    return g, dg


def _shift_rows(prv, cur, nxt, o, rows):
    if o == 0:
        return cur
    if o == SUBLANES:
        return nxt
    if o == -SUBLANES:
        return prv
    if o > 0:
        s = SUBLANES - o
        return jnp.where(rows < s, pltpu.roll(cur, s, 0), pltpu.roll(nxt, s, 0))
    p = -o
    return jnp.where(rows >= p, pltpu.roll(cur, p, 0), pltpu.roll(prv, p, 0))


def _neighbour_chunks(main_ref, prev_ref, next_ref, r0, t_rows, cols, first_tile, last_tile):
    cur = main_ref[pl.ds(r0, SUBLANES), cols]
    before = main_ref[pl.ds(pl.multiple_of(jnp.maximum(r0 - SUBLANES, 0), SUBLANES), SUBLANES), cols]
    after = main_ref[pl.ds(pl.multiple_of(jnp.minimum(r0 + SUBLANES, t_rows - SUBLANES), SUBLANES), SUBLANES), cols]
    halo_prev = jnp.where(first_tile, 0.0, prev_ref[:, cols])
    halo_next = jnp.where(last_tile, 0.0, next_ref[:, cols])
    prv = jnp.where(r0 == 0, halo_prev, before)
    nxt = jnp.where(r0 == t_rows - SUBLANES, halo_next, after)
    return prv, cur, nxt


def _halo_specs(t_rows, n_rows, width, col_block):
    per = t_rows // SUBLANES
    last = n_rows // SUBLANES - 1
    return [
        pl.BlockSpec((t_rows, width), lambda i: (i, col_block)),
        pl.BlockSpec((SUBLANES, width), lambda i: (jnp.maximum(i * per - 1, 0), col_block)),
        pl.BlockSpec((SUBLANES, width), lambda i: (jnp.minimum((i + 1) * per, last), col_block)),
    ]


def _chunk_loop(t_rows, fn, init=None, unroll=1):
    span = SUBLANES * unroll

    def step(ci, carry):
        base = pl.multiple_of(ci * span, span)
        for u in range(unroll):
            carry = fn(base + u * SUBLANES, carry)
        return carry
    return lax.fori_loop(0, t_rows // span, step, init)


def _proj_call(x, w_in):
    s, d = x.shape
    n, _, e4 = w_in.shape
    tm = min(MM_TILE, s)

    def body(x_ref, w_ref, proj_ref, xb_ref):
        @pl.when(pl.program_id(1) == 0)
        def _():
            xb_ref[...] = x_ref[...].astype(BF16)

        proj_ref[...] = jnp.dot(xb_ref[...], w_ref[...], preferred_element_type=F32)

    return pl.pallas_call(
        body, name="proj", grid=(s // tm, n),
        in_specs=[pl.BlockSpec((tm, d), lambda i, j: (i, 0)), pl.BlockSpec((None, d, e4), lambda i, j: (j, 0, 0))],
        out_specs=[pl.BlockSpec((tm, e4), lambda i, j: (i, j)), pl.BlockSpec((tm, d), lambda i, j: (i, 0))],
        out_shape=[jax.ShapeDtypeStruct((s, n * e4), F32), jax.ShapeDtypeStruct((s, d), BF16)],
        compiler_params=_params("arbitrary", "arbitrary"),
    )(x, w_in)


def _conv_call(proj, conv_w, conv_b, c):
    s = proj.shape[0]
    t = min(SEQ_TILE, s)
    n_tiles = s // t

    def body(u_ref, up_ref, un_ref, w_ref, b_ref, xc_ref):
        i = pl.program_id(0)
        rows = lax.broadcasted_iota(jnp.int32, (SUBLANES, c), 0)
        w = w_ref[...]
        b = b_ref[...]

        def chunk(r0, _):
            prv, cur, nxt = _neighbour_chunks(u_ref, up_ref, un_ref, r0, t, slice(None), i == 0, i == n_tiles - 1)
            acc = b + w[1:2] * cur
            acc += w[0:1] * _shift_rows(prv, cur, nxt, -1, rows)
            acc += w[2:3] * _shift_rows(prv, cur, nxt, 1, rows)
            acc += w[3:4] * _shift_rows(prv, cur, nxt, 2, rows)
            xc_ref[pl.ds(r0, SUBLANES), :] = acc

        _chunk_loop(t, chunk)

    return pl.pallas_call(
        body, name="conv_fwd", grid=(n_tiles,),
        in_specs=_halo_specs(t, s, c, 1) + [pl.BlockSpec((4, c), lambda i: (0, 0)), pl.BlockSpec((1, c), lambda i: (0, 0))],
        out_specs=pl.BlockSpec((t, c), lambda i: (i, 0)),
        out_shape=jax.ShapeDtypeStruct((s, c), F32),
        compiler_params=_params("arbitrary"),
    )(proj, proj, proj, conv_w, conv_b)


def _gate_matmuls(xc_ref, wa_ref, wi_ref, pr_s, pi_s, heads):
    for h in range(heads):
        cs = pl.ds(h * LRU_HEAD, LRU_HEAD)
        xb = xc_ref[:, cs].astype(BF16)
        pr_s[:, cs] = jnp.dot(xb, wa_ref[h], preferred_element_type=F32)
        pi_s[:, cs] = jnp.dot(xb, wi_ref[h], preferred_element_type=F32)


def _rg_gates(pr, pi, ba, bi, sp):
    r = _sigmoid(pr + ba)
    ig = _sigmoid(pi + bi)
    log_a = (-RG_C * r) * sp
    a = jnp.exp(log_a)
    mult = jnp.sqrt(_neg_expm1(2.0 * log_a))
    return r, ig, a, mult


def _scan_fwd_call(xc, wa, wi, ba, bi, lam, reverse):
    s, c = xc.shape
    heads = c // LRU_HEAD
    t = min(SEQ_TILE, s)
    n_tiles = s // t
    tile = (lambda i: (n_tiles - 1 - i, 0)) if reverse else (lambda i: (i, 0))
    whole2 = lambda i: (0, 0)
    whole3 = lambda i: (0, 0, 0)

    def body(xc_ref, wa_ref, wi_ref, ba_ref, bi_ref, lam_ref, h_ref, pr_s, pi_s, carry_s):
        @pl.when(pl.program_id(0) == 0)
        def _():
            carry_s[...] = jnp.zeros_like(carry_s)

        _gate_matmuls(xc_ref, wa_ref, wi_ref, pr_s, pi_s, heads)
        ba_v, bi_v = ba_ref[...], bi_ref[...]
        sp = _softplus(-lam_ref[...])

        def chunk(r0, _):
            rs = pl.ds(r0, SUBLANES)
            xcv = xc_ref[rs, :]
            _, ig, a, mult = _rg_gates(pr_s[rs, :], pi_s[rs, :], ba_v, bi_v, sp)
            pr_s[rs, :] = a
            pi_s[rs, :] = mult * ig * xcv

        _chunk_loop(t, chunk)

        def row(j, h):
            r = (t - 1 - j) if reverse else j
            h = pr_s[pl.ds(r, 1), :] * h + pi_s[pl.ds(r, 1), :]
            h_ref[pl.ds(r, 1), :] = h
            return h

        carry_s[...] = lax.fori_loop(0, t, row, carry_s[...], unroll=8)

    return pl.pallas_call(
        body, name="scan_fwd_rev" if reverse else "scan_fwd", grid=(n_tiles,),
        in_specs=[pl.BlockSpec((t, c), tile),
                  pl.BlockSpec((heads, LRU_HEAD, LRU_HEAD), whole3), pl.BlockSpec((heads, LRU_HEAD, LRU_HEAD), whole3),
                  pl.BlockSpec((1, c), whole2), pl.BlockSpec((1, c), whole2), pl.BlockSpec((1, c), whole2)],
        out_specs=pl.BlockSpec((t, c), tile),
        out_shape=jax.ShapeDtypeStruct((s, c), F32),
        scratch_shapes=[pltpu.VMEM((t, c), F32), pltpu.VMEM((t, c), F32), pltpu.VMEM((1, c), F32)],
        compiler_params=_params("arbitrary"),
    )(xc, wa, wi, ba, bi, lam)


def _window_counts(r0, tile_idx, t_rows, n_rows, half, shape):
    pos = tile_idx * t_rows + r0 + lax.broadcasted_iota(jnp.int32, shape, 0)
    hi = jnp.minimum(pos + half, n_rows)
    lo = jnp.maximum(pos - half, 0)
    return (hi - lo).astype(F32)


def _pool_combine_call(proj, h_f, h_b, w_pool, pool_scale, p):
    s = proj.shape[0]
    c = h_f.shape[1]
    pg = p // N_POOL_GROUPS
    t = min(SEQ_TILE, s)
    n_tiles = s // t

    def body(u_ref, up_ref, un_ref, gate_ref, hf_ref, hb_ref, wp_ref, sc_ref, y_ref, d_ref, d_s, yr_s):
        i = pl.program_id(0)
        rows = lax.broadcasted_iota(jnp.int32, (SUBLANES, pg), 0)

        def chunk(r0, _):
            rs = pl.ds(r0, SUBLANES)
            for g, w in enumerate(POOL_WINDOWS):
                cols = pl.ds(g * pg, pg)
                prv, cur, nxt = _neighbour_chunks(u_ref, up_ref, un_ref, r0, t, cols, i == 0, i == n_tiles - 1)
                tot = cur
                for o in range(-(w // 2), w // 2):
                    if o != 0:
                        tot = tot + _shift_rows(prv, cur, nxt, o, rows)
                cnt = _window_counts(r0, i, t, s, w // 2, (SUBLANES, pg))
                d_s[rs, cols] = tot / cnt - cur
            gate, _ = _gelu_and_grad(gate_ref[rs, :])
            yr_s[rs, :] = (hf_ref[rs, :] + hb_ref[rs, :]) * gate

        _chunk_loop(t, chunk)
        y_ref[:, pl.ds(p, c)] = yr_s[...].astype(BF16)
        d_ref[...] = d_s[...].astype(BF16)
        for g in range(N_POOL_GROUPS):
            cols = pl.ds(g * pg, pg)
            out = jnp.dot(d_s[:, cols].astype(BF16), wp_ref[g], preferred_element_type=F32)
            y_ref[:, cols] = (out * sc_ref[:, cols]).astype(BF16)

    return pl.pallas_call(
        body, name="pool_combine", grid=(n_tiles,),
        in_specs=_halo_specs(t, s, p, 0) + [
            pl.BlockSpec((t, c), lambda i: (i, 2)),
            pl.BlockSpec((t, c), lambda i: (i, 0)), pl.BlockSpec((t, c), lambda i: (i, 0)),
            pl.BlockSpec((N_POOL_GROUPS, pg, pg), lambda i: (0, 0, 0)), pl.BlockSpec((1, p), lambda i: (0, 0))],
        out_specs=[pl.BlockSpec((t, p + c), lambda i: (i, 0)), pl.BlockSpec((t, p), lambda i: (i, 0))],
        out_shape=[jax.ShapeDtypeStruct((s, p + c), BF16), jax.ShapeDtypeStruct((s, p), BF16)],
        scratch_shapes=[pltpu.VMEM((t, p), F32), pltpu.VMEM((t, c), F32)],
        compiler_params=_params("arbitrary"),
    )(proj, proj, proj, proj, h_f, h_b, w_pool, pool_scale)


def _layer_norm_rows(z, g, b):
    mu = jnp.mean(z, axis=-1, keepdims=True)
    zc = z - mu
    var = jnp.mean(zc * zc, axis=-1, keepdims=True)
    rstd = lax.rsqrt(var + LN_EPS)
    xh = zc * rstd
    return xh, rstd, xh * g + b


def _layer_norm_bwd_rows(dx, xh, rstd, g):
    dxh = dx * g
    m1 = jnp.mean(dxh, axis=-1, keepdims=True)
    m2 = jnp.mean(dxh * xh, axis=-1, keepdims=True)
    return rstd * (dxh - m1 - xh * m2)


def _out_ln1_call(y, w_out, x, g1, b1):
    s, d = x.shape
    tm = min(SEQ_TILE, s)

    def body(y_ref, w_ref, x_ref, g_ref, b_ref, xh_ref, x1b_ref, rstd_ref, acc_s, x1_s):
        acc_s[...] = jnp.dot(y_ref[...], w_ref[...], preferred_element_type=F32)
        g, b = g_ref[...], b_ref[...]

        def chunk(r0, _):
            rs = pl.ds(r0, SUBLANES)
            xh, rstd, x1 = _layer_norm_rows(ALPHA * x_ref[rs, :] + acc_s[rs, :], g, b)
            xh_ref[rs, :] = xh
            x1_s[rs, :] = x1
            rstd_ref[rs, :] = rstd

        _chunk_loop(tm, chunk, unroll=LN_UNROLL)
        x1b_ref[...] = x1_s[...].astype(BF16)

    return pl.pallas_call(
        body, name="out_ln1", grid=(s // tm,),
        in_specs=[pl.BlockSpec((tm, d), lambda i: (i, 0)), pl.BlockSpec((d, d), lambda i: (0, 0)),
                  pl.BlockSpec((tm, d), lambda i: (i, 0)),
                  pl.BlockSpec((1, d), lambda i: (0, 0)), pl.BlockSpec((1, d), lambda i: (0, 0))],
        out_specs=[pl.BlockSpec((tm, d), lambda i: (i, 0)), pl.BlockSpec((tm, d), lambda i: (i, 0)),
                   pl.BlockSpec((tm, 1), lambda i: (i, 0))],
        out_shape=[jax.ShapeDtypeStruct((s, d), F32), jax.ShapeDtypeStruct((s, d), BF16), jax.ShapeDtypeStruct((s, 1), F32)],
        scratch_shapes=[pltpu.VMEM((tm, d), F32), pltpu.VMEM((tm, d), F32)],
        compiler_params=_params("arbitrary"),
    )(y, w_out, x, g1, b1)


def _mlp_in_call(x1b, w1):
    s, d = x1b.shape
    n, _, f4 = w1.shape
    tm = min(MM_TILE, s)
    tn = min(1024, f4)
    per = f4 // tn

    def body(x_ref, w_ref, r_ref, q_ref):
        r = jnp.maximum(jnp.dot(x_ref[...], w_ref[...], preferred_element_type=F32), 0.0)
        r_ref[...] = r.astype(BF16)
        q_ref[...] = (r * r).astype(BF16)

    return pl.pallas_call(
        body, name="mlp_in", grid=(n * per, s // tm),
        in_specs=[pl.BlockSpec((tm, d), lambda j, i: (i, 0)), pl.BlockSpec((None, d, tn), lambda j, i: (j // per, 0, j % per))],
        out_specs=[pl.BlockSpec((tm, tn), lambda j, i: (i, j)), pl.BlockSpec((tm, tn), lambda j, i: (i, j))],
        out_shape=[jax.ShapeDtypeStruct((s, n * f4), BF16), jax.ShapeDtypeStruct((s, n * f4), BF16)],
        compiler_params=_params("arbitrary", "arbitrary"),
    )(x1b, w1)


def _mlp_out_ln2_call(hsq, w2, xh1, g1, b1, g2, b2, target):
    s, f = hsq.shape
    d = w2.shape[1]
    tm = min(MM_TILE, s)
    tk = min(1024, f)
    nk = f // tk

    def body(h_ref, w_ref, xh1_ref, g1_ref, b1_ref, g2_ref, b2_ref, t_ref,
             dz_ref, dzb_ref, loss_ref, dg_ref, db_ref, acc_s):
        i, k = pl.program_id(0), pl.program_id(1)

        @pl.when((i == 0) & (k == 0))
        def _():
            loss_ref[...] = jnp.zeros_like(loss_ref)
            dg_ref[...] = jnp.zeros_like(dg_ref)
            db_ref[...] = jnp.zeros_like(db_ref)

        @pl.when(k == 0)
        def _():
            acc_s[...] = jnp.zeros_like(acc_s)

        acc_s[...] += jnp.dot(h_ref[...], w_ref[...], preferred_element_type=F32)

        @pl.when(k == nk - 1)
        def _():
            g1, b1, g2, b2 = g1_ref[...], b1_ref[...], g2_ref[...], b2_ref[...]

            def chunk(r0, _):
                rs = pl.ds(r0, SUBLANES)
                x1 = xh1_ref[rs, :] * g1 + b1
                xh2, rstd, x2 = _layer_norm_rows(ALPHA * x1 + acc_s[rs, :], g2, b2)
                diff = x2 - t_ref[rs, :]
                loss_ref[...] += diff * diff
                dx2 = diff * (1.0 / d)
                dg_ref[...] += dx2 * xh2
                db_ref[...] += dx2
                dz = _layer_norm_bwd_rows(dx2, xh2, rstd, g2)
                dz_ref[rs, :] = dz

            _chunk_loop(tm, chunk, unroll=LN_UNROLL)
            dzb_ref[...] = dz_ref[...].astype(BF16)

    row = lambda i, k: (i, 0)
    vec = lambda i, k: (0, 0)
    return pl.pallas_call(
        body, name="mlp_out_ln2", grid=(s // tm, nk),
        in_specs=[pl.BlockSpec((tm, tk), lambda i, k: (i, k)), pl.BlockSpec((tk, d), lambda i, k: (k, 0)),
                  pl.BlockSpec((tm, d), row), pl.BlockSpec((1, d), vec), pl.BlockSpec((1, d), vec),
                  pl.BlockSpec((1, d), vec), pl.BlockSpec((1, d), vec), pl.BlockSpec((tm, d), row)],
        out_specs=[pl.BlockSpec((tm, d), row), pl.BlockSpec((tm, d), row),
                   pl.BlockSpec((SUBLANES, d), vec), pl.BlockSpec((SUBLANES, d), vec), pl.BlockSpec((SUBLANES, d), vec)],
        out_shape=[jax.ShapeDtypeStruct((s, d), F32), jax.ShapeDtypeStruct((s, d), BF16),
                   jax.ShapeDtypeStruct((SUBLANES, d), F32), jax.ShapeDtypeStruct((SUBLANES, d), F32),
                   jax.ShapeDtypeStruct((SUBLANES, d), F32)],
        scratch_shapes=[pltpu.VMEM((tm, d), F32)],
        compiler_params=_params("arbitrary", "arbitrary"),
    )(hsq, w2, xh1, g1, b1, g2, b2, target)


def _weight_grad_call(a, b, tm, tn, out_shape, out_map, name, dep):
    s, m = a.shape
    n = b.shape[1]
    tk = min(1024, s)

    def body(a_ref, b_ref, dep_ref, o_ref):
        @pl.when(pl.program_id(2) == 0)
        def _():
            o_ref[...] = jnp.zeros_like(o_ref)

        o_ref[...] += lax.dot_general(a_ref[...], b_ref[...], (((0,), (0,)), ((), ())), preferred_element_type=F32)

    return pl.pallas_call(
        body, name=name, grid=(m // tm, n // tn, s // tk),
        in_specs=[pl.BlockSpec((tk, tm), lambda i, j, k: (k, i)), pl.BlockSpec((tk, tn), lambda i, j, k: (k, j)), ANY],
        out_specs=pl.BlockSpec((None, None, tm, tn), lambda i, j, k: out_map(i, j)),
        out_shape=jax.ShapeDtypeStruct(out_shape, F32),
        compiler_params=_params("arbitrary", "arbitrary", "arbitrary"),
    )(a, b, dep)


def _weight_grad_resident_call(a, b, tm, out_shape, out_map, name, dep):
    s, m = a.shape
    n = b.shape[1]

    def body(a_ref, b_ref, dep_ref, o_ref):
        o_ref[...] = lax.dot_general(a_ref[...], b_ref[...], (((0,), (0,)), ((), ())), preferred_element_type=F32)

    return pl.pallas_call(
        body, name=name, grid=(m // tm,),
        in_specs=[pl.BlockSpec((s, tm), lambda i: (0, i)), pl.BlockSpec((s, n), lambda i: (0, 0)), ANY],
        out_specs=pl.BlockSpec((None, None, tm, n), lambda i: out_map(i, 0)),
        out_shape=jax.ShapeDtypeStruct(out_shape, F32),
        compiler_params=_params("arbitrary"),
    )(a, b, dep)


def _row_sharded_grad(a, b, name, dep):
    s, m = a.shape
    n = b.shape[1]
    half_rows = m // (2 * N_CHIPS)
    tm = min(1024, half_rows)
    per = half_rows // tm
    out_shape = (2, N_CHIPS, half_rows, n)
    out_map = lambda i, j: ((i // per) % 2, i // (2 * per), i % per, j)
    if tm < 1024 and s * n * 2 <= RESIDENT_OPERAND_BYTES:
        return _weight_grad_resident_call(a, b, tm, out_shape, out_map, name, dep)
    return _weight_grad_call(a, b, tm, min(1024, n), out_shape, out_map, name, dep)


def _col_sharded_grad(a, b, name, dep):
    m, n = a.shape[1], b.shape[1]
    half_rows, shard_cols = m // 2, n // N_CHIPS
    tm = min(1024, half_rows)
    per_m = half_rows // tm
    tn = shard_cols if shard_cols % 1024 else 1024
    per_n = shard_cols // tn
    return _weight_grad_call(a, b, tm, tn, (2, N_CHIPS, half_rows, shard_cols),
                             lambda i, j: (i // per_m, j // per_n, i % per_m, j % per_n), name, dep)


def _dhsq_call(dzb, w2, r, dep):
    s, d = dzb.shape
    f = w2.shape[0]
    tm = min(MM_TILE, s)
    tn = min(1024, f)

    def body(dz_ref, w_ref, r_ref, dep_ref, o_ref):
        dh = lax.dot_general(dz_ref[...], w_ref[...], (((1,), (1,)), ((), ())), preferred_element_type=F32)
        o_ref[...] = (dh * (2.0 * r_ref[...].astype(F32))).astype(BF16)

    return pl.pallas_call(
        body, name="mlp_dpre", grid=(f // tn, s // tm),
        in_specs=[pl.BlockSpec((tm, d), lambda j, i: (i, 0)), pl.BlockSpec((tn, d), lambda j, i: (j, 0)),
                  pl.BlockSpec((tm, tn), lambda j, i: (i, j)), ANY],
        out_specs=pl.BlockSpec((tm, tn), lambda j, i: (i, j)),
        out_shape=jax.ShapeDtypeStruct((s, f), BF16),
        compiler_params=_params("arbitrary", "arbitrary"),
    )(dzb, w2, r, dep)


def _dx1_ln1_bwd_call(dpre, w1, dz2, xh1, rstd1, g1, dep):
    s, f = dpre.shape
    n, d, f4 = w1.shape
    tm = min(MM_TILE, s)
    tk = min(1024, f4)
    per = f4 // tk
    nk = n * per

    def body(dp_ref, w_ref, dz2_ref, xh_ref, rstd_ref, g_ref, dep_ref, dz_ref, dzb_ref, dg_ref, db_ref, acc_s):
        i, k = pl.program_id(0), pl.program_id(1)

        @pl.when((i == 0) & (k == 0))
        def _():
            dg_ref[...] = jnp.zeros_like(dg_ref)
            db_ref[...] = jnp.zeros_like(db_ref)

        @pl.when(k == 0)
        def _():
            acc_s[...] = jnp.zeros_like(acc_s)

        acc_s[...] += lax.dot_general(dp_ref[...], w_ref[...], (((1,), (1,)), ((), ())), preferred_element_type=F32)

        @pl.when(k == nk - 1)
        def _():
            g = g_ref[...]

            def chunk(r0, _):
                rs = pl.ds(r0, SUBLANES)
                dx1 = acc_s[rs, :] + ALPHA * dz2_ref[rs, :]
                xh = xh_ref[rs, :]
                dg_ref[...] += dx1 * xh
                db_ref[...] += dx1
                dz = _layer_norm_bwd_rows(dx1, xh, rstd_ref[rs, :], g)
                dz_ref[rs, :] = dz

            _chunk_loop(tm, chunk, unroll=LN_UNROLL)
            dzb_ref[...] = dz_ref[...].astype(BF16)

    row = lambda i, k: (i, 0)
    vec = lambda i, k: (0, 0)
    return pl.pallas_call(
        body, name="dx1_ln1_bwd", grid=(s // tm, nk),
        in_specs=[pl.BlockSpec((tm, tk), lambda i, k: (i, k)),
                  pl.BlockSpec((None, d, tk), lambda i, k: (k // per, 0, k % per)),
                  pl.BlockSpec((tm, d), row), pl.BlockSpec((tm, d), row), pl.BlockSpec((tm, 1), row),
                  pl.BlockSpec((1, d), vec), ANY],
        out_specs=[pl.BlockSpec((tm, d), row), pl.BlockSpec((tm, d), row),
                   pl.BlockSpec((SUBLANES, d), vec), pl.BlockSpec((SUBLANES, d), vec)],
        out_shape=[jax.ShapeDtypeStruct((s, d), F32), jax.ShapeDtypeStruct((s, d), BF16),
                   jax.ShapeDtypeStruct((SUBLANES, d), F32), jax.ShapeDtypeStruct((SUBLANES, d), F32)],
        scratch_shapes=[pltpu.VMEM((tm, d), F32)],
        compiler_params=_params("arbitrary", "arbitrary"),
    )(dpre, w1, dz2, xh1, rstd1, g1, dep)


def _dy_call(dzb, w_out, dep):
    s, d = dzb.shape
    e = w_out.shape[0]
    tm = min(MM_TILE, s)

    def body(dz_ref, w_ref, dep_ref, o_ref):
        o_ref[...] = lax.dot_general(dz_ref[...], w_ref[...], (((1,), (1,)), ((), ())), preferred_element_type=F32)

    return pl.pallas_call(
        body, name="dy", grid=(s // tm,),
        in_specs=[pl.BlockSpec((tm, d), lambda i: (i, 0)), pl.BlockSpec((e, d), lambda i: (0, 0)), ANY],
        out_specs=pl.BlockSpec((tm, e), lambda i: (i, 0)),
        out_shape=jax.ShapeDtypeStruct((s, e), F32),
        compiler_params=_params("arbitrary"),
    )(dzb, w_out, dep)


def _mixer_bwd_call(dy, d_pool, proj, h_f, h_b, w_pool, pool_scale, p):
    s = dy.shape[0]
    c = h_f.shape[1]
    pg = p // N_POOL_GROUPS
    t = min(SEQ_TILE, s)
    n_tiles = s // t

    def body(dyp_ref, dyr_ref, d_ref, gate_ref, hf_ref, hb_ref, wp_ref, sc_ref,
             e_ref, dh_ref, dgate_ref, dwp_ref, dsc_ref, dd_s):
        i = pl.program_id(0)

        @pl.when(i == 0)
        def _():
            dwp_ref[...] = jnp.zeros_like(dwp_ref)
            dsc_ref[...] = jnp.zeros_like(dsc_ref)

        for g in range(N_POOL_GROUPS):
            cols = pl.ds(g * pg, pg)
            dg = d_ref[:, cols]
            out = jnp.dot(dg, wp_ref[g], preferred_element_type=F32)
            dyp = dyp_ref[:, cols]
            prod = dyp * out
            dsc_ref[:, cols] += jnp.sum(prod.reshape(t // SUBLANES, SUBLANES, pg), axis=0)
            dout = (dyp * sc_ref[:, cols]).astype(BF16)
            dwp_ref[g] += lax.dot_general(dg, dout, (((0,), (0,)), ((), ())), preferred_element_type=F32)
            dd_s[:, cols] = lax.dot_general(dout, wp_ref[g], (((1,), (1,)), ((), ())), preferred_element_type=F32)

        def chunk(r0, _):
            rs = pl.ds(r0, SUBLANES)
            for g, w in enumerate(POOL_WINDOWS):
                cols = pl.ds(g * pg, pg)
                cnt = _window_counts(r0, i, t, s, w // 2, (SUBLANES, pg))
                e_ref[rs, cols] = dd_s[rs, cols] / cnt
            gate, dgate = _gelu_and_grad(gate_ref[rs, :])
            dyr = dyr_ref[rs, :]
            dh_ref[rs, :] = dyr * gate
            dd_s[rs, :] = dyr * (hf_ref[rs, :] + hb_ref[rs, :]) * dgate

        _chunk_loop(t, chunk)
        dgate_ref[...] = dd_s[...].astype(BF16)

    tile = lambda i: (i, 0)
    return pl.pallas_call(
        body, name="mixer_bwd", grid=(n_tiles,),
        in_specs=[pl.BlockSpec((t, p), tile), pl.BlockSpec((t, c), lambda i: (i, 1)), pl.BlockSpec((t, p), tile),
                  pl.BlockSpec((t, c), lambda i: (i, 2)), pl.BlockSpec((t, c), tile), pl.BlockSpec((t, c), tile),
                  pl.BlockSpec((N_POOL_GROUPS, pg, pg), lambda i: (0, 0, 0)), pl.BlockSpec((1, p), lambda i: (0, 0))],
        out_specs=[pl.BlockSpec((t, p), tile), pl.BlockSpec((t, c), tile), pl.BlockSpec((t, c), tile),
                   pl.BlockSpec((N_POOL_GROUPS, pg, pg), lambda i: (0, 0, 0)), pl.BlockSpec((SUBLANES, p), lambda i: (0, 0))],
        out_shape=[jax.ShapeDtypeStruct((s, p), F32), jax.ShapeDtypeStruct((s, c), F32), jax.ShapeDtypeStruct((s, c), BF16),
                   jax.ShapeDtypeStruct((N_POOL_GROUPS, pg, pg), F32), jax.ShapeDtypeStruct((SUBLANES, p), F32)],
        scratch_shapes=[pltpu.VMEM((t, p), F32)],
        compiler_params=_params("arbitrary"),
    )(dy, dy, d_pool, proj, h_f, h_b, w_pool, pool_scale)


def _scan_bwd_call(xc, dh, h_dir, dxc_prev, wa, wi, ba, bi, lam, reverse, dep):
    s, c = xc.shape
    heads = c // LRU_HEAD
    t = min(SEQ_TILE, s)
    n_tiles = s // t
    per = t // SUBLANES
    last_blk = s // SUBLANES - 1
    tile = (lambda i: (i, 0)) if reverse else (lambda i: (n_tiles - 1 - i, 0))
    if reverse:
        halo = lambda i: (jnp.minimum((i + 1) * per, last_blk), 0)
    else:
        halo = lambda i: (jnp.maximum((n_tiles - 1 - i) * per - 1, 0), 0)
    whole2 = lambda i: (0, 0)
    whole3 = lambda i: (0, 0, 0)
    has_prev = dxc_prev is not None

    def body(*refs):
        (xc_ref, dh_ref, h_ref, hh_ref) = refs[:4]
        k = 4
        prev_ref = None
        if has_prev:
            prev_ref = refs[k]
            k += 1
        wa_ref, wi_ref, ba_ref, bi_ref, lam_ref = refs[k:k + 5]
        k += 1
        dxc_ref, dwa_ref, dwi_ref, dba_ref, dbi_ref, dsp_ref = refs[k + 5:k + 11]
        pr_s, pi_s, g_s, carry_s = refs[k + 11:]
        step = pl.program_id(0)
        tile_idx = step if reverse else n_tiles - 1 - step

        @pl.when(step == 0)
        def _():
            carry_s[...] = jnp.zeros_like(carry_s)
            dwa_ref[...] = jnp.zeros_like(dwa_ref)
            dwi_ref[...] = jnp.zeros_like(dwi_ref)
            dba_ref[...] = jnp.zeros_like(dba_ref)
            dbi_ref[...] = jnp.zeros_like(dbi_ref)
            dsp_ref[...] = jnp.zeros_like(dsp_ref)

        _gate_matmuls(xc_ref, wa_ref, wi_ref, pr_s, pi_s, heads)
        ba_v, bi_v = ba_ref[...], bi_ref[...]
        sp = _softplus(-lam_ref[...])
        rows = lax.broadcasted_iota(jnp.int32, (SUBLANES, c), 0)

        def gates(r0, _):
            rs = pl.ds(r0, SUBLANES)
            _, _, a, _ = _rg_gates(pr_s[rs, :], pi_s[rs, :], ba_v, bi_v, sp)
            g_s[rs, :] = a

        _chunk_loop(t, gates)

        def row(j, carry):
            r = j if reverse else (t - 1 - j)
            gt = dh_ref[pl.ds(r, 1), :] + carry
            carry = g_s[pl.ds(r, 1), :] * gt
            g_s[pl.ds(r, 1), :] = gt
            return carry

        carry_s[...] = lax.fori_loop(0, t, row, carry_s[...], unroll=8)

        def chunk(r0, _):
            rs = pl.ds(r0, SUBLANES)
            xcv = xc_ref[rs, :]
            r, ig, a, mult = _rg_gates(pr_s[rs, :], pi_s[rs, :], ba_v, bi_v, sp)
            gt = g_s[rs, :]
            cur = h_ref[rs, :]
            if reverse:
                after = h_ref[pl.ds(pl.multiple_of(jnp.minimum(r0 + SUBLANES, t - SUBLANES), SUBLANES), SUBLANES), :]
                edge = jnp.where(tile_idx == n_tiles - 1, 0.0, hh_ref[...])
                nxt = jnp.where(r0 == t - SUBLANES, edge, after)
                hs = _shift_rows(cur, cur, nxt, 1, rows)
            else:
                before = h_ref[pl.ds(pl.multiple_of(jnp.maximum(r0 - SUBLANES, 0), SUBLANES), SUBLANES), :]
                edge = jnp.where(tile_idx == 0, 0.0, hh_ref[...])
                prv = jnp.where(r0 == 0, edge, before)
                hs = _shift_rows(prv, cur, cur, -1, rows)
            gx = gt * xcv
            dmult = gx * ig
            di = gx * mult
            dlog_a = (gt * hs) * a - dmult * (a * a) / mult
            dr = dlog_a * (-RG_C * sp)
            dsp_ref[...] += dlog_a * (-RG_C * r)
            dpr = dr * r * (1.0 - r)
            dpi = di * ig * (1.0 - ig)
            dba_ref[...] += dpr
            dbi_ref[...] += dpi
            direct = gt * mult * ig
            if has_prev:
                direct = direct + prev_ref[rs, :]
            dxc_ref[rs, :] = direct
            pr_s[rs, :] = dpr
            pi_s[rs, :] = dpi

        _chunk_loop(t, chunk)

        for h in range(heads):
            cs = pl.ds(h * LRU_HEAD, LRU_HEAD)
            xb = xc_ref[:, cs].astype(BF16)
            dprb = pr_s[:, cs].astype(BF16)
            dpib = pi_s[:, cs].astype(BF16)
            dwa_ref[h] += lax.dot_general(xb, dprb, (((0,), (0,)), ((), ())), preferred_element_type=F32)
            dwi_ref[h] += lax.dot_general(xb, dpib, (((0,), (0,)), ((), ())), preferred_element_type=F32)
            dxc_ref[:, cs] += (
                lax.dot_general(dprb, wa_ref[h], (((1,), (1,)), ((), ())), preferred_element_type=F32)
                + lax.dot_general(dpib, wi_ref[h], (((1,), (1,)), ((), ())), preferred_element_type=F32))

    in_specs = [pl.BlockSpec((t, c), tile), pl.BlockSpec((t, c), tile), pl.BlockSpec((t, c), tile),
                pl.BlockSpec((SUBLANES, c), halo)]
    args = [xc, dh, h_dir, h_dir]
    if has_prev:
        in_specs.append(pl.BlockSpec((t, c), tile))
        args.append(dxc_prev)
    in_specs += [pl.BlockSpec((heads, LRU_HEAD, LRU_HEAD), whole3), pl.BlockSpec((heads, LRU_HEAD, LRU_HEAD), whole3),
                 pl.BlockSpec((1, c), whole2), pl.BlockSpec((1, c), whole2), pl.BlockSpec((1, c), whole2)]
    in_specs.append(ANY)
    args += [wa, wi, ba, bi, lam, dep]
    return pl.pallas_call(
        body, name="scan_bwd_rev" if reverse else "scan_bwd", grid=(n_tiles,),
        in_specs=in_specs,
        out_specs=[pl.BlockSpec((t, c), tile),
                   pl.BlockSpec((heads, LRU_HEAD, LRU_HEAD), whole3), pl.BlockSpec((heads, LRU_HEAD, LRU_HEAD), whole3),
                   pl.BlockSpec((SUBLANES, c), whole2), pl.BlockSpec((SUBLANES, c), whole2), pl.BlockSpec((SUBLANES, c), whole2)],
        out_shape=[jax.ShapeDtypeStruct((s, c), F32),
                   jax.ShapeDtypeStruct((heads, LRU_HEAD, LRU_HEAD), F32), jax.ShapeDtypeStruct((heads, LRU_HEAD, LRU_HEAD), F32),
                   jax.ShapeDtypeStruct((SUBLANES, c), F32), jax.ShapeDtypeStruct((SUBLANES, c), F32),
                   jax.ShapeDtypeStruct((SUBLANES, c), F32)],
        scratch_shapes=[pltpu.VMEM((t, c), F32), pltpu.VMEM((t, c), F32), pltpu.VMEM((t, c), F32), pltpu.VMEM((1, c), F32)],
        compiler_params=_params("arbitrary"),
    )(*args)


def _dproj_call(e_pool, dxc, proj, dgate, conv_w, p):
    s, c = dxc.shape
    pg = p // N_POOL_GROUPS
    t = min(SEQ_TILE, s)
    n_tiles = s // t

    def body(e_ref, ep_ref, en_ref, dx_ref, dxp_ref, dxn_ref, u_ref, up_ref, un_ref, dgate_ref, w_ref,
             dproj_ref, dcw_ref, dcb_ref, st_s):
        i = pl.program_id(0)
        first, last = i == 0, i == n_tiles - 1

        @pl.when(first)
        def _():
            dcw_ref[...] = jnp.zeros_like(dcw_ref)
            dcb_ref[...] = jnp.zeros_like(dcb_ref)

        rows_p = lax.broadcasted_iota(jnp.int32, (SUBLANES, pg), 0)
        rows_c = lax.broadcasted_iota(jnp.int32, (SUBLANES, c), 0)
        w = w_ref[...]

        def chunk(r0, _):
            rs = pl.ds(r0, SUBLANES)
            for g, win in enumerate(POOL_WINDOWS):
                cols = pl.ds(g * pg, pg)
                prv, cur, nxt = _neighbour_chunks(e_ref, ep_ref, en_ref, r0, t, cols, first, last)
                tot = cur
                for o in range(-(win // 2) + 1, win // 2 + 1):
                    if o != 0:
                        tot = tot + _shift_rows(prv, cur, nxt, o, rows_p)
                cnt = _window_counts(r0, i, t, s, win // 2, (SUBLANES, pg))
                st_s[rs, cols] = tot - cur * cnt
            prv, cur, nxt = _neighbour_chunks(dx_ref, dxp_ref, dxn_ref, r0, t, slice(None), first, last)
            du = w[1:2] * cur
            du += w[0:1] * _shift_rows(prv, cur, nxt, 1, rows_c)
            du += w[2:3] * _shift_rows(prv, cur, nxt, -1, rows_c)
            du += w[3:4] * _shift_rows(prv, cur, nxt, -2, rows_c)
            st_s[rs, pl.ds(p, c)] = du
            uprv, ucur, unxt = _neighbour_chunks(u_ref, up_ref, un_ref, r0, t, slice(None), first, last)
            dcb_ref[...] += cur
            for j, o in enumerate((-1, 0, 1, 2)):
                dcw_ref[j] += cur * _shift_rows(uprv, ucur, unxt, o, rows_c)

        _chunk_loop(t, chunk)
        dproj_ref[:, pl.ds(0, p + c)] = st_s[...].astype(BF16)
        dproj_ref[:, pl.ds(p + c, c)] = dgate_ref[...]

    return pl.pallas_call(
        body, name="dproj", grid=(n_tiles,),
        in_specs=_halo_specs(t, s, p, 0) + _halo_specs(t, s, c, 0) + _halo_specs(t, s, c, 1) + [
            pl.BlockSpec((t, c), lambda i: (i, 0)), pl.BlockSpec((4, c), lambda i: (0, 0))],
        out_specs=[pl.BlockSpec((t, p + 2 * c), lambda i: (i, 0)),
                   pl.BlockSpec((4, SUBLANES, c), lambda i: (0, 0, 0)), pl.BlockSpec((SUBLANES, c), lambda i: (0, 0))],
        out_shape=[jax.ShapeDtypeStruct((s, p + 2 * c), BF16), jax.ShapeDtypeStruct((4, SUBLANES, c), F32),
                   jax.ShapeDtypeStruct((SUBLANES, c), F32)],
        scratch_shapes=[pltpu.VMEM((t, p + c), F32)],
        compiler_params=_params("arbitrary"),
    )(e_pool, e_pool, e_pool, dxc, dxc, dxc, proj, proj, proj, dgate, conv_w)


def _dx_call(dproj, w_in, dz1, dep):
    s, e = dproj.shape
    n, d, e4 = w_in.shape
    tm = min(MM_TILE, s)

    def body(dp_ref, w_ref, dz_ref, dep_ref, o_ref):
        k = pl.program_id(1)

        @pl.when(k == 0)
        def _():
            o_ref[...] = ALPHA * dz_ref[...]

        o_ref[...] += lax.dot_general(dp_ref[...], w_ref[...], (((1,), (1,)), ((), ())), preferred_element_type=F32)

    return pl.pallas_call(
        body, name="grad_x", grid=(s // tm, n),
        in_specs=[pl.BlockSpec((tm, e4), lambda i, k: (i, k)), pl.BlockSpec((None, d, e4), lambda i, k: (k, 0, 0)),
                  pl.BlockSpec((tm, d), lambda i, k: (i, 0)), ANY],
        out_specs=pl.BlockSpec((tm, d), lambda i, k: (i, 0)),
        out_shape=jax.ShapeDtypeStruct((s, d), F32),
        compiler_params=_params("arbitrary", "arbitrary"),
    )(dproj, w_in, dz1, dep)


def _row_tile(rows, cols, n_arrays):
    limit = max(SUBLANES, ELT_BLOCK_BYTES // (4 * cols * max(1, n_arrays // 4)))
    best = SUBLANES
    for cand in range(SUBLANES, min(rows, limit) + 1, SUBLANES):
        if rows % cand == 0:
            best = cand
    return best if rows % SUBLANES == 0 else rows


def _cast_to_slot_call(a, idx, dtype, name):
    rows, cols = a.shape
    tr = _row_tile(rows, cols, 2)

    def body(idx_ref, a_ref, o_ref):
        o_ref[...] = a_ref[...].astype(dtype)

    return pl.pallas_call(
        body, name=name,
        grid_spec=pltpu.PrefetchScalarGridSpec(
            num_scalar_prefetch=1, grid=(rows // tr,),
            in_specs=[pl.BlockSpec((tr, cols), lambda i, idx_ref: (i, 0))],
            out_specs=pl.BlockSpec((None, tr, cols), lambda i, idx_ref: (idx_ref[1], i, 0))),
        out_shape=jax.ShapeDtypeStruct((N_CHIPS, rows, cols), dtype),
        compiler_params=_params("arbitrary"),
    )(idx, a)


def _add_half_call(g, recv, idx, to_slot, name):
    _, rows, cols = g.shape
    tr = _row_tile(rows, cols, 3)

    def body(idx_ref, g_ref, r_ref, o_ref):
        o_ref[...] = g_ref[...] + r_ref[...]

    if to_slot:
        out_spec = pl.BlockSpec((None, tr, cols), lambda i, idx_ref: (idx_ref[1], i, 0))
        out_shape = jax.ShapeDtypeStruct((N_CHIPS, rows, cols), F32)
    else:
        out_spec = pl.BlockSpec((tr, cols), lambda i, idx_ref: (i, 0))
        out_shape = jax.ShapeDtypeStruct((rows, cols), F32)
    return pl.pallas_call(
        body, name=name,
        grid_spec=pltpu.PrefetchScalarGridSpec(
            num_scalar_prefetch=1, grid=(rows // tr,),
            in_specs=[pl.BlockSpec((None, tr, cols), lambda i, idx_ref: (idx_ref[0], i, 0)),
                      pl.BlockSpec((tr, cols), lambda i, idx_ref: (i, 0))],
            out_specs=out_spec),
        out_shape=out_shape,
        compiler_params=_params("arbitrary"),
    )(idx, g, recv)


def _sum_chips_call(own, recv, idx, name):
    _, rows, cols = recv.shape
    tr = _row_tile(rows, cols, 5)
    out_spec = pl.BlockSpec((None, tr, cols), lambda i, idx_ref: (idx_ref[0], i, 0))
    if own is None:
        def body(idx_ref, r_ref, o_ref):
            o_ref[...] = ((r_ref[0] + r_ref[1]) + r_ref[2]) + r_ref[3]
        in_specs = [pl.BlockSpec((N_CHIPS, tr, cols), lambda i, idx_ref: (0, i, 0))]
        args = (recv,)
    else:
        def body(idx_ref, p_ref, r_ref, o_ref):
            o_ref[...] = ((p_ref[...] + r_ref[0]) + r_ref[1]) + r_ref[2]
        in_specs = [pl.BlockSpec((None, tr, cols), lambda i, idx_ref: (idx_ref[1], i, 0)),
                    pl.BlockSpec((N_CHIPS - 1, tr, cols), lambda i, idx_ref: (0, i, 0))]
        args = (own, recv)
    return pl.pallas_call(
        body, name=name,
        grid_spec=pltpu.PrefetchScalarGridSpec(num_scalar_prefetch=1, grid=(rows // tr,), in_specs=in_specs, out_specs=out_spec),
        out_shape=jax.ShapeDtypeStruct((2, rows, cols), F32),
        compiler_params=_params("arbitrary"),
    )(idx, *args)


def _adamw_call(g, w, m, v, name):
    rows, cols = w.shape
    tr = _row_tile(rows, cols, 7)

    def body(g_ref, w_ref, m_ref, v_ref, d_ref, mo_ref, vo_ref):
        gv = g_ref[...]
        mn = ADAM_B1 * m_ref[...] + (1.0 - ADAM_B1) * gv
        vn = ADAM_B2 * v_ref[...] + (1.0 - ADAM_B2) * (gv * gv)
        m_hat = mn / (1.0 - ADAM_B1 ** ADAM_STEP)
        v_hat = vn / (1.0 - ADAM_B2 ** ADAM_STEP)
        d_ref[...] = -ADAM_LR * (m_hat / (jnp.sqrt(v_hat) + ADAM_EPS) + ADAM_WD * w_ref[...])
        mo_ref[...] = mn
        vo_ref[...] = vn

    spec = pl.BlockSpec((tr, cols), lambda i: (i, 0))
    shape = jax.ShapeDtypeStruct((rows, cols), F32)
    return pl.pallas_call(
        body, name=name, grid=(rows // tr,),
        in_specs=[spec] * 4, out_specs=[spec] * 3, out_shape=[shape] * 3,
        compiler_params=_params("arbitrary"),
    )(g, w, m, v)


def _mesh_place():
    x, y, c = lax.axis_index("x"), lax.axis_index("y"), lax.axis_index("c")
    chips = [(1 - x, y), (x, 1 - y), (1 - x, 1 - y)]
    return x, y, c, chips


def _remote(src, dst, send_sems, recv_sems, idx, device):
    return pltpu.make_async_remote_copy(src_ref=src, dst_ref=dst, send_sem=send_sems.at[idx], recv_sem=recv_sems.at[idx],
                                        device_id=device, device_id_type=MESH)


HBM_SPEC = pl.BlockSpec(memory_space=pltpu.HBM)
SEM_SPEC = pl.BlockSpec(memory_space=pltpu.SEMAPHORE)
ORDERED_EFFECT = pltpu.SideEffectType.DATAFLOW_SIDE_EFFECTING


def _in_hbm(a):
    return pltpu.with_memory_space_constraint(a, pltpu.HBM)


def _start_copies_call(name, bufs, groups):
    n, g = len(bufs), len(groups)

    def body(*refs):
        outs = refs[n:2 * n]
        sems = refs[2 * n:2 * n + 2 * g]
        token = refs[2 * n + 2 * g]
        for i, (which, copies_fn, _) in enumerate(groups):
            for mine, _ in copies_fn([outs[w] for w in which], sems[2 * i], sems[2 * i + 1]):
                mine.start()
        token[...] = jnp.zeros_like(token)

    sem_shapes = [pltpu.SemaphoreType.DMA((cnt,)) for _, _, cnt in groups for _ in range(2)]
    res = pl.pallas_call(
        body, name=name,
        in_specs=[HBM_SPEC] * n,
        out_specs=[HBM_SPEC] * n + [SEM_SPEC] * (2 * g) + [pl.BlockSpec(memory_space=pltpu.VMEM)],
        out_shape=[pltpu.HBM(a.shape, a.dtype) for a in bufs] + sem_shapes + [jax.ShapeDtypeStruct((SUBLANES, LANES), F32)],
        input_output_aliases={a: a for a in range(n)},
        compiler_params=pltpu.CompilerParams(has_side_effects=ORDERED_EFFECT),
    )(*[_in_hbm(a) for a in bufs])
    sems = res[n:n + 2 * g]
    return list(res[:n]), [(sems[2 * i], sems[2 * i + 1]) for i in range(g)], res[n + 2 * g]


def _wait_copies_call(name, bufs, sems, copies_fn, after):
    n = len(bufs)

    def body(*refs):
        ins = refs[:n]
        send_sems, recv_sems = refs[n], refs[n + 1]
        for mine, arriving in copies_fn(list(ins), send_sems, recv_sems):
            arriving.wait_recv()
            mine.wait_send()

    res = pl.pallas_call(
        body, name=name,
        in_specs=[HBM_SPEC] * n + [SEM_SPEC, SEM_SPEC, ANY],
        out_specs=[HBM_SPEC] * n,
        out_shape=[pltpu.HBM(a.shape, a.dtype) for a in bufs],
        input_output_aliases={a: a for a in range(n)},
        compiler_params=pltpu.CompilerParams(has_side_effects=ORDERED_EFFECT),
    )(*bufs, sems[0], sems[1], after)
    return list(res)


def _gather_copies(bufs, send_sems, recv_sems):
    x, y, c, chips = _mesh_place()
    k = 2 * x + y
    out = []
    for a, buf in enumerate(bufs):
        for j, (px, py) in enumerate(chips):
            kj = 2 * px + py
            mine = _remote(buf.at[k, c], buf.at[k, c], send_sems, recv_sems, 3 * a + j, (px, py, c))
            arriving = _remote(buf.at[k, c], buf.at[kj, c], send_sems, recv_sems, 3 * a + j, (px, py, c))
            out.append((mine, arriving))
    return out


def _exchange_copies(n_sharded, n_replicated):
    def copies(bufs, send_sems, recv_sems):
        x, y, c, chips = _mesh_place()
        k = 2 * x + y
        sums, lands = bufs[:n_sharded], bufs[n_sharded:2 * n_sharded]
        repl = bufs[2 * n_sharded:]
        out = []
        for j, (px, py) in enumerate(chips):
            kj = 2 * px + py
            for a in range(n_sharded):
                cp = _remote(sums[a].at[kj], lands[a].at[j], send_sems, recv_sems, 3 * a + j, (px, py, c))
                out.append((cp, cp))
            for a in range(n_replicated):
                idx = 3 * (n_sharded + a) + j
                mine = _remote(repl[a].at[k], repl[a].at[k], send_sems, recv_sems, idx, (px, py, c))
                arriving = _remote(repl[a].at[k], repl[a].at[kj], send_sems, recv_sems, idx, (px, py, c))
                out.append((mine, arriving))
        return out
    return copies


def _sibling_copies(n):
    def copies(bufs, send_sems, recv_sems):
        x, y, c, _ = _mesh_place()
        out = []
        for a in range(n):
            cp = _remote(bufs[a].at[1 - c], bufs[n + a], send_sems, recv_sems, a, (x, y, 1 - c))
            out.append((cp, cp))
        return out
    return copies


def _forward_to_sibling_call(bufs, name):
    n = len(bufs)

    def body(*refs):
        ins, outs = refs[:n], refs[n:2 * n]
        send_sems, recv_sems = refs[2 * n:]
        x, y, c, chips = _mesh_place()
        sibling = (x, y, 1 - c)
        sends = []
        for a in range(n):
            for j, (px, py) in enumerate(chips):
                kj = 2 * px + py
                sends.append(_remote(ins[a].at[kj, c], outs[a].at[kj, c], send_sems, recv_sems, 3 * a + j, sibling))
        for cp in sends:
            cp.start()
        for a in range(n):
            for j, (px, py) in enumerate(chips):
                kj = 2 * px + py
                _remote(ins[a].at[kj, c], outs[a].at[kj, 1 - c], send_sems, recv_sems, 3 * a + j, sibling).wait_recv()
        for cp in sends:
            cp.wait_send()

    return pl.pallas_call(
        body, name=name,
        in_specs=[ANY] * n, out_specs=[ANY] * n,
        out_shape=[jax.ShapeDtypeStruct(a.shape, a.dtype) for a in bufs],
        input_output_aliases={a: a for a in range(n)},
        scratch_shapes=[pltpu.SemaphoreType.DMA((3 * n,)), pltpu.SemaphoreType.DMA((3 * n,))],
    )(*bufs)


def _join_halves_call(bufs, name):
    n = len(bufs)

    def body(*refs):
        ins, outs = refs[:n], refs[n:2 * n]
        send_sems, recv_sems = refs[2 * n:]
        x, y, c, _ = _mesh_place()
        sibling = (x, y, 1 - c)
        copies = [_remote(ins[a].at[c], outs[a].at[c], send_sems, recv_sems, a, sibling) for a in range(n)]
        for cp in copies:
            cp.start()
        for a in range(n):
            _remote(ins[a].at[c], outs[a].at[1 - c], send_sems, recv_sems, a, sibling).wait_recv()
        for cp in copies:
            cp.wait_send()

    return pl.pallas_call(
        body, name=name,
        in_specs=[ANY] * n, out_specs=[ANY] * n,
        out_shape=[jax.ShapeDtypeStruct(a.shape, a.dtype) for a in bufs],
        input_output_aliases={a: a for a in range(n)},
        scratch_shapes=[pltpu.SemaphoreType.DMA((n,)), pltpu.SemaphoreType.DMA((n,))],
    )(*bufs)


def _pack(arrays, rows_multiple):
    flat = jnp.concatenate([a.reshape(-1) for a in arrays])
    per = LANES * rows_multiple
    padded = -(-flat.shape[0] // per) * per
    flat = jnp.pad(flat, (0, padded - flat.shape[0]))
    return flat.reshape(-1, LANES)


def _unpack(packed, shapes):
    flat = packed.reshape(-1)
    out, at = [], 0
    for shp in shapes:
        size = 1
        for dim in shp:
            size *= dim
        out.append(flat[at:at + size].reshape(shp))
        at += size
    return out


def _halves(a):
    return a.reshape((2, a.shape[0] // 2) + a.shape[1:])


def kernel(x, ln_mix_g, ln_mix_b, w_in, w_pool, pool_scale, conv_w, conv_b, w_rg_a, b_rg_a, w_rg_i, b_rg_i, rg_lambda, w_out, ln_ffn_g, ln_ffn_b, w_mlp_in, w_mlp_out, loss_target, m_ln_mix_g, m_ln_mix_b, m_w_in, m_w_pool, m_pool_scale, m_conv_w, m_conv_b, m_w_rg_a, m_b_rg_a, m_w_rg_i, m_b_rg_i, m_rg_lambda, m_w_out, m_ln_ffn_g, m_ln_ffn_b, m_w_mlp_in, m_w_mlp_out, v_ln_mix_g, v_ln_mix_b, v_w_in, v_w_pool, v_pool_scale, v_conv_w, v_conv_b, v_w_rg_a, v_b_rg_a, v_w_rg_i, v_b_rg_i, v_rg_lambda, v_w_out, v_ln_ffn_g, v_ln_ffn_b, v_w_mlp_in, v_w_mlp_out):
    weights = dict(ln_mix_g=ln_mix_g, ln_mix_b=ln_mix_b, w_in=w_in, w_pool=w_pool, pool_scale=pool_scale, conv_w=conv_w,
                   conv_b=conv_b, w_rg_a=w_rg_a, b_rg_a=b_rg_a, w_rg_i=w_rg_i, b_rg_i=b_rg_i, rg_lambda=rg_lambda,
                   w_out=w_out, ln_ffn_g=ln_ffn_g, ln_ffn_b=ln_ffn_b, w_mlp_in=w_mlp_in, w_mlp_out=w_mlp_out)
    m_in = dict(ln_mix_g=m_ln_mix_g, ln_mix_b=m_ln_mix_b, w_in=m_w_in, w_pool=m_w_pool, pool_scale=m_pool_scale,
                conv_w=m_conv_w, conv_b=m_conv_b, w_rg_a=m_w_rg_a, b_rg_a=m_b_rg_a, w_rg_i=m_w_rg_i, b_rg_i=m_b_rg_i,
                rg_lambda=m_rg_lambda, w_out=m_w_out, ln_ffn_g=m_ln_ffn_g, ln_ffn_b=m_ln_ffn_b, w_mlp_in=m_w_mlp_in,
                w_mlp_out=m_w_mlp_out)
    v_in = dict(ln_mix_g=v_ln_mix_g, ln_mix_b=v_ln_mix_b, w_in=v_w_in, w_pool=v_w_pool, pool_scale=v_pool_scale,
                conv_w=v_conv_w, conv_b=v_conv_b, w_rg_a=v_w_rg_a, b_rg_a=v_b_rg_a, w_rg_i=v_w_rg_i, b_rg_i=v_b_rg_i,
                rg_lambda=v_rg_lambda, w_out=v_w_out, ln_ffn_g=v_ln_ffn_g, ln_ffn_b=v_ln_ffn_b, w_mlp_in=v_w_mlp_in,
                w_mlp_out=v_w_mlp_out)
    names = list(weights)

    xs = x[0]
    tgt = loss_target[0]
    s, d = xs.shape
    p = c = d // 2
    pg = p // N_POOL_GROUPS
    heads = c // LRU_HEAD
    core = lax.axis_index("c")
    shard = 2 * lax.axis_index("x") + lax.axis_index("y")

    idx = jnp.stack([core, shard]).astype(jnp.int32)
    small_shard = _pack([conv_w[0], b_rg_a[0], b_rg_i[0], rg_lambda[0]], 2 * SUBLANES)
    to_gather = [(w_in[0], BF16), (w_out[0], BF16), (w_mlp_in[0], BF16), (w_mlp_out[0], BF16),
                 (w_pool[0].reshape(-1, pg), BF16), (small_shard, F32)]
    slots = [_cast_to_slot_call(a, idx, dt, f"gather_slot_{i}") for i, (a, dt) in enumerate(to_gather)]
    views = [sl.reshape(N_CHIPS, 2, sl.shape[1] // 2, sl.shape[2]) for sl in slots]
    first = (0, 4, 5)
    in_flight, g_sems, g_token = _start_copies_call(
        "gather_start", views,
        [(first, _gather_copies, 3 * len(first)), ((1,), _gather_copies, 3), ((2,), _gather_copies, 3), ((3,), _gather_copies, 3)])

    def arrive(which, group, after, tag):
        got = _wait_copies_call(f"gather_wait_{tag}", [in_flight[w] for w in which], g_sems[group], _gather_copies, after)
        return _forward_to_sibling_call(got, f"gather_forward_{tag}")

    gathered = [None] * len(views)
    gathered[0], gathered[4], gathered[5] = arrive(first, 0, g_token, "w_in")
    w_in_f = gathered[0].reshape((N_CHIPS,) + w_in.shape[1:])
    w_pool_f = gathered[4].reshape(N_CHIPS, N_POOL_GROUPS, pg // N_CHIPS, pg).transpose(1, 0, 2, 3).reshape(N_POOL_GROUPS, pg, pg)
    c4 = c // N_CHIPS
    small_parts = [_unpack(gathered[5][k].reshape(-1, LANES), [(4, c4), (2, c4), (2, c4), (2, c4)]) for k in range(N_CHIPS)]
    conv_w_f = jnp.concatenate([sp_[0] for sp_ in small_parts], axis=1)
    b_a_f = jnp.concatenate([sp_[1] for sp_ in small_parts], axis=1)
    b_i_f = jnp.concatenate([sp_[2] for sp_ in small_parts], axis=1)
    lam_f = jnp.concatenate([sp_[3] for sp_ in small_parts], axis=1)
    wa_b = w_rg_a[0].astype(BF16)
    wi_b = w_rg_i[0].astype(BF16)

    proj, xb = _proj_call(xs, w_in_f)
    xc = _conv_call(proj, conv_w_f, conv_b, c)
    h_b = _scan_fwd_call(xc, wa_b[1], wi_b[1], b_a_f[1:2], b_i_f[1:2], lam_f[1:2], True)
    h_f = _scan_fwd_call(xc, wa_b[0], wi_b[0], b_a_f[0:1], b_i_f[0:1], lam_f[0:1], False)
    y, d_pool = _pool_combine_call(proj, h_f, h_b, w_pool_f, pool_scale, p)
    w_out_f = arrive((1,), 1, y, "w_out")[0].reshape(d, d)
    xh1, x1b, rstd1 = _out_ln1_call(y, w_out_f, xs, ln_mix_g, ln_mix_b)
    w1_f = arrive((2,), 2, x1b, "w_mlp_in")[0].reshape((N_CHIPS,) + w_mlp_in.shape[1:])
    r_act, hsq = _mlp_in_call(x1b, w1_f)
    w2_f = arrive((3,), 3, hsq, "w_mlp_out")[0].reshape(N_CHIPS * w_mlp_out.shape[1], d)
    dz2, dz2b, loss8, dg2, db2 = _mlp_out_ln2_call(hsq, w2_f, xh1, ln_mix_g, ln_mix_b, ln_ffn_g, ln_ffn_b, tgt)

    def start_siblings(grads, tag):
        lands = [lax.empty(g.shape[1:], g.dtype) for g in grads]
        copies = _sibling_copies(len(grads))
        flying, sems, token = _start_copies_call(
            f"siblings_start_{tag}", list(grads) + lands, [(tuple(range(2 * len(grads))), copies, len(grads))])
        return (flying, sems[0], copies, len(grads), tag), token

    def finish_siblings(state, after, small_at=None):
        flying, sems, copies, n, tag = state
        got = _wait_copies_call(f"siblings_wait_{tag}", flying, sems, copies, after)
        out = []
        for a in range(n):
            g, rv = got[a], got[n + a]
            cols = g.shape[-1]
            part = _add_half_call(g.reshape(2, -1, cols), rv.reshape(-1, cols), idx, a == small_at, f"reduce_add_{tag}_{a}")
            out.append(part if a == small_at else part.reshape((N_CHIPS,) + rv.shape[1:]))
        return out

    def start_exchange(sums, n_repl, tag):
        n_sh = len(sums) - n_repl
        lands = [lax.empty((N_CHIPS - 1,) + a.shape[1:], a.dtype) for a in sums[:n_sh]]
        bufs = sums[:n_sh] + lands + sums[n_sh:]
        copies = _exchange_copies(n_sh, n_repl)
        flying, sems, token = _start_copies_call(
            f"reduce_start_{tag}", bufs, [(tuple(range(len(bufs))), copies, 3 * len(sums))])
        return (flying, sems[0], copies, n_sh, tag), token

    def finish_exchange(state, after):
        flying, sems, copies, n_sh, tag = state
        got = _wait_copies_call(f"reduce_wait_{tag}", flying, sems, copies, after)
        halves = []
        for a in range(n_sh):
            own, land = got[a], got[n_sh + a]
            cols = own.shape[-1]
            total = _sum_chips_call(own.reshape(N_CHIPS, -1, cols), land.reshape(N_CHIPS - 1, -1, cols), idx,
                                    f"reduce_sum_{tag}_{a}")
            halves.append(total.reshape((2,) + own.shape[1:]))
        for a, rp in enumerate(got[2 * n_sh:]):
            halves.append(_sum_chips_call(None, rp, idx, f"reduce_sum_{tag}_r{a}"))
        return _join_halves_call(halves, f"reduce_join_{tag}")

    g_w2 = _row_sharded_grad(hsq, dz2b, "grad_w_mlp_out", g_token)
    sib_w2, token = start_siblings([g_w2], "w2")
    dpre = _dhsq_call(dz2b, w2_f, r_act, token)
    flying_w2, token = start_exchange(finish_siblings(sib_w2, dpre), 0, "w2")
    g_w1 = _col_sharded_grad(x1b, dpre, "grad_w_mlp_in", token)
    sib_w1, token = start_siblings([g_w1], "w1")
    dz1, dz1b, dg1, db1 = _dx1_ln1_bwd_call(dpre, w1_f, dz2, xh1, rstd1, ln_mix_g, token)
    flying_w1, token = start_exchange(finish_siblings(sib_w1, dz1b), 0, "w1")
    g_wout = _row_sharded_grad(y, dz1b, "grad_w_out", token)
    sib_wout, token = start_siblings([g_wout], "w_out")
    dy = _dy_call(dz1b, w_out_f, token)
    e_pool, dh, dgate, g_wpool, g_pscale8 = _mixer_bwd_call(dy, d_pool, proj, h_f, h_b, w_pool_f, pool_scale, p)
    flying_wout, token = start_exchange(finish_siblings(sib_wout, dgate), 0, "w_out")
    dxc0, g_wa0, g_wi0, g_ba0, g_bi0, g_sp0 = _scan_bwd_call(
        xc, dh, h_f, None, wa_b[0], wi_b[0], b_a_f[0:1], b_i_f[0:1], lam_f[0:1], False, token)
    dxc, g_wa1, g_wi1, g_ba1, g_bi1, g_sp1 = _scan_bwd_call(
        xc, dh, h_b, dxc0, wa_b[1], wi_b[1], b_a_f[1:2], b_i_f[1:2], lam_f[1:2], True, token)
    dproj, g_cw8, g_cb8 = _dproj_call(e_pool, dxc, proj, dgate, conv_w_f, p)
    g_win = _col_sharded_grad(xb, dproj, "grad_w_in", token)

    rowsum = lambda a8: jnp.sum(a8, axis=-2)
    g_lam = jnp.stack([rowsum(g_sp0), rowsum(g_sp1)]) * (-_sigmoid(-lam_f))
    small_grads = {
        "ln_mix_g": rowsum(dg1), "ln_mix_b": rowsum(db1), "ln_ffn_g": rowsum(dg2), "ln_ffn_b": rowsum(db2),
        "pool_scale": rowsum(g_pscale8), "conv_b": rowsum(g_cb8),
        "w_rg_a": jnp.stack([g_wa0, g_wa1]), "w_rg_i": jnp.stack([g_wi0, g_wi1]),
        "w_pool": g_wpool, "conv_w": rowsum(g_cw8),
        "b_rg_a": jnp.stack([rowsum(g_ba0), rowsum(g_ba1)]), "b_rg_i": jnp.stack([rowsum(g_bi0), rowsum(g_bi1)]),
        "rg_lambda": g_lam,
    }
    small_names = list(small_grads)
    small_shapes = [small_grads[nm].shape for nm in small_names]
    g_small = _halves(_pack([small_grads[nm] for nm in small_names], 2 * SUBLANES))
    sib_rest, token = start_siblings([g_win, g_small], "rest")
    grad_x = _dx_call(dproj, w_in_f, dz1, token)
    flying_rest, token = start_exchange(finish_siblings(sib_rest, grad_x, small_at=1), 1, "rest")

    grad_w, delta_w, new_m, new_v = {}, {}, {}, {}

    def adamw(nm, full):
        w2d = weights[nm][0]
        g2d = full.reshape(w2d.shape)
        dl, mn, vn = _adamw_call(g2d, w2d, m_in[nm][0], v_in[nm][0], f"adamw_{nm}")
        grad_w[nm], delta_w[nm], new_m[nm], new_v[nm] = g2d[None], dl[None], mn[None], vn[None]
        return vn

    adamw("w_mlp_out", finish_exchange(flying_w2, token)[0])
    adamw("w_mlp_in", finish_exchange(flying_w1, token)[0])
    last = adamw("w_out", finish_exchange(flying_wout, token)[0])
    joined = finish_exchange(flying_rest, last)
    adamw("w_in", joined[0])

    small_full = dict(zip(small_names, _unpack(joined[1].reshape(-1, LANES), small_shapes)))
    local = dict(small_full)
    local["w_pool"] = lax.dynamic_slice_in_dim(small_full["w_pool"], shard * (pg // N_CHIPS), pg // N_CHIPS, axis=1)
    for nm in ("conv_w", "b_rg_a", "b_rg_i", "rg_lambda"):
        local[nm] = lax.dynamic_slice_in_dim(small_full[nm], shard * c4, c4, axis=1)
    small_w_shapes = [weights[nm].shape for nm in small_names]
    g_pack = _pack([local[nm] for nm in small_names], SUBLANES)
    w_pack = _pack([weights[nm] for nm in small_names], SUBLANES)
    m_pack = _pack([m_in[nm] for nm in small_names], SUBLANES)
    v_pack = _pack([v_in[nm] for nm in small_names], SUBLANES)
    dl_p, mn_p, vn_p = _adamw_call(g_pack, w_pack, m_pack, v_pack, "adamw_small")
    for nm, gl, dl, mn, vn in zip(small_names, _unpack(g_pack, small_w_shapes), _unpack(dl_p, small_w_shapes),
                                  _unpack(mn_p, small_w_shapes), _unpack(vn_p, small_w_shapes)):
        grad_w[nm], delta_w[nm], new_m[nm], new_v[nm] = gl, dl, mn, vn

    loss = lax.psum(jnp.sum(loss8) * (0.5 / d), ("x", "y", "c"))
    return (loss, grad_x[None], *[grad_w[nm] for nm in names], *[delta_w[nm] for nm in names],
            *[new_m[nm] for nm in names], *[new_v[nm] for nm in names])
```

```python
import functools

import jax
import jax.numpy as jnp
from jax import lax
from jax.experimental import pallas as pl
from jax.experimental.pallas import tpu as pltpu

F32 = jnp.float32
BF16 = jnp.bfloat16

N_CHIPS = 4
LANES = 128
SUBLANES = 8
LRU_HEAD = 128
N_POOL_GROUPS = 4
POOL_WINDOWS = (2, 4, 8, 16)
RG_C = 8.0
LN_EPS = 1e-5
ALPHA = 2.0 ** 0.25
ADAM_LR, ADAM_B1, ADAM_B2, ADAM_EPS, ADAM_WD, ADAM_STEP = 0.001, 0.9, 0.999, 1e-08, 0.01, 10
VMEM_LIMIT = 56 * 1024 * 1024
SEQ_TILE = 256
MM_TILE = 512
LN_UNROLL = 4
ELT_BLOCK_BYTES = 2 * 1024 * 1024
RESIDENT_OPERAND_BYTES = 16 * 1024 * 1024
MESH = pl.DeviceIdType.MESH
ANY = pl.BlockSpec(memory_space=pl.ANY)


def _params(*sem):
    return pltpu.CompilerParams(dimension_semantics=sem, vmem_limit_bytes=VMEM_LIMIT)


def _sigmoid(z):
    return 1.0 / (1.0 + jnp.exp(-z))


def _neg_expm1(z):
    series = -(z * (1.0 + z * (0.5 + z * (1.0 / 6.0 + z * (1.0 / 24.0)))))
    return jnp.where(z > -0.01, series, 1.0 - jnp.exp(z))


def _softplus(z):
    return jnp.maximum(z, 0.0) + jnp.log1p(jnp.exp(-jnp.abs(z)))


_GELU_K = 0.7978845608028654
_GELU_C = 0.044715


def _gelu_and_grad(u):
    t = jnp.tanh(_GELU_K * (u + _GELU_C * (u * u * u)))
    g = 0.5 * u * (1.0 + t)
    dg = 0.5 * (1.0 + t) + 0.5 * u * (1.0 - t * t) * (_GELU_K * (1.0 + 3.0 * _GELU_C * u * u))
    return g, dg


def _shift_rows(prv, cur, nxt, o, rows):
    if o == 0:
        return cur
    if o == SUBLANES:
        return nxt
    if o == -SUBLANES:
        return prv
    if o > 0:
        s = SUBLANES - o
        return jnp.where(rows < s, pltpu.roll(cur, s, 0), pltpu.roll(nxt, s, 0))
    p = -o
    return jnp.where(rows >= p, pltpu.roll(cur, p, 0), pltpu.roll(prv, p, 0))


def _neighbour_chunks(main_ref, prev_ref, next_ref, r0, t_rows, cols, first_tile, last_tile):
    cur = main_ref[pl.ds(r0, SUBLANES), cols]
    before = main_ref[pl.ds(pl.multiple_of(jnp.maximum(r0 - SUBLANES, 0), SUBLANES), SUBLANES), cols]
    after = main_ref[pl.ds(pl.multiple_of(jnp.minimum(r0 + SUBLANES, t_rows - SUBLANES), SUBLANES), SUBLANES), cols]
    halo_prev = jnp.where(first_tile, 0.0, prev_ref[:, cols])
    halo_next = jnp.where(last_tile, 0.0, next_ref[:, cols])
    prv = jnp.where(r0 == 0, halo_prev, before)
    nxt = jnp.where(r0 == t_rows - SUBLANES, halo_next, after)
    return prv, cur, nxt


def _halo_specs(t_rows, n_rows, width, col_block):
    per = t_rows // SUBLANES
    last = n_rows // SUBLANES - 1
    return [
        pl.BlockSpec((t_rows, width), lambda i: (i, col_block)),
        pl.BlockSpec((SUBLANES, width), lambda i: (jnp.maximum(i * per - 1, 0), col_block)),
        pl.BlockSpec((SUBLANES, width), lambda i: (jnp.minimum((i + 1) * per, last), col_block)),
    ]


def _chunk_loop(t_rows, fn, init=None, unroll=1):
    span = SUBLANES * unroll

    def step(ci, carry):
        base = pl.multiple_of(ci * span, span)
        for u in range(unroll):
            carry = fn(base + u * SUBLANES, carry)
        return carry
    return lax.fori_loop(0, t_rows // span, step, init)


def _proj_call(x, w_in):
    s, d = x.shape
    n, _, e4 = w_in.shape
    tm = min(MM_TILE, s)

    def body(x_ref, w_ref, proj_ref, xb_ref):
        @pl.when(pl.program_id(1) == 0)
        def _():
            xb_ref[...] = x_ref[...].astype(BF16)

        proj_ref[...] = jnp.dot(xb_ref[...], w_ref[...], preferred_element_type=F32)

    return pl.pallas_call(
        body, name="proj", grid=(s // tm, n),
        in_specs=[pl.BlockSpec((tm, d), lambda i, j: (i, 0)), pl.BlockSpec((None, d, e4), lambda i, j: (j, 0, 0))],
        out_specs=[pl.BlockSpec((tm, e4), lambda i, j: (i, j)), pl.BlockSpec((tm, d), lambda i, j: (i, 0))],
        out_shape=[jax.ShapeDtypeStruct((s, n * e4), F32), jax.ShapeDtypeStruct((s, d), BF16)],
        compiler_params=_params("arbitrary", "arbitrary"),
    )(x, w_in)


def _conv_call(proj, conv_w, conv_b, c):
    s = proj.shape[0]
    t = min(SEQ_TILE, s)
    n_tiles = s // t

    def body(u_ref, up_ref, un_ref, w_ref, b_ref, xc_ref):
        i = pl.program_id(0)
        rows = lax.broadcasted_iota(jnp.int32, (SUBLANES, c), 0)
        w = w_ref[...]
        b = b_ref[...]

        def chunk(r0, _):
            prv, cur, nxt = _neighbour_chunks(u_ref, up_ref, un_ref, r0, t, slice(None), i == 0, i == n_tiles - 1)
            acc = b + w[1:2] * cur
            acc += w[0:1] * _shift_rows(prv, cur, nxt, -1, rows)
            acc += w[2:3] * _shift_rows(prv, cur, nxt, 1, rows)
            acc += w[3:4] * _shift_rows(prv, cur, nxt, 2, rows)
            xc_ref[pl.ds(r0, SUBLANES), :] = acc

        _chunk_loop(t, chunk)

    return pl.pallas_call(
        body, name="conv_fwd", grid=(n_tiles,),
        in_specs=_halo_specs(t, s, c, 1) + [pl.BlockSpec((4, c), lambda i: (0, 0)), pl.BlockSpec((1, c), lambda i: (0, 0))],
        out_specs=pl.BlockSpec((t, c), lambda i: (i, 0)),
        out_shape=jax.ShapeDtypeStruct((s, c), F32),
        compiler_params=_params("arbitrary"),
    )(proj, proj, proj, conv_w, conv_b)


def _gate_matmuls(xc_ref, wa_ref, wi_ref, pr_s, pi_s, heads):
    for h in range(heads):
        cs = pl.ds(h * LRU_HEAD, LRU_HEAD)
        xb = xc_ref[:, cs].astype(BF16)
        pr_s[:, cs] = jnp.dot(xb, wa_ref[h], preferred_element_type=F32)
        pi_s[:, cs] = jnp.dot(xb, wi_ref[h], preferred_element_type=F32)


def _rg_gates(pr, pi, ba, bi, sp):
    r = _sigmoid(pr + ba)
    ig = _sigmoid(pi + bi)
    log_a = (-RG_C * r) * sp
    a = jnp.exp(log_a)
    mult = jnp.sqrt(_neg_expm1(2.0 * log_a))
    return r, ig, a, mult


def _scan_fwd_call(xc, wa, wi, ba, bi, lam, reverse):
    s, c = xc.shape
    heads = c // LRU_HEAD
    t = min(SEQ_TILE, s)
    n_tiles = s // t
    tile = (lambda i: (n_tiles - 1 - i, 0)) if reverse else (lambda i: (i, 0))
    whole2 = lambda i: (0, 0)
    whole3 = lambda i: (0, 0, 0)

    def body(xc_ref, wa_ref, wi_ref, ba_ref, bi_ref, lam_ref, h_ref, pr_s, pi_s, carry_s):
        @pl.when(pl.program_id(0) == 0)
        def _():
            carry_s[...] = jnp.zeros_like(carry_s)

        _gate_matmuls(xc_ref, wa_ref, wi_ref, pr_s, pi_s, heads)
        ba_v, bi_v = ba_ref[...], bi_ref[...]
        sp = _softplus(-lam_ref[...])

        def chunk(r0, _):
            rs = pl.ds(r0, SUBLANES)
            xcv = xc_ref[rs, :]
            _, ig, a, mult = _rg_gates(pr_s[rs, :], pi_s[rs, :], ba_v, bi_v, sp)
            pr_s[rs, :] = a
            pi_s[rs, :] = mult * ig * xcv

        _chunk_loop(t, chunk)

        def row(j, h):
            r = (t - 1 - j) if reverse else j
            h = pr_s[pl.ds(r, 1), :] * h + pi_s[pl.ds(r, 1), :]
            h_ref[pl.ds(r, 1), :] = h
            return h

        carry_s[...] = lax.fori_loop(0, t, row, carry_s[...], unroll=8)

    return pl.pallas_call(
        body, name="scan_fwd_rev" if reverse else "scan_fwd", grid=(n_tiles,),
        in_specs=[pl.BlockSpec((t, c), tile),
                  pl.BlockSpec((heads, LRU_HEAD, LRU_HEAD), whole3), pl.BlockSpec((heads, LRU_HEAD, LRU_HEAD), whole3),
                  pl.BlockSpec((1, c), whole2), pl.BlockSpec((1, c), whole2), pl.BlockSpec((1, c), whole2)],
        out_specs=pl.BlockSpec((t, c), tile),
        out_shape=jax.ShapeDtypeStruct((s, c), F32),
        scratch_shapes=[pltpu.VMEM((t, c), F32), pltpu.VMEM((t, c), F32), pltpu.VMEM((1, c), F32)],
        compiler_params=_params("arbitrary"),
    )(xc, wa, wi, ba, bi, lam)


def _window_counts(r0, tile_idx, t_rows, n_rows, half, shape):
    pos = tile_idx * t_rows + r0 + lax.broadcasted_iota(jnp.int32, shape, 0)
    hi = jnp.minimum(pos + half, n_rows)
    lo = jnp.maximum(pos - half, 0)
    return (hi - lo).astype(F32)


def _pool_combine_call(proj, h_f, h_b, w_pool, pool_scale, p):
    s = proj.shape[0]
    c = h_f.shape[1]
    pg = p // N_POOL_GROUPS
    t = min(SEQ_TILE, s)
    n_tiles = s // t

    def body(u_ref, up_ref, un_ref, gate_ref, hf_ref, hb_ref, wp_ref, sc_ref, y_ref, d_ref, d_s, yr_s):
        i = pl.program_id(0)
        rows = lax.broadcasted_iota(jnp.int32, (SUBLANES, pg), 0)

        def chunk(r0, _):
            rs = pl.ds(r0, SUBLANES)
            for g, w in enumerate(POOL_WINDOWS):
                cols = pl.ds(g * pg, pg)
                prv, cur, nxt = _neighbour_chunks(u_ref, up_ref, un_ref, r0, t, cols, i == 0, i == n_tiles - 1)
                tot = cur
                for o in range(-(w // 2), w // 2):
                    if o != 0:
                        tot = tot + _shift_rows(prv, cur, nxt, o, rows)
                cnt = _window_counts(r0, i, t, s, w // 2, (SUBLANES, pg))
                d_s[rs, cols] = tot / cnt - cur
            gate, _ = _gelu_and_grad(gate_ref[rs, :])
            yr_s[rs, :] = (hf_ref[rs, :] + hb_ref[rs, :]) * gate

        _chunk_loop(t, chunk)
        y_ref[:, pl.ds(p, c)] = yr_s[...].astype(BF16)
        d_ref[...] = d_s[...].astype(BF16)
        for g in range(N_POOL_GROUPS):
            cols = pl.ds(g * pg, pg)
            out = jnp.dot(d_s[:, cols].astype(BF16), wp_ref[g], preferred_element_type=F32)
            y_ref[:, cols] = (out * sc_ref[:, cols]).astype(BF16)

    return pl.pallas_call(
        body, name="pool_combine", grid=(n_tiles,),
        in_specs=_halo_specs(t, s, p, 0) + [
            pl.BlockSpec((t, c), lambda i: (i, 2)),
            pl.BlockSpec((t, c), lambda i: (i, 0)), pl.BlockSpec((t, c), lambda i: (i, 0)),
            pl.BlockSpec((N_POOL_GROUPS, pg, pg), lambda i: (0, 0, 0)), pl.BlockSpec((1, p), lambda i: (0, 0))],
        out_specs=[pl.BlockSpec((t, p + c), lambda i: (i, 0)), pl.BlockSpec((t, p), lambda i: (i, 0))],
        out_shape=[jax.ShapeDtypeStruct((s, p + c), BF16), jax.ShapeDtypeStruct((s, p), BF16)],
        scratch_shapes=[pltpu.VMEM((t, p), F32), pltpu.VMEM((t, c), F32)],
        compiler_params=_params("arbitrary"),
    )(proj, proj, proj, proj, h_f, h_b, w_pool, pool_scale)


def _layer_norm_rows(z, g, b):
    mu = jnp.mean(z, axis=-1, keepdims=True)
    zc = z - mu
    var = jnp.mean(zc * zc, axis=-1, keepdims=True)
    rstd = lax.rsqrt(var + LN_EPS)
    xh = zc * rstd
    return xh, rstd, xh * g + b


def _layer_norm_bwd_rows(dx, xh, rstd, g):
    dxh = dx * g
    m1 = jnp.mean(dxh, axis=-1, keepdims=True)
    m2 = jnp.mean(dxh * xh, axis=-1, keepdims=True)
    return rstd * (dxh - m1 - xh * m2)


def _out_ln1_call(y, w_out, x, g1, b1):
    s, d = x.shape
    tm = min(SEQ_TILE, s)

    def body(y_ref, w_ref, x_ref, g_ref, b_ref, xh_ref, x1b_ref, rstd_ref, acc_s, x1_s):
        acc_s[...] = jnp.dot(y_ref[...], w_ref[...], preferred_element_type=F32)
        g, b = g_ref[...], b_ref[...]

        def chunk(r0, _):
            rs = pl.ds(r0, SUBLANES)
            xh, rstd, x1 = _layer_norm_rows(ALPHA * x_ref[rs, :] + acc_s[rs, :], g, b)
            xh_ref[rs, :] = xh
            x1_s[rs, :] = x1
            rstd_ref[rs, :] = rstd

        _chunk_loop(tm, chunk, unroll=LN_UNROLL)
        x1b_ref[...] = x1_s[...].astype(BF16)

    return pl.pallas_call(
        body, name="out_ln1", grid=(s // tm,),
        in_specs=[pl.BlockSpec((tm, d), lambda i: (i, 0)), pl.BlockSpec((d, d), lambda i: (0, 0)),
                  pl.BlockSpec((tm, d), lambda i: (i, 0)),
                  pl.BlockSpec((1, d), lambda i: (0, 0)), pl.BlockSpec((1, d), lambda i: (0, 0))],
        out_specs=[pl.BlockSpec((tm, d), lambda i: (i, 0)), pl.BlockSpec((tm, d), lambda i: (i, 0)),
                   pl.BlockSpec((tm, 1), lambda i: (i, 0))],
        out_shape=[jax.ShapeDtypeStruct((s, d), F32), jax.ShapeDtypeStruct((s, d), BF16), jax.ShapeDtypeStruct((s, 1), F32)],
        scratch_shapes=[pltpu.VMEM((tm, d), F32), pltpu.VMEM((tm, d), F32)],
        compiler_params=_params("arbitrary"),
    )(y, w_out, x, g1, b1)


def _mlp_in_call(x1b, w1):
    s, d = x1b.shape
    n, _, f4 = w1.shape
    tm = min(MM_TILE, s)
    tn = min(1024, f4)
    per = f4 // tn

    def body(x_ref, w_ref, r_ref, q_ref):
        r = jnp.maximum(jnp.dot(x_ref[...], w_ref[...], preferred_element_type=F32), 0.0)
        r_ref[...] = r.astype(BF16)
        q_ref[...] = (r * r).astype(BF16)

    return pl.pallas_call(
        body, name="mlp_in", grid=(n * per, s // tm),
        in_specs=[pl.BlockSpec((tm, d), lambda j, i: (i, 0)), pl.BlockSpec((None, d, tn), lambda j, i: (j // per, 0, j % per))],
        out_specs=[pl.BlockSpec((tm, tn), lambda j, i: (i, j)), pl.BlockSpec((tm, tn), lambda j, i: (i, j))],
        out_shape=[jax.ShapeDtypeStruct((s, n * f4), BF16), jax.ShapeDtypeStruct((s, n * f4), BF16)],
        compiler_params=_params("arbitrary", "arbitrary"),
    )(x1b, w1)


def _mlp_out_ln2_call(hsq, w2, xh1, g1, b1, g2, b2, target):
    s, f = hsq.shape
    d = w2.shape[1]
    tm = min(MM_TILE, s)
    tk = min(1024, f)
    nk = f // tk

    def body(h_ref, w_ref, xh1_ref, g1_ref, b1_ref, g2_ref, b2_ref, t_ref,
             dz_ref, dzb_ref, loss_ref, dg_ref, db_ref, acc_s):
        i, k = pl.program_id(0), pl.program_id(1)

        @pl.when((i == 0) & (k == 0))
        def _():
            loss_ref[...] = jnp.zeros_like(loss_ref)
            dg_ref[...] = jnp.zeros_like(dg_ref)
            db_ref[...] = jnp.zeros_like(db_ref)

        @pl.when(k == 0)
        def _():
            acc_s[...] = jnp.zeros_like(acc_s)

        acc_s[...] += jnp.dot(h_ref[...], w_ref[...], preferred_element_type=F32)

        @pl.when(k == nk - 1)
        def _():
            g1, b1, g2, b2 = g1_ref[...], b1_ref[...], g2_ref[...], b2_ref[...]

            def chunk(r0, _):
                rs = pl.ds(r0, SUBLANES)
                x1 = xh1_ref[rs, :] * g1 + b1
                xh2, rstd, x2 = _layer_norm_rows(ALPHA * x1 + acc_s[rs, :], g2, b2)
                diff = x2 - t_ref[rs, :]
                loss_ref[...] += diff * diff
                dx2 = diff * (1.0 / d)
                dg_ref[...] += dx2 * xh2
                db_ref[...] += dx2
                dz = _layer_norm_bwd_rows(dx2, xh2, rstd, g2)
                dz_ref[rs, :] = dz

            _chunk_loop(tm, chunk, unroll=LN_UNROLL)
            dzb_ref[...] = dz_ref[...].astype(BF16)

    row = lambda i, k: (i, 0)
    vec = lambda i, k: (0, 0)
    return pl.pallas_call(
        body, name="mlp_out_ln2", grid=(s // tm, nk),
        in_specs=[pl.BlockSpec((tm, tk), lambda i, k: (i, k)), pl.BlockSpec((tk, d), lambda i, k: (k, 0)),
                  pl.BlockSpec((tm, d), row), pl.BlockSpec((1, d), vec), pl.BlockSpec((1, d), vec),
                  pl.BlockSpec((1, d), vec), pl.BlockSpec((1, d), vec), pl.BlockSpec((tm, d), row)],
        out_specs=[pl.BlockSpec((tm, d), row), pl.BlockSpec((tm, d), row),
                   pl.BlockSpec((SUBLANES, d), vec), pl.BlockSpec((SUBLANES, d), vec), pl.BlockSpec((SUBLANES, d), vec)],
        out_shape=[jax.ShapeDtypeStruct((s, d), F32), jax.ShapeDtypeStruct((s, d), BF16),
                   jax.ShapeDtypeStruct((SUBLANES, d), F32), jax.ShapeDtypeStruct((SUBLANES, d), F32),
                   jax.ShapeDtypeStruct((SUBLANES, d), F32)],
        scratch_shapes=[pltpu.VMEM((tm, d), F32)],
        compiler_params=_params("arbitrary", "arbitrary"),
    )(hsq, w2, xh1, g1, b1, g2, b2, target)


def _weight_grad_call(a, b, tm, tn, out_shape, out_map, name, dep):
    s, m = a.shape
    n = b.shape[1]
    tk = min(1024, s)

    def body(a_ref, b_ref, dep_ref, o_ref):
        @pl.when(pl.program_id(2) == 0)
        def _():
            o_ref[...] = jnp.zeros_like(o_ref)

        o_ref[...] += lax.dot_general(a_ref[...], b_ref[...], (((0,), (0,)), ((), ())), preferred_element_type=F32)

    return pl.pallas_call(
        body, name=name, grid=(m // tm, n // tn, s // tk),
        in_specs=[pl.BlockSpec((tk, tm), lambda i, j, k: (k, i)), pl.BlockSpec((tk, tn), lambda i, j, k: (k, j)), ANY],
        out_specs=pl.BlockSpec((None, None, tm, tn), lambda i, j, k: out_map(i, j)),
        out_shape=jax.ShapeDtypeStruct(out_shape, F32),
        compiler_params=_params("arbitrary", "arbitrary", "arbitrary"),
    )(a, b, dep)


def _weight_grad_resident_call(a, b, tm, out_shape, out_map, name, dep):
    s, m = a.shape
    n = b.shape[1]

    def body(a_ref, b_ref, dep_ref, o_ref):
        o_ref[...] = lax.dot_general(a_ref[...], b_ref[...], (((0,), (0,)), ((), ())), preferred_element_type=F32)

    return pl.pallas_call(
        body, name=name, grid=(m // tm,),
        in_specs=[pl.BlockSpec((s, tm), lambda i: (0, i)), pl.BlockSpec((s, n), lambda i: (0, 0)), ANY],
        out_specs=pl.BlockSpec((None, None, tm, n), lambda i: out_map(i, 0)),
        out_shape=jax.ShapeDtypeStruct(out_shape, F32),
        compiler_params=_params("arbitrary"),
    )(a, b, dep)


def _row_sharded_grad(a, b, name, dep):
    s, m = a.shape
    n = b.shape[1]
    half_rows = m // (2 * N_CHIPS)
    tm = min(1024, half_rows)
    per = half_rows // tm
    out_shape = (2, N_CHIPS, half_rows, n)
    out_map = lambda i, j: ((i // per) % 2, i // (2 * per), i % per, j)
    if tm < 1024 and s * n * 2 <= RESIDENT_OPERAND_BYTES:
        return _weight_grad_resident_call(a, b, tm, out_shape, out_map, name, dep)
    return _weight_grad_call(a, b, tm, min(1024, n), out_shape, out_map, name, dep)


def _col_sharded_grad(a, b, name, dep):
    m, n = a.shape[1], b.shape[1]
    half_rows, shard_cols = m // 2, n // N_CHIPS
    tm = min(1024, half_rows)
    per_m = half_rows // tm
    tn = shard_cols if shard_cols % 1024 else 1024
    per_n = shard_cols // tn
    return _weight_grad_call(a, b, tm, tn, (2, N_CHIPS, half_rows, shard_cols),
                             lambda i, j: (i // per_m, j // per_n, i % per_m, j % per_n), name, dep)


def _dhsq_call(dzb, w2, r, dep):
    s, d = dzb.shape
    f = w2.shape[0]
    tm = min(MM_TILE, s)
    tn = min(1024, f)

    def body(dz_ref, w_ref, r_ref, dep_ref, o_ref):
        dh = lax.dot_general(dz_ref[...], w_ref[...], (((1,), (1,)), ((), ())), preferred_element_type=F32)
        o_ref[...] = (dh * (2.0 * r_ref[...].astype(F32))).astype(BF16)

    return pl.pallas_call(
        body, name="mlp_dpre", grid=(f // tn, s // tm),
        in_specs=[pl.BlockSpec((tm, d), lambda j, i: (i, 0)), pl.BlockSpec((tn, d), lambda j, i: (j, 0)),
                  pl.BlockSpec((tm, tn), lambda j, i: (i, j)), ANY],
        out_specs=pl.BlockSpec((tm, tn), lambda j, i: (i, j)),
        out_shape=jax.ShapeDtypeStruct((s, f), BF16),
        compiler_params=_params("arbitrary", "arbitrary"),
    )(dzb, w2, r, dep)


def _dx1_ln1_bwd_call(dpre, w1, dz2, xh1, rstd1, g1, dep):
    s, f = dpre.shape
    n, d, f4 = w1.shape
    tm = min(MM_TILE, s)
    tk = min(1024, f4)
    per = f4 // tk
    nk = n * per

    def body(dp_ref, w_ref, dz2_ref, xh_ref, rstd_ref, g_ref, dep_ref, dz_ref, dzb_ref, dg_ref, db_ref, acc_s):
        i, k = pl.program_id(0), pl.program_id(1)

        @pl.when((i == 0) & (k == 0))
        def _():
            dg_ref[...] = jnp.zeros_like(dg_ref)
            db_ref[...] = jnp.zeros_like(db_ref)

        @pl.when(k == 0)
        def _():
            acc_s[...] = jnp.zeros_like(acc_s)

        acc_s[...] += lax.dot_general(dp_ref[...], w_ref[...], (((1,), (1,)), ((), ())), preferred_element_type=F32)

        @pl.when(k == nk - 1)
        def _():
            g = g_ref[...]

            def chunk(r0, _):
                rs = pl.ds(r0, SUBLANES)
                dx1 = acc_s[rs, :] + ALPHA * dz2_ref[rs, :]
                xh = xh_ref[rs, :]
                dg_ref[...] += dx1 * xh
                db_ref[...] += dx1
                dz = _layer_norm_bwd_rows(dx1, xh, rstd_ref[rs, :], g)
                dz_ref[rs, :] = dz

            _chunk_loop(tm, chunk, unroll=LN_UNROLL)
            dzb_ref[...] = dz_ref[...].astype(BF16)

    row = lambda i, k: (i, 0)
    vec = lambda i, k: (0, 0)
    return pl.pallas_call(
        body, name="dx1_ln1_bwd", grid=(s // tm, nk),
        in_specs=[pl.BlockSpec((tm, tk), lambda i, k: (i, k)),
                  pl.BlockSpec((None, d, tk), lambda i, k: (k // per, 0, k % per)),
                  pl.BlockSpec((tm, d), row), pl.BlockSpec((tm, d), row), pl.BlockSpec((tm, 1), row),
                  pl.BlockSpec((1, d), vec), ANY],
        out_specs=[pl.BlockSpec((tm, d), row), pl.BlockSpec((tm, d), row),
                   pl.BlockSpec((SUBLANES, d), vec), pl.BlockSpec((SUBLANES, d), vec)],
        out_shape=[jax.ShapeDtypeStruct((s, d), F32), jax.ShapeDtypeStruct((s, d), BF16),
                   jax.ShapeDtypeStruct((SUBLANES, d), F32), jax.ShapeDtypeStruct((SUBLANES, d), F32)],
        scratch_shapes=[pltpu.VMEM((tm, d), F32)],
        compiler_params=_params("arbitrary", "arbitrary"),
    )(dpre, w1, dz2, xh1, rstd1, g1, dep)


def _dy_call(dzb, w_out, dep):
    s, d = dzb.shape
    e = w_out.shape[0]
    tm = min(MM_TILE, s)

    def body(dz_ref, w_ref, dep_ref, o_ref):
        o_ref[...] = lax.dot_general(dz_ref[...], w_ref[...], (((1,), (1,)), ((), ())), preferred_element_type=F32)

    return pl.pallas_call(
        body, name="dy", grid=(s // tm,),
        in_specs=[pl.BlockSpec((tm, d), lambda i: (i, 0)), pl.BlockSpec((e, d), lambda i: (0, 0)), ANY],
        out_specs=pl.BlockSpec((tm, e), lambda i: (i, 0)),
        out_shape=jax.ShapeDtypeStruct((s, e), F32),
        compiler_params=_params("arbitrary"),
    )(dzb, w_out, dep)


def _mixer_bwd_call(dy, d_pool, proj, h_f, h_b, w_pool, pool_scale, p):
    s = dy.shape[0]
    c = h_f.shape[1]
    pg = p // N_POOL_GROUPS
    t = min(SEQ_TILE, s)
    n_tiles = s // t

    def body(dyp_ref, dyr_ref, d_ref, gate_ref, hf_ref, hb_ref, wp_ref, sc_ref,
             e_ref, dh_ref, dgate_ref, dwp_ref, dsc_ref, dd_s):
        i = pl.program_id(0)

        @pl.when(i == 0)
        def _():
            dwp_ref[...] = jnp.zeros_like(dwp_ref)
            dsc_ref[...] = jnp.zeros_like(dsc_ref)

        for g in range(N_POOL_GROUPS):
            cols = pl.ds(g * pg, pg)
            dg = d_ref[:, cols]
            out = jnp.dot(dg, wp_ref[g], preferred_element_type=F32)
            dyp = dyp_ref[:, cols]
            prod = dyp * out
            dsc_ref[:, cols] += jnp.sum(prod.reshape(t // SUBLANES, SUBLANES, pg), axis=0)
            dout = (dyp * sc_ref[:, cols]).astype(BF16)
            dwp_ref[g] += lax.dot_general(dg, dout, (((0,), (0,)), ((), ())), preferred_element_type=F32)
            dd_s[:, cols] = lax.dot_general(dout, wp_ref[g], (((1,), (1,)), ((), ())), preferred_element_type=F32)

        def chunk(r0, _):
            rs = pl.ds(r0, SUBLANES)
            for g, w in enumerate(POOL_WINDOWS):
                cols = pl.ds(g * pg, pg)
                cnt = _window_counts(r0, i, t, s, w // 2, (SUBLANES, pg))
                e_ref[rs, cols] = dd_s[rs, cols] / cnt
            gate, dgate = _gelu_and_grad(gate_ref[rs, :])
            dyr = dyr_ref[rs, :]
            dh_ref[rs, :] = dyr * gate
            dd_s[rs, :] = dyr * (hf_ref[rs, :] + hb_ref[rs, :]) * dgate

        _chunk_loop(t, chunk)
        dgate_ref[...] = dd_s[...].astype(BF16)

    tile = lambda i: (i, 0)
    return pl.pallas_call(
        body, name="mixer_bwd", grid=(n_tiles,),
        in_specs=[pl.BlockSpec((t, p), tile), pl.BlockSpec((t, c), lambda i: (i, 1)), pl.BlockSpec((t, p), tile),
                  pl.BlockSpec((t, c), lambda i: (i, 2)), pl.BlockSpec((t, c), tile), pl.BlockSpec((t, c), tile),
                  pl.BlockSpec((N_POOL_GROUPS, pg, pg), lambda i: (0, 0, 0)), pl.BlockSpec((1, p), lambda i: (0, 0))],
        out_specs=[pl.BlockSpec((t, p), tile), pl.BlockSpec((t, c), tile), pl.BlockSpec((t, c), tile),
                   pl.BlockSpec((N_POOL_GROUPS, pg, pg), lambda i: (0, 0, 0)), pl.BlockSpec((SUBLANES, p), lambda i: (0, 0))],
        out_shape=[jax.ShapeDtypeStruct((s, p), F32), jax.ShapeDtypeStruct((s, c), F32), jax.ShapeDtypeStruct((s, c), BF16),
                   jax.ShapeDtypeStruct((N_POOL_GROUPS, pg, pg), F32), jax.ShapeDtypeStruct((SUBLANES, p), F32)],
        scratch_shapes=[pltpu.VMEM((t, p), F32)],
        compiler_params=_params("arbitrary"),
    )(dy, dy, d_pool, proj, h_f, h_b, w_pool, pool_scale)


def _scan_bwd_call(xc, dh, h_dir, dxc_prev, wa, wi, ba, bi, lam, reverse, dep):
    s, c = xc.shape
    heads = c // LRU_HEAD
    t = min(SEQ_TILE, s)
    n_tiles = s // t
    per = t // SUBLANES
    last_blk = s // SUBLANES - 1
    tile = (lambda i: (i, 0)) if reverse else (lambda i: (n_tiles - 1 - i, 0))
    if reverse:
        halo = lambda i: (jnp.minimum((i + 1) * per, last_blk), 0)
    else:
        halo = lambda i: (jnp.maximum((n_tiles - 1 - i) * per - 1, 0), 0)
    whole2 = lambda i: (0, 0)
    whole3 = lambda i: (0, 0, 0)
    has_prev = dxc_prev is not None

    def body(*refs):
        (xc_ref, dh_ref, h_ref, hh_ref) = refs[:4]
        k = 4
        prev_ref = None
        if has_prev:
            prev_ref = refs[k]
            k += 1
        wa_ref, wi_ref, ba_ref, bi_ref, lam_ref = refs[k:k + 5]
        k += 1
        dxc_ref, dwa_ref, dwi_ref, dba_ref, dbi_ref, dsp_ref = refs[k + 5:k + 11]
        pr_s, pi_s, g_s, carry_s = refs[k + 11:]
        step = pl.program_id(0)
        tile_idx = step if reverse else n_tiles - 1 - step

        @pl.when(step == 0)
        def _():
            carry_s[...] = jnp.zeros_like(carry_s)
            dwa_ref[...] = jnp.zeros_like(dwa_ref)
            dwi_ref[...] = jnp.zeros_like(dwi_ref)
            dba_ref[...] = jnp.zeros_like(dba_ref)
            dbi_ref[...] = jnp.zeros_like(dbi_ref)
            dsp_ref[...] = jnp.zeros_like(dsp_ref)

        _gate_matmuls(xc_ref, wa_ref, wi_ref, pr_s, pi_s, heads)
        ba_v, bi_v = ba_ref[...], bi_ref[...]
        sp = _softplus(-lam_ref[...])
        rows = lax.broadcasted_iota(jnp.int32, (SUBLANES, c), 0)

        def gates(r0, _):
            rs = pl.ds(r0, SUBLANES)
            _, _, a, _ = _rg_gates(pr_s[rs, :], pi_s[rs, :], ba_v, bi_v, sp)
            g_s[rs, :] = a

        _chunk_loop(t, gates)

        def row(j, carry):
            r = j if reverse else (t - 1 - j)
            gt = dh_ref[pl.ds(r, 1), :] + carry
            carry = g_s[pl.ds(r, 1), :] * gt
            g_s[pl.ds(r, 1), :] = gt
            return carry

        carry_s[...] = lax.fori_loop(0, t, row, carry_s[...], unroll=8)

        def chunk(r0, _):
            rs = pl.ds(r0, SUBLANES)
            xcv = xc_ref[rs, :]
            r, ig, a, mult = _rg_gates(pr_s[rs, :], pi_s[rs, :], ba_v, bi_v, sp)
            gt = g_s[rs, :]
            cur = h_ref[rs, :]
            if reverse:
                after = h_ref[pl.ds(pl.multiple_of(jnp.minimum(r0 + SUBLANES, t - SUBLANES), SUBLANES), SUBLANES), :]
                edge = jnp.where(tile_idx == n_tiles - 1, 0.0, hh_ref[...])
                nxt = jnp.where(r0 == t - SUBLANES, edge, after)
                hs = _shift_rows(cur, cur, nxt, 1, rows)
            else:
                before = h_ref[pl.ds(pl.multiple_of(jnp.maximum(r0 - SUBLANES, 0), SUBLANES), SUBLANES), :]
                edge = jnp.where(tile_idx == 0, 0.0, hh_ref[...])
                prv = jnp.where(r0 == 0, edge, before)
                hs = _shift_rows(prv, cur, cur, -1, rows)
            gx = gt * xcv
            dmult = gx * ig
            di = gx * mult
            dlog_a = (gt * hs) * a - dmult * (a * a) / mult
            dr = dlog_a * (-RG_C * sp)
            dsp_ref[...] += dlog_a * (-RG_C * r)
            dpr = dr * r * (1.0 - r)
            dpi = di * ig * (1.0 - ig)
            dba_ref[...] += dpr
            dbi_ref[...] += dpi
            direct = gt * mult * ig
            if has_prev:
                direct = direct + prev_ref[rs, :]
            dxc_ref[rs, :] = direct
            pr_s[rs, :] = dpr
            pi_s[rs, :] = dpi

        _chunk_loop(t, chunk)

        for h in range(heads):
            cs = pl.ds(h * LRU_HEAD, LRU_HEAD)
            xb = xc_ref[:, cs].astype(BF16)
            dprb = pr_s[:, cs].astype(BF16)
            dpib = pi_s[:, cs].astype(BF16)
            dwa_ref[h] += lax.dot_general(xb, dprb, (((0,), (0,)), ((), ())), preferred_element_type=F32)
            dwi_ref[h] += lax.dot_general(xb, dpib, (((0,), (0,)), ((), ())), preferred_element_type=F32)
            dxc_ref[:, cs] += (
                lax.dot_general(dprb, wa_ref[h], (((1,), (1,)), ((), ())), preferred_element_type=F32)
                + lax.dot_general(dpib, wi_ref[h], (((1,), (1,)), ((), ())), preferred_element_type=F32))

    in_specs = [pl.BlockSpec((t, c), tile), pl.BlockSpec((t, c), tile), pl.BlockSpec((t, c), tile),
                pl.BlockSpec((SUBLANES, c), halo)]
    args = [xc, dh, h_dir, h_dir]
    if has_prev:
        in_specs.append(pl.BlockSpec((t, c), tile))
        args.append(dxc_prev)
    in_specs += [pl.BlockSpec((heads, LRU_HEAD, LRU_HEAD), whole3), pl.BlockSpec((heads, LRU_HEAD, LRU_HEAD), whole3),
                 pl.BlockSpec((1, c), whole2), pl.BlockSpec((1, c), whole2), pl.BlockSpec((1, c), whole2)]
    in_specs.append(ANY)
    args += [wa, wi, ba, bi, lam, dep]
    return pl.pallas_call(
        body, name="scan_bwd_rev" if reverse else "scan_bwd", grid=(n_tiles,),
        in_specs=in_specs,
        out_specs=[pl.BlockSpec((t, c), tile),
                   pl.BlockSpec((heads, LRU_HEAD, LRU_HEAD), whole3), pl.BlockSpec((heads, LRU_HEAD, LRU_HEAD), whole3),
                   pl.BlockSpec((SUBLANES, c), whole2), pl.BlockSpec((SUBLANES, c), whole2), pl.BlockSpec((SUBLANES, c), whole2)],
        out_shape=[jax.ShapeDtypeStruct((s, c), F32),
                   jax.ShapeDtypeStruct((heads, LRU_HEAD, LRU_HEAD), F32), jax.ShapeDtypeStruct((heads, LRU_HEAD, LRU_HEAD), F32),
                   jax.ShapeDtypeStruct((SUBLANES, c), F32), jax.ShapeDtypeStruct((SUBLANES, c), F32),
                   jax.ShapeDtypeStruct((SUBLANES, c), F32)],
        scratch_shapes=[pltpu.VMEM((t, c), F32), pltpu.VMEM((t, c), F32), pltpu.VMEM((t, c), F32), pltpu.VMEM((1, c), F32)],
        compiler_params=_params("arbitrary"),
    )(*args)


def _dproj_call(e_pool, dxc, proj, dgate, conv_w, p):
    s, c = dxc.shape
    pg = p // N_POOL_GROUPS
    t = min(SEQ_TILE, s)
    n_tiles = s // t

    def body(e_ref, ep_ref, en_ref, dx_ref, dxp_ref, dxn_ref, u_ref, up_ref, un_ref, dgate_ref, w_ref,
             dproj_ref, dcw_ref, dcb_ref, st_s):
        i = pl.program_id(0)
        first, last = i == 0, i == n_tiles - 1

        @pl.when(first)
        def _():
            dcw_ref[...] = jnp.zeros_like(dcw_ref)
            dcb_ref[...] = jnp.zeros_like(dcb_ref)

        rows_p = lax.broadcasted_iota(jnp.int32, (SUBLANES, pg), 0)
        rows_c = lax.broadcasted_iota(jnp.int32, (SUBLANES, c), 0)
        w = w_ref[...]

        def chunk(r0, _):
            rs = pl.ds(r0, SUBLANES)
            for g, win in enumerate(POOL_WINDOWS):
                cols = pl.ds(g * pg, pg)
                prv, cur, nxt = _neighbour_chunks(e_ref, ep_ref, en_ref, r0, t, cols, first, last)
                tot = cur
                for o in range(-(win // 2) + 1, win // 2 + 1):
                    if o != 0:
                        tot = tot + _shift_rows(prv, cur, nxt, o, rows_p)
                cnt = _window_counts(r0, i, t, s, win // 2, (SUBLANES, pg))
                st_s[rs, cols] = tot - cur * cnt
            prv, cur, nxt = _neighbour_chunks(dx_ref, dxp_ref, dxn_ref, r0, t, slice(None), first, last)
            du = w[1:2] * cur
            du += w[0:1] * _shift_rows(prv, cur, nxt, 1, rows_c)
            du += w[2:3] * _shift_rows(prv, cur, nxt, -1, rows_c)
            du += w[3:4] * _shift_rows(prv, cur, nxt, -2, rows_c)
            st_s[rs, pl.ds(p, c)] = du
            uprv, ucur, unxt = _neighbour_chunks(u_ref, up_ref, un_ref, r0, t, slice(None), first, last)
            dcb_ref[...] += cur
            for j, o in enumerate((-1, 0, 1, 2)):
                dcw_ref[j] += cur * _shift_rows(uprv, ucur, unxt, o, rows_c)

        _chunk_loop(t, chunk)
        dproj_ref[:, pl.ds(0, p + c)] = st_s[...].astype(BF16)
        dproj_ref[:, pl.ds(p + c, c)] = dgate_ref[...]

    return pl.pallas_call(
        body, name="dproj", grid=(n_tiles,),
        in_specs=_halo_specs(t, s, p, 0) + _halo_specs(t, s, c, 0) + _halo_specs(t, s, c, 1) + [
            pl.BlockSpec((t, c), lambda i: (i, 0)), pl.BlockSpec((4, c), lambda i: (0, 0))],
        out_specs=[pl.BlockSpec((t, p + 2 * c), lambda i: (i, 0)),
                   pl.BlockSpec((4, SUBLANES, c), lambda i: (0, 0, 0)), pl.BlockSpec((SUBLANES, c), lambda i: (0, 0))],
        out_shape=[jax.ShapeDtypeStruct((s, p + 2 * c), BF16), jax.ShapeDtypeStruct((4, SUBLANES, c), F32),
                   jax.ShapeDtypeStruct((SUBLANES, c), F32)],
        scratch_shapes=[pltpu.VMEM((t, p + c), F32)],
        compiler_params=_params("arbitrary"),
    )(e_pool, e_pool, e_pool, dxc, dxc, dxc, proj, proj, proj, dgate, conv_w)


def _dx_call(dproj, w_in, dz1, dep):
    s, e = dproj.shape
    n, d, e4 = w_in.shape
    tm = min(MM_TILE, s)

    def body(dp_ref, w_ref, dz_ref, dep_ref, o_ref):
        k = pl.program_id(1)

        @pl.when(k == 0)
        def _():
            o_ref[...] = ALPHA * dz_ref[...]

        o_ref[...] += lax.dot_general(dp_ref[...], w_ref[...], (((1,), (1,)), ((), ())), preferred_element_type=F32)

    return pl.pallas_call(
        body, name="grad_x", grid=(s // tm, n),
        in_specs=[pl.BlockSpec((tm, e4), lambda i, k: (i, k)), pl.BlockSpec((None, d, e4), lambda i, k: (k, 0, 0)),
                  pl.BlockSpec((tm, d), lambda i, k: (i, 0)), ANY],
        out_specs=pl.BlockSpec((tm, d), lambda i, k: (i, 0)),
        out_shape=jax.ShapeDtypeStruct((s, d), F32),
        compiler_params=_params("arbitrary", "arbitrary"),
    )(dproj, w_in, dz1, dep)


def _row_tile(rows, cols, n_arrays):
    limit = max(SUBLANES, ELT_BLOCK_BYTES // (4 * cols * max(1, n_arrays // 4)))
    best = SUBLANES
    for cand in range(SUBLANES, min(rows, limit) + 1, SUBLANES):
        if rows % cand == 0:
            best = cand
    return best if rows % SUBLANES == 0 else rows


def _cast_to_slot_call(a, idx, dtype, name):
    rows, cols = a.shape
    tr = _row_tile(rows, cols, 2)

    def body(idx_ref, a_ref, o_ref):
        o_ref[...] = a_ref[...].astype(dtype)

    return pl.pallas_call(
        body, name=name,
        grid_spec=pltpu.PrefetchScalarGridSpec(
            num_scalar_prefetch=1, grid=(rows // tr,),
            in_specs=[pl.BlockSpec((tr, cols), lambda i, idx_ref: (i, 0))],
            out_specs=pl.BlockSpec((None, tr, cols), lambda i, idx_ref: (idx_ref[1], i, 0))),
        out_shape=jax.ShapeDtypeStruct((N_CHIPS, rows, cols), dtype),
        compiler_params=_params("arbitrary"),
    )(idx, a)


def _add_half_call(g, recv, idx, to_slot, name):
    _, rows, cols = g.shape
    tr = _row_tile(rows, cols, 3)

    def body(idx_ref, g_ref, r_ref, o_ref):
        o_ref[...] = g_ref[...] + r_ref[...]

    if to_slot:
        out_spec = pl.BlockSpec((None, tr, cols), lambda i, idx_ref: (idx_ref[1], i, 0))
        out_shape = jax.ShapeDtypeStruct((N_CHIPS, rows, cols), F32)
    else:
        out_spec = pl.BlockSpec((tr, cols), lambda i, idx_ref: (i, 0))
        out_shape = jax.ShapeDtypeStruct((rows, cols), F32)
    return pl.pallas_call(
        body, name=name,
        grid_spec=pltpu.PrefetchScalarGridSpec(
            num_scalar_prefetch=1, grid=(rows // tr,),
            in_specs=[pl.BlockSpec((None, tr, cols), lambda i, idx_ref: (idx_ref[0], i, 0)),
                      pl.BlockSpec((tr, cols), lambda i, idx_ref: (i, 0))],
            out_specs=out_spec),
        out_shape=out_shape,
        compiler_params=_params("arbitrary"),
    )(idx, g, recv)


def _sum_chips_call(own, recv, idx, name):
    _, rows, cols = recv.shape
    tr = _row_tile(rows, cols, 5)
    out_spec = pl.BlockSpec((None, tr, cols), lambda i, idx_ref: (idx_ref[0], i, 0))
    if own is None:
        def body(idx_ref, r_ref, o_ref):
            o_ref[...] = ((r_ref[0] + r_ref[1]) + r_ref[2]) + r_ref[3]
        in_specs = [pl.BlockSpec((N_CHIPS, tr, cols), lambda i, idx_ref: (0, i, 0))]
        args = (recv,)
    else:
        def body(idx_ref, p_ref, r_ref, o_ref):
            o_ref[...] = ((p_ref[...] + r_ref[0]) + r_ref[1]) + r_ref[2]
        in_specs = [pl.BlockSpec((None, tr, cols), lambda i, idx_ref: (idx_ref[1], i, 0)),
                    pl.BlockSpec((N_CHIPS - 1, tr, cols), lambda i, idx_ref: (0, i, 0))]
        args = (own, recv)
    return pl.pallas_call(
        body, name=name,
        grid_spec=pltpu.PrefetchScalarGridSpec(num_scalar_prefetch=1, grid=(rows // tr,), in_specs=in_specs, out_specs=out_spec),
        out_shape=jax.ShapeDtypeStruct((2, rows, cols), F32),
        compiler_params=_params("arbitrary"),
    )(idx, *args)


def _adamw_call(g, w, m, v, name):
    rows, cols = w.shape
    tr = _row_tile(rows, cols, 7)

    def body(g_ref, w_ref, m_ref, v_ref, d_ref, mo_ref, vo_ref):
        gv = g_ref[...]
        mn = ADAM_B1 * m_ref[...] + (1.0 - ADAM_B1) * gv
        vn = ADAM_B2 * v_ref[...] + (1.0 - ADAM_B2) * (gv * gv)
        m_hat = mn / (1.0 - ADAM_B1 ** ADAM_STEP)
        v_hat = vn / (1.0 - ADAM_B2 ** ADAM_STEP)
        d_ref[...] = -ADAM_LR * (m_hat / (jnp.sqrt(v_hat) + ADAM_EPS) + ADAM_WD * w_ref[...])
        mo_ref[...] = mn
        vo_ref[...] = vn

    spec = pl.BlockSpec((tr, cols), lambda i: (i, 0))
    shape = jax.ShapeDtypeStruct((rows, cols), F32)
    return pl.pallas_call(
        body, name=name, grid=(rows // tr,),
        in_specs=[spec] * 4, out_specs=[spec] * 3, out_shape=[shape] * 3,
        compiler_params=_params("arbitrary"),
    )(g, w, m, v)


def _mesh_place():
    x, y, c = lax.axis_index("x"), lax.axis_index("y"), lax.axis_index("c")
    chips = [(1 - x, y), (x, 1 - y), (1 - x, 1 - y)]
    return x, y, c, chips


def _remote(src, dst, send_sems, recv_sems, idx, device):
    return pltpu.make_async_remote_copy(src_ref=src, dst_ref=dst, send_sem=send_sems.at[idx], recv_sem=recv_sems.at[idx],
                                        device_id=device, device_id_type=MESH)


HBM_SPEC = pl.BlockSpec(memory_space=pltpu.HBM)
SEM_SPEC = pl.BlockSpec(memory_space=pltpu.SEMAPHORE)
ORDERED_EFFECT = pltpu.SideEffectType.DATAFLOW_SIDE_EFFECTING


def _in_hbm(a):
    return pltpu.with_memory_space_constraint(a, pltpu.HBM)


def _start_copies_call(name, bufs, groups, after=None):
    n, g = len(bufs), len(groups)
    extra = [] if after is None else [after]
    first_out = n + len(extra)

    def body(*refs):
        outs = refs[first_out:first_out + n]
        sems = refs[first_out + n:first_out + n + 2 * g]
        token = refs[first_out + n + 2 * g]
        for i, (which, copies_fn, _) in enumerate(groups):
            for mine, _ in copies_fn([outs[w] for w in which], sems[2 * i], sems[2 * i + 1]):
                mine.start()
        token[...] = jnp.zeros_like(token)

    sem_shapes = [pltpu.SemaphoreType.DMA((cnt,)) for _, _, cnt in groups for _ in range(2)]
    res = pl.pallas_call(
        body, name=name,
        in_specs=[HBM_SPEC] * n + [ANY] * len(extra),
        out_specs=[HBM_SPEC] * n + [SEM_SPEC] * (2 * g) + [pl.BlockSpec(memory_space=pltpu.VMEM)],
        out_shape=[pltpu.HBM(a.shape, a.dtype) for a in bufs] + sem_shapes + [jax.ShapeDtypeStruct((SUBLANES, LANES), F32)],
        input_output_aliases={a: a for a in range(n)},
        compiler_params=pltpu.CompilerParams(has_side_effects=ORDERED_EFFECT),
    )(*[_in_hbm(a) for a in bufs], *extra)
    sems = res[n:n + 2 * g]
    return list(res[:n]), [(sems[2 * i], sems[2 * i + 1]) for i in range(g)], res[n + 2 * g]


def _wait_copies_call(name, bufs, sems, copies_fn, after):
    n = len(bufs)

    def body(*refs):
        ins = refs[:n]
        send_sems, recv_sems = refs[n], refs[n + 1]
        for mine, arriving in copies_fn(list(ins), send_sems, recv_sems):
            arriving.wait_recv()
            mine.wait_send()

    res = pl.pallas_call(
        body, name=name,
        in_specs=[HBM_SPEC] * n + [SEM_SPEC, SEM_SPEC, ANY],
        out_specs=[HBM_SPEC] * n,
        out_shape=[pltpu.HBM(a.shape, a.dtype) for a in bufs],
        input_output_aliases={a: a for a in range(n)},
        compiler_params=pltpu.CompilerParams(has_side_effects=ORDERED_EFFECT),
    )(*bufs, sems[0], sems[1], after)
    return list(res)


def _gather_copies(bufs, send_sems, recv_sems):
    x, y, c, chips = _mesh_place()
    k = 2 * x + y
    out = []
    for a, buf in enumerate(bufs):
        for j, (px, py) in enumerate(chips):
            kj = 2 * px + py
            mine = _remote(buf.at[k, c], buf.at[k, c], send_sems, recv_sems, 3 * a + j, (px, py, c))
            arriving = _remote(buf.at[k, c], buf.at[kj, c], send_sems, recv_sems, 3 * a + j, (px, py, c))
            out.append((mine, arriving))
    return out


def _exchange_copies(n_sharded, n_replicated):
    def copies(bufs, send_sems, recv_sems):
        x, y, c, chips = _mesh_place()
        k = 2 * x + y
        sums, lands = bufs[:n_sharded], bufs[n_sharded:2 * n_sharded]
        repl = bufs[2 * n_sharded:]
        out = []
        for j, (px, py) in enumerate(chips):
            kj = 2 * px + py
            for a in range(n_sharded):
                cp = _remote(sums[a].at[kj], lands[a].at[j], send_sems, recv_sems, 3 * a + j, (px, py, c))
                out.append((cp, cp))
            for a in range(n_replicated):
                idx = 3 * (n_sharded + a) + j
                mine = _remote(repl[a].at[k], repl[a].at[k], send_sems, recv_sems, idx, (px, py, c))
                arriving = _remote(repl[a].at[k], repl[a].at[kj], send_sems, recv_sems, idx, (px, py, c))
                out.append((mine, arriving))
        return out
    return copies


def _sibling_copies(n):
    def copies(bufs, send_sems, recv_sems):
        x, y, c, _ = _mesh_place()
        out = []
        for a in range(n):
            cp = _remote(bufs[a].at[1 - c], bufs[n + a], send_sems, recv_sems, a, (x, y, 1 - c))
            out.append((cp, cp))
        return out
    return copies


def _forward_to_sibling_call(bufs, name):
    n = len(bufs)

    def body(*refs):
        ins, outs = refs[:n], refs[n:2 * n]
        send_sems, recv_sems = refs[2 * n:]
        x, y, c, chips = _mesh_place()
        sibling = (x, y, 1 - c)
        sends = []
        for a in range(n):
            for j, (px, py) in enumerate(chips):
                kj = 2 * px + py
                sends.append(_remote(ins[a].at[kj, c], outs[a].at[kj, c], send_sems, recv_sems, 3 * a + j, sibling))
        for cp in sends:
            cp.start()
        for a in range(n):
            for j, (px, py) in enumerate(chips):
                kj = 2 * px + py
                _remote(ins[a].at[kj, c], outs[a].at[kj, 1 - c], send_sems, recv_sems, 3 * a + j, sibling).wait_recv()
        for cp in sends:
            cp.wait_send()

    return pl.pallas_call(
        body, name=name,
        in_specs=[ANY] * n, out_specs=[ANY] * n,
        out_shape=[jax.ShapeDtypeStruct(a.shape, a.dtype) for a in bufs],
        input_output_aliases={a: a for a in range(n)},
        scratch_shapes=[pltpu.SemaphoreType.DMA((3 * n,)), pltpu.SemaphoreType.DMA((3 * n,))],
    )(*bufs)


def _join_halves_call(bufs, name):
    n = len(bufs)

    def body(*refs):
        ins, outs = refs[:n], refs[n:2 * n]
        send_sems, recv_sems = refs[2 * n:]
        x, y, c, _ = _mesh_place()
        sibling = (x, y, 1 - c)
        copies = [_remote(ins[a].at[c], outs[a].at[c], send_sems, recv_sems, a, sibling) for a in range(n)]
        for cp in copies:
            cp.start()
        for a in range(n):
            _remote(ins[a].at[c], outs[a].at[1 - c], send_sems, recv_sems, a, sibling).wait_recv()
        for cp in copies:
            cp.wait_send()

    return pl.pallas_call(
        body, name=name,
        in_specs=[ANY] * n, out_specs=[ANY] * n,
        out_shape=[jax.ShapeDtypeStruct(a.shape, a.dtype) for a in bufs],
        input_output_aliases={a: a for a in range(n)},
        scratch_shapes=[pltpu.SemaphoreType.DMA((n,)), pltpu.SemaphoreType.DMA((n,))],
    )(*bufs)


def _pack(arrays, rows_multiple):
    flat = jnp.concatenate([a.reshape(-1) for a in arrays])
    per = LANES * rows_multiple
    padded = -(-flat.shape[0] // per) * per
    flat = jnp.pad(flat, (0, padded - flat.shape[0]))
    return flat.reshape(-1, LANES)


def _unpack(packed, shapes):
    flat = packed.reshape(-1)
    out, at = [], 0
    for shp in shapes:
        size = 1
        for dim in shp:
            size *= dim
        out.append(flat[at:at + size].reshape(shp))
        at += size
    return out


def _halves(a):
    return a.reshape((2, a.shape[0] // 2) + a.shape[1:])


def kernel(x, ln_mix_g, ln_mix_b, w_in, w_pool, pool_scale, conv_w, conv_b, w_rg_a, b_rg_a, w_rg_i, b_rg_i, rg_lambda, w_out, ln_ffn_g, ln_ffn_b, w_mlp_in, w_mlp_out, loss_target, m_ln_mix_g, m_ln_mix_b, m_w_in, m_w_pool, m_pool_scale, m_conv_w, m_conv_b, m_w_rg_a, m_b_rg_a, m_w_rg_i, m_b_rg_i, m_rg_lambda, m_w_out, m_ln_ffn_g, m_ln_ffn_b, m_w_mlp_in, m_w_mlp_out, v_ln_mix_g, v_ln_mix_b, v_w_in, v_w_pool, v_pool_scale, v_conv_w, v_conv_b, v_w_rg_a, v_b_rg_a, v_w_rg_i, v_b_rg_i, v_rg_lambda, v_w_out, v_ln_ffn_g, v_ln_ffn_b, v_w_mlp_in, v_w_mlp_out):
    weights = dict(ln_mix_g=ln_mix_g, ln_mix_b=ln_mix_b, w_in=w_in, w_pool=w_pool, pool_scale=pool_scale, conv_w=conv_w,
                   conv_b=conv_b, w_rg_a=w_rg_a, b_rg_a=b_rg_a, w_rg_i=w_rg_i, b_rg_i=b_rg_i, rg_lambda=rg_lambda,
                   w_out=w_out, ln_ffn_g=ln_ffn_g, ln_ffn_b=ln_ffn_b, w_mlp_in=w_mlp_in, w_mlp_out=w_mlp_out)
    m_in = dict(ln_mix_g=m_ln_mix_g, ln_mix_b=m_ln_mix_b, w_in=m_w_in, w_pool=m_w_pool, pool_scale=m_pool_scale,
                conv_w=m_conv_w, conv_b=m_conv_b, w_rg_a=m_w_rg_a, b_rg_a=m_b_rg_a, w_rg_i=m_w_rg_i, b_rg_i=m_b_rg_i,
                rg_lambda=m_rg_lambda, w_out=m_w_out, ln_ffn_g=m_ln_ffn_g, ln_ffn_b=m_ln_ffn_b, w_mlp_in=m_w_mlp_in,
                w_mlp_out=m_w_mlp_out)
    v_in = dict(ln_mix_g=v_ln_mix_g, ln_mix_b=v_ln_mix_b, w_in=v_w_in, w_pool=v_w_pool, pool_scale=v_pool_scale,
                conv_w=v_conv_w, conv_b=v_conv_b, w_rg_a=v_w_rg_a, b_rg_a=v_b_rg_a, w_rg_i=v_w_rg_i, b_rg_i=v_b_rg_i,
                rg_lambda=v_rg_lambda, w_out=v_w_out, ln_ffn_g=v_ln_ffn_g, ln_ffn_b=v_ln_ffn_b, w_mlp_in=v_w_mlp_in,
                w_mlp_out=v_w_mlp_out)
    names = list(weights)

    xs = x[0]
    tgt = loss_target[0]
    s, d = xs.shape
    p = c = d // 2
    pg = p // N_POOL_GROUPS
    heads = c // LRU_HEAD
    core = lax.axis_index("c")
    shard = 2 * lax.axis_index("x") + lax.axis_index("y")

    idx = jnp.stack([core, shard]).astype(jnp.int32)
    small_shard = _pack([conv_w[0], b_rg_a[0], b_rg_i[0], rg_lambda[0]], 2 * SUBLANES)
    to_gather = [(w_in[0], BF16), (w_out[0], BF16), (w_mlp_in[0], BF16), (w_mlp_out[0], BF16),
                 (w_pool[0].reshape(-1, pg), BF16), (small_shard, F32)]

    def slot_view(i):
        a, dt = to_gather[i]
        sl = _cast_to_slot_call(a, idx, dt, f"gather_slot_{i}")
        return sl.reshape(N_CHIPS, 2, sl.shape[1] // 2, sl.shape[2])

    first, later = (0, 4, 5), (1, 2, 3)
    fly_a, sems_a, token_a = _start_copies_call(
        "gather_start_first", [slot_view(i) for i in first], [((0, 1, 2), _gather_copies, 3 * len(first))])
    fly_b, sems_b, g_token = _start_copies_call(
        "gather_start_later", [slot_view(i) for i in later],
        [((0,), _gather_copies, 3), ((1,), _gather_copies, 3), ((2,), _gather_copies, 3)], token_a)
    in_flight = {**dict(zip(first, fly_a)), **dict(zip(later, fly_b))}
    g_sems = [sems_a[0]] + list(sems_b)

    def arrive(which, group, after, tag):
        got = _wait_copies_call(f"gather_wait_{tag}", [in_flight[w] for w in which], g_sems[group], _gather_copies, after)
        return _forward_to_sibling_call(got, f"gather_forward_{tag}")

    gathered = [None] * len(to_gather)
    gathered[0], gathered[4], gathered[5] = arrive(first, 0, g_token, "w_in")
    w_in_f = gathered[0].reshape((N_CHIPS,) + w_in.shape[1:])
    w_pool_f = gathered[4].reshape(N_CHIPS, N_POOL_GROUPS, pg // N_CHIPS, pg).transpose(1, 0, 2, 3).reshape(N_POOL_GROUPS, pg, pg)
    c4 = c // N_CHIPS
    small_parts = [_unpack(gathered[5][k].reshape(-1, LANES), [(4, c4), (2, c4), (2, c4), (2, c4)]) for k in range(N_CHIPS)]
    conv_w_f = jnp.concatenate([sp_[0] for sp_ in small_parts], axis=1)
    b_a_f = jnp.concatenate([sp_[1] for sp_ in small_parts], axis=1)
    b_i_f = jnp.concatenate([sp_[2] for sp_ in small_parts], axis=1)
    lam_f = jnp.concatenate([sp_[3] for sp_ in small_parts], axis=1)
    wa_b = w_rg_a[0].astype(BF16)
    wi_b = w_rg_i[0].astype(BF16)

    proj, xb = _proj_call(xs, w_in_f)
    xc = _conv_call(proj, conv_w_f, conv_b, c)
    h_b = _scan_fwd_call(xc, wa_b[1], wi_b[1], b_a_f[1:2], b_i_f[1:2], lam_f[1:2], True)
    h_f = _scan_fwd_call(xc, wa_b[0], wi_b[0], b_a_f[0:1], b_i_f[0:1], lam_f[0:1], False)
    y, d_pool = _pool_combine_call(proj, h_f, h_b, w_pool_f, pool_scale, p)
    w_out_f = arrive((1,), 1, y, "w_out")[0].reshape(d, d)
    xh1, x1b, rstd1 = _out_ln1_call(y, w_out_f, xs, ln_mix_g, ln_mix_b)
    w1_f = arrive((2,), 2, x1b, "w_mlp_in")[0].reshape((N_CHIPS,) + w_mlp_in.shape[1:])
    r_act, hsq = _mlp_in_call(x1b, w1_f)
    w2_f = arrive((3,), 3, hsq, "w_mlp_out")[0].reshape(N_CHIPS * w_mlp_out.shape[1], d)
    dz2, dz2b, loss8, dg2, db2 = _mlp_out_ln2_call(hsq, w2_f, xh1, ln_mix_g, ln_mix_b, ln_ffn_g, ln_ffn_b, tgt)

    def start_siblings(grads, tag, after=None):
        lands = [lax.empty(g.shape[1:], g.dtype) for g in grads]
        copies = _sibling_copies(len(grads))
        flying, sems, token = _start_copies_call(
            f"siblings_start_{tag}", list(grads) + lands, [(tuple(range(2 * len(grads))), copies, len(grads))], after)
        return (flying, sems[0], copies, len(grads), tag), token

    def finish_siblings(state, after, small_at=None):
        flying, sems, copies, n, tag = state
        got = _wait_copies_call(f"siblings_wait_{tag}", flying, sems, copies, after)
        out = []
        for a in range(n):
            g, rv = got[a], got[n + a]
            cols = g.shape[-1]
            part = _add_half_call(g.reshape(2, -1, cols), rv.reshape(-1, cols), idx, a == small_at, f"reduce_add_{tag}_{a}")
            out.append(part if a == small_at else part.reshape((N_CHIPS,) + rv.shape[1:]))
        return out

    def start_exchange(sums, n_repl, tag):
        n_sh = len(sums) - n_repl
        lands = [lax.empty((N_CHIPS - 1,) + a.shape[1:], a.dtype) for a in sums[:n_sh]]
        bufs = sums[:n_sh] + lands + sums[n_sh:]
        copies = _exchange_copies(n_sh, n_repl)
        flying, sems, token = _start_copies_call(
            f"reduce_start_{tag}", bufs, [(tuple(range(len(bufs))), copies, 3 * len(sums))])
        return (flying, sems[0], copies, n_sh, tag), token

    def finish_exchange(state, after):
        flying, sems, copies, n_sh, tag = state
        got = _wait_copies_call(f"reduce_wait_{tag}", flying, sems, copies, after)
        halves = []
        for a in range(n_sh):
            own, land = got[a], got[n_sh + a]
            cols = own.shape[-1]
            total = _sum_chips_call(own.reshape(N_CHIPS, -1, cols), land.reshape(N_CHIPS - 1, -1, cols), idx,
                                    f"reduce_sum_{tag}_{a}")
            halves.append(total.reshape((2,) + own.shape[1:]))
        for a, rp in enumerate(got[2 * n_sh:]):
            halves.append(_sum_chips_call(None, rp, idx, f"reduce_sum_{tag}_r{a}"))
        return _join_halves_call(halves, f"reduce_join_{tag}")

    g_w2 = _row_sharded_grad(hsq, dz2b, "grad_w_mlp_out", g_token)
    sib_w2, token = start_siblings([g_w2], "w2")
    dpre = _dhsq_call(dz2b, w2_f, r_act, token)
    flying_w2, token = start_exchange(finish_siblings(sib_w2, dpre), 0, "w2")
    g_w1 = _col_sharded_grad(x1b, dpre, "grad_w_mlp_in", token)
    sib_w1, token = start_siblings([g_w1], "w1")
    dz1, dz1b, dg1, db1 = _dx1_ln1_bwd_call(dpre, w1_f, dz2, xh1, rstd1, ln_mix_g, token)
    flying_w1, token = start_exchange(finish_siblings(sib_w1, dz1b), 0, "w1")
    g_wout = _row_sharded_grad(y, dz1b, "grad_w_out", token)
    sib_wout, token = start_siblings([g_wout], "w_out")
    dy = _dy_call(dz1b, w_out_f, token)
    e_pool, dh, dgate, g_wpool, g_pscale8 = _mixer_bwd_call(dy, d_pool, proj, h_f, h_b, w_pool_f, pool_scale, p)
    flying_wout, token = start_exchange(finish_siblings(sib_wout, dgate), 0, "w_out")
    dxc0, g_wa0, g_wi0, g_ba0, g_bi0, g_sp0 = _scan_bwd_call(
        xc, dh, h_f, None, wa_b[0], wi_b[0], b_a_f[0:1], b_i_f[0:1], lam_f[0:1], False, token)
    dxc, g_wa1, g_wi1, g_ba1, g_bi1, g_sp1 = _scan_bwd_call(
        xc, dh, h_b, dxc0, wa_b[1], wi_b[1], b_a_f[1:2], b_i_f[1:2], lam_f[1:2], True, token)
    dproj, g_cw8, g_cb8 = _dproj_call(e_pool, dxc, proj, dgate, conv_w_f, p)

    rowsum = lambda a8: jnp.sum(a8, axis=-2)
    g_lam = jnp.stack([rowsum(g_sp0), rowsum(g_sp1)]) * (-_sigmoid(-lam_f))
    small_grads = {
        "ln_mix_g": rowsum(dg1), "ln_mix_b": rowsum(db1), "ln_ffn_g": rowsum(dg2), "ln_ffn_b": rowsum(db2),
        "pool_scale": rowsum(g_pscale8), "conv_b": rowsum(g_cb8),
        "w_rg_a": jnp.stack([g_wa0, g_wa1]), "w_rg_i": jnp.stack([g_wi0, g_wi1]),
        "w_pool": g_wpool, "conv_w": rowsum(g_cw8),
        "b_rg_a": jnp.stack([rowsum(g_ba0), rowsum(g_ba1)]), "b_rg_i": jnp.stack([rowsum(g_bi0), rowsum(g_bi1)]),
        "rg_lambda": g_lam,
    }
    small_names = list(small_grads)
    small_shapes = [small_grads[nm].shape for nm in small_names]
    g_small = _halves(_pack([small_grads[nm] for nm in small_names], 2 * SUBLANES))
    sib_small, token = start_siblings([g_small], "small")
    g_win = _col_sharded_grad(xb, dproj, "grad_w_in", token)
    flying_small, token = start_exchange(finish_siblings(sib_small, g_win, small_at=0), 1, "small")
    sib_win, token = start_siblings([g_win], "w_in", token)
    grad_x = _dx_call(dproj, w_in_f, dz1, token)
    flying_win, token = start_exchange(finish_siblings(sib_win, grad_x), 0, "w_in")

    grad_w, delta_w, new_m, new_v = {}, {}, {}, {}

    def adamw(nm, full):
        w2d = weights[nm][0]
        g2d = full.reshape(w2d.shape)
        dl, mn, vn = _adamw_call(g2d, w2d, m_in[nm][0], v_in[nm][0], f"adamw_{nm}")
        grad_w[nm], delta_w[nm], new_m[nm], new_v[nm] = g2d[None], dl[None], mn[None], vn[None]
        return vn

    last = adamw("w_mlp_out", finish_exchange(flying_w2, token)[0])
    last = adamw("w_mlp_in", finish_exchange(flying_w1, last)[0])
    last = adamw("w_out", finish_exchange(flying_wout, last)[0])

    small_full = dict(zip(small_names, _unpack(finish_exchange(flying_small, last)[0].reshape(-1, LANES), small_shapes)))
    local = dict(small_full)
    local["w_pool"] = lax.dynamic_slice_in_dim(small_full["w_pool"], shard * (pg // N_CHIPS), pg // N_CHIPS, axis=1)
    for nm in ("conv_w", "b_rg_a", "b_rg_i", "rg_lambda"):
        local[nm] = lax.dynamic_slice_in_dim(small_full[nm], shard * c4, c4, axis=1)
    small_w_shapes = [weights[nm].shape for nm in small_names]
    g_pack = _pack([local[nm] for nm in small_names], SUBLANES)
    w_pack = _pack([weights[nm] for nm in small_names], SUBLANES)
    m_pack = _pack([m_in[nm] for nm in small_names], SUBLANES)
    v_pack = _pack([v_in[nm] for nm in small_names], SUBLANES)
    dl_p, mn_p, vn_p = _adamw_call(g_pack, w_pack, m_pack, v_pack, "adamw_small")
    for nm, gl, dl, mn, vn in zip(small_names, _unpack(g_pack, small_w_shapes), _unpack(dl_p, small_w_shapes),
                                  _unpack(mn_p, small_w_shapes), _unpack(vn_p, small_w_shapes)):
        grad_w[nm], delta_w[nm], new_m[nm], new_v[nm] = gl, dl, mn, vn
    adamw("w_in", finish_exchange(flying_win, vn_p)[0])

    loss = lax.psum(jnp.sum(loss8) * (0.5 / d), ("x", "y", "c"))
    return (loss, grad_x[None], *[grad_w[nm] for nm in names], *[delta_w[nm] for nm in names],
            *[new_m[nm] for nm in names], *[new_v[nm] for nm in names])
```

```python
import functools

import jax
import jax.numpy as jnp
from jax import lax
from jax.experimental import pallas as pl
from jax.experimental.pallas import tpu as pltpu

F32 = jnp.float32
BF16 = jnp.bfloat16

N_CHIPS = 4
LANES = 128
SUBLANES = 8
LRU_HEAD = 128
N_POOL_GROUPS = 4
POOL_WINDOWS = (2, 4, 8, 16)
RG_C = 8.0
LN_EPS = 1e-5
ALPHA = 2.0 ** 0.25
ADAM_LR, ADAM_B1, ADAM_B2, ADAM_EPS, ADAM_WD, ADAM_STEP = 0.001, 0.9, 0.999, 1e-08, 0.01, 10
VMEM_LIMIT = 56 * 1024 * 1024
SEQ_TILE = 256
MM_TILE = 512
LN_UNROLL = 4
ELT_BLOCK_BYTES = 2 * 1024 * 1024
RESIDENT_OPERAND_BYTES = 16 * 1024 * 1024
MESH = pl.DeviceIdType.MESH
ANY = pl.BlockSpec(memory_space=pl.ANY)


def _params(*sem):
    return pltpu.CompilerParams(dimension_semantics=sem, vmem_limit_bytes=VMEM_LIMIT)


def _sigmoid(z):
    return 1.0 / (1.0 + jnp.exp(-z))


def _neg_expm1(z):
    series = -(z * (1.0 + z * (0.5 + z * (1.0 / 6.0 + z * (1.0 / 24.0)))))
    return jnp.where(z > -0.01, series, 1.0 - jnp.exp(z))


def _softplus(z):
    return jnp.maximum(z, 0.0) + jnp.log1p(jnp.exp(-jnp.abs(z)))


_GELU_K = 0.7978845608028654
_GELU_C = 0.044715


def _gelu_and_grad(u):
    t = jnp.tanh(_GELU_K * (u + _GELU_C * (u * u * u)))
    g = 0.5 * u * (1.0 + t)
    dg = 0.5 * (1.0 + t) + 0.5 * u * (1.0 - t * t) * (_GELU_K * (1.0 + 3.0 * _GELU_C * u * u))
    return g, dg


def _shift_rows(prv, cur, nxt, o, rows):
    if o == 0:
        return cur
    if o == SUBLANES:
        return nxt
    if o == -SUBLANES:
        return prv
    if o > 0:
        s = SUBLANES - o
        return jnp.where(rows < s, pltpu.roll(cur, s, 0), pltpu.roll(nxt, s, 0))
    p = -o
    return jnp.where(rows >= p, pltpu.roll(cur, p, 0), pltpu.roll(prv, p, 0))


def _neighbour_chunks(main_ref, prev_ref, next_ref, r0, t_rows, cols, first_tile, last_tile):
    cur = main_ref[pl.ds(r0, SUBLANES), cols]
    before = main_ref[pl.ds(pl.multiple_of(jnp.maximum(r0 - SUBLANES, 0), SUBLANES), SUBLANES), cols]
    after = main_ref[pl.ds(pl.multiple_of(jnp.minimum(r0 + SUBLANES, t_rows - SUBLANES), SUBLANES), SUBLANES), cols]
    halo_prev = jnp.where(first_tile, 0.0, prev_ref[:, cols])
    halo_next = jnp.where(last_tile, 0.0, next_ref[:, cols])
    prv = jnp.where(r0 == 0, halo_prev, before)
    nxt = jnp.where(r0 == t_rows - SUBLANES, halo_next, after)
    return prv, cur, nxt


def _halo_specs(t_rows, n_rows, width, col_block):
    per = t_rows // SUBLANES
    last = n_rows // SUBLANES - 1
    return [
        pl.BlockSpec((t_rows, width), lambda i: (i, col_block)),
        pl.BlockSpec((SUBLANES, width), lambda i: (jnp.maximum(i * per - 1, 0), col_block)),
        pl.BlockSpec((SUBLANES, width), lambda i: (jnp.minimum((i + 1) * per, last), col_block)),
    ]


def _chunk_loop(t_rows, fn, init=None, unroll=1):
    span = SUBLANES * unroll

    def step(ci, carry):
        base = pl.multiple_of(ci * span, span)
        for u in range(unroll):
            carry = fn(base + u * SUBLANES, carry)
        return carry
    return lax.fori_loop(0, t_rows // span, step, init)


def _proj_call(x, w_in):
    s, d = x.shape
    n, _, e4 = w_in.shape
    tm = min(MM_TILE, s)

    def body(x_ref, w_ref, proj_ref, xb_ref):
        @pl.when(pl.program_id(1) == 0)
        def _():
            xb_ref[...] = x_ref[...].astype(BF16)

        proj_ref[...] = jnp.dot(xb_ref[...], w_ref[...], preferred_element_type=F32)

    return pl.pallas_call(
        body, name="proj", grid=(s // tm, n),
        in_specs=[pl.BlockSpec((tm, d), lambda i, j: (i, 0)), pl.BlockSpec((None, d, e4), lambda i, j: (j, 0, 0))],
        out_specs=[pl.BlockSpec((tm, e4), lambda i, j: (i, j)), pl.BlockSpec((tm, d), lambda i, j: (i, 0))],
        out_shape=[jax.ShapeDtypeStruct((s, n * e4), F32), jax.ShapeDtypeStruct((s, d), BF16)],
        compiler_params=_params("arbitrary", "arbitrary"),
    )(x, w_in)


def _conv_call(proj, conv_w, conv_b, c):
    s = proj.shape[0]
    t = min(SEQ_TILE, s)
    n_tiles = s // t

    def body(u_ref, up_ref, un_ref, w_ref, b_ref, xc_ref):
        i = pl.program_id(0)
        rows = lax.broadcasted_iota(jnp.int32, (SUBLANES, c), 0)
        w = w_ref[...]
        b = b_ref[...]

        def chunk(r0, _):
            prv, cur, nxt = _neighbour_chunks(u_ref, up_ref, un_ref, r0, t, slice(None), i == 0, i == n_tiles - 1)
            acc = b + w[1:2] * cur
            acc += w[0:1] * _shift_rows(prv, cur, nxt, -1, rows)
            acc += w[2:3] * _shift_rows(prv, cur, nxt, 1, rows)
            acc += w[3:4] * _shift_rows(prv, cur, nxt, 2, rows)
            xc_ref[pl.ds(r0, SUBLANES), :] = acc

        _chunk_loop(t, chunk)

    return pl.pallas_call(
        body, name="conv_fwd", grid=(n_tiles,),
        in_specs=_halo_specs(t, s, c, 1) + [pl.BlockSpec((4, c), lambda i: (0, 0)), pl.BlockSpec((1, c), lambda i: (0, 0))],
        out_specs=pl.BlockSpec((t, c), lambda i: (i, 0)),
        out_shape=jax.ShapeDtypeStruct((s, c), F32),
        compiler_params=_params("arbitrary"),
    )(proj, proj, proj, conv_w, conv_b)


def _gate_matmuls(xc_ref, wa_ref, wi_ref, pr_s, pi_s, heads):
    for h in range(heads):
        cs = pl.ds(h * LRU_HEAD, LRU_HEAD)
        xb = xc_ref[:, cs].astype(BF16)
        pr_s[:, cs] = jnp.dot(xb, wa_ref[h], preferred_element_type=F32)
        pi_s[:, cs] = jnp.dot(xb, wi_ref[h], preferred_element_type=F32)


def _rg_gates(pr, pi, ba, bi, sp):
    r = _sigmoid(pr + ba)
    ig = _sigmoid(pi + bi)
    log_a = (-RG_C * r) * sp
    a = jnp.exp(log_a)
    mult = jnp.sqrt(_neg_expm1(2.0 * log_a))
    return r, ig, a, mult


def _scan_fwd_call(xc, wa, wi, ba, bi, lam, reverse, dep):
    s, c = xc.shape
    heads = c // LRU_HEAD
    t = min(SEQ_TILE, s)
    n_tiles = s // t
    tile = (lambda i: (n_tiles - 1 - i, 0)) if reverse else (lambda i: (i, 0))
    whole2 = lambda i: (0, 0)
    whole3 = lambda i: (0, 0, 0)

    def body(xc_ref, wa_ref, wi_ref, ba_ref, bi_ref, lam_ref, dep_ref, h_ref, r_ref, ig_ref, a_ref, mult_ref,
             pr_s, pi_s, carry_s):
        @pl.when(pl.program_id(0) == 0)
        def _():
            carry_s[...] = jnp.zeros_like(carry_s)

        _gate_matmuls(xc_ref, wa_ref, wi_ref, pr_s, pi_s, heads)
        ba_v, bi_v = ba_ref[...], bi_ref[...]
        sp = _softplus(-lam_ref[...])

        def chunk(r0, _):
            rs = pl.ds(r0, SUBLANES)
            r, ig, a, mult = _rg_gates(pr_s[rs, :], pi_s[rs, :], ba_v, bi_v, sp)
            r_ref[rs, :] = r
            ig_ref[rs, :] = ig
            a_ref[rs, :] = a
            mult_ref[rs, :] = mult
            pi_s[rs, :] = mult * ig * xc_ref[rs, :]

        _chunk_loop(t, chunk)

        def row(j, h):
            r = (t - 1 - j) if reverse else j
            h = a_ref[pl.ds(r, 1), :] * h + pi_s[pl.ds(r, 1), :]
            h_ref[pl.ds(r, 1), :] = h
            return h

        carry_s[...] = lax.fori_loop(0, t, row, carry_s[...], unroll=8)

    return pl.pallas_call(
        body, name="scan_fwd_rev" if reverse else "scan_fwd", grid=(n_tiles,),
        in_specs=[pl.BlockSpec((t, c), tile),
                  pl.BlockSpec((heads, LRU_HEAD, LRU_HEAD), whole3), pl.BlockSpec((heads, LRU_HEAD, LRU_HEAD), whole3),
                  pl.BlockSpec((1, c), whole2), pl.BlockSpec((1, c), whole2), pl.BlockSpec((1, c), whole2), ANY],
        out_specs=[pl.BlockSpec((t, c), tile)] * 5,
        out_shape=[jax.ShapeDtypeStruct((s, c), F32)] * 5,
        scratch_shapes=[pltpu.VMEM((t, c), F32), pltpu.VMEM((t, c), F32), pltpu.VMEM((1, c), F32)],
        compiler_params=_params("arbitrary"),
    )(xc, wa, wi, ba, bi, lam, dep)


def _window_counts(r0, tile_idx, t_rows, n_rows, half, shape):
    pos = tile_idx * t_rows + r0 + lax.broadcasted_iota(jnp.int32, shape, 0)
    hi = jnp.minimum(pos + half, n_rows)
    lo = jnp.maximum(pos - half, 0)
    return (hi - lo).astype(F32)


def _pool_combine_call(proj, h_f, h_b, w_pool, pool_scale, p):
    s = proj.shape[0]
    c = h_f.shape[1]
    pg = p // N_POOL_GROUPS
    t = min(SEQ_TILE, s)
    n_tiles = s // t

    def body(u_ref, up_ref, un_ref, gate_ref, hf_ref, hb_ref, wp_ref, sc_ref, y_ref, d_ref, d_s, yr_s):
        i = pl.program_id(0)
        rows = lax.broadcasted_iota(jnp.int32, (SUBLANES, pg), 0)

        def chunk(r0, _):
            rs = pl.ds(r0, SUBLANES)
            for g, w in enumerate(POOL_WINDOWS):
                cols = pl.ds(g * pg, pg)
                prv, cur, nxt = _neighbour_chunks(u_ref, up_ref, un_ref, r0, t, cols, i == 0, i == n_tiles - 1)
                tot = cur
                for o in range(-(w // 2), w // 2):
                    if o != 0:
                        tot = tot + _shift_rows(prv, cur, nxt, o, rows)
                cnt = _window_counts(r0, i, t, s, w // 2, (SUBLANES, pg))
                d_s[rs, cols] = tot / cnt - cur
            gate, _ = _gelu_and_grad(gate_ref[rs, :])
            yr_s[rs, :] = (hf_ref[rs, :] + hb_ref[rs, :]) * gate

        _chunk_loop(t, chunk)
        y_ref[:, pl.ds(p, c)] = yr_s[...].astype(BF16)
        d_ref[...] = d_s[...].astype(BF16)
        for g in range(N_POOL_GROUPS):
            cols = pl.ds(g * pg, pg)
            out = jnp.dot(d_s[:, cols].astype(BF16), wp_ref[g], preferred_element_type=F32)
            y_ref[:, cols] = (out * sc_ref[:, cols]).astype(BF16)

    return pl.pallas_call(
        body, name="pool_combine", grid=(n_tiles,),
        in_specs=_halo_specs(t, s, p, 0) + [
            pl.BlockSpec((t, c), lambda i: (i, 2)),
            pl.BlockSpec((t, c), lambda i: (i, 0)), pl.BlockSpec((t, c), lambda i: (i, 0)),
            pl.BlockSpec((N_POOL_GROUPS, pg, pg), lambda i: (0, 0, 0)), pl.BlockSpec((1, p), lambda i: (0, 0))],
        out_specs=[pl.BlockSpec((t, p + c), lambda i: (i, 0)), pl.BlockSpec((t, p), lambda i: (i, 0))],
        out_shape=[jax.ShapeDtypeStruct((s, p + c), BF16), jax.ShapeDtypeStruct((s, p), BF16)],
        scratch_shapes=[pltpu.VMEM((t, p), F32), pltpu.VMEM((t, c), F32)],
        compiler_params=_params("arbitrary"),
    )(proj, proj, proj, proj, h_f, h_b, w_pool, pool_scale)


def _layer_norm_rows(z, g, b):
    mu = jnp.mean(z, axis=-1, keepdims=True)
    zc = z - mu
    var = jnp.mean(zc * zc, axis=-1, keepdims=True)
    rstd = lax.rsqrt(var + LN_EPS)
    xh = zc * rstd
    return xh, rstd, xh * g + b


def _layer_norm_bwd_rows(dx, xh, rstd, g):
    dxh = dx * g
    m1 = jnp.mean(dxh, axis=-1, keepdims=True)
    m2 = jnp.mean(dxh * xh, axis=-1, keepdims=True)
    return rstd * (dxh - m1 - xh * m2)


def _out_ln1_call(y, w_out, x, g1, b1, dep):
    s, d = x.shape
    tm = min(SEQ_TILE, s)

    def body(y_ref, w_ref, x_ref, g_ref, b_ref, dep_ref, xh_ref, x1b_ref, rstd_ref, acc_s, x1_s):
        acc_s[...] = jnp.dot(y_ref[...], w_ref[...], preferred_element_type=F32)
        g, b = g_ref[...], b_ref[...]

        def chunk(r0, _):
            rs = pl.ds(r0, SUBLANES)
            xh, rstd, x1 = _layer_norm_rows(ALPHA * x_ref[rs, :] + acc_s[rs, :], g, b)
            xh_ref[rs, :] = xh
            x1_s[rs, :] = x1
            rstd_ref[rs, :] = rstd

        _chunk_loop(tm, chunk, unroll=LN_UNROLL)
        x1b_ref[...] = x1_s[...].astype(BF16)

    return pl.pallas_call(
        body, name="out_ln1", grid=(s // tm,),
        in_specs=[pl.BlockSpec((tm, d), lambda i: (i, 0)), pl.BlockSpec((d, d), lambda i: (0, 0)),
                  pl.BlockSpec((tm, d), lambda i: (i, 0)),
                  pl.BlockSpec((1, d), lambda i: (0, 0)), pl.BlockSpec((1, d), lambda i: (0, 0)), ANY],
        out_specs=[pl.BlockSpec((tm, d), lambda i: (i, 0)), pl.BlockSpec((tm, d), lambda i: (i, 0)),
                   pl.BlockSpec((tm, 1), lambda i: (i, 0))],
        out_shape=[jax.ShapeDtypeStruct((s, d), F32), jax.ShapeDtypeStruct((s, d), BF16), jax.ShapeDtypeStruct((s, 1), F32)],
        scratch_shapes=[pltpu.VMEM((tm, d), F32), pltpu.VMEM((tm, d), F32)],
        compiler_params=_params("arbitrary"),
    )(y, w_out, x, g1, b1, dep)


def _mlp_in_call(x1b, w1, dep):
    s, d = x1b.shape
    n, _, f4 = w1.shape
    tm = min(MM_TILE, s)
    tn = min(1024, f4)
    per = f4 // tn

    def body(x_ref, w_ref, dep_ref, r_ref, q_ref):
        r = jnp.maximum(jnp.dot(x_ref[...], w_ref[...], preferred_element_type=F32), 0.0)
        r_ref[...] = r.astype(BF16)
        q_ref[...] = (r * r).astype(BF16)

    return pl.pallas_call(
        body, name="mlp_in", grid=(n * per, s // tm),
        in_specs=[pl.BlockSpec((tm, d), lambda j, i: (i, 0)), pl.BlockSpec((None, d, tn), lambda j, i: (j // per, 0, j % per)), ANY],
        out_specs=[pl.BlockSpec((tm, tn), lambda j, i: (i, j)), pl.BlockSpec((tm, tn), lambda j, i: (i, j))],
        out_shape=[jax.ShapeDtypeStruct((s, n * f4), BF16), jax.ShapeDtypeStruct((s, n * f4), BF16)],
        compiler_params=_params("arbitrary", "arbitrary"),
    )(x1b, w1, dep)


def _mlp_out_ln2_call(hsq, w2, xh1, g1, b1, g2, b2, target):
    s, f = hsq.shape
    d = w2.shape[1]
    tm = min(MM_TILE, s)
    tk = min(1024, f)
    nk = f // tk

    def body(h_ref, w_ref, xh1_ref, g1_ref, b1_ref, g2_ref, b2_ref, t_ref,
             dz_ref, dzb_ref, loss_ref, dg_ref, db_ref, acc_s):
        i, k = pl.program_id(0), pl.program_id(1)

        @pl.when((i == 0) & (k == 0))
        def _():
            loss_ref[...] = jnp.zeros_like(loss_ref)
            dg_ref[...] = jnp.zeros_like(dg_ref)
            db_ref[...] = jnp.zeros_like(db_ref)

        @pl.when(k == 0)
        def _():
            acc_s[...] = jnp.zeros_like(acc_s)

        acc_s[...] += jnp.dot(h_ref[...], w_ref[...], preferred_element_type=F32)

        @pl.when(k == nk - 1)
        def _():
            g1, b1, g2, b2 = g1_ref[...], b1_ref[...], g2_ref[...], b2_ref[...]

            def chunk(r0, _):
                rs = pl.ds(r0, SUBLANES)
                x1 = xh1_ref[rs, :] * g1 + b1
                xh2, rstd, x2 = _layer_norm_rows(ALPHA * x1 + acc_s[rs, :], g2, b2)
                diff = x2 - t_ref[rs, :]
                loss_ref[...] += diff * diff
                dx2 = diff * (1.0 / d)
                dg_ref[...] += dx2 * xh2
                db_ref[...] += dx2
                dz = _layer_norm_bwd_rows(dx2, xh2, rstd, g2)
                dz_ref[rs, :] = dz

            _chunk_loop(tm, chunk, unroll=LN_UNROLL)
            dzb_ref[...] = dz_ref[...].astype(BF16)

    row = lambda i, k: (i, 0)
    vec = lambda i, k: (0, 0)
    return pl.pallas_call(
        body, name="mlp_out_ln2", grid=(s // tm, nk),
        in_specs=[pl.BlockSpec((tm, tk), lambda i, k: (i, k)), pl.BlockSpec((tk, d), lambda i, k: (k, 0)),
                  pl.BlockSpec((tm, d), row), pl.BlockSpec((1, d), vec), pl.BlockSpec((1, d), vec),
                  pl.BlockSpec((1, d), vec), pl.BlockSpec((1, d), vec), pl.BlockSpec((tm, d), row)],
        out_specs=[pl.BlockSpec((tm, d), row), pl.BlockSpec((tm, d), row),
                   pl.BlockSpec((SUBLANES, d), vec), pl.BlockSpec((SUBLANES, d), vec), pl.BlockSpec((SUBLANES, d), vec)],
        out_shape=[jax.ShapeDtypeStruct((s, d), F32), jax.ShapeDtypeStruct((s, d), BF16),
                   jax.ShapeDtypeStruct((SUBLANES, d), F32), jax.ShapeDtypeStruct((SUBLANES, d), F32),
                   jax.ShapeDtypeStruct((SUBLANES, d), F32)],
        scratch_shapes=[pltpu.VMEM((tm, d), F32)],
        compiler_params=_params("arbitrary", "arbitrary"),
    )(hsq, w2, xh1, g1, b1, g2, b2, target)


def _weight_grad_call(a, b, tm, tn, out_shape, out_map, name, dep):
    s, m = a.shape
    n = b.shape[1]
    tk = min(1024, s)

    def body(a_ref, b_ref, dep_ref, o_ref):
        @pl.when(pl.program_id(2) == 0)
        def _():
            o_ref[...] = jnp.zeros_like(o_ref)

        o_ref[...] += lax.dot_general(a_ref[...], b_ref[...], (((0,), (0,)), ((), ())), preferred_element_type=F32)

    return pl.pallas_call(
        body, name=name, grid=(m // tm, n // tn, s // tk),
        in_specs=[pl.BlockSpec((tk, tm), lambda i, j, k: (k, i)), pl.BlockSpec((tk, tn), lambda i, j, k: (k, j)), ANY],
        out_specs=pl.BlockSpec((None, None, tm, tn), lambda i, j, k: out_map(i, j)),
        out_shape=jax.ShapeDtypeStruct(out_shape, F32),
        compiler_params=_params("arbitrary", "arbitrary", "arbitrary"),
    )(a, b, dep)


def _weight_grad_resident_call(a, b, tm, out_shape, out_map, name, dep):
    s, m = a.shape
    n = b.shape[1]

    def body(a_ref, b_ref, dep_ref, o_ref):
        o_ref[...] = lax.dot_general(a_ref[...], b_ref[...], (((0,), (0,)), ((), ())), preferred_element_type=F32)

    return pl.pallas_call(
        body, name=name, grid=(m // tm,),
        in_specs=[pl.BlockSpec((s, tm), lambda i: (0, i)), pl.BlockSpec((s, n), lambda i: (0, 0)), ANY],
        out_specs=pl.BlockSpec((None, None, tm, n), lambda i: out_map(i, 0)),
        out_shape=jax.ShapeDtypeStruct(out_shape, F32),
        compiler_params=_params("arbitrary"),
    )(a, b, dep)


def _row_sharded_grad(a, b, name, dep):
    s, m = a.shape
    n = b.shape[1]
    half_rows = m // (2 * N_CHIPS)
    tm = min(1024, half_rows)
    per = half_rows // tm
    out_shape = (2, N_CHIPS, half_rows, n)
    out_map = lambda i, j: ((i // per) % 2, i // (2 * per), i % per, j)
    if tm < 1024 and s * n * 2 <= RESIDENT_OPERAND_BYTES:
        return _weight_grad_resident_call(a, b, tm, out_shape, out_map, name, dep)
    return _weight_grad_call(a, b, tm, min(1024, n), out_shape, out_map, name, dep)


def _col_sharded_grad(a, b, name, dep):
    m, n = a.shape[1], b.shape[1]
    half_rows, shard_cols = m // 2, n // N_CHIPS
    tm = min(1024, half_rows)
    per_m = half_rows // tm
    tn = shard_cols if shard_cols % 1024 else 1024
    per_n = shard_cols // tn
    return _weight_grad_call(a, b, tm, tn, (2, N_CHIPS, half_rows, shard_cols),
                             lambda i, j: (i // per_m, j // per_n, i % per_m, j % per_n), name, dep)


def _dhsq_call(dzb, w2, r, dep):
    s, d = dzb.shape
    f = w2.shape[0]
    tm = min(MM_TILE, s)
    tn = min(1024, f)

    def body(dz_ref, w_ref, r_ref, dep_ref, o_ref):
        dh = lax.dot_general(dz_ref[...], w_ref[...], (((1,), (1,)), ((), ())), preferred_element_type=F32)
        o_ref[...] = (dh * (2.0 * r_ref[...].astype(F32))).astype(BF16)

    return pl.pallas_call(
        body, name="mlp_dpre", grid=(f // tn, s // tm),
        in_specs=[pl.BlockSpec((tm, d), lambda j, i: (i, 0)), pl.BlockSpec((tn, d), lambda j, i: (j, 0)),
                  pl.BlockSpec((tm, tn), lambda j, i: (i, j)), ANY],
        out_specs=pl.BlockSpec((tm, tn), lambda j, i: (i, j)),
        out_shape=jax.ShapeDtypeStruct((s, f), BF16),
        compiler_params=_params("arbitrary", "arbitrary"),
    )(dzb, w2, r, dep)


def _dx1_ln1_bwd_call(dpre, w1, dz2, xh1, rstd1, g1, dep):
    s, f = dpre.shape
    n, d, f4 = w1.shape
    tm = min(MM_TILE, s)
    tk = min(1024, f4)
    per = f4 // tk
    nk = n * per

    def body(dp_ref, w_ref, dz2_ref, xh_ref, rstd_ref, g_ref, dep_ref, dz_ref, dzb_ref, dg_ref, db_ref, acc_s):
        i, k = pl.program_id(0), pl.program_id(1)

        @pl.when((i == 0) & (k == 0))
        def _():
            dg_ref[...] = jnp.zeros_like(dg_ref)
            db_ref[...] = jnp.zeros_like(db_ref)

        @pl.when(k == 0)
        def _():
            acc_s[...] = jnp.zeros_like(acc_s)

        acc_s[...] += lax.dot_general(dp_ref[...], w_ref[...], (((1,), (1,)), ((), ())), preferred_element_type=F32)

        @pl.when(k == nk - 1)
        def _():
            g = g_ref[...]

            def chunk(r0, _):
                rs = pl.ds(r0, SUBLANES)
                dx1 = acc_s[rs, :] + ALPHA * dz2_ref[rs, :]
                xh = xh_ref[rs, :]
                dg_ref[...] += dx1 * xh
                db_ref[...] += dx1
                dz = _layer_norm_bwd_rows(dx1, xh, rstd_ref[rs, :], g)
                dz_ref[rs, :] = dz

            _chunk_loop(tm, chunk, unroll=LN_UNROLL)
            dzb_ref[...] = dz_ref[...].astype(BF16)

    row = lambda i, k: (i, 0)
    vec = lambda i, k: (0, 0)
    return pl.pallas_call(
        body, name="dx1_ln1_bwd", grid=(s // tm, nk),
        in_specs=[pl.BlockSpec((tm, tk), lambda i, k: (i, k)),
                  pl.BlockSpec((None, d, tk), lambda i, k: (k // per, 0, k % per)),
                  pl.BlockSpec((tm, d), row), pl.BlockSpec((tm, d), row), pl.BlockSpec((tm, 1), row),
                  pl.BlockSpec((1, d), vec), ANY],
        out_specs=[pl.BlockSpec((tm, d), row), pl.BlockSpec((tm, d), row),
                   pl.BlockSpec((SUBLANES, d), vec), pl.BlockSpec((SUBLANES, d), vec)],
        out_shape=[jax.ShapeDtypeStruct((s, d), F32), jax.ShapeDtypeStruct((s, d), BF16),
                   jax.ShapeDtypeStruct((SUBLANES, d), F32), jax.ShapeDtypeStruct((SUBLANES, d), F32)],
        scratch_shapes=[pltpu.VMEM((tm, d), F32)],
        compiler_params=_params("arbitrary", "arbitrary"),
    )(dpre, w1, dz2, xh1, rstd1, g1, dep)


def _dy_call(dzb, w_out, dep):
    s, d = dzb.shape
    e = w_out.shape[0]
    tm = min(MM_TILE, s)

    def body(dz_ref, w_ref, dep_ref, o_ref):
        o_ref[...] = lax.dot_general(dz_ref[...], w_ref[...], (((1,), (1,)), ((), ())), preferred_element_type=F32)

    return pl.pallas_call(
        body, name="dy", grid=(s // tm,),
        in_specs=[pl.BlockSpec((tm, d), lambda i: (i, 0)), pl.BlockSpec((e, d), lambda i: (0, 0)), ANY],
        out_specs=pl.BlockSpec((tm, e), lambda i: (i, 0)),
        out_shape=jax.ShapeDtypeStruct((s, e), F32),
        compiler_params=_params("arbitrary"),
    )(dzb, w_out, dep)


def _mixer_bwd_call(dy, d_pool, proj, h_f, h_b, w_pool, pool_scale, p):
    s = dy.shape[0]
    c = h_f.shape[1]
    pg = p // N_POOL_GROUPS
    t = min(SEQ_TILE, s)
    n_tiles = s // t

    def body(dyp_ref, dyr_ref, d_ref, gate_ref, hf_ref, hb_ref, wp_ref, sc_ref,
             e_ref, dh_ref, dgate_ref, dwp_ref, dsc_ref, dd_s):
        i = pl.program_id(0)

        @pl.when(i == 0)
        def _():
            dwp_ref[...] = jnp.zeros_like(dwp_ref)
            dsc_ref[...] = jnp.zeros_like(dsc_ref)

        for g in range(N_POOL_GROUPS):
            cols = pl.ds(g * pg, pg)
            dg = d_ref[:, cols]
            out = jnp.dot(dg, wp_ref[g], preferred_element_type=F32)
            dyp = dyp_ref[:, cols]
            prod = dyp * out
            dsc_ref[:, cols] += jnp.sum(prod.reshape(t // SUBLANES, SUBLANES, pg), axis=0)
            dout = (dyp * sc_ref[:, cols]).astype(BF16)
            dwp_ref[g] += lax.dot_general(dg, dout, (((0,), (0,)), ((), ())), preferred_element_type=F32)
            dd_s[:, cols] = lax.dot_general(dout, wp_ref[g], (((1,), (1,)), ((), ())), preferred_element_type=F32)

        def chunk(r0, _):
            rs = pl.ds(r0, SUBLANES)
            for g, w in enumerate(POOL_WINDOWS):
                cols = pl.ds(g * pg, pg)
                cnt = _window_counts(r0, i, t, s, w // 2, (SUBLANES, pg))
                e_ref[rs, cols] = dd_s[rs, cols] / cnt
            gate, dgate = _gelu_and_grad(gate_ref[rs, :])
            dyr = dyr_ref[rs, :]
            dh_ref[rs, :] = dyr * gate
            dd_s[rs, :] = dyr * (hf_ref[rs, :] + hb_ref[rs, :]) * dgate

        _chunk_loop(t, chunk)
        dgate_ref[...] = dd_s[...].astype(BF16)

    tile = lambda i: (i, 0)
    return pl.pallas_call(
        body, name="mixer_bwd", grid=(n_tiles,),
        in_specs=[pl.BlockSpec((t, p), tile), pl.BlockSpec((t, c), lambda i: (i, 1)), pl.BlockSpec((t, p), tile),
                  pl.BlockSpec((t, c), lambda i: (i, 2)), pl.BlockSpec((t, c), tile), pl.BlockSpec((t, c), tile),
                  pl.BlockSpec((N_POOL_GROUPS, pg, pg), lambda i: (0, 0, 0)), pl.BlockSpec((1, p), lambda i: (0, 0))],
        out_specs=[pl.BlockSpec((t, p), tile), pl.BlockSpec((t, c), tile), pl.BlockSpec((t, c), tile),
                   pl.BlockSpec((N_POOL_GROUPS, pg, pg), lambda i: (0, 0, 0)), pl.BlockSpec((SUBLANES, p), lambda i: (0, 0))],
        out_shape=[jax.ShapeDtypeStruct((s, p), F32), jax.ShapeDtypeStruct((s, c), F32), jax.ShapeDtypeStruct((s, c), BF16),
                   jax.ShapeDtypeStruct((N_POOL_GROUPS, pg, pg), F32), jax.ShapeDtypeStruct((SUBLANES, p), F32)],
        scratch_shapes=[pltpu.VMEM((t, p), F32)],
        compiler_params=_params("arbitrary"),
    )(dy, dy, d_pool, proj, h_f, h_b, w_pool, pool_scale)


def _scan_bwd_call(xc, dh, h_dir, gates, dxc_prev, wa, wi, lam, reverse, dep):
    s, c = xc.shape
    heads = c // LRU_HEAD
    t = min(SEQ_TILE, s)
    n_tiles = s // t
    per = t // SUBLANES
    last_blk = s // SUBLANES - 1
    tile = (lambda i: (i, 0)) if reverse else (lambda i: (n_tiles - 1 - i, 0))
    if reverse:
        halo = lambda i: (jnp.minimum((i + 1) * per, last_blk), 0)
    else:
        halo = lambda i: (jnp.maximum((n_tiles - 1 - i) * per - 1, 0), 0)
    whole2 = lambda i: (0, 0)
    whole3 = lambda i: (0, 0, 0)
    has_prev = dxc_prev is not None
    n_in = 11 + int(has_prev) + 1

    def body(*refs):
        xc_ref, dh_ref, h_ref, hh_ref, r_ref, ig_ref, a_ref, mult_ref = refs[:8]
        prev_ref = refs[8] if has_prev else None
        wa_ref, wi_ref, lam_ref = refs[n_in - 4:n_in - 1]
        dxc_ref, dwa_ref, dwi_ref, dba_ref, dbi_ref, dsp_ref = refs[n_in:n_in + 6]
        pr_s, pi_s, g_s, carry_s = refs[n_in + 6:]
        step = pl.program_id(0)
        tile_idx = step if reverse else n_tiles - 1 - step

        @pl.when(step == 0)
        def _():
            carry_s[...] = jnp.zeros_like(carry_s)
            dwa_ref[...] = jnp.zeros_like(dwa_ref)
            dwi_ref[...] = jnp.zeros_like(dwi_ref)
            dba_ref[...] = jnp.zeros_like(dba_ref)
            dbi_ref[...] = jnp.zeros_like(dbi_ref)
            dsp_ref[...] = jnp.zeros_like(dsp_ref)

        sp = _softplus(-lam_ref[...])
        rows = lax.broadcasted_iota(jnp.int32, (SUBLANES, c), 0)

        def row(j, carry):
            r = j if reverse else (t - 1 - j)
            gt = dh_ref[pl.ds(r, 1), :] + carry
            g_s[pl.ds(r, 1), :] = gt
            return a_ref[pl.ds(r, 1), :] * gt

        carry_s[...] = lax.fori_loop(0, t, row, carry_s[...], unroll=8)

        def chunk(r0, _):
            rs = pl.ds(r0, SUBLANES)
            xcv = xc_ref[rs, :]
            r, ig, a, mult = r_ref[rs, :], ig_ref[rs, :], a_ref[rs, :], mult_ref[rs, :]
            gt = g_s[rs, :]
            cur = h_ref[rs, :]
            if reverse:
                after = h_ref[pl.ds(pl.multiple_of(jnp.minimum(r0 + SUBLANES, t - SUBLANES), SUBLANES), SUBLANES), :]
                edge = jnp.where(tile_idx == n_tiles - 1, 0.0, hh_ref[...])
                nxt = jnp.where(r0 == t - SUBLANES, edge, after)
                hs = _shift_rows(cur, cur, nxt, 1, rows)
            else:
                before = h_ref[pl.ds(pl.multiple_of(jnp.maximum(r0 - SUBLANES, 0), SUBLANES), SUBLANES), :]
                edge = jnp.where(tile_idx == 0, 0.0, hh_ref[...])
                prv = jnp.where(r0 == 0, edge, before)
                hs = _shift_rows(prv, cur, cur, -1, rows)
            gx = gt * xcv
            dmult = gx * ig
            di = gx * mult
            dlog_a = (gt * hs) * a - dmult * (a * a) / mult
            dr = dlog_a * (-RG_C * sp)
            dsp_ref[...] += dlog_a * (-RG_C * r)
            dpr = dr * r * (1.0 - r)
            dpi = di * ig * (1.0 - ig)
            dba_ref[...] += dpr
            dbi_ref[...] += dpi
            direct = gt * mult * ig
            if has_prev:
                direct = direct + prev_ref[rs, :]
            dxc_ref[rs, :] = direct
            pr_s[rs, :] = dpr
            pi_s[rs, :] = dpi

        _chunk_loop(t, chunk)

        for h in range(heads):
            cs = pl.ds(h * LRU_HEAD, LRU_HEAD)
            xb = xc_ref[:, cs].astype(BF16)
            dprb = pr_s[:, cs].astype(BF16)
            dpib = pi_s[:, cs].astype(BF16)
            dwa_ref[h] += lax.dot_general(xb, dprb, (((0,), (0,)), ((), ())), preferred_element_type=F32)
            dwi_ref[h] += lax.dot_general(xb, dpib, (((0,), (0,)), ((), ())), preferred_element_type=F32)
            dxc_ref[:, cs] += (
                lax.dot_general(dprb, wa_ref[h], (((1,), (1,)), ((), ())), preferred_element_type=F32)
                + lax.dot_general(dpib, wi_ref[h], (((1,), (1,)), ((), ())), preferred_element_type=F32))

    tile_spec = pl.BlockSpec((t, c), tile)
    in_specs = [tile_spec, tile_spec, tile_spec, pl.BlockSpec((SUBLANES, c), halo)] + [tile_spec] * 4
    args = [xc, dh, h_dir, h_dir, *gates]
    if has_prev:
        in_specs.append(tile_spec)
        args.append(dxc_prev)
    in_specs += [pl.BlockSpec((heads, LRU_HEAD, LRU_HEAD), whole3), pl.BlockSpec((heads, LRU_HEAD, LRU_HEAD), whole3),
                 pl.BlockSpec((1, c), whole2), ANY]
    args += [wa, wi, lam, dep]
    assert len(args) == n_in
    return pl.pallas_call(
        body, name="scan_bwd_rev" if reverse else "scan_bwd", grid=(n_tiles,),
        in_specs=in_specs,
        out_specs=[tile_spec,
                   pl.BlockSpec((heads, LRU_HEAD, LRU_HEAD), whole3), pl.BlockSpec((heads, LRU_HEAD, LRU_HEAD), whole3),
                   pl.BlockSpec((SUBLANES, c), whole2), pl.BlockSpec((SUBLANES, c), whole2), pl.BlockSpec((SUBLANES, c), whole2)],
        out_shape=[jax.ShapeDtypeStruct((s, c), F32),
                   jax.ShapeDtypeStruct((heads, LRU_HEAD, LRU_HEAD), F32), jax.ShapeDtypeStruct((heads, LRU_HEAD, LRU_HEAD), F32),
                   jax.ShapeDtypeStruct((SUBLANES, c), F32), jax.ShapeDtypeStruct((SUBLANES, c), F32),
                   jax.ShapeDtypeStruct((SUBLANES, c), F32)],
        scratch_shapes=[pltpu.VMEM((t, c), F32), pltpu.VMEM((t, c), F32), pltpu.VMEM((t, c), F32), pltpu.VMEM((1, c), F32)],
        compiler_params=_params("arbitrary"),
    )(*args)


def _dproj_call(e_pool, dxc, proj, dgate, conv_w, p):
    s, c = dxc.shape
    pg = p // N_POOL_GROUPS
    t = min(SEQ_TILE, s)
    n_tiles = s // t

    def body(e_ref, ep_ref, en_ref, dx_ref, dxp_ref, dxn_ref, u_ref, up_ref, un_ref, dgate_ref, w_ref,
             dproj_ref, dcw_ref, dcb_ref, st_s):
        i = pl.program_id(0)
        first, last = i == 0, i == n_tiles - 1

        @pl.when(first)
        def _():
            dcw_ref[...] = jnp.zeros_like(dcw_ref)
            dcb_ref[...] = jnp.zeros_like(dcb_ref)

        rows_p = lax.broadcasted_iota(jnp.int32, (SUBLANES, pg), 0)
        rows_c = lax.broadcasted_iota(jnp.int32, (SUBLANES, c), 0)
        w = w_ref[...]

        def chunk(r0, _):
            rs = pl.ds(r0, SUBLANES)
            for g, win in enumerate(POOL_WINDOWS):
                cols = pl.ds(g * pg, pg)
                prv, cur, nxt = _neighbour_chunks(e_ref, ep_ref, en_ref, r0, t, cols, first, last)
                tot = cur
                for o in range(-(win // 2) + 1, win // 2 + 1):
                    if o != 0:
                        tot = tot + _shift_rows(prv, cur, nxt, o, rows_p)
                cnt = _window_counts(r0, i, t, s, win // 2, (SUBLANES, pg))
                st_s[rs, cols] = tot - cur * cnt
            prv, cur, nxt = _neighbour_chunks(dx_ref, dxp_ref, dxn_ref, r0, t, slice(None), first, last)
            du = w[1:2] * cur
            du += w[0:1] * _shift_rows(prv, cur, nxt, 1, rows_c)
            du += w[2:3] * _shift_rows(prv, cur, nxt, -1, rows_c)
            du += w[3:4] * _shift_rows(prv, cur, nxt, -2, rows_c)
            st_s[rs, pl.ds(p, c)] = du
            uprv, ucur, unxt = _neighbour_chunks(u_ref, up_ref, un_ref, r0, t, slice(None), first, last)
            dcb_ref[...] += cur
            for j, o in enumerate((-1, 0, 1, 2)):
                dcw_ref[j] += cur * _shift_rows(uprv, ucur, unxt, o, rows_c)

        _chunk_loop(t, chunk)
        dproj_ref[:, pl.ds(0, p + c)] = st_s[...].astype(BF16)
        dproj_ref[:, pl.ds(p + c, c)] = dgate_ref[...]

    return pl.pallas_call(
        body, name="dproj", grid=(n_tiles,),
        in_specs=_halo_specs(t, s, p, 0) + _halo_specs(t, s, c, 0) + _halo_specs(t, s, c, 1) + [
            pl.BlockSpec((t, c), lambda i: (i, 0)), pl.BlockSpec((4, c), lambda i: (0, 0))],
        out_specs=[pl.BlockSpec((t, p + 2 * c), lambda i: (i, 0)),
                   pl.BlockSpec((4, SUBLANES, c), lambda i: (0, 0, 0)), pl.BlockSpec((SUBLANES, c), lambda i: (0, 0))],
        out_shape=[jax.ShapeDtypeStruct((s, p + 2 * c), BF16), jax.ShapeDtypeStruct((4, SUBLANES, c), F32),
                   jax.ShapeDtypeStruct((SUBLANES, c), F32)],
        scratch_shapes=[pltpu.VMEM((t, p + c), F32)],
        compiler_params=_params("arbitrary"),
    )(e_pool, e_pool, e_pool, dxc, dxc, dxc, proj, proj, proj, dgate, conv_w)


def _dx_call(dproj, w_in, dz1, dep):
    s, e = dproj.shape
    n, d, e4 = w_in.shape
    tm = min(MM_TILE, s)

    def body(dp_ref, w_ref, dz_ref, dep_ref, o_ref):
        k = pl.program_id(1)

        @pl.when(k == 0)
        def _():
            o_ref[...] = ALPHA * dz_ref[...]

        o_ref[...] += lax.dot_general(dp_ref[...], w_ref[...], (((1,), (1,)), ((), ())), preferred_element_type=F32)

    return pl.pallas_call(
        body, name="grad_x", grid=(s // tm, n),
        in_specs=[pl.BlockSpec((tm, e4), lambda i, k: (i, k)), pl.BlockSpec((None, d, e4), lambda i, k: (k, 0, 0)),
                  pl.BlockSpec((tm, d), lambda i, k: (i, 0)), ANY],
        out_specs=pl.BlockSpec((tm, d), lambda i, k: (i, 0)),
        out_shape=jax.ShapeDtypeStruct((s, d), F32),
        compiler_params=_params("arbitrary", "arbitrary"),
    )(dproj, w_in, dz1, dep)


def _row_tile(rows, cols, n_arrays):
    limit = max(SUBLANES, ELT_BLOCK_BYTES // (4 * cols * max(1, n_arrays // 4)))
    best = SUBLANES
    for cand in range(SUBLANES, min(rows, limit) + 1, SUBLANES):
        if rows % cand == 0:
            best = cand
    return best if rows % SUBLANES == 0 else rows


def _cast_to_slot_call(a, idx, dtype, name):
    rows, cols = a.shape
    tr = _row_tile(rows, cols, 2)

    def body(idx_ref, a_ref, o_ref):
        o_ref[...] = a_ref[...].astype(dtype)

    return pl.pallas_call(
        body, name=name,
        grid_spec=pltpu.PrefetchScalarGridSpec(
            num_scalar_prefetch=1, grid=(rows // tr,),
            in_specs=[pl.BlockSpec((tr, cols), lambda i, idx_ref: (i, 0))],
            out_specs=pl.BlockSpec((None, tr, cols), lambda i, idx_ref: (idx_ref[1], i, 0))),
        out_shape=jax.ShapeDtypeStruct((N_CHIPS, rows, cols), dtype),
        compiler_params=_params("arbitrary"),
    )(idx, a)


def _add_half_call(g, recv, idx, to_slot, name):
    _, rows, cols = g.shape
    tr = _row_tile(rows, cols, 3)

    def body(idx_ref, g_ref, r_ref, o_ref):
        o_ref[...] = g_ref[...] + r_ref[...]

    if to_slot:
        out_spec = pl.BlockSpec((None, tr, cols), lambda i, idx_ref: (idx_ref[1], i, 0))
        out_shape = jax.ShapeDtypeStruct((N_CHIPS, rows, cols), F32)
    else:
        out_spec = pl.BlockSpec((tr, cols), lambda i, idx_ref: (i, 0))
        out_shape = jax.ShapeDtypeStruct((rows, cols), F32)
    return pl.pallas_call(
        body, name=name,
        grid_spec=pltpu.PrefetchScalarGridSpec(
            num_scalar_prefetch=1, grid=(rows // tr,),
            in_specs=[pl.BlockSpec((None, tr, cols), lambda i, idx_ref: (idx_ref[0], i, 0)),
                      pl.BlockSpec((tr, cols), lambda i, idx_ref: (i, 0))],
            out_specs=out_spec),
        out_shape=out_shape,
        compiler_params=_params("arbitrary"),
    )(idx, g, recv)


def _sum_chips_call(own, recv, idx, name):
    _, rows, cols = recv.shape
    tr = _row_tile(rows, cols, 5)
    out_spec = pl.BlockSpec((None, tr, cols), lambda i, idx_ref: (idx_ref[0], i, 0))
    if own is None:
        def body(idx_ref, r_ref, o_ref):
            o_ref[...] = ((r_ref[0] + r_ref[1]) + r_ref[2]) + r_ref[3]
        in_specs = [pl.BlockSpec((N_CHIPS, tr, cols), lambda i, idx_ref: (0, i, 0))]
        args = (recv,)
    else:
        def body(idx_ref, p_ref, r_ref, o_ref):
            o_ref[...] = ((p_ref[...] + r_ref[0]) + r_ref[1]) + r_ref[2]
        in_specs = [pl.BlockSpec((None, tr, cols), lambda i, idx_ref: (idx_ref[1], i, 0)),
                    pl.BlockSpec((N_CHIPS - 1, tr, cols), lambda i, idx_ref: (0, i, 0))]
        args = (own, recv)
    return pl.pallas_call(
        body, name=name,
        grid_spec=pltpu.PrefetchScalarGridSpec(num_scalar_prefetch=1, grid=(rows // tr,), in_specs=in_specs, out_specs=out_spec),
        out_shape=jax.ShapeDtypeStruct((2, rows, cols), F32),
        compiler_params=_params("arbitrary"),
    )(idx, *args)


def _adamw_call(g, w, m, v, name):
    rows, cols = w.shape
    tr = _row_tile(rows, cols, 7)

    def body(g_ref, w_ref, m_ref, v_ref, d_ref, mo_ref, vo_ref):
        gv = g_ref[...]
        mn = ADAM_B1 * m_ref[...] + (1.0 - ADAM_B1) * gv
        vn = ADAM_B2 * v_ref[...] + (1.0 - ADAM_B2) * (gv * gv)
        m_hat = mn / (1.0 - ADAM_B1 ** ADAM_STEP)
        v_hat = vn / (1.0 - ADAM_B2 ** ADAM_STEP)
        d_ref[...] = -ADAM_LR * (m_hat / (jnp.sqrt(v_hat) + ADAM_EPS) + ADAM_WD * w_ref[...])
        mo_ref[...] = mn
        vo_ref[...] = vn

    spec = pl.BlockSpec((tr, cols), lambda i: (i, 0))
    shape = jax.ShapeDtypeStruct((rows, cols), F32)
    return pl.pallas_call(
        body, name=name, grid=(rows // tr,),
        in_specs=[spec] * 4, out_specs=[spec] * 3, out_shape=[shape] * 3,
        compiler_params=_params("arbitrary"),
    )(g, w, m, v)


def _mesh_place():
    x, y, c = lax.axis_index("x"), lax.axis_index("y"), lax.axis_index("c")
    chips = [(1 - x, y), (x, 1 - y), (1 - x, 1 - y)]
    return x, y, c, chips


def _remote(src, dst, send_sems, recv_sems, idx, device):
    return pltpu.make_async_remote_copy(src_ref=src, dst_ref=dst, send_sem=send_sems.at[idx], recv_sem=recv_sems.at[idx],
                                        device_id=device, device_id_type=MESH)


HBM_SPEC = pl.BlockSpec(memory_space=pltpu.HBM)
SEM_SPEC = pl.BlockSpec(memory_space=pltpu.SEMAPHORE)
ORDERED_EFFECT = pltpu.SideEffectType.DATAFLOW_SIDE_EFFECTING


def _in_hbm(a):
    return pltpu.with_memory_space_constraint(a, pltpu.HBM)


def _start_copies_call(name, bufs, groups, after=None):
    n, g = len(bufs), len(groups)
    extra = [] if after is None else [after]
    first_out = n + len(extra)

    def body(*refs):
        outs = refs[first_out:first_out + n]
        sems = refs[first_out + n:first_out + n + 2 * g]
        token = refs[first_out + n + 2 * g]
        for i, (which, copies_fn, _) in enumerate(groups):
            for mine, _ in copies_fn([outs[w] for w in which], sems[2 * i], sems[2 * i + 1]):
                mine.start()
        token[...] = jnp.zeros_like(token)

    sem_shapes = [pltpu.SemaphoreType.DMA((cnt,)) for _, _, cnt in groups for _ in range(2)]
    res = pl.pallas_call(
        body, name=name,
        in_specs=[HBM_SPEC] * n + [ANY] * len(extra),
        out_specs=[HBM_SPEC] * n + [SEM_SPEC] * (2 * g) + [pl.BlockSpec(memory_space=pltpu.VMEM)],
        out_shape=[pltpu.HBM(a.shape, a.dtype) for a in bufs] + sem_shapes + [jax.ShapeDtypeStruct((SUBLANES, LANES), F32)],
        input_output_aliases={a: a for a in range(n)},
        compiler_params=pltpu.CompilerParams(has_side_effects=ORDERED_EFFECT),
    )(*[_in_hbm(a) for a in bufs], *extra)
    sems = res[n:n + 2 * g]
    return list(res[:n]), [(sems[2 * i], sems[2 * i + 1]) for i in range(g)], res[n + 2 * g]


def _wait_copies_call(name, bufs, sems, copies_fn, after):
    n = len(bufs)

    def body(*refs):
        ins = refs[:n]
        send_sems, recv_sems = refs[n], refs[n + 1]
        for mine, arriving in copies_fn(list(ins), send_sems, recv_sems):
            arriving.wait_recv()
            mine.wait_send()

    res = pl.pallas_call(
        body, name=name,
        in_specs=[HBM_SPEC] * n + [SEM_SPEC, SEM_SPEC, ANY],
        out_specs=[HBM_SPEC] * n,
        out_shape=[pltpu.HBM(a.shape, a.dtype) for a in bufs],
        input_output_aliases={a: a for a in range(n)},
        compiler_params=pltpu.CompilerParams(has_side_effects=ORDERED_EFFECT),
    )(*bufs, sems[0], sems[1], after)
    return list(res)


def _gather_copies(bufs, send_sems, recv_sems):
    x, y, c, chips = _mesh_place()
    k = 2 * x + y
    out = []
    for a, buf in enumerate(bufs):
        for j, (px, py) in enumerate(chips):
            kj = 2 * px + py
            mine = _remote(buf.at[k, c], buf.at[k, c], send_sems, recv_sems, 3 * a + j, (px, py, c))
            arriving = _remote(buf.at[k, c], buf.at[kj, c], send_sems, recv_sems, 3 * a + j, (px, py, c))
            out.append((mine, arriving))
    return out


def _exchange_copies(n_sharded, n_replicated):
    def copies(bufs, send_sems, recv_sems):
        x, y, c, chips = _mesh_place()
        k = 2 * x + y
        sums, lands = bufs[:n_sharded], bufs[n_sharded:2 * n_sharded]
        repl = bufs[2 * n_sharded:]
        out = []
        for j, (px, py) in enumerate(chips):
            kj = 2 * px + py
            for a in range(n_sharded):
                cp = _remote(sums[a].at[kj], lands[a].at[j], send_sems, recv_sems, 3 * a + j, (px, py, c))
                out.append((cp, cp))
            for a in range(n_replicated):
                idx = 3 * (n_sharded + a) + j
                mine = _remote(repl[a].at[k], repl[a].at[k], send_sems, recv_sems, idx, (px, py, c))
                arriving = _remote(repl[a].at[k], repl[a].at[kj], send_sems, recv_sems, idx, (px, py, c))
                out.append((mine, arriving))
        return out
    return copies


def _sibling_copies(n):
    def copies(bufs, send_sems, recv_sems):
        x, y, c, _ = _mesh_place()
        out = []
        for a in range(n):
            cp = _remote(bufs[a].at[1 - c], bufs[n + a], send_sems, recv_sems, a, (x, y, 1 - c))
            out.append((cp, cp))
        return out
    return copies


def _forward_copies(bufs, send_sems, recv_sems):
    x, y, c, chips = _mesh_place()
    out = []
    for a, buf in enumerate(bufs):
        for j, (px, py) in enumerate(chips):
            kj = 2 * px + py
            mine = _remote(buf.at[kj, c], buf.at[kj, c], send_sems, recv_sems, 3 * a + j, (x, y, 1 - c))
            arriving = _remote(buf.at[kj, c], buf.at[kj, 1 - c], send_sems, recv_sems, 3 * a + j, (x, y, 1 - c))
            out.append((mine, arriving))
    return out


def _forward_to_sibling_call(bufs, name):
    n = len(bufs)

    def body(*refs):
        ins, outs = refs[:n], refs[n:2 * n]
        send_sems, recv_sems = refs[2 * n:]
        x, y, c, chips = _mesh_place()
        sibling = (x, y, 1 - c)
        sends = []
        for a in range(n):
            for j, (px, py) in enumerate(chips):
                kj = 2 * px + py
                sends.append(_remote(ins[a].at[kj, c], outs[a].at[kj, c], send_sems, recv_sems, 3 * a + j, sibling))
        for cp in sends:
            cp.start()
        for a in range(n):
            for j, (px, py) in enumerate(chips):
                kj = 2 * px + py
                _remote(ins[a].at[kj, c], outs[a].at[kj, 1 - c], send_sems, recv_sems, 3 * a + j, sibling).wait_recv()
        for cp in sends:
            cp.wait_send()

    return pl.pallas_call(
        body, name=name,
        in_specs=[ANY] * n, out_specs=[ANY] * n,
        out_shape=[jax.ShapeDtypeStruct(a.shape, a.dtype) for a in bufs],
        input_output_aliases={a: a for a in range(n)},
        scratch_shapes=[pltpu.SemaphoreType.DMA((3 * n,)), pltpu.SemaphoreType.DMA((3 * n,))],
    )(*bufs)


def _join_halves_call(bufs, name):
    n = len(bufs)

    def body(*refs):
        ins, outs = refs[:n], refs[n:2 * n]
        send_sems, recv_sems = refs[2 * n:]
        x, y, c, _ = _mesh_place()
        sibling = (x, y, 1 - c)
        copies = [_remote(ins[a].at[c], outs[a].at[c], send_sems, recv_sems, a, sibling) for a in range(n)]
        for cp in copies:
            cp.start()
        for a in range(n):
            _remote(ins[a].at[c], outs[a].at[1 - c], send_sems, recv_sems, a, sibling).wait_recv()
        for cp in copies:
            cp.wait_send()

    return pl.pallas_call(
        body, name=name,
        in_specs=[ANY] * n, out_specs=[ANY] * n,
        out_shape=[jax.ShapeDtypeStruct(a.shape, a.dtype) for a in bufs],
        input_output_aliases={a: a for a in range(n)},
        scratch_shapes=[pltpu.SemaphoreType.DMA((n,)), pltpu.SemaphoreType.DMA((n,))],
    )(*bufs)


def _pack(arrays, rows_multiple):
    flat = jnp.concatenate([a.reshape(-1) for a in arrays])
    per = LANES * rows_multiple
    padded = -(-flat.shape[0] // per) * per
    flat = jnp.pad(flat, (0, padded - flat.shape[0]))
    return flat.reshape(-1, LANES)


def _unpack(packed, shapes):
    flat = packed.reshape(-1)
    out, at = [], 0
    for shp in shapes:
        size = 1
        for dim in shp:
            size *= dim
        out.append(flat[at:at + size].reshape(shp))
        at += size
    return out


def _halves(a):
    return a.reshape((2, a.shape[0] // 2) + a.shape[1:])


def kernel(x, ln_mix_g, ln_mix_b, w_in, w_pool, pool_scale, conv_w, conv_b, w_rg_a, b_rg_a, w_rg_i, b_rg_i, rg_lambda, w_out, ln_ffn_g, ln_ffn_b, w_mlp_in, w_mlp_out, loss_target, m_ln_mix_g, m_ln_mix_b, m_w_in, m_w_pool, m_pool_scale, m_conv_w, m_conv_b, m_w_rg_a, m_b_rg_a, m_w_rg_i, m_b_rg_i, m_rg_lambda, m_w_out, m_ln_ffn_g, m_ln_ffn_b, m_w_mlp_in, m_w_mlp_out, v_ln_mix_g, v_ln_mix_b, v_w_in, v_w_pool, v_pool_scale, v_conv_w, v_conv_b, v_w_rg_a, v_b_rg_a, v_w_rg_i, v_b_rg_i, v_rg_lambda, v_w_out, v_ln_ffn_g, v_ln_ffn_b, v_w_mlp_in, v_w_mlp_out):
    weights = dict(ln_mix_g=ln_mix_g, ln_mix_b=ln_mix_b, w_in=w_in, w_pool=w_pool, pool_scale=pool_scale, conv_w=conv_w,
                   conv_b=conv_b, w_rg_a=w_rg_a, b_rg_a=b_rg_a, w_rg_i=w_rg_i, b_rg_i=b_rg_i, rg_lambda=rg_lambda,
                   w_out=w_out, ln_ffn_g=ln_ffn_g, ln_ffn_b=ln_ffn_b, w_mlp_in=w_mlp_in, w_mlp_out=w_mlp_out)
    m_in = dict(ln_mix_g=m_ln_mix_g, ln_mix_b=m_ln_mix_b, w_in=m_w_in, w_pool=m_w_pool, pool_scale=m_pool_scale,
                conv_w=m_conv_w, conv_b=m_conv_b, w_rg_a=m_w_rg_a, b_rg_a=m_b_rg_a, w_rg_i=m_w_rg_i, b_rg_i=m_b_rg_i,
                rg_lambda=m_rg_lambda, w_out=m_w_out, ln_ffn_g=m_ln_ffn_g, ln_ffn_b=m_ln_ffn_b, w_mlp_in=m_w_mlp_in,
                w_mlp_out=m_w_mlp_out)
    v_in = dict(ln_mix_g=v_ln_mix_g, ln_mix_b=v_ln_mix_b, w_in=v_w_in, w_pool=v_w_pool, pool_scale=v_pool_scale,
                conv_w=v_conv_w, conv_b=v_conv_b, w_rg_a=v_w_rg_a, b_rg_a=v_b_rg_a, w_rg_i=v_w_rg_i, b_rg_i=v_b_rg_i,
                rg_lambda=v_rg_lambda, w_out=v_w_out, ln_ffn_g=v_ln_ffn_g, ln_ffn_b=v_ln_ffn_b, w_mlp_in=v_w_mlp_in,
                w_mlp_out=v_w_mlp_out)
    names = list(weights)

    xs = x[0]
    tgt = loss_target[0]
    s, d = xs.shape
    p = c = d // 2
    pg = p // N_POOL_GROUPS
    heads = c // LRU_HEAD
    core = lax.axis_index("c")
    shard = 2 * lax.axis_index("x") + lax.axis_index("y")

    idx = jnp.stack([core, shard]).astype(jnp.int32)
    small_shard = _pack([conv_w[0], b_rg_a[0], b_rg_i[0], rg_lambda[0]], 2 * SUBLANES)
    to_gather = [(w_in[0], BF16), (w_out[0], BF16), (w_mlp_in[0], BF16), (w_mlp_out[0], BF16),
                 (w_pool[0].reshape(-1, pg), BF16), (small_shard, F32)]

    def slot_view(i):
        a, dt = to_gather[i]
        sl = _cast_to_slot_call(a, idx, dt, f"gather_slot_{i}")
        return sl.reshape(N_CHIPS, 2, sl.shape[1] // 2, sl.shape[2])

    first, later = (0, 4, 5), (1, 2, 3)
    fly_a, sems_a, token_a = _start_copies_call(
        "gather_start_first", [slot_view(i) for i in first], [((0, 1, 2), _gather_copies, 3 * len(first))])
    fly_b, sems_b, g_token = _start_copies_call(
        "gather_start_later", [slot_view(i) for i in later],
        [((0,), _gather_copies, 3), ((1,), _gather_copies, 3), ((2,), _gather_copies, 3)], token_a)
    in_flight = {**dict(zip(first, fly_a)), **dict(zip(later, fly_b))}
    g_sems = [sems_a[0]] + list(sems_b)

    def arrive(which, group, after, tag):
        return _wait_copies_call(f"gather_wait_{tag}", [in_flight[w] for w in which], g_sems[group], _gather_copies, after)

    def pass_on(got, tag):
        flying, sems, token = _start_copies_call(
            f"gather_forward_start_{tag}", got, [(tuple(range(len(got))), _forward_copies, 3 * len(got))])
        return (flying, sems[0], tag), token

    def passed_on(state, after):
        flying, sems, tag = state
        return _wait_copies_call(f"gather_forward_wait_{tag}", flying, sems, _forward_copies, after)

    gathered = [None] * len(to_gather)
    gathered[0], gathered[4], gathered[5] = _forward_to_sibling_call(arrive(first, 0, g_token, "w_in"), "gather_forward_w_in")
    w_in_f = gathered[0].reshape((N_CHIPS,) + w_in.shape[1:])
    w_pool_f = gathered[4].reshape(N_CHIPS, N_POOL_GROUPS, pg // N_CHIPS, pg).transpose(1, 0, 2, 3).reshape(N_POOL_GROUPS, pg, pg)
    c4 = c // N_CHIPS
    small_parts = [_unpack(gathered[5][k].reshape(-1, LANES), [(4, c4), (2, c4), (2, c4), (2, c4)]) for k in range(N_CHIPS)]
    conv_w_f = jnp.concatenate([sp_[0] for sp_ in small_parts], axis=1)
    b_a_f = jnp.concatenate([sp_[1] for sp_ in small_parts], axis=1)
    b_i_f = jnp.concatenate([sp_[2] for sp_ in small_parts], axis=1)
    lam_f = jnp.concatenate([sp_[3] for sp_ in small_parts], axis=1)
    wa_b = w_rg_a[0].astype(BF16)
    wi_b = w_rg_i[0].astype(BF16)

    proj, xb = _proj_call(xs, w_in_f)
    xc = _conv_call(proj, conv_w_f, conv_b, c)
    fwd_w_out, token = pass_on(arrive((1,), 1, xc, "w_out"), "w_out")
    h_b, *gates_b = _scan_fwd_call(xc, wa_b[1], wi_b[1], b_a_f[1:2], b_i_f[1:2], lam_f[1:2], True, token)
    h_f, *gates_f = _scan_fwd_call(xc, wa_b[0], wi_b[0], b_a_f[0:1], b_i_f[0:1], lam_f[0:1], False, token)
    y, d_pool = _pool_combine_call(proj, h_f, h_b, w_pool_f, pool_scale, p)
    w_out_f = passed_on(fwd_w_out, y)[0].reshape(d, d)
    fwd_w1, token = pass_on(arrive((2,), 2, y, "w_mlp_in"), "w_mlp_in")
    xh1, x1b, rstd1 = _out_ln1_call(y, w_out_f, xs, ln_mix_g, ln_mix_b, token)
    w1_f = passed_on(fwd_w1, x1b)[0].reshape((N_CHIPS,) + w_mlp_in.shape[1:])
    fwd_w2, token = pass_on(arrive((3,), 3, x1b, "w_mlp_out"), "w_mlp_out")
    r_act, hsq = _mlp_in_call(x1b, w1_f, token)
    w2_f = passed_on(fwd_w2, hsq)[0].reshape(N_CHIPS * w_mlp_out.shape[1], d)
    dz2, dz2b, loss8, dg2, db2 = _mlp_out_ln2_call(hsq, w2_f, xh1, ln_mix_g, ln_mix_b, ln_ffn_g, ln_ffn_b, tgt)

    def start_siblings(grads, tag, after=None):
        lands = [lax.empty(g.shape[1:], g.dtype) for g in grads]
        copies = _sibling_copies(len(grads))
        flying, sems, token = _start_copies_call(
            f"siblings_start_{tag}", list(grads) + lands, [(tuple(range(2 * len(grads))), copies, len(grads))], after)
        return (flying, sems[0], copies, len(grads), tag), token

    def finish_siblings(state, after, small_at=None):
        flying, sems, copies, n, tag = state
        got = _wait_copies_call(f"siblings_wait_{tag}", flying, sems, copies, after)
        out = []
        for a in range(n):
            g, rv = got[a], got[n + a]
            cols = g.shape[-1]
            part = _add_half_call(g.reshape(2, -1, cols), rv.reshape(-1, cols), idx, a == small_at, f"reduce_add_{tag}_{a}")
            out.append(part if a == small_at else part.reshape((N_CHIPS,) + rv.shape[1:]))
        return out

    def start_exchange(sums, n_repl, tag):
        n_sh = len(sums) - n_repl
        lands = [lax.empty((N_CHIPS - 1,) + a.shape[1:], a.dtype) for a in sums[:n_sh]]
        bufs = sums[:n_sh] + lands + sums[n_sh:]
        copies = _exchange_copies(n_sh, n_repl)
        flying, sems, token = _start_copies_call(
            f"reduce_start_{tag}", bufs, [(tuple(range(len(bufs))), copies, 3 * len(sums))])
        return (flying, sems[0], copies, n_sh, tag), token

    def finish_exchange(state, after):
        flying, sems, copies, n_sh, tag = state
        got = _wait_copies_call(f"reduce_wait_{tag}", flying, sems, copies, after)
        halves = []
        for a in range(n_sh):
            own, land = got[a], got[n_sh + a]
            cols = own.shape[-1]
            total = _sum_chips_call(own.reshape(N_CHIPS, -1, cols), land.reshape(N_CHIPS - 1, -1, cols), idx,
                                    f"reduce_sum_{tag}_{a}")
            halves.append(total.reshape((2,) + own.shape[1:]))
        for a, rp in enumerate(got[2 * n_sh:]):
            halves.append(_sum_chips_call(None, rp, idx, f"reduce_sum_{tag}_r{a}"))
        return _join_halves_call(halves, f"reduce_join_{tag}")

    g_w2 = _row_sharded_grad(hsq, dz2b, "grad_w_mlp_out", g_token)
    sib_w2, token = start_siblings([g_w2], "w2")
    dpre = _dhsq_call(dz2b, w2_f, r_act, token)
    flying_w2, token = start_exchange(finish_siblings(sib_w2, dpre), 0, "w2")
    g_w1 = _col_sharded_grad(x1b, dpre, "grad_w_mlp_in", token)
    sib_w1, token = start_siblings([g_w1], "w1")
    dz1, dz1b, dg1, db1 = _dx1_ln1_bwd_call(dpre, w1_f, dz2, xh1, rstd1, ln_mix_g, token)
    flying_w1, token = start_exchange(finish_siblings(sib_w1, dz1b), 0, "w1")
    g_wout = _row_sharded_grad(y, dz1b, "grad_w_out", token)
    sib_wout, token = start_siblings([g_wout], "w_out")
    dy = _dy_call(dz1b, w_out_f, token)
    e_pool, dh, dgate, g_wpool, g_pscale8 = _mixer_bwd_call(dy, d_pool, proj, h_f, h_b, w_pool_f, pool_scale, p)
    flying_wout, token = start_exchange(finish_siblings(sib_wout, dgate), 0, "w_out")
    dxc0, g_wa0, g_wi0, g_ba0, g_bi0, g_sp0 = _scan_bwd_call(
        xc, dh, h_f, gates_f, None, wa_b[0], wi_b[0], lam_f[0:1], False, token)
    dxc, g_wa1, g_wi1, g_ba1, g_bi1, g_sp1 = _scan_bwd_call(
        xc, dh, h_b, gates_b, dxc0, wa_b[1], wi_b[1], lam_f[1:2], True, token)
    dproj, g_cw8, g_cb8 = _dproj_call(e_pool, dxc, proj, dgate, conv_w_f, p)

    rowsum = lambda a8: jnp.sum(a8, axis=-2)
    g_lam = jnp.stack([rowsum(g_sp0), rowsum(g_sp1)]) * (-_sigmoid(-lam_f))
    small_grads = {
        "ln_mix_g": rowsum(dg1), "ln_mix_b": rowsum(db1), "ln_ffn_g": rowsum(dg2), "ln_ffn_b": rowsum(db2),
        "pool_scale": rowsum(g_pscale8), "conv_b": rowsum(g_cb8),
        "w_rg_a": jnp.stack([g_wa0, g_wa1]), "w_rg_i": jnp.stack([g_wi0, g_wi1]),
        "w_pool": g_wpool, "conv_w": rowsum(g_cw8),
        "b_rg_a": jnp.stack([rowsum(g_ba0), rowsum(g_ba1)]), "b_rg_i": jnp.stack([rowsum(g_bi0), rowsum(g_bi1)]),
        "rg_lambda": g_lam,
    }
    small_names = list(small_grads)
    small_shapes = [small_grads[nm].shape for nm in small_names]
    g_small = _halves(_pack([small_grads[nm] for nm in small_names], 2 * SUBLANES))
    sib_small, token = start_siblings([g_small], "small")
    g_win = _col_sharded_grad(xb, dproj, "grad_w_in", token)
    flying_small, token = start_exchange(finish_siblings(sib_small, g_win, small_at=0), 1, "small")
    sib_win, token = start_siblings([g_win], "w_in", token)
    grad_x = _dx_call(dproj, w_in_f, dz1, token)
    flying_win, token = start_exchange(finish_siblings(sib_win, grad_x), 0, "w_in")

    grad_w, delta_w, new_m, new_v = {}, {}, {}, {}

    def adamw(nm, full):
        w2d = weights[nm][0]
        g2d = full.reshape(w2d.shape)
        dl, mn, vn = _adamw_call(g2d, w2d, m_in[nm][0], v_in[nm][0], f"adamw_{nm}")
        grad_w[nm], delta_w[nm], new_m[nm], new_v[nm] = g2d[None], dl[None], mn[None], vn[None]
        return vn

    last = adamw("w_mlp_out", finish_exchange(flying_w2, token)[0])
    last = adamw("w_mlp_in", finish_exchange(flying_w1, last)[0])
    last = adamw("w_out", finish_exchange(flying_wout, last)[0])

    small_full = dict(zip(small_names, _unpack(finish_exchange(flying_small, last)[0].reshape(-1, LANES), small_shapes)))
    local = dict(small_full)
    local["w_pool"] = lax.dynamic_slice_in_dim(small_full["w_pool"], shard * (pg // N_CHIPS), pg // N_CHIPS, axis=1)
    for nm in ("conv_w", "b_rg_a", "b_rg_i", "rg_lambda"):
        local[nm] = lax.dynamic_slice_in_dim(small_full[nm], shard * c4, c4, axis=1)
    small_w_shapes = [weights[nm].shape for nm in small_names]
    g_pack = _pack([local[nm] for nm in small_names], SUBLANES)
    w_pack = _pack([weights[nm] for nm in small_names], SUBLANES)
    m_pack = _pack([m_in[nm] for nm in small_names], SUBLANES)
    v_pack = _pack([v_in[nm] for nm in small_names], SUBLANES)
    dl_p, mn_p, vn_p = _adamw_call(g_pack, w_pack, m_pack, v_pack, "adamw_small")
    for nm, gl, dl, mn, vn in zip(small_names, _unpack(g_pack, small_w_shapes), _unpack(dl_p, small_w_shapes),
                                  _unpack(mn_p, small_w_shapes), _unpack(vn_p, small_w_shapes)):
        grad_w[nm], delta_w[nm], new_m[nm], new_v[nm] = gl, dl, mn, vn
    adamw("w_in", finish_exchange(flying_win, vn_p)[0])

    loss = lax.psum(jnp.sum(loss8) * (0.5 / d), ("x", "y", "c"))
    return (loss, grad_x[None], *[grad_w[nm] for nm in names], *[delta_w[nm] for nm in names],
            *[new_m[nm] for nm in names], *[new_v[nm] for nm in names])
```

```python
import functools

import jax
import jax.numpy as jnp
from jax import lax
from jax.experimental import pallas as pl
from jax.experimental.pallas import tpu as pltpu

F32 = jnp.float32
BF16 = jnp.bfloat16

N_CHIPS = 4
LANES = 128
SUBLANES = 8
LRU_HEAD = 128
N_POOL_GROUPS = 4
POOL_WINDOWS = (2, 4, 8, 16)
RG_C = 8.0
LN_EPS = 1e-5
ALPHA = 2.0 ** 0.25
ADAM_LR, ADAM_B1, ADAM_B2, ADAM_EPS, ADAM_WD, ADAM_STEP = 0.001, 0.9, 0.999, 1e-08, 0.01, 10
VMEM_LIMIT = 56 * 1024 * 1024
SEQ_TILE = 256
MM_TILE = 512
LN_MM_K = 2048
LN_UNROLL = 4
ELT_BLOCK_BYTES = 2 * 1024 * 1024
RESIDENT_OPERAND_BYTES = 16 * 1024 * 1024
MESH = pl.DeviceIdType.MESH
ANY = pl.BlockSpec(memory_space=pl.ANY)


def _params(*sem):
    return pltpu.CompilerParams(dimension_semantics=sem, vmem_limit_bytes=VMEM_LIMIT)


def _sigmoid(z):
    return 1.0 / (1.0 + jnp.exp(-z))


def _neg_expm1(z):
    series = -(z * (1.0 + z * (0.5 + z * (1.0 / 6.0 + z * (1.0 / 24.0)))))
    return jnp.where(z > -0.01, series, 1.0 - jnp.exp(z))


def _softplus(z):
    return jnp.maximum(z, 0.0) + jnp.log1p(jnp.exp(-jnp.abs(z)))


_GELU_K = 0.7978845608028654
_GELU_C = 0.044715


def _gelu_and_grad(u):
    t = jnp.tanh(_GELU_K * (u + _GELU_C * (u * u * u)))
    g = 0.5 * u * (1.0 + t)
    dg = 0.5 * (1.0 + t) + 0.5 * u * (1.0 - t * t) * (_GELU_K * (1.0 + 3.0 * _GELU_C * u * u))
    return g, dg


def _shift_rows(prv, cur, nxt, o, rows):
    if o == 0:
        return cur
    if o == SUBLANES:
        return nxt
    if o == -SUBLANES:
        return prv
    if o > 0:
        s = SUBLANES - o
        return jnp.where(rows < s, pltpu.roll(cur, s, 0), pltpu.roll(nxt, s, 0))
    p = -o
    return jnp.where(rows >= p, pltpu.roll(cur, p, 0), pltpu.roll(prv, p, 0))


def _neighbour_chunks(main_ref, prev_ref, next_ref, r0, t_rows, cols, first_tile, last_tile):
    cur = main_ref[pl.ds(r0, SUBLANES), cols]
    before = main_ref[pl.ds(pl.multiple_of(jnp.maximum(r0 - SUBLANES, 0), SUBLANES), SUBLANES), cols]
    after = main_ref[pl.ds(pl.multiple_of(jnp.minimum(r0 + SUBLANES, t_rows - SUBLANES), SUBLANES), SUBLANES), cols]
    halo_prev = jnp.where(first_tile, 0.0, prev_ref[:, cols])
    halo_next = jnp.where(last_tile, 0.0, next_ref[:, cols])
    prv = jnp.where(r0 == 0, halo_prev, before)
    nxt = jnp.where(r0 == t_rows - SUBLANES, halo_next, after)
    return prv, cur, nxt


def _halo_specs(t_rows, n_rows, width, col_block):
    per = t_rows // SUBLANES
    last = n_rows // SUBLANES - 1
    return [
        pl.BlockSpec((t_rows, width), lambda i: (i, col_block)),
        pl.BlockSpec((SUBLANES, width), lambda i: (jnp.maximum(i * per - 1, 0), col_block)),
        pl.BlockSpec((SUBLANES, width), lambda i: (jnp.minimum((i + 1) * per, last), col_block)),
    ]


def _chunk_loop(t_rows, fn, init=None, unroll=1, descending=False):
    span = SUBLANES * unroll

    def step(ci, carry):
        base = pl.multiple_of(((t_rows // span - 1 - ci) if descending else ci) * span, span)
        for u in range(unroll):
            carry = fn(base + ((unroll - 1 - u) if descending else u) * SUBLANES, carry)
        return carry
    return lax.fori_loop(0, t_rows // span, step, init)


def _scan_chunk(a, b, h_in, rows, reverse):
    for dist in (1, 2, 4):
        if reverse:
            keep = rows < SUBLANES - dist
            shift = SUBLANES - dist
        else:
            keep = rows >= dist
            shift = dist
        b = a * jnp.where(keep, pltpu.roll(b, shift, 0), 0.0) + b
        a = a * jnp.where(keep, pltpu.roll(a, shift, 0), 1.0)
    return a * h_in + b


def _cast_call(x, dep):
    s, d = x.shape
    tm = min(MM_TILE, s)

    def body(x_ref, dep_ref, o_ref):
        o_ref[...] = x_ref[...].astype(BF16)

    return pl.pallas_call(
        body, name="cast_x", grid=(s // tm,),
        in_specs=[pl.BlockSpec((tm, d), lambda i: (i, 0)), ANY],
        out_specs=pl.BlockSpec((tm, d), lambda i: (i, 0)),
        out_shape=jax.ShapeDtypeStruct((s, d), BF16),
        compiler_params=_params("arbitrary"),
    )(x, dep)


def _proj_call(xb, w_in):
    s, d = xb.shape
    n, _, e4 = w_in.shape
    tm = min(MM_TILE, s)

    def body(x_ref, w_ref, proj_ref):
        proj_ref[...] = jnp.dot(x_ref[...], w_ref[...], preferred_element_type=F32)

    return pl.pallas_call(
        body, name="proj", grid=(s // tm, n),
        in_specs=[pl.BlockSpec((tm, d), lambda i, j: (i, 0)), pl.BlockSpec((None, d, e4), lambda i, j: (j, 0, 0))],
        out_specs=pl.BlockSpec((tm, e4), lambda i, j: (i, j)),
        out_shape=jax.ShapeDtypeStruct((s, n * e4), F32),
        compiler_params=_params("arbitrary", "arbitrary"),
    )(xb, w_in)


def _conv_call(proj, conv_w, conv_b, c):
    s = proj.shape[0]
    t = min(SEQ_TILE, s)
    n_tiles = s // t

    def body(u_ref, up_ref, un_ref, w_ref, b_ref, xc_ref):
        i = pl.program_id(0)
        rows = lax.broadcasted_iota(jnp.int32, (SUBLANES, c), 0)
        w = w_ref[...]
        b = b_ref[...]

        def chunk(r0, _):
            prv, cur, nxt = _neighbour_chunks(u_ref, up_ref, un_ref, r0, t, slice(None), i == 0, i == n_tiles - 1)
            acc = b + w[1:2] * cur
            acc += w[0:1] * _shift_rows(prv, cur, nxt, -1, rows)
            acc += w[2:3] * _shift_rows(prv, cur, nxt, 1, rows)
            acc += w[3:4] * _shift_rows(prv, cur, nxt, 2, rows)
            xc_ref[pl.ds(r0, SUBLANES), :] = acc

        _chunk_loop(t, chunk)

    return pl.pallas_call(
        body, name="conv_fwd", grid=(n_tiles,),
        in_specs=_halo_specs(t, s, c, 1) + [pl.BlockSpec((4, c), lambda i: (0, 0)), pl.BlockSpec((1, c), lambda i: (0, 0))],
        out_specs=pl.BlockSpec((t, c), lambda i: (i, 0)),
        out_shape=jax.ShapeDtypeStruct((s, c), F32),
        compiler_params=_params("arbitrary"),
    )(proj, proj, proj, conv_w, conv_b)


def _gate_matmuls(xc_ref, wa_ref, wi_ref, pr_s, pi_s, heads):
    for h in range(heads):
        cs = pl.ds(h * LRU_HEAD, LRU_HEAD)
        xb = xc_ref[:, cs].astype(BF16)
        pr_s[:, cs] = jnp.dot(xb, wa_ref[h], preferred_element_type=F32)
        pi_s[:, cs] = jnp.dot(xb, wi_ref[h], preferred_element_type=F32)


def _rg_gates(pr, pi, ba, bi, sp):
    r = _sigmoid(pr + ba)
    ig = _sigmoid(pi + bi)
    log_a = (-RG_C * r) * sp
    a = jnp.exp(log_a)
    mult = jnp.sqrt(_neg_expm1(2.0 * log_a))
    return r, ig, a, mult


def _scan_fwd_call(xc, wa, wi, ba, bi, lam, reverse, dep):
    s, c = xc.shape
    heads = c // LRU_HEAD
    t = min(SEQ_TILE, s)
    n_tiles = s // t
    tile = (lambda i: (n_tiles - 1 - i, 0)) if reverse else (lambda i: (i, 0))
    whole2 = lambda i: (0, 0)
    whole3 = lambda i: (0, 0, 0)

    def body(xc_ref, wa_ref, wi_ref, ba_ref, bi_ref, lam_ref, dep_ref, h_ref, r_ref, ig_ref, a_ref, mult_ref,
             pr_s, pi_s, carry_s):
        @pl.when(pl.program_id(0) == 0)
        def _():
            carry_s[...] = jnp.zeros_like(carry_s)

        _gate_matmuls(xc_ref, wa_ref, wi_ref, pr_s, pi_s, heads)
        ba_v, bi_v = ba_ref[...], bi_ref[...]
        sp = _softplus(-lam_ref[...])

        rows = lax.broadcasted_iota(jnp.int32, (SUBLANES, c), 0)

        def chunk(r0, h_in):
            rs = pl.ds(r0, SUBLANES)
            r, ig, a, mult = _rg_gates(pr_s[rs, :], pi_s[rs, :], ba_v, bi_v, sp)
            r_ref[rs, :] = r
            ig_ref[rs, :] = ig
            a_ref[rs, :] = a
            mult_ref[rs, :] = mult
            h = _scan_chunk(a, mult * ig * xc_ref[rs, :], h_in, rows, reverse)
            h_ref[rs, :] = h
            return h[0:1, :] if reverse else h[SUBLANES - 1:SUBLANES, :]

        carry_s[...] = _chunk_loop(t, chunk, carry_s[...], unroll=2, descending=reverse)

    return pl.pallas_call(
        body, name="scan_fwd_rev" if reverse else "scan_fwd", grid=(n_tiles,),
        in_specs=[pl.BlockSpec((t, c), tile),
                  pl.BlockSpec((heads, LRU_HEAD, LRU_HEAD), whole3), pl.BlockSpec((heads, LRU_HEAD, LRU_HEAD), whole3),
                  pl.BlockSpec((1, c), whole2), pl.BlockSpec((1, c), whole2), pl.BlockSpec((1, c), whole2), ANY],
        out_specs=[pl.BlockSpec((t, c), tile)] * 5,
        out_shape=[jax.ShapeDtypeStruct((s, c), F32)] * 5,
        scratch_shapes=[pltpu.VMEM((t, c), F32), pltpu.VMEM((t, c), F32), pltpu.VMEM((1, c), F32)],
        compiler_params=_params("arbitrary"),
    )(xc, wa, wi, ba, bi, lam, dep)


def _window_counts(r0, tile_idx, t_rows, n_rows, half, shape):
    pos = tile_idx * t_rows + r0 + lax.broadcasted_iota(jnp.int32, shape, 0)
    hi = jnp.minimum(pos + half, n_rows)
    lo = jnp.maximum(pos - half, 0)
    return (hi - lo).astype(F32)


def _pool_combine_call(proj, h_f, h_b, w_pool, pool_scale, p):
    s = proj.shape[0]
    c = h_f.shape[1]
    pg = p // N_POOL_GROUPS
    t = min(SEQ_TILE, s)
    n_tiles = s // t

    def body(u_ref, up_ref, un_ref, gate_ref, hf_ref, hb_ref, wp_ref, sc_ref, y_ref, d_ref, d_s, yr_s):
        i = pl.program_id(0)
        rows = lax.broadcasted_iota(jnp.int32, (SUBLANES, pg), 0)

        def chunk(r0, _):
            rs = pl.ds(r0, SUBLANES)
            for g, w in enumerate(POOL_WINDOWS):
                cols = pl.ds(g * pg, pg)
                prv, cur, nxt = _neighbour_chunks(u_ref, up_ref, un_ref, r0, t, cols, i == 0, i == n_tiles - 1)
                tot = cur
                for o in range(-(w // 2), w // 2):
                    if o != 0:
                        tot = tot + _shift_rows(prv, cur, nxt, o, rows)
                cnt = _window_counts(r0, i, t, s, w // 2, (SUBLANES, pg))
                d_s[rs, cols] = tot / cnt - cur
            gate, _ = _gelu_and_grad(gate_ref[rs, :])
            yr_s[rs, :] = (hf_ref[rs, :] + hb_ref[rs, :]) * gate

        _chunk_loop(t, chunk)
        y_ref[:, pl.ds(p, c)] = yr_s[...].astype(BF16)
        d_ref[...] = d_s[...].astype(BF16)
        for g in range(N_POOL_GROUPS):
            cols = pl.ds(g * pg, pg)
            out = jnp.dot(d_s[:, cols].astype(BF16), wp_ref[g], preferred_element_type=F32)
            y_ref[:, cols] = (out * sc_ref[:, cols]).astype(BF16)

    return pl.pallas_call(
        body, name="pool_combine", grid=(n_tiles,),
        in_specs=_halo_specs(t, s, p, 0) + [
            pl.BlockSpec((t, c), lambda i: (i, 2)),
            pl.BlockSpec((t, c), lambda i: (i, 0)), pl.BlockSpec((t, c), lambda i: (i, 0)),
            pl.BlockSpec((N_POOL_GROUPS, pg, pg), lambda i: (0, 0, 0)), pl.BlockSpec((1, p), lambda i: (0, 0))],
        out_specs=[pl.BlockSpec((t, p + c), lambda i: (i, 0)), pl.BlockSpec((t, p), lambda i: (i, 0))],
        out_shape=[jax.ShapeDtypeStruct((s, p + c), BF16), jax.ShapeDtypeStruct((s, p), BF16)],
        scratch_shapes=[pltpu.VMEM((t, p), F32), pltpu.VMEM((t, c), F32)],
        compiler_params=_params("arbitrary"),
    )(proj, proj, proj, proj, h_f, h_b, w_pool, pool_scale)


def _layer_norm_rows(z, g, b):
    mu = jnp.mean(z, axis=-1, keepdims=True)
    zc = z - mu
    var = jnp.mean(zc * zc, axis=-1, keepdims=True)
    rstd = lax.rsqrt(var + LN_EPS)
    xh = zc * rstd
    return xh, rstd, xh * g + b


def _layer_norm_bwd_rows(dx, xh, rstd, g):
    dxh = dx * g
    m1 = jnp.mean(dxh, axis=-1, keepdims=True)
    m2 = jnp.mean(dxh * xh, axis=-1, keepdims=True)
    return rstd * (dxh - m1 - xh * m2)


def _out_ln1_call(y, w_out, x, g1, b1, dep):
    s, d = x.shape
    tm = min(SEQ_TILE, s)

    def body(y_ref, w_ref, x_ref, g_ref, b_ref, dep_ref, xh_ref, x1b_ref, rstd_ref, acc_s, x1_s):
        acc_s[...] = jnp.dot(y_ref[...], w_ref[...], preferred_element_type=F32)
        g, b = g_ref[...], b_ref[...]

        def chunk(r0, _):
            rs = pl.ds(r0, SUBLANES)
            xh, rstd, x1 = _layer_norm_rows(ALPHA * x_ref[rs, :] + acc_s[rs, :], g, b)
            xh_ref[rs, :] = xh
            x1_s[rs, :] = x1
            rstd_ref[rs, :] = rstd

        _chunk_loop(tm, chunk, unroll=LN_UNROLL)
        x1b_ref[...] = x1_s[...].astype(BF16)

    return pl.pallas_call(
        body, name="out_ln1", grid=(s // tm,),
        in_specs=[pl.BlockSpec((tm, d), lambda i: (i, 0)), pl.BlockSpec((d, d), lambda i: (0, 0)),
                  pl.BlockSpec((tm, d), lambda i: (i, 0)),
                  pl.BlockSpec((1, d), lambda i: (0, 0)), pl.BlockSpec((1, d), lambda i: (0, 0)), ANY],
        out_specs=[pl.BlockSpec((tm, d), lambda i: (i, 0)), pl.BlockSpec((tm, d), lambda i: (i, 0)),
                   pl.BlockSpec((tm, 1), lambda i: (i, 0))],
        out_shape=[jax.ShapeDtypeStruct((s, d), F32), jax.ShapeDtypeStruct((s, d), BF16), jax.ShapeDtypeStruct((s, 1), F32)],
        scratch_shapes=[pltpu.VMEM((tm, d), F32), pltpu.VMEM((tm, d), F32)],
        compiler_params=_params("arbitrary"),
    )(y, w_out, x, g1, b1, dep)


def _mlp_in_call(x1b, w1, dep):
    s, d = x1b.shape
    n, _, f4 = w1.shape
    tm = min(MM_TILE, s)
    tn = min(1024, f4)
    per = f4 // tn

    def body(x_ref, w_ref, dep_ref, r_ref, q_ref):
        r = jnp.maximum(jnp.dot(x_ref[...], w_ref[...], preferred_element_type=F32), 0.0)
        r_ref[...] = r.astype(BF16)
        q_ref[...] = (r * r).astype(BF16)

    return pl.pallas_call(
        body, name="mlp_in", grid=(n * per, s // tm),
        in_specs=[pl.BlockSpec((tm, d), lambda j, i: (i, 0)), pl.BlockSpec((None, d, tn), lambda j, i: (j // per, 0, j % per)), ANY],
        out_specs=[pl.BlockSpec((tm, tn), lambda j, i: (i, j)), pl.BlockSpec((tm, tn), lambda j, i: (i, j))],
        out_shape=[jax.ShapeDtypeStruct((s, n * f4), BF16), jax.ShapeDtypeStruct((s, n * f4), BF16)],
        compiler_params=_params("arbitrary", "arbitrary"),
    )(x1b, w1, dep)


def _mlp_out_ln2_call(hsq, w2, xh1, g1, b1, g2, b2, target):
    s, f = hsq.shape
    d = w2.shape[1]
    tm = min(MM_TILE, s)
    tk = min(LN_MM_K, f)
    nk = f // tk

    def body(h_ref, w_ref, xh1_ref, g1_ref, b1_ref, g2_ref, b2_ref, t_ref,
             dz_ref, dzb_ref, loss_ref, dg_ref, db_ref, acc_s):
        i, k = pl.program_id(0), pl.program_id(1)

        @pl.when((i == 0) & (k == 0))
        def _():
            loss_ref[...] = jnp.zeros_like(loss_ref)
            dg_ref[...] = jnp.zeros_like(dg_ref)
            db_ref[...] = jnp.zeros_like(db_ref)

        @pl.when(k == 0)
        def _():
            acc_s[...] = jnp.zeros_like(acc_s)

        acc_s[...] += jnp.dot(h_ref[...], w_ref[...], preferred_element_type=F32)

        @pl.when(k == nk - 1)
        def _():
            g1, b1, g2, b2 = g1_ref[...], b1_ref[...], g2_ref[...], b2_ref[...]

            def chunk(r0, _):
                rs = pl.ds(r0, SUBLANES)
                x1 = xh1_ref[rs, :] * g1 + b1
                xh2, rstd, x2 = _layer_norm_rows(ALPHA * x1 + acc_s[rs, :], g2, b2)
                diff = x2 - t_ref[rs, :]
                loss_ref[...] += diff * diff
                dx2 = diff * (1.0 / d)
                dg_ref[...] += dx2 * xh2
                db_ref[...] += dx2
                dz = _layer_norm_bwd_rows(dx2, xh2, rstd, g2)
                dz_ref[rs, :] = dz

            _chunk_loop(tm, chunk, unroll=LN_UNROLL)
            dzb_ref[...] = dz_ref[...].astype(BF16)

    row = lambda i, k: (i, 0)
    vec = lambda i, k: (0, 0)
    return pl.pallas_call(
        body, name="mlp_out_ln2", grid=(s // tm, nk),
        in_specs=[pl.BlockSpec((tm, tk), lambda i, k: (i, k)), pl.BlockSpec((tk, d), lambda i, k: (k, 0)),
                  pl.BlockSpec((tm, d), row), pl.BlockSpec((1, d), vec), pl.BlockSpec((1, d), vec),
                  pl.BlockSpec((1, d), vec), pl.BlockSpec((1, d), vec), pl.BlockSpec((tm, d), row)],
        out_specs=[pl.BlockSpec((tm, d), row), pl.BlockSpec((tm, d), row),
                   pl.BlockSpec((SUBLANES, d), vec), pl.BlockSpec((SUBLANES, d), vec), pl.BlockSpec((SUBLANES, d), vec)],
        out_shape=[jax.ShapeDtypeStruct((s, d), F32), jax.ShapeDtypeStruct((s, d), BF16),
                   jax.ShapeDtypeStruct((SUBLANES, d), F32), jax.ShapeDtypeStruct((SUBLANES, d), F32),
                   jax.ShapeDtypeStruct((SUBLANES, d), F32)],
        scratch_shapes=[pltpu.VMEM((tm, d), F32)],
        compiler_params=_params("arbitrary", "arbitrary"),
    )(hsq, w2, xh1, g1, b1, g2, b2, target)


def _weight_grad_call(a, b, tm, tn, out_shape, out_map, name, dep):
    s, m = a.shape
    n = b.shape[1]
    tk = min(1024, s)

    def body(a_ref, b_ref, dep_ref, o_ref):
        @pl.when(pl.program_id(2) == 0)
        def _():
            o_ref[...] = jnp.zeros_like(o_ref)

        o_ref[...] += lax.dot_general(a_ref[...], b_ref[...], (((0,), (0,)), ((), ())), preferred_element_type=F32)

    return pl.pallas_call(
        body, name=name, grid=(m // tm, n // tn, s // tk),
        in_specs=[pl.BlockSpec((tk, tm), lambda i, j, k: (k, i)), pl.BlockSpec((tk, tn), lambda i, j, k: (k, j)), ANY],
        out_specs=pl.BlockSpec((None, None, tm, tn), lambda i, j, k: out_map(i, j)),
        out_shape=jax.ShapeDtypeStruct(out_shape, F32),
        compiler_params=_params("arbitrary", "arbitrary", "arbitrary"),
    )(a, b, dep)


def _weight_grad_resident_call(a, b, tm, out_shape, out_map, name, dep):
    s, m = a.shape
    n = b.shape[1]

    def body(a_ref, b_ref, dep_ref, o_ref):
        o_ref[...] = lax.dot_general(a_ref[...], b_ref[...], (((0,), (0,)), ((), ())), preferred_element_type=F32)

    return pl.pallas_call(
        body, name=name, grid=(m // tm,),
        in_specs=[pl.BlockSpec((s, tm), lambda i: (0, i)), pl.BlockSpec((s, n), lambda i: (0, 0)), ANY],
        out_specs=pl.BlockSpec((None, None, tm, n), lambda i: out_map(i, 0)),
        out_shape=jax.ShapeDtypeStruct(out_shape, F32),
        compiler_params=_params("arbitrary"),
    )(a, b, dep)


def _row_sharded_grad(a, b, name, dep):
    s, m = a.shape
    n = b.shape[1]
    half_rows = m // (2 * N_CHIPS)
    tm = min(1024, half_rows)
    per = half_rows // tm
    out_shape = (2, N_CHIPS, half_rows, n)
    out_map = lambda i, j: ((i // per) % 2, i // (2 * per), i % per, j)
    if tm < 1024 and s * n * 2 <= RESIDENT_OPERAND_BYTES:
        return _weight_grad_resident_call(a, b, tm, out_shape, out_map, name, dep)
    return _weight_grad_call(a, b, tm, min(1024, n), out_shape, out_map, name, dep)


def _col_sharded_grad(a, b, name, dep):
    m, n = a.shape[1], b.shape[1]
    half_rows, shard_cols = m // 2, n // N_CHIPS
    tm = min(1024, half_rows)
    per_m = half_rows // tm
    tn = shard_cols if shard_cols % 1024 else 1024
    per_n = shard_cols // tn
    return _weight_grad_call(a, b, tm, tn, (2, N_CHIPS, half_rows, shard_cols),
                             lambda i, j: (i // per_m, j // per_n, i % per_m, j % per_n), name, dep)


def _dhsq_call(dzb, w2, r, dep):
    s, d = dzb.shape
    f = w2.shape[0]
    tm = min(MM_TILE, s)
    tn = min(1024, f)

    def body(dz_ref, w_ref, r_ref, dep_ref, o_ref):
        dh = lax.dot_general(dz_ref[...], w_ref[...], (((1,), (1,)), ((), ())), preferred_element_type=F32)
        o_ref[...] = (dh * (2.0 * r_ref[...].astype(F32))).astype(BF16)

    return pl.pallas_call(
        body, name="mlp_dpre", grid=(f // tn, s // tm),
        in_specs=[pl.BlockSpec((tm, d), lambda j, i: (i, 0)), pl.BlockSpec((tn, d), lambda j, i: (j, 0)),
                  pl.BlockSpec((tm, tn), lambda j, i: (i, j)), ANY],
        out_specs=pl.BlockSpec((tm, tn), lambda j, i: (i, j)),
        out_shape=jax.ShapeDtypeStruct((s, f), BF16),
        compiler_params=_params("arbitrary", "arbitrary"),
    )(dzb, w2, r, dep)


def _dx1_ln1_bwd_call(dpre, w1, dz2, xh1, rstd1, g1, dep):
    s, f = dpre.shape
    n, d, f4 = w1.shape
    tm = min(MM_TILE, s)
    tk = min(LN_MM_K, f4)
    per = f4 // tk
    nk = n * per

    def body(dp_ref, w_ref, dz2_ref, xh_ref, rstd_ref, g_ref, dep_ref, dz_ref, dzb_ref, dg_ref, db_ref, acc_s):
        i, k = pl.program_id(0), pl.program_id(1)

        @pl.when((i == 0) & (k == 0))
        def _():
            dg_ref[...] = jnp.zeros_like(dg_ref)
            db_ref[...] = jnp.zeros_like(db_ref)

        @pl.when(k == 0)
        def _():
            acc_s[...] = jnp.zeros_like(acc_s)

        acc_s[...] += lax.dot_general(dp_ref[...], w_ref[...], (((1,), (1,)), ((), ())), preferred_element_type=F32)

        @pl.when(k == nk - 1)
        def _():
            g = g_ref[...]

            def chunk(r0, _):
                rs = pl.ds(r0, SUBLANES)
                dx1 = acc_s[rs, :] + ALPHA * dz2_ref[rs, :]
                xh = xh_ref[rs, :]
                dg_ref[...] += dx1 * xh
                db_ref[...] += dx1
                dz = _layer_norm_bwd_rows(dx1, xh, rstd_ref[rs, :], g)
                dz_ref[rs, :] = dz

            _chunk_loop(tm, chunk, unroll=LN_UNROLL)
            dzb_ref[...] = dz_ref[...].astype(BF16)

    row = lambda i, k: (i, 0)
    vec = lambda i, k: (0, 0)
    return pl.pallas_call(
        body, name="dx1_ln1_bwd", grid=(s // tm, nk),
        in_specs=[pl.BlockSpec((tm, tk), lambda i, k: (i, k)),
                  pl.BlockSpec((None, d, tk), lambda i, k: (k // per, 0, k % per)),
                  pl.BlockSpec((tm, d), row), pl.BlockSpec((tm, d), row), pl.BlockSpec((tm, 1), row),
                  pl.BlockSpec((1, d), vec), ANY],
        out_specs=[pl.BlockSpec((tm, d), row), pl.BlockSpec((tm, d), row),
                   pl.BlockSpec((SUBLANES, d), vec), pl.BlockSpec((SUBLANES, d), vec)],
        out_shape=[jax.ShapeDtypeStruct((s, d), F32), jax.ShapeDtypeStruct((s, d), BF16),
                   jax.ShapeDtypeStruct((SUBLANES, d), F32), jax.ShapeDtypeStruct((SUBLANES, d), F32)],
        scratch_shapes=[pltpu.VMEM((tm, d), F32)],
        compiler_params=_params("arbitrary", "arbitrary"),
    )(dpre, w1, dz2, xh1, rstd1, g1, dep)


def _dy_call(dzb, w_out, dep):
    s, d = dzb.shape
    e = w_out.shape[0]
    tm = min(MM_TILE, s)

    def body(dz_ref, w_ref, dep_ref, o_ref):
        o_ref[...] = lax.dot_general(dz_ref[...], w_ref[...], (((1,), (1,)), ((), ())), preferred_element_type=F32)

    return pl.pallas_call(
        body, name="dy", grid=(s // tm,),
        in_specs=[pl.BlockSpec((tm, d), lambda i: (i, 0)), pl.BlockSpec((e, d), lambda i: (0, 0)), ANY],
        out_specs=pl.BlockSpec((tm, e), lambda i: (i, 0)),
        out_shape=jax.ShapeDtypeStruct((s, e), F32),
        compiler_params=_params("arbitrary"),
    )(dzb, w_out, dep)


def _mixer_bwd_call(dy, d_pool, proj, h_f, h_b, w_pool, pool_scale, p):
    s = dy.shape[0]
    c = h_f.shape[1]
    pg = p // N_POOL_GROUPS
    t = min(SEQ_TILE, s)
    n_tiles = s // t

    def body(dyp_ref, dyr_ref, d_ref, gate_ref, hf_ref, hb_ref, wp_ref, sc_ref,
             e_ref, dh_ref, dgate_ref, dwp_ref, dsc_ref, dd_s):
        i = pl.program_id(0)

        @pl.when(i == 0)
        def _():
            dwp_ref[...] = jnp.zeros_like(dwp_ref)
            dsc_ref[...] = jnp.zeros_like(dsc_ref)

        for g in range(N_POOL_GROUPS):
            cols = pl.ds(g * pg, pg)
            dg = d_ref[:, cols]
            out = jnp.dot(dg, wp_ref[g], preferred_element_type=F32)
            dyp = dyp_ref[:, cols]
            prod = dyp * out
            dsc_ref[:, cols] += jnp.sum(prod.reshape(t // SUBLANES, SUBLANES, pg), axis=0)
            dout = (dyp * sc_ref[:, cols]).astype(BF16)
            dwp_ref[g] += lax.dot_general(dg, dout, (((0,), (0,)), ((), ())), preferred_element_type=F32)
            dd_s[:, cols] = lax.dot_general(dout, wp_ref[g], (((1,), (1,)), ((), ())), preferred_element_type=F32)

        def chunk(r0, _):
            rs = pl.ds(r0, SUBLANES)
            for g, w in enumerate(POOL_WINDOWS):
                cols = pl.ds(g * pg, pg)
                cnt = _window_counts(r0, i, t, s, w // 2, (SUBLANES, pg))
                e_ref[rs, cols] = dd_s[rs, cols] / cnt
            gate, dgate = _gelu_and_grad(gate_ref[rs, :])
            dyr = dyr_ref[rs, :]
            dh_ref[rs, :] = dyr * gate
            dd_s[rs, :] = dyr * (hf_ref[rs, :] + hb_ref[rs, :]) * dgate

        _chunk_loop(t, chunk)
        dgate_ref[...] = dd_s[...].astype(BF16)

    tile = lambda i: (i, 0)
    return pl.pallas_call(
        body, name="mixer_bwd", grid=(n_tiles,),
        in_specs=[pl.BlockSpec((t, p), tile), pl.BlockSpec((t, c), lambda i: (i, 1)), pl.BlockSpec((t, p), tile),
                  pl.BlockSpec((t, c), lambda i: (i, 2)), pl.BlockSpec((t, c), tile), pl.BlockSpec((t, c), tile),
                  pl.BlockSpec((N_POOL_GROUPS, pg, pg), lambda i: (0, 0, 0)), pl.BlockSpec((1, p), lambda i: (0, 0))],
        out_specs=[pl.BlockSpec((t, p), tile), pl.BlockSpec((t, c), tile), pl.BlockSpec((t, c), tile),
                   pl.BlockSpec((N_POOL_GROUPS, pg, pg), lambda i: (0, 0, 0)), pl.BlockSpec((SUBLANES, p), lambda i: (0, 0))],
        out_shape=[jax.ShapeDtypeStruct((s, p), F32), jax.ShapeDtypeStruct((s, c), F32), jax.ShapeDtypeStruct((s, c), BF16),
                   jax.ShapeDtypeStruct((N_POOL_GROUPS, pg, pg), F32), jax.ShapeDtypeStruct((SUBLANES, p), F32)],
        scratch_shapes=[pltpu.VMEM((t, p), F32)],
        compiler_params=_params("arbitrary"),
    )(dy, dy, d_pool, proj, h_f, h_b, w_pool, pool_scale)


def _scan_bwd_call(xc, dh, h_dir, gates, dxc_prev, wa, wi, lam, reverse, dep):
    s, c = xc.shape
    heads = c // LRU_HEAD
    t = min(SEQ_TILE, s)
    n_tiles = s // t
    per = t // SUBLANES
    last_blk = s // SUBLANES - 1
    tile = (lambda i: (i, 0)) if reverse else (lambda i: (n_tiles - 1 - i, 0))
    if reverse:
        halo = lambda i: (jnp.minimum((i + 1) * per, last_blk), 0)
    else:
        halo = lambda i: (jnp.maximum((n_tiles - 1 - i) * per - 1, 0), 0)
    whole2 = lambda i: (0, 0)
    whole3 = lambda i: (0, 0, 0)
    has_prev = dxc_prev is not None
    n_in = 11 + int(has_prev) + 1

    def body(*refs):
        xc_ref, dh_ref, h_ref, hh_ref, r_ref, ig_ref, a_ref, mult_ref = refs[:8]
        prev_ref = refs[8] if has_prev else None
        wa_ref, wi_ref, lam_ref = refs[n_in - 4:n_in - 1]
        dxc_ref, dwa_ref, dwi_ref, dba_ref, dbi_ref, dsp_ref = refs[n_in:n_in + 6]
        pr_s, pi_s, carry_s = refs[n_in + 6:]
        step = pl.program_id(0)
        tile_idx = step if reverse else n_tiles - 1 - step

        @pl.when(step == 0)
        def _():
            carry_s[...] = jnp.zeros_like(carry_s)
            dwa_ref[...] = jnp.zeros_like(dwa_ref)
            dwi_ref[...] = jnp.zeros_like(dwi_ref)
            dba_ref[...] = jnp.zeros_like(dba_ref)
            dbi_ref[...] = jnp.zeros_like(dbi_ref)
            dsp_ref[...] = jnp.zeros_like(dsp_ref)

        sp = _softplus(-lam_ref[...])
        rows = lax.broadcasted_iota(jnp.int32, (SUBLANES, c), 0)

        def chunk(r0, u_in):
            rs = pl.ds(r0, SUBLANES)
            xcv = xc_ref[rs, :]
            r, ig, a, mult = r_ref[rs, :], ig_ref[rs, :], a_ref[rs, :], mult_ref[rs, :]
            dhv = dh_ref[rs, :]
            u = _scan_chunk(a, a * dhv, u_in, rows, not reverse)
            if reverse:
                gt = dhv + jnp.where(rows >= 1, pltpu.roll(u, 1, 0), u_in)
                u_out = u[SUBLANES - 1:SUBLANES, :]
            else:
                gt = dhv + jnp.where(rows < SUBLANES - 1, pltpu.roll(u, SUBLANES - 1, 0), u_in)
                u_out = u[0:1, :]
            cur = h_ref[rs, :]
            if reverse:
                after = h_ref[pl.ds(pl.multiple_of(jnp.minimum(r0 + SUBLANES, t - SUBLANES), SUBLANES), SUBLANES), :]
                edge = jnp.where(tile_idx == n_tiles - 1, 0.0, hh_ref[...])
                nxt = jnp.where(r0 == t - SUBLANES, edge, after)
                hs = _shift_rows(cur, cur, nxt, 1, rows)
            else:
                before = h_ref[pl.ds(pl.multiple_of(jnp.maximum(r0 - SUBLANES, 0), SUBLANES), SUBLANES), :]
                edge = jnp.where(tile_idx == 0, 0.0, hh_ref[...])
                prv = jnp.where(r0 == 0, edge, before)
                hs = _shift_rows(prv, cur, cur, -1, rows)
            gx = gt * xcv
            dmult = gx * ig
            di = gx * mult
            dlog_a = (gt * hs) * a - dmult * (a * a) / mult
            dr = dlog_a * (-RG_C * sp)
            dsp_ref[...] += dlog_a * (-RG_C * r)
            dpr = dr * r * (1.0 - r)
            dpi = di * ig * (1.0 - ig)
            dba_ref[...] += dpr
            dbi_ref[...] += dpi
            direct = gt * mult * ig
            if has_prev:
                direct = direct + prev_ref[rs, :]
            dxc_ref[rs, :] = direct
            pr_s[rs, :] = dpr
            pi_s[rs, :] = dpi
            return u_out

        carry_s[...] = _chunk_loop(t, chunk, carry_s[...], unroll=2, descending=not reverse)

        for h in range(heads):
            cs = pl.ds(h * LRU_HEAD, LRU_HEAD)
            xb = xc_ref[:, cs].astype(BF16)
            dprb = pr_s[:, cs].astype(BF16)
            dpib = pi_s[:, cs].astype(BF16)
            dwa_ref[h] += lax.dot_general(xb, dprb, (((0,), (0,)), ((), ())), preferred_element_type=F32)
            dwi_ref[h] += lax.dot_general(xb, dpib, (((0,), (0,)), ((), ())), preferred_element_type=F32)
            dxc_ref[:, cs] += (
                lax.dot_general(dprb, wa_ref[h], (((1,), (1,)), ((), ())), preferred_element_type=F32)
                + lax.dot_general(dpib, wi_ref[h], (((1,), (1,)), ((), ())), preferred_element_type=F32))

    tile_spec = pl.BlockSpec((t, c), tile)
    in_specs = [tile_spec, tile_spec, tile_spec, pl.BlockSpec((SUBLANES, c), halo)] + [tile_spec] * 4
    args = [xc, dh, h_dir, h_dir, *gates]
    if has_prev:
        in_specs.append(tile_spec)
        args.append(dxc_prev)
    in_specs += [pl.BlockSpec((heads, LRU_HEAD, LRU_HEAD), whole3), pl.BlockSpec((heads, LRU_HEAD, LRU_HEAD), whole3),
                 pl.BlockSpec((1, c), whole2), ANY]
    args += [wa, wi, lam, dep]
    assert len(args) == n_in
    return pl.pallas_call(
        body, name="scan_bwd_rev" if reverse else "scan_bwd", grid=(n_tiles,),
        in_specs=in_specs,
        out_specs=[tile_spec,
                   pl.BlockSpec((heads, LRU_HEAD, LRU_HEAD), whole3), pl.BlockSpec((heads, LRU_HEAD, LRU_HEAD), whole3),
                   pl.BlockSpec((SUBLANES, c), whole2), pl.BlockSpec((SUBLANES, c), whole2), pl.BlockSpec((SUBLANES, c), whole2)],
        out_shape=[jax.ShapeDtypeStruct((s, c), F32),
                   jax.ShapeDtypeStruct((heads, LRU_HEAD, LRU_HEAD), F32), jax.ShapeDtypeStruct((heads, LRU_HEAD, LRU_HEAD), F32),
                   jax.ShapeDtypeStruct((SUBLANES, c), F32), jax.ShapeDtypeStruct((SUBLANES, c), F32),
                   jax.ShapeDtypeStruct((SUBLANES, c), F32)],
        scratch_shapes=[pltpu.VMEM((t, c), F32), pltpu.VMEM((t, c), F32), pltpu.VMEM((1, c), F32)],
        compiler_params=_params("arbitrary"),
    )(*args)


def _dproj_call(e_pool, dxc, proj, dgate, conv_w, p):
    s, c = dxc.shape
    pg = p // N_POOL_GROUPS
    t = min(SEQ_TILE, s)
    n_tiles = s // t

    def body(e_ref, ep_ref, en_ref, dx_ref, dxp_ref, dxn_ref, u_ref, up_ref, un_ref, dgate_ref, w_ref,
             dproj_ref, dcw_ref, dcb_ref, st_s):
        i = pl.program_id(0)
        first, last = i == 0, i == n_tiles - 1

        @pl.when(first)
        def _():
            dcw_ref[...] = jnp.zeros_like(dcw_ref)
            dcb_ref[...] = jnp.zeros_like(dcb_ref)

        rows_p = lax.broadcasted_iota(jnp.int32, (SUBLANES, pg), 0)
        rows_c = lax.broadcasted_iota(jnp.int32, (SUBLANES, c), 0)
        w = w_ref[...]

        def chunk(r0, _):
            rs = pl.ds(r0, SUBLANES)
            for g, win in enumerate(POOL_WINDOWS):
                cols = pl.ds(g * pg, pg)
                prv, cur, nxt = _neighbour_chunks(e_ref, ep_ref, en_ref, r0, t, cols, first, last)
                tot = cur
                for o in range(-(win // 2) + 1, win // 2 + 1):
                    if o != 0:
                        tot = tot + _shift_rows(prv, cur, nxt, o, rows_p)
                cnt = _window_counts(r0, i, t, s, win // 2, (SUBLANES, pg))
                st_s[rs, cols] = tot - cur * cnt
            prv, cur, nxt = _neighbour_chunks(dx_ref, dxp_ref, dxn_ref, r0, t, slice(None), first, last)
            du = w[1:2] * cur
            du += w[0:1] * _shift_rows(prv, cur, nxt, 1, rows_c)
            du += w[2:3] * _shift_rows(prv, cur, nxt, -1, rows_c)
            du += w[3:4] * _shift_rows(prv, cur, nxt, -2, rows_c)
            st_s[rs, pl.ds(p, c)] = du
            uprv, ucur, unxt = _neighbour_chunks(u_ref, up_ref, un_ref, r0, t, slice(None), first, last)
            dcb_ref[...] += cur
            for j, o in enumerate((-1, 0, 1, 2)):
                dcw_ref[j] += cur * _shift_rows(uprv, ucur, unxt, o, rows_c)

        _chunk_loop(t, chunk)
        dproj_ref[:, pl.ds(0, p + c)] = st_s[...].astype(BF16)
        dproj_ref[:, pl.ds(p + c, c)] = dgate_ref[...]

    return pl.pallas_call(
        body, name="dproj", grid=(n_tiles,),
        in_specs=_halo_specs(t, s, p, 0) + _halo_specs(t, s, c, 0) + _halo_specs(t, s, c, 1) + [
            pl.BlockSpec((t, c), lambda i: (i, 0)), pl.BlockSpec((4, c), lambda i: (0, 0))],
        out_specs=[pl.BlockSpec((t, p + 2 * c), lambda i: (i, 0)),
                   pl.BlockSpec((4, SUBLANES, c), lambda i: (0, 0, 0)), pl.BlockSpec((SUBLANES, c), lambda i: (0, 0))],
        out_shape=[jax.ShapeDtypeStruct((s, p + 2 * c), BF16), jax.ShapeDtypeStruct((4, SUBLANES, c), F32),
                   jax.ShapeDtypeStruct((SUBLANES, c), F32)],
        scratch_shapes=[pltpu.VMEM((t, p + c), F32)],
        compiler_params=_params("arbitrary"),
    )(e_pool, e_pool, e_pool, dxc, dxc, dxc, proj, proj, proj, dgate, conv_w)


def _dx_call(dproj, w_in, dz1, dep):
    s, e = dproj.shape
    n, d, e4 = w_in.shape
    tm = min(MM_TILE, s)

    def body(dp_ref, w_ref, dz_ref, dep_ref, o_ref):
        k = pl.program_id(1)

        @pl.when(k == 0)
        def _():
            o_ref[...] = ALPHA * dz_ref[...]

        o_ref[...] += lax.dot_general(dp_ref[...], w_ref[...], (((1,), (1,)), ((), ())), preferred_element_type=F32)

    return pl.pallas_call(
        body, name="grad_x", grid=(s // tm, n),
        in_specs=[pl.BlockSpec((tm, e4), lambda i, k: (i, k)), pl.BlockSpec((None, d, e4), lambda i, k: (k, 0, 0)),
                  pl.BlockSpec((tm, d), lambda i, k: (i, 0)), ANY],
        out_specs=pl.BlockSpec((tm, d), lambda i, k: (i, 0)),
        out_shape=jax.ShapeDtypeStruct((s, d), F32),
        compiler_params=_params("arbitrary", "arbitrary"),
    )(dproj, w_in, dz1, dep)


def _row_tile(rows, cols, n_arrays):
    limit = max(SUBLANES, ELT_BLOCK_BYTES // (4 * cols * max(1, n_arrays // 4)))
    best = SUBLANES
    for cand in range(SUBLANES, min(rows, limit) + 1, SUBLANES):
        if rows % cand == 0:
            best = cand
    return best if rows % SUBLANES == 0 else rows


def _cast_to_slot_call(a, idx, dtype, name):
    rows, cols = a.shape
    tr = _row_tile(rows, cols, 2)

    def body(idx_ref, a_ref, o_ref):
        o_ref[...] = a_ref[...].astype(dtype)

    return pl.pallas_call(
        body, name=name,
        grid_spec=pltpu.PrefetchScalarGridSpec(
            num_scalar_prefetch=1, grid=(rows // tr,),
            in_specs=[pl.BlockSpec((tr, cols), lambda i, idx_ref: (i, 0))],
            out_specs=pl.BlockSpec((None, tr, cols), lambda i, idx_ref: (idx_ref[1], i, 0))),
        out_shape=jax.ShapeDtypeStruct((N_CHIPS, rows, cols), dtype),
        compiler_params=_params("arbitrary"),
    )(idx, a)


def _add_half_call(g, recv, idx, to_slot, name):
    _, rows, cols = g.shape
    tr = _row_tile(rows, cols, 3)

    def body(idx_ref, g_ref, r_ref, o_ref):
        o_ref[...] = g_ref[...] + r_ref[...]

    if to_slot:
        out_spec = pl.BlockSpec((None, tr, cols), lambda i, idx_ref: (idx_ref[1], i, 0))
        out_shape = jax.ShapeDtypeStruct((N_CHIPS, rows, cols), F32)
    else:
        out_spec = pl.BlockSpec((tr, cols), lambda i, idx_ref: (i, 0))
        out_shape = jax.ShapeDtypeStruct((rows, cols), F32)
    return pl.pallas_call(
        body, name=name,
        grid_spec=pltpu.PrefetchScalarGridSpec(
            num_scalar_prefetch=1, grid=(rows // tr,),
            in_specs=[pl.BlockSpec((None, tr, cols), lambda i, idx_ref: (idx_ref[0], i, 0)),
                      pl.BlockSpec((tr, cols), lambda i, idx_ref: (i, 0))],
            out_specs=out_spec),
        out_shape=out_shape,
        compiler_params=_params("arbitrary"),
    )(idx, g, recv)


def _sum_chips_call(own, recv, idx, name):
    _, rows, cols = recv.shape
    tr = _row_tile(rows, cols, 5)
    out_spec = pl.BlockSpec((None, tr, cols), lambda i, idx_ref: (idx_ref[0], i, 0))
    if own is None:
        def body(idx_ref, r_ref, o_ref):
            o_ref[...] = ((r_ref[0] + r_ref[1]) + r_ref[2]) + r_ref[3]
        in_specs = [pl.BlockSpec((N_CHIPS, tr, cols), lambda i, idx_ref: (0, i, 0))]
        args = (recv,)
    else:
        def body(idx_ref, p_ref, r_ref, o_ref):
            o_ref[...] = ((p_ref[...] + r_ref[0]) + r_ref[1]) + r_ref[2]
        in_specs = [pl.BlockSpec((None, tr, cols), lambda i, idx_ref: (idx_ref[1], i, 0)),
                    pl.BlockSpec((N_CHIPS - 1, tr, cols), lambda i, idx_ref: (0, i, 0))]
        args = (own, recv)
    return pl.pallas_call(
        body, name=name,
        grid_spec=pltpu.PrefetchScalarGridSpec(num_scalar_prefetch=1, grid=(rows // tr,), in_specs=in_specs, out_specs=out_spec),
        out_shape=jax.ShapeDtypeStruct((2, rows, cols), F32),
        compiler_params=_params("arbitrary"),
    )(idx, *args)


def _adamw_call(g, w, m, v, name):
    rows, cols = w.shape
    tr = _row_tile(rows, cols, 7)

    def body(g_ref, w_ref, m_ref, v_ref, d_ref, mo_ref, vo_ref):
        gv = g_ref[...]
        mn = ADAM_B1 * m_ref[...] + (1.0 - ADAM_B1) * gv
        vn = ADAM_B2 * v_ref[...] + (1.0 - ADAM_B2) * (gv * gv)
        m_hat = mn / (1.0 - ADAM_B1 ** ADAM_STEP)
        v_hat = vn / (1.0 - ADAM_B2 ** ADAM_STEP)
        d_ref[...] = -ADAM_LR * (m_hat / (jnp.sqrt(v_hat) + ADAM_EPS) + ADAM_WD * w_ref[...])
        mo_ref[...] = mn
        vo_ref[...] = vn

    spec = pl.BlockSpec((tr, cols), lambda i: (i, 0))
    shape = jax.ShapeDtypeStruct((rows, cols), F32)
    return pl.pallas_call(
        body, name=name, grid=(rows // tr,),
        in_specs=[spec] * 4, out_specs=[spec] * 3, out_shape=[shape] * 3,
        compiler_params=_params("arbitrary"),
    )(g, w, m, v)


def _mesh_place():
    x, y, c = lax.axis_index("x"), lax.axis_index("y"), lax.axis_index("c")
    chips = [(1 - x, y), (x, 1 - y), (1 - x, 1 - y)]
    return x, y, c, chips


def _remote(src, dst, send_sems, recv_sems, idx, device):
    return pltpu.make_async_remote_copy(src_ref=src, dst_ref=dst, send_sem=send_sems.at[idx], recv_sem=recv_sems.at[idx],
                                        device_id=device, device_id_type=MESH)


HBM_SPEC = pl.BlockSpec(memory_space=pltpu.HBM)
SEM_SPEC = pl.BlockSpec(memory_space=pltpu.SEMAPHORE)
ORDERED_EFFECT = pltpu.SideEffectType.DATAFLOW_SIDE_EFFECTING


def _in_hbm(a):
    return pltpu.with_memory_space_constraint(a, pltpu.HBM)


def _start_copies_call(name, bufs, groups, after=None):
    n, g = len(bufs), len(groups)
    extra = [] if after is None else [after]
    first_out = n + len(extra)

    def body(*refs):
        outs = refs[first_out:first_out + n]
        sems = refs[first_out + n:first_out + n + 2 * g]
        token = refs[first_out + n + 2 * g]
        for i, (which, copies_fn, _) in enumerate(groups):
            for mine, _ in copies_fn([outs[w] for w in which], sems[2 * i], sems[2 * i + 1]):
                mine.start()
        token[...] = jnp.zeros_like(token)

    sem_shapes = [pltpu.SemaphoreType.DMA((cnt,)) for _, _, cnt in groups for _ in range(2)]
    res = pl.pallas_call(
        body, name=name,
        in_specs=[HBM_SPEC] * n + [ANY] * len(extra),
        out_specs=[HBM_SPEC] * n + [SEM_SPEC] * (2 * g) + [pl.BlockSpec(memory_space=pltpu.VMEM)],
        out_shape=[pltpu.HBM(a.shape, a.dtype) for a in bufs] + sem_shapes + [jax.ShapeDtypeStruct((SUBLANES, LANES), F32)],
        input_output_aliases={a: a for a in range(n)},
        compiler_params=pltpu.CompilerParams(has_side_effects=ORDERED_EFFECT),
    )(*[_in_hbm(a) for a in bufs], *extra)
    sems = res[n:n + 2 * g]
    return list(res[:n]), [(sems[2 * i], sems[2 * i + 1]) for i in range(g)], res[n + 2 * g]


def _wait_copies_call(name, bufs, sems, copies_fn, after):
    n = len(bufs)

    def body(*refs):
        ins = refs[:n]
        send_sems, recv_sems = refs[n], refs[n + 1]
        for mine, arriving in copies_fn(list(ins), send_sems, recv_sems):
            arriving.wait_recv()
            mine.wait_send()

    res = pl.pallas_call(
        body, name=name,
        in_specs=[HBM_SPEC] * n + [SEM_SPEC, SEM_SPEC, ANY],
        out_specs=[HBM_SPEC] * n,
        out_shape=[pltpu.HBM(a.shape, a.dtype) for a in bufs],
        input_output_aliases={a: a for a in range(n)},
        compiler_params=pltpu.CompilerParams(has_side_effects=ORDERED_EFFECT),
    )(*bufs, sems[0], sems[1], after)
    return list(res)


def _gather_copies(bufs, send_sems, recv_sems):
    x, y, c, chips = _mesh_place()
    k = 2 * x + y
    out = []
    for a, buf in enumerate(bufs):
        for j, (px, py) in enumerate(chips):
            kj = 2 * px + py
            mine = _remote(buf.at[k, c], buf.at[k, c], send_sems, recv_sems, 3 * a + j, (px, py, c))
            arriving = _remote(buf.at[k, c], buf.at[kj, c], send_sems, recv_sems, 3 * a + j, (px, py, c))
            out.append((mine, arriving))
    return out


def _exchange_copies(n_sharded, n_replicated):
    def copies(bufs, send_sems, recv_sems):
        x, y, c, chips = _mesh_place()
        k = 2 * x + y
        sums, lands = bufs[:n_sharded], bufs[n_sharded:2 * n_sharded]
        repl = bufs[2 * n_sharded:]
        out = []
        for j, (px, py) in enumerate(chips):
            kj = 2 * px + py
            for a in range(n_sharded):
                cp = _remote(sums[a].at[kj], lands[a].at[j], send_sems, recv_sems, 3 * a + j, (px, py, c))
                out.append((cp, cp))
            for a in range(n_replicated):
                idx = 3 * (n_sharded + a) + j
                mine = _remote(repl[a].at[k], repl[a].at[k], send_sems, recv_sems, idx, (px, py, c))
                arriving = _remote(repl[a].at[k], repl[a].at[kj], send_sems, recv_sems, idx, (px, py, c))
                out.append((mine, arriving))
        return out
    return copies


def _sibling_copies(n):
    def copies(bufs, send_sems, recv_sems):
        x, y, c, _ = _mesh_place()
        out = []
        for a in range(n):
            cp = _remote(bufs[a].at[1 - c], bufs[n + a], send_sems, recv_sems, a, (x, y, 1 - c))
            out.append((cp, cp))
        return out
    return copies


def _forward_copies(bufs, send_sems, recv_sems):
    x, y, c, chips = _mesh_place()
    out = []
    for a, buf in enumerate(bufs):
        for j, (px, py) in enumerate(chips):
            kj = 2 * px + py
            mine = _remote(buf.at[kj, c], buf.at[kj, c], send_sems, recv_sems, 3 * a + j, (x, y, 1 - c))
            arriving = _remote(buf.at[kj, c], buf.at[kj, 1 - c], send_sems, recv_sems, 3 * a + j, (x, y, 1 - c))
            out.append((mine, arriving))
    return out


def _forward_to_sibling_call(bufs, name):
    n = len(bufs)

    def body(*refs):
        ins, outs = refs[:n], refs[n:2 * n]
        send_sems, recv_sems = refs[2 * n:]
        x, y, c, chips = _mesh_place()
        sibling = (x, y, 1 - c)
        sends = []
        for a in range(n):
            for j, (px, py) in enumerate(chips):
                kj = 2 * px + py
                sends.append(_remote(ins[a].at[kj, c], outs[a].at[kj, c], send_sems, recv_sems, 3 * a + j, sibling))
        for cp in sends:
            cp.start()
        for a in range(n):
            for j, (px, py) in enumerate(chips):
                kj = 2 * px + py
                _remote(ins[a].at[kj, c], outs[a].at[kj, 1 - c], send_sems, recv_sems, 3 * a + j, sibling).wait_recv()
        for cp in sends:
            cp.wait_send()

    return pl.pallas_call(
        body, name=name,
        in_specs=[ANY] * n, out_specs=[ANY] * n,
        out_shape=[jax.ShapeDtypeStruct(a.shape, a.dtype) for a in bufs],
        input_output_aliases={a: a for a in range(n)},
        scratch_shapes=[pltpu.SemaphoreType.DMA((3 * n,)), pltpu.SemaphoreType.DMA((3 * n,))],
    )(*bufs)


def _join_halves_call(bufs, name):
    n = len(bufs)

    def body(*refs):
        ins, outs = refs[:n], refs[n:2 * n]
        send_sems, recv_sems = refs[2 * n:]
        x, y, c, _ = _mesh_place()
        sibling = (x, y, 1 - c)
        copies = [_remote(ins[a].at[c], outs[a].at[c], send_sems, recv_sems, a, sibling) for a in range(n)]
        for cp in copies:
            cp.start()
        for a in range(n):
            _remote(ins[a].at[c], outs[a].at[1 - c], send_sems, recv_sems, a, sibling).wait_recv()
        for cp in copies:
            cp.wait_send()

    return pl.pallas_call(
        body, name=name,
        in_specs=[ANY] * n, out_specs=[ANY] * n,
        out_shape=[jax.ShapeDtypeStruct(a.shape, a.dtype) for a in bufs],
        input_output_aliases={a: a for a in range(n)},
        scratch_shapes=[pltpu.SemaphoreType.DMA((n,)), pltpu.SemaphoreType.DMA((n,))],
    )(*bufs)


def _pack(arrays, rows_multiple):
    flat = jnp.concatenate([a.reshape(-1) for a in arrays])
    per = LANES * rows_multiple
    padded = -(-flat.shape[0] // per) * per
    flat = jnp.pad(flat, (0, padded - flat.shape[0]))
    return flat.reshape(-1, LANES)


def _unpack(packed, shapes):
    flat = packed.reshape(-1)
    out, at = [], 0
    for shp in shapes:
        size = 1
        for dim in shp:
            size *= dim
        out.append(flat[at:at + size].reshape(shp))
        at += size
    return out


def _halves(a):
    return a.reshape((2, a.shape[0] // 2) + a.shape[1:])


def kernel(x, ln_mix_g, ln_mix_b, w_in, w_pool, pool_scale, conv_w, conv_b, w_rg_a, b_rg_a, w_rg_i, b_rg_i, rg_lambda, w_out, ln_ffn_g, ln_ffn_b, w_mlp_in, w_mlp_out, loss_target, m_ln_mix_g, m_ln_mix_b, m_w_in, m_w_pool, m_pool_scale, m_conv_w, m_conv_b, m_w_rg_a, m_b_rg_a, m_w_rg_i, m_b_rg_i, m_rg_lambda, m_w_out, m_ln_ffn_g, m_ln_ffn_b, m_w_mlp_in, m_w_mlp_out, v_ln_mix_g, v_ln_mix_b, v_w_in, v_w_pool, v_pool_scale, v_conv_w, v_conv_b, v_w_rg_a, v_b_rg_a, v_w_rg_i, v_b_rg_i, v_rg_lambda, v_w_out, v_ln_ffn_g, v_ln_ffn_b, v_w_mlp_in, v_w_mlp_out):
    weights = dict(ln_mix_g=ln_mix_g, ln_mix_b=ln_mix_b, w_in=w_in, w_pool=w_pool, pool_scale=pool_scale, conv_w=conv_w,
                   conv_b=conv_b, w_rg_a=w_rg_a, b_rg_a=b_rg_a, w_rg_i=w_rg_i, b_rg_i=b_rg_i, rg_lambda=rg_lambda,
                   w_out=w_out, ln_ffn_g=ln_ffn_g, ln_ffn_b=ln_ffn_b, w_mlp_in=w_mlp_in, w_mlp_out=w_mlp_out)
    m_in = dict(ln_mix_g=m_ln_mix_g, ln_mix_b=m_ln_mix_b, w_in=m_w_in, w_pool=m_w_pool, pool_scale=m_pool_scale,
                conv_w=m_conv_w, conv_b=m_conv_b, w_rg_a=m_w_rg_a, b_rg_a=m_b_rg_a, w_rg_i=m_w_rg_i, b_rg_i=m_b_rg_i,
                rg_lambda=m_rg_lambda, w_out=m_w_out, ln_ffn_g=m_ln_ffn_g, ln_ffn_b=m_ln_ffn_b, w_mlp_in=m_w_mlp_in,
                w_mlp_out=m_w_mlp_out)
    v_in = dict(ln_mix_g=v_ln_mix_g, ln_mix_b=v_ln_mix_b, w_in=v_w_in, w_pool=v_w_pool, pool_scale=v_pool_scale,
                conv_w=v_conv_w, conv_b=v_conv_b, w_rg_a=v_w_rg_a, b_rg_a=v_b_rg_a, w_rg_i=v_w_rg_i, b_rg_i=v_b_rg_i,
                rg_lambda=v_rg_lambda, w_out=v_w_out, ln_ffn_g=v_ln_ffn_g, ln_ffn_b=v_ln_ffn_b, w_mlp_in=v_w_mlp_in,
                w_mlp_out=v_w_mlp_out)
    names = list(weights)

    xs = x[0]
    tgt = loss_target[0]
    s, d = xs.shape
    p = c = d // 2
    pg = p // N_POOL_GROUPS
    heads = c // LRU_HEAD
    core = lax.axis_index("c")
    shard = 2 * lax.axis_index("x") + lax.axis_index("y")

    idx = jnp.stack([core, shard]).astype(jnp.int32)
    small_shard = _pack([conv_w[0], b_rg_a[0], b_rg_i[0], rg_lambda[0]], 2 * SUBLANES)
    to_gather = [(w_in[0], BF16), (w_out[0], BF16), (w_mlp_in[0], BF16), (w_mlp_out[0], BF16),
                 (w_pool[0].reshape(-1, pg), BF16), (small_shard, F32)]

    def slot_view(i):
        a, dt = to_gather[i]
        sl = _cast_to_slot_call(a, idx, dt, f"gather_slot_{i}")
        return sl.reshape(N_CHIPS, 2, sl.shape[1] // 2, sl.shape[2])

    first, later = (0, 4, 5), (1, 2, 3)
    fly_a, sems_a, token_a = _start_copies_call(
        "gather_start_first", [slot_view(i) for i in first], [((0, 1, 2), _gather_copies, 3 * len(first))])
    fly_b, sems_b, g_token = _start_copies_call(
        "gather_start_later", [slot_view(i) for i in later],
        [((0,), _gather_copies, 3), ((1,), _gather_copies, 3), ((2,), _gather_copies, 3)], token_a)
    in_flight = {**dict(zip(first, fly_a)), **dict(zip(later, fly_b))}
    g_sems = [sems_a[0]] + list(sems_b)

    def arrive(which, group, after, tag):
        return _wait_copies_call(f"gather_wait_{tag}", [in_flight[w] for w in which], g_sems[group], _gather_copies, after)

    def pass_on(got, tag):
        flying, sems, token = _start_copies_call(
            f"gather_forward_start_{tag}", got, [(tuple(range(len(got))), _forward_copies, 3 * len(got))])
        return (flying, sems[0], tag), token

    def passed_on(state, after):
        flying, sems, tag = state
        return _wait_copies_call(f"gather_forward_wait_{tag}", flying, sems, _forward_copies, after)

    gathered = [None] * len(to_gather)
    xb = _cast_call(xs, g_token)
    gathered[0], gathered[4], gathered[5] = _forward_to_sibling_call(arrive(first, 0, xb, "w_in"), "gather_forward_w_in")
    w_in_f = gathered[0].reshape((N_CHIPS,) + w_in.shape[1:])
    w_pool_f = gathered[4].reshape(N_CHIPS, N_POOL_GROUPS, pg // N_CHIPS, pg).transpose(1, 0, 2, 3).reshape(N_POOL_GROUPS, pg, pg)
    c4 = c // N_CHIPS
    small_parts = [_unpack(gathered[5][k].reshape(-1, LANES), [(4, c4), (2, c4), (2, c4), (2, c4)]) for k in range(N_CHIPS)]
    conv_w_f = jnp.concatenate([sp_[0] for sp_ in small_parts], axis=1)
    b_a_f = jnp.concatenate([sp_[1] for sp_ in small_parts], axis=1)
    b_i_f = jnp.concatenate([sp_[2] for sp_ in small_parts], axis=1)
    lam_f = jnp.concatenate([sp_[3] for sp_ in small_parts], axis=1)
    wa_b = w_rg_a[0].astype(BF16)
    wi_b = w_rg_i[0].astype(BF16)

    proj = _proj_call(xb, w_in_f)
    xc = _conv_call(proj, conv_w_f, conv_b, c)
    fwd_w_out, token = pass_on(arrive((1,), 1, xc, "w_out"), "w_out")
    h_b, *gates_b = _scan_fwd_call(xc, wa_b[1], wi_b[1], b_a_f[1:2], b_i_f[1:2], lam_f[1:2], True, token)
    h_f, *gates_f = _scan_fwd_call(xc, wa_b[0], wi_b[0], b_a_f[0:1], b_i_f[0:1], lam_f[0:1], False, token)
    y, d_pool = _pool_combine_call(proj, h_f, h_b, w_pool_f, pool_scale, p)
    w_out_f = passed_on(fwd_w_out, y)[0].reshape(d, d)
    fwd_w1, token = pass_on(arrive((2,), 2, y, "w_mlp_in"), "w_mlp_in")
    xh1, x1b, rstd1 = _out_ln1_call(y, w_out_f, xs, ln_mix_g, ln_mix_b, token)
    w1_f = passed_on(fwd_w1, x1b)[0].reshape((N_CHIPS,) + w_mlp_in.shape[1:])
    fwd_w2, token = pass_on(arrive((3,), 3, x1b, "w_mlp_out"), "w_mlp_out")
    r_act, hsq = _mlp_in_call(x1b, w1_f, token)
    w2_f = passed_on(fwd_w2, hsq)[0].reshape(N_CHIPS * w_mlp_out.shape[1], d)
    dz2, dz2b, loss8, dg2, db2 = _mlp_out_ln2_call(hsq, w2_f, xh1, ln_mix_g, ln_mix_b, ln_ffn_g, ln_ffn_b, tgt)

    def start_siblings(grads, tag, after=None):
        lands = [lax.empty(g.shape[1:], g.dtype) for g in grads]
        copies = _sibling_copies(len(grads))
        flying, sems, token = _start_copies_call(
            f"siblings_start_{tag}", list(grads) + lands, [(tuple(range(2 * len(grads))), copies, len(grads))], after)
        return (flying, sems[0], copies, len(grads), tag), token

    def finish_siblings(state, after, small_at=None):
        flying, sems, copies, n, tag = state
        got = _wait_copies_call(f"siblings_wait_{tag}", flying, sems, copies, after)
        out = []
        for a in range(n):
            g, rv = got[a], got[n + a]
            cols = g.shape[-1]
            part = _add_half_call(g.reshape(2, -1, cols), rv.reshape(-1, cols), idx, a == small_at, f"reduce_add_{tag}_{a}")
            out.append(part if a == small_at else part.reshape((N_CHIPS,) + rv.shape[1:]))
        return out

    def start_exchange(sums, n_repl, tag):
        n_sh = len(sums) - n_repl
        lands = [lax.empty((N_CHIPS - 1,) + a.shape[1:], a.dtype) for a in sums[:n_sh]]
        bufs = sums[:n_sh] + lands + sums[n_sh:]
        copies = _exchange_copies(n_sh, n_repl)
        flying, sems, token = _start_copies_call(
            f"reduce_start_{tag}", bufs, [(tuple(range(len(bufs))), copies, 3 * len(sums))])
        return (flying, sems[0], copies, n_sh, tag), token

    def finish_exchange(state, after):
        flying, sems, copies, n_sh, tag = state
        got = _wait_copies_call(f"reduce_wait_{tag}", flying, sems, copies, after)
        halves = []
        for a in range(n_sh):
            own, land = got[a], got[n_sh + a]
            cols = own.shape[-1]
            total = _sum_chips_call(own.reshape(N_CHIPS, -1, cols), land.reshape(N_CHIPS - 1, -1, cols), idx,
                                    f"reduce_sum_{tag}_{a}")
            halves.append(total.reshape((2,) + own.shape[1:]))
        for a, rp in enumerate(got[2 * n_sh:]):
            halves.append(_sum_chips_call(None, rp, idx, f"reduce_sum_{tag}_r{a}"))
        return _join_halves_call(halves, f"reduce_join_{tag}")

    g_w2 = _row_sharded_grad(hsq, dz2b, "grad_w_mlp_out", g_token)
    sib_w2, token = start_siblings([g_w2], "w2")
    dpre = _dhsq_call(dz2b, w2_f, r_act, token)
    flying_w2, token = start_exchange(finish_siblings(sib_w2, dpre), 0, "w2")
    g_w1 = _col_sharded_grad(x1b, dpre, "grad_w_mlp_in", token)
    sib_w1, token = start_siblings([g_w1], "w1")
    dz1, dz1b, dg1, db1 = _dx1_ln1_bwd_call(dpre, w1_f, dz2, xh1, rstd1, ln_mix_g, token)
    flying_w1, token = start_exchange(finish_siblings(sib_w1, dz1b), 0, "w1")
    g_wout = _row_sharded_grad(y, dz1b, "grad_w_out", token)
    sib_wout, token = start_siblings([g_wout], "w_out")
    dy = _dy_call(dz1b, w_out_f, token)
    e_pool, dh, dgate, g_wpool, g_pscale8 = _mixer_bwd_call(dy, d_pool, proj, h_f, h_b, w_pool_f, pool_scale, p)
    flying_wout, token = start_exchange(finish_siblings(sib_wout, dgate), 0, "w_out")
    dxc0, g_wa0, g_wi0, g_ba0, g_bi0, g_sp0 = _scan_bwd_call(
        xc, dh, h_f, gates_f, None, wa_b[0], wi_b[0], lam_f[0:1], False, token)
    dxc, g_wa1, g_wi1, g_ba1, g_bi1, g_sp1 = _scan_bwd_call(
        xc, dh, h_b, gates_b, dxc0, wa_b[1], wi_b[1], lam_f[1:2], True, token)
    dproj, g_cw8, g_cb8 = _dproj_call(e_pool, dxc, proj, dgate, conv_w_f, p)

    rowsum = lambda a8: jnp.sum(a8, axis=-2)
    g_lam = jnp.stack([rowsum(g_sp0), rowsum(g_sp1)]) * (-_sigmoid(-lam_f))
    small_grads = {
        "ln_mix_g": rowsum(dg1), "ln_mix_b": rowsum(db1), "ln_ffn_g": rowsum(dg2), "ln_ffn_b": rowsum(db2),
        "pool_scale": rowsum(g_pscale8), "conv_b": rowsum(g_cb8),
        "w_rg_a": jnp.stack([g_wa0, g_wa1]), "w_rg_i": jnp.stack([g_wi0, g_wi1]),
        "w_pool": g_wpool, "conv_w": rowsum(g_cw8),
        "b_rg_a": jnp.stack([rowsum(g_ba0), rowsum(g_ba1)]), "b_rg_i": jnp.stack([rowsum(g_bi0), rowsum(g_bi1)]),
        "rg_lambda": g_lam,
    }
    small_names = list(small_grads)
    small_shapes = [small_grads[nm].shape for nm in small_names]
    g_small = _halves(_pack([small_grads[nm] for nm in small_names], 2 * SUBLANES))
    sib_small, token = start_siblings([g_small], "small")
    g_win = _col_sharded_grad(xb, dproj, "grad_w_in", token)
    flying_small, token = start_exchange(finish_siblings(sib_small, g_win, small_at=0), 1, "small")
    sib_win, token = start_siblings([g_win], "w_in", token)
    grad_x = _dx_call(dproj, w_in_f, dz1, token)
    flying_win, token = start_exchange(finish_siblings(sib_win, grad_x), 0, "w_in")

    grad_w, delta_w, new_m, new_v = {}, {}, {}, {}

    def adamw(nm, full):
        w2d = weights[nm][0]
        g2d = full.reshape(w2d.shape)
        dl, mn, vn = _adamw_call(g2d, w2d, m_in[nm][0], v_in[nm][0], f"adamw_{nm}")
        grad_w[nm], delta_w[nm], new_m[nm], new_v[nm] = g2d[None], dl[None], mn[None], vn[None]
        return vn

    last = adamw("w_mlp_out", finish_exchange(flying_w2, token)[0])
    last = adamw("w_mlp_in", finish_exchange(flying_w1, last)[0])
    last = adamw("w_out", finish_exchange(flying_wout, last)[0])

    small_full = dict(zip(small_names, _unpack(finish_exchange(flying_small, last)[0].reshape(-1, LANES), small_shapes)))
    local = dict(small_full)
    local["w_pool"] = lax.dynamic_slice_in_dim(small_full["w_pool"], shard * (pg // N_CHIPS), pg // N_CHIPS, axis=1)
    for nm in ("conv_w", "b_rg_a", "b_rg_i", "rg_lambda"):
        local[nm] = lax.dynamic_slice_in_dim(small_full[nm], shard * c4, c4, axis=1)
    small_w_shapes = [weights[nm].shape for nm in small_names]
    g_pack = _pack([local[nm] for nm in small_names], SUBLANES)
    w_pack = _pack([weights[nm] for nm in small_names], SUBLANES)
    m_pack = _pack([m_in[nm] for nm in small_names], SUBLANES)
    v_pack = _pack([v_in[nm] for nm in small_names], SUBLANES)
    dl_p, mn_p, vn_p = _adamw_call(g_pack, w_pack, m_pack, v_pack, "adamw_small")
    for nm, gl, dl, mn, vn in zip(small_names, _unpack(g_pack, small_w_shapes), _unpack(dl_p, small_w_shapes),
                                  _unpack(mn_p, small_w_shapes), _unpack(vn_p, small_w_shapes)):
        grad_w[nm], delta_w[nm], new_m[nm], new_v[nm] = gl, dl, mn, vn
    adamw("w_in", finish_exchange(flying_win, vn_p)[0])

    loss = lax.psum(jnp.sum(loss8) * (0.5 / d), ("x", "y", "c"))
    return (loss, grad_x[None], *[grad_w[nm] for nm in names], *[delta_w[nm] for nm in names],
            *[new_m[nm] for nm in names], *[new_v[nm] for nm in names])
```

```python
import functools

import jax
import jax.numpy as jnp
from jax import lax
from jax.experimental import pallas as pl
from jax.experimental.pallas import tpu as pltpu

F32 = jnp.float32
BF16 = jnp.bfloat16

N_CHIPS = 4
LANES = 128
SUBLANES = 8
LRU_HEAD = 128
N_POOL_GROUPS = 4
POOL_WINDOWS = (2, 4, 8, 16)
RG_C = 8.0
LN_EPS = 1e-5
ALPHA = 2.0 ** 0.25
ADAM_LR, ADAM_B1, ADAM_B2, ADAM_EPS, ADAM_WD, ADAM_STEP = 0.001, 0.9, 0.999, 1e-08, 0.01, 10
VMEM_LIMIT = 56 * 1024 * 1024
SEQ_TILE = 256
MM_TILE = 512
LN_MM_K = 2048
LN_UNROLL = 4
ELT_BLOCK_BYTES = 2 * 1024 * 1024
RESIDENT_OPERAND_BYTES = 16 * 1024 * 1024
MESH = pl.DeviceIdType.MESH
ANY = pl.BlockSpec(memory_space=pl.ANY)


def _params(*sem):
    return pltpu.CompilerParams(dimension_semantics=sem, vmem_limit_bytes=VMEM_LIMIT)


def _sigmoid(z):
    return 1.0 / (1.0 + jnp.exp(-z))


def _neg_expm1(z):
    series = -(z * (1.0 + z * (0.5 + z * (1.0 / 6.0 + z * (1.0 / 24.0)))))
    return jnp.where(z > -0.01, series, 1.0 - jnp.exp(z))


def _softplus(z):
    return jnp.maximum(z, 0.0) + jnp.log1p(jnp.exp(-jnp.abs(z)))


_GELU_K = 0.7978845608028654
_GELU_C = 0.044715


def _gelu_and_grad(u):
    t = jnp.tanh(_GELU_K * (u + _GELU_C * (u * u * u)))
    g = 0.5 * u * (1.0 + t)
    dg = 0.5 * (1.0 + t) + 0.5 * u * (1.0 - t * t) * (_GELU_K * (1.0 + 3.0 * _GELU_C * u * u))
    return g, dg


def _shift_rows(prv, cur, nxt, o, rows):
    if o == 0:
        return cur
    if o == SUBLANES:
        return nxt
    if o == -SUBLANES:
        return prv
    if o > 0:
        s = SUBLANES - o
        return jnp.where(rows < s, pltpu.roll(cur, s, 0), pltpu.roll(nxt, s, 0))
    p = -o
    return jnp.where(rows >= p, pltpu.roll(cur, p, 0), pltpu.roll(prv, p, 0))


def _neighbour_chunks(main_ref, prev_ref, next_ref, r0, t_rows, cols, first_tile, last_tile):
    cur = main_ref[pl.ds(r0, SUBLANES), cols]
    before = main_ref[pl.ds(pl.multiple_of(jnp.maximum(r0 - SUBLANES, 0), SUBLANES), SUBLANES), cols]
    after = main_ref[pl.ds(pl.multiple_of(jnp.minimum(r0 + SUBLANES, t_rows - SUBLANES), SUBLANES), SUBLANES), cols]
    halo_prev = jnp.where(first_tile, 0.0, prev_ref[:, cols])
    halo_next = jnp.where(last_tile, 0.0, next_ref[:, cols])
    prv = jnp.where(r0 == 0, halo_prev, before)
    nxt = jnp.where(r0 == t_rows - SUBLANES, halo_next, after)
    return prv, cur, nxt


def _halo_specs(t_rows, n_rows, width, col_block):
    per = t_rows // SUBLANES
    last = n_rows // SUBLANES - 1
    return [
        pl.BlockSpec((t_rows, width), lambda i: (i, col_block)),
        pl.BlockSpec((SUBLANES, width), lambda i: (jnp.maximum(i * per - 1, 0), col_block)),
        pl.BlockSpec((SUBLANES, width), lambda i: (jnp.minimum((i + 1) * per, last), col_block)),
    ]


def _chunk_loop(t_rows, fn, init=None, unroll=1, descending=False):
    span = SUBLANES * unroll

    def step(ci, carry):
        base = pl.multiple_of(((t_rows // span - 1 - ci) if descending else ci) * span, span)
        for u in range(unroll):
            carry = fn(base + ((unroll - 1 - u) if descending else u) * SUBLANES, carry)
        return carry
    return lax.fori_loop(0, t_rows // span, step, init)


def _scan_chunk(a, b, h_in, rows, reverse):
    for dist in (1, 2, 4):
        if reverse:
            keep = rows < SUBLANES - dist
            shift = SUBLANES - dist
        else:
            keep = rows >= dist
            shift = dist
        b = a * jnp.where(keep, pltpu.roll(b, shift, 0), 0.0) + b
        a = a * jnp.where(keep, pltpu.roll(a, shift, 0), 1.0)
    return a * h_in + b


def _cast_call(x, dep):
    s, d = x.shape
    tm = min(MM_TILE, s)

    def body(x_ref, dep_ref, o_ref):
        o_ref[...] = x_ref[...].astype(BF16)

    return pl.pallas_call(
        body, name="cast_x", grid=(s // tm,),
        in_specs=[pl.BlockSpec((tm, d), lambda i: (i, 0)), ANY],
        out_specs=pl.BlockSpec((tm, d), lambda i: (i, 0)),
        out_shape=jax.ShapeDtypeStruct((s, d), BF16),
        compiler_params=_params("arbitrary"),
    )(x, dep)


def _proj_call(xb, w_in):
    s, d = xb.shape
    n, _, e4 = w_in.shape
    tm = min(MM_TILE, s)

    def body(x_ref, w_ref, proj_ref):
        proj_ref[...] = jnp.dot(x_ref[...], w_ref[...], preferred_element_type=F32)

    return pl.pallas_call(
        body, name="proj", grid=(s // tm, n),
        in_specs=[pl.BlockSpec((tm, d), lambda i, j: (i, 0)), pl.BlockSpec((None, d, e4), lambda i, j: (j, 0, 0))],
        out_specs=pl.BlockSpec((tm, e4), lambda i, j: (i, j)),
        out_shape=jax.ShapeDtypeStruct((s, n * e4), F32),
        compiler_params=_params("arbitrary", "arbitrary"),
    )(xb, w_in)


def _conv_call(proj, conv_w, conv_b, c):
    s = proj.shape[0]
    t = min(SEQ_TILE, s)
    n_tiles = s // t

    def body(u_ref, up_ref, un_ref, w_ref, b_ref, xc_ref):
        i = pl.program_id(0)
        rows = lax.broadcasted_iota(jnp.int32, (SUBLANES, c), 0)
        w = w_ref[...]
        b = b_ref[...]

        def chunk(r0, _):
            prv, cur, nxt = _neighbour_chunks(u_ref, up_ref, un_ref, r0, t, slice(None), i == 0, i == n_tiles - 1)
            acc = b + w[1:2] * cur
            acc += w[0:1] * _shift_rows(prv, cur, nxt, -1, rows)
            acc += w[2:3] * _shift_rows(prv, cur, nxt, 1, rows)
            acc += w[3:4] * _shift_rows(prv, cur, nxt, 2, rows)
            xc_ref[pl.ds(r0, SUBLANES), :] = acc

        _chunk_loop(t, chunk)

    return pl.pallas_call(
        body, name="conv_fwd", grid=(n_tiles,),
        in_specs=_halo_specs(t, s, c, 1) + [pl.BlockSpec((4, c), lambda i: (0, 0)), pl.BlockSpec((1, c), lambda i: (0, 0))],
        out_specs=pl.BlockSpec((t, c), lambda i: (i, 0)),
        out_shape=jax.ShapeDtypeStruct((s, c), F32),
        compiler_params=_params("arbitrary"),
    )(proj, proj, proj, conv_w, conv_b)


def _gate_matmuls(xc_ref, wa_ref, wi_ref, pr_s, pi_s, heads):
    for h in range(heads):
        cs = pl.ds(h * LRU_HEAD, LRU_HEAD)
        xb = xc_ref[:, cs].astype(BF16)
        pr_s[:, cs] = jnp.dot(xb, wa_ref[h], preferred_element_type=F32)
        pi_s[:, cs] = jnp.dot(xb, wi_ref[h], preferred_element_type=F32)


def _rg_gates(pr, pi, ba, bi, sp):
    r = _sigmoid(pr + ba)
    ig = _sigmoid(pi + bi)
    log_a = (-RG_C * r) * sp
    a = jnp.exp(log_a)
    mult = jnp.sqrt(_neg_expm1(2.0 * log_a))
    return r, ig, a, mult


def _scan_fwd_call(xc, wa, wi, ba, bi, lam, reverse, dep):
    s, c = xc.shape
    heads = c // LRU_HEAD
    t = min(SEQ_TILE, s)
    n_tiles = s // t
    tile = (lambda i: (n_tiles - 1 - i, 0)) if reverse else (lambda i: (i, 0))
    whole2 = lambda i: (0, 0)
    whole3 = lambda i: (0, 0, 0)

    def body(xc_ref, wa_ref, wi_ref, ba_ref, bi_ref, lam_ref, dep_ref, h_ref, r_ref, ig_ref, a_ref, mult_ref,
             pr_s, pi_s, carry_s):
        @pl.when(pl.program_id(0) == 0)
        def _():
            carry_s[...] = jnp.zeros_like(carry_s)

        _gate_matmuls(xc_ref, wa_ref, wi_ref, pr_s, pi_s, heads)
        ba_v, bi_v = ba_ref[...], bi_ref[...]
        sp = _softplus(-lam_ref[...])

        rows = lax.broadcasted_iota(jnp.int32, (SUBLANES, c), 0)

        def chunk(r0, h_in):
            rs = pl.ds(r0, SUBLANES)
            r, ig, a, mult = _rg_gates(pr_s[rs, :], pi_s[rs, :], ba_v, bi_v, sp)
            r_ref[rs, :] = r
            ig_ref[rs, :] = ig
            a_ref[rs, :] = a
            mult_ref[rs, :] = mult
            h = _scan_chunk(a, mult * ig * xc_ref[rs, :], h_in, rows, reverse)
            h_ref[rs, :] = h
            return h[0:1, :] if reverse else h[SUBLANES - 1:SUBLANES, :]

        carry_s[...] = _chunk_loop(t, chunk, carry_s[...], unroll=2, descending=reverse)

    return pl.pallas_call(
        body, name="scan_fwd_rev" if reverse else "scan_fwd", grid=(n_tiles,),
        in_specs=[pl.BlockSpec((t, c), tile),
                  pl.BlockSpec((heads, LRU_HEAD, LRU_HEAD), whole3), pl.BlockSpec((heads, LRU_HEAD, LRU_HEAD), whole3),
                  pl.BlockSpec((1, c), whole2), pl.BlockSpec((1, c), whole2), pl.BlockSpec((1, c), whole2), ANY],
        out_specs=[pl.BlockSpec((t, c), tile)] * 5,
        out_shape=[jax.ShapeDtypeStruct((s, c), F32)] * 5,
        scratch_shapes=[pltpu.VMEM((t, c), F32), pltpu.VMEM((t, c), F32), pltpu.VMEM((1, c), F32)],
        compiler_params=_params("arbitrary"),
    )(xc, wa, wi, ba, bi, lam, dep)


def _window_counts(r0, tile_idx, t_rows, n_rows, half, shape):
    pos = tile_idx * t_rows + r0 + lax.broadcasted_iota(jnp.int32, shape, 0)
    hi = jnp.minimum(pos + half, n_rows)
    lo = jnp.maximum(pos - half, 0)
    return (hi - lo).astype(F32)


def _pool_combine_call(proj, h_f, h_b, w_pool, pool_scale, p):
    s = proj.shape[0]
    c = h_f.shape[1]
    pg = p // N_POOL_GROUPS
    t = min(SEQ_TILE, s)
    n_tiles = s // t

    def body(u_ref, up_ref, un_ref, gate_ref, hf_ref, hb_ref, wp_ref, sc_ref, y_ref, d_ref, d_s, yr_s):
        i = pl.program_id(0)
        rows = lax.broadcasted_iota(jnp.int32, (SUBLANES, pg), 0)

        def chunk(r0, _):
            rs = pl.ds(r0, SUBLANES)
            for g, w in enumerate(POOL_WINDOWS):
                cols = pl.ds(g * pg, pg)
                prv, cur, nxt = _neighbour_chunks(u_ref, up_ref, un_ref, r0, t, cols, i == 0, i == n_tiles - 1)
                tot = cur
                for o in range(-(w // 2), w // 2):
                    if o != 0:
                        tot = tot + _shift_rows(prv, cur, nxt, o, rows)
                cnt = _window_counts(r0, i, t, s, w // 2, (SUBLANES, pg))
                d_s[rs, cols] = tot / cnt - cur
            gate, _ = _gelu_and_grad(gate_ref[rs, :])
            yr_s[rs, :] = (hf_ref[rs, :] + hb_ref[rs, :]) * gate

        _chunk_loop(t, chunk)
        y_ref[:, pl.ds(p, c)] = yr_s[...].astype(BF16)
        d_ref[...] = d_s[...].astype(BF16)
        for g in range(N_POOL_GROUPS):
            cols = pl.ds(g * pg, pg)
            out = jnp.dot(d_s[:, cols].astype(BF16), wp_ref[g], preferred_element_type=F32)
            y_ref[:, cols] = (out * sc_ref[:, cols]).astype(BF16)

    return pl.pallas_call(
        body, name="pool_combine", grid=(n_tiles,),
        in_specs=_halo_specs(t, s, p, 0) + [
            pl.BlockSpec((t, c), lambda i: (i, 2)),
            pl.BlockSpec((t, c), lambda i: (i, 0)), pl.BlockSpec((t, c), lambda i: (i, 0)),
            pl.BlockSpec((N_POOL_GROUPS, pg, pg), lambda i: (0, 0, 0)), pl.BlockSpec((1, p), lambda i: (0, 0))],
        out_specs=[pl.BlockSpec((t, p + c), lambda i: (i, 0)), pl.BlockSpec((t, p), lambda i: (i, 0))],
        out_shape=[jax.ShapeDtypeStruct((s, p + c), BF16), jax.ShapeDtypeStruct((s, p), BF16)],
        scratch_shapes=[pltpu.VMEM((t, p), F32), pltpu.VMEM((t, c), F32)],
        compiler_params=_params("arbitrary"),
    )(proj, proj, proj, proj, h_f, h_b, w_pool, pool_scale)


def _layer_norm_rows(z, g, b):
    mu = jnp.mean(z, axis=-1, keepdims=True)
    zc = z - mu
    var = jnp.mean(zc * zc, axis=-1, keepdims=True)
    rstd = lax.rsqrt(var + LN_EPS)
    xh = zc * rstd
    return xh, rstd, xh * g + b


def _layer_norm_bwd_rows(dx, xh, rstd, g):
    dxh = dx * g
    m1 = jnp.mean(dxh, axis=-1, keepdims=True)
    m2 = jnp.mean(dxh * xh, axis=-1, keepdims=True)
    return rstd * (dxh - m1 - xh * m2)


def _out_ln1_call(y, w_out, x, g1, b1, dep):
    s, d = x.shape
    tm = min(SEQ_TILE, s)

    def body(y_ref, w_ref, x_ref, g_ref, b_ref, dep_ref, xh_ref, x1b_ref, rstd_ref, acc_s, x1_s):
        acc_s[...] = jnp.dot(y_ref[...], w_ref[...], preferred_element_type=F32)
        g, b = g_ref[...], b_ref[...]

        def chunk(r0, _):
            rs = pl.ds(r0, SUBLANES)
            xh, rstd, x1 = _layer_norm_rows(ALPHA * x_ref[rs, :] + acc_s[rs, :], g, b)
            xh_ref[rs, :] = xh
            x1_s[rs, :] = x1
            rstd_ref[rs, :] = rstd

        _chunk_loop(tm, chunk, unroll=LN_UNROLL)
        x1b_ref[...] = x1_s[...].astype(BF16)

    return pl.pallas_call(
        body, name="out_ln1", grid=(s // tm,),
        in_specs=[pl.BlockSpec((tm, d), lambda i: (i, 0)), pl.BlockSpec((d, d), lambda i: (0, 0)),
                  pl.BlockSpec((tm, d), lambda i: (i, 0)),
                  pl.BlockSpec((1, d), lambda i: (0, 0)), pl.BlockSpec((1, d), lambda i: (0, 0)), ANY],
        out_specs=[pl.BlockSpec((tm, d), lambda i: (i, 0)), pl.BlockSpec((tm, d), lambda i: (i, 0)),
                   pl.BlockSpec((tm, 1), lambda i: (i, 0))],
        out_shape=[jax.ShapeDtypeStruct((s, d), F32), jax.ShapeDtypeStruct((s, d), BF16), jax.ShapeDtypeStruct((s, 1), F32)],
        scratch_shapes=[pltpu.VMEM((tm, d), F32), pltpu.VMEM((tm, d), F32)],
        compiler_params=_params("arbitrary"),
    )(y, w_out, x, g1, b1, dep)


def _mlp_in_call(x1b, w1, dep):
    s, d = x1b.shape
    n, _, f4 = w1.shape
    tm = min(MM_TILE, s)
    tn = min(1024, f4)
    per = f4 // tn

    def body(x_ref, w_ref, dep_ref, r_ref, q_ref):
        r = jnp.maximum(jnp.dot(x_ref[...], w_ref[...], preferred_element_type=F32), 0.0)
        r_ref[...] = r.astype(BF16)
        q_ref[...] = (r * r).astype(BF16)

    return pl.pallas_call(
        body, name="mlp_in", grid=(n * per, s // tm),
        in_specs=[pl.BlockSpec((tm, d), lambda j, i: (i, 0)), pl.BlockSpec((None, d, tn), lambda j, i: (j // per, 0, j % per)), ANY],
        out_specs=[pl.BlockSpec((tm, tn), lambda j, i: (i, j)), pl.BlockSpec((tm, tn), lambda j, i: (i, j))],
        out_shape=[jax.ShapeDtypeStruct((s, n * f4), BF16), jax.ShapeDtypeStruct((s, n * f4), BF16)],
        compiler_params=_params("arbitrary", "arbitrary"),
    )(x1b, w1, dep)


def _mlp_out_ln2_call(hsq, w2, xh1, g1, b1, g2, b2, target):
    s, f = hsq.shape
    d = w2.shape[1]
    tm = min(MM_TILE, s)
    tk = min(LN_MM_K, f)
    nk = f // tk

    def body(h_ref, w_ref, xh1_ref, g1_ref, b1_ref, g2_ref, b2_ref, t_ref,
             dz_ref, dzb_ref, loss_ref, dg_ref, db_ref, acc_s):
        i, k = pl.program_id(0), pl.program_id(1)

        @pl.when((i == 0) & (k == 0))
        def _():
            loss_ref[...] = jnp.zeros_like(loss_ref)
            dg_ref[...] = jnp.zeros_like(dg_ref)
            db_ref[...] = jnp.zeros_like(db_ref)

        @pl.when(k == 0)
        def _():
            acc_s[...] = jnp.zeros_like(acc_s)

        acc_s[...] += jnp.dot(h_ref[...], w_ref[...], preferred_element_type=F32)

        @pl.when(k == nk - 1)
        def _():
            g1, b1, g2, b2 = g1_ref[...], b1_ref[...], g2_ref[...], b2_ref[...]

            def chunk(r0, _):
                rs = pl.ds(r0, SUBLANES)
                x1 = xh1_ref[rs, :] * g1 + b1
                xh2, rstd, x2 = _layer_norm_rows(ALPHA * x1 + acc_s[rs, :], g2, b2)
                diff = x2 - t_ref[rs, :]
                loss_ref[...] += diff * diff
                dx2 = diff * (1.0 / d)
                dg_ref[...] += dx2 * xh2
                db_ref[...] += dx2
                dz = _layer_norm_bwd_rows(dx2, xh2, rstd, g2)
                dz_ref[rs, :] = dz

            _chunk_loop(tm, chunk, unroll=LN_UNROLL)
            dzb_ref[...] = dz_ref[...].astype(BF16)

    row = lambda i, k: (i, 0)
    vec = lambda i, k: (0, 0)
    return pl.pallas_call(
        body, name="mlp_out_ln2", grid=(s // tm, nk),
        in_specs=[pl.BlockSpec((tm, tk), lambda i, k: (i, k)), pl.BlockSpec((tk, d), lambda i, k: (k, 0)),
                  pl.BlockSpec((tm, d), row), pl.BlockSpec((1, d), vec), pl.BlockSpec((1, d), vec),
                  pl.BlockSpec((1, d), vec), pl.BlockSpec((1, d), vec), pl.BlockSpec((tm, d), row)],
        out_specs=[pl.BlockSpec((tm, d), row), pl.BlockSpec((tm, d), row),
                   pl.BlockSpec((SUBLANES, d), vec), pl.BlockSpec((SUBLANES, d), vec), pl.BlockSpec((SUBLANES, d), vec)],
        out_shape=[jax.ShapeDtypeStruct((s, d), F32), jax.ShapeDtypeStruct((s, d), BF16),
                   jax.ShapeDtypeStruct((SUBLANES, d), F32), jax.ShapeDtypeStruct((SUBLANES, d), F32),
                   jax.ShapeDtypeStruct((SUBLANES, d), F32)],
        scratch_shapes=[pltpu.VMEM((tm, d), F32)],
        compiler_params=_params("arbitrary", "arbitrary"),
    )(hsq, w2, xh1, g1, b1, g2, b2, target)


def _half_grad_call(a, b, half, row_sharded, init, name, dep):
    s, m = a.shape
    n = b.shape[1]
    if row_sharded:
        rows, cols = m // (2 * N_CHIPS), n
        tm = min(1024, rows)
        per = rows // tm
        tn = min(1024, cols)
        n_i, n_j = N_CHIPS * per, cols // tn
        a_block = lambda i, h: ((i // per) * 2 + h) * per + i % per
        out_block = lambda i, j: (i // per, i % per, j)
    else:
        rows, cols = m // 2, n // N_CHIPS
        tm = min(1024, rows)
        per = rows // tm
        tn = cols if cols % 1024 else 1024
        per_n = cols // tn
        n_i, n_j = per, N_CHIPS * per_n
        a_block = lambda i, h: h * per + i
        out_block = lambda i, j: (j // per_n, i, j % per_n)
    tk = min(2048, s)
    if row_sharded and tm < 1024 and s * n * 2 <= RESIDENT_OPERAND_BYTES:
        tk, tn, n_j = s, n, 1
    has_init = init is not None

    def body(half_ref, a_ref, b_ref, *rest):
        o_ref = rest[-1]

        @pl.when(pl.program_id(2) == 0)
        def _():
            o_ref[...] = rest[0][...] if has_init else jnp.zeros_like(o_ref)

        o_ref[...] += lax.dot_general(a_ref[...], b_ref[...], (((0,), (0,)), ((), ())), preferred_element_type=F32)

    out_spec = pl.BlockSpec((None, tm, tn), lambda i, j, k, h: out_block(i, j))
    in_specs = [pl.BlockSpec((tk, tm), lambda i, j, k, h: (k, a_block(i, h[0]))),
                pl.BlockSpec((tk, tn), lambda i, j, k, h: (k, j))]
    args = [a, b]
    if has_init:
        in_specs.append(out_spec)
        args.append(init)
    in_specs.append(ANY)
    args.append(dep)
    return pl.pallas_call(
        body, name=name,
        grid_spec=pltpu.PrefetchScalarGridSpec(num_scalar_prefetch=1, grid=(n_i, n_j, s // tk), in_specs=in_specs,
                                               out_specs=out_spec),
        out_shape=jax.ShapeDtypeStruct((N_CHIPS, rows, cols), F32),
        compiler_params=_params("arbitrary", "arbitrary", "arbitrary"),
    )(half, *args)


def _dhsq_call(dzb, w2, r, dep):
    s, d = dzb.shape
    f = w2.shape[0]
    tm = min(MM_TILE, s)
    tn = min(1024, f)

    def body(dz_ref, w_ref, r_ref, dep_ref, o_ref):
        dh = lax.dot_general(dz_ref[...], w_ref[...], (((1,), (1,)), ((), ())), preferred_element_type=F32)
        o_ref[...] = (dh * (2.0 * r_ref[...].astype(F32))).astype(BF16)

    return pl.pallas_call(
        body, name="mlp_dpre", grid=(f // tn, s // tm),
        in_specs=[pl.BlockSpec((tm, d), lambda j, i: (i, 0)), pl.BlockSpec((tn, d), lambda j, i: (j, 0)),
                  pl.BlockSpec((tm, tn), lambda j, i: (i, j)), ANY],
        out_specs=pl.BlockSpec((tm, tn), lambda j, i: (i, j)),
        out_shape=jax.ShapeDtypeStruct((s, f), BF16),
        compiler_params=_params("arbitrary", "arbitrary"),
    )(dzb, w2, r, dep)


def _dx1_ln1_bwd_call(dpre, w1, dz2, xh1, rstd1, g1, dep):
    s, f = dpre.shape
    n, d, f4 = w1.shape
    tm = min(MM_TILE, s)
    tk = min(LN_MM_K, f4)
    per = f4 // tk
    nk = n * per

    def body(dp_ref, w_ref, dz2_ref, xh_ref, rstd_ref, g_ref, dep_ref, dz_ref, dzb_ref, dg_ref, db_ref, acc_s):
        i, k = pl.program_id(0), pl.program_id(1)

        @pl.when((i == 0) & (k == 0))
        def _():
            dg_ref[...] = jnp.zeros_like(dg_ref)
            db_ref[...] = jnp.zeros_like(db_ref)

        @pl.when(k == 0)
        def _():
            acc_s[...] = jnp.zeros_like(acc_s)

        acc_s[...] += lax.dot_general(dp_ref[...], w_ref[...], (((1,), (1,)), ((), ())), preferred_element_type=F32)

        @pl.when(k == nk - 1)
        def _():
            g = g_ref[...]

            def chunk(r0, _):
                rs = pl.ds(r0, SUBLANES)
                dx1 = acc_s[rs, :] + ALPHA * dz2_ref[rs, :]
                xh = xh_ref[rs, :]
                dg_ref[...] += dx1 * xh
                db_ref[...] += dx1
                dz = _layer_norm_bwd_rows(dx1, xh, rstd_ref[rs, :], g)
                dz_ref[rs, :] = dz

            _chunk_loop(tm, chunk, unroll=LN_UNROLL)
            dzb_ref[...] = dz_ref[...].astype(BF16)

    row = lambda i, k: (i, 0)
    vec = lambda i, k: (0, 0)
    return pl.pallas_call(
        body, name="dx1_ln1_bwd", grid=(s // tm, nk),
        in_specs=[pl.BlockSpec((tm, tk), lambda i, k: (i, k)),
                  pl.BlockSpec((None, d, tk), lambda i, k: (k // per, 0, k % per)),
                  pl.BlockSpec((tm, d), row), pl.BlockSpec((tm, d), row), pl.BlockSpec((tm, 1), row),
                  pl.BlockSpec((1, d), vec), ANY],
        out_specs=[pl.BlockSpec((tm, d), row), pl.BlockSpec((tm, d), row),
                   pl.BlockSpec((SUBLANES, d), vec), pl.BlockSpec((SUBLANES, d), vec)],
        out_shape=[jax.ShapeDtypeStruct((s, d), F32), jax.ShapeDtypeStruct((s, d), BF16),
                   jax.ShapeDtypeStruct((SUBLANES, d), F32), jax.ShapeDtypeStruct((SUBLANES, d), F32)],
        scratch_shapes=[pltpu.VMEM((tm, d), F32)],
        compiler_params=_params("arbitrary", "arbitrary"),
    )(dpre, w1, dz2, xh1, rstd1, g1, dep)


def _dy_call(dzb, w_out, dep):
    s, d = dzb.shape
    e = w_out.shape[0]
    tm = min(MM_TILE, s)

    def body(dz_ref, w_ref, dep_ref, o_ref):
        o_ref[...] = lax.dot_general(dz_ref[...], w_ref[...], (((1,), (1,)), ((), ())), preferred_element_type=F32)

    return pl.pallas_call(
        body, name="dy", grid=(s // tm,),
        in_specs=[pl.BlockSpec((tm, d), lambda i: (i, 0)), pl.BlockSpec((e, d), lambda i: (0, 0)), ANY],
        out_specs=pl.BlockSpec((tm, e), lambda i: (i, 0)),
        out_shape=jax.ShapeDtypeStruct((s, e), F32),
        compiler_params=_params("arbitrary"),
    )(dzb, w_out, dep)


def _mixer_bwd_call(dy, d_pool, proj, h_f, h_b, w_pool, pool_scale, p):
    s = dy.shape[0]
    c = h_f.shape[1]
    pg = p // N_POOL_GROUPS
    t = min(SEQ_TILE, s)
    n_tiles = s // t

    def body(dyp_ref, dyr_ref, d_ref, gate_ref, hf_ref, hb_ref, wp_ref, sc_ref,
             e_ref, dh_ref, dgate_ref, dwp_ref, dsc_ref, dd_s):
        i = pl.program_id(0)

        @pl.when(i == 0)
        def _():
            dwp_ref[...] = jnp.zeros_like(dwp_ref)
            dsc_ref[...] = jnp.zeros_like(dsc_ref)

        for g in range(N_POOL_GROUPS):
            cols = pl.ds(g * pg, pg)
            dg = d_ref[:, cols]
            out = jnp.dot(dg, wp_ref[g], preferred_element_type=F32)
            dyp = dyp_ref[:, cols]
            prod = dyp * out
            dsc_ref[:, cols] += jnp.sum(prod.reshape(t // SUBLANES, SUBLANES, pg), axis=0)
            dout = (dyp * sc_ref[:, cols]).astype(BF16)
            dwp_ref[g] += lax.dot_general(dg, dout, (((0,), (0,)), ((), ())), preferred_element_type=F32)
            dd_s[:, cols] = lax.dot_general(dout, wp_ref[g], (((1,), (1,)), ((), ())), preferred_element_type=F32)

        def chunk(r0, _):
            rs = pl.ds(r0, SUBLANES)
            for g, w in enumerate(POOL_WINDOWS):
                cols = pl.ds(g * pg, pg)
                cnt = _window_counts(r0, i, t, s, w // 2, (SUBLANES, pg))
                e_ref[rs, cols] = dd_s[rs, cols] / cnt
            gate, dgate = _gelu_and_grad(gate_ref[rs, :])
            dyr = dyr_ref[rs, :]
            dh_ref[rs, :] = dyr * gate
            dd_s[rs, :] = dyr * (hf_ref[rs, :] + hb_ref[rs, :]) * dgate

        _chunk_loop(t, chunk)
        dgate_ref[...] = dd_s[...].astype(BF16)

    tile = lambda i: (i, 0)
    return pl.pallas_call(
        body, name="mixer_bwd", grid=(n_tiles,),
        in_specs=[pl.BlockSpec((t, p), tile), pl.BlockSpec((t, c), lambda i: (i, 1)), pl.BlockSpec((t, p), tile),
                  pl.BlockSpec((t, c), lambda i: (i, 2)), pl.BlockSpec((t, c), tile), pl.BlockSpec((t, c), tile),
                  pl.BlockSpec((N_POOL_GROUPS, pg, pg), lambda i: (0, 0, 0)), pl.BlockSpec((1, p), lambda i: (0, 0))],
        out_specs=[pl.BlockSpec((t, p), tile), pl.BlockSpec((t, c), tile), pl.BlockSpec((t, c), tile),
                   pl.BlockSpec((N_POOL_GROUPS, pg, pg), lambda i: (0, 0, 0)), pl.BlockSpec((SUBLANES, p), lambda i: (0, 0))],
        out_shape=[jax.ShapeDtypeStruct((s, p), F32), jax.ShapeDtypeStruct((s, c), F32), jax.ShapeDtypeStruct((s, c), BF16),
                   jax.ShapeDtypeStruct((N_POOL_GROUPS, pg, pg), F32), jax.ShapeDtypeStruct((SUBLANES, p), F32)],
        scratch_shapes=[pltpu.VMEM((t, p), F32)],
        compiler_params=_params("arbitrary"),
    )(dy, dy, d_pool, proj, h_f, h_b, w_pool, pool_scale)


def _scan_bwd_call(xc, dh, h_dir, gates, dxc_prev, wa, wi, lam, reverse, dep):
    s, c = xc.shape
    heads = c // LRU_HEAD
    t = min(SEQ_TILE, s)
    n_tiles = s // t
    per = t // SUBLANES
    last_blk = s // SUBLANES - 1
    tile = (lambda i: (i, 0)) if reverse else (lambda i: (n_tiles - 1 - i, 0))
    if reverse:
        halo = lambda i: (jnp.minimum((i + 1) * per, last_blk), 0)
    else:
        halo = lambda i: (jnp.maximum((n_tiles - 1 - i) * per - 1, 0), 0)
    whole2 = lambda i: (0, 0)
    whole3 = lambda i: (0, 0, 0)
    has_prev = dxc_prev is not None
    n_in = 11 + int(has_prev) + 1

    def body(*refs):
        xc_ref, dh_ref, h_ref, hh_ref, r_ref, ig_ref, a_ref, mult_ref = refs[:8]
        prev_ref = refs[8] if has_prev else None
        wa_ref, wi_ref, lam_ref = refs[n_in - 4:n_in - 1]
        dxc_ref, dwa_ref, dwi_ref, dba_ref, dbi_ref, dsp_ref = refs[n_in:n_in + 6]
        pr_s, pi_s, carry_s = refs[n_in + 6:]
        step = pl.program_id(0)
        tile_idx = step if reverse else n_tiles - 1 - step

        @pl.when(step == 0)
        def _():
            carry_s[...] = jnp.zeros_like(carry_s)
            dwa_ref[...] = jnp.zeros_like(dwa_ref)
            dwi_ref[...] = jnp.zeros_like(dwi_ref)
            dba_ref[...] = jnp.zeros_like(dba_ref)
            dbi_ref[...] = jnp.zeros_like(dbi_ref)
            dsp_ref[...] = jnp.zeros_like(dsp_ref)

        sp = _softplus(-lam_ref[...])
        rows = lax.broadcasted_iota(jnp.int32, (SUBLANES, c), 0)

        def chunk(r0, u_in):
            rs = pl.ds(r0, SUBLANES)
            xcv = xc_ref[rs, :]
            r, ig, a, mult = r_ref[rs, :], ig_ref[rs, :], a_ref[rs, :], mult_ref[rs, :]
            dhv = dh_ref[rs, :]
            u = _scan_chunk(a, a * dhv, u_in, rows, not reverse)
            if reverse:
                gt = dhv + jnp.where(rows >= 1, pltpu.roll(u, 1, 0), u_in)
                u_out = u[SUBLANES - 1:SUBLANES, :]
            else:
                gt = dhv + jnp.where(rows < SUBLANES - 1, pltpu.roll(u, SUBLANES - 1, 0), u_in)
                u_out = u[0:1, :]
            cur = h_ref[rs, :]
            if reverse:
                after = h_ref[pl.ds(pl.multiple_of(jnp.minimum(r0 + SUBLANES, t - SUBLANES), SUBLANES), SUBLANES), :]
                edge = jnp.where(tile_idx == n_tiles - 1, 0.0, hh_ref[...])
                nxt = jnp.where(r0 == t - SUBLANES, edge, after)
                hs = _shift_rows(cur, cur, nxt, 1, rows)
            else:
                before = h_ref[pl.ds(pl.multiple_of(jnp.maximum(r0 - SUBLANES, 0), SUBLANES), SUBLANES), :]
                edge = jnp.where(tile_idx == 0, 0.0, hh_ref[...])
                prv = jnp.where(r0 == 0, edge, before)
                hs = _shift_rows(prv, cur, cur, -1, rows)
            gx = gt * xcv
            dmult = gx * ig
            di = gx * mult
            dlog_a = (gt * hs) * a - dmult * (a * a) / mult
            dr = dlog_a * (-RG_C * sp)
            dsp_ref[...] += dlog_a * (-RG_C * r)
            dpr = dr * r * (1.0 - r)
            dpi = di * ig * (1.0 - ig)
            dba_ref[...] += dpr
            dbi_ref[...] += dpi
            direct = gt * mult * ig
            if has_prev:
                direct = direct + prev_ref[rs, :]
            dxc_ref[rs, :] = direct
            pr_s[rs, :] = dpr
            pi_s[rs, :] = dpi
            return u_out

        carry_s[...] = _chunk_loop(t, chunk, carry_s[...], unroll=2, descending=not reverse)

        for h in range(heads):
            cs = pl.ds(h * LRU_HEAD, LRU_HEAD)
            xb = xc_ref[:, cs].astype(BF16)
            dprb = pr_s[:, cs].astype(BF16)
            dpib = pi_s[:, cs].astype(BF16)
            dwa_ref[h] += lax.dot_general(xb, dprb, (((0,), (0,)), ((), ())), preferred_element_type=F32)
            dwi_ref[h] += lax.dot_general(xb, dpib, (((0,), (0,)), ((), ())), preferred_element_type=F32)
            dxc_ref[:, cs] += (
                lax.dot_general(dprb, wa_ref[h], (((1,), (1,)), ((), ())), preferred_element_type=F32)
                + lax.dot_general(dpib, wi_ref[h], (((1,), (1,)), ((), ())), preferred_element_type=F32))

    tile_spec = pl.BlockSpec((t, c), tile)
    in_specs = [tile_spec, tile_spec, tile_spec, pl.BlockSpec((SUBLANES, c), halo)] + [tile_spec] * 4
    args = [xc, dh, h_dir, h_dir, *gates]
    if has_prev:
        in_specs.append(tile_spec)
        args.append(dxc_prev)
    in_specs += [pl.BlockSpec((heads, LRU_HEAD, LRU_HEAD), whole3), pl.BlockSpec((heads, LRU_HEAD, LRU_HEAD), whole3),
                 pl.BlockSpec((1, c), whole2), ANY]
    args += [wa, wi, lam, dep]
    assert len(args) == n_in
    return pl.pallas_call(
        body, name="scan_bwd_rev" if reverse else "scan_bwd", grid=(n_tiles,),
        in_specs=in_specs,
        out_specs=[tile_spec,
                   pl.BlockSpec((heads, LRU_HEAD, LRU_HEAD), whole3), pl.BlockSpec((heads, LRU_HEAD, LRU_HEAD), whole3),
                   pl.BlockSpec((SUBLANES, c), whole2), pl.BlockSpec((SUBLANES, c), whole2), pl.BlockSpec((SUBLANES, c), whole2)],
        out_shape=[jax.ShapeDtypeStruct((s, c), F32),
                   jax.ShapeDtypeStruct((heads, LRU_HEAD, LRU_HEAD), F32), jax.ShapeDtypeStruct((heads, LRU_HEAD, LRU_HEAD), F32),
                   jax.ShapeDtypeStruct((SUBLANES, c), F32), jax.ShapeDtypeStruct((SUBLANES, c), F32),
                   jax.ShapeDtypeStruct((SUBLANES, c), F32)],
        scratch_shapes=[pltpu.VMEM((t, c), F32), pltpu.VMEM((t, c), F32), pltpu.VMEM((1, c), F32)],
        compiler_params=_params("arbitrary"),
    )(*args)


def _dproj_call(e_pool, dxc, proj, dgate, conv_w, p):
    s, c = dxc.shape
    pg = p // N_POOL_GROUPS
    t = min(SEQ_TILE, s)
    n_tiles = s // t

    def body(e_ref, ep_ref, en_ref, dx_ref, dxp_ref, dxn_ref, u_ref, up_ref, un_ref, dgate_ref, w_ref,
             dproj_ref, dcw_ref, dcb_ref, st_s):
        i = pl.program_id(0)
        first, last = i == 0, i == n_tiles - 1

        @pl.when(first)
        def _():
            dcw_ref[...] = jnp.zeros_like(dcw_ref)
            dcb_ref[...] = jnp.zeros_like(dcb_ref)

        rows_p = lax.broadcasted_iota(jnp.int32, (SUBLANES, pg), 0)
        rows_c = lax.broadcasted_iota(jnp.int32, (SUBLANES, c), 0)
        w = w_ref[...]

        def chunk(r0, _):
            rs = pl.ds(r0, SUBLANES)
            for g, win in enumerate(POOL_WINDOWS):
                cols = pl.ds(g * pg, pg)
                prv, cur, nxt = _neighbour_chunks(e_ref, ep_ref, en_ref, r0, t, cols, first, last)
                tot = cur
                for o in range(-(win // 2) + 1, win // 2 + 1):
                    if o != 0:
                        tot = tot + _shift_rows(prv, cur, nxt, o, rows_p)
                cnt = _window_counts(r0, i, t, s, win // 2, (SUBLANES, pg))
                st_s[rs, cols] = tot - cur * cnt
            prv, cur, nxt = _neighbour_chunks(dx_ref, dxp_ref, dxn_ref, r0, t, slice(None), first, last)
            du = w[1:2] * cur
            du += w[0:1] * _shift_rows(prv, cur, nxt, 1, rows_c)
            du += w[2:3] * _shift_rows(prv, cur, nxt, -1, rows_c)
            du += w[3:4] * _shift_rows(prv, cur, nxt, -2, rows_c)
            st_s[rs, pl.ds(p, c)] = du
            uprv, ucur, unxt = _neighbour_chunks(u_ref, up_ref, un_ref, r0, t, slice(None), first, last)
            dcb_ref[...] += cur
            for j, o in enumerate((-1, 0, 1, 2)):
                dcw_ref[j] += cur * _shift_rows(uprv, ucur, unxt, o, rows_c)

        _chunk_loop(t, chunk)
        dproj_ref[:, pl.ds(0, p + c)] = st_s[...].astype(BF16)
        dproj_ref[:, pl.ds(p + c, c)] = dgate_ref[...]

    return pl.pallas_call(
        body, name="dproj", grid=(n_tiles,),
        in_specs=_halo_specs(t, s, p, 0) + _halo_specs(t, s, c, 0) + _halo_specs(t, s, c, 1) + [
            pl.BlockSpec((t, c), lambda i: (i, 0)), pl.BlockSpec((4, c), lambda i: (0, 0))],
        out_specs=[pl.BlockSpec((t, p + 2 * c), lambda i: (i, 0)),
                   pl.BlockSpec((4, SUBLANES, c), lambda i: (0, 0, 0)), pl.BlockSpec((SUBLANES, c), lambda i: (0, 0))],
        out_shape=[jax.ShapeDtypeStruct((s, p + 2 * c), BF16), jax.ShapeDtypeStruct((4, SUBLANES, c), F32),
                   jax.ShapeDtypeStruct((SUBLANES, c), F32)],
        scratch_shapes=[pltpu.VMEM((t, p + c), F32)],
        compiler_params=_params("arbitrary"),
    )(e_pool, e_pool, e_pool, dxc, dxc, dxc, proj, proj, proj, dgate, conv_w)


def _dx_call(dproj, w_in, dz1, dep):
    s, e = dproj.shape
    n, d, e4 = w_in.shape
    tm = min(MM_TILE, s)

    def body(dp_ref, w_ref, dz_ref, dep_ref, o_ref):
        k = pl.program_id(1)

        @pl.when(k == 0)
        def _():
            o_ref[...] = ALPHA * dz_ref[...]

        o_ref[...] += lax.dot_general(dp_ref[...], w_ref[...], (((1,), (1,)), ((), ())), preferred_element_type=F32)

    return pl.pallas_call(
        body, name="grad_x", grid=(s // tm, n),
        in_specs=[pl.BlockSpec((tm, e4), lambda i, k: (i, k)), pl.BlockSpec((None, d, e4), lambda i, k: (k, 0, 0)),
                  pl.BlockSpec((tm, d), lambda i, k: (i, 0)), ANY],
        out_specs=pl.BlockSpec((tm, d), lambda i, k: (i, 0)),
        out_shape=jax.ShapeDtypeStruct((s, d), F32),
        compiler_params=_params("arbitrary", "arbitrary"),
    )(dproj, w_in, dz1, dep)


def _row_tile(rows, cols, n_arrays):
    limit = max(SUBLANES, ELT_BLOCK_BYTES // (4 * cols * max(1, n_arrays // 4)))
    best = SUBLANES
    for cand in range(SUBLANES, min(rows, limit) + 1, SUBLANES):
        if rows % cand == 0:
            best = cand
    return best if rows % SUBLANES == 0 else rows


def _cast_to_slot_call(a, idx, dtype, name):
    rows, cols = a.shape
    tr = _row_tile(rows, cols, 2)

    def body(idx_ref, a_ref, o_ref):
        o_ref[...] = a_ref[...].astype(dtype)

    return pl.pallas_call(
        body, name=name,
        grid_spec=pltpu.PrefetchScalarGridSpec(
            num_scalar_prefetch=1, grid=(rows // tr,),
            in_specs=[pl.BlockSpec((tr, cols), lambda i, idx_ref: (i, 0))],
            out_specs=pl.BlockSpec((None, tr, cols), lambda i, idx_ref: (idx_ref[1], i, 0))),
        out_shape=jax.ShapeDtypeStruct((N_CHIPS, rows, cols), dtype),
        compiler_params=_params("arbitrary"),
    )(idx, a)


def _add_half_call(g, recv, idx, to_slot, name):
    _, rows, cols = g.shape
    tr = _row_tile(rows, cols, 3)

    def body(idx_ref, g_ref, r_ref, o_ref):
        o_ref[...] = g_ref[...] + r_ref[...]

    if to_slot:
        out_spec = pl.BlockSpec((None, tr, cols), lambda i, idx_ref: (idx_ref[1], i, 0))
        out_shape = jax.ShapeDtypeStruct((N_CHIPS, rows, cols), F32)
    else:
        out_spec = pl.BlockSpec((tr, cols), lambda i, idx_ref: (i, 0))
        out_shape = jax.ShapeDtypeStruct((rows, cols), F32)
    return pl.pallas_call(
        body, name=name,
        grid_spec=pltpu.PrefetchScalarGridSpec(
            num_scalar_prefetch=1, grid=(rows // tr,),
            in_specs=[pl.BlockSpec((None, tr, cols), lambda i, idx_ref: (idx_ref[0], i, 0)),
                      pl.BlockSpec((tr, cols), lambda i, idx_ref: (i, 0))],
            out_specs=out_spec),
        out_shape=out_shape,
        compiler_params=_params("arbitrary"),
    )(idx, g, recv)


def _sum_chips_call(own, recv, idx, name):
    _, rows, cols = recv.shape
    tr = _row_tile(rows, cols, 5)
    out_spec = pl.BlockSpec((None, tr, cols), lambda i, idx_ref: (idx_ref[0], i, 0))
    if own is None:
        def body(idx_ref, r_ref, o_ref):
            o_ref[...] = ((r_ref[0] + r_ref[1]) + r_ref[2]) + r_ref[3]
        in_specs = [pl.BlockSpec((N_CHIPS, tr, cols), lambda i, idx_ref: (0, i, 0))]
        args = (recv,)
    else:
        def body(idx_ref, p_ref, r_ref, o_ref):
            o_ref[...] = ((p_ref[...] + r_ref[0]) + r_ref[1]) + r_ref[2]
        in_specs = [pl.BlockSpec((None, tr, cols), lambda i, idx_ref: (idx_ref[1], i, 0)),
                    pl.BlockSpec((N_CHIPS - 1, tr, cols), lambda i, idx_ref: (0, i, 0))]
        args = (own, recv)
    return pl.pallas_call(
        body, name=name,
        grid_spec=pltpu.PrefetchScalarGridSpec(num_scalar_prefetch=1, grid=(rows // tr,), in_specs=in_specs, out_specs=out_spec),
        out_shape=jax.ShapeDtypeStruct((2, rows, cols), F32),
        compiler_params=_params("arbitrary"),
    )(idx, *args)


def _adamw_call(g, w, m, v, name):
    rows, cols = w.shape
    tr = _row_tile(rows, cols, 7)

    def body(g_ref, w_ref, m_ref, v_ref, d_ref, mo_ref, vo_ref):
        gv = g_ref[...]
        mn = ADAM_B1 * m_ref[...] + (1.0 - ADAM_B1) * gv
        vn = ADAM_B2 * v_ref[...] + (1.0 - ADAM_B2) * (gv * gv)
        m_hat = mn / (1.0 - ADAM_B1 ** ADAM_STEP)
        v_hat = vn / (1.0 - ADAM_B2 ** ADAM_STEP)
        d_ref[...] = -ADAM_LR * (m_hat / (jnp.sqrt(v_hat) + ADAM_EPS) + ADAM_WD * w_ref[...])
        mo_ref[...] = mn
        vo_ref[...] = vn

    spec = pl.BlockSpec((tr, cols), lambda i: (i, 0))
    shape = jax.ShapeDtypeStruct((rows, cols), F32)
    return pl.pallas_call(
        body, name=name, grid=(rows // tr,),
        in_specs=[spec] * 4, out_specs=[spec] * 3, out_shape=[shape] * 3,
        compiler_params=_params("arbitrary"),
    )(g, w, m, v)


def _mesh_place():
    x, y, c = lax.axis_index("x"), lax.axis_index("y"), lax.axis_index("c")
    chips = [(1 - x, y), (x, 1 - y), (1 - x, 1 - y)]
    return x, y, c, chips


def _remote(src, dst, send_sems, recv_sems, idx, device):
    return pltpu.make_async_remote_copy(src_ref=src, dst_ref=dst, send_sem=send_sems.at[idx], recv_sem=recv_sems.at[idx],
                                        device_id=device, device_id_type=MESH)


HBM_SPEC = pl.BlockSpec(memory_space=pltpu.HBM)
SEM_SPEC = pl.BlockSpec(memory_space=pltpu.SEMAPHORE)
ORDERED_EFFECT = pltpu.SideEffectType.DATAFLOW_SIDE_EFFECTING


def _in_hbm(a):
    return pltpu.with_memory_space_constraint(a, pltpu.HBM)


def _start_copies_call(name, bufs, groups, after=None):
    n, g = len(bufs), len(groups)
    extra = [] if after is None else [after]
    first_out = n + len(extra)

    def body(*refs):
        outs = refs[first_out:first_out + n]
        sems = refs[first_out + n:first_out + n + 2 * g]
        token = refs[first_out + n + 2 * g]
        for i, (which, copies_fn, _) in enumerate(groups):
            for mine, _ in copies_fn([outs[w] for w in which], sems[2 * i], sems[2 * i + 1]):
                mine.start()
        token[...] = jnp.zeros_like(token)

    sem_shapes = [pltpu.SemaphoreType.DMA((cnt,)) for _, _, cnt in groups for _ in range(2)]
    res = pl.pallas_call(
        body, name=name,
        in_specs=[HBM_SPEC] * n + [ANY] * len(extra),
        out_specs=[HBM_SPEC] * n + [SEM_SPEC] * (2 * g) + [pl.BlockSpec(memory_space=pltpu.VMEM)],
        out_shape=[pltpu.HBM(a.shape, a.dtype) for a in bufs] + sem_shapes + [jax.ShapeDtypeStruct((SUBLANES, LANES), F32)],
        input_output_aliases={a: a for a in range(n)},
        compiler_params=pltpu.CompilerParams(has_side_effects=ORDERED_EFFECT),
    )(*[_in_hbm(a) for a in bufs], *extra)
    sems = res[n:n + 2 * g]
    return list(res[:n]), [(sems[2 * i], sems[2 * i + 1]) for i in range(g)], res[n + 2 * g]


def _wait_copies_call(name, bufs, sems, copies_fn, after):
    n = len(bufs)

    def body(*refs):
        ins = refs[:n]
        send_sems, recv_sems = refs[n], refs[n + 1]
        for mine, arriving in copies_fn(list(ins), send_sems, recv_sems):
            arriving.wait_recv()
            mine.wait_send()

    res = pl.pallas_call(
        body, name=name,
        in_specs=[HBM_SPEC] * n + [SEM_SPEC, SEM_SPEC, ANY],
        out_specs=[HBM_SPEC] * n,
        out_shape=[pltpu.HBM(a.shape, a.dtype) for a in bufs],
        input_output_aliases={a: a for a in range(n)},
        compiler_params=pltpu.CompilerParams(has_side_effects=ORDERED_EFFECT),
    )(*bufs, sems[0], sems[1], after)
    return list(res)


def _gather_copies(bufs, send_sems, recv_sems):
    x, y, c, chips = _mesh_place()
    k = 2 * x + y
    out = []
    for a, buf in enumerate(bufs):
        for j, (px, py) in enumerate(chips):
            kj = 2 * px + py
            mine = _remote(buf.at[k, c], buf.at[k, c], send_sems, recv_sems, 3 * a + j, (px, py, c))
            arriving = _remote(buf.at[k, c], buf.at[kj, c], send_sems, recv_sems, 3 * a + j, (px, py, c))
            out.append((mine, arriving))
    return out


def _exchange_copies(n_sharded, n_replicated):
    def copies(bufs, send_sems, recv_sems):
        x, y, c, chips = _mesh_place()
        k = 2 * x + y
        sums, lands = bufs[:n_sharded], bufs[n_sharded:2 * n_sharded]
        repl = bufs[2 * n_sharded:]
        out = []
        for j, (px, py) in enumerate(chips):
            kj = 2 * px + py
            for a in range(n_sharded):
                cp = _remote(sums[a].at[kj], lands[a].at[j], send_sems, recv_sems, 3 * a + j, (px, py, c))
                out.append((cp, cp))
            for a in range(n_replicated):
                idx = 3 * (n_sharded + a) + j
                mine = _remote(repl[a].at[k], repl[a].at[k], send_sems, recv_sems, idx, (px, py, c))
                arriving = _remote(repl[a].at[k], repl[a].at[kj], send_sems, recv_sems, idx, (px, py, c))
                out.append((mine, arriving))
        return out
    return copies


def _sibling_copies(n, halves):
    def copies(bufs, send_sems, recv_sems):
        x, y, c, _ = _mesh_place()
        out = []
        for a in range(n):
            src = bufs[a].at[1 - c] if halves else bufs[a]
            cp = _remote(src, bufs[n + a], send_sems, recv_sems, a, (x, y, 1 - c))
            out.append((cp, cp))
        return out
    return copies


def _forward_copies(bufs, send_sems, recv_sems):
    x, y, c, chips = _mesh_place()
    out = []
    for a, buf in enumerate(bufs):
        for j, (px, py) in enumerate(chips):
            kj = 2 * px + py
            mine = _remote(buf.at[kj, c], buf.at[kj, c], send_sems, recv_sems, 3 * a + j, (x, y, 1 - c))
            arriving = _remote(buf.at[kj, c], buf.at[kj, 1 - c], send_sems, recv_sems, 3 * a + j, (x, y, 1 - c))
            out.append((mine, arriving))
    return out


def _forward_to_sibling_call(bufs, name):
    n = len(bufs)

    def body(*refs):
        ins, outs = refs[:n], refs[n:2 * n]
        send_sems, recv_sems = refs[2 * n:]
        x, y, c, chips = _mesh_place()
        sibling = (x, y, 1 - c)
        sends = []
        for a in range(n):
            for j, (px, py) in enumerate(chips):
                kj = 2 * px + py
                sends.append(_remote(ins[a].at[kj, c], outs[a].at[kj, c], send_sems, recv_sems, 3 * a + j, sibling))
        for cp in sends:
            cp.start()
        for a in range(n):
            for j, (px, py) in enumerate(chips):
                kj = 2 * px + py
                _remote(ins[a].at[kj, c], outs[a].at[kj, 1 - c], send_sems, recv_sems, 3 * a + j, sibling).wait_recv()
        for cp in sends:
            cp.wait_send()

    return pl.pallas_call(
        body, name=name,
        in_specs=[ANY] * n, out_specs=[ANY] * n,
        out_shape=[jax.ShapeDtypeStruct(a.shape, a.dtype) for a in bufs],
        input_output_aliases={a: a for a in range(n)},
        scratch_shapes=[pltpu.SemaphoreType.DMA((3 * n,)), pltpu.SemaphoreType.DMA((3 * n,))],
    )(*bufs)


def _join_halves_call(bufs, name):
    n = len(bufs)

    def body(*refs):
        ins, outs = refs[:n], refs[n:2 * n]
        send_sems, recv_sems = refs[2 * n:]
        x, y, c, _ = _mesh_place()
        sibling = (x, y, 1 - c)
        copies = [_remote(ins[a].at[c], outs[a].at[c], send_sems, recv_sems, a, sibling) for a in range(n)]
        for cp in copies:
            cp.start()
        for a in range(n):
            _remote(ins[a].at[c], outs[a].at[1 - c], send_sems, recv_sems, a, sibling).wait_recv()
        for cp in copies:
            cp.wait_send()

    return pl.pallas_call(
        body, name=name,
        in_specs=[ANY] * n, out_specs=[ANY] * n,
        out_shape=[jax.ShapeDtypeStruct(a.shape, a.dtype) for a in bufs],
        input_output_aliases={a: a for a in range(n)},
        scratch_shapes=[pltpu.SemaphoreType.DMA((n,)), pltpu.SemaphoreType.DMA((n,))],
    )(*bufs)


def _pack(arrays, rows_multiple):
    flat = jnp.concatenate([a.reshape(-1) for a in arrays])
    per = LANES * rows_multiple
    padded = -(-flat.shape[0] // per) * per
    flat = jnp.pad(flat, (0, padded - flat.shape[0]))
    return flat.reshape(-1, LANES)


def _unpack(packed, shapes):
    flat = packed.reshape(-1)
    out, at = [], 0
    for shp in shapes:
        size = 1
        for dim in shp:
            size *= dim
        out.append(flat[at:at + size].reshape(shp))
        at += size
    return out


def _halves(a):
    return a.reshape((2, a.shape[0] // 2) + a.shape[1:])


def kernel(x, ln_mix_g, ln_mix_b, w_in, w_pool, pool_scale, conv_w, conv_b, w_rg_a, b_rg_a, w_rg_i, b_rg_i, rg_lambda, w_out, ln_ffn_g, ln_ffn_b, w_mlp_in, w_mlp_out, loss_target, m_ln_mix_g, m_ln_mix_b, m_w_in, m_w_pool, m_pool_scale, m_conv_w, m_conv_b, m_w_rg_a, m_b_rg_a, m_w_rg_i, m_b_rg_i, m_rg_lambda, m_w_out, m_ln_ffn_g, m_ln_ffn_b, m_w_mlp_in, m_w_mlp_out, v_ln_mix_g, v_ln_mix_b, v_w_in, v_w_pool, v_pool_scale, v_conv_w, v_conv_b, v_w_rg_a, v_b_rg_a, v_w_rg_i, v_b_rg_i, v_rg_lambda, v_w_out, v_ln_ffn_g, v_ln_ffn_b, v_w_mlp_in, v_w_mlp_out):
    weights = dict(ln_mix_g=ln_mix_g, ln_mix_b=ln_mix_b, w_in=w_in, w_pool=w_pool, pool_scale=pool_scale, conv_w=conv_w,
                   conv_b=conv_b, w_rg_a=w_rg_a, b_rg_a=b_rg_a, w_rg_i=w_rg_i, b_rg_i=b_rg_i, rg_lambda=rg_lambda,
                   w_out=w_out, ln_ffn_g=ln_ffn_g, ln_ffn_b=ln_ffn_b, w_mlp_in=w_mlp_in, w_mlp_out=w_mlp_out)
    m_in = dict(ln_mix_g=m_ln_mix_g, ln_mix_b=m_ln_mix_b, w_in=m_w_in, w_pool=m_w_pool, pool_scale=m_pool_scale,
                conv_w=m_conv_w, conv_b=m_conv_b, w_rg_a=m_w_rg_a, b_rg_a=m_b_rg_a, w_rg_i=m_w_rg_i, b_rg_i=m_b_rg_i,
                rg_lambda=m_rg_lambda, w_out=m_w_out, ln_ffn_g=m_ln_ffn_g, ln_ffn_b=m_ln_ffn_b, w_mlp_in=m_w_mlp_in,
                w_mlp_out=m_w_mlp_out)
    v_in = dict(ln_mix_g=v_ln_mix_g, ln_mix_b=v_ln_mix_b, w_in=v_w_in, w_pool=v_w_pool, pool_scale=v_pool_scale,
                conv_w=v_conv_w, conv_b=v_conv_b, w_rg_a=v_w_rg_a, b_rg_a=v_b_rg_a, w_rg_i=v_w_rg_i, b_rg_i=v_b_rg_i,
                rg_lambda=v_rg_lambda, w_out=v_w_out, ln_ffn_g=v_ln_ffn_g, ln_ffn_b=v_ln_ffn_b, w_mlp_in=v_w_mlp_in,
                w_mlp_out=v_w_mlp_out)
    names = list(weights)

    xs = x[0]
    tgt = loss_target[0]
    s, d = xs.shape
    p = c = d // 2
    pg = p // N_POOL_GROUPS
    heads = c // LRU_HEAD
    core = lax.axis_index("c")
    shard = 2 * lax.axis_index("x") + lax.axis_index("y")

    idx = jnp.stack([core, shard]).astype(jnp.int32)
    small_shard = _pack([conv_w[0], b_rg_a[0], b_rg_i[0], rg_lambda[0]], 2 * SUBLANES)
    to_gather = [(w_in[0], BF16), (w_out[0], BF16), (w_mlp_in[0], BF16), (w_mlp_out[0], BF16),
                 (w_pool[0].reshape(-1, pg), BF16), (small_shard, F32)]

    def slot_view(i):
        a, dt = to_gather[i]
        sl = _cast_to_slot_call(a, idx, dt, f"gather_slot_{i}")
        return sl.reshape(N_CHIPS, 2, sl.shape[1] // 2, sl.shape[2])

    first, later = (0, 4, 5), (1, 2, 3)
    fly_a, sems_a, token_a = _start_copies_call(
        "gather_start_first", [slot_view(i) for i in first], [((0, 1, 2), _gather_copies, 3 * len(first))])
    fly_b, sems_b, g_token = _start_copies_call(
        "gather_start_later", [slot_view(i) for i in later],
        [((0,), _gather_copies, 3), ((1,), _gather_copies, 3), ((2,), _gather_copies, 3)], token_a)
    in_flight = {**dict(zip(first, fly_a)), **dict(zip(later, fly_b))}
    g_sems = [sems_a[0]] + list(sems_b)

    def arrive(which, group, after, tag):
        return _wait_copies_call(f"gather_wait_{tag}", [in_flight[w] for w in which], g_sems[group], _gather_copies, after)

    def pass_on(got, tag):
        flying, sems, token = _start_copies_call(
            f"gather_forward_start_{tag}", got, [(tuple(range(len(got))), _forward_copies, 3 * len(got))])
        return (flying, sems[0], tag), token

    def passed_on(state, after):
        flying, sems, tag = state
        return _wait_copies_call(f"gather_forward_wait_{tag}", flying, sems, _forward_copies, after)

    gathered = [None] * len(to_gather)
    xb = _cast_call(xs, g_token)
    gathered[0], gathered[4], gathered[5] = _forward_to_sibling_call(arrive(first, 0, xb, "w_in"), "gather_forward_w_in")
    w_in_f = gathered[0].reshape((N_CHIPS,) + w_in.shape[1:])
    w_pool_f = gathered[4].reshape(N_CHIPS, N_POOL_GROUPS, pg // N_CHIPS, pg).transpose(1, 0, 2, 3).reshape(N_POOL_GROUPS, pg, pg)
    c4 = c // N_CHIPS
    small_parts = [_unpack(gathered[5][k].reshape(-1, LANES), [(4, c4), (2, c4), (2, c4), (2, c4)]) for k in range(N_CHIPS)]
    conv_w_f = jnp.concatenate([sp_[0] for sp_ in small_parts], axis=1)
    b_a_f = jnp.concatenate([sp_[1] for sp_ in small_parts], axis=1)
    b_i_f = jnp.concatenate([sp_[2] for sp_ in small_parts], axis=1)
    lam_f = jnp.concatenate([sp_[3] for sp_ in small_parts], axis=1)
    wa_b = w_rg_a[0].astype(BF16)
    wi_b = w_rg_i[0].astype(BF16)

    proj = _proj_call(xb, w_in_f)
    xc = _conv_call(proj, conv_w_f, conv_b, c)
    fwd_w_out, token = pass_on(arrive((1,), 1, xc, "w_out"), "w_out")
    h_b, *gates_b = _scan_fwd_call(xc, wa_b[1], wi_b[1], b_a_f[1:2], b_i_f[1:2], lam_f[1:2], True, token)
    h_f, *gates_f = _scan_fwd_call(xc, wa_b[0], wi_b[0], b_a_f[0:1], b_i_f[0:1], lam_f[0:1], False, token)
    y, d_pool = _pool_combine_call(proj, h_f, h_b, w_pool_f, pool_scale, p)
    w_out_f = passed_on(fwd_w_out, y)[0].reshape(d, d)
    fwd_w1, token = pass_on(arrive((2,), 2, y, "w_mlp_in"), "w_mlp_in")
    xh1, x1b, rstd1 = _out_ln1_call(y, w_out_f, xs, ln_mix_g, ln_mix_b, token)
    w1_f = passed_on(fwd_w1, x1b)[0].reshape((N_CHIPS,) + w_mlp_in.shape[1:])
    fwd_w2, token = pass_on(arrive((3,), 3, x1b, "w_mlp_out"), "w_mlp_out")
    r_act, hsq = _mlp_in_call(x1b, w1_f, token)
    w2_f = passed_on(fwd_w2, hsq)[0].reshape(N_CHIPS * w_mlp_out.shape[1], d)
    dz2, dz2b, loss8, dg2, db2 = _mlp_out_ln2_call(hsq, w2_f, xh1, ln_mix_g, ln_mix_b, ln_ffn_g, ln_ffn_b, tgt)

    def start_siblings(grads, halves, tag, after=None):
        lands = [lax.empty(g.shape[1:] if halves else g.shape, g.dtype) for g in grads]
        copies = _sibling_copies(len(grads), halves)
        flying, sems, token = _start_copies_call(
            f"siblings_start_{tag}", list(grads) + lands, [(tuple(range(2 * len(grads))), copies, len(grads))], after)
        return (flying, sems[0], copies, len(grads), tag), token

    def finish_siblings(state, after):
        flying, sems, copies, n, tag = state
        got = _wait_copies_call(f"siblings_wait_{tag}", flying, sems, copies, after)
        return got[:n], got[n:]

    half_own = jnp.reshape(core, (1,)).astype(jnp.int32)
    half_sibling = 1 - half_own

    def chip_sum_of(a, b, row_sharded, tag, dep, overlapped):
        for_sibling = _half_grad_call(a, b, half_sibling, row_sharded, None, f"grad_{tag}_for_sibling", dep)
        state, token = start_siblings([for_sibling], False, tag)
        results = overlapped(token)
        _, (from_sibling,) = finish_siblings(state, results[0])
        return _half_grad_call(a, b, half_own, row_sharded, from_sibling, f"grad_{tag}", token), results

    def start_exchange(sums, n_repl, tag):
        n_sh = len(sums) - n_repl
        lands = [lax.empty((N_CHIPS - 1,) + a.shape[1:], a.dtype) for a in sums[:n_sh]]
        bufs = sums[:n_sh] + lands + sums[n_sh:]
        copies = _exchange_copies(n_sh, n_repl)
        flying, sems, token = _start_copies_call(
            f"reduce_start_{tag}", bufs, [(tuple(range(len(bufs))), copies, 3 * len(sums))])
        return (flying, sems[0], copies, n_sh, tag), token

    def finish_exchange(state, after):
        flying, sems, copies, n_sh, tag = state
        got = _wait_copies_call(f"reduce_wait_{tag}", flying, sems, copies, after)
        halves = []
        for a in range(n_sh):
            own, land = got[a], got[n_sh + a]
            cols = own.shape[-1]
            total = _sum_chips_call(own.reshape(N_CHIPS, -1, cols), land.reshape(N_CHIPS - 1, -1, cols), idx,
                                    f"reduce_sum_{tag}_{a}")
            halves.append(total.reshape((2,) + own.shape[1:]))
        for a, rp in enumerate(got[2 * n_sh:]):
            halves.append(_sum_chips_call(None, rp, idx, f"reduce_sum_{tag}_r{a}"))
        return _join_halves_call(halves, f"reduce_join_{tag}")

    sum_w2, (dpre,) = chip_sum_of(hsq, dz2b, True, "w_mlp_out", g_token,
                                  lambda tok: (_dhsq_call(dz2b, w2_f, r_act, tok),))
    flying_w2, token = start_exchange([sum_w2], 0, "w2")
    sum_w1, (dz1, dz1b, dg1, db1) = chip_sum_of(
        x1b, dpre, False, "w_mlp_in", token,
        lambda tok: _dx1_ln1_bwd_call(dpre, w1_f, dz2, xh1, rstd1, ln_mix_g, tok))
    flying_w1, token = start_exchange([sum_w1], 0, "w1")

    def dy_and_mixer(tok):
        dy = _dy_call(dz1b, w_out_f, tok)
        return _mixer_bwd_call(dy, d_pool, proj, h_f, h_b, w_pool_f, pool_scale, p)

    sum_wout, (e_pool, dh, dgate, g_wpool, g_pscale8) = chip_sum_of(y, dz1b, True, "w_out", token, dy_and_mixer)
    flying_wout, token = start_exchange([sum_wout], 0, "w_out")
    dxc0, g_wa0, g_wi0, g_ba0, g_bi0, g_sp0 = _scan_bwd_call(
        xc, dh, h_f, gates_f, None, wa_b[0], wi_b[0], lam_f[0:1], False, token)
    dxc, g_wa1, g_wi1, g_ba1, g_bi1, g_sp1 = _scan_bwd_call(
        xc, dh, h_b, gates_b, dxc0, wa_b[1], wi_b[1], lam_f[1:2], True, token)
    dproj, g_cw8, g_cb8 = _dproj_call(e_pool, dxc, proj, dgate, conv_w_f, p)

    rowsum = lambda a8: jnp.sum(a8, axis=-2)
    g_lam = jnp.stack([rowsum(g_sp0), rowsum(g_sp1)]) * (-_sigmoid(-lam_f))
    small_grads = {
        "ln_mix_g": rowsum(dg1), "ln_mix_b": rowsum(db1), "ln_ffn_g": rowsum(dg2), "ln_ffn_b": rowsum(db2),
        "pool_scale": rowsum(g_pscale8), "conv_b": rowsum(g_cb8),
        "w_rg_a": jnp.stack([g_wa0, g_wa1]), "w_rg_i": jnp.stack([g_wi0, g_wi1]),
        "w_pool": g_wpool, "conv_w": rowsum(g_cw8),
        "b_rg_a": jnp.stack([rowsum(g_ba0), rowsum(g_ba1)]), "b_rg_i": jnp.stack([rowsum(g_bi0), rowsum(g_bi1)]),
        "rg_lambda": g_lam,
    }
    small_names = list(small_grads)
    small_shapes = [small_grads[nm].shape for nm in small_names]
    g_small = _halves(_pack([small_grads[nm] for nm in small_names], 2 * SUBLANES))
    sib_small, token = start_siblings([g_small], True, "small")
    flying_small = []

    def small_exchange_and_grad_x(tok):
        (mine,), (theirs,) = finish_siblings(sib_small, tok)
        small_sum = _add_half_call(mine, theirs, idx, True, "reduce_add_small")
        state, tok = start_exchange([small_sum], 1, "small")
        flying_small.append(state)
        return (_dx_call(dproj, w_in_f, dz1, tok),)

    sum_win, (grad_x,) = chip_sum_of(xb, dproj, False, "w_in", token, small_exchange_and_grad_x)
    flying_small = flying_small[0]
    flying_win, token = start_exchange([sum_win], 0, "w_in")

    grad_w, delta_w, new_m, new_v = {}, {}, {}, {}

    def adamw(nm, full):
        w2d = weights[nm][0]
        g2d = full.reshape(w2d.shape)
        dl, mn, vn = _adamw_call(g2d, w2d, m_in[nm][0], v_in[nm][0], f"adamw_{nm}")
        grad_w[nm], delta_w[nm], new_m[nm], new_v[nm] = g2d[None], dl[None], mn[None], vn[None]
        return vn

    last = adamw("w_mlp_out", finish_exchange(flying_w2, token)[0])
    last = adamw("w_mlp_in", finish_exchange(flying_w1, last)[0])
    last = adamw("w_out", finish_exchange(flying_wout, last)[0])

    small_full = dict(zip(small_names, _unpack(finish_exchange(flying_small, last)[0].reshape(-1, LANES), small_shapes)))
    local = dict(small_full)
    local["w_pool"] = lax.dynamic_slice_in_dim(small_full["w_pool"], shard * (pg // N_CHIPS), pg // N_CHIPS, axis=1)
    for nm in ("conv_w", "b_rg_a", "b_rg_i", "rg_lambda"):
        local[nm] = lax.dynamic_slice_in_dim(small_full[nm], shard * c4, c4, axis=1)
    small_w_shapes = [weights[nm].shape for nm in small_names]
    g_pack = _pack([local[nm] for nm in small_names], SUBLANES)
    w_pack = _pack([weights[nm] for nm in small_names], SUBLANES)
    m_pack = _pack([m_in[nm] for nm in small_names], SUBLANES)
    v_pack = _pack([v_in[nm] for nm in small_names], SUBLANES)
    dl_p, mn_p, vn_p = _adamw_call(g_pack, w_pack, m_pack, v_pack, "adamw_small")
    for nm, gl, dl, mn, vn in zip(small_names, _unpack(g_pack, small_w_shapes), _unpack(dl_p, small_w_shapes),
                                  _unpack(mn_p, small_w_shapes), _unpack(vn_p, small_w_shapes)):
        grad_w[nm], delta_w[nm], new_m[nm], new_v[nm] = gl, dl, mn, vn
    adamw("w_in", finish_exchange(flying_win, vn_p)[0])

    loss = lax.psum(jnp.sum(loss8) * (0.5 / d), ("x", "y", "c"))
    return (loss, grad_x[None], *[grad_w[nm] for nm in names], *[delta_w[nm] for nm in names],
            *[new_m[nm] for nm in names], *[new_v[nm] for nm in names])
```

```python
import functools

import jax
import jax.numpy as jnp
from jax import lax
from jax.experimental import pallas as pl
from jax.experimental.pallas import tpu as pltpu

F32 = jnp.float32
BF16 = jnp.bfloat16

N_CHIPS = 4
LANES = 128
SUBLANES = 8
LRU_HEAD = 128
N_POOL_GROUPS = 4
POOL_WINDOWS = (2, 4, 8, 16)
RG_C = 8.0
LN_EPS = 1e-5
ALPHA = 2.0 ** 0.25
ADAM_LR, ADAM_B1, ADAM_B2, ADAM_EPS, ADAM_WD, ADAM_STEP = 0.001, 0.9, 0.999, 1e-08, 0.01, 10
VMEM_LIMIT = 56 * 1024 * 1024
SEQ_TILE = 256
MM_TILE = 512
LN_MM_K = 2048
LN_UNROLL = 4
ELT_BLOCK_BYTES = 2 * 1024 * 1024
RESIDENT_OPERAND_BYTES = 16 * 1024 * 1024
MESH = pl.DeviceIdType.MESH
ANY = pl.BlockSpec(memory_space=pl.ANY)


def _params(*sem):
    return pltpu.CompilerParams(dimension_semantics=sem, vmem_limit_bytes=VMEM_LIMIT)


def _sigmoid(z):
    return 1.0 / (1.0 + jnp.exp(-z))


def _neg_expm1(z):
    series = -(z * (1.0 + z * (0.5 + z * (1.0 / 6.0 + z * (1.0 / 24.0)))))
    return jnp.where(z > -0.01, series, 1.0 - jnp.exp(z))


def _softplus(z):
    return jnp.maximum(z, 0.0) + jnp.log1p(jnp.exp(-jnp.abs(z)))


_GELU_K = 0.7978845608028654
_GELU_C = 0.044715


def _gelu_and_grad(u):
    t = jnp.tanh(_GELU_K * (u + _GELU_C * (u * u * u)))
    g = 0.5 * u * (1.0 + t)
    dg = 0.5 * (1.0 + t) + 0.5 * u * (1.0 - t * t) * (_GELU_K * (1.0 + 3.0 * _GELU_C * u * u))
    return g, dg


def _shift_rows(prv, cur, nxt, o, rows):
    if o == 0:
        return cur
    if o == SUBLANES:
        return nxt
    if o == -SUBLANES:
        return prv
    if o > 0:
        s = SUBLANES - o
        return jnp.where(rows < s, pltpu.roll(cur, s, 0), pltpu.roll(nxt, s, 0))
    p = -o
    return jnp.where(rows >= p, pltpu.roll(cur, p, 0), pltpu.roll(prv, p, 0))


def _neighbour_chunks(main_ref, prev_ref, next_ref, r0, t_rows, cols, first_tile, last_tile):
    cur = main_ref[pl.ds(r0, SUBLANES), cols]
    before = main_ref[pl.ds(pl.multiple_of(jnp.maximum(r0 - SUBLANES, 0), SUBLANES), SUBLANES), cols]
    after = main_ref[pl.ds(pl.multiple_of(jnp.minimum(r0 + SUBLANES, t_rows - SUBLANES), SUBLANES), SUBLANES), cols]
    halo_prev = jnp.where(first_tile, 0.0, prev_ref[:, cols])
    halo_next = jnp.where(last_tile, 0.0, next_ref[:, cols])
    prv = jnp.where(r0 == 0, halo_prev, before)
    nxt = jnp.where(r0 == t_rows - SUBLANES, halo_next, after)
    return prv, cur, nxt


def _halo_specs(t_rows, n_rows, width, col_block):
    per = t_rows // SUBLANES
    last = n_rows // SUBLANES - 1
    return [
        pl.BlockSpec((t_rows, width), lambda i: (i, col_block)),
        pl.BlockSpec((SUBLANES, width), lambda i: (jnp.maximum(i * per - 1, 0), col_block)),
        pl.BlockSpec((SUBLANES, width), lambda i: (jnp.minimum((i + 1) * per, last), col_block)),
    ]


def _chunk_loop(t_rows, fn, init=None, unroll=1, descending=False):
    span = SUBLANES * unroll

    def step(ci, carry):
        base = pl.multiple_of(((t_rows // span - 1 - ci) if descending else ci) * span, span)
        for u in range(unroll):
            carry = fn(base + ((unroll - 1 - u) if descending else u) * SUBLANES, carry)
        return carry
    return lax.fori_loop(0, t_rows // span, step, init)


def _scan_chunk(a, b, h_in, rows, reverse):
    for dist in (1, 2, 4):
        if reverse:
            keep = rows < SUBLANES - dist
            shift = SUBLANES - dist
        else:
            keep = rows >= dist
            shift = dist
        b = a * jnp.where(keep, pltpu.roll(b, shift, 0), 0.0) + b
        a = a * jnp.where(keep, pltpu.roll(a, shift, 0), 1.0)
    return a * h_in + b


def _cast_call(x, dep):
    s, d = x.shape
    tm = min(MM_TILE, s)

    def body(x_ref, dep_ref, o_ref):
        o_ref[...] = x_ref[...].astype(BF16)

    return pl.pallas_call(
        body, name="cast_x", grid=(s // tm,),
        in_specs=[pl.BlockSpec((tm, d), lambda i: (i, 0)), ANY],
        out_specs=pl.BlockSpec((tm, d), lambda i: (i, 0)),
        out_shape=jax.ShapeDtypeStruct((s, d), BF16),
        compiler_params=_params("arbitrary"),
    )(x, dep)


def _proj_call(xb, w_in):
    s, d = xb.shape
    n, _, e4 = w_in.shape
    tm = min(2 * MM_TILE, s)

    def body(x_ref, w_ref, proj_ref):
        proj_ref[...] = jnp.dot(x_ref[...], w_ref[...], preferred_element_type=F32)

    return pl.pallas_call(
        body, name="proj", grid=(s // tm, n),
        in_specs=[pl.BlockSpec((tm, d), lambda i, j: (i, 0)), pl.BlockSpec((None, d, e4), lambda i, j: (j, 0, 0))],
        out_specs=pl.BlockSpec((tm, e4), lambda i, j: (i, j)),
        out_shape=jax.ShapeDtypeStruct((s, n * e4), F32),
        compiler_params=_params("arbitrary", "arbitrary"),
    )(xb, w_in)


def _conv_call(proj, conv_w, conv_b, c):
    s = proj.shape[0]
    t = min(SEQ_TILE, s)
    n_tiles = s // t

    def body(u_ref, up_ref, un_ref, w_ref, b_ref, xc_ref):
        i = pl.program_id(0)
        rows = lax.broadcasted_iota(jnp.int32, (SUBLANES, c), 0)
        w = w_ref[...]
        b = b_ref[...]

        def chunk(r0, _):
            prv, cur, nxt = _neighbour_chunks(u_ref, up_ref, un_ref, r0, t, slice(None), i == 0, i == n_tiles - 1)
            acc = b + w[1:2] * cur
            acc += w[0:1] * _shift_rows(prv, cur, nxt, -1, rows)
            acc += w[2:3] * _shift_rows(prv, cur, nxt, 1, rows)
            acc += w[3:4] * _shift_rows(prv, cur, nxt, 2, rows)
            xc_ref[pl.ds(r0, SUBLANES), :] = acc

        _chunk_loop(t, chunk)

    return pl.pallas_call(
        body, name="conv_fwd", grid=(n_tiles,),
        in_specs=_halo_specs(t, s, c, 1) + [pl.BlockSpec((4, c), lambda i: (0, 0)), pl.BlockSpec((1, c), lambda i: (0, 0))],
        out_specs=pl.BlockSpec((t, c), lambda i: (i, 0)),
        out_shape=jax.ShapeDtypeStruct((s, c), F32),
        compiler_params=_params("arbitrary"),
    )(proj, proj, proj, conv_w, conv_b)


def _gate_matmuls(xc_ref, wa_ref, wi_ref, pr_s, pi_s, heads):
    for h in range(heads):
        cs = pl.ds(h * LRU_HEAD, LRU_HEAD)
        xb = xc_ref[:, cs].astype(BF16)
        pr_s[:, cs] = jnp.dot(xb, wa_ref[h], preferred_element_type=F32)
        pi_s[:, cs] = jnp.dot(xb, wi_ref[h], preferred_element_type=F32)


def _rg_gates(pr, pi, ba, bi, sp):
    r = _sigmoid(pr + ba)
    ig = _sigmoid(pi + bi)
    log_a = (-RG_C * r) * sp
    a = jnp.exp(log_a)
    mult = jnp.sqrt(_neg_expm1(2.0 * log_a))
    return r, ig, a, mult


def _scan_fwd_call(xc, wa, wi, ba, bi, lam, reverse, dep):
    s, c = xc.shape
    heads = c // LRU_HEAD
    t = min(SEQ_TILE, s)
    n_tiles = s // t
    tile = (lambda i: (n_tiles - 1 - i, 0)) if reverse else (lambda i: (i, 0))
    whole2 = lambda i: (0, 0)
    whole3 = lambda i: (0, 0, 0)

    def body(xc_ref, wa_ref, wi_ref, ba_ref, bi_ref, lam_ref, dep_ref, h_ref, r_ref, ig_ref, a_ref, mult_ref,
             pr_s, pi_s, carry_s):
        @pl.when(pl.program_id(0) == 0)
        def _():
            carry_s[...] = jnp.zeros_like(carry_s)

        _gate_matmuls(xc_ref, wa_ref, wi_ref, pr_s, pi_s, heads)
        ba_v, bi_v = ba_ref[...], bi_ref[...]
        sp = _softplus(-lam_ref[...])

        rows = lax.broadcasted_iota(jnp.int32, (SUBLANES, c), 0)

        def chunk(r0, h_in):
            rs = pl.ds(r0, SUBLANES)
            r, ig, a, mult = _rg_gates(pr_s[rs, :], pi_s[rs, :], ba_v, bi_v, sp)
            r_ref[rs, :] = r
            ig_ref[rs, :] = ig
            a_ref[rs, :] = a
            mult_ref[rs, :] = mult
            h = _scan_chunk(a, mult * ig * xc_ref[rs, :], h_in, rows, reverse)
            h_ref[rs, :] = h
            return h[0:1, :] if reverse else h[SUBLANES - 1:SUBLANES, :]

        carry_s[...] = _chunk_loop(t, chunk, carry_s[...], unroll=2, descending=reverse)

    return pl.pallas_call(
        body, name="scan_fwd_rev" if reverse else "scan_fwd", grid=(n_tiles,),
        in_specs=[pl.BlockSpec((t, c), tile),
                  pl.BlockSpec((heads, LRU_HEAD, LRU_HEAD), whole3), pl.BlockSpec((heads, LRU_HEAD, LRU_HEAD), whole3),
                  pl.BlockSpec((1, c), whole2), pl.BlockSpec((1, c), whole2), pl.BlockSpec((1, c), whole2), ANY],
        out_specs=[pl.BlockSpec((t, c), tile)] * 5,
        out_shape=[jax.ShapeDtypeStruct((s, c), F32)] * 5,
        scratch_shapes=[pltpu.VMEM((t, c), F32), pltpu.VMEM((t, c), F32), pltpu.VMEM((1, c), F32)],
        compiler_params=_params("arbitrary"),
    )(xc, wa, wi, ba, bi, lam, dep)


def _window_counts(r0, tile_idx, t_rows, n_rows, half, shape):
    pos = tile_idx * t_rows + r0 + lax.broadcasted_iota(jnp.int32, shape, 0)
    hi = jnp.minimum(pos + half, n_rows)
    lo = jnp.maximum(pos - half, 0)
    return (hi - lo).astype(F32)


def _pool_combine_call(proj, h_f, h_b, w_pool, pool_scale, p):
    s = proj.shape[0]
    c = h_f.shape[1]
    pg = p // N_POOL_GROUPS
    t = min(SEQ_TILE, s)
    n_tiles = s // t

    def body(u_ref, up_ref, un_ref, gate_ref, hf_ref, hb_ref, wp_ref, sc_ref, y_ref, d_ref, d_s, yr_s):
        i = pl.program_id(0)
        rows = lax.broadcasted_iota(jnp.int32, (SUBLANES, pg), 0)

        def chunk(r0, _):
            rs = pl.ds(r0, SUBLANES)
            for g, w in enumerate(POOL_WINDOWS):
                cols = pl.ds(g * pg, pg)
                prv, cur, nxt = _neighbour_chunks(u_ref, up_ref, un_ref, r0, t, cols, i == 0, i == n_tiles - 1)
                tot = cur
                for o in range(-(w // 2), w // 2):
                    if o != 0:
                        tot = tot + _shift_rows(prv, cur, nxt, o, rows)
                cnt = _window_counts(r0, i, t, s, w // 2, (SUBLANES, pg))
                d_s[rs, cols] = tot / cnt - cur
            gate, _ = _gelu_and_grad(gate_ref[rs, :])
            yr_s[rs, :] = (hf_ref[rs, :] + hb_ref[rs, :]) * gate

        _chunk_loop(t, chunk)
        y_ref[:, pl.ds(p, c)] = yr_s[...].astype(BF16)
        d_ref[...] = d_s[...].astype(BF16)
        for g in range(N_POOL_GROUPS):
            cols = pl.ds(g * pg, pg)
            out = jnp.dot(d_s[:, cols].astype(BF16), wp_ref[g], preferred_element_type=F32)
            y_ref[:, cols] = (out * sc_ref[:, cols]).astype(BF16)

    return pl.pallas_call(
        body, name="pool_combine", grid=(n_tiles,),
        in_specs=_halo_specs(t, s, p, 0) + [
            pl.BlockSpec((t, c), lambda i: (i, 2)),
            pl.BlockSpec((t, c), lambda i: (i, 0)), pl.BlockSpec((t, c), lambda i: (i, 0)),
            pl.BlockSpec((N_POOL_GROUPS, pg, pg), lambda i: (0, 0, 0)), pl.BlockSpec((1, p), lambda i: (0, 0))],
        out_specs=[pl.BlockSpec((t, p + c), lambda i: (i, 0)), pl.BlockSpec((t, p), lambda i: (i, 0))],
        out_shape=[jax.ShapeDtypeStruct((s, p + c), BF16), jax.ShapeDtypeStruct((s, p), BF16)],
        scratch_shapes=[pltpu.VMEM((t, p), F32), pltpu.VMEM((t, c), F32)],
        compiler_params=_params("arbitrary"),
    )(proj, proj, proj, proj, h_f, h_b, w_pool, pool_scale)


def _layer_norm_rows(z, g, b):
    mu = jnp.mean(z, axis=-1, keepdims=True)
    zc = z - mu
    var = jnp.mean(zc * zc, axis=-1, keepdims=True)
    rstd = lax.rsqrt(var + LN_EPS)
    xh = zc * rstd
    return xh, rstd, xh * g + b


def _layer_norm_bwd_rows(dx, xh, rstd, g):
    dxh = dx * g
    m1 = jnp.mean(dxh, axis=-1, keepdims=True)
    m2 = jnp.mean(dxh * xh, axis=-1, keepdims=True)
    return rstd * (dxh - m1 - xh * m2)


def _out_ln1_call(y, w_out, x, g1, b1, dep):
    s, d = x.shape
    tm = min(SEQ_TILE, s)

    def body(y_ref, w_ref, x_ref, g_ref, b_ref, dep_ref, xh_ref, x1b_ref, rstd_ref, acc_s, x1_s):
        acc_s[...] = jnp.dot(y_ref[...], w_ref[...], preferred_element_type=F32)
        g, b = g_ref[...], b_ref[...]

        def chunk(r0, _):
            rs = pl.ds(r0, SUBLANES)
            xh, rstd, x1 = _layer_norm_rows(ALPHA * x_ref[rs, :] + acc_s[rs, :], g, b)
            xh_ref[rs, :] = xh
            x1_s[rs, :] = x1
            rstd_ref[rs, :] = rstd

        _chunk_loop(tm, chunk, unroll=LN_UNROLL)
        x1b_ref[...] = x1_s[...].astype(BF16)

    return pl.pallas_call(
        body, name="out_ln1", grid=(s // tm,),
        in_specs=[pl.BlockSpec((tm, d), lambda i: (i, 0)), pl.BlockSpec((d, d), lambda i: (0, 0)),
                  pl.BlockSpec((tm, d), lambda i: (i, 0)),
                  pl.BlockSpec((1, d), lambda i: (0, 0)), pl.BlockSpec((1, d), lambda i: (0, 0)), ANY],
        out_specs=[pl.BlockSpec((tm, d), lambda i: (i, 0)), pl.BlockSpec((tm, d), lambda i: (i, 0)),
                   pl.BlockSpec((tm, 1), lambda i: (i, 0))],
        out_shape=[jax.ShapeDtypeStruct((s, d), F32), jax.ShapeDtypeStruct((s, d), BF16), jax.ShapeDtypeStruct((s, 1), F32)],
        scratch_shapes=[pltpu.VMEM((tm, d), F32), pltpu.VMEM((tm, d), F32)],
        compiler_params=_params("arbitrary"),
    )(y, w_out, x, g1, b1, dep)


def _mlp_in_call(x1b, w1, dep, done):
    s, d = x1b.shape
    n, _, f4 = w1.shape
    tm = min(MM_TILE, s)
    tn = min(1024, f4)
    per = f4 // tn
    blocks = n * per // 2
    first = 0 if done is None else blocks
    extra = [] if done is None else list(done)

    def body(x_ref, w_ref, dep_ref, *rest):
        r_ref, q_ref = rest[-2:]
        r = jnp.maximum(jnp.dot(x_ref[...], w_ref[...], preferred_element_type=F32), 0.0)
        r_ref[...] = r.astype(BF16)
        q_ref[...] = (r * r).astype(BF16)

    out_spec = pl.BlockSpec((tm, tn), lambda j, i: (i, first + j))
    return pl.pallas_call(
        body, name="mlp_in" if done is None else "mlp_in_rest", grid=(blocks, s // tm),
        in_specs=[pl.BlockSpec((tm, d), lambda j, i: (i, 0)),
                  pl.BlockSpec((None, d, tn), lambda j, i: ((first + j) // per, 0, (first + j) % per)), ANY] + [ANY] * len(extra),
        out_specs=[out_spec, out_spec],
        out_shape=[jax.ShapeDtypeStruct((s, n * f4), BF16), jax.ShapeDtypeStruct((s, n * f4), BF16)],
        input_output_aliases={3: 0, 4: 1} if extra else {},
        compiler_params=_params("arbitrary", "arbitrary"),
    )(x1b, w1, dep, *extra)


def _mlp_out_ln2_call(hsq, w2, xh1, g1, b1, g2, b2, target):
    s, f = hsq.shape
    d = w2.shape[1]
    tm = min(MM_TILE, s)
    tk = min(LN_MM_K, f)
    nk = f // tk

    def body(h_ref, w_ref, xh1_ref, g1_ref, b1_ref, g2_ref, b2_ref, t_ref,
             dz_ref, dzb_ref, loss_ref, dg_ref, db_ref, acc_s):
        i, k = pl.program_id(0), pl.program_id(1)

        @pl.when((i == 0) & (k == 0))
        def _():
            loss_ref[...] = jnp.zeros_like(loss_ref)
            dg_ref[...] = jnp.zeros_like(dg_ref)
            db_ref[...] = jnp.zeros_like(db_ref)

        @pl.when(k == 0)
        def _():
            acc_s[...] = jnp.zeros_like(acc_s)

        acc_s[...] += jnp.dot(h_ref[...], w_ref[...], preferred_element_type=F32)

        @pl.when(k == nk - 1)
        def _():
            g1, b1, g2, b2 = g1_ref[...], b1_ref[...], g2_ref[...], b2_ref[...]

            def chunk(r0, _):
                rs = pl.ds(r0, SUBLANES)
                x1 = xh1_ref[rs, :] * g1 + b1
                xh2, rstd, x2 = _layer_norm_rows(ALPHA * x1 + acc_s[rs, :], g2, b2)
                diff = x2 - t_ref[rs, :]
                loss_ref[...] += diff * diff
                dx2 = diff * (1.0 / d)
                dg_ref[...] += dx2 * xh2
                db_ref[...] += dx2
                dz = _layer_norm_bwd_rows(dx2, xh2, rstd, g2)
                dz_ref[rs, :] = dz

            _chunk_loop(tm, chunk, unroll=LN_UNROLL)
            dzb_ref[...] = dz_ref[...].astype(BF16)

    row = lambda i, k: (i, 0)
    vec = lambda i, k: (0, 0)
    return pl.pallas_call(
        body, name="mlp_out_ln2", grid=(s // tm, nk),
        in_specs=[pl.BlockSpec((tm, tk), lambda i, k: (i, k)), pl.BlockSpec((tk, d), lambda i, k: (k, 0)),
                  pl.BlockSpec((tm, d), row), pl.BlockSpec((1, d), vec), pl.BlockSpec((1, d), vec),
                  pl.BlockSpec((1, d), vec), pl.BlockSpec((1, d), vec), pl.BlockSpec((tm, d), row)],
        out_specs=[pl.BlockSpec((tm, d), row), pl.BlockSpec((tm, d), row),
                   pl.BlockSpec((SUBLANES, d), vec), pl.BlockSpec((SUBLANES, d), vec), pl.BlockSpec((SUBLANES, d), vec)],
        out_shape=[jax.ShapeDtypeStruct((s, d), F32), jax.ShapeDtypeStruct((s, d), BF16),
                   jax.ShapeDtypeStruct((SUBLANES, d), F32), jax.ShapeDtypeStruct((SUBLANES, d), F32),
                   jax.ShapeDtypeStruct((SUBLANES, d), F32)],
        scratch_shapes=[pltpu.VMEM((tm, d), F32)],
        compiler_params=_params("arbitrary", "arbitrary"),
    )(hsq, w2, xh1, g1, b1, g2, b2, target)


def _half_grad_call(a, b, half, row_sharded, init, name, dep):
    s, m = a.shape
    n = b.shape[1]
    if row_sharded:
        rows, cols = m // (2 * N_CHIPS), n
        tm = min(1024, rows)
        per = rows // tm
        tn = min(1024, cols)
        n_i, n_j = N_CHIPS * per, cols // tn
        a_block = lambda i, h: ((i // per) * 2 + h) * per + i % per
        out_block = lambda i, j: (i // per, i % per, j)
    else:
        rows, cols = m // 2, n // N_CHIPS
        tm = min(1024, rows)
        per = rows // tm
        tn = cols if cols % 1024 else 1024
        per_n = cols // tn
        n_i, n_j = per, N_CHIPS * per_n
        a_block = lambda i, h: h * per + i
        out_block = lambda i, j: (j // per_n, i, j % per_n)
    tk = min(2048, s)
    if row_sharded and tm < 1024 and s * n * 2 <= RESIDENT_OPERAND_BYTES:
        tk, tn, n_j = s, n, 1
    has_init = init is not None

    def body(half_ref, a_ref, b_ref, *rest):
        o_ref = rest[-1]

        @pl.when(pl.program_id(2) == 0)
        def _():
            o_ref[...] = rest[0][...] if has_init else jnp.zeros_like(o_ref)

        o_ref[...] += lax.dot_general(a_ref[...], b_ref[...], (((0,), (0,)), ((), ())), preferred_element_type=F32)

    out_spec = pl.BlockSpec((None, tm, tn), lambda i, j, k, h: out_block(i, j))
    in_specs = [pl.BlockSpec((tk, tm), lambda i, j, k, h: (k, a_block(i, h[0]))),
                pl.BlockSpec((tk, tn), lambda i, j, k, h: (k, j))]
    args = [a, b]
    if has_init:
        in_specs.append(out_spec)
        args.append(init)
    in_specs.append(ANY)
    args.append(dep)
    return pl.pallas_call(
        body, name=name,
        grid_spec=pltpu.PrefetchScalarGridSpec(num_scalar_prefetch=1, grid=(n_i, n_j, s // tk), in_specs=in_specs,
                                               out_specs=out_spec),
        out_shape=jax.ShapeDtypeStruct((N_CHIPS, rows, cols), F32),
        compiler_params=_params("arbitrary", "arbitrary", "arbitrary"),
    )(half, *args)


def _dhsq_call(dzb, w2, r, dep):
    s, d = dzb.shape
    f = w2.shape[0]
    tm = min(MM_TILE, s)
    tn = min(1024, f)

    def body(dz_ref, w_ref, r_ref, dep_ref, o_ref):
        dh = lax.dot_general(dz_ref[...], w_ref[...], (((1,), (1,)), ((), ())), preferred_element_type=F32)
        o_ref[...] = (dh * (2.0 * r_ref[...].astype(F32))).astype(BF16)

    return pl.pallas_call(
        body, name="mlp_dpre", grid=(f // tn, s // tm),
        in_specs=[pl.BlockSpec((tm, d), lambda j, i: (i, 0)), pl.BlockSpec((tn, d), lambda j, i: (j, 0)),
                  pl.BlockSpec((tm, tn), lambda j, i: (i, j)), ANY],
        out_specs=pl.BlockSpec((tm, tn), lambda j, i: (i, j)),
        out_shape=jax.ShapeDtypeStruct((s, f), BF16),
        compiler_params=_params("arbitrary", "arbitrary"),
    )(dzb, w2, r, dep)


def _dx1_ln1_bwd_call(dpre, w1, dz2, xh1, rstd1, g1, dep):
    s, f = dpre.shape
    n, d, f4 = w1.shape
    tm = min(MM_TILE, s)
    tk = min(LN_MM_K, f4)
    per = f4 // tk
    nk = n * per

    def body(dp_ref, w_ref, dz2_ref, xh_ref, rstd_ref, g_ref, dep_ref, dz_ref, dzb_ref, dg_ref, db_ref, acc_s):
        i, k = pl.program_id(0), pl.program_id(1)

        @pl.when((i == 0) & (k == 0))
        def _():
            dg_ref[...] = jnp.zeros_like(dg_ref)
            db_ref[...] = jnp.zeros_like(db_ref)

        @pl.when(k == 0)
        def _():
            acc_s[...] = jnp.zeros_like(acc_s)

        acc_s[...] += lax.dot_general(dp_ref[...], w_ref[...], (((1,), (1,)), ((), ())), preferred_element_type=F32)

        @pl.when(k == nk - 1)
        def _():
            g = g_ref[...]

            def chunk(r0, _):
                rs = pl.ds(r0, SUBLANES)
                dx1 = acc_s[rs, :] + ALPHA * dz2_ref[rs, :]
                xh = xh_ref[rs, :]
                dg_ref[...] += dx1 * xh
                db_ref[...] += dx1
                dz = _layer_norm_bwd_rows(dx1, xh, rstd_ref[rs, :], g)
                dz_ref[rs, :] = dz

            _chunk_loop(tm, chunk, unroll=LN_UNROLL)
            dzb_ref[...] = dz_ref[...].astype(BF16)

    row = lambda i, k: (i, 0)
    vec = lambda i, k: (0, 0)
    return pl.pallas_call(
        body, name="dx1_ln1_bwd", grid=(s // tm, nk),
        in_specs=[pl.BlockSpec((tm, tk), lambda i, k: (i, k)),
                  pl.BlockSpec((None, d, tk), lambda i, k: (k // per, 0, k % per)),
                  pl.BlockSpec((tm, d), row), pl.BlockSpec((tm, d), row), pl.BlockSpec((tm, 1), row),
                  pl.BlockSpec((1, d), vec), ANY],
        out_specs=[pl.BlockSpec((tm, d), row), pl.BlockSpec((tm, d), row),
                   pl.BlockSpec((SUBLANES, d), vec), pl.BlockSpec((SUBLANES, d), vec)],
        out_shape=[jax.ShapeDtypeStruct((s, d), F32), jax.ShapeDtypeStruct((s, d), BF16),
                   jax.ShapeDtypeStruct((SUBLANES, d), F32), jax.ShapeDtypeStruct((SUBLANES, d), F32)],
        scratch_shapes=[pltpu.VMEM((tm, d), F32)],
        compiler_params=_params("arbitrary", "arbitrary"),
    )(dpre, w1, dz2, xh1, rstd1, g1, dep)


def _dy_call(dzb, w_out, dep):
    s, d = dzb.shape
    e = w_out.shape[0]
    tm = min(MM_TILE, s)

    def body(dz_ref, w_ref, dep_ref, o_ref):
        o_ref[...] = lax.dot_general(dz_ref[...], w_ref[...], (((1,), (1,)), ((), ())), preferred_element_type=F32)

    return pl.pallas_call(
        body, name="dy", grid=(s // tm,),
        in_specs=[pl.BlockSpec((tm, d), lambda i: (i, 0)), pl.BlockSpec((e, d), lambda i: (0, 0)), ANY],
        out_specs=pl.BlockSpec((tm, e), lambda i: (i, 0)),
        out_shape=jax.ShapeDtypeStruct((s, e), F32),
        compiler_params=_params("arbitrary"),
    )(dzb, w_out, dep)


def _mixer_bwd_call(dy, d_pool, proj, h_f, h_b, w_pool, pool_scale, p):
    s = dy.shape[0]
    c = h_f.shape[1]
    pg = p // N_POOL_GROUPS
    t = min(SEQ_TILE, s)
    n_tiles = s // t

    def body(dyp_ref, dyr_ref, d_ref, gate_ref, hf_ref, hb_ref, wp_ref, sc_ref,
             e_ref, dh_ref, dgate_ref, dwp_ref, dsc_ref, dd_s):
        i = pl.program_id(0)

        @pl.when(i == 0)
        def _():
            dwp_ref[...] = jnp.zeros_like(dwp_ref)
            dsc_ref[...] = jnp.zeros_like(dsc_ref)

        for g in range(N_POOL_GROUPS):
            cols = pl.ds(g * pg, pg)
            dg = d_ref[:, cols]
            out = jnp.dot(dg, wp_ref[g], preferred_element_type=F32)
            dyp = dyp_ref[:, cols]
            prod = dyp * out
            dsc_ref[:, cols] += jnp.sum(prod.reshape(t // SUBLANES, SUBLANES, pg), axis=0)
            dout = (dyp * sc_ref[:, cols]).astype(BF16)
            dwp_ref[g] += lax.dot_general(dg, dout, (((0,), (0,)), ((), ())), preferred_element_type=F32)
            dd_s[:, cols] = lax.dot_general(dout, wp_ref[g], (((1,), (1,)), ((), ())), preferred_element_type=F32)

        def chunk(r0, _):
            rs = pl.ds(r0, SUBLANES)
            for g, w in enumerate(POOL_WINDOWS):
                cols = pl.ds(g * pg, pg)
                cnt = _window_counts(r0, i, t, s, w // 2, (SUBLANES, pg))
                e_ref[rs, cols] = dd_s[rs, cols] / cnt
            gate, dgate = _gelu_and_grad(gate_ref[rs, :])
            dyr = dyr_ref[rs, :]
            dh_ref[rs, :] = dyr * gate
            dd_s[rs, :] = dyr * (hf_ref[rs, :] + hb_ref[rs, :]) * dgate

        _chunk_loop(t, chunk)
        dgate_ref[...] = dd_s[...].astype(BF16)

    tile = lambda i: (i, 0)
    return pl.pallas_call(
        body, name="mixer_bwd", grid=(n_tiles,),
        in_specs=[pl.BlockSpec((t, p), tile), pl.BlockSpec((t, c), lambda i: (i, 1)), pl.BlockSpec((t, p), tile),
                  pl.BlockSpec((t, c), lambda i: (i, 2)), pl.BlockSpec((t, c), tile), pl.BlockSpec((t, c), tile),
                  pl.BlockSpec((N_POOL_GROUPS, pg, pg), lambda i: (0, 0, 0)), pl.BlockSpec((1, p), lambda i: (0, 0))],
        out_specs=[pl.BlockSpec((t, p), tile), pl.BlockSpec((t, c), tile), pl.BlockSpec((t, c), tile),
                   pl.BlockSpec((N_POOL_GROUPS, pg, pg), lambda i: (0, 0, 0)), pl.BlockSpec((SUBLANES, p), lambda i: (0, 0))],
        out_shape=[jax.ShapeDtypeStruct((s, p), F32), jax.ShapeDtypeStruct((s, c), F32), jax.ShapeDtypeStruct((s, c), BF16),
                   jax.ShapeDtypeStruct((N_POOL_GROUPS, pg, pg), F32), jax.ShapeDtypeStruct((SUBLANES, p), F32)],
        scratch_shapes=[pltpu.VMEM((t, p), F32)],
        compiler_params=_params("arbitrary"),
    )(dy, dy, d_pool, proj, h_f, h_b, w_pool, pool_scale)


def _scan_bwd_call(xc, dh, h_dir, gates, dxc_prev, wa, wi, lam, reverse, dep):
    s, c = xc.shape
    heads = c // LRU_HEAD
    t = min(SEQ_TILE, s)
    n_tiles = s // t
    per = t // SUBLANES
    last_blk = s // SUBLANES - 1
    tile = (lambda i: (i, 0)) if reverse else (lambda i: (n_tiles - 1 - i, 0))
    if reverse:
        halo = lambda i: (jnp.minimum((i + 1) * per, last_blk), 0)
    else:
        halo = lambda i: (jnp.maximum((n_tiles - 1 - i) * per - 1, 0), 0)
    whole2 = lambda i: (0, 0)
    whole3 = lambda i: (0, 0, 0)
    has_prev = dxc_prev is not None
    n_in = 11 + int(has_prev) + 1

    def body(*refs):
        xc_ref, dh_ref, h_ref, hh_ref, r_ref, ig_ref, a_ref, mult_ref = refs[:8]
        prev_ref = refs[8] if has_prev else None
        wa_ref, wi_ref, lam_ref = refs[n_in - 4:n_in - 1]
        dxc_ref, dwa_ref, dwi_ref, dba_ref, dbi_ref, dsp_ref = refs[n_in:n_in + 6]
        pr_s, pi_s, carry_s = refs[n_in + 6:]
        step = pl.program_id(0)
        tile_idx = step if reverse else n_tiles - 1 - step

        @pl.when(step == 0)
        def _():
            carry_s[...] = jnp.zeros_like(carry_s)
            dwa_ref[...] = jnp.zeros_like(dwa_ref)
            dwi_ref[...] = jnp.zeros_like(dwi_ref)
            dba_ref[...] = jnp.zeros_like(dba_ref)
            dbi_ref[...] = jnp.zeros_like(dbi_ref)
            dsp_ref[...] = jnp.zeros_like(dsp_ref)

        sp = _softplus(-lam_ref[...])
        rows = lax.broadcasted_iota(jnp.int32, (SUBLANES, c), 0)

        def chunk(r0, u_in):
            rs = pl.ds(r0, SUBLANES)
            xcv = xc_ref[rs, :]
            r, ig, a, mult = r_ref[rs, :], ig_ref[rs, :], a_ref[rs, :], mult_ref[rs, :]
            dhv = dh_ref[rs, :]
            u = _scan_chunk(a, a * dhv, u_in, rows, not reverse)
            if reverse:
                gt = dhv + jnp.where(rows >= 1, pltpu.roll(u, 1, 0), u_in)
                u_out = u[SUBLANES - 1:SUBLANES, :]
            else:
                gt = dhv + jnp.where(rows < SUBLANES - 1, pltpu.roll(u, SUBLANES - 1, 0), u_in)
                u_out = u[0:1, :]
            cur = h_ref[rs, :]
            if reverse:
                after = h_ref[pl.ds(pl.multiple_of(jnp.minimum(r0 + SUBLANES, t - SUBLANES), SUBLANES), SUBLANES), :]
                edge = jnp.where(tile_idx == n_tiles - 1, 0.0, hh_ref[...])
                nxt = jnp.where(r0 == t - SUBLANES, edge, after)
                hs = _shift_rows(cur, cur, nxt, 1, rows)
            else:
                before = h_ref[pl.ds(pl.multiple_of(jnp.maximum(r0 - SUBLANES, 0), SUBLANES), SUBLANES), :]
                edge = jnp.where(tile_idx == 0, 0.0, hh_ref[...])
                prv = jnp.where(r0 == 0, edge, before)
                hs = _shift_rows(prv, cur, cur, -1, rows)
            gx = gt * xcv
            dmult = gx * ig
            di = gx * mult
            dlog_a = (gt * hs) * a - dmult * (a * a) / mult
            dr = dlog_a * (-RG_C * sp)
            dsp_ref[...] += dlog_a * (-RG_C * r)
            dpr = dr * r * (1.0 - r)
            dpi = di * ig * (1.0 - ig)
            dba_ref[...] += dpr
            dbi_ref[...] += dpi
            direct = gt * mult * ig
            if has_prev:
                direct = direct + prev_ref[rs, :]
            dxc_ref[rs, :] = direct
            pr_s[rs, :] = dpr
            pi_s[rs, :] = dpi
            return u_out

        carry_s[...] = _chunk_loop(t, chunk, carry_s[...], unroll=2, descending=not reverse)

        for h in range(heads):
            cs = pl.ds(h * LRU_HEAD, LRU_HEAD)
            xb = xc_ref[:, cs].astype(BF16)
            dprb = pr_s[:, cs].astype(BF16)
            dpib = pi_s[:, cs].astype(BF16)
            dwa_ref[h] += lax.dot_general(xb, dprb, (((0,), (0,)), ((), ())), preferred_element_type=F32)
            dwi_ref[h] += lax.dot_general(xb, dpib, (((0,), (0,)), ((), ())), preferred_element_type=F32)
            dxc_ref[:, cs] += (
                lax.dot_general(dprb, wa_ref[h], (((1,), (1,)), ((), ())), preferred_element_type=F32)
                + lax.dot_general(dpib, wi_ref[h], (((1,), (1,)), ((), ())), preferred_element_type=F32))

    tile_spec = pl.BlockSpec((t, c), tile)
    in_specs = [tile_spec, tile_spec, tile_spec, pl.BlockSpec((SUBLANES, c), halo)] + [tile_spec] * 4
    args = [xc, dh, h_dir, h_dir, *gates]
    if has_prev:
        in_specs.append(tile_spec)
        args.append(dxc_prev)
    in_specs += [pl.BlockSpec((heads, LRU_HEAD, LRU_HEAD), whole3), pl.BlockSpec((heads, LRU_HEAD, LRU_HEAD), whole3),
                 pl.BlockSpec((1, c), whole2), ANY]
    args += [wa, wi, lam, dep]
    assert len(args) == n_in
    return pl.pallas_call(
        body, name="scan_bwd_rev" if reverse else "scan_bwd", grid=(n_tiles,),
        in_specs=in_specs,
        out_specs=[tile_spec,
                   pl.BlockSpec((heads, LRU_HEAD, LRU_HEAD), whole3), pl.BlockSpec((heads, LRU_HEAD, LRU_HEAD), whole3),
                   pl.BlockSpec((SUBLANES, c), whole2), pl.BlockSpec((SUBLANES, c), whole2), pl.BlockSpec((SUBLANES, c), whole2)],
        out_shape=[jax.ShapeDtypeStruct((s, c), F32),
                   jax.ShapeDtypeStruct((heads, LRU_HEAD, LRU_HEAD), F32), jax.ShapeDtypeStruct((heads, LRU_HEAD, LRU_HEAD), F32),
                   jax.ShapeDtypeStruct((SUBLANES, c), F32), jax.ShapeDtypeStruct((SUBLANES, c), F32),
                   jax.ShapeDtypeStruct((SUBLANES, c), F32)],
        scratch_shapes=[pltpu.VMEM((t, c), F32), pltpu.VMEM((t, c), F32), pltpu.VMEM((1, c), F32)],
        compiler_params=_params("arbitrary"),
    )(*args)


def _dproj_call(e_pool, dxc, proj, dgate, conv_w, p):
    s, c = dxc.shape
    pg = p // N_POOL_GROUPS
    t = min(SEQ_TILE, s)
    n_tiles = s // t

    def body(e_ref, ep_ref, en_ref, dx_ref, dxp_ref, dxn_ref, u_ref, up_ref, un_ref, dgate_ref, w_ref,
             dproj_ref, dcw_ref, dcb_ref, st_s):
        i = pl.program_id(0)
        first, last = i == 0, i == n_tiles - 1

        @pl.when(first)
        def _():
            dcw_ref[...] = jnp.zeros_like(dcw_ref)
            dcb_ref[...] = jnp.zeros_like(dcb_ref)

        rows_p = lax.broadcasted_iota(jnp.int32, (SUBLANES, pg), 0)
        rows_c = lax.broadcasted_iota(jnp.int32, (SUBLANES, c), 0)
        w = w_ref[...]

        def chunk(r0, _):
            rs = pl.ds(r0, SUBLANES)
            for g, win in enumerate(POOL_WINDOWS):
                cols = pl.ds(g * pg, pg)
                prv, cur, nxt = _neighbour_chunks(e_ref, ep_ref, en_ref, r0, t, cols, first, last)
                tot = cur
                for o in range(-(win // 2) + 1, win // 2 + 1):
                    if o != 0:
                        tot = tot + _shift_rows(prv, cur, nxt, o, rows_p)
                cnt = _window_counts(r0, i, t, s, win // 2, (SUBLANES, pg))
                st_s[rs, cols] = tot - cur * cnt
            prv, cur, nxt = _neighbour_chunks(dx_ref, dxp_ref, dxn_ref, r0, t, slice(None), first, last)
            du = w[1:2] * cur
            du += w[0:1] * _shift_rows(prv, cur, nxt, 1, rows_c)
            du += w[2:3] * _shift_rows(prv, cur, nxt, -1, rows_c)
            du += w[3:4] * _shift_rows(prv, cur, nxt, -2, rows_c)
            st_s[rs, pl.ds(p, c)] = du
            uprv, ucur, unxt = _neighbour_chunks(u_ref, up_ref, un_ref, r0, t, slice(None), first, last)
            dcb_ref[...] += cur
            for j, o in enumerate((-1, 0, 1, 2)):
                dcw_ref[j] += cur * _shift_rows(uprv, ucur, unxt, o, rows_c)

        _chunk_loop(t, chunk)
        dproj_ref[:, pl.ds(0, p + c)] = st_s[...].astype(BF16)
        dproj_ref[:, pl.ds(p + c, c)] = dgate_ref[...]

    return pl.pallas_call(
        body, name="dproj", grid=(n_tiles,),
        in_specs=_halo_specs(t, s, p, 0) + _halo_specs(t, s, c, 0) + _halo_specs(t, s, c, 1) + [
            pl.BlockSpec((t, c), lambda i: (i, 0)), pl.BlockSpec((4, c), lambda i: (0, 0))],
        out_specs=[pl.BlockSpec((t, p + 2 * c), lambda i: (i, 0)),
                   pl.BlockSpec((4, SUBLANES, c), lambda i: (0, 0, 0)), pl.BlockSpec((SUBLANES, c), lambda i: (0, 0))],
        out_shape=[jax.ShapeDtypeStruct((s, p + 2 * c), BF16), jax.ShapeDtypeStruct((4, SUBLANES, c), F32),
                   jax.ShapeDtypeStruct((SUBLANES, c), F32)],
        scratch_shapes=[pltpu.VMEM((t, p + c), F32)],
        compiler_params=_params("arbitrary"),
    )(e_pool, e_pool, e_pool, dxc, dxc, dxc, proj, proj, proj, dgate, conv_w)


def _dx_call(dproj, w_in, dz1, dep):
    s, e = dproj.shape
    n, d, e4 = w_in.shape
    tm = min(MM_TILE, s)

    def body(dp_ref, w_ref, dz_ref, dep_ref, o_ref):
        k = pl.program_id(1)

        @pl.when(k == 0)
        def _():
            o_ref[...] = ALPHA * dz_ref[...]

        o_ref[...] += lax.dot_general(dp_ref[...], w_ref[...], (((1,), (1,)), ((), ())), preferred_element_type=F32)

    return pl.pallas_call(
        body, name="grad_x", grid=(s // tm, n),
        in_specs=[pl.BlockSpec((tm, e4), lambda i, k: (i, k)), pl.BlockSpec((None, d, e4), lambda i, k: (k, 0, 0)),
                  pl.BlockSpec((tm, d), lambda i, k: (i, 0)), ANY],
        out_specs=pl.BlockSpec((tm, d), lambda i, k: (i, 0)),
        out_shape=jax.ShapeDtypeStruct((s, d), F32),
        compiler_params=_params("arbitrary", "arbitrary"),
    )(dproj, w_in, dz1, dep)


def _row_tile(rows, cols, n_arrays):
    limit = max(SUBLANES, ELT_BLOCK_BYTES // (4 * cols * max(1, n_arrays // 4)))
    best = SUBLANES
    for cand in range(SUBLANES, min(rows, limit) + 1, SUBLANES):
        if rows % cand == 0:
            best = cand
    return best if rows % SUBLANES == 0 else rows


def _cast_to_slot_call(a, idx, dtype, name):
    rows, cols = a.shape
    tr = _row_tile(rows, cols, 2)

    def body(idx_ref, a_ref, o_ref):
        o_ref[...] = a_ref[...].astype(dtype)

    return pl.pallas_call(
        body, name=name,
        grid_spec=pltpu.PrefetchScalarGridSpec(
            num_scalar_prefetch=1, grid=(rows // tr,),
            in_specs=[pl.BlockSpec((tr, cols), lambda i, idx_ref: (i, 0))],
            out_specs=pl.BlockSpec((None, tr, cols), lambda i, idx_ref: (idx_ref[1], i, 0))),
        out_shape=jax.ShapeDtypeStruct((N_CHIPS, rows, cols), dtype),
        compiler_params=_params("arbitrary"),
    )(idx, a)


def _add_half_call(g, recv, idx, to_slot, name):
    _, rows, cols = g.shape
    tr = _row_tile(rows, cols, 3)

    def body(idx_ref, g_ref, r_ref, o_ref):
        o_ref[...] = g_ref[...] + r_ref[...]

    if to_slot:
        out_spec = pl.BlockSpec((None, tr, cols), lambda i, idx_ref: (idx_ref[1], i, 0))
        out_shape = jax.ShapeDtypeStruct((N_CHIPS, rows, cols), F32)
    else:
        out_spec = pl.BlockSpec((tr, cols), lambda i, idx_ref: (i, 0))
        out_shape = jax.ShapeDtypeStruct((rows, cols), F32)
    return pl.pallas_call(
        body, name=name,
        grid_spec=pltpu.PrefetchScalarGridSpec(
            num_scalar_prefetch=1, grid=(rows // tr,),
            in_specs=[pl.BlockSpec((None, tr, cols), lambda i, idx_ref: (idx_ref[0], i, 0)),
                      pl.BlockSpec((tr, cols), lambda i, idx_ref: (i, 0))],
            out_specs=out_spec),
        out_shape=out_shape,
        compiler_params=_params("arbitrary"),
    )(idx, g, recv)


def _sum_chips_call(own, recv, idx, name):
    _, rows, cols = recv.shape
    tr = _row_tile(rows, cols, 5)
    out_spec = pl.BlockSpec((None, tr, cols), lambda i, idx_ref: (idx_ref[0], i, 0))
    if own is None:
        def body(idx_ref, r_ref, o_ref):
            o_ref[...] = ((r_ref[0] + r_ref[1]) + r_ref[2]) + r_ref[3]
        in_specs = [pl.BlockSpec((N_CHIPS, tr, cols), lambda i, idx_ref: (0, i, 0))]
        args = (recv,)
    else:
        def body(idx_ref, p_ref, r_ref, o_ref):
            o_ref[...] = ((p_ref[...] + r_ref[0]) + r_ref[1]) + r_ref[2]
        in_specs = [pl.BlockSpec((None, tr, cols), lambda i, idx_ref: (idx_ref[1], i, 0)),
                    pl.BlockSpec((N_CHIPS - 1, tr, cols), lambda i, idx_ref: (0, i, 0))]
        args = (own, recv)
    return pl.pallas_call(
        body, name=name,
        grid_spec=pltpu.PrefetchScalarGridSpec(num_scalar_prefetch=1, grid=(rows // tr,), in_specs=in_specs, out_specs=out_spec),
        out_shape=jax.ShapeDtypeStruct((2, rows, cols), F32),
        compiler_params=_params("arbitrary"),
    )(idx, *args)


def _adamw_call(g, w, m, v, name):
    rows, cols = w.shape
    tr = _row_tile(rows, cols, 8)

    def body(g_ref, w_ref, m_ref, v_ref, go_ref, d_ref, mo_ref, vo_ref):
        gv = g_ref[...]
        go_ref[...] = gv
        mn = ADAM_B1 * m_ref[...] + (1.0 - ADAM_B1) * gv
        vn = ADAM_B2 * v_ref[...] + (1.0 - ADAM_B2) * (gv * gv)
        m_hat = mn / (1.0 - ADAM_B1 ** ADAM_STEP)
        v_hat = vn / (1.0 - ADAM_B2 ** ADAM_STEP)
        d_ref[...] = -ADAM_LR * (m_hat / (jnp.sqrt(v_hat) + ADAM_EPS) + ADAM_WD * w_ref[...])
        mo_ref[...] = mn
        vo_ref[...] = vn

    spec = pl.BlockSpec((tr, cols), lambda i: (i, 0))
    shape = jax.ShapeDtypeStruct((rows, cols), F32)
    return pl.pallas_call(
        body, name=name, grid=(rows // tr,),
        in_specs=[spec] * 4, out_specs=[spec] * 4, out_shape=[shape] * 4,
        compiler_params=_params("arbitrary"),
    )(g, w, m, v)


def _mesh_place():
    x, y, c = lax.axis_index("x"), lax.axis_index("y"), lax.axis_index("c")
    chips = [(1 - x, y), (x, 1 - y), (1 - x, 1 - y)]
    return x, y, c, chips


def _remote(src, dst, send_sems, recv_sems, idx, device):
    return pltpu.make_async_remote_copy(src_ref=src, dst_ref=dst, send_sem=send_sems.at[idx], recv_sem=recv_sems.at[idx],
                                        device_id=device, device_id_type=MESH)


HBM_SPEC = pl.BlockSpec(memory_space=pltpu.HBM)
SEM_SPEC = pl.BlockSpec(memory_space=pltpu.SEMAPHORE)
ORDERED_EFFECT = pltpu.SideEffectType.DATAFLOW_SIDE_EFFECTING


def _in_hbm(a):
    return pltpu.with_memory_space_constraint(a, pltpu.HBM)


def _start_copies_call(name, bufs, groups, after=None):
    n, g = len(bufs), len(groups)
    extra = [] if after is None else [after]
    first_out = n + len(extra)

    def body(*refs):
        outs = refs[first_out:first_out + n]
        sems = refs[first_out + n:first_out + n + 2 * g]
        token = refs[first_out + n + 2 * g]
        for i, (which, copies_fn, _) in enumerate(groups):
            for mine, _ in copies_fn([outs[w] for w in which], sems[2 * i], sems[2 * i + 1]):
                mine.start()
        token[...] = jnp.zeros_like(token)

    sem_shapes = [pltpu.SemaphoreType.DMA((cnt,)) for _, _, cnt in groups for _ in range(2)]
    res = pl.pallas_call(
        body, name=name,
        in_specs=[HBM_SPEC] * n + [ANY] * len(extra),
        out_specs=[HBM_SPEC] * n + [SEM_SPEC] * (2 * g) + [pl.BlockSpec(memory_space=pltpu.VMEM)],
        out_shape=[pltpu.HBM(a.shape, a.dtype) for a in bufs] + sem_shapes + [jax.ShapeDtypeStruct((SUBLANES, LANES), F32)],
        input_output_aliases={a: a for a in range(n)},
        compiler_params=pltpu.CompilerParams(has_side_effects=ORDERED_EFFECT),
    )(*[_in_hbm(a) for a in bufs], *extra)
    sems = res[n:n + 2 * g]
    return list(res[:n]), [(sems[2 * i], sems[2 * i + 1]) for i in range(g)], res[n + 2 * g]


def _wait_copies_call(name, bufs, sems, copies_fn, after):
    n = len(bufs)

    def body(*refs):
        ins = refs[:n]
        send_sems, recv_sems = refs[n], refs[n + 1]
        for mine, arriving in copies_fn(list(ins), send_sems, recv_sems):
            arriving.wait_recv()
            mine.wait_send()

    res = pl.pallas_call(
        body, name=name,
        in_specs=[HBM_SPEC] * n + [SEM_SPEC, SEM_SPEC, ANY],
        out_specs=[HBM_SPEC] * n,
        out_shape=[pltpu.HBM(a.shape, a.dtype) for a in bufs],
        input_output_aliases={a: a for a in range(n)},
        compiler_params=pltpu.CompilerParams(has_side_effects=ORDERED_EFFECT),
    )(*bufs, sems[0], sems[1], after)
    return list(res)


def _gather_copies(bufs, send_sems, recv_sems):
    x, y, c, chips = _mesh_place()
    k = 2 * x + y
    out = []
    for a, buf in enumerate(bufs):
        for j, (px, py) in enumerate(chips):
            kj = 2 * px + py
            mine = _remote(buf.at[k, c], buf.at[k, c], send_sems, recv_sems, 3 * a + j, (px, py, c))
            arriving = _remote(buf.at[k, c], buf.at[kj, c], send_sems, recv_sems, 3 * a + j, (px, py, c))
            out.append((mine, arriving))
    return out


def _exchange_copies(n_sharded, n_replicated):
    def copies(bufs, send_sems, recv_sems):
        x, y, c, chips = _mesh_place()
        k = 2 * x + y
        sums, lands = bufs[:n_sharded], bufs[n_sharded:2 * n_sharded]
        repl = bufs[2 * n_sharded:]
        out = []
        for j, (px, py) in enumerate(chips):
            kj = 2 * px + py
            for a in range(n_sharded):
                cp = _remote(sums[a].at[kj], lands[a].at[j], send_sems, recv_sems, 3 * a + j, (px, py, c))
                out.append((cp, cp))
            for a in range(n_replicated):
                idx = 3 * (n_sharded + a) + j
                mine = _remote(repl[a].at[k], repl[a].at[k], send_sems, recv_sems, idx, (px, py, c))
                arriving = _remote(repl[a].at[k], repl[a].at[kj], send_sems, recv_sems, idx, (px, py, c))
                out.append((mine, arriving))
        return out
    return copies


def _sibling_copies(n, halves):
    def copies(bufs, send_sems, recv_sems):
        x, y, c, _ = _mesh_place()
        out = []
        for a in range(n):
            src = bufs[a].at[1 - c] if halves else bufs[a]
            cp = _remote(src, bufs[n + a], send_sems, recv_sems, a, (x, y, 1 - c))
            out.append((cp, cp))
        return out
    return copies


def _forward_copies(bufs, send_sems, recv_sems):
    x, y, c, chips = _mesh_place()
    out = []
    for a, buf in enumerate(bufs):
        for j, (px, py) in enumerate(chips):
            kj = 2 * px + py
            mine = _remote(buf.at[kj, c], buf.at[kj, c], send_sems, recv_sems, 3 * a + j, (x, y, 1 - c))
            arriving = _remote(buf.at[kj, c], buf.at[kj, 1 - c], send_sems, recv_sems, 3 * a + j, (x, y, 1 - c))
            out.append((mine, arriving))
    return out


def _forward_to_sibling_call(bufs, name):
    n = len(bufs)

    def body(*refs):
        ins, outs = refs[:n], refs[n:2 * n]
        send_sems, recv_sems = refs[2 * n:]
        x, y, c, chips = _mesh_place()
        sibling = (x, y, 1 - c)
        sends = []
        for a in range(n):
            for j, (px, py) in enumerate(chips):
                kj = 2 * px + py
                sends.append(_remote(ins[a].at[kj, c], outs[a].at[kj, c], send_sems, recv_sems, 3 * a + j, sibling))
        for cp in sends:
            cp.start()
        for a in range(n):
            for j, (px, py) in enumerate(chips):
                kj = 2 * px + py
                _remote(ins[a].at[kj, c], outs[a].at[kj, 1 - c], send_sems, recv_sems, 3 * a + j, sibling).wait_recv()
        for cp in sends:
            cp.wait_send()

    return pl.pallas_call(
        body, name=name,
        in_specs=[ANY] * n, out_specs=[ANY] * n,
        out_shape=[jax.ShapeDtypeStruct(a.shape, a.dtype) for a in bufs],
        input_output_aliases={a: a for a in range(n)},
        scratch_shapes=[pltpu.SemaphoreType.DMA((3 * n,)), pltpu.SemaphoreType.DMA((3 * n,))],
    )(*bufs)


def _join_halves_call(bufs, name):
    n = len(bufs)

    def body(*refs):
        ins, outs = refs[:n], refs[n:2 * n]
        send_sems, recv_sems = refs[2 * n:]
        x, y, c, _ = _mesh_place()
        sibling = (x, y, 1 - c)
        copies = [_remote(ins[a].at[c], outs[a].at[c], send_sems, recv_sems, a, sibling) for a in range(n)]
        for cp in copies:
            cp.start()
        for a in range(n):
            _remote(ins[a].at[c], outs[a].at[1 - c], send_sems, recv_sems, a, sibling).wait_recv()
        for cp in copies:
            cp.wait_send()

    return pl.pallas_call(
        body, name=name,
        in_specs=[ANY] * n, out_specs=[ANY] * n,
        out_shape=[jax.ShapeDtypeStruct(a.shape, a.dtype) for a in bufs],
        input_output_aliases={a: a for a in range(n)},
        scratch_shapes=[pltpu.SemaphoreType.DMA((n,)), pltpu.SemaphoreType.DMA((n,))],
    )(*bufs)


def _pack(arrays, rows_multiple):
    flat = jnp.concatenate([a.reshape(-1) for a in arrays])
    per = LANES * rows_multiple
    padded = -(-flat.shape[0] // per) * per
    flat = jnp.pad(flat, (0, padded - flat.shape[0]))
    return flat.reshape(-1, LANES)


def _unpack(packed, shapes):
    flat = packed.reshape(-1)
    out, at = [], 0
    for shp in shapes:
        size = 1
        for dim in shp:
            size *= dim
        out.append(flat[at:at + size].reshape(shp))
        at += size
    return out


def _halves(a):
    return a.reshape((2, a.shape[0] // 2) + a.shape[1:])


def kernel(x, ln_mix_g, ln_mix_b, w_in, w_pool, pool_scale, conv_w, conv_b, w_rg_a, b_rg_a, w_rg_i, b_rg_i, rg_lambda, w_out, ln_ffn_g, ln_ffn_b, w_mlp_in, w_mlp_out, loss_target, m_ln_mix_g, m_ln_mix_b, m_w_in, m_w_pool, m_pool_scale, m_conv_w, m_conv_b, m_w_rg_a, m_b_rg_a, m_w_rg_i, m_b_rg_i, m_rg_lambda, m_w_out, m_ln_ffn_g, m_ln_ffn_b, m_w_mlp_in, m_w_mlp_out, v_ln_mix_g, v_ln_mix_b, v_w_in, v_w_pool, v_pool_scale, v_conv_w, v_conv_b, v_w_rg_a, v_b_rg_a, v_w_rg_i, v_b_rg_i, v_rg_lambda, v_w_out, v_ln_ffn_g, v_ln_ffn_b, v_w_mlp_in, v_w_mlp_out):
    weights = dict(ln_mix_g=ln_mix_g, ln_mix_b=ln_mix_b, w_in=w_in, w_pool=w_pool, pool_scale=pool_scale, conv_w=conv_w,
                   conv_b=conv_b, w_rg_a=w_rg_a, b_rg_a=b_rg_a, w_rg_i=w_rg_i, b_rg_i=b_rg_i, rg_lambda=rg_lambda,
                   w_out=w_out, ln_ffn_g=ln_ffn_g, ln_ffn_b=ln_ffn_b, w_mlp_in=w_mlp_in, w_mlp_out=w_mlp_out)
    m_in = dict(ln_mix_g=m_ln_mix_g, ln_mix_b=m_ln_mix_b, w_in=m_w_in, w_pool=m_w_pool, pool_scale=m_pool_scale,
                conv_w=m_conv_w, conv_b=m_conv_b, w_rg_a=m_w_rg_a, b_rg_a=m_b_rg_a, w_rg_i=m_w_rg_i, b_rg_i=m_b_rg_i,
                rg_lambda=m_rg_lambda, w_out=m_w_out, ln_ffn_g=m_ln_ffn_g, ln_ffn_b=m_ln_ffn_b, w_mlp_in=m_w_mlp_in,
                w_mlp_out=m_w_mlp_out)
    v_in = dict(ln_mix_g=v_ln_mix_g, ln_mix_b=v_ln_mix_b, w_in=v_w_in, w_pool=v_w_pool, pool_scale=v_pool_scale,
                conv_w=v_conv_w, conv_b=v_conv_b, w_rg_a=v_w_rg_a, b_rg_a=v_b_rg_a, w_rg_i=v_w_rg_i, b_rg_i=v_b_rg_i,
                rg_lambda=v_rg_lambda, w_out=v_w_out, ln_ffn_g=v_ln_ffn_g, ln_ffn_b=v_ln_ffn_b, w_mlp_in=v_w_mlp_in,
                w_mlp_out=v_w_mlp_out)
    names = list(weights)

    xs = x[0]
    tgt = loss_target[0]
    s, d = xs.shape
    p = c = d // 2
    pg = p // N_POOL_GROUPS
    heads = c // LRU_HEAD
    core = lax.axis_index("c")
    shard = 2 * lax.axis_index("x") + lax.axis_index("y")

    idx = jnp.stack([core, shard]).astype(jnp.int32)
    small_shard = _pack([conv_w[0], b_rg_a[0], b_rg_i[0], rg_lambda[0]], 2 * SUBLANES)
    to_gather = [(w_in[0], BF16), (w_out[0], BF16), (w_mlp_in[0], BF16), (w_mlp_out[0], BF16),
                 (w_pool[0].reshape(-1, pg), BF16), (small_shard, F32)]

    def slot_view(i):
        a, dt = to_gather[i]
        sl = _cast_to_slot_call(a, idx, dt, f"gather_slot_{i}")
        return sl.reshape(N_CHIPS, 2, sl.shape[1] // 2, sl.shape[2])

    first, later = (0, 4, 5), (1, 2, 3)
    fly_a, sems_a, token_a = _start_copies_call(
        "gather_start_first", [slot_view(i) for i in first], [((0, 1, 2), _gather_copies, 3 * len(first))])
    fly_b, sems_b, g_token = _start_copies_call(
        "gather_start_later", [slot_view(i) for i in later],
        [((0,), _gather_copies, 3), ((1,), _gather_copies, 3), ((2,), _gather_copies, 3)], token_a)
    in_flight = {**dict(zip(first, fly_a)), **dict(zip(later, fly_b))}
    g_sems = [sems_a[0]] + list(sems_b)

    def arrive(which, group, after, tag):
        return _wait_copies_call(f"gather_wait_{tag}", [in_flight[w] for w in which], g_sems[group], _gather_copies, after)

    def pass_on(got, tag):
        flying, sems, token = _start_copies_call(
            f"gather_forward_start_{tag}", got, [(tuple(range(len(got))), _forward_copies, 3 * len(got))])
        return (flying, sems[0], tag), token

    def passed_on(state, after):
        flying, sems, tag = state
        return _wait_copies_call(f"gather_forward_wait_{tag}", flying, sems, _forward_copies, after)

    gathered = [None] * len(to_gather)
    xb = _cast_call(xs, g_token)
    gathered[0], gathered[4], gathered[5] = _forward_to_sibling_call(arrive(first, 0, xb, "w_in"), "gather_forward_w_in")
    w_in_f = gathered[0].reshape((N_CHIPS,) + w_in.shape[1:])
    w_pool_f = gathered[4].reshape(N_CHIPS, N_POOL_GROUPS, pg // N_CHIPS, pg).transpose(1, 0, 2, 3).reshape(N_POOL_GROUPS, pg, pg)
    c4 = c // N_CHIPS
    small_parts = [_unpack(gathered[5][k].reshape(-1, LANES), [(4, c4), (2, c4), (2, c4), (2, c4)]) for k in range(N_CHIPS)]
    conv_w_f = jnp.concatenate([sp_[0] for sp_ in small_parts], axis=1)
    b_a_f = jnp.concatenate([sp_[1] for sp_ in small_parts], axis=1)
    b_i_f = jnp.concatenate([sp_[2] for sp_ in small_parts], axis=1)
    lam_f = jnp.concatenate([sp_[3] for sp_ in small_parts], axis=1)
    wa_b = w_rg_a[0].astype(BF16)
    wi_b = w_rg_i[0].astype(BF16)

    proj = _proj_call(xb, w_in_f)
    xc = _conv_call(proj, conv_w_f, conv_b, c)
    fwd_w_out, token = pass_on(arrive((1,), 1, xc, "w_out"), "w_out")
    h_b, *gates_b = _scan_fwd_call(xc, wa_b[1], wi_b[1], b_a_f[1:2], b_i_f[1:2], lam_f[1:2], True, token)
    h_f, *gates_f = _scan_fwd_call(xc, wa_b[0], wi_b[0], b_a_f[0:1], b_i_f[0:1], lam_f[0:1], False, token)
    y, d_pool = _pool_combine_call(proj, h_f, h_b, w_pool_f, pool_scale, p)
    w_out_f = passed_on(fwd_w_out, y)[0].reshape(d, d)
    fwd_w1, token = pass_on(arrive((2,), 2, y, "w_mlp_in"), "w_mlp_in")
    xh1, x1b, rstd1 = _out_ln1_call(y, w_out_f, xs, ln_mix_g, ln_mix_b, token)
    w1_f = passed_on(fwd_w1, x1b)[0].reshape((N_CHIPS,) + w_mlp_in.shape[1:])
    first_half = _mlp_in_call(x1b, w1_f, g_token, None)
    fwd_w2, token = pass_on(arrive((3,), 3, first_half[0], "w_mlp_out"), "w_mlp_out")
    r_act, hsq = _mlp_in_call(x1b, w1_f, token, first_half)
    w2_f = passed_on(fwd_w2, hsq)[0].reshape(N_CHIPS * w_mlp_out.shape[1], d)
    dz2, dz2b, loss8, dg2, db2 = _mlp_out_ln2_call(hsq, w2_f, xh1, ln_mix_g, ln_mix_b, ln_ffn_g, ln_ffn_b, tgt)

    def start_siblings(grads, halves, tag, after=None):
        lands = [lax.empty(g.shape[1:] if halves else g.shape, g.dtype) for g in grads]
        copies = _sibling_copies(len(grads), halves)
        flying, sems, token = _start_copies_call(
            f"siblings_start_{tag}", list(grads) + lands, [(tuple(range(2 * len(grads))), copies, len(grads))], after)
        return (flying, sems[0], copies, len(grads), tag), token

    def finish_siblings(state, after):
        flying, sems, copies, n, tag = state
        got = _wait_copies_call(f"siblings_wait_{tag}", flying, sems, copies, after)
        return got[:n], got[n:]

    half_own = jnp.reshape(core, (1,)).astype(jnp.int32)
    half_sibling = 1 - half_own

    def chip_sum_of(a, b, row_sharded, tag, dep, overlapped):
        for_sibling = _half_grad_call(a, b, half_sibling, row_sharded, None, f"grad_{tag}_for_sibling", dep)
        state, token = start_siblings([for_sibling], False, tag)
        results = overlapped(token)
        _, (from_sibling,) = finish_siblings(state, results[0])
        return _half_grad_call(a, b, half_own, row_sharded, from_sibling, f"grad_{tag}", token), results

    def start_exchange(sums, n_repl, tag):
        n_sh = len(sums) - n_repl
        lands = [lax.empty((N_CHIPS - 1,) + a.shape[1:], a.dtype) for a in sums[:n_sh]]
        bufs = sums[:n_sh] + lands + sums[n_sh:]
        copies = _exchange_copies(n_sh, n_repl)
        flying, sems, token = _start_copies_call(
            f"reduce_start_{tag}", bufs, [(tuple(range(len(bufs))), copies, 3 * len(sums))])
        return (flying, sems[0], copies, n_sh, tag), token

    def finish_exchange(state, after):
        flying, sems, copies, n_sh, tag = state
        got = _wait_copies_call(f"reduce_wait_{tag}", flying, sems, copies, after)
        halves = []
        for a in range(n_sh):
            own, land = got[a], got[n_sh + a]
            cols = own.shape[-1]
            total = _sum_chips_call(own.reshape(N_CHIPS, -1, cols), land.reshape(N_CHIPS - 1, -1, cols), idx,
                                    f"reduce_sum_{tag}_{a}")
            halves.append(total.reshape((2,) + own.shape[1:]))
        for a, rp in enumerate(got[2 * n_sh:]):
            halves.append(_sum_chips_call(None, rp, idx, f"reduce_sum_{tag}_r{a}"))
        return _join_halves_call(halves, f"reduce_join_{tag}")

    sum_w2, (dpre,) = chip_sum_of(hsq, dz2b, True, "w_mlp_out", g_token,
                                  lambda tok: (_dhsq_call(dz2b, w2_f, r_act, tok),))
    flying_w2, token = start_exchange([sum_w2], 0, "w2")
    sum_w1, (dz1, dz1b, dg1, db1) = chip_sum_of(
        x1b, dpre, False, "w_mlp_in", token,
        lambda tok: _dx1_ln1_bwd_call(dpre, w1_f, dz2, xh1, rstd1, ln_mix_g, tok))
    flying_w1, token = start_exchange([sum_w1], 0, "w1")

    def dy_and_mixer(tok):
        dy = _dy_call(dz1b, w_out_f, tok)
        return _mixer_bwd_call(dy, d_pool, proj, h_f, h_b, w_pool_f, pool_scale, p)

    sum_wout, (e_pool, dh, dgate, g_wpool, g_pscale8) = chip_sum_of(y, dz1b, True, "w_out", token, dy_and_mixer)
    flying_wout, token = start_exchange([sum_wout], 0, "w_out")
    dxc0, g_wa0, g_wi0, g_ba0, g_bi0, g_sp0 = _scan_bwd_call(
        xc, dh, h_f, gates_f, None, wa_b[0], wi_b[0], lam_f[0:1], False, token)
    dxc, g_wa1, g_wi1, g_ba1, g_bi1, g_sp1 = _scan_bwd_call(
        xc, dh, h_b, gates_b, dxc0, wa_b[1], wi_b[1], lam_f[1:2], True, token)
    dproj, g_cw8, g_cb8 = _dproj_call(e_pool, dxc, proj, dgate, conv_w_f, p)

    rowsum = lambda a8: jnp.sum(a8, axis=-2)
    g_lam = jnp.stack([rowsum(g_sp0), rowsum(g_sp1)]) * (-_sigmoid(-lam_f))
    small_grads = {
        "ln_mix_g": rowsum(dg1), "ln_mix_b": rowsum(db1), "ln_ffn_g": rowsum(dg2), "ln_ffn_b": rowsum(db2),
        "pool_scale": rowsum(g_pscale8), "conv_b": rowsum(g_cb8),
        "w_rg_a": jnp.stack([g_wa0, g_wa1]), "w_rg_i": jnp.stack([g_wi0, g_wi1]),
        "w_pool": g_wpool, "conv_w": rowsum(g_cw8),
        "b_rg_a": jnp.stack([rowsum(g_ba0), rowsum(g_ba1)]), "b_rg_i": jnp.stack([rowsum(g_bi0), rowsum(g_bi1)]),
        "rg_lambda": g_lam,
    }
    small_names = list(small_grads)
    small_shapes = [small_grads[nm].shape for nm in small_names]
    g_small = _halves(_pack([small_grads[nm] for nm in small_names], 2 * SUBLANES))
    sib_small, token = start_siblings([g_small], True, "small")
    flying_small = []

    def small_exchange_and_grad_x(tok):
        (mine,), (theirs,) = finish_siblings(sib_small, tok)
        small_sum = _add_half_call(mine, theirs, idx, True, "reduce_add_small")
        state, tok = start_exchange([small_sum], 1, "small")
        flying_small.append(state)
        return (_dx_call(dproj, w_in_f, dz1, tok),)

    sum_win, (grad_x,) = chip_sum_of(xb, dproj, False, "w_in", token, small_exchange_and_grad_x)
    flying_small = flying_small[0]
    flying_win, token = start_exchange([sum_win], 0, "w_in")

    grad_w, delta_w, new_m, new_v = {}, {}, {}, {}

    def adamw(nm, full):
        w2d = weights[nm][0]
        g2d = full.reshape(w2d.shape)
        go, dl, mn, vn = _adamw_call(g2d, w2d, m_in[nm][0], v_in[nm][0], f"adamw_{nm}")
        grad_w[nm], delta_w[nm], new_m[nm], new_v[nm] = go[None], dl[None], mn[None], vn[None]
        return vn

    last = adamw("w_mlp_out", finish_exchange(flying_w2, token)[0])
    last = adamw("w_mlp_in", finish_exchange(flying_w1, last)[0])
    last = adamw("w_out", finish_exchange(flying_wout, last)[0])

    small_full = dict(zip(small_names, _unpack(finish_exchange(flying_small, last)[0].reshape(-1, LANES), small_shapes)))
    local = dict(small_full)
    local["w_pool"] = lax.dynamic_slice_in_dim(small_full["w_pool"], shard * (pg // N_CHIPS), pg // N_CHIPS, axis=1)
    for nm in ("conv_w", "b_rg_a", "b_rg_i", "rg_lambda"):
        local[nm] = lax.dynamic_slice_in_dim(small_full[nm], shard * c4, c4, axis=1)
    small_w_shapes = [weights[nm].shape for nm in small_names]
    g_pack = _pack([local[nm] for nm in small_names], SUBLANES)
    w_pack = _pack([weights[nm] for nm in small_names], SUBLANES)
    m_pack = _pack([m_in[nm] for nm in small_names], SUBLANES)
    v_pack = _pack([v_in[nm] for nm in small_names], SUBLANES)
    _, dl_p, mn_p, vn_p = _adamw_call(g_pack, w_pack, m_pack, v_pack, "adamw_small")
    for nm, gl, dl, mn, vn in zip(small_names, _unpack(g_pack, small_w_shapes), _unpack(dl_p, small_w_shapes),
                                  _unpack(mn_p, small_w_shapes), _unpack(vn_p, small_w_shapes)):
        grad_w[nm], delta_w[nm], new_m[nm], new_v[nm] = gl, dl, mn, vn
    adamw("w_in", finish_exchange(flying_win, vn_p)[0])

    loss = lax.psum(jnp.sum(loss8) * (0.5 / d), ("x", "y", "c"))
    return (loss, grad_x[None], *[grad_w[nm] for nm in names], *[delta_w[nm] for nm in names],
            *[new_m[nm] for nm in names], *[new_v[nm] for nm in names])
```

```python
import functools

import jax
import jax.numpy as jnp
from jax import lax
from jax.experimental import pallas as pl
from jax.experimental.pallas import tpu as pltpu

F32 = jnp.float32
BF16 = jnp.bfloat16

N_CHIPS = 4
LANES = 128
SUBLANES = 8
LRU_HEAD = 128
N_POOL_GROUPS = 4
POOL_WINDOWS = (2, 4, 8, 16)
RG_C = 8.0
LN_EPS = 1e-5
ALPHA = 2.0 ** 0.25
ADAM_LR, ADAM_B1, ADAM_B2, ADAM_EPS, ADAM_WD, ADAM_STEP = 0.001, 0.9, 0.999, 1e-08, 0.01, 10
VMEM_LIMIT = 56 * 1024 * 1024
SEQ_TILE = 256
MM_TILE = 512
LN_MM_K = 2048
LN_UNROLL = 4
ELT_BLOCK_BYTES = 2 * 1024 * 1024
RESIDENT_OPERAND_BYTES = 16 * 1024 * 1024
MESH = pl.DeviceIdType.MESH
ANY = pl.BlockSpec(memory_space=pl.ANY)


def _params(*sem):
    return pltpu.CompilerParams(dimension_semantics=sem, vmem_limit_bytes=VMEM_LIMIT)


def _sigmoid(z):
    return 1.0 / (1.0 + jnp.exp(-z))


def _neg_expm1(z):
    series = -(z * (1.0 + z * (0.5 + z * (1.0 / 6.0 + z * (1.0 / 24.0)))))
    return jnp.where(z > -0.01, series, 1.0 - jnp.exp(z))


def _softplus(z):
    return jnp.maximum(z, 0.0) + jnp.log1p(jnp.exp(-jnp.abs(z)))


_GELU_K = 0.7978845608028654
_GELU_C = 0.044715


def _gelu_and_grad(u):
    t = jnp.tanh(_GELU_K * (u + _GELU_C * (u * u * u)))
    g = 0.5 * u * (1.0 + t)
    dg = 0.5 * (1.0 + t) + 0.5 * u * (1.0 - t * t) * (_GELU_K * (1.0 + 3.0 * _GELU_C * u * u))
    return g, dg


def _shift_rows(prv, cur, nxt, o, rows):
    if o == 0:
        return cur
    if o == SUBLANES:
        return nxt
    if o == -SUBLANES:
        return prv
    if o > 0:
        return pltpu.roll(jnp.where(rows >= o, cur, nxt), SUBLANES - o, 0)
    p = -o
    return pltpu.roll(jnp.where(rows < SUBLANES - p, cur, prv), p, 0)


def _neighbour_chunks(main_ref, prev_ref, next_ref, r0, t_rows, cols, first_tile, last_tile):
    cur = main_ref[pl.ds(r0, SUBLANES), cols]
    before = main_ref[pl.ds(pl.multiple_of(jnp.maximum(r0 - SUBLANES, 0), SUBLANES), SUBLANES), cols]
    after = main_ref[pl.ds(pl.multiple_of(jnp.minimum(r0 + SUBLANES, t_rows - SUBLANES), SUBLANES), SUBLANES), cols]
    halo_prev = jnp.where(first_tile, 0.0, prev_ref[:, cols])
    halo_next = jnp.where(last_tile, 0.0, next_ref[:, cols])
    prv = jnp.where(r0 == 0, halo_prev, before)
    nxt = jnp.where(r0 == t_rows - SUBLANES, halo_next, after)
    return prv, cur, nxt


def _halo_specs(t_rows, n_rows, width, col_block):
    per = t_rows // SUBLANES
    last = n_rows // SUBLANES - 1
    return [
        pl.BlockSpec((t_rows, width), lambda i: (i, col_block)),
        pl.BlockSpec((SUBLANES, width), lambda i: (jnp.maximum(i * per - 1, 0), col_block)),
        pl.BlockSpec((SUBLANES, width), lambda i: (jnp.minimum((i + 1) * per, last), col_block)),
    ]


def _chunk_loop(t_rows, fn, init=None, unroll=1, descending=False):
    span = SUBLANES * unroll

    def step(ci, carry):
        base = pl.multiple_of(((t_rows // span - 1 - ci) if descending else ci) * span, span)
        for u in range(unroll):
            carry = fn(base + ((unroll - 1 - u) if descending else u) * SUBLANES, carry)
        return carry
    return lax.fori_loop(0, t_rows // span, step, init)


def _scan_chunk(a, b, h_in, rows, reverse):
    for dist in (1, 2, 4):
        if reverse:
            keep = rows < SUBLANES - dist
            shift = SUBLANES - dist
        else:
            keep = rows >= dist
            shift = dist
        b = a * jnp.where(keep, pltpu.roll(b, shift, 0), 0.0) + b
        a = a * jnp.where(keep, pltpu.roll(a, shift, 0), 1.0)
    return a * h_in + b


def _cast_call(x, dep):
    s, d = x.shape
    tm = min(MM_TILE, s)

    def body(x_ref, dep_ref, o_ref):
        o_ref[...] = x_ref[...].astype(BF16)

    return pl.pallas_call(
        body, name="cast_x", grid=(s // tm,),
        in_specs=[pl.BlockSpec((tm, d), lambda i: (i, 0)), ANY],
        out_specs=pl.BlockSpec((tm, d), lambda i: (i, 0)),
        out_shape=jax.ShapeDtypeStruct((s, d), BF16),
        compiler_params=_params("arbitrary"),
    )(x, dep)


def _proj_call(xb, w_in):
    s, d = xb.shape
    n, _, e4 = w_in.shape
    tm = min(2 * MM_TILE, s)

    def body(x_ref, w_ref, proj_ref):
        proj_ref[...] = jnp.dot(x_ref[...], w_ref[...], preferred_element_type=F32)

    return pl.pallas_call(
        body, name="proj", grid=(s // tm, n),
        in_specs=[pl.BlockSpec((tm, d), lambda i, j: (i, 0)), pl.BlockSpec((None, d, e4), lambda i, j: (j, 0, 0))],
        out_specs=pl.BlockSpec((tm, e4), lambda i, j: (i, j)),
        out_shape=jax.ShapeDtypeStruct((s, n * e4), F32),
        compiler_params=_params("arbitrary", "arbitrary"),
    )(xb, w_in)


def _conv_call(proj, conv_w, conv_b, c):
    s = proj.shape[0]
    t = min(SEQ_TILE, s)
    n_tiles = s // t

    def body(u_ref, up_ref, un_ref, w_ref, b_ref, xc_ref):
        i = pl.program_id(0)
        rows = lax.broadcasted_iota(jnp.int32, (SUBLANES, c), 0)
        w = w_ref[...]
        b = b_ref[...]

        def chunk(r0, _):
            prv, cur, nxt = _neighbour_chunks(u_ref, up_ref, un_ref, r0, t, slice(None), i == 0, i == n_tiles - 1)
            acc = b + w[1:2] * cur
            acc += w[0:1] * _shift_rows(prv, cur, nxt, -1, rows)
            acc += w[2:3] * _shift_rows(prv, cur, nxt, 1, rows)
            acc += w[3:4] * _shift_rows(prv, cur, nxt, 2, rows)
            xc_ref[pl.ds(r0, SUBLANES), :] = acc

        _chunk_loop(t, chunk)

    return pl.pallas_call(
        body, name="conv_fwd", grid=(n_tiles,),
        in_specs=_halo_specs(t, s, c, 1) + [pl.BlockSpec((4, c), lambda i: (0, 0)), pl.BlockSpec((1, c), lambda i: (0, 0))],
        out_specs=pl.BlockSpec((t, c), lambda i: (i, 0)),
        out_shape=jax.ShapeDtypeStruct((s, c), F32),
        compiler_params=_params("arbitrary"),
    )(proj, proj, proj, conv_w, conv_b)


def _gate_matmuls(xc_ref, wa_ref, wi_ref, pr_s, pi_s, heads):
    for h in range(heads):
        cs = pl.ds(h * LRU_HEAD, LRU_HEAD)
        xb = xc_ref[:, cs].astype(BF16)
        pr_s[:, cs] = jnp.dot(xb, wa_ref[h], preferred_element_type=F32)
        pi_s[:, cs] = jnp.dot(xb, wi_ref[h], preferred_element_type=F32)


def _rg_gates(pr, pi, ba, bi, sp):
    r = _sigmoid(pr + ba)
    ig = _sigmoid(pi + bi)
    log_a = (-RG_C * r) * sp
    a = jnp.exp(log_a)
    mult = jnp.sqrt(_neg_expm1(2.0 * log_a))
    return r, ig, a, mult


def _scan_fwd_call(xc, wa, wi, ba, bi, lam, reverse, dep):
    s, c = xc.shape
    heads = c // LRU_HEAD
    t = min(SEQ_TILE, s)
    n_tiles = s // t
    tile = (lambda i: (n_tiles - 1 - i, 0)) if reverse else (lambda i: (i, 0))
    whole2 = lambda i: (0, 0)
    whole3 = lambda i: (0, 0, 0)

    def body(xc_ref, wa_ref, wi_ref, ba_ref, bi_ref, lam_ref, dep_ref, h_ref, r_ref, ig_ref, a_ref, mult_ref,
             pr_s, pi_s, carry_s):
        @pl.when(pl.program_id(0) == 0)
        def _():
            carry_s[...] = jnp.zeros_like(carry_s)

        _gate_matmuls(xc_ref, wa_ref, wi_ref, pr_s, pi_s, heads)
        ba_v, bi_v = ba_ref[...], bi_ref[...]
        sp = _softplus(-lam_ref[...])

        rows = lax.broadcasted_iota(jnp.int32, (SUBLANES, c), 0)

        def chunk(r0, h_in):
            rs = pl.ds(r0, SUBLANES)
            r, ig, a, mult = _rg_gates(pr_s[rs, :], pi_s[rs, :], ba_v, bi_v, sp)
            r_ref[rs, :] = r
            ig_ref[rs, :] = ig
            a_ref[rs, :] = a
            mult_ref[rs, :] = mult
            h = _scan_chunk(a, mult * ig * xc_ref[rs, :], h_in, rows, reverse)
            h_ref[rs, :] = h
            return h[0:1, :] if reverse else h[SUBLANES - 1:SUBLANES, :]

        carry_s[...] = _chunk_loop(t, chunk, carry_s[...], unroll=2, descending=reverse)

    return pl.pallas_call(
        body, name="scan_fwd_rev" if reverse else "scan_fwd", grid=(n_tiles,),
        in_specs=[pl.BlockSpec((t, c), tile),
                  pl.BlockSpec((heads, LRU_HEAD, LRU_HEAD), whole3), pl.BlockSpec((heads, LRU_HEAD, LRU_HEAD), whole3),
                  pl.BlockSpec((1, c), whole2), pl.BlockSpec((1, c), whole2), pl.BlockSpec((1, c), whole2), ANY],
        out_specs=[pl.BlockSpec((t, c), tile)] * 5,
        out_shape=[jax.ShapeDtypeStruct((s, c), F32)] * 5,
        scratch_shapes=[pltpu.VMEM((t, c), F32), pltpu.VMEM((t, c), F32), pltpu.VMEM((1, c), F32)],
        compiler_params=_params("arbitrary"),
    )(xc, wa, wi, ba, bi, lam, dep)


def _window_counts(r0, tile_idx, t_rows, n_rows, half, shape):
    pos = tile_idx * t_rows + r0 + lax.broadcasted_iota(jnp.int32, shape, 0)
    hi = jnp.minimum(pos + half, n_rows)
    lo = jnp.maximum(pos - half, 0)
    return (hi - lo).astype(F32)


def _pool_combine_call(proj, h_f, h_b, w_pool, pool_scale, p):
    s = proj.shape[0]
    c = h_f.shape[1]
    pg = p // N_POOL_GROUPS
    t = min(SEQ_TILE, s)
    n_tiles = s // t

    def body(u_ref, up_ref, un_ref, gate_ref, hf_ref, hb_ref, wp_ref, sc_ref, y_ref, d_ref, d_s, yr_s):
        i = pl.program_id(0)
        rows = lax.broadcasted_iota(jnp.int32, (SUBLANES, pg), 0)

        def chunk(r0, _):
            rs = pl.ds(r0, SUBLANES)
            for g, w in enumerate(POOL_WINDOWS):
                cols = pl.ds(g * pg, pg)
                prv, cur, nxt = _neighbour_chunks(u_ref, up_ref, un_ref, r0, t, cols, i == 0, i == n_tiles - 1)
                tot = cur
                for o in range(-(w // 2), w // 2):
                    if o != 0:
                        tot = tot + _shift_rows(prv, cur, nxt, o, rows)
                cnt = _window_counts(r0, i, t, s, w // 2, (SUBLANES, pg))
                d_s[rs, cols] = tot / cnt - cur
            gate, _ = _gelu_and_grad(gate_ref[rs, :])
            yr_s[rs, :] = (hf_ref[rs, :] + hb_ref[rs, :]) * gate

        _chunk_loop(t, chunk)
        y_ref[:, pl.ds(p, c)] = yr_s[...].astype(BF16)
        d_ref[...] = d_s[...].astype(BF16)
        for g in range(N_POOL_GROUPS):
            cols = pl.ds(g * pg, pg)
            out = jnp.dot(d_s[:, cols].astype(BF16), wp_ref[g], preferred_element_type=F32)
            y_ref[:, cols] = (out * sc_ref[:, cols]).astype(BF16)

    return pl.pallas_call(
        body, name="pool_combine", grid=(n_tiles,),
        in_specs=_halo_specs(t, s, p, 0) + [
            pl.BlockSpec((t, c), lambda i: (i, 2)),
            pl.BlockSpec((t, c), lambda i: (i, 0)), pl.BlockSpec((t, c), lambda i: (i, 0)),
            pl.BlockSpec((N_POOL_GROUPS, pg, pg), lambda i: (0, 0, 0)), pl.BlockSpec((1, p), lambda i: (0, 0))],
        out_specs=[pl.BlockSpec((t, p + c), lambda i: (i, 0)), pl.BlockSpec((t, p), lambda i: (i, 0))],
        out_shape=[jax.ShapeDtypeStruct((s, p + c), BF16), jax.ShapeDtypeStruct((s, p), BF16)],
        scratch_shapes=[pltpu.VMEM((t, p), F32), pltpu.VMEM((t, c), F32)],
        compiler_params=_params("arbitrary"),
    )(proj, proj, proj, proj, h_f, h_b, w_pool, pool_scale)


def _layer_norm_rows(z, g, b):
    mu = jnp.mean(z, axis=-1, keepdims=True)
    zc = z - mu
    var = jnp.mean(zc * zc, axis=-1, keepdims=True)
    rstd = lax.rsqrt(var + LN_EPS)
    xh = zc * rstd
    return xh, rstd, xh * g + b


def _layer_norm_bwd_rows(dx, xh, rstd, g):
    dxh = dx * g
    m1 = jnp.mean(dxh, axis=-1, keepdims=True)
    m2 = jnp.mean(dxh * xh, axis=-1, keepdims=True)
    return rstd * (dxh - m1 - xh * m2)


def _out_ln1_call(y, w_out, x, g1, b1, dep):
    s, d = x.shape
    tm = min(SEQ_TILE, s)

    def body(y_ref, w_ref, x_ref, g_ref, b_ref, dep_ref, xh_ref, x1b_ref, rstd_ref, acc_s, x1_s):
        acc_s[...] = jnp.dot(y_ref[...], w_ref[...], preferred_element_type=F32)
        g, b = g_ref[...], b_ref[...]

        def chunk(r0, _):
            rs = pl.ds(r0, SUBLANES)
            xh, rstd, x1 = _layer_norm_rows(ALPHA * x_ref[rs, :] + acc_s[rs, :], g, b)
            xh_ref[rs, :] = xh
            x1_s[rs, :] = x1
            rstd_ref[rs, :] = rstd

        _chunk_loop(tm, chunk, unroll=LN_UNROLL)
        x1b_ref[...] = x1_s[...].astype(BF16)

    return pl.pallas_call(
        body, name="out_ln1", grid=(s // tm,),
        in_specs=[pl.BlockSpec((tm, d), lambda i: (i, 0)), pl.BlockSpec((d, d), lambda i: (0, 0)),
                  pl.BlockSpec((tm, d), lambda i: (i, 0)),
                  pl.BlockSpec((1, d), lambda i: (0, 0)), pl.BlockSpec((1, d), lambda i: (0, 0)), ANY],
        out_specs=[pl.BlockSpec((tm, d), lambda i: (i, 0)), pl.BlockSpec((tm, d), lambda i: (i, 0)),
                   pl.BlockSpec((tm, 1), lambda i: (i, 0))],
        out_shape=[jax.ShapeDtypeStruct((s, d), F32), jax.ShapeDtypeStruct((s, d), BF16), jax.ShapeDtypeStruct((s, 1), F32)],
        scratch_shapes=[pltpu.VMEM((tm, d), F32), pltpu.VMEM((tm, d), F32)],
        compiler_params=_params("arbitrary"),
    )(y, w_out, x, g1, b1, dep)


def _mlp_in_call(x1b, w1, dep, done):
    s, d = x1b.shape
    n, _, f4 = w1.shape
    tm = min(MM_TILE, s)
    tn = min(1024, f4)
    per = f4 // tn
    blocks = n * per // 2
    first = 0 if done is None else blocks
    extra = [] if done is None else list(done)

    def body(x_ref, w_ref, dep_ref, *rest):
        r_ref, q_ref = rest[-2:]
        r = jnp.maximum(jnp.dot(x_ref[...], w_ref[...], preferred_element_type=F32), 0.0)
        r_ref[...] = r.astype(BF16)
        q_ref[...] = (r * r).astype(BF16)

    out_spec = pl.BlockSpec((tm, tn), lambda j, i: (i, first + j))
    return pl.pallas_call(
        body, name="mlp_in" if done is None else "mlp_in_rest", grid=(blocks, s // tm),
        in_specs=[pl.BlockSpec((tm, d), lambda j, i: (i, 0)),
                  pl.BlockSpec((None, d, tn), lambda j, i: ((first + j) // per, 0, (first + j) % per)), ANY] + [ANY] * len(extra),
        out_specs=[out_spec, out_spec],
        out_shape=[jax.ShapeDtypeStruct((s, n * f4), BF16), jax.ShapeDtypeStruct((s, n * f4), BF16)],
        input_output_aliases={3: 0, 4: 1} if extra else {},
        compiler_params=_params("arbitrary", "arbitrary"),
    )(x1b, w1, dep, *extra)


def _mlp_out_ln2_call(hsq, w2, xh1, g1, b1, g2, b2, target):
    s, f = hsq.shape
    d = w2.shape[1]
    tm = min(MM_TILE, s)
    tk = min(LN_MM_K, f)
    nk = f // tk

    def body(h_ref, w_ref, xh1_ref, g1_ref, b1_ref, g2_ref, b2_ref, t_ref,
             dz_ref, dzb_ref, loss_ref, dg_ref, db_ref, acc_s):
        i, k = pl.program_id(0), pl.program_id(1)

        @pl.when((i == 0) & (k == 0))
        def _():
            loss_ref[...] = jnp.zeros_like(loss_ref)
            dg_ref[...] = jnp.zeros_like(dg_ref)
            db_ref[...] = jnp.zeros_like(db_ref)

        @pl.when(k == 0)
        def _():
            acc_s[...] = jnp.zeros_like(acc_s)

        acc_s[...] += jnp.dot(h_ref[...], w_ref[...], preferred_element_type=F32)

        @pl.when(k == nk - 1)
        def _():
            g1, b1, g2, b2 = g1_ref[...], b1_ref[...], g2_ref[...], b2_ref[...]

            def chunk(r0, _):
                rs = pl.ds(r0, SUBLANES)
                x1 = xh1_ref[rs, :] * g1 + b1
                xh2, rstd, x2 = _layer_norm_rows(ALPHA * x1 + acc_s[rs, :], g2, b2)
                diff = x2 - t_ref[rs, :]
                loss_ref[...] += diff * diff
                dx2 = diff * (1.0 / d)
                dg_ref[...] += dx2 * xh2
                db_ref[...] += dx2
                dz = _layer_norm_bwd_rows(dx2, xh2, rstd, g2)
                dz_ref[rs, :] = dz

            _chunk_loop(tm, chunk, unroll=LN_UNROLL)
            dzb_ref[...] = dz_ref[...].astype(BF16)

    row = lambda i, k: (i, 0)
    vec = lambda i, k: (0, 0)
    return pl.pallas_call(
        body, name="mlp_out_ln2", grid=(s // tm, nk),
        in_specs=[pl.BlockSpec((tm, tk), lambda i, k: (i, k)), pl.BlockSpec((tk, d), lambda i, k: (k, 0)),
                  pl.BlockSpec((tm, d), row), pl.BlockSpec((1, d), vec), pl.BlockSpec((1, d), vec),
                  pl.BlockSpec((1, d), vec), pl.BlockSpec((1, d), vec), pl.BlockSpec((tm, d), row)],
        out_specs=[pl.BlockSpec((tm, d), row), pl.BlockSpec((tm, d), row),
                   pl.BlockSpec((SUBLANES, d), vec), pl.BlockSpec((SUBLANES, d), vec), pl.BlockSpec((SUBLANES, d), vec)],
        out_shape=[jax.ShapeDtypeStruct((s, d), F32), jax.ShapeDtypeStruct((s, d), BF16),
                   jax.ShapeDtypeStruct((SUBLANES, d), F32), jax.ShapeDtypeStruct((SUBLANES, d), F32),
                   jax.ShapeDtypeStruct((SUBLANES, d), F32)],
        scratch_shapes=[pltpu.VMEM((tm, d), F32)],
        compiler_params=_params("arbitrary", "arbitrary"),
    )(hsq, w2, xh1, g1, b1, g2, b2, target)


def _half_grad_call(a, b, half, row_sharded, init, name, dep):
    s, m = a.shape
    n = b.shape[1]
    if row_sharded:
        rows, cols = m // (2 * N_CHIPS), n
        tm = min(1024, rows)
        per = rows // tm
        tn = min(1024, cols)
        n_i, n_j = N_CHIPS * per, cols // tn
        a_block = lambda i, h: ((i // per) * 2 + h) * per + i % per
        out_block = lambda i, j: (i // per, i % per, j)
    else:
        rows, cols = m // 2, n // N_CHIPS
        tm = min(1024, rows)
        per = rows // tm
        tn = cols if cols % 1024 else 1024
        per_n = cols // tn
        n_i, n_j = per, N_CHIPS * per_n
        a_block = lambda i, h: h * per + i
        out_block = lambda i, j: (j // per_n, i, j % per_n)
    tk = min(2048, s)
    if row_sharded and tm < 1024 and s * n * 2 <= RESIDENT_OPERAND_BYTES:
        tk, tn, n_j = s, n, 1
    has_init = init is not None

    def body(half_ref, a_ref, b_ref, *rest):
        o_ref = rest[-1]

        @pl.when(pl.program_id(2) == 0)
        def _():
            o_ref[...] = rest[0][...] if has_init else jnp.zeros_like(o_ref)

        o_ref[...] += lax.dot_general(a_ref[...], b_ref[...], (((0,), (0,)), ((), ())), preferred_element_type=F32)

    out_spec = pl.BlockSpec((None, tm, tn), lambda i, j, k, h: out_block(i, j))
    in_specs = [pl.BlockSpec((tk, tm), lambda i, j, k, h: (k, a_block(i, h[0]))),
                pl.BlockSpec((tk, tn), lambda i, j, k, h: (k, j))]
    args = [a, b]
    if has_init:
        in_specs.append(out_spec)
        args.append(init)
    in_specs.append(ANY)
    args.append(dep)
    return pl.pallas_call(
        body, name=name,
        grid_spec=pltpu.PrefetchScalarGridSpec(num_scalar_prefetch=1, grid=(n_i, n_j, s // tk), in_specs=in_specs,
                                               out_specs=out_spec),
        out_shape=jax.ShapeDtypeStruct((N_CHIPS, rows, cols), F32),
        compiler_params=_params("arbitrary", "arbitrary", "arbitrary"),
    )(half, *args)


def _dhsq_call(dzb, w2, r, dep):
    s, d = dzb.shape
    f = w2.shape[0]
    tm = min(MM_TILE, s)
    tn = min(1024, f)

    def body(dz_ref, w_ref, r_ref, dep_ref, o_ref):
        dh = lax.dot_general(dz_ref[...], w_ref[...], (((1,), (1,)), ((), ())), preferred_element_type=F32)
        o_ref[...] = (dh * (2.0 * r_ref[...].astype(F32))).astype(BF16)

    return pl.pallas_call(
        body, name="mlp_dpre", grid=(f // tn, s // tm),
        in_specs=[pl.BlockSpec((tm, d), lambda j, i: (i, 0)), pl.BlockSpec((tn, d), lambda j, i: (j, 0)),
                  pl.BlockSpec((tm, tn), lambda j, i: (i, j)), ANY],
        out_specs=pl.BlockSpec((tm, tn), lambda j, i: (i, j)),
        out_shape=jax.ShapeDtypeStruct((s, f), BF16),
        compiler_params=_params("arbitrary", "arbitrary"),
    )(dzb, w2, r, dep)


def _dx1_ln1_bwd_call(dpre, w1, dz2, xh1, rstd1, g1, dep):
    s, f = dpre.shape
    n, d, f4 = w1.shape
    tm = min(MM_TILE, s)
    tk = min(LN_MM_K, f4)
    per = f4 // tk
    nk = n * per

    def body(dp_ref, w_ref, dz2_ref, xh_ref, rstd_ref, g_ref, dep_ref, dz_ref, dzb_ref, dg_ref, db_ref, acc_s):
        i, k = pl.program_id(0), pl.program_id(1)

        @pl.when((i == 0) & (k == 0))
        def _():
            dg_ref[...] = jnp.zeros_like(dg_ref)
            db_ref[...] = jnp.zeros_like(db_ref)

        @pl.when(k == 0)
        def _():
            acc_s[...] = jnp.zeros_like(acc_s)

        acc_s[...] += lax.dot_general(dp_ref[...], w_ref[...], (((1,), (1,)), ((), ())), preferred_element_type=F32)

        @pl.when(k == nk - 1)
        def _():
            g = g_ref[...]

            def chunk(r0, _):
                rs = pl.ds(r0, SUBLANES)
                dx1 = acc_s[rs, :] + ALPHA * dz2_ref[rs, :]
                xh = xh_ref[rs, :]
                dg_ref[...] += dx1 * xh
                db_ref[...] += dx1
                dz = _layer_norm_bwd_rows(dx1, xh, rstd_ref[rs, :], g)
                dz_ref[rs, :] = dz

            _chunk_loop(tm, chunk, unroll=LN_UNROLL)
            dzb_ref[...] = dz_ref[...].astype(BF16)

    row = lambda i, k: (i, 0)
    vec = lambda i, k: (0, 0)
    return pl.pallas_call(
        body, name="dx1_ln1_bwd", grid=(s // tm, nk),
        in_specs=[pl.BlockSpec((tm, tk), lambda i, k: (i, k)),
                  pl.BlockSpec((None, d, tk), lambda i, k: (k // per, 0, k % per)),
                  pl.BlockSpec((tm, d), row), pl.BlockSpec((tm, d), row), pl.BlockSpec((tm, 1), row),
                  pl.BlockSpec((1, d), vec), ANY],
        out_specs=[pl.BlockSpec((tm, d), row), pl.BlockSpec((tm, d), row),
                   pl.BlockSpec((SUBLANES, d), vec), pl.BlockSpec((SUBLANES, d), vec)],
        out_shape=[jax.ShapeDtypeStruct((s, d), F32), jax.ShapeDtypeStruct((s, d), BF16),
                   jax.ShapeDtypeStruct((SUBLANES, d), F32), jax.ShapeDtypeStruct((SUBLANES, d), F32)],
        scratch_shapes=[pltpu.VMEM((tm, d), F32)],
        compiler_params=_params("arbitrary", "arbitrary"),
    )(dpre, w1, dz2, xh1, rstd1, g1, dep)


def _dy_call(dzb, w_out, dep):
    s, d = dzb.shape
    e = w_out.shape[0]
    tm = min(MM_TILE, s)

    def body(dz_ref, w_ref, dep_ref, o_ref):
        o_ref[...] = lax.dot_general(dz_ref[...], w_ref[...], (((1,), (1,)), ((), ())), preferred_element_type=F32)

    return pl.pallas_call(
        body, name="dy", grid=(s // tm,),
        in_specs=[pl.BlockSpec((tm, d), lambda i: (i, 0)), pl.BlockSpec((e, d), lambda i: (0, 0)), ANY],
        out_specs=pl.BlockSpec((tm, e), lambda i: (i, 0)),
        out_shape=jax.ShapeDtypeStruct((s, e), F32),
        compiler_params=_params("arbitrary"),
    )(dzb, w_out, dep)


def _mixer_bwd_call(dy, d_pool, proj, h_f, h_b, w_pool, pool_scale, p):
    s = dy.shape[0]
    c = h_f.shape[1]
    pg = p // N_POOL_GROUPS
    t = min(SEQ_TILE, s)
    n_tiles = s // t

    def body(dyp_ref, dyr_ref, d_ref, gate_ref, hf_ref, hb_ref, wp_ref, sc_ref,
             e_ref, dh_ref, dgate_ref, dwp_ref, dsc_ref, dd_s):
        i = pl.program_id(0)

        @pl.when(i == 0)
        def _():
            dwp_ref[...] = jnp.zeros_like(dwp_ref)
            dsc_ref[...] = jnp.zeros_like(dsc_ref)

        for g in range(N_POOL_GROUPS):
            cols = pl.ds(g * pg, pg)
            dg = d_ref[:, cols]
            out = jnp.dot(dg, wp_ref[g], preferred_element_type=F32)
            dyp = dyp_ref[:, cols]
            prod = dyp * out
            dsc_ref[:, cols] += jnp.sum(prod.reshape(t // SUBLANES, SUBLANES, pg), axis=0)
            dout = (dyp * sc_ref[:, cols]).astype(BF16)
            dwp_ref[g] += lax.dot_general(dg, dout, (((0,), (0,)), ((), ())), preferred_element_type=F32)
            dd_s[:, cols] = lax.dot_general(dout, wp_ref[g], (((1,), (1,)), ((), ())), preferred_element_type=F32)

        def chunk(r0, _):
            rs = pl.ds(r0, SUBLANES)
            for g, w in enumerate(POOL_WINDOWS):
                cols = pl.ds(g * pg, pg)
                cnt = _window_counts(r0, i, t, s, w // 2, (SUBLANES, pg))
                e_ref[rs, cols] = dd_s[rs, cols] / cnt
            gate, dgate = _gelu_and_grad(gate_ref[rs, :])
            dyr = dyr_ref[rs, :]
            dh_ref[rs, :] = dyr * gate
            dd_s[rs, :] = dyr * (hf_ref[rs, :] + hb_ref[rs, :]) * dgate

        _chunk_loop(t, chunk)
        dgate_ref[...] = dd_s[...].astype(BF16)

    tile = lambda i: (i, 0)
    return pl.pallas_call(
        body, name="mixer_bwd", grid=(n_tiles,),
        in_specs=[pl.BlockSpec((t, p), tile), pl.BlockSpec((t, c), lambda i: (i, 1)), pl.BlockSpec((t, p), tile),
                  pl.BlockSpec((t, c), lambda i: (i, 2)), pl.BlockSpec((t, c), tile), pl.BlockSpec((t, c), tile),
                  pl.BlockSpec((N_POOL_GROUPS, pg, pg), lambda i: (0, 0, 0)), pl.BlockSpec((1, p), lambda i: (0, 0))],
        out_specs=[pl.BlockSpec((t, p), tile), pl.BlockSpec((t, c), tile), pl.BlockSpec((t, c), tile),
                   pl.BlockSpec((N_POOL_GROUPS, pg, pg), lambda i: (0, 0, 0)), pl.BlockSpec((SUBLANES, p), lambda i: (0, 0))],
        out_shape=[jax.ShapeDtypeStruct((s, p), F32), jax.ShapeDtypeStruct((s, c), F32), jax.ShapeDtypeStruct((s, c), BF16),
                   jax.ShapeDtypeStruct((N_POOL_GROUPS, pg, pg), F32), jax.ShapeDtypeStruct((SUBLANES, p), F32)],
        scratch_shapes=[pltpu.VMEM((t, p), F32)],
        compiler_params=_params("arbitrary"),
    )(dy, dy, d_pool, proj, h_f, h_b, w_pool, pool_scale)


def _scan_bwd_call(xc, dh, h_dir, gates, dxc_prev, wa, wi, lam, reverse, dep):
    s, c = xc.shape
    heads = c // LRU_HEAD
    t = min(SEQ_TILE, s)
    n_tiles = s // t
    per = t // SUBLANES
    last_blk = s // SUBLANES - 1
    tile = (lambda i: (i, 0)) if reverse else (lambda i: (n_tiles - 1 - i, 0))
    if reverse:
        halo = lambda i: (jnp.minimum((i + 1) * per, last_blk), 0)
    else:
        halo = lambda i: (jnp.maximum((n_tiles - 1 - i) * per - 1, 0), 0)
    whole2 = lambda i: (0, 0)
    whole3 = lambda i: (0, 0, 0)
    has_prev = dxc_prev is not None
    n_in = 11 + int(has_prev) + 1

    def body(*refs):
        xc_ref, dh_ref, h_ref, hh_ref, r_ref, ig_ref, a_ref, mult_ref = refs[:8]
        prev_ref = refs[8] if has_prev else None
        wa_ref, wi_ref, lam_ref = refs[n_in - 4:n_in - 1]
        dxc_ref, dwa_ref, dwi_ref, dba_ref, dbi_ref, dsp_ref = refs[n_in:n_in + 6]
        pr_s, pi_s, carry_s = refs[n_in + 6:]
        step = pl.program_id(0)
        tile_idx = step if reverse else n_tiles - 1 - step

        @pl.when(step == 0)
        def _():
            carry_s[...] = jnp.zeros_like(carry_s)
            dwa_ref[...] = jnp.zeros_like(dwa_ref)
            dwi_ref[...] = jnp.zeros_like(dwi_ref)
            dba_ref[...] = jnp.zeros_like(dba_ref)
            dbi_ref[...] = jnp.zeros_like(dbi_ref)
            dsp_ref[...] = jnp.zeros_like(dsp_ref)

        sp = _softplus(-lam_ref[...])
        rows = lax.broadcasted_iota(jnp.int32, (SUBLANES, c), 0)

        def chunk(r0, u_in):
            rs = pl.ds(r0, SUBLANES)
            xcv = xc_ref[rs, :]
            r, ig, a, mult = r_ref[rs, :], ig_ref[rs, :], a_ref[rs, :], mult_ref[rs, :]
            dhv = dh_ref[rs, :]
            u = _scan_chunk(a, a * dhv, u_in, rows, not reverse)
            if reverse:
                gt = dhv + jnp.where(rows >= 1, pltpu.roll(u, 1, 0), u_in)
                u_out = u[SUBLANES - 1:SUBLANES, :]
            else:
                gt = dhv + jnp.where(rows < SUBLANES - 1, pltpu.roll(u, SUBLANES - 1, 0), u_in)
                u_out = u[0:1, :]
            cur = h_ref[rs, :]
            if reverse:
                after = h_ref[pl.ds(pl.multiple_of(jnp.minimum(r0 + SUBLANES, t - SUBLANES), SUBLANES), SUBLANES), :]
                edge = jnp.where(tile_idx == n_tiles - 1, 0.0, hh_ref[...])
                nxt = jnp.where(r0 == t - SUBLANES, edge, after)
                hs = _shift_rows(cur, cur, nxt, 1, rows)
            else:
                before = h_ref[pl.ds(pl.multiple_of(jnp.maximum(r0 - SUBLANES, 0), SUBLANES), SUBLANES), :]
                edge = jnp.where(tile_idx == 0, 0.0, hh_ref[...])
                prv = jnp.where(r0 == 0, edge, before)
                hs = _shift_rows(prv, cur, cur, -1, rows)
            gx = gt * xcv
            dmult = gx * ig
            di = gx * mult
            dlog_a = (gt * hs) * a - dmult * (a * a) / mult
            dr = dlog_a * (-RG_C * sp)
            dsp_ref[...] += dlog_a * (-RG_C * r)
            dpr = dr * r * (1.0 - r)
            dpi = di * ig * (1.0 - ig)
            dba_ref[...] += dpr
            dbi_ref[...] += dpi
            direct = gt * mult * ig
            if has_prev:
                direct = direct + prev_ref[rs, :]
            dxc_ref[rs, :] = direct
            pr_s[rs, :] = dpr
            pi_s[rs, :] = dpi
            return u_out

        carry_s[...] = _chunk_loop(t, chunk, carry_s[...], unroll=2, descending=not reverse)

        for h in range(heads):
            cs = pl.ds(h * LRU_HEAD, LRU_HEAD)
            xb = xc_ref[:, cs].astype(BF16)
            dprb = pr_s[:, cs].astype(BF16)
            dpib = pi_s[:, cs].astype(BF16)
            dwa_ref[h] += lax.dot_general(xb, dprb, (((0,), (0,)), ((), ())), preferred_element_type=F32)
            dwi_ref[h] += lax.dot_general(xb, dpib, (((0,), (0,)), ((), ())), preferred_element_type=F32)
            dxc_ref[:, cs] += (
                lax.dot_general(dprb, wa_ref[h], (((1,), (1,)), ((), ())), preferred_element_type=F32)
                + lax.dot_general(dpib, wi_ref[h], (((1,), (1,)), ((), ())), preferred_element_type=F32))

    tile_spec = pl.BlockSpec((t, c), tile)
    in_specs = [tile_spec, tile_spec, tile_spec, pl.BlockSpec((SUBLANES, c), halo)] + [tile_spec] * 4
    args = [xc, dh, h_dir, h_dir, *gates]
    if has_prev:
        in_specs.append(tile_spec)
        args.append(dxc_prev)
    in_specs += [pl.BlockSpec((heads, LRU_HEAD, LRU_HEAD), whole3), pl.BlockSpec((heads, LRU_HEAD, LRU_HEAD), whole3),
                 pl.BlockSpec((1, c), whole2), ANY]
    args += [wa, wi, lam, dep]
    assert len(args) == n_in
    return pl.pallas_call(
        body, name="scan_bwd_rev" if reverse else "scan_bwd", grid=(n_tiles,),
        in_specs=in_specs,
        out_specs=[tile_spec,
                   pl.BlockSpec((heads, LRU_HEAD, LRU_HEAD), whole3), pl.BlockSpec((heads, LRU_HEAD, LRU_HEAD), whole3),
                   pl.BlockSpec((SUBLANES, c), whole2), pl.BlockSpec((SUBLANES, c), whole2), pl.BlockSpec((SUBLANES, c), whole2)],
        out_shape=[jax.ShapeDtypeStruct((s, c), F32),
                   jax.ShapeDtypeStruct((heads, LRU_HEAD, LRU_HEAD), F32), jax.ShapeDtypeStruct((heads, LRU_HEAD, LRU_HEAD), F32),
                   jax.ShapeDtypeStruct((SUBLANES, c), F32), jax.ShapeDtypeStruct((SUBLANES, c), F32),
                   jax.ShapeDtypeStruct((SUBLANES, c), F32)],
        scratch_shapes=[pltpu.VMEM((t, c), F32), pltpu.VMEM((t, c), F32), pltpu.VMEM((1, c), F32)],
        compiler_params=_params("arbitrary"),
    )(*args)


def _dproj_call(e_pool, dxc, proj, dgate, conv_w, p):
    s, c = dxc.shape
    pg = p // N_POOL_GROUPS
    t = min(SEQ_TILE, s)
    n_tiles = s // t

    def body(e_ref, ep_ref, en_ref, dx_ref, dxp_ref, dxn_ref, u_ref, up_ref, un_ref, dgate_ref, w_ref,
             dproj_ref, dcw_ref, dcb_ref, st_s):
        i = pl.program_id(0)
        first, last = i == 0, i == n_tiles - 1

        @pl.when(first)
        def _():
            dcw_ref[...] = jnp.zeros_like(dcw_ref)
            dcb_ref[...] = jnp.zeros_like(dcb_ref)

        rows_p = lax.broadcasted_iota(jnp.int32, (SUBLANES, pg), 0)
        rows_c = lax.broadcasted_iota(jnp.int32, (SUBLANES, c), 0)
        w = w_ref[...]

        def chunk(r0, _):
            rs = pl.ds(r0, SUBLANES)
            for g, win in enumerate(POOL_WINDOWS):
                cols = pl.ds(g * pg, pg)
                prv, cur, nxt = _neighbour_chunks(e_ref, ep_ref, en_ref, r0, t, cols, first, last)
                tot = cur
                for o in range(-(win // 2) + 1, win // 2 + 1):
                    if o != 0:
                        tot = tot + _shift_rows(prv, cur, nxt, o, rows_p)
                cnt = _window_counts(r0, i, t, s, win // 2, (SUBLANES, pg))
                st_s[rs, cols] = tot - cur * cnt
            prv, cur, nxt = _neighbour_chunks(dx_ref, dxp_ref, dxn_ref, r0, t, slice(None), first, last)
            du = w[1:2] * cur
            du += w[0:1] * _shift_rows(prv, cur, nxt, 1, rows_c)
            du += w[2:3] * _shift_rows(prv, cur, nxt, -1, rows_c)
            du += w[3:4] * _shift_rows(prv, cur, nxt, -2, rows_c)
            st_s[rs, pl.ds(p, c)] = du
            uprv, ucur, unxt = _neighbour_chunks(u_ref, up_ref, un_ref, r0, t, slice(None), first, last)
            dcb_ref[...] += cur
            for j, o in enumerate((-1, 0, 1, 2)):
                dcw_ref[j] += cur * _shift_rows(uprv, ucur, unxt, o, rows_c)

        _chunk_loop(t, chunk)
        dproj_ref[:, pl.ds(0, p + c)] = st_s[...].astype(BF16)
        dproj_ref[:, pl.ds(p + c, c)] = dgate_ref[...]

    return pl.pallas_call(
        body, name="dproj", grid=(n_tiles,),
        in_specs=_halo_specs(t, s, p, 0) + _halo_specs(t, s, c, 0) + _halo_specs(t, s, c, 1) + [
            pl.BlockSpec((t, c), lambda i: (i, 0)), pl.BlockSpec((4, c), lambda i: (0, 0))],
        out_specs=[pl.BlockSpec((t, p + 2 * c), lambda i: (i, 0)),
                   pl.BlockSpec((4, SUBLANES, c), lambda i: (0, 0, 0)), pl.BlockSpec((SUBLANES, c), lambda i: (0, 0))],
        out_shape=[jax.ShapeDtypeStruct((s, p + 2 * c), BF16), jax.ShapeDtypeStruct((4, SUBLANES, c), F32),
                   jax.ShapeDtypeStruct((SUBLANES, c), F32)],
        scratch_shapes=[pltpu.VMEM((t, p + c), F32)],
        compiler_params=_params("arbitrary"),
    )(e_pool, e_pool, e_pool, dxc, dxc, dxc, proj, proj, proj, dgate, conv_w)


def _dx_call(dproj, w_in, dz1, dep):
    s, e = dproj.shape
    n, d, e4 = w_in.shape
    tm = min(MM_TILE, s)

    def body(dp_ref, w_ref, dz_ref, dep_ref, o_ref):
        k = pl.program_id(1)

        @pl.when(k == 0)
        def _():
            o_ref[...] = ALPHA * dz_ref[...]

        o_ref[...] += lax.dot_general(dp_ref[...], w_ref[...], (((1,), (1,)), ((), ())), preferred_element_type=F32)

    return pl.pallas_call(
        body, name="grad_x", grid=(s // tm, n),
        in_specs=[pl.BlockSpec((tm, e4), lambda i, k: (i, k)), pl.BlockSpec((None, d, e4), lambda i, k: (k, 0, 0)),
                  pl.BlockSpec((tm, d), lambda i, k: (i, 0)), ANY],
        out_specs=pl.BlockSpec((tm, d), lambda i, k: (i, 0)),
        out_shape=jax.ShapeDtypeStruct((s, d), F32),
        compiler_params=_params("arbitrary", "arbitrary"),
    )(dproj, w_in, dz1, dep)


def _row_tile(rows, cols, n_arrays):
    limit = max(SUBLANES, ELT_BLOCK_BYTES // (4 * cols * max(1, n_arrays // 4)))
    best = SUBLANES
    for cand in range(SUBLANES, min(rows, limit) + 1, SUBLANES):
        if rows % cand == 0:
            best = cand
    return best if rows % SUBLANES == 0 else rows


def _cast_to_slot_call(a, idx, dtype, name):
    rows, cols = a.shape
    tr = _row_tile(rows, cols, 2)

    def body(idx_ref, a_ref, o_ref):
        o_ref[...] = a_ref[...].astype(dtype)

    return pl.pallas_call(
        body, name=name,
        grid_spec=pltpu.PrefetchScalarGridSpec(
            num_scalar_prefetch=1, grid=(rows // tr,),
            in_specs=[pl.BlockSpec((tr, cols), lambda i, idx_ref: (i, 0))],
            out_specs=pl.BlockSpec((None, tr, cols), lambda i, idx_ref: (idx_ref[1], i, 0))),
        out_shape=jax.ShapeDtypeStruct((N_CHIPS, rows, cols), dtype),
        compiler_params=_params("arbitrary"),
    )(idx, a)


def _add_half_call(g, recv, idx, to_slot, name):
    _, rows, cols = g.shape
    tr = _row_tile(rows, cols, 3)

    def body(idx_ref, g_ref, r_ref, o_ref):
        o_ref[...] = g_ref[...] + r_ref[...]

    if to_slot:
        out_spec = pl.BlockSpec((None, tr, cols), lambda i, idx_ref: (idx_ref[1], i, 0))
        out_shape = jax.ShapeDtypeStruct((N_CHIPS, rows, cols), F32)
    else:
        out_spec = pl.BlockSpec((tr, cols), lambda i, idx_ref: (i, 0))
        out_shape = jax.ShapeDtypeStruct((rows, cols), F32)
    return pl.pallas_call(
        body, name=name,
        grid_spec=pltpu.PrefetchScalarGridSpec(
            num_scalar_prefetch=1, grid=(rows // tr,),
            in_specs=[pl.BlockSpec((None, tr, cols), lambda i, idx_ref: (idx_ref[0], i, 0)),
                      pl.BlockSpec((tr, cols), lambda i, idx_ref: (i, 0))],
            out_specs=out_spec),
        out_shape=out_shape,
        compiler_params=_params("arbitrary"),
    )(idx, g, recv)


def _sum_chips_call(own, recv, idx, name):
    _, rows, cols = recv.shape
    tr = _row_tile(rows, cols, 5)
    out_spec = pl.BlockSpec((None, tr, cols), lambda i, idx_ref: (idx_ref[0], i, 0))
    if own is None:
        def body(idx_ref, r_ref, o_ref):
            o_ref[...] = ((r_ref[0] + r_ref[1]) + r_ref[2]) + r_ref[3]
        in_specs = [pl.BlockSpec((N_CHIPS, tr, cols), lambda i, idx_ref: (0, i, 0))]
        args = (recv,)
    else:
        def body(idx_ref, p_ref, r_ref, o_ref):
            o_ref[...] = ((p_ref[...] + r_ref[0]) + r_ref[1]) + r_ref[2]
        in_specs = [pl.BlockSpec((None, tr, cols), lambda i, idx_ref: (idx_ref[1], i, 0)),
                    pl.BlockSpec((N_CHIPS - 1, tr, cols), lambda i, idx_ref: (0, i, 0))]
        args = (own, recv)
    return pl.pallas_call(
        body, name=name,
        grid_spec=pltpu.PrefetchScalarGridSpec(num_scalar_prefetch=1, grid=(rows // tr,), in_specs=in_specs, out_specs=out_spec),
        out_shape=jax.ShapeDtypeStruct((2, rows, cols), F32),
        compiler_params=_params("arbitrary"),
    )(idx, *args)


def _adamw_call(g, w, m, v, name):
    rows, cols = w.shape
    tr = _row_tile(rows, cols, 8)

    def body(g_ref, w_ref, m_ref, v_ref, go_ref, d_ref, mo_ref, vo_ref):
        gv = g_ref[...]
        go_ref[...] = gv
        mn = ADAM_B1 * m_ref[...] + (1.0 - ADAM_B1) * gv
        vn = ADAM_B2 * v_ref[...] + (1.0 - ADAM_B2) * (gv * gv)
        m_hat = mn / (1.0 - ADAM_B1 ** ADAM_STEP)
        v_hat = vn / (1.0 - ADAM_B2 ** ADAM_STEP)
        d_ref[...] = -ADAM_LR * (m_hat / (jnp.sqrt(v_hat) + ADAM_EPS) + ADAM_WD * w_ref[...])
        mo_ref[...] = mn
        vo_ref[...] = vn

    spec = pl.BlockSpec((tr, cols), lambda i: (i, 0))
    shape = jax.ShapeDtypeStruct((rows, cols), F32)
    return pl.pallas_call(
        body, name=name, grid=(rows // tr,),
        in_specs=[spec] * 4, out_specs=[spec] * 4, out_shape=[shape] * 4,
        compiler_params=_params("arbitrary"),
    )(g, w, m, v)


def _mesh_place():
    x, y, c = lax.axis_index("x"), lax.axis_index("y"), lax.axis_index("c")
    chips = [(1 - x, y), (x, 1 - y), (1 - x, 1 - y)]
    return x, y, c, chips


def _remote(src, dst, send_sems, recv_sems, idx, device):
    return pltpu.make_async_remote_copy(src_ref=src, dst_ref=dst, send_sem=send_sems.at[idx], recv_sem=recv_sems.at[idx],
                                        device_id=device, device_id_type=MESH)


HBM_SPEC = pl.BlockSpec(memory_space=pltpu.HBM)
SEM_SPEC = pl.BlockSpec(memory_space=pltpu.SEMAPHORE)
ORDERED_EFFECT = pltpu.SideEffectType.DATAFLOW_SIDE_EFFECTING


def _in_hbm(a):
    return pltpu.with_memory_space_constraint(a, pltpu.HBM)


def _start_copies_call(name, bufs, groups, after=None):
    n, g = len(bufs), len(groups)
    extra = [] if after is None else [after]
    first_out = n + len(extra)

    def body(*refs):
        outs = refs[first_out:first_out + n]
        sems = refs[first_out + n:first_out + n + 2 * g]
        token = refs[first_out + n + 2 * g]
        for i, (which, copies_fn, _) in enumerate(groups):
            for mine, _ in copies_fn([outs[w] for w in which], sems[2 * i], sems[2 * i + 1]):
                mine.start()
        token[...] = jnp.zeros_like(token)

    sem_shapes = [pltpu.SemaphoreType.DMA((cnt,)) for _, _, cnt in groups for _ in range(2)]
    res = pl.pallas_call(
        body, name=name,
        in_specs=[HBM_SPEC] * n + [ANY] * len(extra),
        out_specs=[HBM_SPEC] * n + [SEM_SPEC] * (2 * g) + [pl.BlockSpec(memory_space=pltpu.VMEM)],
        out_shape=[pltpu.HBM(a.shape, a.dtype) for a in bufs] + sem_shapes + [jax.ShapeDtypeStruct((SUBLANES, LANES), F32)],
        input_output_aliases={a: a for a in range(n)},
        compiler_params=pltpu.CompilerParams(has_side_effects=ORDERED_EFFECT),
    )(*[_in_hbm(a) for a in bufs], *extra)
    sems = res[n:n + 2 * g]
    return list(res[:n]), [(sems[2 * i], sems[2 * i + 1]) for i in range(g)], res[n + 2 * g]


def _wait_copies_call(name, bufs, sems, copies_fn, after):
    n = len(bufs)

    def body(*refs):
        ins = refs[:n]
        send_sems, recv_sems = refs[n], refs[n + 1]
        for mine, arriving in copies_fn(list(ins), send_sems, recv_sems):
            arriving.wait_recv()
            mine.wait_send()

    res = pl.pallas_call(
        body, name=name,
        in_specs=[HBM_SPEC] * n + [SEM_SPEC, SEM_SPEC, ANY],
        out_specs=[HBM_SPEC] * n,
        out_shape=[pltpu.HBM(a.shape, a.dtype) for a in bufs],
        input_output_aliases={a: a for a in range(n)},
        compiler_params=pltpu.CompilerParams(has_side_effects=ORDERED_EFFECT),
    )(*bufs, sems[0], sems[1], after)
    return list(res)


def _gather_copies(bufs, send_sems, recv_sems):
    x, y, c, chips = _mesh_place()
    k = 2 * x + y
    out = []
    for a, buf in enumerate(bufs):
        for j, (px, py) in enumerate(chips):
            kj = 2 * px + py
            mine = _remote(buf.at[k, c], buf.at[k, c], send_sems, recv_sems, 3 * a + j, (px, py, c))
            arriving = _remote(buf.at[k, c], buf.at[kj, c], send_sems, recv_sems, 3 * a + j, (px, py, c))
            out.append((mine, arriving))
    return out


def _exchange_copies(n_sharded, n_replicated):
    def copies(bufs, send_sems, recv_sems):
        x, y, c, chips = _mesh_place()
        k = 2 * x + y
        sums, lands = bufs[:n_sharded], bufs[n_sharded:2 * n_sharded]
        repl = bufs[2 * n_sharded:]
        out = []
        for j, (px, py) in enumerate(chips):
            kj = 2 * px + py
            for a in range(n_sharded):
                cp = _remote(sums[a].at[kj], lands[a].at[j], send_sems, recv_sems, 3 * a + j, (px, py, c))
                out.append((cp, cp))
            for a in range(n_replicated):
                idx = 3 * (n_sharded + a) + j
                mine = _remote(repl[a].at[k], repl[a].at[k], send_sems, recv_sems, idx, (px, py, c))
                arriving = _remote(repl[a].at[k], repl[a].at[kj], send_sems, recv_sems, idx, (px, py, c))
                out.append((mine, arriving))
        return out
    return copies


def _sibling_copies(n, halves):
    def copies(bufs, send_sems, recv_sems):
        x, y, c, _ = _mesh_place()
        out = []
        for a in range(n):
            src = bufs[a].at[1 - c] if halves else bufs[a]
            cp = _remote(src, bufs[n + a], send_sems, recv_sems, a, (x, y, 1 - c))
            out.append((cp, cp))
        return out
    return copies


def _forward_copies(bufs, send_sems, recv_sems):
    x, y, c, chips = _mesh_place()
    out = []
    for a, buf in enumerate(bufs):
        for j, (px, py) in enumerate(chips):
            kj = 2 * px + py
            mine = _remote(buf.at[kj, c], buf.at[kj, c], send_sems, recv_sems, 3 * a + j, (x, y, 1 - c))
            arriving = _remote(buf.at[kj, c], buf.at[kj, 1 - c], send_sems, recv_sems, 3 * a + j, (x, y, 1 - c))
            out.append((mine, arriving))
    return out


def _join_copies(bufs, send_sems, recv_sems):
    x, y, c, _ = _mesh_place()
    out = []
    for a, buf in enumerate(bufs):
        mine = _remote(buf.at[c], buf.at[c], send_sems, recv_sems, a, (x, y, 1 - c))
        arriving = _remote(buf.at[c], buf.at[1 - c], send_sems, recv_sems, a, (x, y, 1 - c))
        out.append((mine, arriving))
    return out


def _forward_to_sibling_call(bufs, name):
    n = len(bufs)

    def body(*refs):
        ins, outs = refs[:n], refs[n:2 * n]
        send_sems, recv_sems = refs[2 * n:]
        x, y, c, chips = _mesh_place()
        sibling = (x, y, 1 - c)
        sends = []
        for a in range(n):
            for j, (px, py) in enumerate(chips):
                kj = 2 * px + py
                sends.append(_remote(ins[a].at[kj, c], outs[a].at[kj, c], send_sems, recv_sems, 3 * a + j, sibling))
        for cp in sends:
            cp.start()
        for a in range(n):
            for j, (px, py) in enumerate(chips):
                kj = 2 * px + py
                _remote(ins[a].at[kj, c], outs[a].at[kj, 1 - c], send_sems, recv_sems, 3 * a + j, sibling).wait_recv()
        for cp in sends:
            cp.wait_send()

    return pl.pallas_call(
        body, name=name,
        in_specs=[ANY] * n, out_specs=[ANY] * n,
        out_shape=[jax.ShapeDtypeStruct(a.shape, a.dtype) for a in bufs],
        input_output_aliases={a: a for a in range(n)},
        scratch_shapes=[pltpu.SemaphoreType.DMA((3 * n,)), pltpu.SemaphoreType.DMA((3 * n,))],
    )(*bufs)


def _join_halves_call(bufs, name):
    n = len(bufs)

    def body(*refs):
        ins, outs = refs[:n], refs[n:2 * n]
        send_sems, recv_sems = refs[2 * n:]
        x, y, c, _ = _mesh_place()
        sibling = (x, y, 1 - c)
        copies = [_remote(ins[a].at[c], outs[a].at[c], send_sems, recv_sems, a, sibling) for a in range(n)]
        for cp in copies:
            cp.start()
        for a in range(n):
            _remote(ins[a].at[c], outs[a].at[1 - c], send_sems, recv_sems, a, sibling).wait_recv()
        for cp in copies:
            cp.wait_send()

    return pl.pallas_call(
        body, name=name,
        in_specs=[ANY] * n, out_specs=[ANY] * n,
        out_shape=[jax.ShapeDtypeStruct(a.shape, a.dtype) for a in bufs],
        input_output_aliases={a: a for a in range(n)},
        scratch_shapes=[pltpu.SemaphoreType.DMA((n,)), pltpu.SemaphoreType.DMA((n,))],
    )(*bufs)


def _pack(arrays, rows_multiple):
    flat = jnp.concatenate([a.reshape(-1) for a in arrays])
    per = LANES * rows_multiple
    padded = -(-flat.shape[0] // per) * per
    flat = jnp.pad(flat, (0, padded - flat.shape[0]))
    return flat.reshape(-1, LANES)


def _unpack(packed, shapes):
    flat = packed.reshape(-1)
    out, at = [], 0
    for shp in shapes:
        size = 1
        for dim in shp:
            size *= dim
        out.append(flat[at:at + size].reshape(shp))
        at += size
    return out


def _halves(a):
    return a.reshape((2, a.shape[0] // 2) + a.shape[1:])


def kernel(x, ln_mix_g, ln_mix_b, w_in, w_pool, pool_scale, conv_w, conv_b, w_rg_a, b_rg_a, w_rg_i, b_rg_i, rg_lambda, w_out, ln_ffn_g, ln_ffn_b, w_mlp_in, w_mlp_out, loss_target, m_ln_mix_g, m_ln_mix_b, m_w_in, m_w_pool, m_pool_scale, m_conv_w, m_conv_b, m_w_rg_a, m_b_rg_a, m_w_rg_i, m_b_rg_i, m_rg_lambda, m_w_out, m_ln_ffn_g, m_ln_ffn_b, m_w_mlp_in, m_w_mlp_out, v_ln_mix_g, v_ln_mix_b, v_w_in, v_w_pool, v_pool_scale, v_conv_w, v_conv_b, v_w_rg_a, v_b_rg_a, v_w_rg_i, v_b_rg_i, v_rg_lambda, v_w_out, v_ln_ffn_g, v_ln_ffn_b, v_w_mlp_in, v_w_mlp_out):
    weights = dict(ln_mix_g=ln_mix_g, ln_mix_b=ln_mix_b, w_in=w_in, w_pool=w_pool, pool_scale=pool_scale, conv_w=conv_w,
                   conv_b=conv_b, w_rg_a=w_rg_a, b_rg_a=b_rg_a, w_rg_i=w_rg_i, b_rg_i=b_rg_i, rg_lambda=rg_lambda,
                   w_out=w_out, ln_ffn_g=ln_ffn_g, ln_ffn_b=ln_ffn_b, w_mlp_in=w_mlp_in, w_mlp_out=w_mlp_out)
    m_in = dict(ln_mix_g=m_ln_mix_g, ln_mix_b=m_ln_mix_b, w_in=m_w_in, w_pool=m_w_pool, pool_scale=m_pool_scale,
                conv_w=m_conv_w, conv_b=m_conv_b, w_rg_a=m_w_rg_a, b_rg_a=m_b_rg_a, w_rg_i=m_w_rg_i, b_rg_i=m_b_rg_i,
                rg_lambda=m_rg_lambda, w_out=m_w_out, ln_ffn_g=m_ln_ffn_g, ln_ffn_b=m_ln_ffn_b, w_mlp_in=m_w_mlp_in,
                w_mlp_out=m_w_mlp_out)
    v_in = dict(ln_mix_g=v_ln_mix_g, ln_mix_b=v_ln_mix_b, w_in=v_w_in, w_pool=v_w_pool, pool_scale=v_pool_scale,
                conv_w=v_conv_w, conv_b=v_conv_b, w_rg_a=v_w_rg_a, b_rg_a=v_b_rg_a, w_rg_i=v_w_rg_i, b_rg_i=v_b_rg_i,
                rg_lambda=v_rg_lambda, w_out=v_w_out, ln_ffn_g=v_ln_ffn_g, ln_ffn_b=v_ln_ffn_b, w_mlp_in=v_w_mlp_in,
                w_mlp_out=v_w_mlp_out)
    names = list(weights)

    xs = x[0]
    tgt = loss_target[0]
    s, d = xs.shape
    p = c = d // 2
    pg = p // N_POOL_GROUPS
    heads = c // LRU_HEAD
    core = lax.axis_index("c")
    shard = 2 * lax.axis_index("x") + lax.axis_index("y")

    idx = jnp.stack([core, shard]).astype(jnp.int32)
    small_shard = _pack([conv_w[0], b_rg_a[0], b_rg_i[0], rg_lambda[0]], 2 * SUBLANES)
    to_gather = [(w_in[0], BF16), (w_out[0], BF16), (w_mlp_in[0], BF16), (w_mlp_out[0], BF16),
                 (w_pool[0].reshape(-1, pg), BF16), (small_shard, F32)]

    def slot_view(i):
        a, dt = to_gather[i]
        sl = _cast_to_slot_call(a, idx, dt, f"gather_slot_{i}")
        return sl.reshape(N_CHIPS, 2, sl.shape[1] // 2, sl.shape[2])

    first, later = (0, 4, 5), (1, 2, 3)
    fly_a, sems_a, token_a = _start_copies_call(
        "gather_start_first", [slot_view(i) for i in first], [((0, 1, 2), _gather_copies, 3 * len(first))])
    fly_b, sems_b, g_token = _start_copies_call(
        "gather_start_later", [slot_view(i) for i in later],
        [((0,), _gather_copies, 3), ((1,), _gather_copies, 3), ((2,), _gather_copies, 3)], token_a)
    in_flight = {**dict(zip(first, fly_a)), **dict(zip(later, fly_b))}
    g_sems = [sems_a[0]] + list(sems_b)

    def arrive(which, group, after, tag):
        return _wait_copies_call(f"gather_wait_{tag}", [in_flight[w] for w in which], g_sems[group], _gather_copies, after)

    def pass_on(got, tag):
        flying, sems, token = _start_copies_call(
            f"gather_forward_start_{tag}", got, [(tuple(range(len(got))), _forward_copies, 3 * len(got))])
        return (flying, sems[0], tag), token

    def passed_on(state, after):
        flying, sems, tag = state
        return _wait_copies_call(f"gather_forward_wait_{tag}", flying, sems, _forward_copies, after)

    gathered = [None] * len(to_gather)
    xb = _cast_call(xs, g_token)
    gathered[0], gathered[4], gathered[5] = _forward_to_sibling_call(arrive(first, 0, xb, "w_in"), "gather_forward_w_in")
    w_in_f = gathered[0].reshape((N_CHIPS,) + w_in.shape[1:])
    w_pool_f = gathered[4].reshape(N_CHIPS, N_POOL_GROUPS, pg // N_CHIPS, pg).transpose(1, 0, 2, 3).reshape(N_POOL_GROUPS, pg, pg)
    c4 = c // N_CHIPS
    small_parts = [_unpack(gathered[5][k].reshape(-1, LANES), [(4, c4), (2, c4), (2, c4), (2, c4)]) for k in range(N_CHIPS)]
    conv_w_f = jnp.concatenate([sp_[0] for sp_ in small_parts], axis=1)
    b_a_f = jnp.concatenate([sp_[1] for sp_ in small_parts], axis=1)
    b_i_f = jnp.concatenate([sp_[2] for sp_ in small_parts], axis=1)
    lam_f = jnp.concatenate([sp_[3] for sp_ in small_parts], axis=1)
    wa_b = w_rg_a[0].astype(BF16)
    wi_b = w_rg_i[0].astype(BF16)

    proj = _proj_call(xb, w_in_f)
    xc = _conv_call(proj, conv_w_f, conv_b, c)
    fwd_w_out, token = pass_on(arrive((1,), 1, xc, "w_out"), "w_out")
    h_b, *gates_b = _scan_fwd_call(xc, wa_b[1], wi_b[1], b_a_f[1:2], b_i_f[1:2], lam_f[1:2], True, token)
    h_f, *gates_f = _scan_fwd_call(xc, wa_b[0], wi_b[0], b_a_f[0:1], b_i_f[0:1], lam_f[0:1], False, token)
    y, d_pool = _pool_combine_call(proj, h_f, h_b, w_pool_f, pool_scale, p)
    w_out_f = passed_on(fwd_w_out, y)[0].reshape(d, d)
    fwd_w1, token = pass_on(arrive((2,), 2, y, "w_mlp_in"), "w_mlp_in")
    xh1, x1b, rstd1 = _out_ln1_call(y, w_out_f, xs, ln_mix_g, ln_mix_b, token)
    w1_f = passed_on(fwd_w1, x1b)[0].reshape((N_CHIPS,) + w_mlp_in.shape[1:])
    first_half = _mlp_in_call(x1b, w1_f, g_token, None)
    fwd_w2, token = pass_on(arrive((3,), 3, first_half[0], "w_mlp_out"), "w_mlp_out")
    r_act, hsq = _mlp_in_call(x1b, w1_f, token, first_half)
    w2_f = passed_on(fwd_w2, hsq)[0].reshape(N_CHIPS * w_mlp_out.shape[1], d)
    dz2, dz2b, loss8, dg2, db2 = _mlp_out_ln2_call(hsq, w2_f, xh1, ln_mix_g, ln_mix_b, ln_ffn_g, ln_ffn_b, tgt)

    def start_siblings(grads, halves, tag, after=None):
        lands = [lax.empty(g.shape[1:] if halves else g.shape, g.dtype) for g in grads]
        copies = _sibling_copies(len(grads), halves)
        flying, sems, token = _start_copies_call(
            f"siblings_start_{tag}", list(grads) + lands, [(tuple(range(2 * len(grads))), copies, len(grads))], after)
        return (flying, sems[0], copies, len(grads), tag), token

    def finish_siblings(state, after):
        flying, sems, copies, n, tag = state
        got = _wait_copies_call(f"siblings_wait_{tag}", flying, sems, copies, after)
        return got[:n], got[n:]

    half_own = jnp.reshape(core, (1,)).astype(jnp.int32)
    half_sibling = 1 - half_own

    def chip_sum_of(a, b, row_sharded, tag, dep, overlapped):
        for_sibling = _half_grad_call(a, b, half_sibling, row_sharded, None, f"grad_{tag}_for_sibling", dep)
        state, token = start_siblings([for_sibling], False, tag)
        results = overlapped(token)
        _, (from_sibling,) = finish_siblings(state, results[0])
        return _half_grad_call(a, b, half_own, row_sharded, from_sibling, f"grad_{tag}", token), results

    def start_exchange(sums, n_repl, tag):
        n_sh = len(sums) - n_repl
        lands = [lax.empty((N_CHIPS - 1,) + a.shape[1:], a.dtype) for a in sums[:n_sh]]
        bufs = sums[:n_sh] + lands + sums[n_sh:]
        copies = _exchange_copies(n_sh, n_repl)
        flying, sems, token = _start_copies_call(
            f"reduce_start_{tag}", bufs, [(tuple(range(len(bufs))), copies, 3 * len(sums))])
        return (flying, sems[0], copies, n_sh, tag), token

    def finish_exchange(state, after):
        flying, sems, copies, n_sh, tag = state
        got = _wait_copies_call(f"reduce_wait_{tag}", flying, sems, copies, after)
        halves = []
        for a in range(n_sh):
            own, land = got[a], got[n_sh + a]
            cols = own.shape[-1]
            total = _sum_chips_call(own.reshape(N_CHIPS, -1, cols), land.reshape(N_CHIPS - 1, -1, cols), idx,
                                    f"reduce_sum_{tag}_{a}")
            halves.append(total.reshape((2,) + own.shape[1:]))
        for a, rp in enumerate(got[2 * n_sh:]):
            halves.append(_sum_chips_call(None, rp, idx, f"reduce_sum_{tag}_r{a}"))
        return halves

    def start_join(halves, tag):
        flying, sems, token = _start_copies_call(
            f"join_start_{tag}", halves, [(tuple(range(len(halves))), _join_copies, len(halves))])
        return (flying, sems[0], tag), token

    def finish_join(state, after):
        flying, sems, tag = state
        return _wait_copies_call(f"join_wait_{tag}", flying, sems, _join_copies, after)

    sum_w2, (dpre,) = chip_sum_of(hsq, dz2b, True, "w_mlp_out", g_token,
                                  lambda tok: (_dhsq_call(dz2b, w2_f, r_act, tok),))
    flying_w2, token = start_exchange([sum_w2], 0, "w2")
    sum_w1, (dz1, dz1b, dg1, db1) = chip_sum_of(
        x1b, dpre, False, "w_mlp_in", token,
        lambda tok: _dx1_ln1_bwd_call(dpre, w1_f, dz2, xh1, rstd1, ln_mix_g, tok))
    flying_w1, token = start_exchange([sum_w1], 0, "w1")

    def dy_and_mixer(tok):
        dy = _dy_call(dz1b, w_out_f, tok)
        return _mixer_bwd_call(dy, d_pool, proj, h_f, h_b, w_pool_f, pool_scale, p)

    sum_wout, (e_pool, dh, dgate, g_wpool, g_pscale8) = chip_sum_of(y, dz1b, True, "w_out", token, dy_and_mixer)
    flying_wout, token = start_exchange([sum_wout], 0, "w_out")
    dxc0, g_wa0, g_wi0, g_ba0, g_bi0, g_sp0 = _scan_bwd_call(
        xc, dh, h_f, gates_f, None, wa_b[0], wi_b[0], lam_f[0:1], False, token)
    dxc, g_wa1, g_wi1, g_ba1, g_bi1, g_sp1 = _scan_bwd_call(
        xc, dh, h_b, gates_b, dxc0, wa_b[1], wi_b[1], lam_f[1:2], True, token)
    dproj, g_cw8, g_cb8 = _dproj_call(e_pool, dxc, proj, dgate, conv_w_f, p)

    rowsum = lambda a8: jnp.sum(a8, axis=-2)
    g_lam = jnp.stack([rowsum(g_sp0), rowsum(g_sp1)]) * (-_sigmoid(-lam_f))
    small_grads = {
        "ln_mix_g": rowsum(dg1), "ln_mix_b": rowsum(db1), "ln_ffn_g": rowsum(dg2), "ln_ffn_b": rowsum(db2),
        "pool_scale": rowsum(g_pscale8), "conv_b": rowsum(g_cb8),
        "w_rg_a": jnp.stack([g_wa0, g_wa1]), "w_rg_i": jnp.stack([g_wi0, g_wi1]),
        "w_pool": g_wpool, "conv_w": rowsum(g_cw8),
        "b_rg_a": jnp.stack([rowsum(g_ba0), rowsum(g_ba1)]), "b_rg_i": jnp.stack([rowsum(g_bi0), rowsum(g_bi1)]),
        "rg_lambda": g_lam,
    }
    small_names = list(small_grads)
    small_shapes = [small_grads[nm].shape for nm in small_names]
    g_small = _halves(_pack([small_grads[nm] for nm in small_names], 2 * SUBLANES))
    sib_small, token = start_siblings([g_small], True, "small")
    flying_small = []

    def small_exchange_and_grad_x(tok):
        (mine,), (theirs,) = finish_siblings(sib_small, tok)
        small_sum = _add_half_call(mine, theirs, idx, True, "reduce_add_small")
        state, tok = start_exchange([small_sum], 1, "small")
        flying_small.append(state)
        return (_dx_call(dproj, w_in_f, dz1, tok),)

    sum_win, (grad_x,) = chip_sum_of(xb, dproj, False, "w_in", token, small_exchange_and_grad_x)
    flying_small = flying_small[0]
    flying_win, token = start_exchange([sum_win], 0, "w_in")

    grad_w, delta_w, new_m, new_v = {}, {}, {}, {}

    def adamw(nm, full):
        w2d = weights[nm][0]
        g2d = full.reshape(w2d.shape)
        go, dl, mn, vn = _adamw_call(g2d, w2d, m_in[nm][0], v_in[nm][0], f"adamw_{nm}")
        grad_w[nm], delta_w[nm], new_m[nm], new_v[nm] = go[None], dl[None], mn[None], vn[None]
        return vn

    join_w2, token = start_join(finish_exchange(flying_w2, token), "w2")
    join_w1, token = start_join(finish_exchange(flying_w1, token), "w1")
    join_wout, token = start_join(finish_exchange(flying_wout, token), "w_out")
    last = adamw("w_mlp_out", finish_join(join_w2, token)[0])
    last = adamw("w_mlp_in", finish_join(join_w1, last)[0])
    last = adamw("w_out", finish_join(join_wout, last)[0])

    small_joined = _join_halves_call(finish_exchange(flying_small, last), "reduce_join_small")[0]
    small_full = dict(zip(small_names, _unpack(small_joined.reshape(-1, LANES), small_shapes)))
    local = dict(small_full)
    local["w_pool"] = lax.dynamic_slice_in_dim(small_full["w_pool"], shard * (pg // N_CHIPS), pg // N_CHIPS, axis=1)
    for nm in ("conv_w", "b_rg_a", "b_rg_i", "rg_lambda"):
        local[nm] = lax.dynamic_slice_in_dim(small_full[nm], shard * c4, c4, axis=1)
    small_w_shapes = [weights[nm].shape for nm in small_names]
    g_pack = _pack([local[nm] for nm in small_names], SUBLANES)
    w_pack = _pack([weights[nm] for nm in small_names], SUBLANES)
    m_pack = _pack([m_in[nm] for nm in small_names], SUBLANES)
    v_pack = _pack([v_in[nm] for nm in small_names], SUBLANES)
    _, dl_p, mn_p, vn_p = _adamw_call(g_pack, w_pack, m_pack, v_pack, "adamw_small")
    for nm, gl, dl, mn, vn in zip(small_names, _unpack(g_pack, small_w_shapes), _unpack(dl_p, small_w_shapes),
                                  _unpack(mn_p, small_w_shapes), _unpack(vn_p, small_w_shapes)):
        grad_w[nm], delta_w[nm], new_m[nm], new_v[nm] = gl, dl, mn, vn
    adamw("w_in", _join_halves_call(finish_exchange(flying_win, vn_p), "reduce_join_w_in")[0])

    loss = lax.psum(jnp.sum(loss8) * (0.5 / d), ("x", "y", "c"))
    return (loss, grad_x[None], *[grad_w[nm] for nm in names], *[delta_w[nm] for nm in names],
            *[new_m[nm] for nm in names], *[new_v[nm] for nm in names])
```

```python
import jax
import jax.numpy as jnp
from jax import lax
from jax.experimental import pallas as pl
from jax.experimental.pallas import tpu as pltpu

F32 = jnp.float32
BF16 = jnp.bfloat16

N_CHIPS = 4
LANES = 128
SUBLANES = 8
LRU_HEAD = 128
N_POOL_GROUPS = 4
POOL_WINDOWS = (2, 4, 8, 16)
RG_C = 8.0
LN_EPS = 1e-5
ALPHA = 2.0 ** 0.25
ADAM_LR, ADAM_B1, ADAM_B2, ADAM_EPS, ADAM_WD, ADAM_STEP = 0.001, 0.9, 0.999, 1e-08, 0.01, 10
VMEM_LIMIT = 56 * 1024 * 1024
SEQ_TILE = 256
MM_TILE = 512
LN_MM_K = 2048
LN_UNROLL = 8
ELT_BLOCK_BYTES = 2 * 1024 * 1024
RESIDENT_OPERAND_BYTES = 16 * 1024 * 1024
MESH = pl.DeviceIdType.MESH
ANY = pl.BlockSpec(memory_space=pl.ANY)


def _params(*sem):
    return pltpu.CompilerParams(dimension_semantics=sem, vmem_limit_bytes=VMEM_LIMIT)


def _sigmoid(z):
    return 1.0 / (1.0 + jnp.exp(-z))


def _neg_expm1(z):
    series = -(z * (1.0 + z * (0.5 + z * (1.0 / 6.0 + z * (1.0 / 24.0)))))
    return jnp.where(z > -0.01, series, 1.0 - jnp.exp(z))


def _softplus(z):
    return jnp.maximum(z, 0.0) + jnp.log1p(jnp.exp(-jnp.abs(z)))


_GELU_K = 0.7978845608028654
_GELU_C = 0.044715


def _gelu_and_grad(u):
    t = jnp.tanh(_GELU_K * (u + _GELU_C * (u * u * u)))
    g = 0.5 * u * (1.0 + t)
    dg = 0.5 * (1.0 + t) + 0.5 * u * (1.0 - t * t) * (_GELU_K * (1.0 + 3.0 * _GELU_C * u * u))
    return g, dg


def _shift_rows(prv, cur, nxt, o, rows):
    if o == 0:
        return cur
    if o == SUBLANES:
        return nxt
    if o == -SUBLANES:
        return prv
    if o > 0:
        return pltpu.roll(jnp.where(rows >= o, cur, nxt), SUBLANES - o, 0)
    p = -o
    return pltpu.roll(jnp.where(rows < SUBLANES - p, cur, prv), p, 0)


def _neighbour_chunks(main_ref, prev_ref, next_ref, r0, t_rows, cols, first_tile, last_tile):
    cur = main_ref[pl.ds(r0, SUBLANES), cols]
    before = main_ref[pl.ds(pl.multiple_of(jnp.maximum(r0 - SUBLANES, 0), SUBLANES), SUBLANES), cols]
    after = main_ref[pl.ds(pl.multiple_of(jnp.minimum(r0 + SUBLANES, t_rows - SUBLANES), SUBLANES), SUBLANES), cols]
    halo_prev = jnp.where(first_tile, 0.0, prev_ref[:, cols])
    halo_next = jnp.where(last_tile, 0.0, next_ref[:, cols])
    prv = jnp.where(r0 == 0, halo_prev, before)
    nxt = jnp.where(r0 == t_rows - SUBLANES, halo_next, after)
    return prv, cur, nxt


def _halo_specs(t_rows, n_rows, width, col_block):
    per = t_rows // SUBLANES
    last = n_rows // SUBLANES - 1
    return [
        pl.BlockSpec((t_rows, width), lambda i: (i, col_block)),
        pl.BlockSpec((SUBLANES, width), lambda i: (jnp.maximum(i * per - 1, 0), col_block)),
        pl.BlockSpec((SUBLANES, width), lambda i: (jnp.minimum((i + 1) * per, last), col_block)),
    ]


def _chunk_loop(t_rows, fn, init=None, unroll=1, descending=False):
    span = SUBLANES * unroll

    def step(ci, carry):
        base = pl.multiple_of(((t_rows // span - 1 - ci) if descending else ci) * span, span)
        for u in range(unroll):
            carry = fn(base + ((unroll - 1 - u) if descending else u) * SUBLANES, carry)
        return carry
    return lax.fori_loop(0, t_rows // span, step, init)


def _scan_chunk(a, b, h_in, rows, reverse):
    for dist in (1, 2, 4):
        if reverse:
            keep = rows < SUBLANES - dist
            shift = SUBLANES - dist
        else:
            keep = rows >= dist
            shift = dist
        b = a * jnp.where(keep, pltpu.roll(b, shift, 0), 0.0) + b
        a = a * jnp.where(keep, pltpu.roll(a, shift, 0), 1.0)
    return a * h_in + b


def _cast_call(x, dep):
    s, d = x.shape
    tm = min(MM_TILE, s)

    def body(x_ref, dep_ref, o_ref):
        o_ref[...] = x_ref[...].astype(BF16)

    return pl.pallas_call(
        body, name="cast_x", grid=(s // tm,),
        in_specs=[pl.BlockSpec((tm, d), lambda i: (i, 0)), ANY],
        out_specs=pl.BlockSpec((tm, d), lambda i: (i, 0)),
        out_shape=jax.ShapeDtypeStruct((s, d), BF16),
        compiler_params=_params("arbitrary"),
    )(x, dep)


def _proj_call(xb, w_in):
    s, d = xb.shape
    n, _, e4 = w_in.shape
    tm = min(2 * MM_TILE, s)

    def body(x_ref, w_ref, proj_ref):
        proj_ref[...] = jnp.dot(x_ref[...], w_ref[...], preferred_element_type=F32)

    return pl.pallas_call(
        body, name="proj", grid=(s // tm, n),
        in_specs=[pl.BlockSpec((tm, d), lambda i, j: (i, 0)), pl.BlockSpec((None, d, e4), lambda i, j: (j, 0, 0))],
        out_specs=pl.BlockSpec((tm, e4), lambda i, j: (i, j)),
        out_shape=jax.ShapeDtypeStruct((s, n * e4), F32),
        compiler_params=_params("arbitrary", "arbitrary"),
    )(xb, w_in)


def _conv_call(proj, conv_w, conv_b, c):
    s = proj.shape[0]
    t = min(SEQ_TILE, s)
    n_tiles = s // t

    def body(u_ref, up_ref, un_ref, w_ref, b_ref, xc_ref):
        i = pl.program_id(0)
        rows = lax.broadcasted_iota(jnp.int32, (SUBLANES, c), 0)
        w = w_ref[...]
        b = b_ref[...]

        def chunk(r0, _):
            prv, cur, nxt = _neighbour_chunks(u_ref, up_ref, un_ref, r0, t, slice(None), i == 0, i == n_tiles - 1)
            acc = b + w[1:2] * cur
            acc += w[0:1] * _shift_rows(prv, cur, nxt, -1, rows)
            acc += w[2:3] * _shift_rows(prv, cur, nxt, 1, rows)
            acc += w[3:4] * _shift_rows(prv, cur, nxt, 2, rows)
            xc_ref[pl.ds(r0, SUBLANES), :] = acc

        _chunk_loop(t, chunk)

    return pl.pallas_call(
        body, name="conv_fwd", grid=(n_tiles,),
        in_specs=_halo_specs(t, s, c, 1) + [pl.BlockSpec((4, c), lambda i: (0, 0)), pl.BlockSpec((1, c), lambda i: (0, 0))],
        out_specs=pl.BlockSpec((t, c), lambda i: (i, 0)),
        out_shape=jax.ShapeDtypeStruct((s, c), F32),
        compiler_params=_params("arbitrary"),
    )(proj, proj, proj, conv_w, conv_b)


def _gate_matmuls(xc_ref, wa_ref, wi_ref, pr_s, pi_s, heads):
    for h in range(heads):
        cs = pl.ds(h * LRU_HEAD, LRU_HEAD)
        xb = xc_ref[:, cs].astype(BF16)
        pr_s[:, cs] = jnp.dot(xb, wa_ref[h], preferred_element_type=F32)
        pi_s[:, cs] = jnp.dot(xb, wi_ref[h], preferred_element_type=F32)


def _rg_gates(pr, pi, ba, bi, sp):
    r = _sigmoid(pr + ba)
    ig = _sigmoid(pi + bi)
    log_a = (-RG_C * r) * sp
    a = jnp.exp(log_a)
    mult = jnp.sqrt(_neg_expm1(2.0 * log_a))
    return r, ig, a, mult


def _scan_fwd_call(xc, wa, wi, ba, bi, lam, reverse, dep):
    s, c = xc.shape
    heads = c // LRU_HEAD
    t = min(SEQ_TILE, s)
    n_tiles = s // t
    tile = (lambda i: (n_tiles - 1 - i, 0)) if reverse else (lambda i: (i, 0))
    whole2 = lambda i: (0, 0)
    whole3 = lambda i: (0, 0, 0)

    def body(xc_ref, wa_ref, wi_ref, ba_ref, bi_ref, lam_ref, dep_ref, h_ref, r_ref, ig_ref, a_ref, mult_ref,
             pr_s, pi_s, carry_s):
        @pl.when(pl.program_id(0) == 0)
        def _():
            carry_s[...] = jnp.zeros_like(carry_s)

        _gate_matmuls(xc_ref, wa_ref, wi_ref, pr_s, pi_s, heads)
        ba_v, bi_v = ba_ref[...], bi_ref[...]
        sp = _softplus(-lam_ref[...])

        rows = lax.broadcasted_iota(jnp.int32, (SUBLANES, c), 0)

        def chunk(r0, h_in):
            rs = pl.ds(r0, SUBLANES)
            r, ig, a, mult = _rg_gates(pr_s[rs, :], pi_s[rs, :], ba_v, bi_v, sp)
            r_ref[rs, :] = r
            ig_ref[rs, :] = ig
            a_ref[rs, :] = a
            mult_ref[rs, :] = mult
            h = _scan_chunk(a, mult * ig * xc_ref[rs, :], h_in, rows, reverse)
            h_ref[rs, :] = h
            return h[0:1, :] if reverse else h[SUBLANES - 1:SUBLANES, :]

        carry_s[...] = _chunk_loop(t, chunk, carry_s[...], unroll=2, descending=reverse)

    return pl.pallas_call(
        body, name="scan_fwd_rev" if reverse else "scan_fwd", grid=(n_tiles,),
        in_specs=[pl.BlockSpec((t, c), tile),
                  pl.BlockSpec((heads, LRU_HEAD, LRU_HEAD), whole3), pl.BlockSpec((heads, LRU_HEAD, LRU_HEAD), whole3),
                  pl.BlockSpec((1, c), whole2), pl.BlockSpec((1, c), whole2), pl.BlockSpec((1, c), whole2), ANY],
        out_specs=[pl.BlockSpec((t, c), tile)] * 5,
        out_shape=[jax.ShapeDtypeStruct((s, c), F32)] * 5,
        scratch_shapes=[pltpu.VMEM((t, c), F32), pltpu.VMEM((t, c), F32), pltpu.VMEM((1, c), F32)],
        compiler_params=_params("arbitrary"),
    )(xc, wa, wi, ba, bi, lam, dep)


def _window_counts(r0, tile_idx, t_rows, n_rows, half, shape):
    pos = tile_idx * t_rows + r0 + lax.broadcasted_iota(jnp.int32, shape, 0)
    hi = jnp.minimum(pos + half, n_rows)
    lo = jnp.maximum(pos - half, 0)
    return (hi - lo).astype(F32)


def _pool_combine_call(proj, h_f, h_b, w_pool, pool_scale, p):
    s = proj.shape[0]
    c = h_f.shape[1]
    pg = p // N_POOL_GROUPS
    t = min(SEQ_TILE, s)
    n_tiles = s // t

    def body(u_ref, up_ref, un_ref, gate_ref, hf_ref, hb_ref, wp_ref, sc_ref, y_ref, d_ref, d_s, yr_s):
        i = pl.program_id(0)
        rows = lax.broadcasted_iota(jnp.int32, (SUBLANES, pg), 0)

        def chunk(r0, _):
            rs = pl.ds(r0, SUBLANES)
            for g, w in enumerate(POOL_WINDOWS):
                cols = pl.ds(g * pg, pg)
                prv, cur, nxt = _neighbour_chunks(u_ref, up_ref, un_ref, r0, t, cols, i == 0, i == n_tiles - 1)
                tot = cur
                for o in range(-(w // 2), w // 2):
                    if o != 0:
                        tot = tot + _shift_rows(prv, cur, nxt, o, rows)
                cnt = _window_counts(r0, i, t, s, w // 2, (SUBLANES, pg))
                d_s[rs, cols] = tot / cnt - cur
            gate, _ = _gelu_and_grad(gate_ref[rs, :])
            yr_s[rs, :] = (hf_ref[rs, :] + hb_ref[rs, :]) * gate

        _chunk_loop(t, chunk)
        y_ref[:, pl.ds(p, c)] = yr_s[...].astype(BF16)
        d_ref[...] = d_s[...].astype(BF16)
        for g in range(N_POOL_GROUPS):
            cols = pl.ds(g * pg, pg)
            out = jnp.dot(d_s[:, cols].astype(BF16), wp_ref[g], preferred_element_type=F32)
            y_ref[:, cols] = (out * sc_ref[:, cols]).astype(BF16)

    return pl.pallas_call(
        body, name="pool_combine", grid=(n_tiles,),
        in_specs=_halo_specs(t, s, p, 0) + [
            pl.BlockSpec((t, c), lambda i: (i, 2)),
            pl.BlockSpec((t, c), lambda i: (i, 0)), pl.BlockSpec((t, c), lambda i: (i, 0)),
            pl.BlockSpec((N_POOL_GROUPS, pg, pg), lambda i: (0, 0, 0)), pl.BlockSpec((1, p), lambda i: (0, 0))],
        out_specs=[pl.BlockSpec((t, p + c), lambda i: (i, 0)), pl.BlockSpec((t, p), lambda i: (i, 0))],
        out_shape=[jax.ShapeDtypeStruct((s, p + c), BF16), jax.ShapeDtypeStruct((s, p), BF16)],
        scratch_shapes=[pltpu.VMEM((t, p), F32), pltpu.VMEM((t, c), F32)],
        compiler_params=_params("arbitrary"),
    )(proj, proj, proj, proj, h_f, h_b, w_pool, pool_scale)


def _layer_norm_rows(z, g, b):
    mu = jnp.mean(z, axis=-1, keepdims=True)
    zc = z - mu
    var = jnp.mean(zc * zc, axis=-1, keepdims=True)
    rstd = lax.rsqrt(var + LN_EPS)
    xh = zc * rstd
    return xh, rstd, xh * g + b


def _layer_norm_bwd_rows(dx, xh, rstd, g):
    dxh = dx * g
    m1 = jnp.mean(dxh, axis=-1, keepdims=True)
    m2 = jnp.mean(dxh * xh, axis=-1, keepdims=True)
    return rstd * (dxh - m1 - xh * m2)


def _out_ln1_call(y, w_out, x, g1, b1, dep):
    s, d = x.shape
    tm = min(SEQ_TILE, s)

    def body(y_ref, w_ref, x_ref, g_ref, b_ref, dep_ref, xh_ref, x1b_ref, rstd_ref, acc_s, x1_s):
        acc_s[...] = jnp.dot(y_ref[...], w_ref[...], preferred_element_type=F32)
        g, b = g_ref[...], b_ref[...]

        def chunk(r0, _):
            rs = pl.ds(r0, SUBLANES)
            xh, rstd, x1 = _layer_norm_rows(ALPHA * x_ref[rs, :] + acc_s[rs, :], g, b)
            xh_ref[rs, :] = xh
            x1_s[rs, :] = x1
            rstd_ref[rs, :] = rstd

        _chunk_loop(tm, chunk, unroll=LN_UNROLL)
        x1b_ref[...] = x1_s[...].astype(BF16)

    return pl.pallas_call(
        body, name="out_ln1", grid=(s // tm,),
        in_specs=[pl.BlockSpec((tm, d), lambda i: (i, 0)), pl.BlockSpec((d, d), lambda i: (0, 0)),
                  pl.BlockSpec((tm, d), lambda i: (i, 0)),
                  pl.BlockSpec((1, d), lambda i: (0, 0)), pl.BlockSpec((1, d), lambda i: (0, 0)), ANY],
        out_specs=[pl.BlockSpec((tm, d), lambda i: (i, 0)), pl.BlockSpec((tm, d), lambda i: (i, 0)),
                   pl.BlockSpec((tm, 1), lambda i: (i, 0))],
        out_shape=[jax.ShapeDtypeStruct((s, d), F32), jax.ShapeDtypeStruct((s, d), BF16), jax.ShapeDtypeStruct((s, 1), F32)],
        scratch_shapes=[pltpu.VMEM((tm, d), F32), pltpu.VMEM((tm, d), F32)],
        compiler_params=_params("arbitrary"),
    )(y, w_out, x, g1, b1, dep)


def _mlp_in_call(x1b, w1, dep, done):
    s, d = x1b.shape
    n, _, f4 = w1.shape
    tm = min(MM_TILE, s)
    tn = min(1024, f4)
    per = f4 // tn
    blocks = n * per // 2
    first = 0 if done is None else blocks
    extra = [] if done is None else list(done)

    def body(x_ref, w_ref, dep_ref, *rest):
        r_ref, q_ref = rest[-2:]
        r = jnp.maximum(jnp.dot(x_ref[...], w_ref[...], preferred_element_type=F32), 0.0)
        r_ref[...] = r.astype(BF16)
        q_ref[...] = (r * r).astype(BF16)

    out_spec = pl.BlockSpec((tm, tn), lambda j, i: (i, first + j))
    return pl.pallas_call(
        body, name="mlp_in" if done is None else "mlp_in_rest", grid=(blocks, s // tm),
        in_specs=[pl.BlockSpec((tm, d), lambda j, i: (i, 0)),
                  pl.BlockSpec((None, d, tn), lambda j, i: ((first + j) // per, 0, (first + j) % per)), ANY] + [ANY] * len(extra),
        out_specs=[out_spec, out_spec],
        out_shape=[jax.ShapeDtypeStruct((s, n * f4), BF16), jax.ShapeDtypeStruct((s, n * f4), BF16)],
        input_output_aliases={3: 0, 4: 1} if extra else {},
        compiler_params=_params("arbitrary", "arbitrary"),
    )(x1b, w1, dep, *extra)


def _mlp_out_ln2_call(hsq, w2, xh1, g1, b1, g2, b2, target):
    s, f = hsq.shape
    d = w2.shape[1]
    tm = min(MM_TILE, s)
    tk = min(LN_MM_K, f)
    nk = f // tk

    def body(h_ref, w_ref, xh1_ref, g1_ref, b1_ref, g2_ref, b2_ref, t_ref,
             dz_ref, dzb_ref, loss_ref, dg_ref, db_ref, acc_s):
        i, k = pl.program_id(0), pl.program_id(1)

        @pl.when((i == 0) & (k == 0))
        def _():
            loss_ref[...] = jnp.zeros_like(loss_ref)
            dg_ref[...] = jnp.zeros_like(dg_ref)
            db_ref[...] = jnp.zeros_like(db_ref)

        @pl.when(k == 0)
        def _():
            acc_s[...] = jnp.zeros_like(acc_s)

        acc_s[...] += jnp.dot(h_ref[...], w_ref[...], preferred_element_type=F32)

        @pl.when(k == nk - 1)
        def _():
            g1, b1, g2, b2 = g1_ref[...], b1_ref[...], g2_ref[...], b2_ref[...]

            def chunk(r0, _):
                rs = pl.ds(r0, SUBLANES)
                x1 = xh1_ref[rs, :] * g1 + b1
                xh2, rstd, x2 = _layer_norm_rows(ALPHA * x1 + acc_s[rs, :], g2, b2)
                diff = x2 - t_ref[rs, :]
                loss_ref[...] += diff * diff
                dx2 = diff * (1.0 / d)
                dg_ref[...] += dx2 * xh2
                db_ref[...] += dx2
                dz = _layer_norm_bwd_rows(dx2, xh2, rstd, g2)
                dz_ref[rs, :] = dz

            _chunk_loop(tm, chunk, unroll=LN_UNROLL)
            dzb_ref[...] = dz_ref[...].astype(BF16)

    row = lambda i, k: (i, 0)
    vec = lambda i, k: (0, 0)
    return pl.pallas_call(
        body, name="mlp_out_ln2", grid=(s // tm, nk),
        in_specs=[pl.BlockSpec((tm, tk), lambda i, k: (i, k)), pl.BlockSpec((tk, d), lambda i, k: (k, 0)),
                  pl.BlockSpec((tm, d), row), pl.BlockSpec((1, d), vec), pl.BlockSpec((1, d), vec),
                  pl.BlockSpec((1, d), vec), pl.BlockSpec((1, d), vec), pl.BlockSpec((tm, d), row)],
        out_specs=[pl.BlockSpec((tm, d), row), pl.BlockSpec((tm, d), row),
                   pl.BlockSpec((SUBLANES, d), vec), pl.BlockSpec((SUBLANES, d), vec), pl.BlockSpec((SUBLANES, d), vec)],
        out_shape=[jax.ShapeDtypeStruct((s, d), F32), jax.ShapeDtypeStruct((s, d), BF16),
                   jax.ShapeDtypeStruct((SUBLANES, d), F32), jax.ShapeDtypeStruct((SUBLANES, d), F32),
                   jax.ShapeDtypeStruct((SUBLANES, d), F32)],
        scratch_shapes=[pltpu.VMEM((tm, d), F32)],
        compiler_params=_params("arbitrary", "arbitrary"),
    )(hsq, w2, xh1, g1, b1, g2, b2, target)


def _half_grad_call(a, b, half, row_sharded, init, name, dep):
    s, m = a.shape
    n = b.shape[1]
    if row_sharded:
        rows, cols = m // (2 * N_CHIPS), n
        tm = min(1024, rows)
        per = rows // tm
        tn = min(1024, cols)
        n_i, n_j = N_CHIPS * per, cols // tn
        a_block = lambda i, h: ((i // per) * 2 + h) * per + i % per
        out_block = lambda i, j: (i // per, i % per, j)
    else:
        rows, cols = m // 2, n // N_CHIPS
        tm = min(1024, rows)
        per = rows // tm
        tn = cols if cols % 1024 else 1024
        per_n = cols // tn
        n_i, n_j = per, N_CHIPS * per_n
        a_block = lambda i, h: h * per + i
        out_block = lambda i, j: (j // per_n, i, j % per_n)
    tk = min(2048, s)
    if row_sharded and tm < 1024 and s * n * 2 <= RESIDENT_OPERAND_BYTES:
        tk, tn, n_j = s, n, 1
    has_init = init is not None

    def body(half_ref, a_ref, b_ref, *rest):
        o_ref = rest[-1]

        @pl.when(pl.program_id(2) == 0)
        def _():
            o_ref[...] = rest[0][...] if has_init else jnp.zeros_like(o_ref)

        o_ref[...] += lax.dot_general(a_ref[...], b_ref[...], (((0,), (0,)), ((), ())), preferred_element_type=F32)

    out_spec = pl.BlockSpec((None, tm, tn), lambda i, j, k, h: out_block(i, j))
    in_specs = [pl.BlockSpec((tk, tm), lambda i, j, k, h: (k, a_block(i, h[0]))),
                pl.BlockSpec((tk, tn), lambda i, j, k, h: (k, j))]
    args = [a, b]
    if has_init:
        in_specs.append(out_spec)
        args.append(init)
    in_specs.append(ANY)
    args.append(dep)
    return pl.pallas_call(
        body, name=name,
        grid_spec=pltpu.PrefetchScalarGridSpec(num_scalar_prefetch=1, grid=(n_i, n_j, s // tk), in_specs=in_specs,
                                               out_specs=out_spec),
        out_shape=jax.ShapeDtypeStruct((N_CHIPS, rows, cols), F32),
        compiler_params=_params("arbitrary", "arbitrary", "arbitrary"),
    )(half, *args)


def _dhsq_call(dzb, w2, r, dep):
    s, d = dzb.shape
    f = w2.shape[0]
    tm = min(MM_TILE, s)
    tn = min(1024, f)

    def body(dz_ref, w_ref, r_ref, dep_ref, o_ref):
        dh = lax.dot_general(dz_ref[...], w_ref[...], (((1,), (1,)), ((), ())), preferred_element_type=F32)
        o_ref[...] = (dh * (2.0 * r_ref[...].astype(F32))).astype(BF16)

    return pl.pallas_call(
        body, name="mlp_dpre", grid=(f // tn, s // tm),
        in_specs=[pl.BlockSpec((tm, d), lambda j, i: (i, 0)), pl.BlockSpec((tn, d), lambda j, i: (j, 0)),
                  pl.BlockSpec((tm, tn), lambda j, i: (i, j)), ANY],
        out_specs=pl.BlockSpec((tm, tn), lambda j, i: (i, j)),
        out_shape=jax.ShapeDtypeStruct((s, f), BF16),
        compiler_params=_params("arbitrary", "arbitrary"),
    )(dzb, w2, r, dep)


def _dx1_ln1_bwd_call(dpre, w1, dz2, xh1, rstd1, g1, dep):
    s, f = dpre.shape
    n, d, f4 = w1.shape
    tm = min(MM_TILE, s)
    tk = min(LN_MM_K, f4)
    per = f4 // tk
    nk = n * per

    def body(dp_ref, w_ref, dz2_ref, xh_ref, rstd_ref, g_ref, dep_ref, dz_ref, dzb_ref, dg_ref, db_ref, acc_s):
        i, k = pl.program_id(0), pl.program_id(1)

        @pl.when((i == 0) & (k == 0))
        def _():
            dg_ref[...] = jnp.zeros_like(dg_ref)
            db_ref[...] = jnp.zeros_like(db_ref)

        @pl.when(k == 0)
        def _():
            acc_s[...] = jnp.zeros_like(acc_s)

        acc_s[...] += lax.dot_general(dp_ref[...], w_ref[...], (((1,), (1,)), ((), ())), preferred_element_type=F32)

        @pl.when(k == nk - 1)
        def _():
            g = g_ref[...]

            def chunk(r0, _):
                rs = pl.ds(r0, SUBLANES)
                dx1 = acc_s[rs, :] + ALPHA * dz2_ref[rs, :]
                xh = xh_ref[rs, :]
                dg_ref[...] += dx1 * xh
                db_ref[...] += dx1
                dz = _layer_norm_bwd_rows(dx1, xh, rstd_ref[rs, :], g)
                dz_ref[rs, :] = dz

            _chunk_loop(tm, chunk, unroll=LN_UNROLL)
            dzb_ref[...] = dz_ref[...].astype(BF16)

    row = lambda i, k: (i, 0)
    vec = lambda i, k: (0, 0)
    return pl.pallas_call(
        body, name="dx1_ln1_bwd", grid=(s // tm, nk),
        in_specs=[pl.BlockSpec((tm, tk), lambda i, k: (i, k)),
                  pl.BlockSpec((None, d, tk), lambda i, k: (k // per, 0, k % per)),
                  pl.BlockSpec((tm, d), row), pl.BlockSpec((tm, d), row), pl.BlockSpec((tm, 1), row),
                  pl.BlockSpec((1, d), vec), ANY],
        out_specs=[pl.BlockSpec((tm, d), row), pl.BlockSpec((tm, d), row),
                   pl.BlockSpec((SUBLANES, d), vec), pl.BlockSpec((SUBLANES, d), vec)],
        out_shape=[jax.ShapeDtypeStruct((s, d), F32), jax.ShapeDtypeStruct((s, d), BF16),
                   jax.ShapeDtypeStruct((SUBLANES, d), F32), jax.ShapeDtypeStruct((SUBLANES, d), F32)],
        scratch_shapes=[pltpu.VMEM((tm, d), F32)],
        compiler_params=_params("arbitrary", "arbitrary"),
    )(dpre, w1, dz2, xh1, rstd1, g1, dep)


def _dy_call(dzb, w_out, dep):
    s, d = dzb.shape
    e = w_out.shape[0]
    tm = min(MM_TILE, s)

    def body(dz_ref, w_ref, dep_ref, o_ref):
        o_ref[...] = lax.dot_general(dz_ref[...], w_ref[...], (((1,), (1,)), ((), ())), preferred_element_type=F32)

    return pl.pallas_call(
        body, name="dy", grid=(s // tm,),
        in_specs=[pl.BlockSpec((tm, d), lambda i: (i, 0)), pl.BlockSpec((e, d), lambda i: (0, 0)), ANY],
        out_specs=pl.BlockSpec((tm, e), lambda i: (i, 0)),
        out_shape=jax.ShapeDtypeStruct((s, e), F32),
        compiler_params=_params("arbitrary"),
    )(dzb, w_out, dep)


def _mixer_bwd_call(dy, d_pool, proj, h_f, h_b, w_pool, pool_scale, p):
    s = dy.shape[0]
    c = h_f.shape[1]
    pg = p // N_POOL_GROUPS
    t = min(SEQ_TILE, s)
    n_tiles = s // t

    def body(dyp_ref, dyr_ref, d_ref, gate_ref, hf_ref, hb_ref, wp_ref, sc_ref,
             e_ref, dh_ref, dgate_ref, dwp_ref, dsc_ref, dd_s):
        i = pl.program_id(0)

        @pl.when(i == 0)
        def _():
            dwp_ref[...] = jnp.zeros_like(dwp_ref)
            dsc_ref[...] = jnp.zeros_like(dsc_ref)

        for g in range(N_POOL_GROUPS):
            cols = pl.ds(g * pg, pg)
            dg = d_ref[:, cols]
            out = jnp.dot(dg, wp_ref[g], preferred_element_type=F32)
            dyp = dyp_ref[:, cols]
            prod = dyp * out
            dsc_ref[:, cols] += jnp.sum(prod.reshape(t // SUBLANES, SUBLANES, pg), axis=0)
            dout = (dyp * sc_ref[:, cols]).astype(BF16)
            dwp_ref[g] += lax.dot_general(dg, dout, (((0,), (0,)), ((), ())), preferred_element_type=F32)
            dd_s[:, cols] = lax.dot_general(dout, wp_ref[g], (((1,), (1,)), ((), ())), preferred_element_type=F32)

        def chunk(r0, _):
            rs = pl.ds(r0, SUBLANES)
            for g, w in enumerate(POOL_WINDOWS):
                cols = pl.ds(g * pg, pg)
                cnt = _window_counts(r0, i, t, s, w // 2, (SUBLANES, pg))
                e_ref[rs, cols] = dd_s[rs, cols] / cnt
            gate, dgate = _gelu_and_grad(gate_ref[rs, :])
            dyr = dyr_ref[rs, :]
            dh_ref[rs, :] = dyr * gate
            dd_s[rs, :] = dyr * (hf_ref[rs, :] + hb_ref[rs, :]) * dgate

        _chunk_loop(t, chunk)
        dgate_ref[...] = dd_s[...].astype(BF16)

    tile = lambda i: (i, 0)
    return pl.pallas_call(
        body, name="mixer_bwd", grid=(n_tiles,),
        in_specs=[pl.BlockSpec((t, p), tile), pl.BlockSpec((t, c), lambda i: (i, 1)), pl.BlockSpec((t, p), tile),
                  pl.BlockSpec((t, c), lambda i: (i, 2)), pl.BlockSpec((t, c), tile), pl.BlockSpec((t, c), tile),
                  pl.BlockSpec((N_POOL_GROUPS, pg, pg), lambda i: (0, 0, 0)), pl.BlockSpec((1, p), lambda i: (0, 0))],
        out_specs=[pl.BlockSpec((t, p), tile), pl.BlockSpec((t, c), tile), pl.BlockSpec((t, c), tile),
                   pl.BlockSpec((N_POOL_GROUPS, pg, pg), lambda i: (0, 0, 0)), pl.BlockSpec((SUBLANES, p), lambda i: (0, 0))],
        out_shape=[jax.ShapeDtypeStruct((s, p), F32), jax.ShapeDtypeStruct((s, c), F32), jax.ShapeDtypeStruct((s, c), BF16),
                   jax.ShapeDtypeStruct((N_POOL_GROUPS, pg, pg), F32), jax.ShapeDtypeStruct((SUBLANES, p), F32)],
        scratch_shapes=[pltpu.VMEM((t, p), F32)],
        compiler_params=_params("arbitrary"),
    )(dy, dy, d_pool, proj, h_f, h_b, w_pool, pool_scale)


def _scan_bwd_call(xc, dh, h_dir, gates, dxc_prev, wa, wi, lam, reverse, dep):
    s, c = xc.shape
    heads = c // LRU_HEAD
    t = min(SEQ_TILE, s)
    n_tiles = s // t
    per = t // SUBLANES
    last_blk = s // SUBLANES - 1
    tile = (lambda i: (i, 0)) if reverse else (lambda i: (n_tiles - 1 - i, 0))
    if reverse:
        halo = lambda i: (jnp.minimum((i + 1) * per, last_blk), 0)
    else:
        halo = lambda i: (jnp.maximum((n_tiles - 1 - i) * per - 1, 0), 0)
    whole2 = lambda i: (0, 0)
    whole3 = lambda i: (0, 0, 0)
    has_prev = dxc_prev is not None
    n_in = 11 + int(has_prev) + 1

    def body(*refs):
        xc_ref, dh_ref, h_ref, hh_ref, r_ref, ig_ref, a_ref, mult_ref = refs[:8]
        prev_ref = refs[8] if has_prev else None
        wa_ref, wi_ref, lam_ref = refs[n_in - 4:n_in - 1]
        dxc_ref, dwa_ref, dwi_ref, dba_ref, dbi_ref, dsp_ref = refs[n_in:n_in + 6]
        pr_s, pi_s, carry_s = refs[n_in + 6:]
        step = pl.program_id(0)
        tile_idx = step if reverse else n_tiles - 1 - step

        @pl.when(step == 0)
        def _():
            carry_s[...] = jnp.zeros_like(carry_s)
            dwa_ref[...] = jnp.zeros_like(dwa_ref)
            dwi_ref[...] = jnp.zeros_like(dwi_ref)
            dba_ref[...] = jnp.zeros_like(dba_ref)
            dbi_ref[...] = jnp.zeros_like(dbi_ref)
            dsp_ref[...] = jnp.zeros_like(dsp_ref)

        sp = _softplus(-lam_ref[...])
        rows = lax.broadcasted_iota(jnp.int32, (SUBLANES, c), 0)

        def chunk(r0, u_in):
            rs = pl.ds(r0, SUBLANES)
            xcv = xc_ref[rs, :]
            r, ig, a, mult = r_ref[rs, :], ig_ref[rs, :], a_ref[rs, :], mult_ref[rs, :]
            dhv = dh_ref[rs, :]
            u = _scan_chunk(a, a * dhv, u_in, rows, not reverse)
            if reverse:
                gt = dhv + jnp.where(rows >= 1, pltpu.roll(u, 1, 0), u_in)
                u_out = u[SUBLANES - 1:SUBLANES, :]
            else:
                gt = dhv + jnp.where(rows < SUBLANES - 1, pltpu.roll(u, SUBLANES - 1, 0), u_in)
                u_out = u[0:1, :]
            cur = h_ref[rs, :]
            if reverse:
                after = h_ref[pl.ds(pl.multiple_of(jnp.minimum(r0 + SUBLANES, t - SUBLANES), SUBLANES), SUBLANES), :]
                edge = jnp.where(tile_idx == n_tiles - 1, 0.0, hh_ref[...])
                nxt = jnp.where(r0 == t - SUBLANES, edge, after)
                hs = _shift_rows(cur, cur, nxt, 1, rows)
            else:
                before = h_ref[pl.ds(pl.multiple_of(jnp.maximum(r0 - SUBLANES, 0), SUBLANES), SUBLANES), :]
                edge = jnp.where(tile_idx == 0, 0.0, hh_ref[...])
                prv = jnp.where(r0 == 0, edge, before)
                hs = _shift_rows(prv, cur, cur, -1, rows)
            gx = gt * xcv
            dmult = gx * ig
            di = gx * mult
            dlog_a = (gt * hs) * a - dmult * (a * a) / mult
            dr = dlog_a * (-RG_C * sp)
            dsp_ref[...] += dlog_a * (-RG_C * r)
            dpr = dr * r * (1.0 - r)
            dpi = di * ig * (1.0 - ig)
            dba_ref[...] += dpr
            dbi_ref[...] += dpi
            direct = gt * mult * ig
            if has_prev:
                direct = direct + prev_ref[rs, :]
            dxc_ref[rs, :] = direct
            pr_s[rs, :] = dpr
            pi_s[rs, :] = dpi
            return u_out

        carry_s[...] = _chunk_loop(t, chunk, carry_s[...], unroll=2, descending=not reverse)

        for h in range(heads):
            cs = pl.ds(h * LRU_HEAD, LRU_HEAD)
            xb = xc_ref[:, cs].astype(BF16)
            dprb = pr_s[:, cs].astype(BF16)
            dpib = pi_s[:, cs].astype(BF16)
            dwa_ref[h] += lax.dot_general(xb, dprb, (((0,), (0,)), ((), ())), preferred_element_type=F32)
            dwi_ref[h] += lax.dot_general(xb, dpib, (((0,), (0,)), ((), ())), preferred_element_type=F32)
            dxc_ref[:, cs] += (
                lax.dot_general(dprb, wa_ref[h], (((1,), (1,)), ((), ())), preferred_element_type=F32)
                + lax.dot_general(dpib, wi_ref[h], (((1,), (1,)), ((), ())), preferred_element_type=F32))

    tile_spec = pl.BlockSpec((t, c), tile)
    in_specs = [tile_spec, tile_spec, tile_spec, pl.BlockSpec((SUBLANES, c), halo)] + [tile_spec] * 4
    args = [xc, dh, h_dir, h_dir, *gates]
    if has_prev:
        in_specs.append(tile_spec)
        args.append(dxc_prev)
    in_specs += [pl.BlockSpec((heads, LRU_HEAD, LRU_HEAD), whole3), pl.BlockSpec((heads, LRU_HEAD, LRU_HEAD), whole3),
                 pl.BlockSpec((1, c), whole2), ANY]
    args += [wa, wi, lam, dep]
    assert len(args) == n_in
    return pl.pallas_call(
        body, name="scan_bwd_rev" if reverse else "scan_bwd", grid=(n_tiles,),
        in_specs=in_specs,
        out_specs=[tile_spec,
                   pl.BlockSpec((heads, LRU_HEAD, LRU_HEAD), whole3), pl.BlockSpec((heads, LRU_HEAD, LRU_HEAD), whole3),
                   pl.BlockSpec((SUBLANES, c), whole2), pl.BlockSpec((SUBLANES, c), whole2), pl.BlockSpec((SUBLANES, c), whole2)],
        out_shape=[jax.ShapeDtypeStruct((s, c), F32),
                   jax.ShapeDtypeStruct((heads, LRU_HEAD, LRU_HEAD), F32), jax.ShapeDtypeStruct((heads, LRU_HEAD, LRU_HEAD), F32),
                   jax.ShapeDtypeStruct((SUBLANES, c), F32), jax.ShapeDtypeStruct((SUBLANES, c), F32),
                   jax.ShapeDtypeStruct((SUBLANES, c), F32)],
        scratch_shapes=[pltpu.VMEM((t, c), F32), pltpu.VMEM((t, c), F32), pltpu.VMEM((1, c), F32)],
        compiler_params=_params("arbitrary"),
    )(*args)


def _dproj_call(e_pool, dxc, proj, dgate, conv_w, p):
    s, c = dxc.shape
    pg = p // N_POOL_GROUPS
    t = min(SEQ_TILE, s)
    n_tiles = s // t

    def body(e_ref, ep_ref, en_ref, dx_ref, dxp_ref, dxn_ref, u_ref, up_ref, un_ref, dgate_ref, w_ref,
             dproj_ref, dcw_ref, dcb_ref, st_s):
        i = pl.program_id(0)
        first, last = i == 0, i == n_tiles - 1

        @pl.when(first)
        def _():
            dcw_ref[...] = jnp.zeros_like(dcw_ref)
            dcb_ref[...] = jnp.zeros_like(dcb_ref)

        rows_p = lax.broadcasted_iota(jnp.int32, (SUBLANES, pg), 0)
        rows_c = lax.broadcasted_iota(jnp.int32, (SUBLANES, c), 0)
        w = w_ref[...]

        def chunk(r0, _):
            rs = pl.ds(r0, SUBLANES)
            for g, win in enumerate(POOL_WINDOWS):
                cols = pl.ds(g * pg, pg)
                prv, cur, nxt = _neighbour_chunks(e_ref, ep_ref, en_ref, r0, t, cols, first, last)
                tot = cur
                for o in range(-(win // 2) + 1, win // 2 + 1):
                    if o != 0:
                        tot = tot + _shift_rows(prv, cur, nxt, o, rows_p)
                cnt = _window_counts(r0, i, t, s, win // 2, (SUBLANES, pg))
                st_s[rs, cols] = tot - cur * cnt
            prv, cur, nxt = _neighbour_chunks(dx_ref, dxp_ref, dxn_ref, r0, t, slice(None), first, last)
            du = w[1:2] * cur
            du += w[0:1] * _shift_rows(prv, cur, nxt, 1, rows_c)
            du += w[2:3] * _shift_rows(prv, cur, nxt, -1, rows_c)
            du += w[3:4] * _shift_rows(prv, cur, nxt, -2, rows_c)
            st_s[rs, pl.ds(p, c)] = du
            uprv, ucur, unxt = _neighbour_chunks(u_ref, up_ref, un_ref, r0, t, slice(None), first, last)
            dcb_ref[...] += cur
            for j, o in enumerate((-1, 0, 1, 2)):
                dcw_ref[j] += cur * _shift_rows(uprv, ucur, unxt, o, rows_c)

        _chunk_loop(t, chunk)
        dproj_ref[:, pl.ds(0, p + c)] = st_s[...].astype(BF16)
        dproj_ref[:, pl.ds(p + c, c)] = dgate_ref[...]

    return pl.pallas_call(
        body, name="dproj", grid=(n_tiles,),
        in_specs=_halo_specs(t, s, p, 0) + _halo_specs(t, s, c, 0) + _halo_specs(t, s, c, 1) + [
            pl.BlockSpec((t, c), lambda i: (i, 0)), pl.BlockSpec((4, c), lambda i: (0, 0))],
        out_specs=[pl.BlockSpec((t, p + 2 * c), lambda i: (i, 0)),
                   pl.BlockSpec((4, SUBLANES, c), lambda i: (0, 0, 0)), pl.BlockSpec((SUBLANES, c), lambda i: (0, 0))],
        out_shape=[jax.ShapeDtypeStruct((s, p + 2 * c), BF16), jax.ShapeDtypeStruct((4, SUBLANES, c), F32),
                   jax.ShapeDtypeStruct((SUBLANES, c), F32)],
        scratch_shapes=[pltpu.VMEM((t, p + c), F32)],
        compiler_params=_params("arbitrary"),
    )(e_pool, e_pool, e_pool, dxc, dxc, dxc, proj, proj, proj, dgate, conv_w)


def _dx_call(dproj, w_in, dz1, dep):
    s, e = dproj.shape
    n, d, e4 = w_in.shape
    tm = min(MM_TILE, s)

    def body(dp_ref, w_ref, dz_ref, dep_ref, o_ref):
        k = pl.program_id(1)

        @pl.when(k == 0)
        def _():
            o_ref[...] = ALPHA * dz_ref[...]

        o_ref[...] += lax.dot_general(dp_ref[...], w_ref[...], (((1,), (1,)), ((), ())), preferred_element_type=F32)

    return pl.pallas_call(
        body, name="grad_x", grid=(s // tm, n),
        in_specs=[pl.BlockSpec((tm, e4), lambda i, k: (i, k)), pl.BlockSpec((None, d, e4), lambda i, k: (k, 0, 0)),
                  pl.BlockSpec((tm, d), lambda i, k: (i, 0)), ANY],
        out_specs=pl.BlockSpec((tm, d), lambda i, k: (i, 0)),
        out_shape=jax.ShapeDtypeStruct((s, d), F32),
        compiler_params=_params("arbitrary", "arbitrary"),
    )(dproj, w_in, dz1, dep)


def _row_tile(rows, cols, n_arrays):
    limit = max(SUBLANES, ELT_BLOCK_BYTES // (4 * cols * max(1, n_arrays // 4)))
    best = SUBLANES
    for cand in range(SUBLANES, min(rows, limit) + 1, SUBLANES):
        if rows % cand == 0:
            best = cand
    return best if rows % SUBLANES == 0 else rows


def _cast_to_slot_call(a, idx, dtype, name):
    rows, cols = a.shape
    tr = _row_tile(rows, cols, 2)

    def body(idx_ref, a_ref, o_ref):
        o_ref[...] = a_ref[...].astype(dtype)

    return pl.pallas_call(
        body, name=name,
        grid_spec=pltpu.PrefetchScalarGridSpec(
            num_scalar_prefetch=1, grid=(rows // tr,),
            in_specs=[pl.BlockSpec((tr, cols), lambda i, idx_ref: (i, 0))],
            out_specs=pl.BlockSpec((None, tr, cols), lambda i, idx_ref: (idx_ref[1], i, 0))),
        out_shape=jax.ShapeDtypeStruct((N_CHIPS, rows, cols), dtype),
        compiler_params=_params("arbitrary"),
    )(idx, a)


def _add_half_call(g, recv, idx, name):
    _, rows, cols = g.shape
    tr = _row_tile(rows, cols, 3)

    def body(idx_ref, g_ref, r_ref, o_ref):
        o_ref[...] = g_ref[...] + r_ref[...]

    return pl.pallas_call(
        body, name=name,
        grid_spec=pltpu.PrefetchScalarGridSpec(
            num_scalar_prefetch=1, grid=(rows // tr,),
            in_specs=[pl.BlockSpec((None, tr, cols), lambda i, idx_ref: (idx_ref[0], i, 0)),
                      pl.BlockSpec((tr, cols), lambda i, idx_ref: (i, 0))],
            out_specs=pl.BlockSpec((None, tr, cols), lambda i, idx_ref: (idx_ref[1], i, 0))),
        out_shape=jax.ShapeDtypeStruct((N_CHIPS, rows, cols), F32),
        compiler_params=_params("arbitrary"),
    )(idx, g, recv)


def _sum_chips_call(own, recv, idx, name):
    _, rows, cols = recv.shape
    tr = _row_tile(rows, cols, 5)
    out_spec = pl.BlockSpec((None, tr, cols), lambda i, idx_ref: (idx_ref[0], i, 0))
    if own is None:
        def body(idx_ref, r_ref, o_ref):
            o_ref[...] = ((r_ref[0] + r_ref[1]) + r_ref[2]) + r_ref[3]
        in_specs = [pl.BlockSpec((N_CHIPS, tr, cols), lambda i, idx_ref: (0, i, 0))]
        args = (recv,)
    else:
        def body(idx_ref, p_ref, r_ref, o_ref):
            o_ref[...] = ((p_ref[...] + r_ref[0]) + r_ref[1]) + r_ref[2]
        in_specs = [pl.BlockSpec((None, tr, cols), lambda i, idx_ref: (idx_ref[1], i, 0)),
                    pl.BlockSpec((N_CHIPS - 1, tr, cols), lambda i, idx_ref: (0, i, 0))]
        args = (own, recv)
    return pl.pallas_call(
        body, name=name,
        grid_spec=pltpu.PrefetchScalarGridSpec(num_scalar_prefetch=1, grid=(rows // tr,), in_specs=in_specs, out_specs=out_spec),
        out_shape=jax.ShapeDtypeStruct((2, rows, cols), F32),
        compiler_params=_params("arbitrary"),
    )(idx, *args)


def _adamw_call(g, w, m, v, name):
    rows, cols = w.shape
    tr = _row_tile(rows, cols, 8)

    def body(g_ref, w_ref, m_ref, v_ref, go_ref, d_ref, mo_ref, vo_ref):
        gv = g_ref[...]
        go_ref[...] = gv
        mn = ADAM_B1 * m_ref[...] + (1.0 - ADAM_B1) * gv
        vn = ADAM_B2 * v_ref[...] + (1.0 - ADAM_B2) * (gv * gv)
        m_hat = mn / (1.0 - ADAM_B1 ** ADAM_STEP)
        v_hat = vn / (1.0 - ADAM_B2 ** ADAM_STEP)
        d_ref[...] = -ADAM_LR * (m_hat / (jnp.sqrt(v_hat) + ADAM_EPS) + ADAM_WD * w_ref[...])
        mo_ref[...] = mn
        vo_ref[...] = vn

    spec = pl.BlockSpec((tr, cols), lambda i: (i, 0))
    shape = jax.ShapeDtypeStruct((rows, cols), F32)
    return pl.pallas_call(
        body, name=name, grid=(rows // tr,),
        in_specs=[spec] * 4, out_specs=[spec] * 4, out_shape=[shape] * 4,
        compiler_params=_params("arbitrary"),
    )(g, w, m, v)


def _mesh_place():
    x, y, c = lax.axis_index("x"), lax.axis_index("y"), lax.axis_index("c")
    chips = [(1 - x, y), (x, 1 - y), (1 - x, 1 - y)]
    return x, y, c, chips


def _remote(src, dst, send_sems, recv_sems, idx, device):
    return pltpu.make_async_remote_copy(src_ref=src, dst_ref=dst, send_sem=send_sems.at[idx], recv_sem=recv_sems.at[idx],
                                        device_id=device, device_id_type=MESH)


HBM_SPEC = pl.BlockSpec(memory_space=pltpu.HBM)
SEM_SPEC = pl.BlockSpec(memory_space=pltpu.SEMAPHORE)
ORDERED_EFFECT = pltpu.SideEffectType.DATAFLOW_SIDE_EFFECTING


def _in_hbm(a):
    return pltpu.with_memory_space_constraint(a, pltpu.HBM)


def _start_copies_call(name, bufs, groups, after=None):
    n, g = len(bufs), len(groups)
    extra = [] if after is None else [after]
    first_out = n + len(extra)

    def body(*refs):
        outs = refs[first_out:first_out + n]
        sems = refs[first_out + n:first_out + n + 2 * g]
        token = refs[first_out + n + 2 * g]
        for i, (which, copies_fn, _) in enumerate(groups):
            for mine, _ in copies_fn([outs[w] for w in which], sems[2 * i], sems[2 * i + 1]):
                mine.start()
        token[...] = jnp.zeros_like(token)

    sem_shapes = [pltpu.SemaphoreType.DMA((cnt,)) for _, _, cnt in groups for _ in range(2)]
    res = pl.pallas_call(
        body, name=name,
        in_specs=[HBM_SPEC] * n + [ANY] * len(extra),
        out_specs=[HBM_SPEC] * n + [SEM_SPEC] * (2 * g) + [pl.BlockSpec(memory_space=pltpu.VMEM)],
        out_shape=[pltpu.HBM(a.shape, a.dtype) for a in bufs] + sem_shapes + [jax.ShapeDtypeStruct((SUBLANES, LANES), F32)],
        input_output_aliases={a: a for a in range(n)},
        compiler_params=pltpu.CompilerParams(has_side_effects=ORDERED_EFFECT),
    )(*[_in_hbm(a) for a in bufs], *extra)
    sems = res[n:n + 2 * g]
    return list(res[:n]), [(sems[2 * i], sems[2 * i + 1]) for i in range(g)], res[n + 2 * g]


def _wait_copies_call(name, bufs, sems, copies_fn, after):
    n = len(bufs)

    def body(*refs):
        ins = refs[:n]
        send_sems, recv_sems = refs[n], refs[n + 1]
        for mine, arriving in copies_fn(list(ins), send_sems, recv_sems):
            arriving.wait_recv()
            mine.wait_send()

    res = pl.pallas_call(
        body, name=name,
        in_specs=[HBM_SPEC] * n + [SEM_SPEC, SEM_SPEC, ANY],
        out_specs=[HBM_SPEC] * n,
        out_shape=[pltpu.HBM(a.shape, a.dtype) for a in bufs],
        input_output_aliases={a: a for a in range(n)},
        compiler_params=pltpu.CompilerParams(has_side_effects=ORDERED_EFFECT),
    )(*bufs, sems[0], sems[1], after)
    return list(res)


def _gather_copies(bufs, send_sems, recv_sems):
    x, y, c, chips = _mesh_place()
    k = 2 * x + y
    out = []
    for a, buf in enumerate(bufs):
        for j, (px, py) in enumerate(chips):
            kj = 2 * px + py
            mine = _remote(buf.at[k, c], buf.at[k, c], send_sems, recv_sems, 3 * a + j, (px, py, c))
            arriving = _remote(buf.at[k, c], buf.at[kj, c], send_sems, recv_sems, 3 * a + j, (px, py, c))
            out.append((mine, arriving))
    return out


def _exchange_copies(n_sharded, n_replicated):
    def copies(bufs, send_sems, recv_sems):
        x, y, c, chips = _mesh_place()
        k = 2 * x + y
        sums, lands = bufs[:n_sharded], bufs[n_sharded:2 * n_sharded]
        repl = bufs[2 * n_sharded:]
        out = []
        for j, (px, py) in enumerate(chips):
            kj = 2 * px + py
            for a in range(n_sharded):
                cp = _remote(sums[a].at[kj], lands[a].at[j], send_sems, recv_sems, 3 * a + j, (px, py, c))
                out.append((cp, cp))
            for a in range(n_replicated):
                idx = 3 * (n_sharded + a) + j
                mine = _remote(repl[a].at[k], repl[a].at[k], send_sems, recv_sems, idx, (px, py, c))
                arriving = _remote(repl[a].at[k], repl[a].at[kj], send_sems, recv_sems, idx, (px, py, c))
                out.append((mine, arriving))
        return out
    return copies


def _sibling_copies(n, halves):
    def copies(bufs, send_sems, recv_sems):
        x, y, c, _ = _mesh_place()
        out = []
        for a in range(n):
            src = bufs[a].at[1 - c] if halves else bufs[a]
            cp = _remote(src, bufs[n + a], send_sems, recv_sems, a, (x, y, 1 - c))
            out.append((cp, cp))
        return out
    return copies


def _forward_copies(bufs, send_sems, recv_sems):
    x, y, c, chips = _mesh_place()
    out = []
    for a, buf in enumerate(bufs):
        for j, (px, py) in enumerate(chips):
            kj = 2 * px + py
            mine = _remote(buf.at[kj, c], buf.at[kj, c], send_sems, recv_sems, 3 * a + j, (x, y, 1 - c))
            arriving = _remote(buf.at[kj, c], buf.at[kj, 1 - c], send_sems, recv_sems, 3 * a + j, (x, y, 1 - c))
            out.append((mine, arriving))
    return out


def _join_copies(bufs, send_sems, recv_sems):
    x, y, c, _ = _mesh_place()
    out = []
    for a, buf in enumerate(bufs):
        mine = _remote(buf.at[c], buf.at[c], send_sems, recv_sems, a, (x, y, 1 - c))
        arriving = _remote(buf.at[c], buf.at[1 - c], send_sems, recv_sems, a, (x, y, 1 - c))
        out.append((mine, arriving))
    return out


def _forward_to_sibling_call(bufs, name):
    n = len(bufs)

    def body(*refs):
        ins, outs = refs[:n], refs[n:2 * n]
        send_sems, recv_sems = refs[2 * n:]
        x, y, c, chips = _mesh_place()
        sibling = (x, y, 1 - c)
        sends = []
        for a in range(n):
            for j, (px, py) in enumerate(chips):
                kj = 2 * px + py
                sends.append(_remote(ins[a].at[kj, c], outs[a].at[kj, c], send_sems, recv_sems, 3 * a + j, sibling))
        for cp in sends:
            cp.start()
        for a in range(n):
            for j, (px, py) in enumerate(chips):
                kj = 2 * px + py
                _remote(ins[a].at[kj, c], outs[a].at[kj, 1 - c], send_sems, recv_sems, 3 * a + j, sibling).wait_recv()
        for cp in sends:
            cp.wait_send()

    return pl.pallas_call(
        body, name=name,
        in_specs=[ANY] * n, out_specs=[ANY] * n,
        out_shape=[jax.ShapeDtypeStruct(a.shape, a.dtype) for a in bufs],
        input_output_aliases={a: a for a in range(n)},
        scratch_shapes=[pltpu.SemaphoreType.DMA((3 * n,)), pltpu.SemaphoreType.DMA((3 * n,))],
    )(*bufs)


def _join_halves_call(bufs, name):
    n = len(bufs)

    def body(*refs):
        ins, outs = refs[:n], refs[n:2 * n]
        send_sems, recv_sems = refs[2 * n:]
        x, y, c, _ = _mesh_place()
        sibling = (x, y, 1 - c)
        copies = [_remote(ins[a].at[c], outs[a].at[c], send_sems, recv_sems, a, sibling) for a in range(n)]
        for cp in copies:
            cp.start()
        for a in range(n):
            _remote(ins[a].at[c], outs[a].at[1 - c], send_sems, recv_sems, a, sibling).wait_recv()
        for cp in copies:
            cp.wait_send()

    return pl.pallas_call(
        body, name=name,
        in_specs=[ANY] * n, out_specs=[ANY] * n,
        out_shape=[jax.ShapeDtypeStruct(a.shape, a.dtype) for a in bufs],
        input_output_aliases={a: a for a in range(n)},
        scratch_shapes=[pltpu.SemaphoreType.DMA((n,)), pltpu.SemaphoreType.DMA((n,))],
    )(*bufs)


def _pack(arrays, rows_multiple):
    flat = jnp.concatenate([a.reshape(-1) for a in arrays])
    per = LANES * rows_multiple
    padded = -(-flat.shape[0] // per) * per
    flat = jnp.pad(flat, (0, padded - flat.shape[0]))
    return flat.reshape(-1, LANES)


def _unpack(packed, shapes):
    flat = packed.reshape(-1)
    out, at = [], 0
    for shp in shapes:
        size = 1
        for dim in shp:
            size *= dim
        out.append(flat[at:at + size].reshape(shp))
        at += size
    return out


def _halves(a):
    return a.reshape((2, a.shape[0] // 2) + a.shape[1:])


def kernel(x, ln_mix_g, ln_mix_b, w_in, w_pool, pool_scale, conv_w, conv_b, w_rg_a, b_rg_a, w_rg_i, b_rg_i, rg_lambda, w_out, ln_ffn_g, ln_ffn_b, w_mlp_in, w_mlp_out, loss_target, m_ln_mix_g, m_ln_mix_b, m_w_in, m_w_pool, m_pool_scale, m_conv_w, m_conv_b, m_w_rg_a, m_b_rg_a, m_w_rg_i, m_b_rg_i, m_rg_lambda, m_w_out, m_ln_ffn_g, m_ln_ffn_b, m_w_mlp_in, m_w_mlp_out, v_ln_mix_g, v_ln_mix_b, v_w_in, v_w_pool, v_pool_scale, v_conv_w, v_conv_b, v_w_rg_a, v_b_rg_a, v_w_rg_i, v_b_rg_i, v_rg_lambda, v_w_out, v_ln_ffn_g, v_ln_ffn_b, v_w_mlp_in, v_w_mlp_out):
    weights = dict(ln_mix_g=ln_mix_g, ln_mix_b=ln_mix_b, w_in=w_in, w_pool=w_pool, pool_scale=pool_scale, conv_w=conv_w,
                   conv_b=conv_b, w_rg_a=w_rg_a, b_rg_a=b_rg_a, w_rg_i=w_rg_i, b_rg_i=b_rg_i, rg_lambda=rg_lambda,
                   w_out=w_out, ln_ffn_g=ln_ffn_g, ln_ffn_b=ln_ffn_b, w_mlp_in=w_mlp_in, w_mlp_out=w_mlp_out)
    m_in = dict(ln_mix_g=m_ln_mix_g, ln_mix_b=m_ln_mix_b, w_in=m_w_in, w_pool=m_w_pool, pool_scale=m_pool_scale,
                conv_w=m_conv_w, conv_b=m_conv_b, w_rg_a=m_w_rg_a, b_rg_a=m_b_rg_a, w_rg_i=m_w_rg_i, b_rg_i=m_b_rg_i,
                rg_lambda=m_rg_lambda, w_out=m_w_out, ln_ffn_g=m_ln_ffn_g, ln_ffn_b=m_ln_ffn_b, w_mlp_in=m_w_mlp_in,
                w_mlp_out=m_w_mlp_out)
    v_in = dict(ln_mix_g=v_ln_mix_g, ln_mix_b=v_ln_mix_b, w_in=v_w_in, w_pool=v_w_pool, pool_scale=v_pool_scale,
                conv_w=v_conv_w, conv_b=v_conv_b, w_rg_a=v_w_rg_a, b_rg_a=v_b_rg_a, w_rg_i=v_w_rg_i, b_rg_i=v_b_rg_i,
                rg_lambda=v_rg_lambda, w_out=v_w_out, ln_ffn_g=v_ln_ffn_g, ln_ffn_b=v_ln_ffn_b, w_mlp_in=v_w_mlp_in,
                w_mlp_out=v_w_mlp_out)
    names = list(weights)

    xs = x[0]
    tgt = loss_target[0]
    s, d = xs.shape
    p = c = d // 2
    pg = p // N_POOL_GROUPS
    core = lax.axis_index("c")
    shard = 2 * lax.axis_index("x") + lax.axis_index("y")

    idx = jnp.stack([core, shard]).astype(jnp.int32)
    small_shard = _pack([conv_w[0], b_rg_a[0], b_rg_i[0], rg_lambda[0]], 2 * SUBLANES)
    to_gather = [(w_in[0], BF16), (w_out[0], BF16), (w_mlp_in[0], BF16), (w_mlp_out[0], BF16),
                 (w_pool[0].reshape(-1, pg), BF16), (small_shard, F32)]

    def slot_view(i):
        a, dt = to_gather[i]
        sl = _cast_to_slot_call(a, idx, dt, f"gather_slot_{i}")
        return sl.reshape(N_CHIPS, 2, sl.shape[1] // 2, sl.shape[2])

    first, later = (0, 4, 5), (1, 2, 3)
    fly_a, sems_a, token_a = _start_copies_call(
        "gather_start_first", [slot_view(i) for i in first], [((0, 1, 2), _gather_copies, 3 * len(first))])
    fly_b, sems_b, g_token = _start_copies_call(
        "gather_start_later", [slot_view(i) for i in later],
        [((0,), _gather_copies, 3), ((1,), _gather_copies, 3), ((2,), _gather_copies, 3)], token_a)
    in_flight = {**dict(zip(first, fly_a)), **dict(zip(later, fly_b))}
    g_sems = [sems_a[0]] + list(sems_b)

    def arrive(which, group, after, tag):
        return _wait_copies_call(f"gather_wait_{tag}", [in_flight[w] for w in which], g_sems[group], _gather_copies, after)

    def pass_on(got, tag):
        flying, sems, token = _start_copies_call(
            f"gather_forward_start_{tag}", got, [(tuple(range(len(got))), _forward_copies, 3 * len(got))])
        return (flying, sems[0], tag), token

    def passed_on(state, after):
        flying, sems, tag = state
        return _wait_copies_call(f"gather_forward_wait_{tag}", flying, sems, _forward_copies, after)

    gathered = [None] * len(to_gather)
    xb = _cast_call(xs, g_token)
    gathered[0], gathered[4], gathered[5] = _forward_to_sibling_call(arrive(first, 0, xb, "w_in"), "gather_forward_w_in")
    w_in_f = gathered[0].reshape((N_CHIPS,) + w_in.shape[1:])
    w_pool_f = gathered[4].reshape(N_CHIPS, N_POOL_GROUPS, pg // N_CHIPS, pg).transpose(1, 0, 2, 3).reshape(N_POOL_GROUPS, pg, pg)
    c4 = c // N_CHIPS
    small_parts = [_unpack(gathered[5][k].reshape(-1, LANES), [(4, c4), (2, c4), (2, c4), (2, c4)]) for k in range(N_CHIPS)]
    conv_w_f = jnp.concatenate([sp_[0] for sp_ in small_parts], axis=1)
    b_a_f = jnp.concatenate([sp_[1] for sp_ in small_parts], axis=1)
    b_i_f = jnp.concatenate([sp_[2] for sp_ in small_parts], axis=1)
    lam_f = jnp.concatenate([sp_[3] for sp_ in small_parts], axis=1)
    wa_b = w_rg_a[0].astype(BF16)
    wi_b = w_rg_i[0].astype(BF16)

    proj = _proj_call(xb, w_in_f)
    xc = _conv_call(proj, conv_w_f, conv_b, c)
    fwd_w_out, token = pass_on(arrive((1,), 1, xc, "w_out"), "w_out")
    h_b, *gates_b = _scan_fwd_call(xc, wa_b[1], wi_b[1], b_a_f[1:2], b_i_f[1:2], lam_f[1:2], True, token)
    h_f, *gates_f = _scan_fwd_call(xc, wa_b[0], wi_b[0], b_a_f[0:1], b_i_f[0:1], lam_f[0:1], False, token)
    y, d_pool = _pool_combine_call(proj, h_f, h_b, w_pool_f, pool_scale, p)
    w_out_f = passed_on(fwd_w_out, y)[0].reshape(d, d)
    fwd_w1, token = pass_on(arrive((2,), 2, y, "w_mlp_in"), "w_mlp_in")
    xh1, x1b, rstd1 = _out_ln1_call(y, w_out_f, xs, ln_mix_g, ln_mix_b, token)
    w1_f = passed_on(fwd_w1, x1b)[0].reshape((N_CHIPS,) + w_mlp_in.shape[1:])
    first_half = _mlp_in_call(x1b, w1_f, g_token, None)
    fwd_w2, token = pass_on(arrive((3,), 3, first_half[0], "w_mlp_out"), "w_mlp_out")
    r_act, hsq = _mlp_in_call(x1b, w1_f, token, first_half)
    w2_f = passed_on(fwd_w2, hsq)[0].reshape(N_CHIPS * w_mlp_out.shape[1], d)
    dz2, dz2b, loss8, dg2, db2 = _mlp_out_ln2_call(hsq, w2_f, xh1, ln_mix_g, ln_mix_b, ln_ffn_g, ln_ffn_b, tgt)

    def start_siblings(grads, halves, tag, after=None):
        lands = [lax.empty(g.shape[1:] if halves else g.shape, g.dtype) for g in grads]
        copies = _sibling_copies(len(grads), halves)
        flying, sems, token = _start_copies_call(
            f"siblings_start_{tag}", list(grads) + lands, [(tuple(range(2 * len(grads))), copies, len(grads))], after)
        return (flying, sems[0], copies, len(grads), tag), token

    def finish_siblings(state, after):
        flying, sems, copies, n, tag = state
        got = _wait_copies_call(f"siblings_wait_{tag}", flying, sems, copies, after)
        return got[:n], got[n:]

    half_own = jnp.reshape(core, (1,)).astype(jnp.int32)
    half_sibling = 1 - half_own

    def chip_sum_of(a, b, row_sharded, tag, dep, overlapped):
        for_sibling = _half_grad_call(a, b, half_sibling, row_sharded, None, f"grad_{tag}_for_sibling", dep)
        state, token = start_siblings([for_sibling], False, tag)
        results = overlapped(token)
        _, (from_sibling,) = finish_siblings(state, results[0])
        return _half_grad_call(a, b, half_own, row_sharded, from_sibling, f"grad_{tag}", token), results

    def start_exchange(sums, n_repl, tag):
        n_sh = len(sums) - n_repl
        lands = [lax.empty((N_CHIPS - 1,) + a.shape[1:], a.dtype) for a in sums[:n_sh]]
        bufs = sums[:n_sh] + lands + sums[n_sh:]
        copies = _exchange_copies(n_sh, n_repl)
        flying, sems, token = _start_copies_call(
            f"reduce_start_{tag}", bufs, [(tuple(range(len(bufs))), copies, 3 * len(sums))])
        return (flying, sems[0], copies, n_sh, tag), token

    def finish_exchange(state, after):
        flying, sems, copies, n_sh, tag = state
        got = _wait_copies_call(f"reduce_wait_{tag}", flying, sems, copies, after)
        halves = []
        for a in range(n_sh):
            own, land = got[a], got[n_sh + a]
            cols = own.shape[-1]
            total = _sum_chips_call(own.reshape(N_CHIPS, -1, cols), land.reshape(N_CHIPS - 1, -1, cols), idx,
                                    f"reduce_sum_{tag}_{a}")
            halves.append(total.reshape((2,) + own.shape[1:]))
        for a, rp in enumerate(got[2 * n_sh:]):
            halves.append(_sum_chips_call(None, rp, idx, f"reduce_sum_{tag}_r{a}"))
        return halves

    def start_join(halves, tag):
        flying, sems, token = _start_copies_call(
            f"join_start_{tag}", halves, [(tuple(range(len(halves))), _join_copies, len(halves))])
        return (flying, sems[0], tag), token

    def finish_join(state, after):
        flying, sems, tag = state
        return _wait_copies_call(f"join_wait_{tag}", flying, sems, _join_copies, after)

    sum_w2, (dpre,) = chip_sum_of(hsq, dz2b, True, "w_mlp_out", g_token,
                                  lambda tok: (_dhsq_call(dz2b, w2_f, r_act, tok),))
    flying_w2, token = start_exchange([sum_w2], 0, "w2")
    sum_w1, (dz1, dz1b, dg1, db1) = chip_sum_of(
        x1b, dpre, False, "w_mlp_in", token,
        lambda tok: _dx1_ln1_bwd_call(dpre, w1_f, dz2, xh1, rstd1, ln_mix_g, tok))
    flying_w1, token = start_exchange([sum_w1], 0, "w1")

    def dy_and_mixer(tok):
        dy = _dy_call(dz1b, w_out_f, tok)
        return _mixer_bwd_call(dy, d_pool, proj, h_f, h_b, w_pool_f, pool_scale, p)

    sum_wout, (e_pool, dh, dgate, g_wpool, g_pscale8) = chip_sum_of(y, dz1b, True, "w_out", token, dy_and_mixer)
    flying_wout, token = start_exchange([sum_wout], 0, "w_out")
    dxc0, g_wa0, g_wi0, g_ba0, g_bi0, g_sp0 = _scan_bwd_call(
        xc, dh, h_f, gates_f, None, wa_b[0], wi_b[0], lam_f[0:1], False, token)
    dxc, g_wa1, g_wi1, g_ba1, g_bi1, g_sp1 = _scan_bwd_call(
        xc, dh, h_b, gates_b, dxc0, wa_b[1], wi_b[1], lam_f[1:2], True, token)
    dproj, g_cw8, g_cb8 = _dproj_call(e_pool, dxc, proj, dgate, conv_w_f, p)

    rowsum = lambda a8: jnp.sum(a8, axis=-2)
    g_lam = jnp.stack([rowsum(g_sp0), rowsum(g_sp1)]) * (-_sigmoid(-lam_f))
    small_grads = {
        "ln_mix_g": rowsum(dg1), "ln_mix_b": rowsum(db1), "ln_ffn_g": rowsum(dg2), "ln_ffn_b": rowsum(db2),
        "pool_scale": rowsum(g_pscale8), "conv_b": rowsum(g_cb8),
        "w_rg_a": jnp.stack([g_wa0, g_wa1]), "w_rg_i": jnp.stack([g_wi0, g_wi1]),
        "w_pool": g_wpool, "conv_w": rowsum(g_cw8),
        "b_rg_a": jnp.stack([rowsum(g_ba0), rowsum(g_ba1)]), "b_rg_i": jnp.stack([rowsum(g_bi0), rowsum(g_bi1)]),
        "rg_lambda": g_lam,
    }
    small_names = list(small_grads)
    small_shapes = [small_grads[nm].shape for nm in small_names]
    g_small = _halves(_pack([small_grads[nm] for nm in small_names], 2 * SUBLANES))
    sib_small, token = start_siblings([g_small], True, "small")
    flying_small = []

    def small_exchange_and_grad_x(tok):
        (mine,), (theirs,) = finish_siblings(sib_small, tok)
        small_sum = _add_half_call(mine, theirs, idx, "reduce_add_small")
        state, tok = start_exchange([small_sum], 1, "small")
        flying_small.append(state)
        return (_dx_call(dproj, w_in_f, dz1, tok),)

    sum_win, (grad_x,) = chip_sum_of(xb, dproj, False, "w_in", token, small_exchange_and_grad_x)
    flying_small = flying_small[0]
    flying_win, token = start_exchange([sum_win], 0, "w_in")

    grad_w, delta_w, new_m, new_v = {}, {}, {}, {}

    def adamw(nm, full):
        w2d = weights[nm][0]
        g2d = full.reshape(w2d.shape)
        go, dl, mn, vn = _adamw_call(g2d, w2d, m_in[nm][0], v_in[nm][0], f"adamw_{nm}")
        grad_w[nm], delta_w[nm], new_m[nm], new_v[nm] = go[None], dl[None], mn[None], vn[None]
        return vn

    join_w2, token = start_join(finish_exchange(flying_w2, token), "w2")
    join_w1, token = start_join(finish_exchange(flying_w1, token), "w1")
    join_wout, token = start_join(finish_exchange(flying_wout, token), "w_out")
    last = adamw("w_mlp_out", finish_join(join_w2, token)[0])
    last = adamw("w_mlp_in", finish_join(join_w1, last)[0])
    last = adamw("w_out", finish_join(join_wout, last)[0])

    small_joined = _join_halves_call(finish_exchange(flying_small, last), "reduce_join_small")[0]
    small_full = dict(zip(small_names, _unpack(small_joined.reshape(-1, LANES), small_shapes)))
    local = dict(small_full)
    local["w_pool"] = lax.dynamic_slice_in_dim(small_full["w_pool"], shard * (pg // N_CHIPS), pg // N_CHIPS, axis=1)
    for nm in ("conv_w", "b_rg_a", "b_rg_i", "rg_lambda"):
        local[nm] = lax.dynamic_slice_in_dim(small_full[nm], shard * c4, c4, axis=1)
    small_w_shapes = [weights[nm].shape for nm in small_names]
    g_pack = _pack([local[nm] for nm in small_names], SUBLANES)
    w_pack = _pack([weights[nm] for nm in small_names], SUBLANES)
    m_pack = _pack([m_in[nm] for nm in small_names], SUBLANES)
    v_pack = _pack([v_in[nm] for nm in small_names], SUBLANES)
    _, dl_p, mn_p, vn_p = _adamw_call(g_pack, w_pack, m_pack, v_pack, "adamw_small")
    for nm, gl, dl, mn, vn in zip(small_names, _unpack(g_pack, small_w_shapes), _unpack(dl_p, small_w_shapes),
                                  _unpack(mn_p, small_w_shapes), _unpack(vn_p, small_w_shapes)):
        grad_w[nm], delta_w[nm], new_m[nm], new_v[nm] = gl, dl, mn, vn
    adamw("w_in", _join_halves_call(finish_exchange(flying_win, vn_p), "reduce_join_w_in")[0])

    loss = lax.psum(jnp.sum(loss8) * (0.5 / d), ("x", "y", "c"))
    return (loss, grad_x[None], *[grad_w[nm] for nm in names], *[delta_w[nm] for nm in names],
            *[new_m[nm] for nm in names], *[new_v[nm] for nm in names])
```

```python
import jax
import jax.numpy as jnp
from jax import lax
from jax.experimental import pallas as pl
from jax.experimental.pallas import tpu as pltpu

F32 = jnp.float32
BF16 = jnp.bfloat16

N_CHIPS = 4
LANES = 128
SUBLANES = 8
LRU_HEAD = 128
N_POOL_GROUPS = 4
POOL_WINDOWS = (2, 4, 8, 16)
RG_C = 8.0
LN_EPS = 1e-5
ALPHA = 2.0 ** 0.25
ADAM_LR, ADAM_B1, ADAM_B2, ADAM_EPS, ADAM_WD, ADAM_STEP = 0.001, 0.9, 0.999, 1e-08, 0.01, 10
VMEM_LIMIT = 56 * 1024 * 1024
SEQ_TILE = 256
MM_TILE = 512
LN_MM_K = 2048
LN_UNROLL = 8
ELT_BLOCK_BYTES = 2 * 1024 * 1024
RESIDENT_OPERAND_BYTES = 16 * 1024 * 1024
MESH = pl.DeviceIdType.MESH
ANY = pl.BlockSpec(memory_space=pl.ANY)


def _params(*sem):
    return pltpu.CompilerParams(dimension_semantics=sem, vmem_limit_bytes=VMEM_LIMIT)


def _sigmoid(z):
    return 1.0 / (1.0 + jnp.exp(-z))


def _neg_expm1(z):
    series = -(z * (1.0 + z * (0.5 + z * (1.0 / 6.0 + z * (1.0 / 24.0)))))
    return jnp.where(z > -0.01, series, 1.0 - jnp.exp(z))


def _softplus(z):
    return jnp.maximum(z, 0.0) + jnp.log1p(jnp.exp(-jnp.abs(z)))


_GELU_K = 0.7978845608028654
_GELU_C = 0.044715


def _gelu_and_grad(u):
    t = jnp.tanh(_GELU_K * (u + _GELU_C * (u * u * u)))
    g = 0.5 * u * (1.0 + t)
    dg = 0.5 * (1.0 + t) + 0.5 * u * (1.0 - t * t) * (_GELU_K * (1.0 + 3.0 * _GELU_C * u * u))
    return g, dg


def _shift_rows(prv, cur, nxt, o, rows):
    if o == 0:
        return cur
    if o == SUBLANES:
        return nxt
    if o == -SUBLANES:
        return prv
    if o > 0:
        return pltpu.roll(jnp.where(rows >= o, cur, nxt), SUBLANES - o, 0)
    p = -o
    return pltpu.roll(jnp.where(rows < SUBLANES - p, cur, prv), p, 0)


def _neighbour_chunks(main_ref, prev_ref, next_ref, r0, t_rows, cols, first_tile, last_tile):
    cur = main_ref[pl.ds(r0, SUBLANES), cols]
    before = main_ref[pl.ds(pl.multiple_of(jnp.maximum(r0 - SUBLANES, 0), SUBLANES), SUBLANES), cols]
    after = main_ref[pl.ds(pl.multiple_of(jnp.minimum(r0 + SUBLANES, t_rows - SUBLANES), SUBLANES), SUBLANES), cols]
    halo_prev = jnp.where(first_tile, 0.0, prev_ref[:, cols])
    halo_next = jnp.where(last_tile, 0.0, next_ref[:, cols])
    prv = jnp.where(r0 == 0, halo_prev, before)
    nxt = jnp.where(r0 == t_rows - SUBLANES, halo_next, after)
    return prv, cur, nxt


def _halo_specs(t_rows, n_rows, width, col_block):
    per = t_rows // SUBLANES
    last = n_rows // SUBLANES - 1
    return [
        pl.BlockSpec((t_rows, width), lambda i: (i, col_block)),
        pl.BlockSpec((SUBLANES, width), lambda i: (jnp.maximum(i * per - 1, 0), col_block)),
        pl.BlockSpec((SUBLANES, width), lambda i: (jnp.minimum((i + 1) * per, last), col_block)),
    ]


def _chunk_loop(t_rows, fn, init=None, unroll=1, descending=False):
    span = SUBLANES * unroll

    def step(ci, carry):
        base = pl.multiple_of(((t_rows // span - 1 - ci) if descending else ci) * span, span)
        for u in range(unroll):
            carry = fn(base + ((unroll - 1 - u) if descending else u) * SUBLANES, carry)
        return carry
    return lax.fori_loop(0, t_rows // span, step, init)


def _scan_chunk(a, b, h_in, rows, reverse):
    for dist in (1, 2, 4):
        if reverse:
            keep = rows < SUBLANES - dist
            shift = SUBLANES - dist
        else:
            keep = rows >= dist
            shift = dist
        b = a * jnp.where(keep, pltpu.roll(b, shift, 0), 0.0) + b
        a = a * jnp.where(keep, pltpu.roll(a, shift, 0), 1.0)
    return a * h_in + b


def _cast_call(x, dep):
    s, d = x.shape
    tm = min(MM_TILE, s)

    def body(x_ref, dep_ref, o_ref):
        o_ref[...] = x_ref[...].astype(BF16)

    return pl.pallas_call(
        body, name="cast_x", grid=(s // tm,),
        in_specs=[pl.BlockSpec((tm, d), lambda i: (i, 0)), ANY],
        out_specs=pl.BlockSpec((tm, d), lambda i: (i, 0)),
        out_shape=jax.ShapeDtypeStruct((s, d), BF16),
        compiler_params=_params("arbitrary"),
    )(x, dep)


def _proj_call(xb, w_in):
    s, d = xb.shape
    n, _, e4 = w_in.shape
    tm = min(2 * MM_TILE, s)

    def body(x_ref, w_ref, proj_ref):
        proj_ref[...] = jnp.dot(x_ref[...], w_ref[...], preferred_element_type=F32)

    return pl.pallas_call(
        body, name="proj", grid=(s // tm, n),
        in_specs=[pl.BlockSpec((tm, d), lambda i, j: (i, 0)), pl.BlockSpec((None, d, e4), lambda i, j: (j, 0, 0))],
        out_specs=pl.BlockSpec((tm, e4), lambda i, j: (i, j)),
        out_shape=jax.ShapeDtypeStruct((s, n * e4), F32),
        compiler_params=_params("arbitrary", "arbitrary"),
    )(xb, w_in)


def _conv_call(proj, conv_w, conv_b, c):
    s = proj.shape[0]
    t = min(SEQ_TILE, s)
    n_tiles = s // t

    def body(u_ref, up_ref, un_ref, w_ref, b_ref, xc_ref):
        i = pl.program_id(0)
        rows = lax.broadcasted_iota(jnp.int32, (SUBLANES, c), 0)
        w = w_ref[...]
        b = b_ref[...]

        def chunk(r0, _):
            prv, cur, nxt = _neighbour_chunks(u_ref, up_ref, un_ref, r0, t, slice(None), i == 0, i == n_tiles - 1)
            acc = b + w[1:2] * cur
            acc += w[0:1] * _shift_rows(prv, cur, nxt, -1, rows)
            acc += w[2:3] * _shift_rows(prv, cur, nxt, 1, rows)
            acc += w[3:4] * _shift_rows(prv, cur, nxt, 2, rows)
            xc_ref[pl.ds(r0, SUBLANES), :] = acc

        _chunk_loop(t, chunk)

    return pl.pallas_call(
        body, name="conv_fwd", grid=(n_tiles,),
        in_specs=_halo_specs(t, s, c, 1) + [pl.BlockSpec((4, c), lambda i: (0, 0)), pl.BlockSpec((1, c), lambda i: (0, 0))],
        out_specs=pl.BlockSpec((t, c), lambda i: (i, 0)),
        out_shape=jax.ShapeDtypeStruct((s, c), F32),
        compiler_params=_params("arbitrary"),
    )(proj, proj, proj, conv_w, conv_b)


def _gate_matmuls(xc_ref, wa_ref, wi_ref, pr_s, pi_s, heads):
    for h in range(heads):
        cs = pl.ds(h * LRU_HEAD, LRU_HEAD)
        xb = xc_ref[:, cs].astype(BF16)
        pr_s[:, cs] = jnp.dot(xb, wa_ref[h], preferred_element_type=F32)
        pi_s[:, cs] = jnp.dot(xb, wi_ref[h], preferred_element_type=F32)


def _rg_gates(pr, pi, ba, bi, sp):
    r = _sigmoid(pr + ba)
    ig = _sigmoid(pi + bi)
    log_a = (-RG_C * r) * sp
    a = jnp.exp(log_a)
    mult = jnp.sqrt(_neg_expm1(2.0 * log_a))
    return r, ig, a, mult


def _scan_fwd_call(xc, wa, wi, ba, bi, lam, reverse, dep):
    s, c = xc.shape
    heads = c // LRU_HEAD
    t = min(SEQ_TILE, s)
    n_tiles = s // t
    tile = (lambda i: (n_tiles - 1 - i, 0)) if reverse else (lambda i: (i, 0))
    whole2 = lambda i: (0, 0)
    whole3 = lambda i: (0, 0, 0)

    def body(xc_ref, wa_ref, wi_ref, ba_ref, bi_ref, lam_ref, dep_ref, h_ref, r_ref, ig_ref, a_ref, mult_ref,
             pr_s, pi_s, carry_s):
        @pl.when(pl.program_id(0) == 0)
        def _():
            carry_s[...] = jnp.zeros_like(carry_s)

        _gate_matmuls(xc_ref, wa_ref, wi_ref, pr_s, pi_s, heads)
        ba_v, bi_v = ba_ref[...], bi_ref[...]
        sp = _softplus(-lam_ref[...])

        rows = lax.broadcasted_iota(jnp.int32, (SUBLANES, c), 0)

        def chunk(r0, h_in):
            rs = pl.ds(r0, SUBLANES)
            r, ig, a, mult = _rg_gates(pr_s[rs, :], pi_s[rs, :], ba_v, bi_v, sp)
            r_ref[rs, :] = r
            ig_ref[rs, :] = ig
            a_ref[rs, :] = a
            mult_ref[rs, :] = mult
            h = _scan_chunk(a, mult * ig * xc_ref[rs, :], h_in, rows, reverse)
            h_ref[rs, :] = h
            return h[0:1, :] if reverse else h[SUBLANES - 1:SUBLANES, :]

        carry_s[...] = _chunk_loop(t, chunk, carry_s[...], unroll=2, descending=reverse)

    return pl.pallas_call(
        body, name="scan_fwd_rev" if reverse else "scan_fwd", grid=(n_tiles,),
        in_specs=[pl.BlockSpec((t, c), tile),
                  pl.BlockSpec((heads, LRU_HEAD, LRU_HEAD), whole3), pl.BlockSpec((heads, LRU_HEAD, LRU_HEAD), whole3),
                  pl.BlockSpec((1, c), whole2), pl.BlockSpec((1, c), whole2), pl.BlockSpec((1, c), whole2), ANY],
        out_specs=[pl.BlockSpec((t, c), tile)] * 5,
        out_shape=[jax.ShapeDtypeStruct((s, c), F32)] * 5,
        scratch_shapes=[pltpu.VMEM((t, c), F32), pltpu.VMEM((t, c), F32), pltpu.VMEM((1, c), F32)],
        compiler_params=_params("arbitrary"),
    )(xc, wa, wi, ba, bi, lam, dep)


def _window_counts(r0, tile_idx, t_rows, n_rows, half, shape):
    pos = tile_idx * t_rows + r0 + lax.broadcasted_iota(jnp.int32, shape, 0)
    hi = jnp.minimum(pos + half, n_rows)
    lo = jnp.maximum(pos - half, 0)
    return (hi - lo).astype(F32)


def _pool_combine_call(proj, h_f, h_b, w_pool, pool_scale, p):
    s = proj.shape[0]
    c = h_f.shape[1]
    pg = p // N_POOL_GROUPS
    t = min(SEQ_TILE, s)
    n_tiles = s // t

    def body(u_ref, up_ref, un_ref, gate_ref, hf_ref, hb_ref, wp_ref, sc_ref, y_ref, d_ref, d_s, yr_s):
        i = pl.program_id(0)
        rows = lax.broadcasted_iota(jnp.int32, (SUBLANES, pg), 0)

        def chunk(r0, _):
            rs = pl.ds(r0, SUBLANES)
            for g, w in enumerate(POOL_WINDOWS):
                cols = pl.ds(g * pg, pg)
                prv, cur, nxt = _neighbour_chunks(u_ref, up_ref, un_ref, r0, t, cols, i == 0, i == n_tiles - 1)
                tot = cur
                for o in range(-(w // 2), w // 2):
                    if o != 0:
                        tot = tot + _shift_rows(prv, cur, nxt, o, rows)
                cnt = _window_counts(r0, i, t, s, w // 2, (SUBLANES, pg))
                d_s[rs, cols] = tot / cnt - cur
            gate, _ = _gelu_and_grad(gate_ref[rs, :])
            yr_s[rs, :] = (hf_ref[rs, :] + hb_ref[rs, :]) * gate

        _chunk_loop(t, chunk)
        y_ref[:, pl.ds(p, c)] = yr_s[...].astype(BF16)
        d_ref[...] = d_s[...].astype(BF16)
        for g in range(N_POOL_GROUPS):
            cols = pl.ds(g * pg, pg)
            out = jnp.dot(d_s[:, cols].astype(BF16), wp_ref[g], preferred_element_type=F32)
            y_ref[:, cols] = (out * sc_ref[:, cols]).astype(BF16)

    return pl.pallas_call(
        body, name="pool_combine", grid=(n_tiles,),
        in_specs=_halo_specs(t, s, p, 0) + [
            pl.BlockSpec((t, c), lambda i: (i, 2)),
            pl.BlockSpec((t, c), lambda i: (i, 0)), pl.BlockSpec((t, c), lambda i: (i, 0)),
            pl.BlockSpec((N_POOL_GROUPS, pg, pg), lambda i: (0, 0, 0)), pl.BlockSpec((1, p), lambda i: (0, 0))],
        out_specs=[pl.BlockSpec((t, p + c), lambda i: (i, 0)), pl.BlockSpec((t, p), lambda i: (i, 0))],
        out_shape=[jax.ShapeDtypeStruct((s, p + c), BF16), jax.ShapeDtypeStruct((s, p), BF16)],
        scratch_shapes=[pltpu.VMEM((t, p), F32), pltpu.VMEM((t, c), F32)],
        compiler_params=_params("arbitrary"),
    )(proj, proj, proj, proj, h_f, h_b, w_pool, pool_scale)


def _layer_norm_rows(z, g, b):
    mu = jnp.mean(z, axis=-1, keepdims=True)
    zc = z - mu
    var = jnp.mean(zc * zc, axis=-1, keepdims=True)
    rstd = lax.rsqrt(var + LN_EPS)
    xh = zc * rstd
    return xh, rstd, xh * g + b


def _layer_norm_bwd_rows(dx, xh, rstd, g):
    dxh = dx * g
    m1 = jnp.mean(dxh, axis=-1, keepdims=True)
    m2 = jnp.mean(dxh * xh, axis=-1, keepdims=True)
    return rstd * (dxh - m1 - xh * m2)


def _out_ln1_call(y, w_out, x, g1, b1, dep):
    s, d = x.shape
    tm = min(SEQ_TILE, s)

    def body(y_ref, w_ref, x_ref, g_ref, b_ref, dep_ref, xh_ref, x1b_ref, rstd_ref, acc_s, x1_s):
        acc_s[...] = jnp.dot(y_ref[...], w_ref[...], preferred_element_type=F32)
        g, b = g_ref[...], b_ref[...]

        def chunk(r0, _):
            rs = pl.ds(r0, SUBLANES)
            xh, rstd, x1 = _layer_norm_rows(ALPHA * x_ref[rs, :] + acc_s[rs, :], g, b)
            xh_ref[rs, :] = xh
            x1_s[rs, :] = x1
            rstd_ref[rs, :] = rstd

        _chunk_loop(tm, chunk, unroll=LN_UNROLL)
        x1b_ref[...] = x1_s[...].astype(BF16)

    return pl.pallas_call(
        body, name="out_ln1", grid=(s // tm,),
        in_specs=[pl.BlockSpec((tm, d), lambda i: (i, 0)), pl.BlockSpec((d, d), lambda i: (0, 0)),
                  pl.BlockSpec((tm, d), lambda i: (i, 0)),
                  pl.BlockSpec((1, d), lambda i: (0, 0)), pl.BlockSpec((1, d), lambda i: (0, 0)), ANY],
        out_specs=[pl.BlockSpec((tm, d), lambda i: (i, 0)), pl.BlockSpec((tm, d), lambda i: (i, 0)),
                   pl.BlockSpec((tm, 1), lambda i: (i, 0))],
        out_shape=[jax.ShapeDtypeStruct((s, d), F32), jax.ShapeDtypeStruct((s, d), BF16), jax.ShapeDtypeStruct((s, 1), F32)],
        scratch_shapes=[pltpu.VMEM((tm, d), F32), pltpu.VMEM((tm, d), F32)],
        compiler_params=_params("arbitrary"),
    )(y, w_out, x, g1, b1, dep)


def _mlp_in_call(x1b, w1, dep, done):
    s, d = x1b.shape
    n, _, f4 = w1.shape
    tm = min(MM_TILE, s)
    tn = min(1024, f4)
    per = f4 // tn
    blocks = n * per // 2
    first = 0 if done is None else blocks
    extra = [] if done is None else list(done)

    def body(x_ref, w_ref, dep_ref, *rest):
        r_ref, q_ref = rest[-2:]
        r = jnp.maximum(jnp.dot(x_ref[...], w_ref[...], preferred_element_type=F32), 0.0)
        r_ref[...] = r.astype(BF16)
        q_ref[...] = (r * r).astype(BF16)

    out_spec = pl.BlockSpec((tm, tn), lambda j, i: (i, first + j))
    return pl.pallas_call(
        body, name="mlp_in" if done is None else "mlp_in_rest", grid=(blocks, s // tm),
        in_specs=[pl.BlockSpec((tm, d), lambda j, i: (i, 0)),
                  pl.BlockSpec((None, d, tn), lambda j, i: ((first + j) // per, 0, (first + j) % per)), ANY] + [ANY] * len(extra),
        out_specs=[out_spec, out_spec],
        out_shape=[jax.ShapeDtypeStruct((s, n * f4), BF16), jax.ShapeDtypeStruct((s, n * f4), BF16)],
        input_output_aliases={3: 0, 4: 1} if extra else {},
        compiler_params=_params("arbitrary", "arbitrary"),
    )(x1b, w1, dep, *extra)


def _mlp_out_ln2_call(hsq, w2, xh1, g1, b1, g2, b2, target):
    s, f = hsq.shape
    d = w2.shape[1]
    tm = min(MM_TILE, s)
    tk = min(LN_MM_K, f)
    nk = f // tk

    def body(h_ref, w_ref, xh1_ref, g1_ref, b1_ref, g2_ref, b2_ref, t_ref,
             dz_ref, dzb_ref, loss_ref, dg_ref, db_ref, acc_s):
        i, k = pl.program_id(0), pl.program_id(1)

        @pl.when((i == 0) & (k == 0))
        def _():
            loss_ref[...] = jnp.zeros_like(loss_ref)
            dg_ref[...] = jnp.zeros_like(dg_ref)
            db_ref[...] = jnp.zeros_like(db_ref)

        @pl.when(k == 0)
        def _():
            acc_s[...] = jnp.zeros_like(acc_s)

        acc_s[...] += jnp.dot(h_ref[...], w_ref[...], preferred_element_type=F32)

        @pl.when(k == nk - 1)
        def _():
            g1, b1, g2, b2 = g1_ref[...], b1_ref[...], g2_ref[...], b2_ref[...]

            def chunk(r0, _):
                rs = pl.ds(r0, SUBLANES)
                x1 = xh1_ref[rs, :] * g1 + b1
                xh2, rstd, x2 = _layer_norm_rows(ALPHA * x1 + acc_s[rs, :], g2, b2)
                diff = x2 - t_ref[rs, :]
                loss_ref[...] += diff * diff
                dx2 = diff * (1.0 / d)
                dg_ref[...] += dx2 * xh2
                db_ref[...] += dx2
                dz = _layer_norm_bwd_rows(dx2, xh2, rstd, g2)
                dz_ref[rs, :] = dz

            _chunk_loop(tm, chunk, unroll=LN_UNROLL)
            dzb_ref[...] = dz_ref[...].astype(BF16)

    row = lambda i, k: (i, 0)
    vec = lambda i, k: (0, 0)
    return pl.pallas_call(
        body, name="mlp_out_ln2", grid=(s // tm, nk),
        in_specs=[pl.BlockSpec((tm, tk), lambda i, k: (i, k)), pl.BlockSpec((tk, d), lambda i, k: (k, 0)),
                  pl.BlockSpec((tm, d), row), pl.BlockSpec((1, d), vec), pl.BlockSpec((1, d), vec),
                  pl.BlockSpec((1, d), vec), pl.BlockSpec((1, d), vec), pl.BlockSpec((tm, d), row)],
        out_specs=[pl.BlockSpec((tm, d), row), pl.BlockSpec((tm, d), row),
                   pl.BlockSpec((SUBLANES, d), vec), pl.BlockSpec((SUBLANES, d), vec), pl.BlockSpec((SUBLANES, d), vec)],
        out_shape=[jax.ShapeDtypeStruct((s, d), F32), jax.ShapeDtypeStruct((s, d), BF16),
                   jax.ShapeDtypeStruct((SUBLANES, d), F32), jax.ShapeDtypeStruct((SUBLANES, d), F32),
                   jax.ShapeDtypeStruct((SUBLANES, d), F32)],
        scratch_shapes=[pltpu.VMEM((tm, d), F32)],
        compiler_params=_params("arbitrary", "arbitrary"),
    )(hsq, w2, xh1, g1, b1, g2, b2, target)


def _half_grad_call(a, b, half, row_sharded, init, name, dep):
    s, m = a.shape
    n = b.shape[1]
    if row_sharded:
        rows, cols = m // (2 * N_CHIPS), n
        tm = min(1024, rows)
        per = rows // tm
        tn = min(1024, cols)
        n_i, n_j = N_CHIPS * per, cols // tn
        a_block = lambda i, h: ((i // per) * 2 + h) * per + i % per
        out_block = lambda i, j: (i // per, i % per, j)
    else:
        rows, cols = m // 2, n // N_CHIPS
        tm = min(1024, rows)
        per = rows // tm
        tn = cols if cols % 1024 else 1024
        per_n = cols // tn
        n_i, n_j = per, N_CHIPS * per_n
        a_block = lambda i, h: h * per + i
        out_block = lambda i, j: (j // per_n, i, j % per_n)
    tk = min(2048, s)
    if row_sharded and tm < 1024 and s * n * 2 <= RESIDENT_OPERAND_BYTES:
        tk, tn, n_j = s, n, 1
    has_init = init is not None

    def body(half_ref, a_ref, b_ref, *rest):
        o_ref = rest[-1]

        @pl.when(pl.program_id(2) == 0)
        def _():
            o_ref[...] = rest[0][...] if has_init else jnp.zeros_like(o_ref)

        o_ref[...] += lax.dot_general(a_ref[...], b_ref[...], (((0,), (0,)), ((), ())), preferred_element_type=F32)

    out_spec = pl.BlockSpec((None, tm, tn), lambda i, j, k, h: out_block(i, j))
    in_specs = [pl.BlockSpec((tk, tm), lambda i, j, k, h: (k, a_block(i, h[0]))),
                pl.BlockSpec((tk, tn), lambda i, j, k, h: (k, j))]
    args = [a, b]
    if has_init:
        in_specs.append(out_spec)
        args.append(init)
    in_specs.append(ANY)
    args.append(dep)
    return pl.pallas_call(
        body, name=name,
        grid_spec=pltpu.PrefetchScalarGridSpec(num_scalar_prefetch=1, grid=(n_i, n_j, s // tk), in_specs=in_specs,
                                               out_specs=out_spec),
        out_shape=jax.ShapeDtypeStruct((N_CHIPS, rows, cols), F32),
        compiler_params=_params("arbitrary", "arbitrary", "arbitrary"),
    )(half, *args)


def _dhsq_call(dzb, w2, r, dep):
    s, d = dzb.shape
    f = w2.shape[0]
    tm = min(MM_TILE, s)
    tn = min(1024, f)

    def body(dz_ref, w_ref, r_ref, dep_ref, o_ref):
        dh = lax.dot_general(dz_ref[...], w_ref[...], (((1,), (1,)), ((), ())), preferred_element_type=F32)
        o_ref[...] = (dh * (2.0 * r_ref[...].astype(F32))).astype(BF16)

    return pl.pallas_call(
        body, name="mlp_dpre", grid=(f // tn, s // tm),
        in_specs=[pl.BlockSpec((tm, d), lambda j, i: (i, 0)), pl.BlockSpec((tn, d), lambda j, i: (j, 0)),
                  pl.BlockSpec((tm, tn), lambda j, i: (i, j)), ANY],
        out_specs=pl.BlockSpec((tm, tn), lambda j, i: (i, j)),
        out_shape=jax.ShapeDtypeStruct((s, f), BF16),
        compiler_params=_params("arbitrary", "arbitrary"),
    )(dzb, w2, r, dep)


def _dx1_ln1_bwd_call(dpre, w1, dz2, xh1, rstd1, g1, dep):
    s, f = dpre.shape
    n, d, f4 = w1.shape
    tm = min(MM_TILE, s)
    tk = min(LN_MM_K, f4)
    per = f4 // tk
    nk = n * per

    def body(dp_ref, w_ref, dz2_ref, xh_ref, rstd_ref, g_ref, dep_ref, dz_ref, dzb_ref, dg_ref, db_ref, acc_s):
        i, k = pl.program_id(0), pl.program_id(1)

        @pl.when((i == 0) & (k == 0))
        def _():
            dg_ref[...] = jnp.zeros_like(dg_ref)
            db_ref[...] = jnp.zeros_like(db_ref)

        @pl.when(k == 0)
        def _():
            acc_s[...] = jnp.zeros_like(acc_s)

        acc_s[...] += lax.dot_general(dp_ref[...], w_ref[...], (((1,), (1,)), ((), ())), preferred_element_type=F32)

        @pl.when(k == nk - 1)
        def _():
            g = g_ref[...]

            def chunk(r0, _):
                rs = pl.ds(r0, SUBLANES)
                dx1 = acc_s[rs, :] + ALPHA * dz2_ref[rs, :]
                xh = xh_ref[rs, :]
                dg_ref[...] += dx1 * xh
                db_ref[...] += dx1
                dz = _layer_norm_bwd_rows(dx1, xh, rstd_ref[rs, :], g)
                dz_ref[rs, :] = dz

            _chunk_loop(tm, chunk, unroll=LN_UNROLL)
            dzb_ref[...] = dz_ref[...].astype(BF16)

    row = lambda i, k: (i, 0)
    vec = lambda i, k: (0, 0)
    return pl.pallas_call(
        body, name="dx1_ln1_bwd", grid=(s // tm, nk),
        in_specs=[pl.BlockSpec((tm, tk), lambda i, k: (i, k)),
                  pl.BlockSpec((None, d, tk), lambda i, k: (k // per, 0, k % per)),
                  pl.BlockSpec((tm, d), row), pl.BlockSpec((tm, d), row), pl.BlockSpec((tm, 1), row),
                  pl.BlockSpec((1, d), vec), ANY],
        out_specs=[pl.BlockSpec((tm, d), row), pl.BlockSpec((tm, d), row),
                   pl.BlockSpec((SUBLANES, d), vec), pl.BlockSpec((SUBLANES, d), vec)],
        out_shape=[jax.ShapeDtypeStruct((s, d), F32), jax.ShapeDtypeStruct((s, d), BF16),
                   jax.ShapeDtypeStruct((SUBLANES, d), F32), jax.ShapeDtypeStruct((SUBLANES, d), F32)],
        scratch_shapes=[pltpu.VMEM((tm, d), F32)],
        compiler_params=_params("arbitrary", "arbitrary"),
    )(dpre, w1, dz2, xh1, rstd1, g1, dep)


def _dy_call(dzb, w_out, dep):
    s, d = dzb.shape
    e = w_out.shape[0]
    tm = min(MM_TILE, s)

    def body(dz_ref, w_ref, dep_ref, o_ref):
        o_ref[...] = lax.dot_general(dz_ref[...], w_ref[...], (((1,), (1,)), ((), ())), preferred_element_type=F32)

    return pl.pallas_call(
        body, name="dy", grid=(s // tm,),
        in_specs=[pl.BlockSpec((tm, d), lambda i: (i, 0)), pl.BlockSpec((e, d), lambda i: (0, 0)), ANY],
        out_specs=pl.BlockSpec((tm, e), lambda i: (i, 0)),
        out_shape=jax.ShapeDtypeStruct((s, e), F32),
        compiler_params=_params("arbitrary"),
    )(dzb, w_out, dep)


def _mixer_bwd_call(dy, d_pool, proj, h_f, h_b, w_pool, pool_scale, p):
    s = dy.shape[0]
    c = h_f.shape[1]
    pg = p // N_POOL_GROUPS
    t = min(SEQ_TILE, s)
    n_tiles = s // t

    def body(dyp_ref, dyr_ref, d_ref, gate_ref, hf_ref, hb_ref, wp_ref, sc_ref,
             e_ref, dh_ref, dgate_ref, dwp_ref, dsc_ref, dd_s):
        i = pl.program_id(0)

        @pl.when(i == 0)
        def _():
            dwp_ref[...] = jnp.zeros_like(dwp_ref)
            dsc_ref[...] = jnp.zeros_like(dsc_ref)

        for g in range(N_POOL_GROUPS):
            cols = pl.ds(g * pg, pg)
            dg = d_ref[:, cols]
            out = jnp.dot(dg, wp_ref[g], preferred_element_type=F32)
            dyp = dyp_ref[:, cols]
            prod = dyp * out
            dsc_ref[:, cols] += jnp.sum(prod.reshape(t // SUBLANES, SUBLANES, pg), axis=0)
            dout = (dyp * sc_ref[:, cols]).astype(BF16)
            dwp_ref[g] += lax.dot_general(dg, dout, (((0,), (0,)), ((), ())), preferred_element_type=F32)
            dd_s[:, cols] = lax.dot_general(dout, wp_ref[g], (((1,), (1,)), ((), ())), preferred_element_type=F32)

        def chunk(r0, _):
            rs = pl.ds(r0, SUBLANES)
            for g, w in enumerate(POOL_WINDOWS):
                cols = pl.ds(g * pg, pg)
                cnt = _window_counts(r0, i, t, s, w // 2, (SUBLANES, pg))
                e_ref[rs, cols] = dd_s[rs, cols] / cnt
            gate, dgate = _gelu_and_grad(gate_ref[rs, :])
            dyr = dyr_ref[rs, :]
            dh_ref[rs, :] = dyr * gate
            dd_s[rs, :] = dyr * (hf_ref[rs, :] + hb_ref[rs, :]) * dgate

        _chunk_loop(t, chunk)
        dgate_ref[...] = dd_s[...].astype(BF16)

    tile = lambda i: (i, 0)
    return pl.pallas_call(
        body, name="mixer_bwd", grid=(n_tiles,),
        in_specs=[pl.BlockSpec((t, p), tile), pl.BlockSpec((t, c), lambda i: (i, 1)), pl.BlockSpec((t, p), tile),
                  pl.BlockSpec((t, c), lambda i: (i, 2)), pl.BlockSpec((t, c), tile), pl.BlockSpec((t, c), tile),
                  pl.BlockSpec((N_POOL_GROUPS, pg, pg), lambda i: (0, 0, 0)), pl.BlockSpec((1, p), lambda i: (0, 0))],
        out_specs=[pl.BlockSpec((t, p), tile), pl.BlockSpec((t, c), tile), pl.BlockSpec((t, c), tile),
                   pl.BlockSpec((N_POOL_GROUPS, pg, pg), lambda i: (0, 0, 0)), pl.BlockSpec((SUBLANES, p), lambda i: (0, 0))],
        out_shape=[jax.ShapeDtypeStruct((s, p), F32), jax.ShapeDtypeStruct((s, c), F32), jax.ShapeDtypeStruct((s, c), BF16),
                   jax.ShapeDtypeStruct((N_POOL_GROUPS, pg, pg), F32), jax.ShapeDtypeStruct((SUBLANES, p), F32)],
        scratch_shapes=[pltpu.VMEM((t, p), F32)],
        compiler_params=_params("arbitrary"),
    )(dy, dy, d_pool, proj, h_f, h_b, w_pool, pool_scale)


def _scan_bwd_call(xc, dh, h_dir, gates, dxc_prev, wa, wi, lam, reverse, dep):
    s, c = xc.shape
    heads = c // LRU_HEAD
    t = min(SEQ_TILE, s)
    n_tiles = s // t
    per = t // SUBLANES
    last_blk = s // SUBLANES - 1
    tile = (lambda i: (i, 0)) if reverse else (lambda i: (n_tiles - 1 - i, 0))
    if reverse:
        halo = lambda i: (jnp.minimum((i + 1) * per, last_blk), 0)
    else:
        halo = lambda i: (jnp.maximum((n_tiles - 1 - i) * per - 1, 0), 0)
    whole2 = lambda i: (0, 0)
    whole3 = lambda i: (0, 0, 0)
    has_prev = dxc_prev is not None
    n_in = 11 + int(has_prev) + 1

    def body(*refs):
        xc_ref, dh_ref, h_ref, hh_ref, r_ref, ig_ref, a_ref, mult_ref = refs[:8]
        prev_ref = refs[8] if has_prev else None
        wa_ref, wi_ref, lam_ref = refs[n_in - 4:n_in - 1]
        dxc_ref, dwa_ref, dwi_ref, dba_ref, dbi_ref, dsp_ref = refs[n_in:n_in + 6]
        pr_s, pi_s, carry_s = refs[n_in + 6:]
        step = pl.program_id(0)
        tile_idx = step if reverse else n_tiles - 1 - step

        @pl.when(step == 0)
        def _():
            carry_s[...] = jnp.zeros_like(carry_s)
            dwa_ref[...] = jnp.zeros_like(dwa_ref)
            dwi_ref[...] = jnp.zeros_like(dwi_ref)
            dba_ref[...] = jnp.zeros_like(dba_ref)
            dbi_ref[...] = jnp.zeros_like(dbi_ref)
            dsp_ref[...] = jnp.zeros_like(dsp_ref)

        sp = _softplus(-lam_ref[...])
        rows = lax.broadcasted_iota(jnp.int32, (SUBLANES, c), 0)

        def chunk(r0, u_in):
            rs = pl.ds(r0, SUBLANES)
            xcv = xc_ref[rs, :]
            r, ig, a, mult = r_ref[rs, :], ig_ref[rs, :], a_ref[rs, :], mult_ref[rs, :]
            dhv = dh_ref[rs, :]
            u = _scan_chunk(a, a * dhv, u_in, rows, not reverse)
            if reverse:
                gt = dhv + jnp.where(rows >= 1, pltpu.roll(u, 1, 0), u_in)
                u_out = u[SUBLANES - 1:SUBLANES, :]
            else:
                gt = dhv + jnp.where(rows < SUBLANES - 1, pltpu.roll(u, SUBLANES - 1, 0), u_in)
                u_out = u[0:1, :]
            cur = h_ref[rs, :]
            if reverse:
                after = h_ref[pl.ds(pl.multiple_of(jnp.minimum(r0 + SUBLANES, t - SUBLANES), SUBLANES), SUBLANES), :]
                edge = jnp.where(tile_idx == n_tiles - 1, 0.0, hh_ref[...])
                nxt = jnp.where(r0 == t - SUBLANES, edge, after)
                hs = _shift_rows(cur, cur, nxt, 1, rows)
            else:
                before = h_ref[pl.ds(pl.multiple_of(jnp.maximum(r0 - SUBLANES, 0), SUBLANES), SUBLANES), :]
                edge = jnp.where(tile_idx == 0, 0.0, hh_ref[...])
                prv = jnp.where(r0 == 0, edge, before)
                hs = _shift_rows(prv, cur, cur, -1, rows)
            gx = gt * xcv
            dmult = gx * ig
            di = gx * mult
            dlog_a = (gt * hs) * a - dmult * (a * a) / mult
            dr = dlog_a * (-RG_C * sp)
            dsp_ref[...] += dlog_a * (-RG_C * r)
            dpr = dr * r * (1.0 - r)
            dpi = di * ig * (1.0 - ig)
            dba_ref[...] += dpr
            dbi_ref[...] += dpi
            direct = gt * mult * ig
            if has_prev:
                direct = direct + prev_ref[rs, :]
            dxc_ref[rs, :] = direct
            pr_s[rs, :] = dpr
            pi_s[rs, :] = dpi
            return u_out

        carry_s[...] = _chunk_loop(t, chunk, carry_s[...], unroll=2, descending=not reverse)

        for h in range(heads):
            cs = pl.ds(h * LRU_HEAD, LRU_HEAD)
            xb = xc_ref[:, cs].astype(BF16)
            dprb = pr_s[:, cs].astype(BF16)
            dpib = pi_s[:, cs].astype(BF16)
            dwa_ref[h] += lax.dot_general(xb, dprb, (((0,), (0,)), ((), ())), preferred_element_type=F32)
            dwi_ref[h] += lax.dot_general(xb, dpib, (((0,), (0,)), ((), ())), preferred_element_type=F32)
            dxc_ref[:, cs] += (
                lax.dot_general(dprb, wa_ref[h], (((1,), (1,)), ((), ())), preferred_element_type=F32)
                + lax.dot_general(dpib, wi_ref[h], (((1,), (1,)), ((), ())), preferred_element_type=F32))

    tile_spec = pl.BlockSpec((t, c), tile)
    in_specs = [tile_spec, tile_spec, tile_spec, pl.BlockSpec((SUBLANES, c), halo)] + [tile_spec] * 4
    args = [xc, dh, h_dir, h_dir, *gates]
    if has_prev:
        in_specs.append(tile_spec)
        args.append(dxc_prev)
    in_specs += [pl.BlockSpec((heads, LRU_HEAD, LRU_HEAD), whole3), pl.BlockSpec((heads, LRU_HEAD, LRU_HEAD), whole3),
                 pl.BlockSpec((1, c), whole2), ANY]
    args += [wa, wi, lam, dep]
    assert len(args) == n_in
    return pl.pallas_call(
        body, name="scan_bwd_rev" if reverse else "scan_bwd", grid=(n_tiles,),
        in_specs=in_specs,
        out_specs=[tile_spec,
                   pl.BlockSpec((heads, LRU_HEAD, LRU_HEAD), whole3), pl.BlockSpec((heads, LRU_HEAD, LRU_HEAD), whole3),
                   pl.BlockSpec((SUBLANES, c), whole2), pl.BlockSpec((SUBLANES, c), whole2), pl.BlockSpec((SUBLANES, c), whole2)],
        out_shape=[jax.ShapeDtypeStruct((s, c), F32),
                   jax.ShapeDtypeStruct((heads, LRU_HEAD, LRU_HEAD), F32), jax.ShapeDtypeStruct((heads, LRU_HEAD, LRU_HEAD), F32),
                   jax.ShapeDtypeStruct((SUBLANES, c), F32), jax.ShapeDtypeStruct((SUBLANES, c), F32),
                   jax.ShapeDtypeStruct((SUBLANES, c), F32)],
        scratch_shapes=[pltpu.VMEM((t, c), F32), pltpu.VMEM((t, c), F32), pltpu.VMEM((1, c), F32)],
        compiler_params=_params("arbitrary"),
    )(*args)


def _dproj_call(e_pool, dxc, proj, dgate, conv_w, p):
    s, c = dxc.shape
    pg = p // N_POOL_GROUPS
    t = min(SEQ_TILE, s)
    n_tiles = s // t

    def body(e_ref, ep_ref, en_ref, dx_ref, dxp_ref, dxn_ref, u_ref, up_ref, un_ref, dgate_ref, w_ref,
             dproj_ref, dcw_ref, dcb_ref, st_s):
        i = pl.program_id(0)
        first, last = i == 0, i == n_tiles - 1

        @pl.when(first)
        def _():
            dcw_ref[...] = jnp.zeros_like(dcw_ref)
            dcb_ref[...] = jnp.zeros_like(dcb_ref)

        rows_p = lax.broadcasted_iota(jnp.int32, (SUBLANES, pg), 0)
        rows_c = lax.broadcasted_iota(jnp.int32, (SUBLANES, c), 0)
        w = w_ref[...]

        def chunk(r0, _):
            rs = pl.ds(r0, SUBLANES)
            for g, win in enumerate(POOL_WINDOWS):
                cols = pl.ds(g * pg, pg)
                prv, cur, nxt = _neighbour_chunks(e_ref, ep_ref, en_ref, r0, t, cols, first, last)
                tot = cur
                for o in range(-(win // 2) + 1, win // 2 + 1):
                    if o != 0:
                        tot = tot + _shift_rows(prv, cur, nxt, o, rows_p)
                cnt = _window_counts(r0, i, t, s, win // 2, (SUBLANES, pg))
                st_s[rs, cols] = tot - cur * cnt
            prv, cur, nxt = _neighbour_chunks(dx_ref, dxp_ref, dxn_ref, r0, t, slice(None), first, last)
            du = w[1:2] * cur
            du += w[0:1] * _shift_rows(prv, cur, nxt, 1, rows_c)
            du += w[2:3] * _shift_rows(prv, cur, nxt, -1, rows_c)
            du += w[3:4] * _shift_rows(prv, cur, nxt, -2, rows_c)
            st_s[rs, pl.ds(p, c)] = du
            uprv, ucur, unxt = _neighbour_chunks(u_ref, up_ref, un_ref, r0, t, slice(None), first, last)
            dcb_ref[...] += cur
            for j, o in enumerate((-1, 0, 1, 2)):
                dcw_ref[j] += cur * _shift_rows(uprv, ucur, unxt, o, rows_c)

        _chunk_loop(t, chunk)
        dproj_ref[:, pl.ds(0, p + c)] = st_s[...].astype(BF16)
        dproj_ref[:, pl.ds(p + c, c)] = dgate_ref[...]

    return pl.pallas_call(
        body, name="dproj", grid=(n_tiles,),
        in_specs=_halo_specs(t, s, p, 0) + _halo_specs(t, s, c, 0) + _halo_specs(t, s, c, 1) + [
            pl.BlockSpec((t, c), lambda i: (i, 0)), pl.BlockSpec((4, c), lambda i: (0, 0))],
        out_specs=[pl.BlockSpec((t, p + 2 * c), lambda i: (i, 0)),
                   pl.BlockSpec((4, SUBLANES, c), lambda i: (0, 0, 0)), pl.BlockSpec((SUBLANES, c), lambda i: (0, 0))],
        out_shape=[jax.ShapeDtypeStruct((s, p + 2 * c), BF16), jax.ShapeDtypeStruct((4, SUBLANES, c), F32),
                   jax.ShapeDtypeStruct((SUBLANES, c), F32)],
        scratch_shapes=[pltpu.VMEM((t, p + c), F32)],
        compiler_params=_params("arbitrary"),
    )(e_pool, e_pool, e_pool, dxc, dxc, dxc, proj, proj, proj, dgate, conv_w)


def _dx_call(dproj, w_in, dz1, dep):
    s, e = dproj.shape
    n, d, e4 = w_in.shape
    tm = min(MM_TILE, s)

    def body(dp_ref, w_ref, dz_ref, dep_ref, o_ref):
        k = pl.program_id(1)

        @pl.when(k == 0)
        def _():
            o_ref[...] = ALPHA * dz_ref[...]

        o_ref[...] += lax.dot_general(dp_ref[...], w_ref[...], (((1,), (1,)), ((), ())), preferred_element_type=F32)

    return pl.pallas_call(
        body, name="grad_x", grid=(s // tm, n),
        in_specs=[pl.BlockSpec((tm, e4), lambda i, k: (i, k)), pl.BlockSpec((None, d, e4), lambda i, k: (k, 0, 0)),
                  pl.BlockSpec((tm, d), lambda i, k: (i, 0)), ANY],
        out_specs=pl.BlockSpec((tm, d), lambda i, k: (i, 0)),
        out_shape=jax.ShapeDtypeStruct((s, d), F32),
        compiler_params=_params("arbitrary", "arbitrary"),
    )(dproj, w_in, dz1, dep)


def _row_tile(rows, cols, n_arrays):
    limit = max(SUBLANES, ELT_BLOCK_BYTES // (4 * cols * max(1, n_arrays // 4)))
    best = SUBLANES
    for cand in range(SUBLANES, min(rows, limit) + 1, SUBLANES):
        if rows % cand == 0:
            best = cand
    return best if rows % SUBLANES == 0 else rows


def _cast_to_slot_call(a, idx, dtype, name, dep):
    rows, cols = a.shape
    tr = _row_tile(rows, cols, 2)
    extra = [] if dep is None else [dep]

    def body(idx_ref, a_ref, *rest):
        rest[-1][...] = a_ref[...].astype(dtype)

    return pl.pallas_call(
        body, name=name,
        grid_spec=pltpu.PrefetchScalarGridSpec(
            num_scalar_prefetch=1, grid=(rows // tr,),
            in_specs=[pl.BlockSpec((tr, cols), lambda i, idx_ref: (i, 0))] + [ANY] * len(extra),
            out_specs=pl.BlockSpec((None, tr, cols), lambda i, idx_ref: (idx_ref[1], i, 0))),
        out_shape=jax.ShapeDtypeStruct((N_CHIPS, rows, cols), dtype),
        compiler_params=_params("arbitrary"),
    )(idx, a, *extra)


def _add_half_call(g, recv, idx, name):
    _, rows, cols = g.shape
    tr = _row_tile(rows, cols, 3)

    def body(idx_ref, g_ref, r_ref, o_ref):
        o_ref[...] = g_ref[...] + r_ref[...]

    return pl.pallas_call(
        body, name=name,
        grid_spec=pltpu.PrefetchScalarGridSpec(
            num_scalar_prefetch=1, grid=(rows // tr,),
            in_specs=[pl.BlockSpec((None, tr, cols), lambda i, idx_ref: (idx_ref[0], i, 0)),
                      pl.BlockSpec((tr, cols), lambda i, idx_ref: (i, 0))],
            out_specs=pl.BlockSpec((None, tr, cols), lambda i, idx_ref: (idx_ref[1], i, 0))),
        out_shape=jax.ShapeDtypeStruct((N_CHIPS, rows, cols), F32),
        compiler_params=_params("arbitrary"),
    )(idx, g, recv)


def _sum_chips_call(own, recv, idx, name):
    _, rows, cols = recv.shape
    tr = _row_tile(rows, cols, 5)
    out_spec = pl.BlockSpec((None, tr, cols), lambda i, idx_ref: (idx_ref[0], i, 0))
    if own is None:
        def body(idx_ref, r_ref, o_ref):
            o_ref[...] = ((r_ref[0] + r_ref[1]) + r_ref[2]) + r_ref[3]
        in_specs = [pl.BlockSpec((N_CHIPS, tr, cols), lambda i, idx_ref: (0, i, 0))]
        args = (recv,)
    else:
        def body(idx_ref, p_ref, r_ref, o_ref):
            o_ref[...] = ((p_ref[...] + r_ref[0]) + r_ref[1]) + r_ref[2]
        in_specs = [pl.BlockSpec((None, tr, cols), lambda i, idx_ref: (idx_ref[1], i, 0)),
                    pl.BlockSpec((N_CHIPS - 1, tr, cols), lambda i, idx_ref: (0, i, 0))]
        args = (own, recv)
    return pl.pallas_call(
        body, name=name,
        grid_spec=pltpu.PrefetchScalarGridSpec(num_scalar_prefetch=1, grid=(rows // tr,), in_specs=in_specs, out_specs=out_spec),
        out_shape=jax.ShapeDtypeStruct((2, rows, cols), F32),
        compiler_params=_params("arbitrary"),
    )(idx, *args)


def _adamw_call(g, w, m, v, name):
    rows, cols = w.shape
    tr = _row_tile(rows, cols, 8)

    def body(g_ref, w_ref, m_ref, v_ref, go_ref, d_ref, mo_ref, vo_ref):
        gv = g_ref[...]
        go_ref[...] = gv
        mn = ADAM_B1 * m_ref[...] + (1.0 - ADAM_B1) * gv
        vn = ADAM_B2 * v_ref[...] + (1.0 - ADAM_B2) * (gv * gv)
        m_hat = mn / (1.0 - ADAM_B1 ** ADAM_STEP)
        v_hat = vn / (1.0 - ADAM_B2 ** ADAM_STEP)
        d_ref[...] = -ADAM_LR * (m_hat / (jnp.sqrt(v_hat) + ADAM_EPS) + ADAM_WD * w_ref[...])
        mo_ref[...] = mn
        vo_ref[...] = vn

    spec = pl.BlockSpec((tr, cols), lambda i: (i, 0))
    shape = jax.ShapeDtypeStruct((rows, cols), F32)
    return pl.pallas_call(
        body, name=name, grid=(rows // tr,),
        in_specs=[spec] * 4, out_specs=[spec] * 4, out_shape=[shape] * 4,
        compiler_params=_params("arbitrary"),
    )(g, w, m, v)


def _mesh_place():
    x, y, c = lax.axis_index("x"), lax.axis_index("y"), lax.axis_index("c")
    chips = [(1 - x, y), (x, 1 - y), (1 - x, 1 - y)]
    return x, y, c, chips


def _remote(src, dst, send_sems, recv_sems, idx, device):
    return pltpu.make_async_remote_copy(src_ref=src, dst_ref=dst, send_sem=send_sems.at[idx], recv_sem=recv_sems.at[idx],
                                        device_id=device, device_id_type=MESH)


HBM_SPEC = pl.BlockSpec(memory_space=pltpu.HBM)
SEM_SPEC = pl.BlockSpec(memory_space=pltpu.SEMAPHORE)
ORDERED_EFFECT = pltpu.SideEffectType.DATAFLOW_SIDE_EFFECTING


def _in_hbm(a):
    return pltpu.with_memory_space_constraint(a, pltpu.HBM)


def _start_copies_call(name, bufs, groups, after=None):
    n, g = len(bufs), len(groups)
    extra = [] if after is None else [after]
    first_out = n + len(extra)

    def body(*refs):
        outs = refs[first_out:first_out + n]
        sems = refs[first_out + n:first_out + n + 2 * g]
        token = refs[first_out + n + 2 * g]
        for i, (which, copies_fn, _) in enumerate(groups):
            for mine, _ in copies_fn([outs[w] for w in which], sems[2 * i], sems[2 * i + 1]):
                mine.start()
        token[...] = jnp.zeros_like(token)

    sem_shapes = [pltpu.SemaphoreType.DMA((cnt,)) for _, _, cnt in groups for _ in range(2)]
    res = pl.pallas_call(
        body, name=name,
        in_specs=[HBM_SPEC] * n + [ANY] * len(extra),
        out_specs=[HBM_SPEC] * n + [SEM_SPEC] * (2 * g) + [pl.BlockSpec(memory_space=pltpu.VMEM)],
        out_shape=[pltpu.HBM(a.shape, a.dtype) for a in bufs] + sem_shapes + [jax.ShapeDtypeStruct((SUBLANES, LANES), F32)],
        input_output_aliases={a: a for a in range(n)},
        compiler_params=pltpu.CompilerParams(has_side_effects=ORDERED_EFFECT),
    )(*[_in_hbm(a) for a in bufs], *extra)
    sems = res[n:n + 2 * g]
    return list(res[:n]), [(sems[2 * i], sems[2 * i + 1]) for i in range(g)], res[n + 2 * g]


def _wait_copies_call(name, bufs, sems, copies_fn, after):
    n = len(bufs)

    def body(*refs):
        ins = refs[:n]
        send_sems, recv_sems = refs[n], refs[n + 1]
        for mine, arriving in copies_fn(list(ins), send_sems, recv_sems):
            arriving.wait_recv()
            mine.wait_send()

    res = pl.pallas_call(
        body, name=name,
        in_specs=[HBM_SPEC] * n + [SEM_SPEC, SEM_SPEC, ANY],
        out_specs=[HBM_SPEC] * n,
        out_shape=[pltpu.HBM(a.shape, a.dtype) for a in bufs],
        input_output_aliases={a: a for a in range(n)},
        compiler_params=pltpu.CompilerParams(has_side_effects=ORDERED_EFFECT),
    )(*bufs, sems[0], sems[1], after)
    return list(res)


def _gather_copies(bufs, send_sems, recv_sems):
    x, y, c, chips = _mesh_place()
    k = 2 * x + y
    out = []
    for a, buf in enumerate(bufs):
        for j, (px, py) in enumerate(chips):
            kj = 2 * px + py
            mine = _remote(buf.at[k, c], buf.at[k, c], send_sems, recv_sems, 3 * a + j, (px, py, c))
            arriving = _remote(buf.at[k, c], buf.at[kj, c], send_sems, recv_sems, 3 * a + j, (px, py, c))
            out.append((mine, arriving))
    return out


def _exchange_copies(n_sharded, n_replicated):
    def copies(bufs, send_sems, recv_sems):
        x, y, c, chips = _mesh_place()
        k = 2 * x + y
        sums, lands = bufs[:n_sharded], bufs[n_sharded:2 * n_sharded]
        repl = bufs[2 * n_sharded:]
        out = []
        for j, (px, py) in enumerate(chips):
            kj = 2 * px + py
            for a in range(n_sharded):
                cp = _remote(sums[a].at[kj], lands[a].at[j], send_sems, recv_sems, 3 * a + j, (px, py, c))
                out.append((cp, cp))
            for a in range(n_replicated):
                idx = 3 * (n_sharded + a) + j
                mine = _remote(repl[a].at[k], repl[a].at[k], send_sems, recv_sems, idx, (px, py, c))
                arriving = _remote(repl[a].at[k], repl[a].at[kj], send_sems, recv_sems, idx, (px, py, c))
                out.append((mine, arriving))
        return out
    return copies


def _sibling_copies(n, halves):
    def copies(bufs, send_sems, recv_sems):
        x, y, c, _ = _mesh_place()
        out = []
        for a in range(n):
            src = bufs[a].at[1 - c] if halves else bufs[a]
            cp = _remote(src, bufs[n + a], send_sems, recv_sems, a, (x, y, 1 - c))
            out.append((cp, cp))
        return out
    return copies


def _forward_copies(bufs, send_sems, recv_sems):
    x, y, c, chips = _mesh_place()
    out = []
    for a, buf in enumerate(bufs):
        for j, (px, py) in enumerate(chips):
            kj = 2 * px + py
            mine = _remote(buf.at[kj, c], buf.at[kj, c], send_sems, recv_sems, 3 * a + j, (x, y, 1 - c))
            arriving = _remote(buf.at[kj, c], buf.at[kj, 1 - c], send_sems, recv_sems, 3 * a + j, (x, y, 1 - c))
            out.append((mine, arriving))
    return out


def _join_copies(bufs, send_sems, recv_sems):
    x, y, c, _ = _mesh_place()
    out = []
    for a, buf in enumerate(bufs):
        mine = _remote(buf.at[c], buf.at[c], send_sems, recv_sems, a, (x, y, 1 - c))
        arriving = _remote(buf.at[c], buf.at[1 - c], send_sems, recv_sems, a, (x, y, 1 - c))
        out.append((mine, arriving))
    return out


def _forward_to_sibling_call(bufs, name):
    n = len(bufs)

    def body(*refs):
        ins, outs = refs[:n], refs[n:2 * n]
        send_sems, recv_sems = refs[2 * n:]
        x, y, c, chips = _mesh_place()
        sibling = (x, y, 1 - c)
        sends = []
        for a in range(n):
            for j, (px, py) in enumerate(chips):
                kj = 2 * px + py
                sends.append(_remote(ins[a].at[kj, c], outs[a].at[kj, c], send_sems, recv_sems, 3 * a + j, sibling))
        for cp in sends:
            cp.start()
        for a in range(n):
            for j, (px, py) in enumerate(chips):
                kj = 2 * px + py
                _remote(ins[a].at[kj, c], outs[a].at[kj, 1 - c], send_sems, recv_sems, 3 * a + j, sibling).wait_recv()
        for cp in sends:
            cp.wait_send()

    return pl.pallas_call(
        body, name=name,
        in_specs=[ANY] * n, out_specs=[ANY] * n,
        out_shape=[jax.ShapeDtypeStruct(a.shape, a.dtype) for a in bufs],
        input_output_aliases={a: a for a in range(n)},
        scratch_shapes=[pltpu.SemaphoreType.DMA((3 * n,)), pltpu.SemaphoreType.DMA((3 * n,))],
    )(*bufs)


def _join_halves_call(bufs, name):
    n = len(bufs)

    def body(*refs):
        ins, outs = refs[:n], refs[n:2 * n]
        send_sems, recv_sems = refs[2 * n:]
        x, y, c, _ = _mesh_place()
        sibling = (x, y, 1 - c)
        copies = [_remote(ins[a].at[c], outs[a].at[c], send_sems, recv_sems, a, sibling) for a in range(n)]
        for cp in copies:
            cp.start()
        for a in range(n):
            _remote(ins[a].at[c], outs[a].at[1 - c], send_sems, recv_sems, a, sibling).wait_recv()
        for cp in copies:
            cp.wait_send()

    return pl.pallas_call(
        body, name=name,
        in_specs=[ANY] * n, out_specs=[ANY] * n,
        out_shape=[jax.ShapeDtypeStruct(a.shape, a.dtype) for a in bufs],
        input_output_aliases={a: a for a in range(n)},
        scratch_shapes=[pltpu.SemaphoreType.DMA((n,)), pltpu.SemaphoreType.DMA((n,))],
    )(*bufs)


def _pack(arrays, rows_multiple):
    flat = jnp.concatenate([a.reshape(-1) for a in arrays])
    per = LANES * rows_multiple
    padded = -(-flat.shape[0] // per) * per
    flat = jnp.pad(flat, (0, padded - flat.shape[0]))
    return flat.reshape(-1, LANES)


def _unpack(packed, shapes):
    flat = packed.reshape(-1)
    out, at = [], 0
    for shp in shapes:
        size = 1
        for dim in shp:
            size *= dim
        out.append(flat[at:at + size].reshape(shp))
        at += size
    return out


def _halves(a):
    return a.reshape((2, a.shape[0] // 2) + a.shape[1:])


def kernel(x, ln_mix_g, ln_mix_b, w_in, w_pool, pool_scale, conv_w, conv_b, w_rg_a, b_rg_a, w_rg_i, b_rg_i, rg_lambda, w_out, ln_ffn_g, ln_ffn_b, w_mlp_in, w_mlp_out, loss_target, m_ln_mix_g, m_ln_mix_b, m_w_in, m_w_pool, m_pool_scale, m_conv_w, m_conv_b, m_w_rg_a, m_b_rg_a, m_w_rg_i, m_b_rg_i, m_rg_lambda, m_w_out, m_ln_ffn_g, m_ln_ffn_b, m_w_mlp_in, m_w_mlp_out, v_ln_mix_g, v_ln_mix_b, v_w_in, v_w_pool, v_pool_scale, v_conv_w, v_conv_b, v_w_rg_a, v_b_rg_a, v_w_rg_i, v_b_rg_i, v_rg_lambda, v_w_out, v_ln_ffn_g, v_ln_ffn_b, v_w_mlp_in, v_w_mlp_out):
    weights = dict(ln_mix_g=ln_mix_g, ln_mix_b=ln_mix_b, w_in=w_in, w_pool=w_pool, pool_scale=pool_scale, conv_w=conv_w,
                   conv_b=conv_b, w_rg_a=w_rg_a, b_rg_a=b_rg_a, w_rg_i=w_rg_i, b_rg_i=b_rg_i, rg_lambda=rg_lambda,
                   w_out=w_out, ln_ffn_g=ln_ffn_g, ln_ffn_b=ln_ffn_b, w_mlp_in=w_mlp_in, w_mlp_out=w_mlp_out)
    m_in = dict(ln_mix_g=m_ln_mix_g, ln_mix_b=m_ln_mix_b, w_in=m_w_in, w_pool=m_w_pool, pool_scale=m_pool_scale,
                conv_w=m_conv_w, conv_b=m_conv_b, w_rg_a=m_w_rg_a, b_rg_a=m_b_rg_a, w_rg_i=m_w_rg_i, b_rg_i=m_b_rg_i,
                rg_lambda=m_rg_lambda, w_out=m_w_out, ln_ffn_g=m_ln_ffn_g, ln_ffn_b=m_ln_ffn_b, w_mlp_in=m_w_mlp_in,
                w_mlp_out=m_w_mlp_out)
    v_in = dict(ln_mix_g=v_ln_mix_g, ln_mix_b=v_ln_mix_b, w_in=v_w_in, w_pool=v_w_pool, pool_scale=v_pool_scale,
                conv_w=v_conv_w, conv_b=v_conv_b, w_rg_a=v_w_rg_a, b_rg_a=v_b_rg_a, w_rg_i=v_w_rg_i, b_rg_i=v_b_rg_i,
                rg_lambda=v_rg_lambda, w_out=v_w_out, ln_ffn_g=v_ln_ffn_g, ln_ffn_b=v_ln_ffn_b, w_mlp_in=v_w_mlp_in,
                w_mlp_out=v_w_mlp_out)
    names = list(weights)

    xs = x[0]
    tgt = loss_target[0]
    s, d = xs.shape
    p = c = d // 2
    pg = p // N_POOL_GROUPS
    core = lax.axis_index("c")
    shard = 2 * lax.axis_index("x") + lax.axis_index("y")

    idx = jnp.stack([core, shard]).astype(jnp.int32)
    small_shard = _pack([conv_w[0], b_rg_a[0], b_rg_i[0], rg_lambda[0]], 2 * SUBLANES)
    to_gather = [(w_in[0], BF16), (w_out[0], BF16), (w_mlp_in[0], BF16), (w_mlp_out[0], BF16),
                 (w_pool[0].reshape(-1, pg), BF16), (small_shard, F32)]

    def slot_view(i, dep):
        a, dt = to_gather[i]
        sl = _cast_to_slot_call(a, idx, dt, f"gather_slot_{i}", dep)
        return sl.reshape(N_CHIPS, 2, sl.shape[1] // 2, sl.shape[2])

    first, later = (0, 4, 5), (1, 2, 3)
    fly_a, sems_a, token_a = _start_copies_call(
        "gather_start_first", [slot_view(i, None) for i in first], [((0, 1, 2), _gather_copies, 3 * len(first))])
    later_views = []
    for i in later:
        later_views.append(slot_view(i, later_views[-1] if later_views else token_a))
    xb = _cast_call(xs, later_views[-1])
    got_first = _wait_copies_call("gather_wait_w_in", fly_a, sems_a[0], _gather_copies, xb)
    fly_b, sems_b, g_token = _start_copies_call(
        "gather_start_later", later_views + got_first,
        [((0,), _gather_copies, 3), ((1,), _gather_copies, 3), ((2,), _gather_copies, 3)])
    got_first = fly_b[len(later):]
    in_flight = dict(zip(later, fly_b))
    g_sems = [None] + list(sems_b)

    def arrive(which, group, after, tag):
        return _wait_copies_call(f"gather_wait_{tag}", [in_flight[w] for w in which], g_sems[group], _gather_copies, after)

    def pass_on(got, tag):
        flying, sems, token = _start_copies_call(
            f"gather_forward_start_{tag}", got, [(tuple(range(len(got))), _forward_copies, 3 * len(got))])
        return (flying, sems[0], tag), token

    def passed_on(state, after):
        flying, sems, tag = state
        return _wait_copies_call(f"gather_forward_wait_{tag}", flying, sems, _forward_copies, after)

    gathered = [None] * len(to_gather)
    gathered[0], gathered[4], gathered[5] = _forward_to_sibling_call(got_first, "gather_forward_w_in")
    w_in_f = gathered[0].reshape((N_CHIPS,) + w_in.shape[1:])
    w_pool_f = gathered[4].reshape(N_CHIPS, N_POOL_GROUPS, pg // N_CHIPS, pg).transpose(1, 0, 2, 3).reshape(N_POOL_GROUPS, pg, pg)
    c4 = c // N_CHIPS
    small_parts = [_unpack(gathered[5][k].reshape(-1, LANES), [(4, c4), (2, c4), (2, c4), (2, c4)]) for k in range(N_CHIPS)]
    conv_w_f = jnp.concatenate([sp_[0] for sp_ in small_parts], axis=1)
    b_a_f = jnp.concatenate([sp_[1] for sp_ in small_parts], axis=1)
    b_i_f = jnp.concatenate([sp_[2] for sp_ in small_parts], axis=1)
    lam_f = jnp.concatenate([sp_[3] for sp_ in small_parts], axis=1)
    wa_b = w_rg_a[0].astype(BF16)
    wi_b = w_rg_i[0].astype(BF16)

    proj = _proj_call(xb, w_in_f)
    xc = _conv_call(proj, conv_w_f, conv_b, c)
    fwd_w_out, token = pass_on(arrive((1,), 1, xc, "w_out"), "w_out")
    h_b, *gates_b = _scan_fwd_call(xc, wa_b[1], wi_b[1], b_a_f[1:2], b_i_f[1:2], lam_f[1:2], True, token)
    h_f, *gates_f = _scan_fwd_call(xc, wa_b[0], wi_b[0], b_a_f[0:1], b_i_f[0:1], lam_f[0:1], False, token)
    y, d_pool = _pool_combine_call(proj, h_f, h_b, w_pool_f, pool_scale, p)
    w_out_f = passed_on(fwd_w_out, y)[0].reshape(d, d)
    fwd_w1, token = pass_on(arrive((2,), 2, y, "w_mlp_in"), "w_mlp_in")
    xh1, x1b, rstd1 = _out_ln1_call(y, w_out_f, xs, ln_mix_g, ln_mix_b, token)
    w1_f = passed_on(fwd_w1, x1b)[0].reshape((N_CHIPS,) + w_mlp_in.shape[1:])
    first_half = _mlp_in_call(x1b, w1_f, g_token, None)
    fwd_w2, token = pass_on(arrive((3,), 3, first_half[0], "w_mlp_out"), "w_mlp_out")
    r_act, hsq = _mlp_in_call(x1b, w1_f, token, first_half)
    w2_f = passed_on(fwd_w2, hsq)[0].reshape(N_CHIPS * w_mlp_out.shape[1], d)
    dz2, dz2b, loss8, dg2, db2 = _mlp_out_ln2_call(hsq, w2_f, xh1, ln_mix_g, ln_mix_b, ln_ffn_g, ln_ffn_b, tgt)

    def start_siblings(grads, halves, tag, after=None):
        lands = [lax.empty(g.shape[1:] if halves else g.shape, g.dtype) for g in grads]
        copies = _sibling_copies(len(grads), halves)
        flying, sems, token = _start_copies_call(
            f"siblings_start_{tag}", list(grads) + lands, [(tuple(range(2 * len(grads))), copies, len(grads))], after)
        return (flying, sems[0], copies, len(grads), tag), token

    def finish_siblings(state, after):
        flying, sems, copies, n, tag = state
        got = _wait_copies_call(f"siblings_wait_{tag}", flying, sems, copies, after)
        return got[:n], got[n:]

    half_own = jnp.reshape(core, (1,)).astype(jnp.int32)
    half_sibling = 1 - half_own

    def chip_sum_of(a, b, row_sharded, tag, dep, overlapped):
        for_sibling = _half_grad_call(a, b, half_sibling, row_sharded, None, f"grad_{tag}_for_sibling", dep)
        state, token = start_siblings([for_sibling], False, tag)
        results = overlapped(token)
        _, (from_sibling,) = finish_siblings(state, results[0])
        return _half_grad_call(a, b, half_own, row_sharded, from_sibling, f"grad_{tag}", token), results

    def start_exchange(sums, n_repl, tag):
        n_sh = len(sums) - n_repl
        lands = [lax.empty((N_CHIPS - 1,) + a.shape[1:], a.dtype) for a in sums[:n_sh]]
        bufs = sums[:n_sh] + lands + sums[n_sh:]
        copies = _exchange_copies(n_sh, n_repl)
        flying, sems, token = _start_copies_call(
            f"reduce_start_{tag}", bufs, [(tuple(range(len(bufs))), copies, 3 * len(sums))])
        return (flying, sems[0], copies, n_sh, tag), token

    def finish_exchange(state, after):
        flying, sems, copies, n_sh, tag = state
        got = _wait_copies_call(f"reduce_wait_{tag}", flying, sems, copies, after)
        halves = []
        for a in range(n_sh):
            own, land = got[a], got[n_sh + a]
            cols = own.shape[-1]
            total = _sum_chips_call(own.reshape(N_CHIPS, -1, cols), land.reshape(N_CHIPS - 1, -1, cols), idx,
                                    f"reduce_sum_{tag}_{a}")
            halves.append(total.reshape((2,) + own.shape[1:]))
        for a, rp in enumerate(got[2 * n_sh:]):
            halves.append(_sum_chips_call(None, rp, idx, f"reduce_sum_{tag}_r{a}"))
        return halves

    def start_join(halves, tag):
        flying, sems, token = _start_copies_call(
            f"join_start_{tag}", halves, [(tuple(range(len(halves))), _join_copies, len(halves))])
        return (flying, sems[0], tag), token

    def finish_join(state, after):
        flying, sems, tag = state
        return _wait_copies_call(f"join_wait_{tag}", flying, sems, _join_copies, after)

    sum_w2, (dpre,) = chip_sum_of(hsq, dz2b, True, "w_mlp_out", g_token,
                                  lambda tok: (_dhsq_call(dz2b, w2_f, r_act, tok),))
    flying_w2, token = start_exchange([sum_w2], 0, "w2")
    sum_w1, (dz1, dz1b, dg1, db1) = chip_sum_of(
        x1b, dpre, False, "w_mlp_in", token,
        lambda tok: _dx1_ln1_bwd_call(dpre, w1_f, dz2, xh1, rstd1, ln_mix_g, tok))
    flying_w1, token = start_exchange([sum_w1], 0, "w1")

    def dy_and_mixer(tok):
        dy = _dy_call(dz1b, w_out_f, tok)
        return _mixer_bwd_call(dy, d_pool, proj, h_f, h_b, w_pool_f, pool_scale, p)

    sum_wout, (e_pool, dh, dgate, g_wpool, g_pscale8) = chip_sum_of(y, dz1b, True, "w_out", token, dy_and_mixer)
    flying_wout, token = start_exchange([sum_wout], 0, "w_out")
    dxc0, g_wa0, g_wi0, g_ba0, g_bi0, g_sp0 = _scan_bwd_call(
        xc, dh, h_f, gates_f, None, wa_b[0], wi_b[0], lam_f[0:1], False, token)
    dxc, g_wa1, g_wi1, g_ba1, g_bi1, g_sp1 = _scan_bwd_call(
        xc, dh, h_b, gates_b, dxc0, wa_b[1], wi_b[1], lam_f[1:2], True, token)
    dproj, g_cw8, g_cb8 = _dproj_call(e_pool, dxc, proj, dgate, conv_w_f, p)

    rowsum = lambda a8: jnp.sum(a8, axis=-2)
    g_lam = jnp.stack([rowsum(g_sp0), rowsum(g_sp1)]) * (-_sigmoid(-lam_f))
    small_grads = {
        "ln_mix_g": rowsum(dg1), "ln_mix_b": rowsum(db1), "ln_ffn_g": rowsum(dg2), "ln_ffn_b": rowsum(db2),
        "pool_scale": rowsum(g_pscale8), "conv_b": rowsum(g_cb8),
        "w_rg_a": jnp.stack([g_wa0, g_wa1]), "w_rg_i": jnp.stack([g_wi0, g_wi1]),
        "w_pool": g_wpool, "conv_w": rowsum(g_cw8),
        "b_rg_a": jnp.stack([rowsum(g_ba0), rowsum(g_ba1)]), "b_rg_i": jnp.stack([rowsum(g_bi0), rowsum(g_bi1)]),
        "rg_lambda": g_lam,
    }
    small_names = list(small_grads)
    small_shapes = [small_grads[nm].shape for nm in small_names]
    loss_share = jnp.reshape(jnp.sum(loss8) * (0.5 / d), (1,))
    g_small = _halves(_pack([small_grads[nm] for nm in small_names] + [loss_share], 2 * SUBLANES))
    sib_small, token = start_siblings([g_small], True, "small")
    flying_small = []

    def small_exchange_and_grad_x(tok):
        (mine,), (theirs,) = finish_siblings(sib_small, tok)
        small_sum = _add_half_call(mine, theirs, idx, "reduce_add_small")
        state, tok = start_exchange([small_sum], 1, "small")
        flying_small.append(state)
        return (_dx_call(dproj, w_in_f, dz1, tok),)

    sum_win, (grad_x,) = chip_sum_of(xb, dproj, False, "w_in", token, small_exchange_and_grad_x)
    flying_small = flying_small[0]
    flying_win, token = start_exchange([sum_win], 0, "w_in")

    grad_w, delta_w, new_m, new_v = {}, {}, {}, {}

    def adamw(nm, full):
        w2d = weights[nm][0]
        g2d = full.reshape(w2d.shape)
        go, dl, mn, vn = _adamw_call(g2d, w2d, m_in[nm][0], v_in[nm][0], f"adamw_{nm}")
        grad_w[nm], delta_w[nm], new_m[nm], new_v[nm] = go[None], dl[None], mn[None], vn[None]
        return vn

    join_w2, token = start_join(finish_exchange(flying_w2, token), "w2")
    join_w1, token = start_join(finish_exchange(flying_w1, token), "w1")
    join_wout, token = start_join(finish_exchange(flying_wout, token), "w_out")
    last = adamw("w_mlp_out", finish_join(join_w2, token)[0])
    last = adamw("w_mlp_in", finish_join(join_w1, last)[0])
    last = adamw("w_out", finish_join(join_wout, last)[0])

    small_joined = _join_halves_call(finish_exchange(flying_small, last), "reduce_join_small")[0]
    *small_sums, loss_sum = _unpack(small_joined.reshape(-1, LANES), small_shapes + [(1,)])
    small_full = dict(zip(small_names, small_sums))
    local = dict(small_full)
    local["w_pool"] = lax.dynamic_slice_in_dim(small_full["w_pool"], shard * (pg // N_CHIPS), pg // N_CHIPS, axis=1)
    for nm in ("conv_w", "b_rg_a", "b_rg_i", "rg_lambda"):
        local[nm] = lax.dynamic_slice_in_dim(small_full[nm], shard * c4, c4, axis=1)
    small_w_shapes = [weights[nm].shape for nm in small_names]
    g_pack = _pack([local[nm] for nm in small_names], SUBLANES)
    w_pack = _pack([weights[nm] for nm in small_names], SUBLANES)
    m_pack = _pack([m_in[nm] for nm in small_names], SUBLANES)
    v_pack = _pack([v_in[nm] for nm in small_names], SUBLANES)
    _, dl_p, mn_p, vn_p = _adamw_call(g_pack, w_pack, m_pack, v_pack, "adamw_small")
    for nm, gl, dl, mn, vn in zip(small_names, _unpack(g_pack, small_w_shapes), _unpack(dl_p, small_w_shapes),
                                  _unpack(mn_p, small_w_shapes), _unpack(vn_p, small_w_shapes)):
        grad_w[nm], delta_w[nm], new_m[nm], new_v[nm] = gl, dl, mn, vn
    adamw("w_in", _join_halves_call(finish_exchange(flying_win, vn_p), "reduce_join_w_in")[0])

    loss = loss_sum[0]
    return (loss, grad_x[None], *[grad_w[nm] for nm in names], *[delta_w[nm] for nm in names],
            *[new_m[nm] for nm in names], *[new_v[nm] for nm in names])
```

```python
import jax
import jax.numpy as jnp
from jax import lax
from jax.experimental import pallas as pl
from jax.experimental.pallas import tpu as pltpu

F32 = jnp.float32
BF16 = jnp.bfloat16

N_CHIPS = 4
LANES = 128
SUBLANES = 8
LRU_HEAD = 128
N_POOL_GROUPS = 4
POOL_WINDOWS = (2, 4, 8, 16)
RG_C = 8.0
LN_EPS = 1e-5
ALPHA = 2.0 ** 0.25
ADAM_LR, ADAM_B1, ADAM_B2, ADAM_EPS, ADAM_WD, ADAM_STEP = 0.001, 0.9, 0.999, 1e-08, 0.01, 10
VMEM_LIMIT = 56 * 1024 * 1024
SEQ_TILE = 256
MM_TILE = 512
LN_MM_K = 2048
LN_UNROLL = 8
ELT_BLOCK_BYTES = 2 * 1024 * 1024
RESIDENT_OPERAND_BYTES = 16 * 1024 * 1024
MESH = pl.DeviceIdType.MESH
ANY = pl.BlockSpec(memory_space=pl.ANY)


def _params(*sem):
    return pltpu.CompilerParams(dimension_semantics=sem, vmem_limit_bytes=VMEM_LIMIT)


def _sigmoid(z):
    return 1.0 / (1.0 + jnp.exp(-z))


def _neg_expm1(z):
    series = -(z * (1.0 + z * (0.5 + z * (1.0 / 6.0 + z * (1.0 / 24.0)))))
    return jnp.where(z > -0.01, series, 1.0 - jnp.exp(z))


def _softplus(z):
    return jnp.maximum(z, 0.0) + jnp.log1p(jnp.exp(-jnp.abs(z)))


_GELU_K = 0.7978845608028654
_GELU_C = 0.044715


def _gelu_and_grad(u):
    t = jnp.tanh(_GELU_K * (u + _GELU_C * (u * u * u)))
    g = 0.5 * u * (1.0 + t)
    dg = 0.5 * (1.0 + t) + 0.5 * u * (1.0 - t * t) * (_GELU_K * (1.0 + 3.0 * _GELU_C * u * u))
    return g, dg


def _shift_rows(prv, cur, nxt, o, rows):
    if o == 0:
        return cur
    if o == SUBLANES:
        return nxt
    if o == -SUBLANES:
        return prv
    if o > 0:
        return pltpu.roll(jnp.where(rows >= o, cur, nxt), SUBLANES - o, 0)
    p = -o
    return pltpu.roll(jnp.where(rows < SUBLANES - p, cur, prv), p, 0)


def _neighbour_chunks(main_ref, prev_ref, next_ref, r0, t_rows, cols, first_tile, last_tile):
    cur = main_ref[pl.ds(r0, SUBLANES), cols]
    before = main_ref[pl.ds(pl.multiple_of(jnp.maximum(r0 - SUBLANES, 0), SUBLANES), SUBLANES), cols]
    after = main_ref[pl.ds(pl.multiple_of(jnp.minimum(r0 + SUBLANES, t_rows - SUBLANES), SUBLANES), SUBLANES), cols]
    halo_prev = jnp.where(first_tile, 0.0, prev_ref[:, cols])
    halo_next = jnp.where(last_tile, 0.0, next_ref[:, cols])
    prv = jnp.where(r0 == 0, halo_prev, before)
    nxt = jnp.where(r0 == t_rows - SUBLANES, halo_next, after)
    return prv, cur, nxt


def _halo_specs(t_rows, n_rows, width, col_block):
    per = t_rows // SUBLANES
    last = n_rows // SUBLANES - 1
    return [
        pl.BlockSpec((t_rows, width), lambda i: (i, col_block)),
        pl.BlockSpec((SUBLANES, width), lambda i: (jnp.maximum(i * per - 1, 0), col_block)),
        pl.BlockSpec((SUBLANES, width), lambda i: (jnp.minimum((i + 1) * per, last), col_block)),
    ]


def _chunk_loop(t_rows, fn, init=None, unroll=1, descending=False):
    span = SUBLANES * unroll

    def step(ci, carry):
        base = pl.multiple_of(((t_rows // span - 1 - ci) if descending else ci) * span, span)
        for u in range(unroll):
            carry = fn(base + ((unroll - 1 - u) if descending else u) * SUBLANES, carry)
        return carry
    return lax.fori_loop(0, t_rows // span, step, init)


def _scan_chunk(a, b, h_in, rows, reverse):
    for dist in (1, 2, 4):
        if reverse:
            keep = rows < SUBLANES - dist
            shift = SUBLANES - dist
        else:
            keep = rows >= dist
            shift = dist
        b = a * jnp.where(keep, pltpu.roll(b, shift, 0), 0.0) + b
        a = a * jnp.where(keep, pltpu.roll(a, shift, 0), 1.0)
    return a * h_in + b


def _cast_call(x, dep):
    s, d = x.shape
    tm = min(MM_TILE, s)

    def body(x_ref, dep_ref, o_ref):
        o_ref[...] = x_ref[...].astype(BF16)

    return pl.pallas_call(
        body, name="cast_x", grid=(s // tm,),
        in_specs=[pl.BlockSpec((tm, d), lambda i: (i, 0)), ANY],
        out_specs=pl.BlockSpec((tm, d), lambda i: (i, 0)),
        out_shape=jax.ShapeDtypeStruct((s, d), BF16),
        compiler_params=_params("arbitrary"),
    )(x, dep)


def _proj_call(xb, w_in):
    s, d = xb.shape
    n, _, e4 = w_in.shape
    tm = min(2 * MM_TILE, s)

    def body(x_ref, w_ref, proj_ref):
        proj_ref[...] = jnp.dot(x_ref[...], w_ref[...], preferred_element_type=F32)

    return pl.pallas_call(
        body, name="proj", grid=(s // tm, n),
        in_specs=[pl.BlockSpec((tm, d), lambda i, j: (i, 0)), pl.BlockSpec((None, d, e4), lambda i, j: (j, 0, 0))],
        out_specs=pl.BlockSpec((tm, e4), lambda i, j: (i, j)),
        out_shape=jax.ShapeDtypeStruct((s, n * e4), F32),
        compiler_params=_params("arbitrary", "arbitrary"),
    )(xb, w_in)


def _conv_call(proj, conv_w, conv_b, c):
    s = proj.shape[0]
    t = min(SEQ_TILE, s)
    n_tiles = s // t

    def body(u_ref, up_ref, un_ref, w_ref, b_ref, xc_ref):
        i = pl.program_id(0)
        rows = lax.broadcasted_iota(jnp.int32, (SUBLANES, c), 0)
        w = w_ref[...]
        b = b_ref[...]

        def chunk(r0, _):
            prv, cur, nxt = _neighbour_chunks(u_ref, up_ref, un_ref, r0, t, slice(None), i == 0, i == n_tiles - 1)
            acc = b + w[1:2] * cur
            acc += w[0:1] * _shift_rows(prv, cur, nxt, -1, rows)
            acc += w[2:3] * _shift_rows(prv, cur, nxt, 1, rows)
            acc += w[3:4] * _shift_rows(prv, cur, nxt, 2, rows)
            xc_ref[pl.ds(r0, SUBLANES), :] = acc

        _chunk_loop(t, chunk)

    return pl.pallas_call(
        body, name="conv_fwd", grid=(n_tiles,),
        in_specs=_halo_specs(t, s, c, 1) + [pl.BlockSpec((4, c), lambda i: (0, 0)), pl.BlockSpec((1, c), lambda i: (0, 0))],
        out_specs=pl.BlockSpec((t, c), lambda i: (i, 0)),
        out_shape=jax.ShapeDtypeStruct((s, c), F32),
        compiler_params=_params("arbitrary"),
    )(proj, proj, proj, conv_w, conv_b)


def _gate_matmuls(xc_ref, wa_ref, wi_ref, pr_s, pi_s, heads):
    for h in range(heads):
        cs = pl.ds(h * LRU_HEAD, LRU_HEAD)
        xb = xc_ref[:, cs].astype(BF16)
        pr_s[:, cs] = jnp.dot(xb, wa_ref[h], preferred_element_type=F32)
        pi_s[:, cs] = jnp.dot(xb, wi_ref[h], preferred_element_type=F32)


def _rg_gates(pr, pi, ba, bi, sp):
    r = _sigmoid(pr + ba)
    ig = _sigmoid(pi + bi)
    log_a = (-RG_C * r) * sp
    a = jnp.exp(log_a)
    mult = jnp.sqrt(_neg_expm1(2.0 * log_a))
    return r, ig, a, mult


def _scan_fwd_call(xc, wa, wi, ba, bi, lam, reverse, dep):
    s, c = xc.shape
    heads = c // LRU_HEAD
    t = min(SEQ_TILE, s)
    n_tiles = s // t
    tile = (lambda i: (n_tiles - 1 - i, 0)) if reverse else (lambda i: (i, 0))
    whole2 = lambda i: (0, 0)
    whole3 = lambda i: (0, 0, 0)

    def body(xc_ref, wa_ref, wi_ref, ba_ref, bi_ref, lam_ref, dep_ref, h_ref, r_ref, ig_ref, a_ref, mult_ref,
             pr_s, pi_s, carry_s):
        @pl.when(pl.program_id(0) == 0)
        def _():
            carry_s[...] = jnp.zeros_like(carry_s)

        _gate_matmuls(xc_ref, wa_ref, wi_ref, pr_s, pi_s, heads)
        ba_v, bi_v = ba_ref[...], bi_ref[...]
        sp = _softplus(-lam_ref[...])

        rows = lax.broadcasted_iota(jnp.int32, (SUBLANES, c), 0)

        def chunk(r0, h_in):
            rs = pl.ds(r0, SUBLANES)
            r, ig, a, mult = _rg_gates(pr_s[rs, :], pi_s[rs, :], ba_v, bi_v, sp)
            r_ref[rs, :] = r
            ig_ref[rs, :] = ig
            a_ref[rs, :] = a
            mult_ref[rs, :] = mult
            h = _scan_chunk(a, mult * ig * xc_ref[rs, :], h_in, rows, reverse)
            h_ref[rs, :] = h
            return h[0:1, :] if reverse else h[SUBLANES - 1:SUBLANES, :]

        carry_s[...] = _chunk_loop(t, chunk, carry_s[...], unroll=2, descending=reverse)

    return pl.pallas_call(
        body, name="scan_fwd_rev" if reverse else "scan_fwd", grid=(n_tiles,),
        in_specs=[pl.BlockSpec((t, c), tile),
                  pl.BlockSpec((heads, LRU_HEAD, LRU_HEAD), whole3), pl.BlockSpec((heads, LRU_HEAD, LRU_HEAD), whole3),
                  pl.BlockSpec((1, c), whole2), pl.BlockSpec((1, c), whole2), pl.BlockSpec((1, c), whole2), ANY],
        out_specs=[pl.BlockSpec((t, c), tile)] * 5,
        out_shape=[jax.ShapeDtypeStruct((s, c), F32)] * 5,
        scratch_shapes=[pltpu.VMEM((t, c), F32), pltpu.VMEM((t, c), F32), pltpu.VMEM((1, c), F32)],
        compiler_params=_params("arbitrary"),
    )(xc, wa, wi, ba, bi, lam, dep)


def _window_counts(r0, tile_idx, t_rows, n_rows, half, shape):
    pos = tile_idx * t_rows + r0 + lax.broadcasted_iota(jnp.int32, shape, 0)
    hi = jnp.minimum(pos + half, n_rows)
    lo = jnp.maximum(pos - half, 0)
    return (hi - lo).astype(F32)


def _pool_combine_call(proj, h_f, h_b, w_pool, pool_scale, p):
    s = proj.shape[0]
    c = h_f.shape[1]
    pg = p // N_POOL_GROUPS
    t = min(SEQ_TILE, s)
    n_tiles = s // t

    def body(u_ref, up_ref, un_ref, gate_ref, hf_ref, hb_ref, wp_ref, sc_ref, y_ref, d_ref, d_s, yr_s):
        i = pl.program_id(0)
        rows = lax.broadcasted_iota(jnp.int32, (SUBLANES, pg), 0)

        def chunk(r0, _):
            rs = pl.ds(r0, SUBLANES)
            for g, w in enumerate(POOL_WINDOWS):
                cols = pl.ds(g * pg, pg)
                prv, cur, nxt = _neighbour_chunks(u_ref, up_ref, un_ref, r0, t, cols, i == 0, i == n_tiles - 1)
                tot = cur
                for o in range(-(w // 2), w // 2):
                    if o != 0:
                        tot = tot + _shift_rows(prv, cur, nxt, o, rows)
                cnt = _window_counts(r0, i, t, s, w // 2, (SUBLANES, pg))
                d_s[rs, cols] = tot / cnt - cur
            gate, _ = _gelu_and_grad(gate_ref[rs, :])
            yr_s[rs, :] = (hf_ref[rs, :] + hb_ref[rs, :]) * gate

        _chunk_loop(t, chunk)
        y_ref[:, pl.ds(p, c)] = yr_s[...].astype(BF16)
        d_ref[...] = d_s[...].astype(BF16)
        for g in range(N_POOL_GROUPS):
            cols = pl.ds(g * pg, pg)
            out = jnp.dot(d_s[:, cols].astype(BF16), wp_ref[g], preferred_element_type=F32)
            y_ref[:, cols] = (out * sc_ref[:, cols]).astype(BF16)

    return pl.pallas_call(
        body, name="pool_combine", grid=(n_tiles,),
        in_specs=_halo_specs(t, s, p, 0) + [
            pl.BlockSpec((t, c), lambda i: (i, 2)),
            pl.BlockSpec((t, c), lambda i: (i, 0)), pl.BlockSpec((t, c), lambda i: (i, 0)),
            pl.BlockSpec((N_POOL_GROUPS, pg, pg), lambda i: (0, 0, 0)), pl.BlockSpec((1, p), lambda i: (0, 0))],
        out_specs=[pl.BlockSpec((t, p + c), lambda i: (i, 0)), pl.BlockSpec((t, p), lambda i: (i, 0))],
        out_shape=[jax.ShapeDtypeStruct((s, p + c), BF16), jax.ShapeDtypeStruct((s, p), BF16)],
        scratch_shapes=[pltpu.VMEM((t, p), F32), pltpu.VMEM((t, c), F32)],
        compiler_params=_params("arbitrary"),
    )(proj, proj, proj, proj, h_f, h_b, w_pool, pool_scale)


def _layer_norm_rows(z, g, b):
    mu = jnp.mean(z, axis=-1, keepdims=True)
    zc = z - mu
    var = jnp.mean(zc * zc, axis=-1, keepdims=True)
    rstd = lax.rsqrt(var + LN_EPS)
    xh = zc * rstd
    return xh, rstd, xh * g + b


def _layer_norm_bwd_rows(dx, xh, rstd, g):
    dxh = dx * g
    m1 = jnp.mean(dxh, axis=-1, keepdims=True)
    m2 = jnp.mean(dxh * xh, axis=-1, keepdims=True)
    return rstd * (dxh - m1 - xh * m2)


def _out_ln1_call(y, w_out, x, g1, b1, dep):
    s, d = x.shape
    tm = min(SEQ_TILE, s)

    def body(y_ref, w_ref, x_ref, g_ref, b_ref, dep_ref, xh_ref, x1b_ref, rstd_ref, acc_s, x1_s):
        acc_s[...] = jnp.dot(y_ref[...], w_ref[...], preferred_element_type=F32)
        g, b = g_ref[...], b_ref[...]

        def chunk(r0, _):
            rs = pl.ds(r0, SUBLANES)
            xh, rstd, x1 = _layer_norm_rows(ALPHA * x_ref[rs, :] + acc_s[rs, :], g, b)
            xh_ref[rs, :] = xh
            x1_s[rs, :] = x1
            rstd_ref[rs, :] = rstd

        _chunk_loop(tm, chunk, unroll=LN_UNROLL)
        x1b_ref[...] = x1_s[...].astype(BF16)

    return pl.pallas_call(
        body, name="out_ln1", grid=(s // tm,),
        in_specs=[pl.BlockSpec((tm, d), lambda i: (i, 0)), pl.BlockSpec((d, d), lambda i: (0, 0)),
                  pl.BlockSpec((tm, d), lambda i: (i, 0)),
                  pl.BlockSpec((1, d), lambda i: (0, 0)), pl.BlockSpec((1, d), lambda i: (0, 0)), ANY],
        out_specs=[pl.BlockSpec((tm, d), lambda i: (i, 0)), pl.BlockSpec((tm, d), lambda i: (i, 0)),
                   pl.BlockSpec((tm, 1), lambda i: (i, 0))],
        out_shape=[jax.ShapeDtypeStruct((s, d), F32), jax.ShapeDtypeStruct((s, d), BF16), jax.ShapeDtypeStruct((s, 1), F32)],
        scratch_shapes=[pltpu.VMEM((tm, d), F32), pltpu.VMEM((tm, d), F32)],
        compiler_params=_params("arbitrary"),
    )(y, w_out, x, g1, b1, dep)


def _mlp_in_call(x1b, w1, dep, done):
    s, d = x1b.shape
    n, _, f4 = w1.shape
    tm = min(MM_TILE, s)
    tn = min(1024, f4)
    per = f4 // tn
    blocks = n * per // 2
    first = 0 if done is None else blocks
    extra = [] if done is None else list(done)

    def body(x_ref, w_ref, dep_ref, *rest):
        r_ref, q_ref = rest[-2:]
        r = jnp.maximum(jnp.dot(x_ref[...], w_ref[...], preferred_element_type=F32), 0.0)
        r_ref[...] = r.astype(BF16)
        q_ref[...] = (r * r).astype(BF16)

    out_spec = pl.BlockSpec((tm, tn), lambda j, i: (i, first + j))
    return pl.pallas_call(
        body, name="mlp_in" if done is None else "mlp_in_rest", grid=(blocks, s // tm),
        in_specs=[pl.BlockSpec((tm, d), lambda j, i: (i, 0)),
                  pl.BlockSpec((None, d, tn), lambda j, i: ((first + j) // per, 0, (first + j) % per)), ANY] + [ANY] * len(extra),
        out_specs=[out_spec, out_spec],
        out_shape=[jax.ShapeDtypeStruct((s, n * f4), BF16), jax.ShapeDtypeStruct((s, n * f4), BF16)],
        input_output_aliases={3: 0, 4: 1} if extra else {},
        compiler_params=_params("arbitrary", "arbitrary"),
    )(x1b, w1, dep, *extra)


def _mlp_out_ln2_call(hsq, w2, xh1, g1, b1, g2, b2, target):
    s, f = hsq.shape
    d = w2.shape[1]
    tm = min(MM_TILE, s)
    tk = min(LN_MM_K, f)
    nk = f // tk

    def body(h_ref, w_ref, xh1_ref, g1_ref, b1_ref, g2_ref, b2_ref, t_ref,
             dz_ref, dzb_ref, loss_ref, dg_ref, db_ref, acc_s):
        i, k = pl.program_id(0), pl.program_id(1)

        @pl.when((i == 0) & (k == 0))
        def _():
            loss_ref[...] = jnp.zeros_like(loss_ref)
            dg_ref[...] = jnp.zeros_like(dg_ref)
            db_ref[...] = jnp.zeros_like(db_ref)

        @pl.when(k == 0)
        def _():
            acc_s[...] = jnp.zeros_like(acc_s)

        acc_s[...] += jnp.dot(h_ref[...], w_ref[...], preferred_element_type=F32)

        @pl.when(k == nk - 1)
        def _():
            g1, b1, g2, b2 = g1_ref[...], b1_ref[...], g2_ref[...], b2_ref[...]

            def chunk(r0, _):
                rs = pl.ds(r0, SUBLANES)
                x1 = xh1_ref[rs, :] * g1 + b1
                xh2, rstd, x2 = _layer_norm_rows(ALPHA * x1 + acc_s[rs, :], g2, b2)
                diff = x2 - t_ref[rs, :]
                loss_ref[...] += diff * diff
                dx2 = diff * (1.0 / d)
                dg_ref[...] += dx2 * xh2
                db_ref[...] += dx2
                dz = _layer_norm_bwd_rows(dx2, xh2, rstd, g2)
                dz_ref[rs, :] = dz

            _chunk_loop(tm, chunk, unroll=LN_UNROLL)
            dzb_ref[...] = dz_ref[...].astype(BF16)

    row = lambda i, k: (i, 0)
    vec = lambda i, k: (0, 0)
    return pl.pallas_call(
        body, name="mlp_out_ln2", grid=(s // tm, nk),
        in_specs=[pl.BlockSpec((tm, tk), lambda i, k: (i, k)), pl.BlockSpec((tk, d), lambda i, k: (k, 0)),
                  pl.BlockSpec((tm, d), row), pl.BlockSpec((1, d), vec), pl.BlockSpec((1, d), vec),
                  pl.BlockSpec((1, d), vec), pl.BlockSpec((1, d), vec), pl.BlockSpec((tm, d), row)],
        out_specs=[pl.BlockSpec((tm, d), row), pl.BlockSpec((tm, d), row),
                   pl.BlockSpec((SUBLANES, d), vec), pl.BlockSpec((SUBLANES, d), vec), pl.BlockSpec((SUBLANES, d), vec)],
        out_shape=[jax.ShapeDtypeStruct((s, d), F32), jax.ShapeDtypeStruct((s, d), BF16),
                   jax.ShapeDtypeStruct((SUBLANES, d), F32), jax.ShapeDtypeStruct((SUBLANES, d), F32),
                   jax.ShapeDtypeStruct((SUBLANES, d), F32)],
        scratch_shapes=[pltpu.VMEM((tm, d), F32)],
        compiler_params=_params("arbitrary", "arbitrary"),
    )(hsq, w2, xh1, g1, b1, g2, b2, target)


def _half_grad_call(a, b, half, row_sharded, init, name, dep):
    s, m = a.shape
    n = b.shape[1]
    if row_sharded:
        rows, cols = m // (2 * N_CHIPS), n
        tm = min(1024, rows)
        per = rows // tm
        tn = min(1024, cols)
        n_i, n_j = N_CHIPS * per, cols // tn
        a_block = lambda i, h: ((i // per) * 2 + h) * per + i % per
        out_block = lambda i, j: (i // per, i % per, j)
    else:
        rows, cols = m // 2, n // N_CHIPS
        tm = min(1024, rows)
        per = rows // tm
        tn = cols if cols % 1024 else 1024
        per_n = cols // tn
        n_i, n_j = per, N_CHIPS * per_n
        a_block = lambda i, h: h * per + i
        out_block = lambda i, j: (j // per_n, i, j % per_n)
    tk = min(2048, s)
    if row_sharded and tm < 1024 and s * n * 2 <= RESIDENT_OPERAND_BYTES:
        tk, tn, n_j = s, n, 1
    has_init = init is not None

    def body(half_ref, a_ref, b_ref, *rest):
        o_ref = rest[-1]

        @pl.when(pl.program_id(2) == 0)
        def _():
            o_ref[...] = rest[0][...] if has_init else jnp.zeros_like(o_ref)

        o_ref[...] += lax.dot_general(a_ref[...], b_ref[...], (((0,), (0,)), ((), ())), preferred_element_type=F32)

    out_spec = pl.BlockSpec((None, tm, tn), lambda i, j, k, h: out_block(i, j))
    in_specs = [pl.BlockSpec((tk, tm), lambda i, j, k, h: (k, a_block(i, h[0]))),
                pl.BlockSpec((tk, tn), lambda i, j, k, h: (k, j))]
    args = [a, b]
    if has_init:
        in_specs.append(out_spec)
        args.append(init)
    in_specs.append(ANY)
    args.append(dep)
    return pl.pallas_call(
        body, name=name,
        grid_spec=pltpu.PrefetchScalarGridSpec(num_scalar_prefetch=1, grid=(n_i, n_j, s // tk), in_specs=in_specs,
                                               out_specs=out_spec),
        out_shape=jax.ShapeDtypeStruct((N_CHIPS, rows, cols), F32),
        compiler_params=_params("arbitrary", "arbitrary", "arbitrary"),
    )(half, *args)


def _dhsq_call(dzb, w2, r, dep):
    s, d = dzb.shape
    f = w2.shape[0]
    tm = min(MM_TILE, s)
    tn = min(1024, f)

    def body(dz_ref, w_ref, r_ref, dep_ref, o_ref):
        dh = lax.dot_general(dz_ref[...], w_ref[...], (((1,), (1,)), ((), ())), preferred_element_type=F32)
        o_ref[...] = (dh * (2.0 * r_ref[...].astype(F32))).astype(BF16)

    return pl.pallas_call(
        body, name="mlp_dpre", grid=(f // tn, s // tm),
        in_specs=[pl.BlockSpec((tm, d), lambda j, i: (i, 0)), pl.BlockSpec((tn, d), lambda j, i: (j, 0)),
                  pl.BlockSpec((tm, tn), lambda j, i: (i, j)), ANY],
        out_specs=pl.BlockSpec((tm, tn), lambda j, i: (i, j)),
        out_shape=jax.ShapeDtypeStruct((s, f), BF16),
        compiler_params=_params("arbitrary", "arbitrary"),
    )(dzb, w2, r, dep)


def _dx1_ln1_bwd_call(dpre, w1, dz2, xh1, rstd1, g1, dep):
    s, f = dpre.shape
    n, d, f4 = w1.shape
    tm = min(MM_TILE, s)
    tk = min(LN_MM_K, f4)
    per = f4 // tk
    nk = n * per

    def body(dp_ref, w_ref, dz2_ref, xh_ref, rstd_ref, g_ref, dep_ref, dz_ref, dzb_ref, dg_ref, db_ref, acc_s):
        i, k = pl.program_id(0), pl.program_id(1)

        @pl.when((i == 0) & (k == 0))
        def _():
            dg_ref[...] = jnp.zeros_like(dg_ref)
            db_ref[...] = jnp.zeros_like(db_ref)

        @pl.when(k == 0)
        def _():
            acc_s[...] = jnp.zeros_like(acc_s)

        acc_s[...] += lax.dot_general(dp_ref[...], w_ref[...], (((1,), (1,)), ((), ())), preferred_element_type=F32)

        @pl.when(k == nk - 1)
        def _():
            g = g_ref[...]

            def chunk(r0, _):
                rs = pl.ds(r0, SUBLANES)
                dx1 = acc_s[rs, :] + ALPHA * dz2_ref[rs, :]
                xh = xh_ref[rs, :]
                dg_ref[...] += dx1 * xh
                db_ref[...] += dx1
                dz = _layer_norm_bwd_rows(dx1, xh, rstd_ref[rs, :], g)
                dz_ref[rs, :] = dz

            _chunk_loop(tm, chunk, unroll=LN_UNROLL)
            dzb_ref[...] = dz_ref[...].astype(BF16)

    row = lambda i, k: (i, 0)
    vec = lambda i, k: (0, 0)
    return pl.pallas_call(
        body, name="dx1_ln1_bwd", grid=(s // tm, nk),
        in_specs=[pl.BlockSpec((tm, tk), lambda i, k: (i, k)),
                  pl.BlockSpec((None, d, tk), lambda i, k: (k // per, 0, k % per)),
                  pl.BlockSpec((tm, d), row), pl.BlockSpec((tm, d), row), pl.BlockSpec((tm, 1), row),
                  pl.BlockSpec((1, d), vec), ANY],
        out_specs=[pl.BlockSpec((tm, d), row), pl.BlockSpec((tm, d), row),
                   pl.BlockSpec((SUBLANES, d), vec), pl.BlockSpec((SUBLANES, d), vec)],
        out_shape=[jax.ShapeDtypeStruct((s, d), F32), jax.ShapeDtypeStruct((s, d), BF16),
                   jax.ShapeDtypeStruct((SUBLANES, d), F32), jax.ShapeDtypeStruct((SUBLANES, d), F32)],
        scratch_shapes=[pltpu.VMEM((tm, d), F32)],
        compiler_params=_params("arbitrary", "arbitrary"),
    )(dpre, w1, dz2, xh1, rstd1, g1, dep)


def _dy_call(dzb, w_out, dep):
    s, d = dzb.shape
    e = w_out.shape[0]
    tm = min(MM_TILE, s)

    def body(dz_ref, w_ref, dep_ref, o_ref):
        o_ref[...] = lax.dot_general(dz_ref[...], w_ref[...], (((1,), (1,)), ((), ())), preferred_element_type=F32)

    return pl.pallas_call(
        body, name="dy", grid=(s // tm,),
        in_specs=[pl.BlockSpec((tm, d), lambda i: (i, 0)), pl.BlockSpec((e, d), lambda i: (0, 0)), ANY],
        out_specs=pl.BlockSpec((tm, e), lambda i: (i, 0)),
        out_shape=jax.ShapeDtypeStruct((s, e), F32),
        compiler_params=_params("arbitrary"),
    )(dzb, w_out, dep)


def _mixer_bwd_call(dy, d_pool, proj, h_f, h_b, w_pool, pool_scale, p):
    s = dy.shape[0]
    c = h_f.shape[1]
    pg = p // N_POOL_GROUPS
    t = min(SEQ_TILE, s)
    n_tiles = s // t

    def body(dyp_ref, dyr_ref, d_ref, gate_ref, hf_ref, hb_ref, wp_ref, sc_ref,
             e_ref, dh_ref, dgate_ref, dwp_ref, dsc_ref, dd_s):
        i = pl.program_id(0)

        @pl.when(i == 0)
        def _():
            dwp_ref[...] = jnp.zeros_like(dwp_ref)
            dsc_ref[...] = jnp.zeros_like(dsc_ref)

        for g in range(N_POOL_GROUPS):
            cols = pl.ds(g * pg, pg)
            dg = d_ref[:, cols]
            out = jnp.dot(dg, wp_ref[g], preferred_element_type=F32)
            dyp = dyp_ref[:, cols]
            prod = dyp * out
            dsc_ref[:, cols] += jnp.sum(prod.reshape(t // SUBLANES, SUBLANES, pg), axis=0)
            dout = (dyp * sc_ref[:, cols]).astype(BF16)
            dwp_ref[g] += lax.dot_general(dg, dout, (((0,), (0,)), ((), ())), preferred_element_type=F32)
            dd_s[:, cols] = lax.dot_general(dout, wp_ref[g], (((1,), (1,)), ((), ())), preferred_element_type=F32)

        def chunk(r0, _):
            rs = pl.ds(r0, SUBLANES)
            for g, w in enumerate(POOL_WINDOWS):
                cols = pl.ds(g * pg, pg)
                cnt = _window_counts(r0, i, t, s, w // 2, (SUBLANES, pg))
                e_ref[rs, cols] = dd_s[rs, cols] / cnt
            gate, dgate = _gelu_and_grad(gate_ref[rs, :])
            dyr = dyr_ref[rs, :]
            dh_ref[rs, :] = dyr * gate
            dd_s[rs, :] = dyr * (hf_ref[rs, :] + hb_ref[rs, :]) * dgate

        _chunk_loop(t, chunk)
        dgate_ref[...] = dd_s[...].astype(BF16)

    tile = lambda i: (i, 0)
    return pl.pallas_call(
        body, name="mixer_bwd", grid=(n_tiles,),
        in_specs=[pl.BlockSpec((t, p), tile), pl.BlockSpec((t, c), lambda i: (i, 1)), pl.BlockSpec((t, p), tile),
                  pl.BlockSpec((t, c), lambda i: (i, 2)), pl.BlockSpec((t, c), tile), pl.BlockSpec((t, c), tile),
                  pl.BlockSpec((N_POOL_GROUPS, pg, pg), lambda i: (0, 0, 0)), pl.BlockSpec((1, p), lambda i: (0, 0))],
        out_specs=[pl.BlockSpec((t, p), tile), pl.BlockSpec((t, c), tile), pl.BlockSpec((t, c), tile),
                   pl.BlockSpec((N_POOL_GROUPS, pg, pg), lambda i: (0, 0, 0)), pl.BlockSpec((SUBLANES, p), lambda i: (0, 0))],
        out_shape=[jax.ShapeDtypeStruct((s, p), F32), jax.ShapeDtypeStruct((s, c), F32), jax.ShapeDtypeStruct((s, c), BF16),
                   jax.ShapeDtypeStruct((N_POOL_GROUPS, pg, pg), F32), jax.ShapeDtypeStruct((SUBLANES, p), F32)],
        scratch_shapes=[pltpu.VMEM((t, p), F32)],
        compiler_params=_params("arbitrary"),
    )(dy, dy, d_pool, proj, h_f, h_b, w_pool, pool_scale)


def _scan_bwd_call(xc, dh, h_dir, gates, dxc_prev, wa, wi, lam, reverse, dep):
    s, c = xc.shape
    heads = c // LRU_HEAD
    t = min(SEQ_TILE, s)
    n_tiles = s // t
    per = t // SUBLANES
    last_blk = s // SUBLANES - 1
    tile = (lambda i: (i, 0)) if reverse else (lambda i: (n_tiles - 1 - i, 0))
    if reverse:
        halo = lambda i: (jnp.minimum((i + 1) * per, last_blk), 0)
    else:
        halo = lambda i: (jnp.maximum((n_tiles - 1 - i) * per - 1, 0), 0)
    whole2 = lambda i: (0, 0)
    whole3 = lambda i: (0, 0, 0)
    has_prev = dxc_prev is not None
    n_in = 11 + int(has_prev) + 1

    def body(*refs):
        xc_ref, dh_ref, h_ref, hh_ref, r_ref, ig_ref, a_ref, mult_ref = refs[:8]
        prev_ref = refs[8] if has_prev else None
        wa_ref, wi_ref, lam_ref = refs[n_in - 4:n_in - 1]
        dxc_ref, dwa_ref, dwi_ref, dba_ref, dbi_ref, dsp_ref = refs[n_in:n_in + 6]
        pr_s, pi_s, carry_s = refs[n_in + 6:]
        step = pl.program_id(0)
        tile_idx = step if reverse else n_tiles - 1 - step

        @pl.when(step == 0)
        def _():
            carry_s[...] = jnp.zeros_like(carry_s)
            dwa_ref[...] = jnp.zeros_like(dwa_ref)
            dwi_ref[...] = jnp.zeros_like(dwi_ref)
            dba_ref[...] = jnp.zeros_like(dba_ref)
            dbi_ref[...] = jnp.zeros_like(dbi_ref)
            dsp_ref[...] = jnp.zeros_like(dsp_ref)

        sp = _softplus(-lam_ref[...])
        rows = lax.broadcasted_iota(jnp.int32, (SUBLANES, c), 0)

        def chunk(r0, u_in):
            rs = pl.ds(r0, SUBLANES)
            xcv = xc_ref[rs, :]
            r, ig, a, mult = r_ref[rs, :], ig_ref[rs, :], a_ref[rs, :], mult_ref[rs, :]
            dhv = dh_ref[rs, :]
            u = _scan_chunk(a, a * dhv, u_in, rows, not reverse)
            if reverse:
                gt = dhv + jnp.where(rows >= 1, pltpu.roll(u, 1, 0), u_in)
                u_out = u[SUBLANES - 1:SUBLANES, :]
            else:
                gt = dhv + jnp.where(rows < SUBLANES - 1, pltpu.roll(u, SUBLANES - 1, 0), u_in)
                u_out = u[0:1, :]
            cur = h_ref[rs, :]
            if reverse:
                after = h_ref[pl.ds(pl.multiple_of(jnp.minimum(r0 + SUBLANES, t - SUBLANES), SUBLANES), SUBLANES), :]
                edge = jnp.where(tile_idx == n_tiles - 1, 0.0, hh_ref[...])
                nxt = jnp.where(r0 == t - SUBLANES, edge, after)
                hs = _shift_rows(cur, cur, nxt, 1, rows)
            else:
                before = h_ref[pl.ds(pl.multiple_of(jnp.maximum(r0 - SUBLANES, 0), SUBLANES), SUBLANES), :]
                edge = jnp.where(tile_idx == 0, 0.0, hh_ref[...])
                prv = jnp.where(r0 == 0, edge, before)
                hs = _shift_rows(prv, cur, cur, -1, rows)
            gx = gt * xcv
            dmult = gx * ig
            di = gx * mult
            dlog_a = (gt * hs) * a - dmult * (a * a) / mult
            dr = dlog_a * (-RG_C * sp)
            dsp_ref[...] += dlog_a * (-RG_C * r)
            dpr = dr * r * (1.0 - r)
            dpi = di * ig * (1.0 - ig)
            dba_ref[...] += dpr
            dbi_ref[...] += dpi
            direct = gt * mult * ig
            if has_prev:
                direct = direct + prev_ref[rs, :]
            dxc_ref[rs, :] = direct
            pr_s[rs, :] = dpr
            pi_s[rs, :] = dpi
            return u_out

        carry_s[...] = _chunk_loop(t, chunk, carry_s[...], unroll=2, descending=not reverse)

        for h in range(heads):
            cs = pl.ds(h * LRU_HEAD, LRU_HEAD)
            xb = xc_ref[:, cs].astype(BF16)
            dprb = pr_s[:, cs].astype(BF16)
            dpib = pi_s[:, cs].astype(BF16)
            dwa_ref[h] += lax.dot_general(xb, dprb, (((0,), (0,)), ((), ())), preferred_element_type=F32)
            dwi_ref[h] += lax.dot_general(xb, dpib, (((0,), (0,)), ((), ())), preferred_element_type=F32)
            dxc_ref[:, cs] += (
                lax.dot_general(dprb, wa_ref[h], (((1,), (1,)), ((), ())), preferred_element_type=F32)
                + lax.dot_general(dpib, wi_ref[h], (((1,), (1,)), ((), ())), preferred_element_type=F32))

    tile_spec = pl.BlockSpec((t, c), tile)
    in_specs = [tile_spec, tile_spec, tile_spec, pl.BlockSpec((SUBLANES, c), halo)] + [tile_spec] * 4
    args = [xc, dh, h_dir, h_dir, *gates]
    if has_prev:
        in_specs.append(tile_spec)
        args.append(dxc_prev)
    in_specs += [pl.BlockSpec((heads, LRU_HEAD, LRU_HEAD), whole3), pl.BlockSpec((heads, LRU_HEAD, LRU_HEAD), whole3),
                 pl.BlockSpec((1, c), whole2), ANY]
    args += [wa, wi, lam, dep]
    assert len(args) == n_in
    return pl.pallas_call(
        body, name="scan_bwd_rev" if reverse else "scan_bwd", grid=(n_tiles,),
        in_specs=in_specs,
        out_specs=[tile_spec,
                   pl.BlockSpec((heads, LRU_HEAD, LRU_HEAD), whole3), pl.BlockSpec((heads, LRU_HEAD, LRU_HEAD), whole3),
                   pl.BlockSpec((SUBLANES, c), whole2), pl.BlockSpec((SUBLANES, c), whole2), pl.BlockSpec((SUBLANES, c), whole2)],
        out_shape=[jax.ShapeDtypeStruct((s, c), F32),
                   jax.ShapeDtypeStruct((heads, LRU_HEAD, LRU_HEAD), F32), jax.ShapeDtypeStruct((heads, LRU_HEAD, LRU_HEAD), F32),
                   jax.ShapeDtypeStruct((SUBLANES, c), F32), jax.ShapeDtypeStruct((SUBLANES, c), F32),
                   jax.ShapeDtypeStruct((SUBLANES, c), F32)],
        scratch_shapes=[pltpu.VMEM((t, c), F32), pltpu.VMEM((t, c), F32), pltpu.VMEM((1, c), F32)],
        compiler_params=_params("arbitrary"),
    )(*args)


def _dproj_call(e_pool, dxc, proj, dgate, conv_w, p):
    s, c = dxc.shape
    pg = p // N_POOL_GROUPS
    t = min(SEQ_TILE, s)
    n_tiles = s // t

    def body(e_ref, ep_ref, en_ref, dx_ref, dxp_ref, dxn_ref, u_ref, up_ref, un_ref, dgate_ref, w_ref,
             dproj_ref, dcw_ref, dcb_ref, st_s):
        i = pl.program_id(0)
        first, last = i == 0, i == n_tiles - 1

        @pl.when(first)
        def _():
            dcw_ref[...] = jnp.zeros_like(dcw_ref)
            dcb_ref[...] = jnp.zeros_like(dcb_ref)

        rows_p = lax.broadcasted_iota(jnp.int32, (SUBLANES, pg), 0)
        rows_c = lax.broadcasted_iota(jnp.int32, (SUBLANES, c), 0)
        w = w_ref[...]

        def chunk(r0, _):
            rs = pl.ds(r0, SUBLANES)
            for g, win in enumerate(POOL_WINDOWS):
                cols = pl.ds(g * pg, pg)
                prv, cur, nxt = _neighbour_chunks(e_ref, ep_ref, en_ref, r0, t, cols, first, last)
                tot = cur
                for o in range(-(win // 2) + 1, win // 2 + 1):
                    if o != 0:
                        tot = tot + _shift_rows(prv, cur, nxt, o, rows_p)
                cnt = _window_counts(r0, i, t, s, win // 2, (SUBLANES, pg))
                st_s[rs, cols] = tot - cur * cnt
            prv, cur, nxt = _neighbour_chunks(dx_ref, dxp_ref, dxn_ref, r0, t, slice(None), first, last)
            du = w[1:2] * cur
            du += w[0:1] * _shift_rows(prv, cur, nxt, 1, rows_c)
            du += w[2:3] * _shift_rows(prv, cur, nxt, -1, rows_c)
            du += w[3:4] * _shift_rows(prv, cur, nxt, -2, rows_c)
            st_s[rs, pl.ds(p, c)] = du
            uprv, ucur, unxt = _neighbour_chunks(u_ref, up_ref, un_ref, r0, t, slice(None), first, last)
            dcb_ref[...] += cur
            for j, o in enumerate((-1, 0, 1, 2)):
                dcw_ref[j] += cur * _shift_rows(uprv, ucur, unxt, o, rows_c)

        _chunk_loop(t, chunk)
        dproj_ref[:, pl.ds(0, p + c)] = st_s[...].astype(BF16)
        dproj_ref[:, pl.ds(p + c, c)] = dgate_ref[...]

    return pl.pallas_call(
        body, name="dproj", grid=(n_tiles,),
        in_specs=_halo_specs(t, s, p, 0) + _halo_specs(t, s, c, 0) + _halo_specs(t, s, c, 1) + [
            pl.BlockSpec((t, c), lambda i: (i, 0)), pl.BlockSpec((4, c), lambda i: (0, 0))],
        out_specs=[pl.BlockSpec((t, p + 2 * c), lambda i: (i, 0)),
                   pl.BlockSpec((4, SUBLANES, c), lambda i: (0, 0, 0)), pl.BlockSpec((SUBLANES, c), lambda i: (0, 0))],
        out_shape=[jax.ShapeDtypeStruct((s, p + 2 * c), BF16), jax.ShapeDtypeStruct((4, SUBLANES, c), F32),
                   jax.ShapeDtypeStruct((SUBLANES, c), F32)],
        scratch_shapes=[pltpu.VMEM((t, p + c), F32)],
        compiler_params=_params("arbitrary"),
    )(e_pool, e_pool, e_pool, dxc, dxc, dxc, proj, proj, proj, dgate, conv_w)


def _dx_call(dproj, w_in, dz1, dep):
    s, e = dproj.shape
    n, d, e4 = w_in.shape
    tm = min(MM_TILE, s)

    def body(dp_ref, w_hbm, dz_ref, dep_ref, o_ref, w_s, sems):
        @pl.when(pl.program_id(0) == 0)
        def _():
            copies = [pltpu.make_async_copy(w_hbm.at[k], w_s.at[:, pl.ds(k * e4, e4)], sems.at[k]) for k in range(n)]
            for cp in copies:
                cp.start()
            for cp in copies:
                cp.wait()

        o_ref[...] = ALPHA * dz_ref[...] + lax.dot_general(
            dp_ref[...], w_s[...], (((1,), (1,)), ((), ())), preferred_element_type=F32)

    return pl.pallas_call(
        body, name="grad_x", grid=(s // tm,),
        in_specs=[pl.BlockSpec((tm, e), lambda i: (i, 0)), ANY, pl.BlockSpec((tm, d), lambda i: (i, 0)), ANY],
        out_specs=pl.BlockSpec((tm, d), lambda i: (i, 0)),
        out_shape=jax.ShapeDtypeStruct((s, d), F32),
        scratch_shapes=[pltpu.VMEM((d, e), BF16), pltpu.SemaphoreType.DMA((n,))],
        compiler_params=_params("arbitrary"),
    )(dproj, w_in, dz1, dep)


def _row_tile(rows, cols, n_arrays):
    limit = max(SUBLANES, ELT_BLOCK_BYTES // (4 * cols * max(1, n_arrays // 4)))
    best = SUBLANES
    for cand in range(SUBLANES, min(rows, limit) + 1, SUBLANES):
        if rows % cand == 0:
            best = cand
    return best if rows % SUBLANES == 0 else rows


def _cast_to_slot_call(a, idx, dtype, name, dep):
    rows, cols = a.shape
    tr = _row_tile(rows, cols, 2)
    extra = [] if dep is None else [dep]

    def body(idx_ref, a_ref, *rest):
        rest[-1][...] = a_ref[...].astype(dtype)

    return pl.pallas_call(
        body, name=name,
        grid_spec=pltpu.PrefetchScalarGridSpec(
            num_scalar_prefetch=1, grid=(rows // tr,),
            in_specs=[pl.BlockSpec((tr, cols), lambda i, idx_ref: (i, 0))] + [ANY] * len(extra),
            out_specs=pl.BlockSpec((None, tr, cols), lambda i, idx_ref: (idx_ref[1], i, 0))),
        out_shape=jax.ShapeDtypeStruct((N_CHIPS, rows, cols), dtype),
        compiler_params=_params("arbitrary"),
    )(idx, a, *extra)


def _add_half_call(g, recv, idx, name):
    _, rows, cols = g.shape
    tr = _row_tile(rows, cols, 3)

    def body(idx_ref, g_ref, r_ref, o_ref):
        o_ref[...] = g_ref[...] + r_ref[...]

    return pl.pallas_call(
        body, name=name,
        grid_spec=pltpu.PrefetchScalarGridSpec(
            num_scalar_prefetch=1, grid=(rows // tr,),
            in_specs=[pl.BlockSpec((None, tr, cols), lambda i, idx_ref: (idx_ref[0], i, 0)),
                      pl.BlockSpec((tr, cols), lambda i, idx_ref: (i, 0))],
            out_specs=pl.BlockSpec((None, tr, cols), lambda i, idx_ref: (idx_ref[1], i, 0))),
        out_shape=jax.ShapeDtypeStruct((N_CHIPS, rows, cols), F32),
        compiler_params=_params("arbitrary"),
    )(idx, g, recv)


def _sum_chips_call(own, recv, idx, name):
    _, rows, cols = recv.shape
    tr = _row_tile(rows, cols, 5)
    out_spec = pl.BlockSpec((None, tr, cols), lambda i, idx_ref: (idx_ref[0], i, 0))
    if own is None:
        def body(idx_ref, r_ref, o_ref):
            o_ref[...] = ((r_ref[0] + r_ref[1]) + r_ref[2]) + r_ref[3]
        in_specs = [pl.BlockSpec((N_CHIPS, tr, cols), lambda i, idx_ref: (0, i, 0))]
        args = (recv,)
    else:
        def body(idx_ref, p_ref, r_ref, o_ref):
            o_ref[...] = ((p_ref[...] + r_ref[0]) + r_ref[1]) + r_ref[2]
        in_specs = [pl.BlockSpec((None, tr, cols), lambda i, idx_ref: (idx_ref[1], i, 0)),
                    pl.BlockSpec((N_CHIPS - 1, tr, cols), lambda i, idx_ref: (0, i, 0))]
        args = (own, recv)
    return pl.pallas_call(
        body, name=name,
        grid_spec=pltpu.PrefetchScalarGridSpec(num_scalar_prefetch=1, grid=(rows // tr,), in_specs=in_specs, out_specs=out_spec),
        out_shape=jax.ShapeDtypeStruct((2, rows, cols), F32),
        compiler_params=_params("arbitrary"),
    )(idx, *args)


def _adamw_call(g, w, m, v, name):
    rows, cols = w.shape
    tr = _row_tile(rows, cols, 8)

    def body(g_ref, w_ref, m_ref, v_ref, go_ref, d_ref, mo_ref, vo_ref):
        gv = g_ref[...]
        go_ref[...] = gv
        mn = ADAM_B1 * m_ref[...] + (1.0 - ADAM_B1) * gv
        vn = ADAM_B2 * v_ref[...] + (1.0 - ADAM_B2) * (gv * gv)
        m_hat = mn / (1.0 - ADAM_B1 ** ADAM_STEP)
        v_hat = vn / (1.0 - ADAM_B2 ** ADAM_STEP)
        d_ref[...] = -ADAM_LR * (m_hat / (jnp.sqrt(v_hat) + ADAM_EPS) + ADAM_WD * w_ref[...])
        mo_ref[...] = mn
        vo_ref[...] = vn

    spec = pl.BlockSpec((tr, cols), lambda i: (i, 0))
    shape = jax.ShapeDtypeStruct((rows, cols), F32)
    return pl.pallas_call(
        body, name=name, grid=(rows // tr,),
        in_specs=[spec] * 4, out_specs=[spec] * 4, out_shape=[shape] * 4,
        compiler_params=_params("arbitrary"),
    )(g, w, m, v)


def _mesh_place():
    x, y, c = lax.axis_index("x"), lax.axis_index("y"), lax.axis_index("c")
    chips = [(1 - x, y), (x, 1 - y), (1 - x, 1 - y)]
    return x, y, c, chips


def _remote(src, dst, send_sems, recv_sems, idx, device):
    return pltpu.make_async_remote_copy(src_ref=src, dst_ref=dst, send_sem=send_sems.at[idx], recv_sem=recv_sems.at[idx],
                                        device_id=device, device_id_type=MESH)


HBM_SPEC = pl.BlockSpec(memory_space=pltpu.HBM)
SEM_SPEC = pl.BlockSpec(memory_space=pltpu.SEMAPHORE)
ORDERED_EFFECT = pltpu.SideEffectType.DATAFLOW_SIDE_EFFECTING


def _in_hbm(a):
    return pltpu.with_memory_space_constraint(a, pltpu.HBM)


def _start_copies_call(name, bufs, groups, after=None):
    n, g = len(bufs), len(groups)
    extra = [] if after is None else [after]
    first_out = n + len(extra)

    def body(*refs):
        outs = refs[first_out:first_out + n]
        sems = refs[first_out + n:first_out + n + 2 * g]
        token = refs[first_out + n + 2 * g]
        for i, (which, copies_fn, _) in enumerate(groups):
            for mine, _ in copies_fn([outs[w] for w in which], sems[2 * i], sems[2 * i + 1]):
                mine.start()
        token[...] = jnp.zeros_like(token)

    sem_shapes = [pltpu.SemaphoreType.DMA((cnt,)) for _, _, cnt in groups for _ in range(2)]
    res = pl.pallas_call(
        body, name=name,
        in_specs=[HBM_SPEC] * n + [ANY] * len(extra),
        out_specs=[HBM_SPEC] * n + [SEM_SPEC] * (2 * g) + [pl.BlockSpec(memory_space=pltpu.VMEM)],
        out_shape=[pltpu.HBM(a.shape, a.dtype) for a in bufs] + sem_shapes + [jax.ShapeDtypeStruct((SUBLANES, LANES), F32)],
        input_output_aliases={a: a for a in range(n)},
        compiler_params=pltpu.CompilerParams(has_side_effects=ORDERED_EFFECT),
    )(*[_in_hbm(a) for a in bufs], *extra)
    sems = res[n:n + 2 * g]
    return list(res[:n]), [(sems[2 * i], sems[2 * i + 1]) for i in range(g)], res[n + 2 * g]


def _wait_copies_call(name, bufs, sems, copies_fn, after):
    n = len(bufs)

    def body(*refs):
        ins = refs[:n]
        send_sems, recv_sems = refs[n], refs[n + 1]
        for mine, arriving in copies_fn(list(ins), send_sems, recv_sems):
            arriving.wait_recv()
            mine.wait_send()

    res = pl.pallas_call(
        body, name=name,
        in_specs=[HBM_SPEC] * n + [SEM_SPEC, SEM_SPEC, ANY],
        out_specs=[HBM_SPEC] * n,
        out_shape=[pltpu.HBM(a.shape, a.dtype) for a in bufs],
        input_output_aliases={a: a for a in range(n)},
        compiler_params=pltpu.CompilerParams(has_side_effects=ORDERED_EFFECT),
    )(*bufs, sems[0], sems[1], after)
    return list(res)


def _gather_copies(bufs, send_sems, recv_sems):
    x, y, c, chips = _mesh_place()
    k = 2 * x + y
    out = []
    for a, buf in enumerate(bufs):
        for j, (px, py) in enumerate(chips):
            kj = 2 * px + py
            mine = _remote(buf.at[k, c], buf.at[k, c], send_sems, recv_sems, 3 * a + j, (px, py, c))
            arriving = _remote(buf.at[k, c], buf.at[kj, c], send_sems, recv_sems, 3 * a + j, (px, py, c))
            out.append((mine, arriving))
    return out


def _exchange_copies(n_sharded, n_replicated):
    def copies(bufs, send_sems, recv_sems):
        x, y, c, chips = _mesh_place()
        k = 2 * x + y
        sums, lands = bufs[:n_sharded], bufs[n_sharded:2 * n_sharded]
        repl = bufs[2 * n_sharded:]
        out = []
        for j, (px, py) in enumerate(chips):
            kj = 2 * px + py
            for a in range(n_sharded):
                cp = _remote(sums[a].at[kj], lands[a].at[j], send_sems, recv_sems, 3 * a + j, (px, py, c))
                out.append((cp, cp))
            for a in range(n_replicated):
                idx = 3 * (n_sharded + a) + j
                mine = _remote(repl[a].at[k], repl[a].at[k], send_sems, recv_sems, idx, (px, py, c))
                arriving = _remote(repl[a].at[k], repl[a].at[kj], send_sems, recv_sems, idx, (px, py, c))
                out.append((mine, arriving))
        return out
    return copies


def _sibling_copies(n, halves):
    def copies(bufs, send_sems, recv_sems):
        x, y, c, _ = _mesh_place()
        out = []
        for a in range(n):
            src = bufs[a].at[1 - c] if halves else bufs[a]
            cp = _remote(src, bufs[n + a], send_sems, recv_sems, a, (x, y, 1 - c))
            out.append((cp, cp))
        return out
    return copies


def _forward_copies(bufs, send_sems, recv_sems):
    x, y, c, chips = _mesh_place()
    out = []
    for a, buf in enumerate(bufs):
        for j, (px, py) in enumerate(chips):
            kj = 2 * px + py
            mine = _remote(buf.at[kj, c], buf.at[kj, c], send_sems, recv_sems, 3 * a + j, (x, y, 1 - c))
            arriving = _remote(buf.at[kj, c], buf.at[kj, 1 - c], send_sems, recv_sems, 3 * a + j, (x, y, 1 - c))
            out.append((mine, arriving))
    return out


def _join_copies(bufs, send_sems, recv_sems):
    x, y, c, _ = _mesh_place()
    out = []
    for a, buf in enumerate(bufs):
        mine = _remote(buf.at[c], buf.at[c], send_sems, recv_sems, a, (x, y, 1 - c))
        arriving = _remote(buf.at[c], buf.at[1 - c], send_sems, recv_sems, a, (x, y, 1 - c))
        out.append((mine, arriving))
    return out


def _forward_to_sibling_call(bufs, name):
    n = len(bufs)

    def body(*refs):
        ins, outs = refs[:n], refs[n:2 * n]
        send_sems, recv_sems = refs[2 * n:]
        x, y, c, chips = _mesh_place()
        sibling = (x, y, 1 - c)
        sends = []
        for a in range(n):
            for j, (px, py) in enumerate(chips):
                kj = 2 * px + py
                sends.append(_remote(ins[a].at[kj, c], outs[a].at[kj, c], send_sems, recv_sems, 3 * a + j, sibling))
        for cp in sends:
            cp.start()
        for a in range(n):
            for j, (px, py) in enumerate(chips):
                kj = 2 * px + py
                _remote(ins[a].at[kj, c], outs[a].at[kj, 1 - c], send_sems, recv_sems, 3 * a + j, sibling).wait_recv()
        for cp in sends:
            cp.wait_send()

    return pl.pallas_call(
        body, name=name,
        in_specs=[ANY] * n, out_specs=[ANY] * n,
        out_shape=[jax.ShapeDtypeStruct(a.shape, a.dtype) for a in bufs],
        input_output_aliases={a: a for a in range(n)},
        scratch_shapes=[pltpu.SemaphoreType.DMA((3 * n,)), pltpu.SemaphoreType.DMA((3 * n,))],
    )(*bufs)


def _join_halves_call(bufs, name):
    n = len(bufs)

    def body(*refs):
        ins, outs = refs[:n], refs[n:2 * n]
        send_sems, recv_sems = refs[2 * n:]
        x, y, c, _ = _mesh_place()
        sibling = (x, y, 1 - c)
        copies = [_remote(ins[a].at[c], outs[a].at[c], send_sems, recv_sems, a, sibling) for a in range(n)]
        for cp in copies:
            cp.start()
        for a in range(n):
            _remote(ins[a].at[c], outs[a].at[1 - c], send_sems, recv_sems, a, sibling).wait_recv()
        for cp in copies:
            cp.wait_send()

    return pl.pallas_call(
        body, name=name,
        in_specs=[ANY] * n, out_specs=[ANY] * n,
        out_shape=[jax.ShapeDtypeStruct(a.shape, a.dtype) for a in bufs],
        input_output_aliases={a: a for a in range(n)},
        scratch_shapes=[pltpu.SemaphoreType.DMA((n,)), pltpu.SemaphoreType.DMA((n,))],
    )(*bufs)


def _pack(arrays, rows_multiple):
    flat = jnp.concatenate([a.reshape(-1) for a in arrays])
    per = LANES * rows_multiple
    padded = -(-flat.shape[0] // per) * per
    flat = jnp.pad(flat, (0, padded - flat.shape[0]))
    return flat.reshape(-1, LANES)


def _unpack(packed, shapes):
    flat = packed.reshape(-1)
    out, at = [], 0
    for shp in shapes:
        size = 1
        for dim in shp:
            size *= dim
        out.append(flat[at:at + size].reshape(shp))
        at += size
    return out


def _halves(a):
    return a.reshape((2, a.shape[0] // 2) + a.shape[1:])


def kernel(x, ln_mix_g, ln_mix_b, w_in, w_pool, pool_scale, conv_w, conv_b, w_rg_a, b_rg_a, w_rg_i, b_rg_i, rg_lambda, w_out, ln_ffn_g, ln_ffn_b, w_mlp_in, w_mlp_out, loss_target, m_ln_mix_g, m_ln_mix_b, m_w_in, m_w_pool, m_pool_scale, m_conv_w, m_conv_b, m_w_rg_a, m_b_rg_a, m_w_rg_i, m_b_rg_i, m_rg_lambda, m_w_out, m_ln_ffn_g, m_ln_ffn_b, m_w_mlp_in, m_w_mlp_out, v_ln_mix_g, v_ln_mix_b, v_w_in, v_w_pool, v_pool_scale, v_conv_w, v_conv_b, v_w_rg_a, v_b_rg_a, v_w_rg_i, v_b_rg_i, v_rg_lambda, v_w_out, v_ln_ffn_g, v_ln_ffn_b, v_w_mlp_in, v_w_mlp_out):
    weights = dict(ln_mix_g=ln_mix_g, ln_mix_b=ln_mix_b, w_in=w_in, w_pool=w_pool, pool_scale=pool_scale, conv_w=conv_w,
                   conv_b=conv_b, w_rg_a=w_rg_a, b_rg_a=b_rg_a, w_rg_i=w_rg_i, b_rg_i=b_rg_i, rg_lambda=rg_lambda,
                   w_out=w_out, ln_ffn_g=ln_ffn_g, ln_ffn_b=ln_ffn_b, w_mlp_in=w_mlp_in, w_mlp_out=w_mlp_out)
    m_in = dict(ln_mix_g=m_ln_mix_g, ln_mix_b=m_ln_mix_b, w_in=m_w_in, w_pool=m_w_pool, pool_scale=m_pool_scale,
                conv_w=m_conv_w, conv_b=m_conv_b, w_rg_a=m_w_rg_a, b_rg_a=m_b_rg_a, w_rg_i=m_w_rg_i, b_rg_i=m_b_rg_i,
                rg_lambda=m_rg_lambda, w_out=m_w_out, ln_ffn_g=m_ln_ffn_g, ln_ffn_b=m_ln_ffn_b, w_mlp_in=m_w_mlp_in,
                w_mlp_out=m_w_mlp_out)
    v_in = dict(ln_mix_g=v_ln_mix_g, ln_mix_b=v_ln_mix_b, w_in=v_w_in, w_pool=v_w_pool, pool_scale=v_pool_scale,
                conv_w=v_conv_w, conv_b=v_conv_b, w_rg_a=v_w_rg_a, b_rg_a=v_b_rg_a, w_rg_i=v_w_rg_i, b_rg_i=v_b_rg_i,
                rg_lambda=v_rg_lambda, w_out=v_w_out, ln_ffn_g=v_ln_ffn_g, ln_ffn_b=v_ln_ffn_b, w_mlp_in=v_w_mlp_in,
                w_mlp_out=v_w_mlp_out)
    names = list(weights)

    xs = x[0]
    tgt = loss_target[0]
    s, d = xs.shape
    p = c = d // 2
    pg = p // N_POOL_GROUPS
    core = lax.axis_index("c")
    shard = 2 * lax.axis_index("x") + lax.axis_index("y")

    idx = jnp.stack([core, shard]).astype(jnp.int32)
    small_shard = _pack([conv_w[0], b_rg_a[0], b_rg_i[0], rg_lambda[0]], 2 * SUBLANES)
    to_gather = [(w_in[0], BF16), (w_out[0], BF16), (w_mlp_in[0], BF16), (w_mlp_out[0], BF16),
                 (w_pool[0].reshape(-1, pg), BF16), (small_shard, F32)]

    def slot_view(i, dep):
        a, dt = to_gather[i]
        sl = _cast_to_slot_call(a, idx, dt, f"gather_slot_{i}", dep)
        return sl.reshape(N_CHIPS, 2, sl.shape[1] // 2, sl.shape[2])

    first, later = (0, 4, 5), (1, 2, 3)
    fly_a, sems_a, token_a = _start_copies_call(
        "gather_start_first", [slot_view(i, None) for i in first], [((0, 1, 2), _gather_copies, 3 * len(first))])
    later_views = []
    for i in later:
        later_views.append(slot_view(i, later_views[-1] if later_views else token_a))
    xb = _cast_call(xs, later_views[-1])
    got_first = _wait_copies_call("gather_wait_w_in", fly_a, sems_a[0], _gather_copies, xb)
    fly_b, sems_b, g_token = _start_copies_call(
        "gather_start_later", later_views + got_first,
        [((0,), _gather_copies, 3), ((1,), _gather_copies, 3), ((2,), _gather_copies, 3)])
    got_first = fly_b[len(later):]
    in_flight = dict(zip(later, fly_b))
    g_sems = [None] + list(sems_b)

    def arrive(which, group, after, tag):
        return _wait_copies_call(f"gather_wait_{tag}", [in_flight[w] for w in which], g_sems[group], _gather_copies, after)

    def pass_on(got, tag):
        flying, sems, token = _start_copies_call(
            f"gather_forward_start_{tag}", got, [(tuple(range(len(got))), _forward_copies, 3 * len(got))])
        return (flying, sems[0], tag), token

    def passed_on(state, after):
        flying, sems, tag = state
        return _wait_copies_call(f"gather_forward_wait_{tag}", flying, sems, _forward_copies, after)

    gathered = [None] * len(to_gather)
    gathered[0], gathered[4], gathered[5] = _forward_to_sibling_call(got_first, "gather_forward_w_in")
    w_in_f = gathered[0].reshape((N_CHIPS,) + w_in.shape[1:])
    w_pool_f = gathered[4].reshape(N_CHIPS, N_POOL_GROUPS, pg // N_CHIPS, pg).transpose(1, 0, 2, 3).reshape(N_POOL_GROUPS, pg, pg)
    c4 = c // N_CHIPS
    small_parts = [_unpack(gathered[5][k].reshape(-1, LANES), [(4, c4), (2, c4), (2, c4), (2, c4)]) for k in range(N_CHIPS)]
    conv_w_f = jnp.concatenate([sp_[0] for sp_ in small_parts], axis=1)
    b_a_f = jnp.concatenate([sp_[1] for sp_ in small_parts], axis=1)
    b_i_f = jnp.concatenate([sp_[2] for sp_ in small_parts], axis=1)
    lam_f = jnp.concatenate([sp_[3] for sp_ in small_parts], axis=1)
    wa_b = w_rg_a[0].astype(BF16)
    wi_b = w_rg_i[0].astype(BF16)

    proj = _proj_call(xb, w_in_f)
    xc = _conv_call(proj, conv_w_f, conv_b, c)
    fwd_w_out, token = pass_on(arrive((1,), 1, xc, "w_out"), "w_out")
    h_b, *gates_b = _scan_fwd_call(xc, wa_b[1], wi_b[1], b_a_f[1:2], b_i_f[1:2], lam_f[1:2], True, token)
    h_f, *gates_f = _scan_fwd_call(xc, wa_b[0], wi_b[0], b_a_f[0:1], b_i_f[0:1], lam_f[0:1], False, token)
    y, d_pool = _pool_combine_call(proj, h_f, h_b, w_pool_f, pool_scale, p)
    w_out_f = passed_on(fwd_w_out, y)[0].reshape(d, d)
    fwd_w1, token = pass_on(arrive((2,), 2, y, "w_mlp_in"), "w_mlp_in")
    xh1, x1b, rstd1 = _out_ln1_call(y, w_out_f, xs, ln_mix_g, ln_mix_b, token)
    w1_f = passed_on(fwd_w1, x1b)[0].reshape((N_CHIPS,) + w_mlp_in.shape[1:])
    first_half = _mlp_in_call(x1b, w1_f, g_token, None)
    fwd_w2, token = pass_on(arrive((3,), 3, first_half[0], "w_mlp_out"), "w_mlp_out")
    r_act, hsq = _mlp_in_call(x1b, w1_f, token, first_half)
    w2_f = passed_on(fwd_w2, hsq)[0].reshape(N_CHIPS * w_mlp_out.shape[1], d)
    dz2, dz2b, loss8, dg2, db2 = _mlp_out_ln2_call(hsq, w2_f, xh1, ln_mix_g, ln_mix_b, ln_ffn_g, ln_ffn_b, tgt)

    def start_siblings(grads, halves, tag, after=None):
        lands = [lax.empty(g.shape[1:] if halves else g.shape, g.dtype) for g in grads]
        copies = _sibling_copies(len(grads), halves)
        flying, sems, token = _start_copies_call(
            f"siblings_start_{tag}", list(grads) + lands, [(tuple(range(2 * len(grads))), copies, len(grads))], after)
        return (flying, sems[0], copies, len(grads), tag), token

    def finish_siblings(state, after):
        flying, sems, copies, n, tag = state
        got = _wait_copies_call(f"siblings_wait_{tag}", flying, sems, copies, after)
        return got[:n], got[n:]

    half_own = jnp.reshape(core, (1,)).astype(jnp.int32)
    half_sibling = 1 - half_own

    def chip_sum_of(a, b, row_sharded, tag, dep, overlapped):
        for_sibling = _half_grad_call(a, b, half_sibling, row_sharded, None, f"grad_{tag}_for_sibling", dep)
        state, token = start_siblings([for_sibling], False, tag)
        results = overlapped(token)
        _, (from_sibling,) = finish_siblings(state, results[0])
        return _half_grad_call(a, b, half_own, row_sharded, from_sibling, f"grad_{tag}", token), results

    def start_exchange(sums, n_repl, tag):
        n_sh = len(sums) - n_repl
        lands = [lax.empty((N_CHIPS - 1,) + a.shape[1:], a.dtype) for a in sums[:n_sh]]
        bufs = sums[:n_sh] + lands + sums[n_sh:]
        copies = _exchange_copies(n_sh, n_repl)
        flying, sems, token = _start_copies_call(
            f"reduce_start_{tag}", bufs, [(tuple(range(len(bufs))), copies, 3 * len(sums))])
        return (flying, sems[0], copies, n_sh, tag), token

    def finish_exchange(state, after):
        flying, sems, copies, n_sh, tag = state
        got = _wait_copies_call(f"reduce_wait_{tag}", flying, sems, copies, after)
        halves = []
        for a in range(n_sh):
            own, land = got[a], got[n_sh + a]
            cols = own.shape[-1]
            total = _sum_chips_call(own.reshape(N_CHIPS, -1, cols), land.reshape(N_CHIPS - 1, -1, cols), idx,
                                    f"reduce_sum_{tag}_{a}")
            halves.append(total.reshape((2,) + own.shape[1:]))
        for a, rp in enumerate(got[2 * n_sh:]):
            halves.append(_sum_chips_call(None, rp, idx, f"reduce_sum_{tag}_r{a}"))
        return halves

    def start_join(halves, tag):
        flying, sems, token = _start_copies_call(
            f"join_start_{tag}", halves, [(tuple(range(len(halves))), _join_copies, len(halves))])
        return (flying, sems[0], tag), token

    def finish_join(state, after):
        flying, sems, tag = state
        return _wait_copies_call(f"join_wait_{tag}", flying, sems, _join_copies, after)

    sum_w2, (dpre,) = chip_sum_of(hsq, dz2b, True, "w_mlp_out", g_token,
                                  lambda tok: (_dhsq_call(dz2b, w2_f, r_act, tok),))
    flying_w2, token = start_exchange([sum_w2], 0, "w2")
    sum_w1, (dz1, dz1b, dg1, db1) = chip_sum_of(
        x1b, dpre, False, "w_mlp_in", token,
        lambda tok: _dx1_ln1_bwd_call(dpre, w1_f, dz2, xh1, rstd1, ln_mix_g, tok))
    flying_w1, token = start_exchange([sum_w1], 0, "w1")

    def dy_and_mixer(tok):
        dy = _dy_call(dz1b, w_out_f, tok)
        return _mixer_bwd_call(dy, d_pool, proj, h_f, h_b, w_pool_f, pool_scale, p)

    sum_wout, (e_pool, dh, dgate, g_wpool, g_pscale8) = chip_sum_of(y, dz1b, True, "w_out", token, dy_and_mixer)
    flying_wout, token = start_exchange([sum_wout], 0, "w_out")
    dxc0, g_wa0, g_wi0, g_ba0, g_bi0, g_sp0 = _scan_bwd_call(
        xc, dh, h_f, gates_f, None, wa_b[0], wi_b[0], lam_f[0:1], False, token)
    dxc, g_wa1, g_wi1, g_ba1, g_bi1, g_sp1 = _scan_bwd_call(
        xc, dh, h_b, gates_b, dxc0, wa_b[1], wi_b[1], lam_f[1:2], True, token)
    dproj, g_cw8, g_cb8 = _dproj_call(e_pool, dxc, proj, dgate, conv_w_f, p)

    rowsum = lambda a8: jnp.sum(a8, axis=-2)
    g_lam = jnp.stack([rowsum(g_sp0), rowsum(g_sp1)]) * (-_sigmoid(-lam_f))
    small_grads = {
        "ln_mix_g": rowsum(dg1), "ln_mix_b": rowsum(db1), "ln_ffn_g": rowsum(dg2), "ln_ffn_b": rowsum(db2),
        "pool_scale": rowsum(g_pscale8), "conv_b": rowsum(g_cb8),
        "w_rg_a": jnp.stack([g_wa0, g_wa1]), "w_rg_i": jnp.stack([g_wi0, g_wi1]),
        "w_pool": g_wpool, "conv_w": rowsum(g_cw8),
        "b_rg_a": jnp.stack([rowsum(g_ba0), rowsum(g_ba1)]), "b_rg_i": jnp.stack([rowsum(g_bi0), rowsum(g_bi1)]),
        "rg_lambda": g_lam,
    }
    small_names = list(small_grads)
    small_shapes = [small_grads[nm].shape for nm in small_names]
    loss_share = jnp.reshape(jnp.sum(loss8) * (0.5 / d), (1,))
    g_small = _halves(_pack([small_grads[nm] for nm in small_names] + [loss_share], 2 * SUBLANES))
    sib_small, token = start_siblings([g_small], True, "small")
    flying_small = []

    def small_exchange_and_grad_x(tok):
        (mine,), (theirs,) = finish_siblings(sib_small, tok)
        small_sum = _add_half_call(mine, theirs, idx, "reduce_add_small")
        state, tok = start_exchange([small_sum], 1, "small")
        flying_small.append(state)
        return (_dx_call(dproj, w_in_f, dz1, tok),)

    sum_win, (grad_x,) = chip_sum_of(xb, dproj, False, "w_in", token, small_exchange_and_grad_x)
    flying_small = flying_small[0]
    flying_win, token = start_exchange([sum_win], 0, "w_in")

    grad_w, delta_w, new_m, new_v = {}, {}, {}, {}

    def adamw(nm, full):
        w2d = weights[nm][0]
        g2d = full.reshape(w2d.shape)
        go, dl, mn, vn = _adamw_call(g2d, w2d, m_in[nm][0], v_in[nm][0], f"adamw_{nm}")
        grad_w[nm], delta_w[nm], new_m[nm], new_v[nm] = go[None], dl[None], mn[None], vn[None]
        return vn

    join_w2, token = start_join(finish_exchange(flying_w2, token), "w2")
    join_w1, token = start_join(finish_exchange(flying_w1, token), "w1")
    join_wout, token = start_join(finish_exchange(flying_wout, token), "w_out")
    last = adamw("w_mlp_out", finish_join(join_w2, token)[0])
    last = adamw("w_mlp_in", finish_join(join_w1, last)[0])
    last = adamw("w_out", finish_join(join_wout, last)[0])

    small_joined = _join_halves_call(finish_exchange(flying_small, last), "reduce_join_small")[0]
    *small_sums, loss_sum = _unpack(small_joined.reshape(-1, LANES), small_shapes + [(1,)])
    small_full = dict(zip(small_names, small_sums))
    local = dict(small_full)
    local["w_pool"] = lax.dynamic_slice_in_dim(small_full["w_pool"], shard * (pg // N_CHIPS), pg // N_CHIPS, axis=1)
    for nm in ("conv_w", "b_rg_a", "b_rg_i", "rg_lambda"):
        local[nm] = lax.dynamic_slice_in_dim(small_full[nm], shard * c4, c4, axis=1)
    small_w_shapes = [weights[nm].shape for nm in small_names]
    g_pack = _pack([local[nm] for nm in small_names], SUBLANES)
    w_pack = _pack([weights[nm] for nm in small_names], SUBLANES)
    m_pack = _pack([m_in[nm] for nm in small_names], SUBLANES)
    v_pack = _pack([v_in[nm] for nm in small_names], SUBLANES)
    _, dl_p, mn_p, vn_p = _adamw_call(g_pack, w_pack, m_pack, v_pack, "adamw_small")
    for nm, gl, dl, mn, vn in zip(small_names, _unpack(g_pack, small_w_shapes), _unpack(dl_p, small_w_shapes),
                                  _unpack(mn_p, small_w_shapes), _unpack(vn_p, small_w_shapes)):
        grad_w[nm], delta_w[nm], new_m[nm], new_v[nm] = gl, dl, mn, vn
    adamw("w_in", _join_halves_call(finish_exchange(flying_win, vn_p), "reduce_join_w_in")[0])

    loss = loss_sum[0]
    return (loss, grad_x[None], *[grad_w[nm] for nm in names], *[delta_w[nm] for nm in names],
            *[new_m[nm] for nm in names], *[new_v[nm] for nm in names])
```

```python
import jax
import jax.numpy as jnp
from jax import lax
from jax.experimental import pallas as pl
from jax.experimental.pallas import tpu as pltpu

F32 = jnp.float32
BF16 = jnp.bfloat16

N_CHIPS = 4
LANES = 128
SUBLANES = 8
LRU_HEAD = 128
N_POOL_GROUPS = 4
POOL_WINDOWS = (2, 4, 8, 16)
RG_C = 8.0
LN_EPS = 1e-5
ALPHA = 2.0 ** 0.25
ADAM_LR, ADAM_B1, ADAM_B2, ADAM_EPS, ADAM_WD, ADAM_STEP = 0.001, 0.9, 0.999, 1e-08, 0.01, 10
VMEM_LIMIT = 56 * 1024 * 1024
SEQ_TILE = 256
MM_TILE = 512
LN_MM_K = 2048
LN_UNROLL = 8
ELT_BLOCK_BYTES = 2 * 1024 * 1024
RESIDENT_OPERAND_BYTES = 16 * 1024 * 1024
MESH = pl.DeviceIdType.MESH
ANY = pl.BlockSpec(memory_space=pl.ANY)


def _params(*sem):
    return pltpu.CompilerParams(dimension_semantics=sem, vmem_limit_bytes=VMEM_LIMIT)


def _sigmoid(z):
    return 1.0 / (1.0 + jnp.exp(-z))


def _neg_expm1(z):
    series = -(z * (1.0 + z * (0.5 + z * (1.0 / 6.0 + z * (1.0 / 24.0)))))
    return jnp.where(z > -0.01, series, 1.0 - jnp.exp(z))


def _softplus(z):
    return jnp.maximum(z, 0.0) + jnp.log1p(jnp.exp(-jnp.abs(z)))


_GELU_K = 0.7978845608028654
_GELU_C = 0.044715


def _gelu_and_grad(u):
    t = jnp.tanh(_GELU_K * (u + _GELU_C * (u * u * u)))
    g = 0.5 * u * (1.0 + t)
    dg = 0.5 * (1.0 + t) + 0.5 * u * (1.0 - t * t) * (_GELU_K * (1.0 + 3.0 * _GELU_C * u * u))
    return g, dg


def _shift_rows(prv, cur, nxt, o, rows):
    if o == 0:
        return cur
    if o == SUBLANES:
        return nxt
    if o == -SUBLANES:
        return prv
    if o > 0:
        return pltpu.roll(jnp.where(rows >= o, cur, nxt), SUBLANES - o, 0)
    p = -o
    return pltpu.roll(jnp.where(rows < SUBLANES - p, cur, prv), p, 0)


def _neighbour_chunks(main_ref, prev_ref, next_ref, r0, t_rows, cols, first_tile, last_tile):
    cur = main_ref[pl.ds(r0, SUBLANES), cols]
    before = main_ref[pl.ds(pl.multiple_of(jnp.maximum(r0 - SUBLANES, 0), SUBLANES), SUBLANES), cols]
    after = main_ref[pl.ds(pl.multiple_of(jnp.minimum(r0 + SUBLANES, t_rows - SUBLANES), SUBLANES), SUBLANES), cols]
    halo_prev = jnp.where(first_tile, 0.0, prev_ref[:, cols])
    halo_next = jnp.where(last_tile, 0.0, next_ref[:, cols])
    prv = jnp.where(r0 == 0, halo_prev, before)
    nxt = jnp.where(r0 == t_rows - SUBLANES, halo_next, after)
    return prv, cur, nxt


def _halo_specs(t_rows, n_rows, width, col_block):
    per = t_rows // SUBLANES
    last = n_rows // SUBLANES - 1
    return [
        pl.BlockSpec((t_rows, width), lambda i: (i, col_block)),
        pl.BlockSpec((SUBLANES, width), lambda i: (jnp.maximum(i * per - 1, 0), col_block)),
        pl.BlockSpec((SUBLANES, width), lambda i: (jnp.minimum((i + 1) * per, last), col_block)),
    ]


def _chunk_loop(t_rows, fn, init=None, unroll=1, descending=False):
    span = SUBLANES * unroll

    def step(ci, carry):
        base = pl.multiple_of(((t_rows // span - 1 - ci) if descending else ci) * span, span)
        for u in range(unroll):
            carry = fn(base + ((unroll - 1 - u) if descending else u) * SUBLANES, carry)
        return carry
    return lax.fori_loop(0, t_rows // span, step, init)


def _scan_chunk(a, b, h_in, rows, reverse):
    for dist in (1, 2, 4):
        if reverse:
            keep = rows < SUBLANES - dist
            shift = SUBLANES - dist
        else:
            keep = rows >= dist
            shift = dist
        b = a * jnp.where(keep, pltpu.roll(b, shift, 0), 0.0) + b
        a = a * jnp.where(keep, pltpu.roll(a, shift, 0), 1.0)
    return a * h_in + b


def _cast_call(x, dep):
    s, d = x.shape
    tm = min(MM_TILE, s)

    def body(x_ref, dep_ref, o_ref):
        o_ref[...] = x_ref[...].astype(BF16)

    return pl.pallas_call(
        body, name="cast_x", grid=(s // tm,),
        in_specs=[pl.BlockSpec((tm, d), lambda i: (i, 0)), ANY],
        out_specs=pl.BlockSpec((tm, d), lambda i: (i, 0)),
        out_shape=jax.ShapeDtypeStruct((s, d), BF16),
        compiler_params=_params("arbitrary"),
    )(x, dep)


def _proj_call(xb, w_in):
    s, d = xb.shape
    n, _, e4 = w_in.shape
    tm = min(MM_TILE, s)

    def body(x_ref, w_hbm, proj_ref, w_s, sems):
        @pl.when(pl.program_id(0) == 0)
        def _():
            copies = [pltpu.make_async_copy(w_hbm.at[k], w_s.at[:, pl.ds(k * e4, e4)], sems.at[k]) for k in range(n)]
            for cp in copies:
                cp.start()
            for cp in copies:
                cp.wait()

        proj_ref[...] = jnp.dot(x_ref[...], w_s[...], preferred_element_type=F32)

    return pl.pallas_call(
        body, name="proj", grid=(s // tm,),
        in_specs=[pl.BlockSpec((tm, d), lambda i: (i, 0)), ANY],
        out_specs=pl.BlockSpec((tm, n * e4), lambda i: (i, 0)),
        out_shape=jax.ShapeDtypeStruct((s, n * e4), F32),
        scratch_shapes=[pltpu.VMEM((d, n * e4), BF16), pltpu.SemaphoreType.DMA((n,))],
        compiler_params=_params("arbitrary"),
    )(xb, w_in)


def _conv_call(proj, conv_w, conv_b, c):
    s = proj.shape[0]
    t = min(SEQ_TILE, s)
    n_tiles = s // t

    def body(u_ref, up_ref, un_ref, w_ref, b_ref, xc_ref):
        i = pl.program_id(0)
        rows = lax.broadcasted_iota(jnp.int32, (SUBLANES, c), 0)
        w = w_ref[...]
        b = b_ref[...]

        def chunk(r0, _):
            prv, cur, nxt = _neighbour_chunks(u_ref, up_ref, un_ref, r0, t, slice(None), i == 0, i == n_tiles - 1)
            acc = b + w[1:2] * cur
            acc += w[0:1] * _shift_rows(prv, cur, nxt, -1, rows)
            acc += w[2:3] * _shift_rows(prv, cur, nxt, 1, rows)
            acc += w[3:4] * _shift_rows(prv, cur, nxt, 2, rows)
            xc_ref[pl.ds(r0, SUBLANES), :] = acc

        _chunk_loop(t, chunk)

    return pl.pallas_call(
        body, name="conv_fwd", grid=(n_tiles,),
        in_specs=_halo_specs(t, s, c, 1) + [pl.BlockSpec((4, c), lambda i: (0, 0)), pl.BlockSpec((1, c), lambda i: (0, 0))],
        out_specs=pl.BlockSpec((t, c), lambda i: (i, 0)),
        out_shape=jax.ShapeDtypeStruct((s, c), F32),
        compiler_params=_params("arbitrary"),
    )(proj, proj, proj, conv_w, conv_b)


def _gate_matmuls(xc_ref, wa_ref, wi_ref, pr_s, pi_s, heads):
    for h in range(heads):
        cs = pl.ds(h * LRU_HEAD, LRU_HEAD)
        xb = xc_ref[:, cs].astype(BF16)
        pr_s[:, cs] = jnp.dot(xb, wa_ref[h], preferred_element_type=F32)
        pi_s[:, cs] = jnp.dot(xb, wi_ref[h], preferred_element_type=F32)


def _rg_gates(pr, pi, ba, bi, sp):
    r = _sigmoid(pr + ba)
    ig = _sigmoid(pi + bi)
    log_a = (-RG_C * r) * sp
    a = jnp.exp(log_a)
    mult = jnp.sqrt(_neg_expm1(2.0 * log_a))
    return r, ig, a, mult


def _scan_fwd_call(xc, wa, wi, ba, bi, lam, reverse, dep):
    s, c = xc.shape
    heads = c // LRU_HEAD
    t = min(SEQ_TILE, s)
    n_tiles = s // t
    tile = (lambda i: (n_tiles - 1 - i, 0)) if reverse else (lambda i: (i, 0))
    whole2 = lambda i: (0, 0)
    whole3 = lambda i: (0, 0, 0)

    def body(xc_ref, wa_ref, wi_ref, ba_ref, bi_ref, lam_ref, dep_ref, h_ref, r_ref, ig_ref, a_ref, mult_ref,
             pr_s, pi_s, carry_s):
        @pl.when(pl.program_id(0) == 0)
        def _():
            carry_s[...] = jnp.zeros_like(carry_s)

        _gate_matmuls(xc_ref, wa_ref, wi_ref, pr_s, pi_s, heads)
        ba_v, bi_v = ba_ref[...], bi_ref[...]
        sp = _softplus(-lam_ref[...])

        rows = lax.broadcasted_iota(jnp.int32, (SUBLANES, c), 0)

        def chunk(r0, h_in):
            rs = pl.ds(r0, SUBLANES)
            r, ig, a, mult = _rg_gates(pr_s[rs, :], pi_s[rs, :], ba_v, bi_v, sp)
            r_ref[rs, :] = r
            ig_ref[rs, :] = ig
            a_ref[rs, :] = a
            mult_ref[rs, :] = mult
            h = _scan_chunk(a, mult * ig * xc_ref[rs, :], h_in, rows, reverse)
            h_ref[rs, :] = h
            return h[0:1, :] if reverse else h[SUBLANES - 1:SUBLANES, :]

        carry_s[...] = _chunk_loop(t, chunk, carry_s[...], unroll=2, descending=reverse)

    return pl.pallas_call(
        body, name="scan_fwd_rev" if reverse else "scan_fwd", grid=(n_tiles,),
        in_specs=[pl.BlockSpec((t, c), tile),
                  pl.BlockSpec((heads, LRU_HEAD, LRU_HEAD), whole3), pl.BlockSpec((heads, LRU_HEAD, LRU_HEAD), whole3),
                  pl.BlockSpec((1, c), whole2), pl.BlockSpec((1, c), whole2), pl.BlockSpec((1, c), whole2), ANY],
        out_specs=[pl.BlockSpec((t, c), tile)] * 5,
        out_shape=[jax.ShapeDtypeStruct((s, c), F32)] * 5,
        scratch_shapes=[pltpu.VMEM((t, c), F32), pltpu.VMEM((t, c), F32), pltpu.VMEM((1, c), F32)],
        compiler_params=_params("arbitrary"),
    )(xc, wa, wi, ba, bi, lam, dep)


def _window_counts(r0, tile_idx, t_rows, n_rows, half, shape):
    pos = tile_idx * t_rows + r0 + lax.broadcasted_iota(jnp.int32, shape, 0)
    hi = jnp.minimum(pos + half, n_rows)
    lo = jnp.maximum(pos - half, 0)
    return (hi - lo).astype(F32)


def _window_inverse_counts(r0, tile_idx, t_rows, n_rows, half, width):
    inv = 1.0 / _window_counts(r0, tile_idx, t_rows, n_rows, half, (SUBLANES, LANES))
    return jnp.tile(inv, (1, width // LANES))


def _pool_combine_call(proj, h_f, h_b, w_pool, pool_scale, p):
    s = proj.shape[0]
    c = h_f.shape[1]
    pg = p // N_POOL_GROUPS
    t = min(SEQ_TILE, s)
    n_tiles = s // t

    def body(u_ref, up_ref, un_ref, gate_ref, hf_ref, hb_ref, wp_ref, sc_ref, y_ref, d_ref, d_s, yr_s):
        i = pl.program_id(0)
        rows = lax.broadcasted_iota(jnp.int32, (SUBLANES, pg), 0)

        def chunk(r0, _):
            rs = pl.ds(r0, SUBLANES)
            for g, w in enumerate(POOL_WINDOWS):
                cols = pl.ds(g * pg, pg)
                prv, cur, nxt = _neighbour_chunks(u_ref, up_ref, un_ref, r0, t, cols, i == 0, i == n_tiles - 1)
                tot = cur
                for o in range(-(w // 2), w // 2):
                    if o != 0:
                        tot = tot + _shift_rows(prv, cur, nxt, o, rows)
                d_s[rs, cols] = tot * _window_inverse_counts(r0, i, t, s, w // 2, pg) - cur
            gate, _ = _gelu_and_grad(gate_ref[rs, :])
            yr_s[rs, :] = (hf_ref[rs, :] + hb_ref[rs, :]) * gate

        _chunk_loop(t, chunk)
        y_ref[:, pl.ds(p, c)] = yr_s[...].astype(BF16)
        d_ref[...] = d_s[...].astype(BF16)
        for g in range(N_POOL_GROUPS):
            cols = pl.ds(g * pg, pg)
            out = jnp.dot(d_s[:, cols].astype(BF16), wp_ref[g], preferred_element_type=F32)
            y_ref[:, cols] = (out * sc_ref[:, cols]).astype(BF16)

    return pl.pallas_call(
        body, name="pool_combine", grid=(n_tiles,),
        in_specs=_halo_specs(t, s, p, 0) + [
            pl.BlockSpec((t, c), lambda i: (i, 2)),
            pl.BlockSpec((t, c), lambda i: (i, 0)), pl.BlockSpec((t, c), lambda i: (i, 0)),
            pl.BlockSpec((N_POOL_GROUPS, pg, pg), lambda i: (0, 0, 0)), pl.BlockSpec((1, p), lambda i: (0, 0))],
        out_specs=[pl.BlockSpec((t, p + c), lambda i: (i, 0)), pl.BlockSpec((t, p), lambda i: (i, 0))],
        out_shape=[jax.ShapeDtypeStruct((s, p + c), BF16), jax.ShapeDtypeStruct((s, p), BF16)],
        scratch_shapes=[pltpu.VMEM((t, p), F32), pltpu.VMEM((t, c), F32)],
        compiler_params=_params("arbitrary"),
    )(proj, proj, proj, proj, h_f, h_b, w_pool, pool_scale)


def _layer_norm_rows(z, g, b):
    mu = jnp.mean(z, axis=-1, keepdims=True)
    zc = z - mu
    var = jnp.mean(zc * zc, axis=-1, keepdims=True)
    rstd = lax.rsqrt(var + LN_EPS)
    xh = zc * rstd
    return xh, rstd, xh * g + b


def _layer_norm_bwd_rows(dx, xh, rstd, g):
    dxh = dx * g
    m1 = jnp.mean(dxh, axis=-1, keepdims=True)
    m2 = jnp.mean(dxh * xh, axis=-1, keepdims=True)
    return rstd * (dxh - m1 - xh * m2)


def _out_ln1_call(y, w_out, x, g1, b1, dep):
    s, d = x.shape
    tm = min(SEQ_TILE, s)

    def body(y_ref, w_ref, x_ref, g_ref, b_ref, dep_ref, xh_ref, x1b_ref, rstd_ref, acc_s, x1_s):
        acc_s[...] = jnp.dot(y_ref[...], w_ref[...], preferred_element_type=F32)
        g, b = g_ref[...], b_ref[...]

        def chunk(r0, _):
            rs = pl.ds(r0, SUBLANES)
            xh, rstd, x1 = _layer_norm_rows(ALPHA * x_ref[rs, :] + acc_s[rs, :], g, b)
            xh_ref[rs, :] = xh
            x1_s[rs, :] = x1
            rstd_ref[rs, :] = rstd

        _chunk_loop(tm, chunk, unroll=LN_UNROLL)
        x1b_ref[...] = x1_s[...].astype(BF16)

    return pl.pallas_call(
        body, name="out_ln1", grid=(s // tm,),
        in_specs=[pl.BlockSpec((tm, d), lambda i: (i, 0)), pl.BlockSpec((d, d), lambda i: (0, 0)),
                  pl.BlockSpec((tm, d), lambda i: (i, 0)),
                  pl.BlockSpec((1, d), lambda i: (0, 0)), pl.BlockSpec((1, d), lambda i: (0, 0)), ANY],
        out_specs=[pl.BlockSpec((tm, d), lambda i: (i, 0)), pl.BlockSpec((tm, d), lambda i: (i, 0)),
                   pl.BlockSpec((tm, 1), lambda i: (i, 0))],
        out_shape=[jax.ShapeDtypeStruct((s, d), F32), jax.ShapeDtypeStruct((s, d), BF16), jax.ShapeDtypeStruct((s, 1), F32)],
        scratch_shapes=[pltpu.VMEM((tm, d), F32), pltpu.VMEM((tm, d), F32)],
        compiler_params=_params("arbitrary"),
    )(y, w_out, x, g1, b1, dep)


def _mlp_in_call(x1b, w1, dep, done):
    s, d = x1b.shape
    n, _, f4 = w1.shape
    tm = min(MM_TILE, s)
    tn = min(1024, f4)
    per = f4 // tn
    blocks = n * per // 2
    first = 0 if done is None else blocks
    extra = [] if done is None else list(done)

    def body(x_ref, w_ref, dep_ref, *rest):
        r_ref, q_ref = rest[-2:]
        r = jnp.maximum(jnp.dot(x_ref[...], w_ref[...], preferred_element_type=F32), 0.0)
        r_ref[...] = r.astype(BF16)
        q_ref[...] = (r * r).astype(BF16)

    out_spec = pl.BlockSpec((tm, tn), lambda j, i: (i, first + j))
    return pl.pallas_call(
        body, name="mlp_in" if done is None else "mlp_in_rest", grid=(blocks, s // tm),
        in_specs=[pl.BlockSpec((tm, d), lambda j, i: (i, 0)),
                  pl.BlockSpec((None, d, tn), lambda j, i: ((first + j) // per, 0, (first + j) % per)), ANY] + [ANY] * len(extra),
        out_specs=[out_spec, out_spec],
        out_shape=[jax.ShapeDtypeStruct((s, n * f4), BF16), jax.ShapeDtypeStruct((s, n * f4), BF16)],
        input_output_aliases={3: 0, 4: 1} if extra else {},
        compiler_params=_params("arbitrary", "arbitrary"),
    )(x1b, w1, dep, *extra)


def _mlp_out_ln2_call(hsq, w2, xh1, g1, b1, g2, b2, target):
    s, f = hsq.shape
    d = w2.shape[1]
    tm = min(MM_TILE, s)
    tk = min(LN_MM_K, f)
    nk = f // tk

    def body(h_ref, w_ref, xh1_ref, g1_ref, b1_ref, g2_ref, b2_ref, t_ref,
             dz_ref, dzb_ref, loss_ref, dg_ref, db_ref, acc_s):
        i, k = pl.program_id(0), pl.program_id(1)

        @pl.when((i == 0) & (k == 0))
        def _():
            loss_ref[...] = jnp.zeros_like(loss_ref)
            dg_ref[...] = jnp.zeros_like(dg_ref)
            db_ref[...] = jnp.zeros_like(db_ref)

        @pl.when(k == 0)
        def _():
            acc_s[...] = jnp.zeros_like(acc_s)

        acc_s[...] += jnp.dot(h_ref[...], w_ref[...], preferred_element_type=F32)

        @pl.when(k == nk - 1)
        def _():
            g1, b1, g2, b2 = g1_ref[...], b1_ref[...], g2_ref[...], b2_ref[...]

            def chunk(r0, _):
                rs = pl.ds(r0, SUBLANES)
                x1 = xh1_ref[rs, :] * g1 + b1
                xh2, rstd, x2 = _layer_norm_rows(ALPHA * x1 + acc_s[rs, :], g2, b2)
                diff = x2 - t_ref[rs, :]
                loss_ref[...] += diff * diff
                dx2 = diff * (1.0 / d)
                dg_ref[...] += dx2 * xh2
                db_ref[...] += dx2
                dz = _layer_norm_bwd_rows(dx2, xh2, rstd, g2)
                dz_ref[rs, :] = dz

            _chunk_loop(tm, chunk, unroll=LN_UNROLL)
            dzb_ref[...] = dz_ref[...].astype(BF16)

    row = lambda i, k: (i, 0)
    vec = lambda i, k: (0, 0)
    return pl.pallas_call(
        body, name="mlp_out_ln2", grid=(s // tm, nk),
        in_specs=[pl.BlockSpec((tm, tk), lambda i, k: (i, k)), pl.BlockSpec((tk, d), lambda i, k: (k, 0)),
                  pl.BlockSpec((tm, d), row), pl.BlockSpec((1, d), vec), pl.BlockSpec((1, d), vec),
                  pl.BlockSpec((1, d), vec), pl.BlockSpec((1, d), vec), pl.BlockSpec((tm, d), row)],
        out_specs=[pl.BlockSpec((tm, d), row), pl.BlockSpec((tm, d), row),
                   pl.BlockSpec((SUBLANES, d), vec), pl.BlockSpec((SUBLANES, d), vec), pl.BlockSpec((SUBLANES, d), vec)],
        out_shape=[jax.ShapeDtypeStruct((s, d), F32), jax.ShapeDtypeStruct((s, d), BF16),
                   jax.ShapeDtypeStruct((SUBLANES, d), F32), jax.ShapeDtypeStruct((SUBLANES, d), F32),
                   jax.ShapeDtypeStruct((SUBLANES, d), F32)],
        scratch_shapes=[pltpu.VMEM((tm, d), F32)],
        compiler_params=_params("arbitrary", "arbitrary"),
    )(hsq, w2, xh1, g1, b1, g2, b2, target)


def _half_grad_call(a, b, half, row_sharded, init, name, dep):
    s, m = a.shape
    n = b.shape[1]
    if row_sharded:
        rows, cols = m // (2 * N_CHIPS), n
        tm = min(1024, rows)
        per = rows // tm
        tn = min(1024, cols)
        n_i, n_j = N_CHIPS * per, cols // tn
        a_block = lambda i, h: ((i // per) * 2 + h) * per + i % per
        out_block = lambda i, j: (i // per, i % per, j)
    else:
        rows, cols = m // 2, n // N_CHIPS
        tm = min(1024, rows)
        per = rows // tm
        tn = cols if cols % 1024 else 1024
        per_n = cols // tn
        n_i, n_j = per, N_CHIPS * per_n
        a_block = lambda i, h: h * per + i
        out_block = lambda i, j: (j // per_n, i, j % per_n)
    tk = min(2048, s)
    if row_sharded and tm < 1024 and s * n * 2 <= RESIDENT_OPERAND_BYTES:
        tk, tn, n_j = s, n, 1
    has_init = init is not None

    def body(half_ref, a_ref, b_ref, *rest):
        o_ref = rest[-1]

        @pl.when(pl.program_id(2) == 0)
        def _():
            o_ref[...] = rest[0][...] if has_init else jnp.zeros_like(o_ref)

        o_ref[...] += lax.dot_general(a_ref[...], b_ref[...], (((0,), (0,)), ((), ())), preferred_element_type=F32)

    out_spec = pl.BlockSpec((None, tm, tn), lambda i, j, k, h: out_block(i, j))
    in_specs = [pl.BlockSpec((tk, tm), lambda i, j, k, h: (k, a_block(i, h[0]))),
                pl.BlockSpec((tk, tn), lambda i, j, k, h: (k, j))]
    args = [a, b]
    if has_init:
        in_specs.append(out_spec)
        args.append(init)
    in_specs.append(ANY)
    args.append(dep)
    return pl.pallas_call(
        body, name=name,
        grid_spec=pltpu.PrefetchScalarGridSpec(num_scalar_prefetch=1, grid=(n_i, n_j, s // tk), in_specs=in_specs,
                                               out_specs=out_spec),
        out_shape=jax.ShapeDtypeStruct((N_CHIPS, rows, cols), F32),
        compiler_params=_params("arbitrary", "arbitrary", "arbitrary"),
    )(half, *args)


def _dhsq_call(dzb, w2, r, dep):
    s, d = dzb.shape
    f = w2.shape[0]
    tm = min(MM_TILE, s)
    tn = min(1024, f)

    def body(dz_ref, w_ref, r_ref, dep_ref, o_ref):
        dh = lax.dot_general(dz_ref[...], w_ref[...], (((1,), (1,)), ((), ())), preferred_element_type=F32)
        o_ref[...] = (dh * (2.0 * r_ref[...].astype(F32))).astype(BF16)

    return pl.pallas_call(
        body, name="mlp_dpre", grid=(f // tn, s // tm),
        in_specs=[pl.BlockSpec((tm, d), lambda j, i: (i, 0)), pl.BlockSpec((tn, d), lambda j, i: (j, 0)),
                  pl.BlockSpec((tm, tn), lambda j, i: (i, j)), ANY],
        out_specs=pl.BlockSpec((tm, tn), lambda j, i: (i, j)),
        out_shape=jax.ShapeDtypeStruct((s, f), BF16),
        compiler_params=_params("arbitrary", "arbitrary"),
    )(dzb, w2, r, dep)


def _dx1_ln1_bwd_call(dpre, w1, dz2, xh1, rstd1, g1, dep):
    s, f = dpre.shape
    n, d, f4 = w1.shape
    tm = min(MM_TILE, s)
    tk = min(LN_MM_K, f4)
    per = f4 // tk
    nk = n * per

    def body(dp_ref, w_ref, dz2_ref, xh_ref, rstd_ref, g_ref, dep_ref, dz_ref, dzb_ref, dg_ref, db_ref, acc_s):
        i, k = pl.program_id(0), pl.program_id(1)

        @pl.when((i == 0) & (k == 0))
        def _():
            dg_ref[...] = jnp.zeros_like(dg_ref)
            db_ref[...] = jnp.zeros_like(db_ref)

        @pl.when(k == 0)
        def _():
            acc_s[...] = jnp.zeros_like(acc_s)

        acc_s[...] += lax.dot_general(dp_ref[...], w_ref[...], (((1,), (1,)), ((), ())), preferred_element_type=F32)

        @pl.when(k == nk - 1)
        def _():
            g = g_ref[...]

            def chunk(r0, _):
                rs = pl.ds(r0, SUBLANES)
                dx1 = acc_s[rs, :] + ALPHA * dz2_ref[rs, :]
                xh = xh_ref[rs, :]
                dg_ref[...] += dx1 * xh
                db_ref[...] += dx1
                dz = _layer_norm_bwd_rows(dx1, xh, rstd_ref[rs, :], g)
                dz_ref[rs, :] = dz

            _chunk_loop(tm, chunk, unroll=LN_UNROLL)
            dzb_ref[...] = dz_ref[...].astype(BF16)

    row = lambda i, k: (i, 0)
    vec = lambda i, k: (0, 0)
    return pl.pallas_call(
        body, name="dx1_ln1_bwd", grid=(s // tm, nk),
        in_specs=[pl.BlockSpec((tm, tk), lambda i, k: (i, k)),
                  pl.BlockSpec((None, d, tk), lambda i, k: (k // per, 0, k % per)),
                  pl.BlockSpec((tm, d), row), pl.BlockSpec((tm, d), row), pl.BlockSpec((tm, 1), row),
                  pl.BlockSpec((1, d), vec), ANY],
        out_specs=[pl.BlockSpec((tm, d), row), pl.BlockSpec((tm, d), row),
                   pl.BlockSpec((SUBLANES, d), vec), pl.BlockSpec((SUBLANES, d), vec)],
        out_shape=[jax.ShapeDtypeStruct((s, d), F32), jax.ShapeDtypeStruct((s, d), BF16),
                   jax.ShapeDtypeStruct((SUBLANES, d), F32), jax.ShapeDtypeStruct((SUBLANES, d), F32)],
        scratch_shapes=[pltpu.VMEM((tm, d), F32)],
        compiler_params=_params("arbitrary", "arbitrary"),
    )(dpre, w1, dz2, xh1, rstd1, g1, dep)


def _dy_call(dzb, w_out, dep):
    s, d = dzb.shape
    e = w_out.shape[0]
    tm = min(MM_TILE, s)

    def body(dz_ref, w_ref, dep_ref, o_ref):
        o_ref[...] = lax.dot_general(dz_ref[...], w_ref[...], (((1,), (1,)), ((), ())), preferred_element_type=F32)

    return pl.pallas_call(
        body, name="dy", grid=(s // tm,),
        in_specs=[pl.BlockSpec((tm, d), lambda i: (i, 0)), pl.BlockSpec((e, d), lambda i: (0, 0)), ANY],
        out_specs=pl.BlockSpec((tm, e), lambda i: (i, 0)),
        out_shape=jax.ShapeDtypeStruct((s, e), F32),
        compiler_params=_params("arbitrary"),
    )(dzb, w_out, dep)


def _mixer_bwd_call(dy, d_pool, proj, h_f, h_b, w_pool, pool_scale, p):
    s = dy.shape[0]
    c = h_f.shape[1]
    pg = p // N_POOL_GROUPS
    t = min(SEQ_TILE, s)
    n_tiles = s // t

    def body(dyp_ref, dyr_ref, d_ref, gate_ref, hf_ref, hb_ref, wp_ref, sc_ref,
             e_ref, dh_ref, dgate_ref, dwp_ref, dsc_ref, dd_s):
        i = pl.program_id(0)

        @pl.when(i == 0)
        def _():
            dwp_ref[...] = jnp.zeros_like(dwp_ref)
            dsc_ref[...] = jnp.zeros_like(dsc_ref)

        for g in range(N_POOL_GROUPS):
            cols = pl.ds(g * pg, pg)
            dg = d_ref[:, cols]
            out = jnp.dot(dg, wp_ref[g], preferred_element_type=F32)
            dyp = dyp_ref[:, cols]
            prod = dyp * out
            dsc_ref[:, cols] += jnp.sum(prod.reshape(t // SUBLANES, SUBLANES, pg), axis=0)
            dout = (dyp * sc_ref[:, cols]).astype(BF16)
            dwp_ref[g] += lax.dot_general(dg, dout, (((0,), (0,)), ((), ())), preferred_element_type=F32)
            dd_s[:, cols] = lax.dot_general(dout, wp_ref[g], (((1,), (1,)), ((), ())), preferred_element_type=F32)

        def chunk(r0, _):
            rs = pl.ds(r0, SUBLANES)
            for g, w in enumerate(POOL_WINDOWS):
                cols = pl.ds(g * pg, pg)
                e_ref[rs, cols] = dd_s[rs, cols] * _window_inverse_counts(r0, i, t, s, w // 2, pg)
            gate, dgate = _gelu_and_grad(gate_ref[rs, :])
            dyr = dyr_ref[rs, :]
            dh_ref[rs, :] = dyr * gate
            dd_s[rs, :] = dyr * (hf_ref[rs, :] + hb_ref[rs, :]) * dgate

        _chunk_loop(t, chunk)
        dgate_ref[...] = dd_s[...].astype(BF16)

    tile = lambda i: (i, 0)
    return pl.pallas_call(
        body, name="mixer_bwd", grid=(n_tiles,),
        in_specs=[pl.BlockSpec((t, p), tile), pl.BlockSpec((t, c), lambda i: (i, 1)), pl.BlockSpec((t, p), tile),
                  pl.BlockSpec((t, c), lambda i: (i, 2)), pl.BlockSpec((t, c), tile), pl.BlockSpec((t, c), tile),
                  pl.BlockSpec((N_POOL_GROUPS, pg, pg), lambda i: (0, 0, 0)), pl.BlockSpec((1, p), lambda i: (0, 0))],
        out_specs=[pl.BlockSpec((t, p), tile), pl.BlockSpec((t, c), tile), pl.BlockSpec((t, c), tile),
                   pl.BlockSpec((N_POOL_GROUPS, pg, pg), lambda i: (0, 0, 0)), pl.BlockSpec((SUBLANES, p), lambda i: (0, 0))],
        out_shape=[jax.ShapeDtypeStruct((s, p), F32), jax.ShapeDtypeStruct((s, c), F32), jax.ShapeDtypeStruct((s, c), BF16),
                   jax.ShapeDtypeStruct((N_POOL_GROUPS, pg, pg), F32), jax.ShapeDtypeStruct((SUBLANES, p), F32)],
        scratch_shapes=[pltpu.VMEM((t, p), F32)],
        compiler_params=_params("arbitrary"),
    )(dy, dy, d_pool, proj, h_f, h_b, w_pool, pool_scale)


def _scan_bwd_call(xc, dh, h_dir, gates, dxc_prev, wa, wi, lam, reverse, dep):
    s, c = xc.shape
    heads = c // LRU_HEAD
    t = min(SEQ_TILE, s)
    n_tiles = s // t
    per = t // SUBLANES
    last_blk = s // SUBLANES - 1
    tile = (lambda i: (i, 0)) if reverse else (lambda i: (n_tiles - 1 - i, 0))
    if reverse:
        halo = lambda i: (jnp.minimum((i + 1) * per, last_blk), 0)
    else:
        halo = lambda i: (jnp.maximum((n_tiles - 1 - i) * per - 1, 0), 0)
    whole2 = lambda i: (0, 0)
    whole3 = lambda i: (0, 0, 0)
    has_prev = dxc_prev is not None
    n_in = 11 + int(has_prev) + 1

    def body(*refs):
        xc_ref, dh_ref, h_ref, hh_ref, r_ref, ig_ref, a_ref, mult_ref = refs[:8]
        prev_ref = refs[8] if has_prev else None
        wa_ref, wi_ref, lam_ref = refs[n_in - 4:n_in - 1]
        dxc_ref, dwa_ref, dwi_ref, dba_ref, dbi_ref, dsp_ref = refs[n_in:n_in + 6]
        pr_s, pi_s, carry_s = refs[n_in + 6:]
        step = pl.program_id(0)
        tile_idx = step if reverse else n_tiles - 1 - step

        @pl.when(step == 0)
        def _():
            carry_s[...] = jnp.zeros_like(carry_s)
            dwa_ref[...] = jnp.zeros_like(dwa_ref)
            dwi_ref[...] = jnp.zeros_like(dwi_ref)
            dba_ref[...] = jnp.zeros_like(dba_ref)
            dbi_ref[...] = jnp.zeros_like(dbi_ref)
            dsp_ref[...] = jnp.zeros_like(dsp_ref)

        sp = _softplus(-lam_ref[...])
        rows = lax.broadcasted_iota(jnp.int32, (SUBLANES, c), 0)

        def chunk(r0, u_in):
            rs = pl.ds(r0, SUBLANES)
            xcv = xc_ref[rs, :]
            r, ig, a, mult = r_ref[rs, :], ig_ref[rs, :], a_ref[rs, :], mult_ref[rs, :]
            dhv = dh_ref[rs, :]
            u = _scan_chunk(a, a * dhv, u_in, rows, not reverse)
            if reverse:
                gt = dhv + jnp.where(rows >= 1, pltpu.roll(u, 1, 0), u_in)
                u_out = u[SUBLANES - 1:SUBLANES, :]
            else:
                gt = dhv + jnp.where(rows < SUBLANES - 1, pltpu.roll(u, SUBLANES - 1, 0), u_in)
                u_out = u[0:1, :]
            cur = h_ref[rs, :]
            if reverse:
                after = h_ref[pl.ds(pl.multiple_of(jnp.minimum(r0 + SUBLANES, t - SUBLANES), SUBLANES), SUBLANES), :]
                edge = jnp.where(tile_idx == n_tiles - 1, 0.0, hh_ref[...])
                nxt = jnp.where(r0 == t - SUBLANES, edge, after)
                hs = _shift_rows(cur, cur, nxt, 1, rows)
            else:
                before = h_ref[pl.ds(pl.multiple_of(jnp.maximum(r0 - SUBLANES, 0), SUBLANES), SUBLANES), :]
                edge = jnp.where(tile_idx == 0, 0.0, hh_ref[...])
                prv = jnp.where(r0 == 0, edge, before)
                hs = _shift_rows(prv, cur, cur, -1, rows)
            gx = gt * xcv
            dmult = gx * ig
            di = gx * mult
            dlog_a = (gt * hs) * a - dmult * (a * a) / mult
            dr = dlog_a * (-RG_C * sp)
            dsp_ref[...] += dlog_a * (-RG_C * r)
            dpr = dr * r * (1.0 - r)
            dpi = di * ig * (1.0 - ig)
            dba_ref[...] += dpr
            dbi_ref[...] += dpi
            direct = gt * mult * ig
            if has_prev:
                direct = direct + prev_ref[rs, :]
            dxc_ref[rs, :] = direct
            pr_s[rs, :] = dpr
            pi_s[rs, :] = dpi
            return u_out

        carry_s[...] = _chunk_loop(t, chunk, carry_s[...], unroll=2, descending=not reverse)

        for h in range(heads):
            cs = pl.ds(h * LRU_HEAD, LRU_HEAD)
            xb = xc_ref[:, cs].astype(BF16)
            dprb = pr_s[:, cs].astype(BF16)
            dpib = pi_s[:, cs].astype(BF16)
            dwa_ref[h] += lax.dot_general(xb, dprb, (((0,), (0,)), ((), ())), preferred_element_type=F32)
            dwi_ref[h] += lax.dot_general(xb, dpib, (((0,), (0,)), ((), ())), preferred_element_type=F32)
            dxc_ref[:, cs] += (
                lax.dot_general(dprb, wa_ref[h], (((1,), (1,)), ((), ())), preferred_element_type=F32)
                + lax.dot_general(dpib, wi_ref[h], (((1,), (1,)), ((), ())), preferred_element_type=F32))

    tile_spec = pl.BlockSpec((t, c), tile)
    in_specs = [tile_spec, tile_spec, tile_spec, pl.BlockSpec((SUBLANES, c), halo)] + [tile_spec] * 4
    args = [xc, dh, h_dir, h_dir, *gates]
    if has_prev:
        in_specs.append(tile_spec)
        args.append(dxc_prev)
    in_specs += [pl.BlockSpec((heads, LRU_HEAD, LRU_HEAD), whole3), pl.BlockSpec((heads, LRU_HEAD, LRU_HEAD), whole3),
                 pl.BlockSpec((1, c), whole2), ANY]
    args += [wa, wi, lam, dep]
    assert len(args) == n_in
    return pl.pallas_call(
        body, name="scan_bwd_rev" if reverse else "scan_bwd", grid=(n_tiles,),
        in_specs=in_specs,
        out_specs=[tile_spec,
                   pl.BlockSpec((heads, LRU_HEAD, LRU_HEAD), whole3), pl.BlockSpec((heads, LRU_HEAD, LRU_HEAD), whole3),
                   pl.BlockSpec((SUBLANES, c), whole2), pl.BlockSpec((SUBLANES, c), whole2), pl.BlockSpec((SUBLANES, c), whole2)],
        out_shape=[jax.ShapeDtypeStruct((s, c), F32),
                   jax.ShapeDtypeStruct((heads, LRU_HEAD, LRU_HEAD), F32), jax.ShapeDtypeStruct((heads, LRU_HEAD, LRU_HEAD), F32),
                   jax.ShapeDtypeStruct((SUBLANES, c), F32), jax.ShapeDtypeStruct((SUBLANES, c), F32),
                   jax.ShapeDtypeStruct((SUBLANES, c), F32)],
        scratch_shapes=[pltpu.VMEM((t, c), F32), pltpu.VMEM((t, c), F32), pltpu.VMEM((1, c), F32)],
        compiler_params=_params("arbitrary"),
    )(*args)


def _dproj_call(e_pool, dxc, proj, dgate, conv_w, p):
    s, c = dxc.shape
    pg = p // N_POOL_GROUPS
    t = min(SEQ_TILE, s)
    n_tiles = s // t

    def body(e_ref, ep_ref, en_ref, dx_ref, dxp_ref, dxn_ref, u_ref, up_ref, un_ref, dgate_ref, w_ref,
             dproj_ref, dcw_ref, dcb_ref, st_s):
        i = pl.program_id(0)
        first, last = i == 0, i == n_tiles - 1

        @pl.when(first)
        def _():
            dcw_ref[...] = jnp.zeros_like(dcw_ref)
            dcb_ref[...] = jnp.zeros_like(dcb_ref)

        rows_p = lax.broadcasted_iota(jnp.int32, (SUBLANES, pg), 0)
        rows_c = lax.broadcasted_iota(jnp.int32, (SUBLANES, c), 0)
        w = w_ref[...]

        def chunk(r0, _):
            rs = pl.ds(r0, SUBLANES)
            for g, win in enumerate(POOL_WINDOWS):
                cols = pl.ds(g * pg, pg)
                prv, cur, nxt = _neighbour_chunks(e_ref, ep_ref, en_ref, r0, t, cols, first, last)
                tot = cur
                for o in range(-(win // 2) + 1, win // 2 + 1):
                    if o != 0:
                        tot = tot + _shift_rows(prv, cur, nxt, o, rows_p)
                cnt = _window_counts(r0, i, t, s, win // 2, (SUBLANES, pg))
                st_s[rs, cols] = tot - cur * cnt
            prv, cur, nxt = _neighbour_chunks(dx_ref, dxp_ref, dxn_ref, r0, t, slice(None), first, last)
            du = w[1:2] * cur
            du += w[0:1] * _shift_rows(prv, cur, nxt, 1, rows_c)
            du += w[2:3] * _shift_rows(prv, cur, nxt, -1, rows_c)
            du += w[3:4] * _shift_rows(prv, cur, nxt, -2, rows_c)
            st_s[rs, pl.ds(p, c)] = du
            uprv, ucur, unxt = _neighbour_chunks(u_ref, up_ref, un_ref, r0, t, slice(None), first, last)
            dcb_ref[...] += cur
            for j, o in enumerate((-1, 0, 1, 2)):
                dcw_ref[j] += cur * _shift_rows(uprv, ucur, unxt, o, rows_c)

        _chunk_loop(t, chunk)
        dproj_ref[:, pl.ds(0, p + c)] = st_s[...].astype(BF16)
        dproj_ref[:, pl.ds(p + c, c)] = dgate_ref[...]

    return pl.pallas_call(
        body, name="dproj", grid=(n_tiles,),
        in_specs=_halo_specs(t, s, p, 0) + _halo_specs(t, s, c, 0) + _halo_specs(t, s, c, 1) + [
            pl.BlockSpec((t, c), lambda i: (i, 0)), pl.BlockSpec((4, c), lambda i: (0, 0))],
        out_specs=[pl.BlockSpec((t, p + 2 * c), lambda i: (i, 0)),
                   pl.BlockSpec((4, SUBLANES, c), lambda i: (0, 0, 0)), pl.BlockSpec((SUBLANES, c), lambda i: (0, 0))],
        out_shape=[jax.ShapeDtypeStruct((s, p + 2 * c), BF16), jax.ShapeDtypeStruct((4, SUBLANES, c), F32),
                   jax.ShapeDtypeStruct((SUBLANES, c), F32)],
        scratch_shapes=[pltpu.VMEM((t, p + c), F32)],
        compiler_params=_params("arbitrary"),
    )(e_pool, e_pool, e_pool, dxc, dxc, dxc, proj, proj, proj, dgate, conv_w)


def _dx_call(dproj, w_in, dz1, dep):
    s, e = dproj.shape
    n, d, e4 = w_in.shape
    tm = min(MM_TILE, s)

    def body(dp_ref, w_hbm, dz_ref, dep_ref, o_ref, w_s, sems):
        @pl.when(pl.program_id(0) == 0)
        def _():
            copies = [pltpu.make_async_copy(w_hbm.at[k], w_s.at[:, pl.ds(k * e4, e4)], sems.at[k]) for k in range(n)]
            for cp in copies:
                cp.start()
            for cp in copies:
                cp.wait()

        o_ref[...] = ALPHA * dz_ref[...] + lax.dot_general(
            dp_ref[...], w_s[...], (((1,), (1,)), ((), ())), preferred_element_type=F32)

    return pl.pallas_call(
        body, name="grad_x", grid=(s // tm,),
        in_specs=[pl.BlockSpec((tm, e), lambda i: (i, 0)), ANY, pl.BlockSpec((tm, d), lambda i: (i, 0)), ANY],
        out_specs=pl.BlockSpec((tm, d), lambda i: (i, 0)),
        out_shape=jax.ShapeDtypeStruct((s, d), F32),
        scratch_shapes=[pltpu.VMEM((d, e), BF16), pltpu.SemaphoreType.DMA((n,))],
        compiler_params=_params("arbitrary"),
    )(dproj, w_in, dz1, dep)


def _row_tile(rows, cols, n_arrays):
    limit = max(SUBLANES, ELT_BLOCK_BYTES // (4 * cols * max(1, n_arrays // 4)))
    best = SUBLANES
    for cand in range(SUBLANES, min(rows, limit) + 1, SUBLANES):
        if rows % cand == 0:
            best = cand
    return best if rows % SUBLANES == 0 else rows


def _cast_to_slot_call(a, idx, dtype, name, dep):
    rows, cols = a.shape
    tr = _row_tile(rows, cols, 2)
    extra = [] if dep is None else [dep]

    def body(idx_ref, a_ref, *rest):
        rest[-1][...] = a_ref[...].astype(dtype)

    return pl.pallas_call(
        body, name=name,
        grid_spec=pltpu.PrefetchScalarGridSpec(
            num_scalar_prefetch=1, grid=(rows // tr,),
            in_specs=[pl.BlockSpec((tr, cols), lambda i, idx_ref: (i, 0))] + [ANY] * len(extra),
            out_specs=pl.BlockSpec((None, tr, cols), lambda i, idx_ref: (idx_ref[1], i, 0))),
        out_shape=jax.ShapeDtypeStruct((N_CHIPS, rows, cols), dtype),
        compiler_params=_params("arbitrary"),
    )(idx, a, *extra)


def _add_half_call(g, recv, idx, name):
    _, rows, cols = g.shape
    tr = _row_tile(rows, cols, 3)

    def body(idx_ref, g_ref, r_ref, o_ref):
        o_ref[...] = g_ref[...] + r_ref[...]

    return pl.pallas_call(
        body, name=name,
        grid_spec=pltpu.PrefetchScalarGridSpec(
            num_scalar_prefetch=1, grid=(rows // tr,),
            in_specs=[pl.BlockSpec((None, tr, cols), lambda i, idx_ref: (idx_ref[0], i, 0)),
                      pl.BlockSpec((tr, cols), lambda i, idx_ref: (i, 0))],
            out_specs=pl.BlockSpec((None, tr, cols), lambda i, idx_ref: (idx_ref[1], i, 0))),
        out_shape=jax.ShapeDtypeStruct((N_CHIPS, rows, cols), F32),
        compiler_params=_params("arbitrary"),
    )(idx, g, recv)


def _sum_chips_call(own, recv, idx, name):
    _, rows, cols = recv.shape
    tr = _row_tile(rows, cols, 5)
    out_spec = pl.BlockSpec((None, tr, cols), lambda i, idx_ref: (idx_ref[0], i, 0))
    if own is None:
        def body(idx_ref, r_ref, o_ref):
            o_ref[...] = ((r_ref[0] + r_ref[1]) + r_ref[2]) + r_ref[3]
        in_specs = [pl.BlockSpec((N_CHIPS, tr, cols), lambda i, idx_ref: (0, i, 0))]
        args = (recv,)
    else:
        def body(idx_ref, p_ref, r_ref, o_ref):
            o_ref[...] = ((p_ref[...] + r_ref[0]) + r_ref[1]) + r_ref[2]
        in_specs = [pl.BlockSpec((None, tr, cols), lambda i, idx_ref: (idx_ref[1], i, 0)),
                    pl.BlockSpec((N_CHIPS - 1, tr, cols), lambda i, idx_ref: (0, i, 0))]
        args = (own, recv)
    return pl.pallas_call(
        body, name=name,
        grid_spec=pltpu.PrefetchScalarGridSpec(num_scalar_prefetch=1, grid=(rows // tr,), in_specs=in_specs, out_specs=out_spec),
        out_shape=jax.ShapeDtypeStruct((2, rows, cols), F32),
        compiler_params=_params("arbitrary"),
    )(idx, *args)


def _adamw_call(g, w, m, v, name):
    rows, cols = w.shape
    tr = _row_tile(rows, cols, 4)

    def body(g_ref, w_ref, m_ref, v_ref, go_ref, d_ref, mo_ref, vo_ref):
        gv = g_ref[...]
        go_ref[...] = gv
        mn = ADAM_B1 * m_ref[...] + (1.0 - ADAM_B1) * gv
        vn = ADAM_B2 * v_ref[...] + (1.0 - ADAM_B2) * (gv * gv)
        m_hat = mn / (1.0 - ADAM_B1 ** ADAM_STEP)
        v_hat = vn / (1.0 - ADAM_B2 ** ADAM_STEP)
        d_ref[...] = -ADAM_LR * (m_hat / (jnp.sqrt(v_hat) + ADAM_EPS) + ADAM_WD * w_ref[...])
        mo_ref[...] = mn
        vo_ref[...] = vn

    spec = pl.BlockSpec((tr, cols), lambda i: (i, 0))
    shape = jax.ShapeDtypeStruct((rows, cols), F32)
    return pl.pallas_call(
        body, name=name, grid=(rows // tr,),
        in_specs=[spec] * 4, out_specs=[spec] * 4, out_shape=[shape] * 4,
        compiler_params=_params("arbitrary"),
    )(g, w, m, v)


def _mesh_place():
    x, y, c = lax.axis_index("x"), lax.axis_index("y"), lax.axis_index("c")
    chips = [(1 - x, y), (x, 1 - y), (1 - x, 1 - y)]
    return x, y, c, chips


def _remote(src, dst, send_sems, recv_sems, idx, device):
    return pltpu.make_async_remote_copy(src_ref=src, dst_ref=dst, send_sem=send_sems.at[idx], recv_sem=recv_sems.at[idx],
                                        device_id=device, device_id_type=MESH)


HBM_SPEC = pl.BlockSpec(memory_space=pltpu.HBM)
SEM_SPEC = pl.BlockSpec(memory_space=pltpu.SEMAPHORE)
ORDERED_EFFECT = pltpu.SideEffectType.DATAFLOW_SIDE_EFFECTING


def _in_hbm(a):
    return pltpu.with_memory_space_constraint(a, pltpu.HBM)


def _start_copies_call(name, bufs, groups, after=None):
    n, g = len(bufs), len(groups)
    extra = [] if after is None else [after]
    first_out = n + len(extra)

    def body(*refs):
        outs = refs[first_out:first_out + n]
        sems = refs[first_out + n:first_out + n + 2 * g]
        token = refs[first_out + n + 2 * g]
        for i, (which, copies_fn, _) in enumerate(groups):
            for mine, _ in copies_fn([outs[w] for w in which], sems[2 * i], sems[2 * i + 1]):
                mine.start()
        token[...] = jnp.zeros_like(token)

    sem_shapes = [pltpu.SemaphoreType.DMA((cnt,)) for _, _, cnt in groups for _ in range(2)]
    res = pl.pallas_call(
        body, name=name,
        in_specs=[HBM_SPEC] * n + [ANY] * len(extra),
        out_specs=[HBM_SPEC] * n + [SEM_SPEC] * (2 * g) + [pl.BlockSpec(memory_space=pltpu.VMEM)],
        out_shape=[pltpu.HBM(a.shape, a.dtype) for a in bufs] + sem_shapes + [jax.ShapeDtypeStruct((SUBLANES, LANES), F32)],
        input_output_aliases={a: a for a in range(n)},
        compiler_params=pltpu.CompilerParams(has_side_effects=ORDERED_EFFECT),
    )(*[_in_hbm(a) for a in bufs], *extra)
    sems = res[n:n + 2 * g]
    return list(res[:n]), [(sems[2 * i], sems[2 * i + 1]) for i in range(g)], res[n + 2 * g]


def _wait_copies_call(name, bufs, sems, copies_fn, after):
    n = len(bufs)

    def body(*refs):
        ins = refs[:n]
        send_sems, recv_sems = refs[n], refs[n + 1]
        for mine, arriving in copies_fn(list(ins), send_sems, recv_sems):
            arriving.wait_recv()
            mine.wait_send()

    res = pl.pallas_call(
        body, name=name,
        in_specs=[HBM_SPEC] * n + [SEM_SPEC, SEM_SPEC, ANY],
        out_specs=[HBM_SPEC] * n,
        out_shape=[pltpu.HBM(a.shape, a.dtype) for a in bufs],
        input_output_aliases={a: a for a in range(n)},
        compiler_params=pltpu.CompilerParams(has_side_effects=ORDERED_EFFECT),
    )(*bufs, sems[0], sems[1], after)
    return list(res)


def _gather_copies(bufs, send_sems, recv_sems):
    x, y, c, chips = _mesh_place()
    k = 2 * x + y
    out = []
    for a, buf in enumerate(bufs):
        for j, (px, py) in enumerate(chips):
            kj = 2 * px + py
            mine = _remote(buf.at[k, c], buf.at[k, c], send_sems, recv_sems, 3 * a + j, (px, py, c))
            arriving = _remote(buf.at[k, c], buf.at[kj, c], send_sems, recv_sems, 3 * a + j, (px, py, c))
            out.append((mine, arriving))
    return out


def _exchange_copies(n_sharded, n_replicated):
    def copies(bufs, send_sems, recv_sems):
        x, y, c, chips = _mesh_place()
        k = 2 * x + y
        sums, lands = bufs[:n_sharded], bufs[n_sharded:2 * n_sharded]
        repl = bufs[2 * n_sharded:]
        out = []
        for j, (px, py) in enumerate(chips):
            kj = 2 * px + py
            for a in range(n_sharded):
                cp = _remote(sums[a].at[kj], lands[a].at[j], send_sems, recv_sems, 3 * a + j, (px, py, c))
                out.append((cp, cp))
            for a in range(n_replicated):
                idx = 3 * (n_sharded + a) + j
                mine = _remote(repl[a].at[k], repl[a].at[k], send_sems, recv_sems, idx, (px, py, c))
                arriving = _remote(repl[a].at[k], repl[a].at[kj], send_sems, recv_sems, idx, (px, py, c))
                out.append((mine, arriving))
        return out
    return copies


def _sibling_copies(n, halves):
    def copies(bufs, send_sems, recv_sems):
        x, y, c, _ = _mesh_place()
        out = []
        for a in range(n):
            src = bufs[a].at[1 - c] if halves else bufs[a]
            cp = _remote(src, bufs[n + a], send_sems, recv_sems, a, (x, y, 1 - c))
            out.append((cp, cp))
        return out
    return copies


def _forward_copies(bufs, send_sems, recv_sems):
    x, y, c, chips = _mesh_place()
    out = []
    for a, buf in enumerate(bufs):
        for j, (px, py) in enumerate(chips):
            kj = 2 * px + py
            mine = _remote(buf.at[kj, c], buf.at[kj, c], send_sems, recv_sems, 3 * a + j, (x, y, 1 - c))
            arriving = _remote(buf.at[kj, c], buf.at[kj, 1 - c], send_sems, recv_sems, 3 * a + j, (x, y, 1 - c))
            out.append((mine, arriving))
    return out


def _join_copies(bufs, send_sems, recv_sems):
    x, y, c, _ = _mesh_place()
    out = []
    for a, buf in enumerate(bufs):
        mine = _remote(buf.at[c], buf.at[c], send_sems, recv_sems, a, (x, y, 1 - c))
        arriving = _remote(buf.at[c], buf.at[1 - c], send_sems, recv_sems, a, (x, y, 1 - c))
        out.append((mine, arriving))
    return out


def _forward_to_sibling_call(bufs, name):
    n = len(bufs)

    def body(*refs):
        ins, outs = refs[:n], refs[n:2 * n]
        send_sems, recv_sems = refs[2 * n:]
        x, y, c, chips = _mesh_place()
        sibling = (x, y, 1 - c)
        sends = []
        for a in range(n):
            for j, (px, py) in enumerate(chips):
                kj = 2 * px + py
                sends.append(_remote(ins[a].at[kj, c], outs[a].at[kj, c], send_sems, recv_sems, 3 * a + j, sibling))
        for cp in sends:
            cp.start()
        for a in range(n):
            for j, (px, py) in enumerate(chips):
                kj = 2 * px + py
                _remote(ins[a].at[kj, c], outs[a].at[kj, 1 - c], send_sems, recv_sems, 3 * a + j, sibling).wait_recv()
        for cp in sends:
            cp.wait_send()

    return pl.pallas_call(
        body, name=name,
        in_specs=[ANY] * n, out_specs=[ANY] * n,
        out_shape=[jax.ShapeDtypeStruct(a.shape, a.dtype) for a in bufs],
        input_output_aliases={a: a for a in range(n)},
        scratch_shapes=[pltpu.SemaphoreType.DMA((3 * n,)), pltpu.SemaphoreType.DMA((3 * n,))],
    )(*bufs)


def _pack(arrays, rows_multiple):
    flat = jnp.concatenate([a.reshape(-1) for a in arrays])
    per = LANES * rows_multiple
    padded = -(-flat.shape[0] // per) * per
    flat = jnp.pad(flat, (0, padded - flat.shape[0]))
    return flat.reshape(-1, LANES)


def _unpack(packed, shapes):
    flat = packed.reshape(-1)
    out, at = [], 0
    for shp in shapes:
        size = 1
        for dim in shp:
            size *= dim
        out.append(flat[at:at + size].reshape(shp))
        at += size
    return out


def _halves(a):
    return a.reshape((2, a.shape[0] // 2) + a.shape[1:])


def kernel(x, ln_mix_g, ln_mix_b, w_in, w_pool, pool_scale, conv_w, conv_b, w_rg_a, b_rg_a, w_rg_i, b_rg_i, rg_lambda, w_out, ln_ffn_g, ln_ffn_b, w_mlp_in, w_mlp_out, loss_target, m_ln_mix_g, m_ln_mix_b, m_w_in, m_w_pool, m_pool_scale, m_conv_w, m_conv_b, m_w_rg_a, m_b_rg_a, m_w_rg_i, m_b_rg_i, m_rg_lambda, m_w_out, m_ln_ffn_g, m_ln_ffn_b, m_w_mlp_in, m_w_mlp_out, v_ln_mix_g, v_ln_mix_b, v_w_in, v_w_pool, v_pool_scale, v_conv_w, v_conv_b, v_w_rg_a, v_b_rg_a, v_w_rg_i, v_b_rg_i, v_rg_lambda, v_w_out, v_ln_ffn_g, v_ln_ffn_b, v_w_mlp_in, v_w_mlp_out):
    weights = dict(ln_mix_g=ln_mix_g, ln_mix_b=ln_mix_b, w_in=w_in, w_pool=w_pool, pool_scale=pool_scale, conv_w=conv_w,
                   conv_b=conv_b, w_rg_a=w_rg_a, b_rg_a=b_rg_a, w_rg_i=w_rg_i, b_rg_i=b_rg_i, rg_lambda=rg_lambda,
                   w_out=w_out, ln_ffn_g=ln_ffn_g, ln_ffn_b=ln_ffn_b, w_mlp_in=w_mlp_in, w_mlp_out=w_mlp_out)
    m_in = dict(ln_mix_g=m_ln_mix_g, ln_mix_b=m_ln_mix_b, w_in=m_w_in, w_pool=m_w_pool, pool_scale=m_pool_scale,
                conv_w=m_conv_w, conv_b=m_conv_b, w_rg_a=m_w_rg_a, b_rg_a=m_b_rg_a, w_rg_i=m_w_rg_i, b_rg_i=m_b_rg_i,
                rg_lambda=m_rg_lambda, w_out=m_w_out, ln_ffn_g=m_ln_ffn_g, ln_ffn_b=m_ln_ffn_b, w_mlp_in=m_w_mlp_in,
                w_mlp_out=m_w_mlp_out)
    v_in = dict(ln_mix_g=v_ln_mix_g, ln_mix_b=v_ln_mix_b, w_in=v_w_in, w_pool=v_w_pool, pool_scale=v_pool_scale,
                conv_w=v_conv_w, conv_b=v_conv_b, w_rg_a=v_w_rg_a, b_rg_a=v_b_rg_a, w_rg_i=v_w_rg_i, b_rg_i=v_b_rg_i,
                rg_lambda=v_rg_lambda, w_out=v_w_out, ln_ffn_g=v_ln_ffn_g, ln_ffn_b=v_ln_ffn_b, w_mlp_in=v_w_mlp_in,
                w_mlp_out=v_w_mlp_out)
    names = list(weights)

    xs = x[0]
    tgt = loss_target[0]
    s, d = xs.shape
    p = c = d // 2
    pg = p // N_POOL_GROUPS
    core = lax.axis_index("c")
    shard = 2 * lax.axis_index("x") + lax.axis_index("y")

    idx = jnp.stack([core, shard]).astype(jnp.int32)
    small_shard = _pack([conv_w[0], b_rg_a[0], b_rg_i[0], rg_lambda[0]], 2 * SUBLANES)
    to_gather = [(w_in[0], BF16), (w_out[0], BF16), (w_mlp_in[0], BF16), (w_mlp_out[0], BF16),
                 (w_pool[0].reshape(-1, pg), BF16), (small_shard, F32)]

    def slot_view(i, dep):
        a, dt = to_gather[i]
        sl = _cast_to_slot_call(a, idx, dt, f"gather_slot_{i}", dep)
        return sl.reshape(N_CHIPS, 2, sl.shape[1] // 2, sl.shape[2])

    first, later = (0, 4, 5), (1, 2, 3)
    fly_a, sems_a, token_a = _start_copies_call(
        "gather_start_first", [slot_view(i, None) for i in first], [((0, 1, 2), _gather_copies, 3 * len(first))])
    later_views = []
    for i in later:
        later_views.append(slot_view(i, later_views[-1] if later_views else token_a))
    xb = _cast_call(xs, later_views[-1])
    got_first = _wait_copies_call("gather_wait_w_in", fly_a, sems_a[0], _gather_copies, xb)
    fly_b, sems_b, g_token = _start_copies_call(
        "gather_start_later", later_views + got_first,
        [((0,), _gather_copies, 3), ((1,), _gather_copies, 3), ((2,), _gather_copies, 3)])
    got_first = fly_b[len(later):]
    in_flight = dict(zip(later, fly_b))
    g_sems = [None] + list(sems_b)

    def arrive(which, group, after, tag):
        return _wait_copies_call(f"gather_wait_{tag}", [in_flight[w] for w in which], g_sems[group], _gather_copies, after)

    def pass_on(got, tag):
        flying, sems, token = _start_copies_call(
            f"gather_forward_start_{tag}", got, [(tuple(range(len(got))), _forward_copies, 3 * len(got))])
        return (flying, sems[0], tag), token

    def passed_on(state, after):
        flying, sems, tag = state
        return _wait_copies_call(f"gather_forward_wait_{tag}", flying, sems, _forward_copies, after)

    gathered = [None] * len(to_gather)
    gathered[0], gathered[4], gathered[5] = _forward_to_sibling_call(got_first, "gather_forward_w_in")
    w_in_f = gathered[0].reshape((N_CHIPS,) + w_in.shape[1:])
    w_pool_f = gathered[4].reshape(N_CHIPS, N_POOL_GROUPS, pg // N_CHIPS, pg).transpose(1, 0, 2, 3).reshape(N_POOL_GROUPS, pg, pg)
    c4 = c // N_CHIPS
    small_parts = [_unpack(gathered[5][k].reshape(-1, LANES), [(4, c4), (2, c4), (2, c4), (2, c4)]) for k in range(N_CHIPS)]
    conv_w_f = jnp.concatenate([sp_[0] for sp_ in small_parts], axis=1)
    b_a_f = jnp.concatenate([sp_[1] for sp_ in small_parts], axis=1)
    b_i_f = jnp.concatenate([sp_[2] for sp_ in small_parts], axis=1)
    lam_f = jnp.concatenate([sp_[3] for sp_ in small_parts], axis=1)
    wa_b = w_rg_a[0].astype(BF16)
    wi_b = w_rg_i[0].astype(BF16)

    proj = _proj_call(xb, w_in_f)
    xc = _conv_call(proj, conv_w_f, conv_b, c)
    fwd_w_out, token = pass_on(arrive((1,), 1, xc, "w_out"), "w_out")
    h_b, *gates_b = _scan_fwd_call(xc, wa_b[1], wi_b[1], b_a_f[1:2], b_i_f[1:2], lam_f[1:2], True, token)
    h_f, *gates_f = _scan_fwd_call(xc, wa_b[0], wi_b[0], b_a_f[0:1], b_i_f[0:1], lam_f[0:1], False, token)
    y, d_pool = _pool_combine_call(proj, h_f, h_b, w_pool_f, pool_scale, p)
    w_out_f = passed_on(fwd_w_out, y)[0].reshape(d, d)
    fwd_w1, token = pass_on(arrive((2,), 2, y, "w_mlp_in"), "w_mlp_in")
    xh1, x1b, rstd1 = _out_ln1_call(y, w_out_f, xs, ln_mix_g, ln_mix_b, token)
    w1_f = passed_on(fwd_w1, x1b)[0].reshape((N_CHIPS,) + w_mlp_in.shape[1:])
    first_half = _mlp_in_call(x1b, w1_f, g_token, None)
    fwd_w2, token = pass_on(arrive((3,), 3, first_half[0], "w_mlp_out"), "w_mlp_out")
    r_act, hsq = _mlp_in_call(x1b, w1_f, token, first_half)
    w2_f = passed_on(fwd_w2, hsq)[0].reshape(N_CHIPS * w_mlp_out.shape[1], d)
    dz2, dz2b, loss8, dg2, db2 = _mlp_out_ln2_call(hsq, w2_f, xh1, ln_mix_g, ln_mix_b, ln_ffn_g, ln_ffn_b, tgt)

    def start_siblings(grads, halves, tag, after=None):
        lands = [lax.empty(g.shape[1:] if halves else g.shape, g.dtype) for g in grads]
        copies = _sibling_copies(len(grads), halves)
        flying, sems, token = _start_copies_call(
            f"siblings_start_{tag}", list(grads) + lands, [(tuple(range(2 * len(grads))), copies, len(grads))], after)
        return (flying, sems[0], copies, len(grads), tag), token

    def finish_siblings(state, after):
        flying, sems, copies, n, tag = state
        got = _wait_copies_call(f"siblings_wait_{tag}", flying, sems, copies, after)
        return got[:n], got[n:]

    half_own = jnp.reshape(core, (1,)).astype(jnp.int32)
    half_sibling = 1 - half_own

    def chip_sum_of(a, b, row_sharded, tag, dep, overlapped):
        for_sibling = _half_grad_call(a, b, half_sibling, row_sharded, None, f"grad_{tag}_for_sibling", dep)
        state, token = start_siblings([for_sibling], False, tag)
        results = overlapped(token)
        _, (from_sibling,) = finish_siblings(state, results[0])
        return _half_grad_call(a, b, half_own, row_sharded, from_sibling, f"grad_{tag}", token), results

    def start_exchange(sums, n_repl, tag):
        n_sh = len(sums) - n_repl
        lands = [lax.empty((N_CHIPS - 1,) + a.shape[1:], a.dtype) for a in sums[:n_sh]]
        bufs = sums[:n_sh] + lands + sums[n_sh:]
        copies = _exchange_copies(n_sh, n_repl)
        flying, sems, token = _start_copies_call(
            f"reduce_start_{tag}", bufs, [(tuple(range(len(bufs))), copies, 3 * len(sums))])
        return (flying, sems[0], copies, n_sh, tag), token

    def finish_exchange(state, after):
        flying, sems, copies, n_sh, tag = state
        got = _wait_copies_call(f"reduce_wait_{tag}", flying, sems, copies, after)
        halves = []
        for a in range(n_sh):
            own, land = got[a], got[n_sh + a]
            cols = own.shape[-1]
            total = _sum_chips_call(own.reshape(N_CHIPS, -1, cols), land.reshape(N_CHIPS - 1, -1, cols), idx,
                                    f"reduce_sum_{tag}_{a}")
            halves.append(total.reshape((2,) + own.shape[1:]))
        for a, rp in enumerate(got[2 * n_sh:]):
            halves.append(_sum_chips_call(None, rp, idx, f"reduce_sum_{tag}_r{a}"))
        return halves

    def start_join(halves, tag):
        flying, sems, token = _start_copies_call(
            f"join_start_{tag}", halves, [(tuple(range(len(halves))), _join_copies, len(halves))])
        return (flying, sems[0], tag), token

    def finish_join(state, after):
        flying, sems, tag = state
        return _wait_copies_call(f"join_wait_{tag}", flying, sems, _join_copies, after)

    sum_w2, (dpre,) = chip_sum_of(hsq, dz2b, True, "w_mlp_out", g_token,
                                  lambda tok: (_dhsq_call(dz2b, w2_f, r_act, tok),))
    flying_w2, token = start_exchange([sum_w2], 0, "w2")
    sum_w1, (dz1, dz1b, dg1, db1) = chip_sum_of(
        x1b, dpre, False, "w_mlp_in", token,
        lambda tok: _dx1_ln1_bwd_call(dpre, w1_f, dz2, xh1, rstd1, ln_mix_g, tok))
    flying_w1, token = start_exchange([sum_w1], 0, "w1")

    def dy_and_mixer(tok):
        dy = _dy_call(dz1b, w_out_f, tok)
        return _mixer_bwd_call(dy, d_pool, proj, h_f, h_b, w_pool_f, pool_scale, p)

    sum_wout, (e_pool, dh, dgate, g_wpool, g_pscale8) = chip_sum_of(y, dz1b, True, "w_out", token, dy_and_mixer)
    flying_wout, token = start_exchange([sum_wout], 0, "w_out")
    dxc0, g_wa0, g_wi0, g_ba0, g_bi0, g_sp0 = _scan_bwd_call(
        xc, dh, h_f, gates_f, None, wa_b[0], wi_b[0], lam_f[0:1], False, token)
    dxc, g_wa1, g_wi1, g_ba1, g_bi1, g_sp1 = _scan_bwd_call(
        xc, dh, h_b, gates_b, dxc0, wa_b[1], wi_b[1], lam_f[1:2], True, token)
    dproj, g_cw8, g_cb8 = _dproj_call(e_pool, dxc, proj, dgate, conv_w_f, p)

    rowsum = lambda a8: jnp.sum(a8, axis=-2)
    g_lam = jnp.stack([rowsum(g_sp0), rowsum(g_sp1)]) * (-_sigmoid(-lam_f))
    small_grads = {
        "ln_mix_g": rowsum(dg1), "ln_mix_b": rowsum(db1), "ln_ffn_g": rowsum(dg2), "ln_ffn_b": rowsum(db2),
        "pool_scale": rowsum(g_pscale8), "conv_b": rowsum(g_cb8),
        "w_rg_a": jnp.stack([g_wa0, g_wa1]), "w_rg_i": jnp.stack([g_wi0, g_wi1]),
        "w_pool": g_wpool, "conv_w": rowsum(g_cw8),
        "b_rg_a": jnp.stack([rowsum(g_ba0), rowsum(g_ba1)]), "b_rg_i": jnp.stack([rowsum(g_bi0), rowsum(g_bi1)]),
        "rg_lambda": g_lam,
    }
    small_names = list(small_grads)
    small_shapes = [small_grads[nm].shape for nm in small_names]
    loss_share = jnp.reshape(jnp.sum(loss8) * (0.5 / d), (1,))
    g_small = _halves(_pack([small_grads[nm] for nm in small_names] + [loss_share], 2 * SUBLANES))
    sib_small, token = start_siblings([g_small], True, "small")
    flying_small = []

    def small_exchange_and_grad_x(tok):
        (mine,), (theirs,) = finish_siblings(sib_small, tok)
        small_sum = _add_half_call(mine, theirs, idx, "reduce_add_small")
        state, tok = start_exchange([small_sum], 1, "small")
        flying_small.append(state)
        return (_dx_call(dproj, w_in_f, dz1, tok),)

    sum_win, (grad_x,) = chip_sum_of(xb, dproj, False, "w_in", token, small_exchange_and_grad_x)
    flying_small = flying_small[0]
    flying_win, token = start_exchange([sum_win], 0, "w_in")

    grad_w, delta_w, new_m, new_v = {}, {}, {}, {}

    def adamw(nm, full):
        w2d = weights[nm][0]
        g2d = full.reshape(w2d.shape)
        go, dl, mn, vn = _adamw_call(g2d, w2d, m_in[nm][0], v_in[nm][0], f"adamw_{nm}")
        grad_w[nm], delta_w[nm], new_m[nm], new_v[nm] = go[None], dl[None], mn[None], vn[None]
        return vn

    join_w2, token = start_join(finish_exchange(flying_w2, token), "w2")
    join_w1, token = start_join(finish_exchange(flying_w1, token), "w1")
    join_wout, token = start_join(finish_exchange(flying_wout, token), "w_out")
    last = adamw("w_mlp_out", finish_join(join_w2, token)[0])
    last = adamw("w_mlp_in", finish_join(join_w1, last)[0])
    last = adamw("w_out", finish_join(join_wout, last)[0])

    join_small, token = start_join(finish_exchange(flying_small, last), "small")
    join_win, token = start_join(finish_exchange(flying_win, token), "w_in")
    small_joined = finish_join(join_small, token)[0]
    *small_sums, loss_sum = _unpack(small_joined.reshape(-1, LANES), small_shapes + [(1,)])
    small_full = dict(zip(small_names, small_sums))
    local = dict(small_full)
    local["w_pool"] = lax.dynamic_slice_in_dim(small_full["w_pool"], shard * (pg // N_CHIPS), pg // N_CHIPS, axis=1)
    for nm in ("conv_w", "b_rg_a", "b_rg_i", "rg_lambda"):
        local[nm] = lax.dynamic_slice_in_dim(small_full[nm], shard * c4, c4, axis=1)
    small_w_shapes = [weights[nm].shape for nm in small_names]
    g_pack = _pack([local[nm] for nm in small_names], SUBLANES)
    w_pack = _pack([weights[nm] for nm in small_names], SUBLANES)
    m_pack = _pack([m_in[nm] for nm in small_names], SUBLANES)
    v_pack = _pack([v_in[nm] for nm in small_names], SUBLANES)
    _, dl_p, mn_p, vn_p = _adamw_call(g_pack, w_pack, m_pack, v_pack, "adamw_small")
    for nm, gl, dl, mn, vn in zip(small_names, _unpack(g_pack, small_w_shapes), _unpack(dl_p, small_w_shapes),
                                  _unpack(mn_p, small_w_shapes), _unpack(vn_p, small_w_shapes)):
        grad_w[nm], delta_w[nm], new_m[nm], new_v[nm] = gl, dl, mn, vn
    adamw("w_in", finish_join(join_win, vn_p)[0])

    loss = loss_sum[0]
    return (loss, grad_x[None], *[grad_w[nm] for nm in names], *[delta_w[nm] for nm in names],
            *[new_m[nm] for nm in names], *[new_v[nm] for nm in names])
```

```python
import jax
import jax.numpy as jnp
from jax import lax
from jax.experimental import pallas as pl
from jax.experimental.pallas import tpu as pltpu

F32 = jnp.float32
BF16 = jnp.bfloat16

N_CHIPS = 4
LANES = 128
SUBLANES = 8
LRU_HEAD = 128
N_POOL_GROUPS = 4
POOL_WINDOWS = (2, 4, 8, 16)
RG_C = 8.0
LN_EPS = 1e-5
ALPHA = 2.0 ** 0.25
ADAM_LR, ADAM_B1, ADAM_B2, ADAM_EPS, ADAM_WD, ADAM_STEP = 0.001, 0.9, 0.999, 1e-08, 0.01, 10
VMEM_LIMIT = 56 * 1024 * 1024
SEQ_TILE = 256
MM_TILE = 512
LN_MM_K = 2048
LN_UNROLL = 8
ELT_BLOCK_BYTES = 2 * 1024 * 1024
RESIDENT_OPERAND_BYTES = 16 * 1024 * 1024
MESH = pl.DeviceIdType.MESH
ANY = pl.BlockSpec(memory_space=pl.ANY)


def _params(*sem):
    return pltpu.CompilerParams(dimension_semantics=sem, vmem_limit_bytes=VMEM_LIMIT)


def _sigmoid(z):
    return 1.0 / (1.0 + jnp.exp(-z))


def _neg_expm1(z):
    series = -(z * (1.0 + z * (0.5 + z * (1.0 / 6.0 + z * (1.0 / 24.0)))))
    return jnp.where(z > -0.01, series, 1.0 - jnp.exp(z))


def _softplus(z):
    return jnp.maximum(z, 0.0) + jnp.log1p(jnp.exp(-jnp.abs(z)))


_GELU_K = 0.7978845608028654
_GELU_C = 0.044715


def _gelu_and_grad(u):
    t = jnp.tanh(_GELU_K * (u + _GELU_C * (u * u * u)))
    g = 0.5 * u * (1.0 + t)
    dg = 0.5 * (1.0 + t) + 0.5 * u * (1.0 - t * t) * (_GELU_K * (1.0 + 3.0 * _GELU_C * u * u))
    return g, dg


def _shift_rows(prv, cur, nxt, o, rows):
    if o == 0:
        return cur
    if o == SUBLANES:
        return nxt
    if o == -SUBLANES:
        return prv
    if o > 0:
        return pltpu.roll(jnp.where(rows >= o, cur, nxt), SUBLANES - o, 0)
    p = -o
    return pltpu.roll(jnp.where(rows < SUBLANES - p, cur, prv), p, 0)


def _neighbour_chunks(main_ref, prev_ref, next_ref, r0, t_rows, cols, first_tile, last_tile):
    cur = main_ref[pl.ds(r0, SUBLANES), cols]
    before = main_ref[pl.ds(pl.multiple_of(jnp.maximum(r0 - SUBLANES, 0), SUBLANES), SUBLANES), cols]
    after = main_ref[pl.ds(pl.multiple_of(jnp.minimum(r0 + SUBLANES, t_rows - SUBLANES), SUBLANES), SUBLANES), cols]
    halo_prev = jnp.where(first_tile, 0.0, prev_ref[:, cols])
    halo_next = jnp.where(last_tile, 0.0, next_ref[:, cols])
    prv = jnp.where(r0 == 0, halo_prev, before)
    nxt = jnp.where(r0 == t_rows - SUBLANES, halo_next, after)
    return prv, cur, nxt


def _halo_specs(t_rows, n_rows, width, col_block):
    per = t_rows // SUBLANES
    last = n_rows // SUBLANES - 1
    return [
        pl.BlockSpec((t_rows, width), lambda i: (i, col_block)),
        pl.BlockSpec((SUBLANES, width), lambda i: (jnp.maximum(i * per - 1, 0), col_block)),
        pl.BlockSpec((SUBLANES, width), lambda i: (jnp.minimum((i + 1) * per, last), col_block)),
    ]


def _chunk_loop(t_rows, fn, init=None, unroll=1, descending=False):
    span = SUBLANES * unroll

    def step(ci, carry):
        base = pl.multiple_of(((t_rows // span - 1 - ci) if descending else ci) * span, span)
        for u in range(unroll):
            carry = fn(base + ((unroll - 1 - u) if descending else u) * SUBLANES, carry)
        return carry
    return lax.fori_loop(0, t_rows // span, step, init)


def _scan_chunk(a, b, h_in, rows, reverse):
    for dist in (1, 2, 4):
        if reverse:
            keep = rows < SUBLANES - dist
            shift = SUBLANES - dist
        else:
            keep = rows >= dist
            shift = dist
        b = a * jnp.where(keep, pltpu.roll(b, shift, 0), 0.0) + b
        a = a * jnp.where(keep, pltpu.roll(a, shift, 0), 1.0)
    return a * h_in + b


def _cast_call(x, dep):
    s, d = x.shape
    tm = min(MM_TILE, s)

    def body(x_ref, dep_ref, o_ref):
        o_ref[...] = x_ref[...].astype(BF16)

    return pl.pallas_call(
        body, name="cast_x", grid=(s // tm,),
        in_specs=[pl.BlockSpec((tm, d), lambda i: (i, 0)), ANY],
        out_specs=pl.BlockSpec((tm, d), lambda i: (i, 0)),
        out_shape=jax.ShapeDtypeStruct((s, d), BF16),
        compiler_params=_params("arbitrary"),
    )(x, dep)


def _proj_call(xb, w_in):
    s, d = xb.shape
    n, _, e4 = w_in.shape
    tm = min(MM_TILE, s)

    def body(x_ref, w_hbm, proj_ref, w_s, sems):
        @pl.when(pl.program_id(0) == 0)
        def _():
            copies = [pltpu.make_async_copy(w_hbm.at[k], w_s.at[:, pl.ds(k * e4, e4)], sems.at[k]) for k in range(n)]
            for cp in copies:
                cp.start()
            for cp in copies:
                cp.wait()

        proj_ref[...] = jnp.dot(x_ref[...], w_s[...], preferred_element_type=F32)

    return pl.pallas_call(
        body, name="proj", grid=(s // tm,),
        in_specs=[pl.BlockSpec((tm, d), lambda i: (i, 0)), ANY],
        out_specs=pl.BlockSpec((tm, n * e4), lambda i: (i, 0)),
        out_shape=jax.ShapeDtypeStruct((s, n * e4), F32),
        scratch_shapes=[pltpu.VMEM((d, n * e4), BF16), pltpu.SemaphoreType.DMA((n,))],
        compiler_params=_params("arbitrary"),
    )(xb, w_in)


def _conv_call(proj, conv_w, conv_b, c):
    s = proj.shape[0]
    t = min(SEQ_TILE, s)
    n_tiles = s // t

    def body(u_ref, up_ref, un_ref, w_ref, b_ref, xc_ref):
        i = pl.program_id(0)
        rows = lax.broadcasted_iota(jnp.int32, (SUBLANES, c), 0)
        w = w_ref[...]
        b = b_ref[...]

        def chunk(r0, _):
            prv, cur, nxt = _neighbour_chunks(u_ref, up_ref, un_ref, r0, t, slice(None), i == 0, i == n_tiles - 1)
            acc = b + w[1:2] * cur
            acc += w[0:1] * _shift_rows(prv, cur, nxt, -1, rows)
            acc += w[2:3] * _shift_rows(prv, cur, nxt, 1, rows)
            acc += w[3:4] * _shift_rows(prv, cur, nxt, 2, rows)
            xc_ref[pl.ds(r0, SUBLANES), :] = acc

        _chunk_loop(t, chunk)

    return pl.pallas_call(
        body, name="conv_fwd", grid=(n_tiles,),
        in_specs=_halo_specs(t, s, c, 1) + [pl.BlockSpec((4, c), lambda i: (0, 0)), pl.BlockSpec((1, c), lambda i: (0, 0))],
        out_specs=pl.BlockSpec((t, c), lambda i: (i, 0)),
        out_shape=jax.ShapeDtypeStruct((s, c), F32),
        compiler_params=_params("arbitrary"),
    )(proj, proj, proj, conv_w, conv_b)


def _gate_matmuls(xc_ref, wa_ref, wi_ref, pr_s, pi_s, heads):
    for h in range(heads):
        cs = pl.ds(h * LRU_HEAD, LRU_HEAD)
        xb = xc_ref[:, cs].astype(BF16)
        pr_s[:, cs] = jnp.dot(xb, wa_ref[h], preferred_element_type=F32)
        pi_s[:, cs] = jnp.dot(xb, wi_ref[h], preferred_element_type=F32)


def _rg_gates(pr, pi, ba, bi, sp):
    r = _sigmoid(pr + ba)
    ig = _sigmoid(pi + bi)
    log_a = (-RG_C * r) * sp
    a = jnp.exp(log_a)
    mult = jnp.sqrt(_neg_expm1(2.0 * log_a))
    return r, ig, a, mult


def _scan_fwd_call(xc, wa, wi, ba, bi, lam, reverse, dep):
    s, c = xc.shape
    heads = c // LRU_HEAD
    t = min(SEQ_TILE, s)
    n_tiles = s // t
    tile = (lambda i: (n_tiles - 1 - i, 0)) if reverse else (lambda i: (i, 0))
    whole2 = lambda i: (0, 0)
    whole3 = lambda i: (0, 0, 0)

    def body(xc_ref, wa_ref, wi_ref, ba_ref, bi_ref, lam_ref, dep_ref, h_ref, r_ref, ig_ref, a_ref, mult_ref,
             pr_s, pi_s, carry_s):
        @pl.when(pl.program_id(0) == 0)
        def _():
            carry_s[...] = jnp.zeros_like(carry_s)

        _gate_matmuls(xc_ref, wa_ref, wi_ref, pr_s, pi_s, heads)
        ba_v, bi_v = ba_ref[...], bi_ref[...]
        sp = _softplus(-lam_ref[...])

        rows = lax.broadcasted_iota(jnp.int32, (SUBLANES, c), 0)

        def chunk(r0, h_in):
            rs = pl.ds(r0, SUBLANES)
            r, ig, a, mult = _rg_gates(pr_s[rs, :], pi_s[rs, :], ba_v, bi_v, sp)
            r_ref[rs, :] = r
            ig_ref[rs, :] = ig
            a_ref[rs, :] = a
            mult_ref[rs, :] = mult
            h = _scan_chunk(a, mult * ig * xc_ref[rs, :], h_in, rows, reverse)
            h_ref[rs, :] = h
            return h[0:1, :] if reverse else h[SUBLANES - 1:SUBLANES, :]

        carry_s[...] = _chunk_loop(t, chunk, carry_s[...], unroll=2, descending=reverse)

    return pl.pallas_call(
        body, name="scan_fwd_rev" if reverse else "scan_fwd", grid=(n_tiles,),
        in_specs=[pl.BlockSpec((t, c), tile),
                  pl.BlockSpec((heads, LRU_HEAD, LRU_HEAD), whole3), pl.BlockSpec((heads, LRU_HEAD, LRU_HEAD), whole3),
                  pl.BlockSpec((1, c), whole2), pl.BlockSpec((1, c), whole2), pl.BlockSpec((1, c), whole2), ANY],
        out_specs=[pl.BlockSpec((t, c), tile)] * 5,
        out_shape=[jax.ShapeDtypeStruct((s, c), F32)] * 5,
        scratch_shapes=[pltpu.VMEM((t, c), F32), pltpu.VMEM((t, c), F32), pltpu.VMEM((1, c), F32)],
        compiler_params=_params("arbitrary"),
    )(xc, wa, wi, ba, bi, lam, dep)


def _window_counts(r0, tile_idx, t_rows, n_rows, half, shape):
    pos = tile_idx * t_rows + r0 + lax.broadcasted_iota(jnp.int32, shape, 0)
    hi = jnp.minimum(pos + half, n_rows)
    lo = jnp.maximum(pos - half, 0)
    return (hi - lo).astype(F32)


def _window_inverse_counts(r0, tile_idx, t_rows, n_rows, half, width):
    inv = 1.0 / _window_counts(r0, tile_idx, t_rows, n_rows, half, (SUBLANES, LANES))
    return jnp.tile(inv, (1, width // LANES))


def _pool_combine_call(proj, h_f, h_b, w_pool, pool_scale, p):
    s = proj.shape[0]
    c = h_f.shape[1]
    pg = p // N_POOL_GROUPS
    t = min(SEQ_TILE, s)
    n_tiles = s // t

    def body(u_ref, up_ref, un_ref, gate_ref, hf_ref, hb_ref, wp_ref, sc_ref, y_ref, d_ref, d_s, yr_s):
        i = pl.program_id(0)
        rows = lax.broadcasted_iota(jnp.int32, (SUBLANES, pg), 0)

        def chunk(r0, _):
            rs = pl.ds(r0, SUBLANES)
            for g, w in enumerate(POOL_WINDOWS):
                cols = pl.ds(g * pg, pg)
                prv, cur, nxt = _neighbour_chunks(u_ref, up_ref, un_ref, r0, t, cols, i == 0, i == n_tiles - 1)
                tot = cur
                for o in range(-(w // 2), w // 2):
                    if o != 0:
                        tot = tot + _shift_rows(prv, cur, nxt, o, rows)
                d_s[rs, cols] = tot * _window_inverse_counts(r0, i, t, s, w // 2, pg) - cur
            gate, _ = _gelu_and_grad(gate_ref[rs, :])
            yr_s[rs, :] = (hf_ref[rs, :] + hb_ref[rs, :]) * gate

        _chunk_loop(t, chunk)
        y_ref[:, pl.ds(p, c)] = yr_s[...].astype(BF16)
        d_ref[...] = d_s[...].astype(BF16)
        for g in range(N_POOL_GROUPS):
            cols = pl.ds(g * pg, pg)
            out = jnp.dot(d_s[:, cols].astype(BF16), wp_ref[g], preferred_element_type=F32)
            y_ref[:, cols] = (out * sc_ref[:, cols]).astype(BF16)

    return pl.pallas_call(
        body, name="pool_combine", grid=(n_tiles,),
        in_specs=_halo_specs(t, s, p, 0) + [
            pl.BlockSpec((t, c), lambda i: (i, 2)),
            pl.BlockSpec((t, c), lambda i: (i, 0)), pl.BlockSpec((t, c), lambda i: (i, 0)),
            pl.BlockSpec((N_POOL_GROUPS, pg, pg), lambda i: (0, 0, 0)), pl.BlockSpec((1, p), lambda i: (0, 0))],
        out_specs=[pl.BlockSpec((t, p + c), lambda i: (i, 0)), pl.BlockSpec((t, p), lambda i: (i, 0))],
        out_shape=[jax.ShapeDtypeStruct((s, p + c), BF16), jax.ShapeDtypeStruct((s, p), BF16)],
        scratch_shapes=[pltpu.VMEM((t, p), F32), pltpu.VMEM((t, c), F32)],
        compiler_params=_params("arbitrary"),
    )(proj, proj, proj, proj, h_f, h_b, w_pool, pool_scale)


def _layer_norm_rows(z, g, b):
    mu = jnp.mean(z, axis=-1, keepdims=True)
    zc = z - mu
    var = jnp.mean(zc * zc, axis=-1, keepdims=True)
    rstd = lax.rsqrt(var + LN_EPS)
    xh = zc * rstd
    return xh, rstd, xh * g + b


def _layer_norm_bwd_rows(dx, xh, rstd, g):
    dxh = dx * g
    m1 = jnp.mean(dxh, axis=-1, keepdims=True)
    m2 = jnp.mean(dxh * xh, axis=-1, keepdims=True)
    return rstd * (dxh - m1 - xh * m2)


def _out_ln1_call(y, w_out, x, g1, b1, dep):
    s, d = x.shape
    tm = min(SEQ_TILE, s)

    def body(y_ref, w_ref, x_ref, g_ref, b_ref, dep_ref, xh_ref, x1b_ref, rstd_ref, acc_s, x1_s):
        acc_s[...] = jnp.dot(y_ref[...], w_ref[...], preferred_element_type=F32)
        g, b = g_ref[...], b_ref[...]

        def chunk(r0, _):
            rs = pl.ds(r0, SUBLANES)
            xh, rstd, x1 = _layer_norm_rows(ALPHA * x_ref[rs, :] + acc_s[rs, :], g, b)
            xh_ref[rs, :] = xh
            x1_s[rs, :] = x1
            rstd_ref[rs, :] = rstd

        _chunk_loop(tm, chunk, unroll=LN_UNROLL)
        x1b_ref[...] = x1_s[...].astype(BF16)

    return pl.pallas_call(
        body, name="out_ln1", grid=(s // tm,),
        in_specs=[pl.BlockSpec((tm, d), lambda i: (i, 0)), pl.BlockSpec((d, d), lambda i: (0, 0)),
                  pl.BlockSpec((tm, d), lambda i: (i, 0)),
                  pl.BlockSpec((1, d), lambda i: (0, 0)), pl.BlockSpec((1, d), lambda i: (0, 0)), ANY],
        out_specs=[pl.BlockSpec((tm, d), lambda i: (i, 0)), pl.BlockSpec((tm, d), lambda i: (i, 0)),
                   pl.BlockSpec((tm, 1), lambda i: (i, 0))],
        out_shape=[jax.ShapeDtypeStruct((s, d), F32), jax.ShapeDtypeStruct((s, d), BF16), jax.ShapeDtypeStruct((s, 1), F32)],
        scratch_shapes=[pltpu.VMEM((tm, d), F32), pltpu.VMEM((tm, d), F32)],
        compiler_params=_params("arbitrary"),
    )(y, w_out, x, g1, b1, dep)


def _mlp_in_call(x1b, w1, dep, done):
    s, d = x1b.shape
    n, _, f4 = w1.shape
    tm = min(MM_TILE, s)
    tn = min(1024, f4)
    per = f4 // tn
    blocks = n * per // 2
    first = 0 if done is None else blocks
    extra = [] if done is None else list(done)

    def body(x_ref, w_ref, dep_ref, *rest):
        r_ref, q_ref = rest[-2:]
        r = jnp.maximum(jnp.dot(x_ref[...], w_ref[...], preferred_element_type=F32), 0.0)
        r_ref[...] = r.astype(BF16)
        q_ref[...] = (r * r).astype(BF16)

    out_spec = pl.BlockSpec((tm, tn), lambda j, i: (i, first + j))
    return pl.pallas_call(
        body, name="mlp_in" if done is None else "mlp_in_rest", grid=(blocks, s // tm),
        in_specs=[pl.BlockSpec((tm, d), lambda j, i: (i, 0)),
                  pl.BlockSpec((None, d, tn), lambda j, i: ((first + j) // per, 0, (first + j) % per)), ANY] + [ANY] * len(extra),
        out_specs=[out_spec, out_spec],
        out_shape=[jax.ShapeDtypeStruct((s, n * f4), BF16), jax.ShapeDtypeStruct((s, n * f4), BF16)],
        input_output_aliases={3: 0, 4: 1} if extra else {},
        compiler_params=_params("arbitrary", "arbitrary"),
    )(x1b, w1, dep, *extra)


def _mlp_out_ln2_call(hsq, w2, xh1, g1, b1, g2, b2, target):
    s, f = hsq.shape
    d = w2.shape[1]
    tm = min(MM_TILE, s)
    tk = min(LN_MM_K, f)
    nk = f // tk

    def body(h_ref, w_ref, xh1_ref, g1_ref, b1_ref, g2_ref, b2_ref, t_ref,
             dz_ref, dzb_ref, loss_ref, dg_ref, db_ref, acc_s):
        i, k = pl.program_id(0), pl.program_id(1)

        @pl.when((i == 0) & (k == 0))
        def _():
            loss_ref[...] = jnp.zeros_like(loss_ref)
            dg_ref[...] = jnp.zeros_like(dg_ref)
            db_ref[...] = jnp.zeros_like(db_ref)

        @pl.when(k == 0)
        def _():
            acc_s[...] = jnp.zeros_like(acc_s)

        acc_s[...] += jnp.dot(h_ref[...], w_ref[...], preferred_element_type=F32)

        @pl.when(k == nk - 1)
        def _():
            g1, b1, g2, b2 = g1_ref[...], b1_ref[...], g2_ref[...], b2_ref[...]

            def chunk(r0, _):
                rs = pl.ds(r0, SUBLANES)
                x1 = xh1_ref[rs, :] * g1 + b1
                xh2, rstd, x2 = _layer_norm_rows(ALPHA * x1 + acc_s[rs, :], g2, b2)
                diff = x2 - t_ref[rs, :]
                loss_ref[...] += diff * diff
                dx2 = diff * (1.0 / d)
                dg_ref[...] += dx2 * xh2
                db_ref[...] += dx2
                dz = _layer_norm_bwd_rows(dx2, xh2, rstd, g2)
                dz_ref[rs, :] = dz

            _chunk_loop(tm, chunk, unroll=LN_UNROLL)
            dzb_ref[...] = dz_ref[...].astype(BF16)

    row = lambda i, k: (i, 0)
    vec = lambda i, k: (0, 0)
    return pl.pallas_call(
        body, name="mlp_out_ln2", grid=(s // tm, nk),
        in_specs=[pl.BlockSpec((tm, tk), lambda i, k: (i, k)), pl.BlockSpec((tk, d), lambda i, k: (k, 0)),
                  pl.BlockSpec((tm, d), row), pl.BlockSpec((1, d), vec), pl.BlockSpec((1, d), vec),
                  pl.BlockSpec((1, d), vec), pl.BlockSpec((1, d), vec), pl.BlockSpec((tm, d), row)],
        out_specs=[pl.BlockSpec((tm, d), row), pl.BlockSpec((tm, d), row),
                   pl.BlockSpec((SUBLANES, d), vec), pl.BlockSpec((SUBLANES, d), vec), pl.BlockSpec((SUBLANES, d), vec)],
        out_shape=[jax.ShapeDtypeStruct((s, d), F32), jax.ShapeDtypeStruct((s, d), BF16),
                   jax.ShapeDtypeStruct((SUBLANES, d), F32), jax.ShapeDtypeStruct((SUBLANES, d), F32),
                   jax.ShapeDtypeStruct((SUBLANES, d), F32)],
        scratch_shapes=[pltpu.VMEM((tm, d), F32)],
        compiler_params=_params("arbitrary", "arbitrary"),
    )(hsq, w2, xh1, g1, b1, g2, b2, target)


def _half_grad_call(a, b, half, row_sharded, init, name, dep):
    s, m = a.shape
    n = b.shape[1]
    if row_sharded:
        rows, cols = m // (2 * N_CHIPS), n
        tm = min(1024, rows)
        per = rows // tm
        tn = min(1024, cols)
        n_i, n_j = N_CHIPS * per, cols // tn
        a_block = lambda i, h: ((i // per) * 2 + h) * per + i % per
        out_block = lambda i, j: (i // per, i % per, j)
    else:
        rows, cols = m // 2, n // N_CHIPS
        tm = min(1024, rows)
        per = rows // tm
        tn = cols if cols % 1024 else 1024
        per_n = cols // tn
        n_i, n_j = per, N_CHIPS * per_n
        a_block = lambda i, h: h * per + i
        out_block = lambda i, j: (j // per_n, i, j % per_n)
    tk = min(2048, s)
    if row_sharded and tm < 1024 and s * n * 2 <= RESIDENT_OPERAND_BYTES:
        tk, tn, n_j = s, n, 1
    has_init = init is not None

    def body(half_ref, a_ref, b_ref, *rest):
        o_ref = rest[-1]

        @pl.when(pl.program_id(2) == 0)
        def _():
            o_ref[...] = rest[0][...] if has_init else jnp.zeros_like(o_ref)

        o_ref[...] += lax.dot_general(a_ref[...], b_ref[...], (((0,), (0,)), ((), ())), preferred_element_type=F32)

    out_spec = pl.BlockSpec((None, tm, tn), lambda i, j, k, h: out_block(i, j))
    in_specs = [pl.BlockSpec((tk, tm), lambda i, j, k, h: (k, a_block(i, h[0]))),
                pl.BlockSpec((tk, tn), lambda i, j, k, h: (k, j))]
    args = [a, b]
    if has_init:
        in_specs.append(out_spec)
        args.append(init)
    in_specs.append(ANY)
    args.append(dep)
    return pl.pallas_call(
        body, name=name,
        grid_spec=pltpu.PrefetchScalarGridSpec(num_scalar_prefetch=1, grid=(n_i, n_j, s // tk), in_specs=in_specs,
                                               out_specs=out_spec),
        out_shape=jax.ShapeDtypeStruct((N_CHIPS, rows, cols), F32),
        compiler_params=_params("arbitrary", "arbitrary", "arbitrary"),
    )(half, *args)


def _dhsq_call(dzb, w2, r, dep):
    s, d = dzb.shape
    f = w2.shape[0]
    tm = min(MM_TILE, s)
    tn = min(1024, f)

    def body(dz_ref, w_ref, r_ref, dep_ref, o_ref):
        dh = lax.dot_general(dz_ref[...], w_ref[...], (((1,), (1,)), ((), ())), preferred_element_type=F32)
        o_ref[...] = (dh * (2.0 * r_ref[...].astype(F32))).astype(BF16)

    return pl.pallas_call(
        body, name="mlp_dpre", grid=(f // tn, s // tm),
        in_specs=[pl.BlockSpec((tm, d), lambda j, i: (i, 0)), pl.BlockSpec((tn, d), lambda j, i: (j, 0)),
                  pl.BlockSpec((tm, tn), lambda j, i: (i, j)), ANY],
        out_specs=pl.BlockSpec((tm, tn), lambda j, i: (i, j)),
        out_shape=jax.ShapeDtypeStruct((s, f), BF16),
        compiler_params=_params("arbitrary", "arbitrary"),
    )(dzb, w2, r, dep)


def _dx1_ln1_bwd_call(dpre, w1, dz2, xh1, rstd1, g1, dep):
    s, f = dpre.shape
    n, d, f4 = w1.shape
    tm = min(MM_TILE, s)
    tk = min(LN_MM_K, f4)
    per = f4 // tk
    nk = n * per

    def body(dp_ref, w_ref, dz2_ref, xh_ref, rstd_ref, g_ref, dep_ref, dz_ref, dzb_ref, dg_ref, db_ref, acc_s):
        i, k = pl.program_id(0), pl.program_id(1)

        @pl.when((i == 0) & (k == 0))
        def _():
            dg_ref[...] = jnp.zeros_like(dg_ref)
            db_ref[...] = jnp.zeros_like(db_ref)

        @pl.when(k == 0)
        def _():
            acc_s[...] = jnp.zeros_like(acc_s)

        acc_s[...] += lax.dot_general(dp_ref[...], w_ref[...], (((1,), (1,)), ((), ())), preferred_element_type=F32)

        @pl.when(k == nk - 1)
        def _():
            g = g_ref[...]

            def chunk(r0, _):
                rs = pl.ds(r0, SUBLANES)
                dx1 = acc_s[rs, :] + ALPHA * dz2_ref[rs, :]
                xh = xh_ref[rs, :]
                dg_ref[...] += dx1 * xh
                db_ref[...] += dx1
                dz = _layer_norm_bwd_rows(dx1, xh, rstd_ref[rs, :], g)
                dz_ref[rs, :] = dz

            _chunk_loop(tm, chunk, unroll=LN_UNROLL)
            dzb_ref[...] = dz_ref[...].astype(BF16)

    row = lambda i, k: (i, 0)
    vec = lambda i, k: (0, 0)
    return pl.pallas_call(
        body, name="dx1_ln1_bwd", grid=(s // tm, nk),
        in_specs=[pl.BlockSpec((tm, tk), lambda i, k: (i, k)),
                  pl.BlockSpec((None, d, tk), lambda i, k: (k // per, 0, k % per)),
                  pl.BlockSpec((tm, d), row), pl.BlockSpec((tm, d), row), pl.BlockSpec((tm, 1), row),
                  pl.BlockSpec((1, d), vec), ANY],
        out_specs=[pl.BlockSpec((tm, d), row), pl.BlockSpec((tm, d), row),
                   pl.BlockSpec((SUBLANES, d), vec), pl.BlockSpec((SUBLANES, d), vec)],
        out_shape=[jax.ShapeDtypeStruct((s, d), F32), jax.ShapeDtypeStruct((s, d), BF16),
                   jax.ShapeDtypeStruct((SUBLANES, d), F32), jax.ShapeDtypeStruct((SUBLANES, d), F32)],
        scratch_shapes=[pltpu.VMEM((tm, d), F32)],
        compiler_params=_params("arbitrary", "arbitrary"),
    )(dpre, w1, dz2, xh1, rstd1, g1, dep)


def _dy_call(dzb, w_out, dep):
    s, d = dzb.shape
    e = w_out.shape[0]
    tm = min(MM_TILE, s)

    def body(dz_ref, w_ref, dep_ref, o_ref):
        o_ref[...] = lax.dot_general(dz_ref[...], w_ref[...], (((1,), (1,)), ((), ())), preferred_element_type=F32)

    return pl.pallas_call(
        body, name="dy", grid=(s // tm,),
        in_specs=[pl.BlockSpec((tm, d), lambda i: (i, 0)), pl.BlockSpec((e, d), lambda i: (0, 0)), ANY],
        out_specs=pl.BlockSpec((tm, e), lambda i: (i, 0)),
        out_shape=jax.ShapeDtypeStruct((s, e), F32),
        compiler_params=_params("arbitrary"),
    )(dzb, w_out, dep)


def _mixer_bwd_call(dy, d_pool, proj, h_f, h_b, w_pool, pool_scale, p):
    s = dy.shape[0]
    c = h_f.shape[1]
    pg = p // N_POOL_GROUPS
    t = min(SEQ_TILE, s)
    n_tiles = s // t

    def body(dyp_ref, dyr_ref, d_ref, gate_ref, hf_ref, hb_ref, wp_ref, sc_ref,
             e_ref, dh_ref, dgate_ref, dwp_ref, dsc_ref, dd_s):
        i = pl.program_id(0)

        @pl.when(i == 0)
        def _():
            dwp_ref[...] = jnp.zeros_like(dwp_ref)
            dsc_ref[...] = jnp.zeros_like(dsc_ref)

        for g in range(N_POOL_GROUPS):
            cols = pl.ds(g * pg, pg)
            dg = d_ref[:, cols]
            out = jnp.dot(dg, wp_ref[g], preferred_element_type=F32)
            dyp = dyp_ref[:, cols]
            prod = dyp * out
            dsc_ref[:, cols] += jnp.sum(prod.reshape(t // SUBLANES, SUBLANES, pg), axis=0)
            dout = (dyp * sc_ref[:, cols]).astype(BF16)
            dwp_ref[g] += lax.dot_general(dg, dout, (((0,), (0,)), ((), ())), preferred_element_type=F32)
            dd_s[:, cols] = lax.dot_general(dout, wp_ref[g], (((1,), (1,)), ((), ())), preferred_element_type=F32)

        def chunk(r0, _):
            rs = pl.ds(r0, SUBLANES)
            for g, w in enumerate(POOL_WINDOWS):
                cols = pl.ds(g * pg, pg)
                e_ref[rs, cols] = dd_s[rs, cols] * _window_inverse_counts(r0, i, t, s, w // 2, pg)
            gate, dgate = _gelu_and_grad(gate_ref[rs, :])
            dyr = dyr_ref[rs, :]
            dh_ref[rs, :] = dyr * gate
            dd_s[rs, :] = dyr * (hf_ref[rs, :] + hb_ref[rs, :]) * dgate

        _chunk_loop(t, chunk)
        dgate_ref[...] = dd_s[...].astype(BF16)

    tile = lambda i: (i, 0)
    return pl.pallas_call(
        body, name="mixer_bwd", grid=(n_tiles,),
        in_specs=[pl.BlockSpec((t, p), tile), pl.BlockSpec((t, c), lambda i: (i, 1)), pl.BlockSpec((t, p), tile),
                  pl.BlockSpec((t, c), lambda i: (i, 2)), pl.BlockSpec((t, c), tile), pl.BlockSpec((t, c), tile),
                  pl.BlockSpec((N_POOL_GROUPS, pg, pg), lambda i: (0, 0, 0)), pl.BlockSpec((1, p), lambda i: (0, 0))],
        out_specs=[pl.BlockSpec((t, p), tile), pl.BlockSpec((t, c), tile), pl.BlockSpec((t, c), tile),
                   pl.BlockSpec((N_POOL_GROUPS, pg, pg), lambda i: (0, 0, 0)), pl.BlockSpec((SUBLANES, p), lambda i: (0, 0))],
        out_shape=[jax.ShapeDtypeStruct((s, p), F32), jax.ShapeDtypeStruct((s, c), F32), jax.ShapeDtypeStruct((s, c), BF16),
                   jax.ShapeDtypeStruct((N_POOL_GROUPS, pg, pg), F32), jax.ShapeDtypeStruct((SUBLANES, p), F32)],
        scratch_shapes=[pltpu.VMEM((t, p), F32)],
        compiler_params=_params("arbitrary"),
    )(dy, dy, d_pool, proj, h_f, h_b, w_pool, pool_scale)


def _scan_bwd_call(xc, dh, h_dir, gates, dxc_prev, wa, wi, lam, reverse, dep):
    s, c = xc.shape
    heads = c // LRU_HEAD
    t = min(SEQ_TILE, s)
    n_tiles = s // t
    per = t // SUBLANES
    last_blk = s // SUBLANES - 1
    tile = (lambda i: (i, 0)) if reverse else (lambda i: (n_tiles - 1 - i, 0))
    if reverse:
        halo = lambda i: (jnp.minimum((i + 1) * per, last_blk), 0)
    else:
        halo = lambda i: (jnp.maximum((n_tiles - 1 - i) * per - 1, 0), 0)
    whole2 = lambda i: (0, 0)
    whole3 = lambda i: (0, 0, 0)
    has_prev = dxc_prev is not None
    n_in = 11 + int(has_prev) + 1

    def body(*refs):
        xc_ref, dh_ref, h_ref, hh_ref, r_ref, ig_ref, a_ref, mult_ref = refs[:8]
        prev_ref = refs[8] if has_prev else None
        wa_ref, wi_ref, lam_ref = refs[n_in - 4:n_in - 1]
        dxc_ref, dwa_ref, dwi_ref, dba_ref, dbi_ref, dsp_ref = refs[n_in:n_in + 6]
        pr_s, pi_s, carry_s = refs[n_in + 6:]
        step = pl.program_id(0)
        tile_idx = step if reverse else n_tiles - 1 - step

        @pl.when(step == 0)
        def _():
            carry_s[...] = jnp.zeros_like(carry_s)
            dwa_ref[...] = jnp.zeros_like(dwa_ref)
            dwi_ref[...] = jnp.zeros_like(dwi_ref)
            dba_ref[...] = jnp.zeros_like(dba_ref)
            dbi_ref[...] = jnp.zeros_like(dbi_ref)
            dsp_ref[...] = jnp.zeros_like(dsp_ref)

        sp = _softplus(-lam_ref[...])
        rows = lax.broadcasted_iota(jnp.int32, (SUBLANES, c), 0)

        def chunk(r0, u_in):
            rs = pl.ds(r0, SUBLANES)
            xcv = xc_ref[rs, :]
            r, ig, a, mult = r_ref[rs, :], ig_ref[rs, :], a_ref[rs, :], mult_ref[rs, :]
            dhv = dh_ref[rs, :]
            u = _scan_chunk(a, a * dhv, u_in, rows, not reverse)
            if reverse:
                gt = dhv + jnp.where(rows >= 1, pltpu.roll(u, 1, 0), u_in)
                u_out = u[SUBLANES - 1:SUBLANES, :]
            else:
                gt = dhv + jnp.where(rows < SUBLANES - 1, pltpu.roll(u, SUBLANES - 1, 0), u_in)
                u_out = u[0:1, :]
            cur = h_ref[rs, :]
            if reverse:
                after = h_ref[pl.ds(pl.multiple_of(jnp.minimum(r0 + SUBLANES, t - SUBLANES), SUBLANES), SUBLANES), :]
                edge = jnp.where(tile_idx == n_tiles - 1, 0.0, hh_ref[...])
                nxt = jnp.where(r0 == t - SUBLANES, edge, after)
                hs = _shift_rows(cur, cur, nxt, 1, rows)
            else:
                before = h_ref[pl.ds(pl.multiple_of(jnp.maximum(r0 - SUBLANES, 0), SUBLANES), SUBLANES), :]
                edge = jnp.where(tile_idx == 0, 0.0, hh_ref[...])
                prv = jnp.where(r0 == 0, edge, before)
                hs = _shift_rows(prv, cur, cur, -1, rows)
            gx = gt * xcv
            dmult = gx * ig
            di = gx * mult
            dlog_a = (gt * hs) * a - dmult * (a * a) / mult
            dr = dlog_a * (-RG_C * sp)
            dsp_ref[...] += dlog_a * (-RG_C * r)
            dpr = dr * r * (1.0 - r)
            dpi = di * ig * (1.0 - ig)
            dba_ref[...] += dpr
            dbi_ref[...] += dpi
            direct = gt * mult * ig
            if has_prev:
                direct = direct + prev_ref[rs, :]
            dxc_ref[rs, :] = direct
            pr_s[rs, :] = dpr
            pi_s[rs, :] = dpi
            return u_out

        carry_s[...] = _chunk_loop(t, chunk, carry_s[...], unroll=2, descending=not reverse)

        for h in range(heads):
            cs = pl.ds(h * LRU_HEAD, LRU_HEAD)
            xb = xc_ref[:, cs].astype(BF16)
            dprb = pr_s[:, cs].astype(BF16)
            dpib = pi_s[:, cs].astype(BF16)
            dwa_ref[h] += lax.dot_general(xb, dprb, (((0,), (0,)), ((), ())), preferred_element_type=F32)
            dwi_ref[h] += lax.dot_general(xb, dpib, (((0,), (0,)), ((), ())), preferred_element_type=F32)
            dxc_ref[:, cs] += (
                lax.dot_general(dprb, wa_ref[h], (((1,), (1,)), ((), ())), preferred_element_type=F32)
                + lax.dot_general(dpib, wi_ref[h], (((1,), (1,)), ((), ())), preferred_element_type=F32))

    tile_spec = pl.BlockSpec((t, c), tile)
    in_specs = [tile_spec, tile_spec, tile_spec, pl.BlockSpec((SUBLANES, c), halo)] + [tile_spec] * 4
    args = [xc, dh, h_dir, h_dir, *gates]
    if has_prev:
        in_specs.append(tile_spec)
        args.append(dxc_prev)
    in_specs += [pl.BlockSpec((heads, LRU_HEAD, LRU_HEAD), whole3), pl.BlockSpec((heads, LRU_HEAD, LRU_HEAD), whole3),
                 pl.BlockSpec((1, c), whole2), ANY]
    args += [wa, wi, lam, dep]
    assert len(args) == n_in
    return pl.pallas_call(
        body, name="scan_bwd_rev" if reverse else "scan_bwd", grid=(n_tiles,),
        in_specs=in_specs,
        out_specs=[tile_spec,
                   pl.BlockSpec((heads, LRU_HEAD, LRU_HEAD), whole3), pl.BlockSpec((heads, LRU_HEAD, LRU_HEAD), whole3),
                   pl.BlockSpec((SUBLANES, c), whole2), pl.BlockSpec((SUBLANES, c), whole2), pl.BlockSpec((SUBLANES, c), whole2)],
        out_shape=[jax.ShapeDtypeStruct((s, c), F32),
                   jax.ShapeDtypeStruct((heads, LRU_HEAD, LRU_HEAD), F32), jax.ShapeDtypeStruct((heads, LRU_HEAD, LRU_HEAD), F32),
                   jax.ShapeDtypeStruct((SUBLANES, c), F32), jax.ShapeDtypeStruct((SUBLANES, c), F32),
                   jax.ShapeDtypeStruct((SUBLANES, c), F32)],
        scratch_shapes=[pltpu.VMEM((t, c), F32), pltpu.VMEM((t, c), F32), pltpu.VMEM((1, c), F32)],
        compiler_params=_params("arbitrary"),
    )(*args)


def _dproj_call(e_pool, dxc, proj, dgate, conv_w, p):
    s, c = dxc.shape
    pg = p // N_POOL_GROUPS
    t = min(SEQ_TILE, s)
    n_tiles = s // t

    def body(e_ref, ep_ref, en_ref, dx_ref, dxp_ref, dxn_ref, u_ref, up_ref, un_ref, dgate_ref, w_ref,
             dproj_ref, dcw_ref, dcb_ref, st_s):
        i = pl.program_id(0)
        first, last = i == 0, i == n_tiles - 1

        @pl.when(first)
        def _():
            dcw_ref[...] = jnp.zeros_like(dcw_ref)
            dcb_ref[...] = jnp.zeros_like(dcb_ref)

        rows_p = lax.broadcasted_iota(jnp.int32, (SUBLANES, pg), 0)
        rows_c = lax.broadcasted_iota(jnp.int32, (SUBLANES, c), 0)
        w = w_ref[...]

        def chunk(r0, _):
            rs = pl.ds(r0, SUBLANES)
            for g, win in enumerate(POOL_WINDOWS):
                cols = pl.ds(g * pg, pg)
                prv, cur, nxt = _neighbour_chunks(e_ref, ep_ref, en_ref, r0, t, cols, first, last)
                tot = cur
                for o in range(-(win // 2) + 1, win // 2 + 1):
                    if o != 0:
                        tot = tot + _shift_rows(prv, cur, nxt, o, rows_p)
                cnt = _window_counts(r0, i, t, s, win // 2, (SUBLANES, pg))
                st_s[rs, cols] = tot - cur * cnt
            prv, cur, nxt = _neighbour_chunks(dx_ref, dxp_ref, dxn_ref, r0, t, slice(None), first, last)
            du = w[1:2] * cur
            du += w[0:1] * _shift_rows(prv, cur, nxt, 1, rows_c)
            du += w[2:3] * _shift_rows(prv, cur, nxt, -1, rows_c)
            du += w[3:4] * _shift_rows(prv, cur, nxt, -2, rows_c)
            st_s[rs, pl.ds(p, c)] = du
            uprv, ucur, unxt = _neighbour_chunks(u_ref, up_ref, un_ref, r0, t, slice(None), first, last)
            dcb_ref[...] += cur
            for j, o in enumerate((-1, 0, 1, 2)):
                dcw_ref[j] += cur * _shift_rows(uprv, ucur, unxt, o, rows_c)

        _chunk_loop(t, chunk)
        dproj_ref[:, pl.ds(0, p + c)] = st_s[...].astype(BF16)
        dproj_ref[:, pl.ds(p + c, c)] = dgate_ref[...]

    return pl.pallas_call(
        body, name="dproj", grid=(n_tiles,),
        in_specs=_halo_specs(t, s, p, 0) + _halo_specs(t, s, c, 0) + _halo_specs(t, s, c, 1) + [
            pl.BlockSpec((t, c), lambda i: (i, 0)), pl.BlockSpec((4, c), lambda i: (0, 0))],
        out_specs=[pl.BlockSpec((t, p + 2 * c), lambda i: (i, 0)),
                   pl.BlockSpec((4, SUBLANES, c), lambda i: (0, 0, 0)), pl.BlockSpec((SUBLANES, c), lambda i: (0, 0))],
        out_shape=[jax.ShapeDtypeStruct((s, p + 2 * c), BF16), jax.ShapeDtypeStruct((4, SUBLANES, c), F32),
                   jax.ShapeDtypeStruct((SUBLANES, c), F32)],
        scratch_shapes=[pltpu.VMEM((t, p + c), F32)],
        compiler_params=_params("arbitrary"),
    )(e_pool, e_pool, e_pool, dxc, dxc, dxc, proj, proj, proj, dgate, conv_w)


def _dx_call(dproj, w_in, dz1, dep):
    s, e = dproj.shape
    n, d, e4 = w_in.shape
    tm = min(MM_TILE, s)

    def body(dp_ref, w_hbm, dz_ref, dep_ref, o_ref, w_s, sems):
        @pl.when(pl.program_id(0) == 0)
        def _():
            copies = [pltpu.make_async_copy(w_hbm.at[k], w_s.at[:, pl.ds(k * e4, e4)], sems.at[k]) for k in range(n)]
            for cp in copies:
                cp.start()
            for cp in copies:
                cp.wait()

        o_ref[...] = ALPHA * dz_ref[...] + lax.dot_general(
            dp_ref[...], w_s[...], (((1,), (1,)), ((), ())), preferred_element_type=F32)

    return pl.pallas_call(
        body, name="grad_x", grid=(s // tm,),
        in_specs=[pl.BlockSpec((tm, e), lambda i: (i, 0)), ANY, pl.BlockSpec((tm, d), lambda i: (i, 0)), ANY],
        out_specs=pl.BlockSpec((tm, d), lambda i: (i, 0)),
        out_shape=jax.ShapeDtypeStruct((s, d), F32),
        scratch_shapes=[pltpu.VMEM((d, e), BF16), pltpu.SemaphoreType.DMA((n,))],
        compiler_params=_params("arbitrary"),
    )(dproj, w_in, dz1, dep)


def _row_tile(rows, cols, n_arrays):
    limit = max(SUBLANES, ELT_BLOCK_BYTES // (4 * cols * max(1, n_arrays // 4)))
    best = SUBLANES
    for cand in range(SUBLANES, min(rows, limit) + 1, SUBLANES):
        if rows % cand == 0:
            best = cand
    return best if rows % SUBLANES == 0 else rows


def _cast_to_slot_call(a, idx, dtype, name, dep):
    rows, cols = a.shape
    tr = _row_tile(rows, cols, 2)
    extra = [] if dep is None else [dep]

    def body(idx_ref, a_ref, *rest):
        rest[-1][...] = a_ref[...].astype(dtype)

    return pl.pallas_call(
        body, name=name,
        grid_spec=pltpu.PrefetchScalarGridSpec(
            num_scalar_prefetch=1, grid=(rows // tr,),
            in_specs=[pl.BlockSpec((tr, cols), lambda i, idx_ref: (i, 0))] + [ANY] * len(extra),
            out_specs=pl.BlockSpec((None, tr, cols), lambda i, idx_ref: (idx_ref[1], i, 0))),
        out_shape=jax.ShapeDtypeStruct((N_CHIPS, rows, cols), dtype),
        compiler_params=_params("arbitrary"),
    )(idx, a, *extra)


def _add_half_call(g, recv, idx, name):
    _, rows, cols = g.shape
    tr = _row_tile(rows, cols, 3)

    def body(idx_ref, g_ref, r_ref, o_ref):
        o_ref[...] = g_ref[...] + r_ref[...]

    return pl.pallas_call(
        body, name=name,
        grid_spec=pltpu.PrefetchScalarGridSpec(
            num_scalar_prefetch=1, grid=(rows // tr,),
            in_specs=[pl.BlockSpec((None, tr, cols), lambda i, idx_ref: (idx_ref[0], i, 0)),
                      pl.BlockSpec((tr, cols), lambda i, idx_ref: (i, 0))],
            out_specs=pl.BlockSpec((None, tr, cols), lambda i, idx_ref: (idx_ref[1], i, 0))),
        out_shape=jax.ShapeDtypeStruct((N_CHIPS, rows, cols), F32),
        compiler_params=_params("arbitrary"),
    )(idx, g, recv)


def _sum_chips_call(own, recv, idx, name):
    _, rows, cols = recv.shape
    tr = _row_tile(rows, cols, 5)
    out_spec = pl.BlockSpec((None, tr, cols), lambda i, idx_ref: (idx_ref[0], i, 0))
    if own is None:
        def body(idx_ref, r_ref, o_ref):
            o_ref[...] = ((r_ref[0] + r_ref[1]) + r_ref[2]) + r_ref[3]
        in_specs = [pl.BlockSpec((N_CHIPS, tr, cols), lambda i, idx_ref: (0, i, 0))]
        args = (recv,)
    else:
        def body(idx_ref, p_ref, r_ref, o_ref):
            o_ref[...] = ((p_ref[...] + r_ref[0]) + r_ref[1]) + r_ref[2]
        in_specs = [pl.BlockSpec((None, tr, cols), lambda i, idx_ref: (idx_ref[1], i, 0)),
                    pl.BlockSpec((N_CHIPS - 1, tr, cols), lambda i, idx_ref: (0, i, 0))]
        args = (own, recv)
    return pl.pallas_call(
        body, name=name,
        grid_spec=pltpu.PrefetchScalarGridSpec(num_scalar_prefetch=1, grid=(rows // tr,), in_specs=in_specs, out_specs=out_spec),
        out_shape=jax.ShapeDtypeStruct((2, rows, cols), F32),
        compiler_params=_params("arbitrary"),
    )(idx, *args)


def _adamw_math(g, w, m, v):
    mn = ADAM_B1 * m + (1.0 - ADAM_B1) * g
    vn = ADAM_B2 * v + (1.0 - ADAM_B2) * (g * g)
    m_hat = mn / (1.0 - ADAM_B1 ** ADAM_STEP)
    v_hat = vn / (1.0 - ADAM_B2 ** ADAM_STEP)
    return -ADAM_LR * (m_hat / (jnp.sqrt(v_hat) + ADAM_EPS) + ADAM_WD * w), mn, vn


def _adamw_call(g, w, m, v, name):
    rows, cols = w.shape
    tr = _row_tile(rows, cols, 4)

    def body(g_ref, w_ref, m_ref, v_ref, go_ref, d_ref, mo_ref, vo_ref):
        gv = g_ref[...]
        go_ref[...] = gv
        d_ref[...], mo_ref[...], vo_ref[...] = _adamw_math(gv, w_ref[...], m_ref[...], v_ref[...])

    spec = pl.BlockSpec((tr, cols), lambda i: (i, 0))
    shape = jax.ShapeDtypeStruct((rows, cols), F32)
    return pl.pallas_call(
        body, name=name, grid=(rows // tr,),
        in_specs=[spec] * 4, out_specs=[spec] * 4, out_shape=[shape] * 4,
        compiler_params=_params("arbitrary"),
    )(g, w, m, v)


def _adamw_small_call(gs, ws, ms, vs):
    n = len(ws)

    def body(*refs):
        ins = [refs[k * n:(k + 1) * n] for k in range(4)]
        outs = [refs[(4 + k) * n:(5 + k) * n] for k in range(3)]
        for a in range(n):
            outs[0][a][...], outs[1][a][...], outs[2][a][...] = _adamw_math(*[ins[k][a][...] for k in range(4)])

    whole = pl.BlockSpec(memory_space=pltpu.VMEM)
    res = pl.pallas_call(
        body, name="adamw_small",
        in_specs=[whole] * (4 * n), out_specs=[whole] * (3 * n),
        out_shape=[jax.ShapeDtypeStruct(w.shape, F32) for w in ws] * 3,
        compiler_params=pltpu.CompilerParams(vmem_limit_bytes=VMEM_LIMIT),
    )(*gs, *ws, *ms, *vs)
    return res[:n], res[n:2 * n], res[2 * n:]


def _mesh_place():
    x, y, c = lax.axis_index("x"), lax.axis_index("y"), lax.axis_index("c")
    chips = [(1 - x, y), (x, 1 - y), (1 - x, 1 - y)]
    return x, y, c, chips


def _remote(src, dst, send_sems, recv_sems, idx, device):
    return pltpu.make_async_remote_copy(src_ref=src, dst_ref=dst, send_sem=send_sems.at[idx], recv_sem=recv_sems.at[idx],
                                        device_id=device, device_id_type=MESH)


HBM_SPEC = pl.BlockSpec(memory_space=pltpu.HBM)
SEM_SPEC = pl.BlockSpec(memory_space=pltpu.SEMAPHORE)
ORDERED_EFFECT = pltpu.SideEffectType.DATAFLOW_SIDE_EFFECTING


def _in_hbm(a):
    return pltpu.with_memory_space_constraint(a, pltpu.HBM)


def _start_copies_call(name, bufs, groups, after=None):
    n, g = len(bufs), len(groups)
    extra = [] if after is None else [after]
    first_out = n + len(extra)

    def body(*refs):
        outs = refs[first_out:first_out + n]
        sems = refs[first_out + n:first_out + n + 2 * g]
        token = refs[first_out + n + 2 * g]
        for i, (which, copies_fn, _) in enumerate(groups):
            for mine, _ in copies_fn([outs[w] for w in which], sems[2 * i], sems[2 * i + 1]):
                mine.start()
        token[...] = jnp.zeros_like(token)

    sem_shapes = [pltpu.SemaphoreType.DMA((cnt,)) for _, _, cnt in groups for _ in range(2)]
    res = pl.pallas_call(
        body, name=name,
        in_specs=[HBM_SPEC] * n + [ANY] * len(extra),
        out_specs=[HBM_SPEC] * n + [SEM_SPEC] * (2 * g) + [pl.BlockSpec(memory_space=pltpu.VMEM)],
        out_shape=[pltpu.HBM(a.shape, a.dtype) for a in bufs] + sem_shapes + [jax.ShapeDtypeStruct((SUBLANES, LANES), F32)],
        input_output_aliases={a: a for a in range(n)},
        compiler_params=pltpu.CompilerParams(has_side_effects=ORDERED_EFFECT),
    )(*[_in_hbm(a) for a in bufs], *extra)
    sems = res[n:n + 2 * g]
    return list(res[:n]), [(sems[2 * i], sems[2 * i + 1]) for i in range(g)], res[n + 2 * g]


def _wait_copies_call(name, bufs, sems, copies_fn, after):
    n = len(bufs)

    def body(*refs):
        ins = refs[:n]
        send_sems, recv_sems = refs[n], refs[n + 1]
        for mine, arriving in copies_fn(list(ins), send_sems, recv_sems):
            arriving.wait_recv()
            mine.wait_send()

    res = pl.pallas_call(
        body, name=name,
        in_specs=[HBM_SPEC] * n + [SEM_SPEC, SEM_SPEC, ANY],
        out_specs=[HBM_SPEC] * n,
        out_shape=[pltpu.HBM(a.shape, a.dtype) for a in bufs],
        input_output_aliases={a: a for a in range(n)},
        compiler_params=pltpu.CompilerParams(has_side_effects=ORDERED_EFFECT),
    )(*bufs, sems[0], sems[1], after)
    return list(res)


def _gather_copies(bufs, send_sems, recv_sems):
    x, y, c, chips = _mesh_place()
    k = 2 * x + y
    out = []
    for a, buf in enumerate(bufs):
        for j, (px, py) in enumerate(chips):
            kj = 2 * px + py
            mine = _remote(buf.at[k, c], buf.at[k, c], send_sems, recv_sems, 3 * a + j, (px, py, c))
            arriving = _remote(buf.at[k, c], buf.at[kj, c], send_sems, recv_sems, 3 * a + j, (px, py, c))
            out.append((mine, arriving))
    return out


def _exchange_copies(n_sharded, n_replicated):
    def copies(bufs, send_sems, recv_sems):
        x, y, c, chips = _mesh_place()
        k = 2 * x + y
        sums, lands = bufs[:n_sharded], bufs[n_sharded:2 * n_sharded]
        repl = bufs[2 * n_sharded:]
        out = []
        for j, (px, py) in enumerate(chips):
            kj = 2 * px + py
            for a in range(n_sharded):
                cp = _remote(sums[a].at[kj], lands[a].at[j], send_sems, recv_sems, 3 * a + j, (px, py, c))
                out.append((cp, cp))
            for a in range(n_replicated):
                idx = 3 * (n_sharded + a) + j
                mine = _remote(repl[a].at[k], repl[a].at[k], send_sems, recv_sems, idx, (px, py, c))
                arriving = _remote(repl[a].at[k], repl[a].at[kj], send_sems, recv_sems, idx, (px, py, c))
                out.append((mine, arriving))
        return out
    return copies


def _sibling_copies(n, halves):
    def copies(bufs, send_sems, recv_sems):
        x, y, c, _ = _mesh_place()
        out = []
        for a in range(n):
            src = bufs[a].at[1 - c] if halves else bufs[a]
            cp = _remote(src, bufs[n + a], send_sems, recv_sems, a, (x, y, 1 - c))
            out.append((cp, cp))
        return out
    return copies


def _forward_copies(bufs, send_sems, recv_sems):
    x, y, c, chips = _mesh_place()
    out = []
    for a, buf in enumerate(bufs):
        for j, (px, py) in enumerate(chips):
            kj = 2 * px + py
            mine = _remote(buf.at[kj, c], buf.at[kj, c], send_sems, recv_sems, 3 * a + j, (x, y, 1 - c))
            arriving = _remote(buf.at[kj, c], buf.at[kj, 1 - c], send_sems, recv_sems, 3 * a + j, (x, y, 1 - c))
            out.append((mine, arriving))
    return out


def _join_copies(bufs, send_sems, recv_sems):
    x, y, c, _ = _mesh_place()
    out = []
    for a, buf in enumerate(bufs):
        mine = _remote(buf.at[c], buf.at[c], send_sems, recv_sems, a, (x, y, 1 - c))
        arriving = _remote(buf.at[c], buf.at[1 - c], send_sems, recv_sems, a, (x, y, 1 - c))
        out.append((mine, arriving))
    return out


def _forward_to_sibling_call(bufs, name):
    n = len(bufs)

    def body(*refs):
        ins, outs = refs[:n], refs[n:2 * n]
        send_sems, recv_sems = refs[2 * n:]
        x, y, c, chips = _mesh_place()
        sibling = (x, y, 1 - c)
        sends = []
        for a in range(n):
            for j, (px, py) in enumerate(chips):
                kj = 2 * px + py
                sends.append(_remote(ins[a].at[kj, c], outs[a].at[kj, c], send_sems, recv_sems, 3 * a + j, sibling))
        for cp in sends:
            cp.start()
        for a in range(n):
            for j, (px, py) in enumerate(chips):
                kj = 2 * px + py
                _remote(ins[a].at[kj, c], outs[a].at[kj, 1 - c], send_sems, recv_sems, 3 * a + j, sibling).wait_recv()
        for cp in sends:
            cp.wait_send()

    return pl.pallas_call(
        body, name=name,
        in_specs=[ANY] * n, out_specs=[ANY] * n,
        out_shape=[jax.ShapeDtypeStruct(a.shape, a.dtype) for a in bufs],
        input_output_aliases={a: a for a in range(n)},
        scratch_shapes=[pltpu.SemaphoreType.DMA((3 * n,)), pltpu.SemaphoreType.DMA((3 * n,))],
    )(*bufs)


def _pack(arrays, rows_multiple):
    flat = jnp.concatenate([a.reshape(-1) for a in arrays])
    per = LANES * rows_multiple
    padded = -(-flat.shape[0] // per) * per
    flat = jnp.pad(flat, (0, padded - flat.shape[0]))
    return flat.reshape(-1, LANES)


def _unpack(packed, shapes):
    flat = packed.reshape(-1)
    out, at = [], 0
    for shp in shapes:
        size = 1
        for dim in shp:
            size *= dim
        out.append(flat[at:at + size].reshape(shp))
        at += size
    return out


def _halves(a):
    return a.reshape((2, a.shape[0] // 2) + a.shape[1:])


def kernel(x, ln_mix_g, ln_mix_b, w_in, w_pool, pool_scale, conv_w, conv_b, w_rg_a, b_rg_a, w_rg_i, b_rg_i, rg_lambda, w_out, ln_ffn_g, ln_ffn_b, w_mlp_in, w_mlp_out, loss_target, m_ln_mix_g, m_ln_mix_b, m_w_in, m_w_pool, m_pool_scale, m_conv_w, m_conv_b, m_w_rg_a, m_b_rg_a, m_w_rg_i, m_b_rg_i, m_rg_lambda, m_w_out, m_ln_ffn_g, m_ln_ffn_b, m_w_mlp_in, m_w_mlp_out, v_ln_mix_g, v_ln_mix_b, v_w_in, v_w_pool, v_pool_scale, v_conv_w, v_conv_b, v_w_rg_a, v_b_rg_a, v_w_rg_i, v_b_rg_i, v_rg_lambda, v_w_out, v_ln_ffn_g, v_ln_ffn_b, v_w_mlp_in, v_w_mlp_out):
    weights = dict(ln_mix_g=ln_mix_g, ln_mix_b=ln_mix_b, w_in=w_in, w_pool=w_pool, pool_scale=pool_scale, conv_w=conv_w,
                   conv_b=conv_b, w_rg_a=w_rg_a, b_rg_a=b_rg_a, w_rg_i=w_rg_i, b_rg_i=b_rg_i, rg_lambda=rg_lambda,
                   w_out=w_out, ln_ffn_g=ln_ffn_g, ln_ffn_b=ln_ffn_b, w_mlp_in=w_mlp_in, w_mlp_out=w_mlp_out)
    m_in = dict(ln_mix_g=m_ln_mix_g, ln_mix_b=m_ln_mix_b, w_in=m_w_in, w_pool=m_w_pool, pool_scale=m_pool_scale,
                conv_w=m_conv_w, conv_b=m_conv_b, w_rg_a=m_w_rg_a, b_rg_a=m_b_rg_a, w_rg_i=m_w_rg_i, b_rg_i=m_b_rg_i,
                rg_lambda=m_rg_lambda, w_out=m_w_out, ln_ffn_g=m_ln_ffn_g, ln_ffn_b=m_ln_ffn_b, w_mlp_in=m_w_mlp_in,
                w_mlp_out=m_w_mlp_out)
    v_in = dict(ln_mix_g=v_ln_mix_g, ln_mix_b=v_ln_mix_b, w_in=v_w_in, w_pool=v_w_pool, pool_scale=v_pool_scale,
                conv_w=v_conv_w, conv_b=v_conv_b, w_rg_a=v_w_rg_a, b_rg_a=v_b_rg_a, w_rg_i=v_w_rg_i, b_rg_i=v_b_rg_i,
                rg_lambda=v_rg_lambda, w_out=v_w_out, ln_ffn_g=v_ln_ffn_g, ln_ffn_b=v_ln_ffn_b, w_mlp_in=v_w_mlp_in,
                w_mlp_out=v_w_mlp_out)
    names = list(weights)

    xs = x[0]
    tgt = loss_target[0]
    s, d = xs.shape
    p = c = d // 2
    pg = p // N_POOL_GROUPS
    core = lax.axis_index("c")
    shard = 2 * lax.axis_index("x") + lax.axis_index("y")

    idx = jnp.stack([core, shard]).astype(jnp.int32)
    small_shard = _pack([conv_w[0], b_rg_a[0], b_rg_i[0], rg_lambda[0]], 2 * SUBLANES)
    to_gather = [(w_in[0], BF16), (w_out[0], BF16), (w_mlp_in[0], BF16), (w_mlp_out[0], BF16),
                 (w_pool[0].reshape(-1, pg), BF16), (small_shard, F32)]

    def slot_view(i, dep):
        a, dt = to_gather[i]
        sl = _cast_to_slot_call(a, idx, dt, f"gather_slot_{i}", dep)
        return sl.reshape(N_CHIPS, 2, sl.shape[1] // 2, sl.shape[2])

    first, later = (0, 4, 5), (1, 2, 3)
    fly_a, sems_a, token_a = _start_copies_call(
        "gather_start_first", [slot_view(i, None) for i in first], [((0, 1, 2), _gather_copies, 3 * len(first))])
    later_views = []
    for i in later:
        later_views.append(slot_view(i, later_views[-1] if later_views else token_a))
    xb = _cast_call(xs, later_views[-1])
    got_first = _wait_copies_call("gather_wait_w_in", fly_a, sems_a[0], _gather_copies, xb)
    fly_b, sems_b, g_token = _start_copies_call(
        "gather_start_later", later_views + got_first,
        [((0,), _gather_copies, 3), ((1,), _gather_copies, 3), ((2,), _gather_copies, 3)])
    got_first = fly_b[len(later):]
    in_flight = dict(zip(later, fly_b))
    g_sems = [None] + list(sems_b)

    def arrive(which, group, after, tag):
        return _wait_copies_call(f"gather_wait_{tag}", [in_flight[w] for w in which], g_sems[group], _gather_copies, after)

    def pass_on(got, tag):
        flying, sems, token = _start_copies_call(
            f"gather_forward_start_{tag}", got, [(tuple(range(len(got))), _forward_copies, 3 * len(got))])
        return (flying, sems[0], tag), token

    def passed_on(state, after):
        flying, sems, tag = state
        return _wait_copies_call(f"gather_forward_wait_{tag}", flying, sems, _forward_copies, after)

    gathered = [None] * len(to_gather)
    gathered[0], gathered[4], gathered[5] = _forward_to_sibling_call(got_first, "gather_forward_w_in")
    w_in_f = gathered[0].reshape((N_CHIPS,) + w_in.shape[1:])
    w_pool_f = gathered[4].reshape(N_CHIPS, N_POOL_GROUPS, pg // N_CHIPS, pg).transpose(1, 0, 2, 3).reshape(N_POOL_GROUPS, pg, pg)
    c4 = c // N_CHIPS
    small_parts = [_unpack(gathered[5][k].reshape(-1, LANES), [(4, c4), (2, c4), (2, c4), (2, c4)]) for k in range(N_CHIPS)]
    conv_w_f = jnp.concatenate([sp_[0] for sp_ in small_parts], axis=1)
    b_a_f = jnp.concatenate([sp_[1] for sp_ in small_parts], axis=1)
    b_i_f = jnp.concatenate([sp_[2] for sp_ in small_parts], axis=1)
    lam_f = jnp.concatenate([sp_[3] for sp_ in small_parts], axis=1)
    wa_b = w_rg_a[0].astype(BF16)
    wi_b = w_rg_i[0].astype(BF16)

    proj = _proj_call(xb, w_in_f)
    xc = _conv_call(proj, conv_w_f, conv_b, c)
    fwd_w_out, token = pass_on(arrive((1,), 1, xc, "w_out"), "w_out")
    h_b, *gates_b = _scan_fwd_call(xc, wa_b[1], wi_b[1], b_a_f[1:2], b_i_f[1:2], lam_f[1:2], True, token)
    h_f, *gates_f = _scan_fwd_call(xc, wa_b[0], wi_b[0], b_a_f[0:1], b_i_f[0:1], lam_f[0:1], False, token)
    y, d_pool = _pool_combine_call(proj, h_f, h_b, w_pool_f, pool_scale, p)
    w_out_f = passed_on(fwd_w_out, y)[0].reshape(d, d)
    fwd_w1, token = pass_on(arrive((2,), 2, y, "w_mlp_in"), "w_mlp_in")
    xh1, x1b, rstd1 = _out_ln1_call(y, w_out_f, xs, ln_mix_g, ln_mix_b, token)
    w1_f = passed_on(fwd_w1, x1b)[0].reshape((N_CHIPS,) + w_mlp_in.shape[1:])
    first_half = _mlp_in_call(x1b, w1_f, g_token, None)
    fwd_w2, token = pass_on(arrive((3,), 3, first_half[0], "w_mlp_out"), "w_mlp_out")
    r_act, hsq = _mlp_in_call(x1b, w1_f, token, first_half)
    w2_f = passed_on(fwd_w2, hsq)[0].reshape(N_CHIPS * w_mlp_out.shape[1], d)
    dz2, dz2b, loss8, dg2, db2 = _mlp_out_ln2_call(hsq, w2_f, xh1, ln_mix_g, ln_mix_b, ln_ffn_g, ln_ffn_b, tgt)

    def start_siblings(grads, halves, tag, after=None):
        lands = [lax.empty(g.shape[1:] if halves else g.shape, g.dtype) for g in grads]
        copies = _sibling_copies(len(grads), halves)
        flying, sems, token = _start_copies_call(
            f"siblings_start_{tag}", list(grads) + lands, [(tuple(range(2 * len(grads))), copies, len(grads))], after)
        return (flying, sems[0], copies, len(grads), tag), token

    def finish_siblings(state, after):
        flying, sems, copies, n, tag = state
        got = _wait_copies_call(f"siblings_wait_{tag}", flying, sems, copies, after)
        return got[:n], got[n:]

    half_own = jnp.reshape(core, (1,)).astype(jnp.int32)
    half_sibling = 1 - half_own

    def chip_sum_of(a, b, row_sharded, tag, dep, overlapped):
        for_sibling = _half_grad_call(a, b, half_sibling, row_sharded, None, f"grad_{tag}_for_sibling", dep)
        state, token = start_siblings([for_sibling], False, tag)
        results = overlapped(token)
        _, (from_sibling,) = finish_siblings(state, results[0])
        return _half_grad_call(a, b, half_own, row_sharded, from_sibling, f"grad_{tag}", token), results

    def start_exchange(sums, n_repl, tag):
        n_sh = len(sums) - n_repl
        lands = [lax.empty((N_CHIPS - 1,) + a.shape[1:], a.dtype) for a in sums[:n_sh]]
        bufs = sums[:n_sh] + lands + sums[n_sh:]
        copies = _exchange_copies(n_sh, n_repl)
        flying, sems, token = _start_copies_call(
            f"reduce_start_{tag}", bufs, [(tuple(range(len(bufs))), copies, 3 * len(sums))])
        return (flying, sems[0], copies, n_sh, tag), token

    def finish_exchange(state, after):
        flying, sems, copies, n_sh, tag = state
        got = _wait_copies_call(f"reduce_wait_{tag}", flying, sems, copies, after)
        halves = []
        for a in range(n_sh):
            own, land = got[a], got[n_sh + a]
            cols = own.shape[-1]
            total = _sum_chips_call(own.reshape(N_CHIPS, -1, cols), land.reshape(N_CHIPS - 1, -1, cols), idx,
                                    f"reduce_sum_{tag}_{a}")
            halves.append(total.reshape((2,) + own.shape[1:]))
        for a, rp in enumerate(got[2 * n_sh:]):
            halves.append(_sum_chips_call(None, rp, idx, f"reduce_sum_{tag}_r{a}"))
        return halves

    def start_join(halves, tag):
        flying, sems, token = _start_copies_call(
            f"join_start_{tag}", halves, [(tuple(range(len(halves))), _join_copies, len(halves))])
        return (flying, sems[0], tag), token

    def finish_join(state, after):
        flying, sems, tag = state
        return _wait_copies_call(f"join_wait_{tag}", flying, sems, _join_copies, after)

    sum_w2, (dpre,) = chip_sum_of(hsq, dz2b, True, "w_mlp_out", g_token,
                                  lambda tok: (_dhsq_call(dz2b, w2_f, r_act, tok),))
    flying_w2, token = start_exchange([sum_w2], 0, "w2")
    sum_w1, (dz1, dz1b, dg1, db1) = chip_sum_of(
        x1b, dpre, False, "w_mlp_in", token,
        lambda tok: _dx1_ln1_bwd_call(dpre, w1_f, dz2, xh1, rstd1, ln_mix_g, tok))
    flying_w1, token = start_exchange([sum_w1], 0, "w1")

    def dy_and_mixer(tok):
        dy = _dy_call(dz1b, w_out_f, tok)
        return _mixer_bwd_call(dy, d_pool, proj, h_f, h_b, w_pool_f, pool_scale, p)

    sum_wout, (e_pool, dh, dgate, g_wpool, g_pscale8) = chip_sum_of(y, dz1b, True, "w_out", token, dy_and_mixer)
    flying_wout, token = start_exchange([sum_wout], 0, "w_out")
    dxc0, g_wa0, g_wi0, g_ba0, g_bi0, g_sp0 = _scan_bwd_call(
        xc, dh, h_f, gates_f, None, wa_b[0], wi_b[0], lam_f[0:1], False, token)
    dxc, g_wa1, g_wi1, g_ba1, g_bi1, g_sp1 = _scan_bwd_call(
        xc, dh, h_b, gates_b, dxc0, wa_b[1], wi_b[1], lam_f[1:2], True, token)
    dproj, g_cw8, g_cb8 = _dproj_call(e_pool, dxc, proj, dgate, conv_w_f, p)

    rowsum = lambda a8: jnp.sum(a8, axis=-2)
    g_lam = jnp.stack([rowsum(g_sp0), rowsum(g_sp1)]) * (-_sigmoid(-lam_f))
    small_grads = {
        "ln_mix_g": rowsum(dg1), "ln_mix_b": rowsum(db1), "ln_ffn_g": rowsum(dg2), "ln_ffn_b": rowsum(db2),
        "pool_scale": rowsum(g_pscale8), "conv_b": rowsum(g_cb8),
        "w_rg_a": jnp.stack([g_wa0, g_wa1]), "w_rg_i": jnp.stack([g_wi0, g_wi1]),
        "w_pool": g_wpool, "conv_w": rowsum(g_cw8),
        "b_rg_a": jnp.stack([rowsum(g_ba0), rowsum(g_ba1)]), "b_rg_i": jnp.stack([rowsum(g_bi0), rowsum(g_bi1)]),
        "rg_lambda": g_lam,
    }
    small_names = list(small_grads)
    small_shapes = [small_grads[nm].shape for nm in small_names]
    loss_share = jnp.reshape(jnp.sum(loss8) * (0.5 / d), (1,))
    g_small = _halves(_pack([small_grads[nm] for nm in small_names] + [loss_share], 2 * SUBLANES))
    sib_small, token = start_siblings([g_small], True, "small")
    flying_small = []

    def small_exchange_and_grad_x(tok):
        (mine,), (theirs,) = finish_siblings(sib_small, tok)
        small_sum = _add_half_call(mine, theirs, idx, "reduce_add_small")
        state, tok = start_exchange([small_sum], 1, "small")
        flying_small.append(state)
        return (_dx_call(dproj, w_in_f, dz1, tok),)

    sum_win, (grad_x,) = chip_sum_of(xb, dproj, False, "w_in", token, small_exchange_and_grad_x)
    flying_small = flying_small[0]
    flying_win, token = start_exchange([sum_win], 0, "w_in")

    grad_w, delta_w, new_m, new_v = {}, {}, {}, {}

    def adamw(nm, full):
        w2d = weights[nm][0]
        g2d = full.reshape(w2d.shape)
        go, dl, mn, vn = _adamw_call(g2d, w2d, m_in[nm][0], v_in[nm][0], f"adamw_{nm}")
        grad_w[nm], delta_w[nm], new_m[nm], new_v[nm] = go[None], dl[None], mn[None], vn[None]
        return vn

    join_w2, token = start_join(finish_exchange(flying_w2, token), "w2")
    join_w1, token = start_join(finish_exchange(flying_w1, token), "w1")
    join_wout, token = start_join(finish_exchange(flying_wout, token), "w_out")
    last = adamw("w_mlp_out", finish_join(join_w2, token)[0])
    last = adamw("w_mlp_in", finish_join(join_w1, last)[0])
    last = adamw("w_out", finish_join(join_wout, last)[0])

    join_small, token = start_join(finish_exchange(flying_small, last), "small")
    join_win, token = start_join(finish_exchange(flying_win, token), "w_in")
    small_joined = finish_join(join_small, token)[0]
    *small_sums, loss_sum = _unpack(small_joined.reshape(-1, LANES), small_shapes + [(1,)])
    small_full = dict(zip(small_names, small_sums))
    local = dict(small_full)
    local["w_pool"] = lax.dynamic_slice_in_dim(small_full["w_pool"], shard * (pg // N_CHIPS), pg // N_CHIPS, axis=1)
    for nm in ("conv_w", "b_rg_a", "b_rg_i", "rg_lambda"):
        local[nm] = lax.dynamic_slice_in_dim(small_full[nm], shard * c4, c4, axis=1)
    small_g = [local[nm].reshape(weights[nm].shape) for nm in small_names]
    small_d, small_m, small_v = _adamw_small_call(
        small_g, [weights[nm] for nm in small_names], [m_in[nm] for nm in small_names], [v_in[nm] for nm in small_names])
    for nm, gl, dl, mn, vn in zip(small_names, small_g, small_d, small_m, small_v):
        grad_w[nm], delta_w[nm], new_m[nm], new_v[nm] = gl, dl, mn, vn
    adamw("w_in", finish_join(join_win, small_v[0])[0])

    loss = loss_sum[0]
    return (loss, grad_x[None], *[grad_w[nm] for nm in names], *[delta_w[nm] for nm in names],
            *[new_m[nm] for nm in names], *[new_v[nm] for nm in names])
```

```python
import jax
import jax.numpy as jnp
from jax import lax
from jax.experimental import pallas as pl
from jax.experimental.pallas import tpu as pltpu

F32 = jnp.float32
BF16 = jnp.bfloat16

N_CHIPS = 4
LANES = 128
SUBLANES = 8
LRU_HEAD = 128
N_POOL_GROUPS = 4
POOL_WINDOWS = (2, 4, 8, 16)
RG_C = 8.0
LN_EPS = 1e-5
ALPHA = 2.0 ** 0.25
ADAM_LR, ADAM_B1, ADAM_B2, ADAM_EPS, ADAM_WD, ADAM_STEP = 0.001, 0.9, 0.999, 1e-08, 0.01, 10
VMEM_LIMIT = 56 * 1024 * 1024
SEQ_TILE = 256
MM_TILE = 512
LN_MM_K = 2048
LN_UNROLL = 8
ELT_BLOCK_BYTES = 2 * 1024 * 1024
RESIDENT_OPERAND_BYTES = 16 * 1024 * 1024
MESH = pl.DeviceIdType.MESH
ANY = pl.BlockSpec(memory_space=pl.ANY)


def _params(*sem):
    return pltpu.CompilerParams(dimension_semantics=sem, vmem_limit_bytes=VMEM_LIMIT)


def _sigmoid(z):
    return 1.0 / (1.0 + jnp.exp(-z))


def _neg_expm1(z):
    series = -(z * (1.0 + z * (0.5 + z * (1.0 / 6.0 + z * (1.0 / 24.0)))))
    return jnp.where(z > -0.01, series, 1.0 - jnp.exp(z))


def _softplus(z):
    return jnp.maximum(z, 0.0) + jnp.log1p(jnp.exp(-jnp.abs(z)))


_GELU_K = 0.7978845608028654
_GELU_C = 0.044715


def _gelu_and_grad(u):
    t = jnp.tanh(_GELU_K * (u + _GELU_C * (u * u * u)))
    g = 0.5 * u * (1.0 + t)
    dg = 0.5 * (1.0 + t) + 0.5 * u * (1.0 - t * t) * (_GELU_K * (1.0 + 3.0 * _GELU_C * u * u))
    return g, dg


def _shift_rows(prv, cur, nxt, o, rows):
    if o == 0:
        return cur
    if o == SUBLANES:
        return nxt
    if o == -SUBLANES:
        return prv
    if o > 0:
        return pltpu.roll(jnp.where(rows >= o, cur, nxt), SUBLANES - o, 0)
    p = -o
    return pltpu.roll(jnp.where(rows < SUBLANES - p, cur, prv), p, 0)


def _neighbour_chunks(main_ref, prev_ref, next_ref, r0, t_rows, cols, first_tile, last_tile):
    cur = main_ref[pl.ds(r0, SUBLANES), cols]
    before = main_ref[pl.ds(pl.multiple_of(jnp.maximum(r0 - SUBLANES, 0), SUBLANES), SUBLANES), cols]
    after = main_ref[pl.ds(pl.multiple_of(jnp.minimum(r0 + SUBLANES, t_rows - SUBLANES), SUBLANES), SUBLANES), cols]
    halo_prev = jnp.where(first_tile, 0.0, prev_ref[:, cols])
    halo_next = jnp.where(last_tile, 0.0, next_ref[:, cols])
    prv = jnp.where(r0 == 0, halo_prev, before)
    nxt = jnp.where(r0 == t_rows - SUBLANES, halo_next, after)
    return prv, cur, nxt


def _halo_specs(t_rows, n_rows, width, col_block):
    per = t_rows // SUBLANES
    last = n_rows // SUBLANES - 1
    return [
        pl.BlockSpec((t_rows, width), lambda i: (i, col_block)),
        pl.BlockSpec((SUBLANES, width), lambda i: (jnp.maximum(i * per - 1, 0), col_block)),
        pl.BlockSpec((SUBLANES, width), lambda i: (jnp.minimum((i + 1) * per, last), col_block)),
    ]


def _chunk_loop(t_rows, fn, init=None, unroll=1, descending=False):
    span = SUBLANES * unroll

    def step(ci, carry):
        base = pl.multiple_of(((t_rows // span - 1 - ci) if descending else ci) * span, span)
        for u in range(unroll):
            carry = fn(base + ((unroll - 1 - u) if descending else u) * SUBLANES, carry)
        return carry
    return lax.fori_loop(0, t_rows // span, step, init)


def _scan_chunk(a, b, h_in, rows, reverse):
    for dist in (1, 2, 4):
        if reverse:
            keep = rows < SUBLANES - dist
            shift = SUBLANES - dist
        else:
            keep = rows >= dist
            shift = dist
        b = a * jnp.where(keep, pltpu.roll(b, shift, 0), 0.0) + b
        a = a * jnp.where(keep, pltpu.roll(a, shift, 0), 1.0)
    return a * h_in + b


def _cast_call(x, dep):
    s, d = x.shape
    tm = min(MM_TILE, s)

    def body(x_ref, dep_ref, o_ref):
        o_ref[...] = x_ref[...].astype(BF16)

    return pl.pallas_call(
        body, name="cast_x", grid=(s // tm,),
        in_specs=[pl.BlockSpec((tm, d), lambda i: (i, 0)), ANY],
        out_specs=pl.BlockSpec((tm, d), lambda i: (i, 0)),
        out_shape=jax.ShapeDtypeStruct((s, d), BF16),
        compiler_params=_params("arbitrary"),
    )(x, dep)


def _proj_call(xb, w_in):
    s, d = xb.shape
    n, _, e4 = w_in.shape
    tm = min(MM_TILE, s)

    def body(x_ref, w_hbm, proj_ref, w_s, sems):
        @pl.when(pl.program_id(0) == 0)
        def _():
            copies = [pltpu.make_async_copy(w_hbm.at[k], w_s.at[:, pl.ds(k * e4, e4)], sems.at[k]) for k in range(n)]
            for cp in copies:
                cp.start()
            for cp in copies:
                cp.wait()

        proj_ref[...] = jnp.dot(x_ref[...], w_s[...], preferred_element_type=F32)

    return pl.pallas_call(
        body, name="proj", grid=(s // tm,),
        in_specs=[pl.BlockSpec((tm, d), lambda i: (i, 0)), ANY],
        out_specs=pl.BlockSpec((tm, n * e4), lambda i: (i, 0)),
        out_shape=jax.ShapeDtypeStruct((s, n * e4), F32),
        scratch_shapes=[pltpu.VMEM((d, n * e4), BF16), pltpu.SemaphoreType.DMA((n,))],
        compiler_params=_params("arbitrary"),
    )(xb, w_in)


def _conv_call(proj, conv_w, conv_b, c):
    s = proj.shape[0]
    t = min(SEQ_TILE, s)
    n_tiles = s // t

    def body(u_ref, up_ref, un_ref, w_ref, b_ref, xc_ref):
        i = pl.program_id(0)
        rows = lax.broadcasted_iota(jnp.int32, (SUBLANES, c), 0)
        w = w_ref[...]
        b = b_ref[...]

        def chunk(r0, _):
            prv, cur, nxt = _neighbour_chunks(u_ref, up_ref, un_ref, r0, t, slice(None), i == 0, i == n_tiles - 1)
            acc = b + w[1:2] * cur
            acc += w[0:1] * _shift_rows(prv, cur, nxt, -1, rows)
            acc += w[2:3] * _shift_rows(prv, cur, nxt, 1, rows)
            acc += w[3:4] * _shift_rows(prv, cur, nxt, 2, rows)
            xc_ref[pl.ds(r0, SUBLANES), :] = acc

        _chunk_loop(t, chunk)

    return pl.pallas_call(
        body, name="conv_fwd", grid=(n_tiles,),
        in_specs=_halo_specs(t, s, c, 1) + [pl.BlockSpec((4, c), lambda i: (0, 0)), pl.BlockSpec((1, c), lambda i: (0, 0))],
        out_specs=pl.BlockSpec((t, c), lambda i: (i, 0)),
        out_shape=jax.ShapeDtypeStruct((s, c), F32),
        compiler_params=_params("arbitrary"),
    )(proj, proj, proj, conv_w, conv_b)


def _gate_matmuls(xc_ref, wa_ref, wi_ref, pr_s, pi_s, heads):
    for h in range(heads):
        cs = pl.ds(h * LRU_HEAD, LRU_HEAD)
        xb = xc_ref[:, cs].astype(BF16)
        pr_s[:, cs] = jnp.dot(xb, wa_ref[h], preferred_element_type=F32)
        pi_s[:, cs] = jnp.dot(xb, wi_ref[h], preferred_element_type=F32)


def _rg_gates(pr, pi, ba, bi, sp):
    r = _sigmoid(pr + ba)
    ig = _sigmoid(pi + bi)
    log_a = (-RG_C * r) * sp
    a = jnp.exp(log_a)
    mult = jnp.sqrt(_neg_expm1(2.0 * log_a))
    return r, ig, a, mult


def _scan_fwd_call(xc, wa, wi, ba, bi, lam, reverse, dep):
    s, c = xc.shape
    heads = c // LRU_HEAD
    t = min(SEQ_TILE, s)
    n_tiles = s // t
    tile = (lambda i: (n_tiles - 1 - i, 0)) if reverse else (lambda i: (i, 0))
    whole2 = lambda i: (0, 0)
    whole3 = lambda i: (0, 0, 0)

    def body(xc_ref, wa_ref, wi_ref, ba_ref, bi_ref, lam_ref, dep_ref, h_ref, r_ref, ig_ref, a_ref, mult_ref,
             pr_s, pi_s, carry_s):
        @pl.when(pl.program_id(0) == 0)
        def _():
            carry_s[...] = jnp.zeros_like(carry_s)

        _gate_matmuls(xc_ref, wa_ref, wi_ref, pr_s, pi_s, heads)
        ba_v, bi_v = ba_ref[...], bi_ref[...]
        sp = _softplus(-lam_ref[...])

        rows = lax.broadcasted_iota(jnp.int32, (SUBLANES, c), 0)

        def chunk(r0, h_in):
            rs = pl.ds(r0, SUBLANES)
            r, ig, a, mult = _rg_gates(pr_s[rs, :], pi_s[rs, :], ba_v, bi_v, sp)
            r_ref[rs, :] = r
            ig_ref[rs, :] = ig
            a_ref[rs, :] = a
            mult_ref[rs, :] = mult
            h = _scan_chunk(a, mult * ig * xc_ref[rs, :], h_in, rows, reverse)
            h_ref[rs, :] = h
            return h[0:1, :] if reverse else h[SUBLANES - 1:SUBLANES, :]

        carry_s[...] = _chunk_loop(t, chunk, carry_s[...], unroll=2, descending=reverse)

    return pl.pallas_call(
        body, name="scan_fwd_rev" if reverse else "scan_fwd", grid=(n_tiles,),
        in_specs=[pl.BlockSpec((t, c), tile),
                  pl.BlockSpec((heads, LRU_HEAD, LRU_HEAD), whole3), pl.BlockSpec((heads, LRU_HEAD, LRU_HEAD), whole3),
                  pl.BlockSpec((1, c), whole2), pl.BlockSpec((1, c), whole2), pl.BlockSpec((1, c), whole2), ANY],
        out_specs=[pl.BlockSpec((t, c), tile)] * 5,
        out_shape=[jax.ShapeDtypeStruct((s, c), F32)] * 5,
        scratch_shapes=[pltpu.VMEM((t, c), F32), pltpu.VMEM((t, c), F32), pltpu.VMEM((1, c), F32)],
        compiler_params=_params("arbitrary"),
    )(xc, wa, wi, ba, bi, lam, dep)


def _window_counts(r0, tile_idx, t_rows, n_rows, half, shape):
    pos = tile_idx * t_rows + r0 + lax.broadcasted_iota(jnp.int32, shape, 0)
    hi = jnp.minimum(pos + half, n_rows)
    lo = jnp.maximum(pos - half, 0)
    return (hi - lo).astype(F32)


def _window_inverse_counts(r0, tile_idx, t_rows, n_rows, half, width):
    inv = 1.0 / _window_counts(r0, tile_idx, t_rows, n_rows, half, (SUBLANES, LANES))
    return jnp.tile(inv, (1, width // LANES))


def _pool_combine_call(proj, h_f, h_b, w_pool, pool_scale, p):
    s = proj.shape[0]
    c = h_f.shape[1]
    pg = p // N_POOL_GROUPS
    t = min(SEQ_TILE, s)
    n_tiles = s // t

    def body(u_ref, up_ref, un_ref, gate_ref, hf_ref, hb_ref, wp_ref, sc_ref, y_ref, d_ref, d_s, yr_s):
        i = pl.program_id(0)
        rows = lax.broadcasted_iota(jnp.int32, (SUBLANES, pg), 0)

        def chunk(r0, _):
            rs = pl.ds(r0, SUBLANES)
            for g, w in enumerate(POOL_WINDOWS):
                cols = pl.ds(g * pg, pg)
                prv, cur, nxt = _neighbour_chunks(u_ref, up_ref, un_ref, r0, t, cols, i == 0, i == n_tiles - 1)
                tot = cur
                for o in range(-(w // 2), w // 2):
                    if o != 0:
                        tot = tot + _shift_rows(prv, cur, nxt, o, rows)
                d_s[rs, cols] = tot * _window_inverse_counts(r0, i, t, s, w // 2, pg) - cur
            gate, _ = _gelu_and_grad(gate_ref[rs, :])
            yr_s[rs, :] = (hf_ref[rs, :] + hb_ref[rs, :]) * gate

        _chunk_loop(t, chunk)
        y_ref[:, pl.ds(p, c)] = yr_s[...].astype(BF16)
        d_ref[...] = d_s[...].astype(BF16)
        for g in range(N_POOL_GROUPS):
            cols = pl.ds(g * pg, pg)
            out = jnp.dot(d_s[:, cols].astype(BF16), wp_ref[g], preferred_element_type=F32)
            y_ref[:, cols] = (out * sc_ref[:, cols]).astype(BF16)

    return pl.pallas_call(
        body, name="pool_combine", grid=(n_tiles,),
        in_specs=_halo_specs(t, s, p, 0) + [
            pl.BlockSpec((t, c), lambda i: (i, 2)),
            pl.BlockSpec((t, c), lambda i: (i, 0)), pl.BlockSpec((t, c), lambda i: (i, 0)),
            pl.BlockSpec((N_POOL_GROUPS, pg, pg), lambda i: (0, 0, 0)), pl.BlockSpec((1, p), lambda i: (0, 0))],
        out_specs=[pl.BlockSpec((t, p + c), lambda i: (i, 0)), pl.BlockSpec((t, p), lambda i: (i, 0))],
        out_shape=[jax.ShapeDtypeStruct((s, p + c), BF16), jax.ShapeDtypeStruct((s, p), BF16)],
        scratch_shapes=[pltpu.VMEM((t, p), F32), pltpu.VMEM((t, c), F32)],
        compiler_params=_params("arbitrary"),
    )(proj, proj, proj, proj, h_f, h_b, w_pool, pool_scale)


def _layer_norm_rows(z, g, b):
    mu = jnp.mean(z, axis=-1, keepdims=True)
    zc = z - mu
    var = jnp.mean(zc * zc, axis=-1, keepdims=True)
    rstd = lax.rsqrt(var + LN_EPS)
    xh = zc * rstd
    return xh, rstd, xh * g + b


def _layer_norm_bwd_rows(dx, xh, rstd, g):
    dxh = dx * g
    m1 = jnp.mean(dxh, axis=-1, keepdims=True)
    m2 = jnp.mean(dxh * xh, axis=-1, keepdims=True)
    return rstd * (dxh - m1 - xh * m2)


def _out_ln1_call(y, w_out, x, g1, b1, dep):
    s, d = x.shape
    tm = min(SEQ_TILE, s)

    def body(y_ref, w_ref, x_ref, g_ref, b_ref, dep_ref, xh_ref, x1b_ref, rstd_ref, acc_s, x1_s):
        acc_s[...] = jnp.dot(y_ref[...], w_ref[...], preferred_element_type=F32)
        g, b = g_ref[...], b_ref[...]

        def chunk(r0, _):
            rs = pl.ds(r0, SUBLANES)
            xh, rstd, x1 = _layer_norm_rows(ALPHA * x_ref[rs, :] + acc_s[rs, :], g, b)
            xh_ref[rs, :] = xh
            x1_s[rs, :] = x1
            rstd_ref[rs, :] = rstd

        _chunk_loop(tm, chunk, unroll=LN_UNROLL)
        x1b_ref[...] = x1_s[...].astype(BF16)

    return pl.pallas_call(
        body, name="out_ln1", grid=(s // tm,),
        in_specs=[pl.BlockSpec((tm, d), lambda i: (i, 0)), pl.BlockSpec((d, d), lambda i: (0, 0)),
                  pl.BlockSpec((tm, d), lambda i: (i, 0)),
                  pl.BlockSpec((1, d), lambda i: (0, 0)), pl.BlockSpec((1, d), lambda i: (0, 0)), ANY],
        out_specs=[pl.BlockSpec((tm, d), lambda i: (i, 0)), pl.BlockSpec((tm, d), lambda i: (i, 0)),
                   pl.BlockSpec((tm, 1), lambda i: (i, 0))],
        out_shape=[jax.ShapeDtypeStruct((s, d), F32), jax.ShapeDtypeStruct((s, d), BF16), jax.ShapeDtypeStruct((s, 1), F32)],
        scratch_shapes=[pltpu.VMEM((tm, d), F32), pltpu.VMEM((tm, d), F32)],
        compiler_params=_params("arbitrary"),
    )(y, w_out, x, g1, b1, dep)


def _mlp_in_call(x1b, w1, dep, done):
    s, d = x1b.shape
    n, _, f4 = w1.shape
    tm = min(MM_TILE, s)
    tn = min(1024, f4)
    per = f4 // tn
    blocks = n * per // 2
    first = 0 if done is None else blocks
    extra = [] if done is None else list(done)

    def body(x_ref, w_ref, dep_ref, *rest):
        r_ref, q_ref = rest[-2:]
        r = jnp.maximum(jnp.dot(x_ref[...], w_ref[...], preferred_element_type=F32), 0.0)
        r_ref[...] = r.astype(BF16)
        q_ref[...] = (r * r).astype(BF16)

    out_spec = pl.BlockSpec((tm, tn), lambda j, i: (i, first + j))
    return pl.pallas_call(
        body, name="mlp_in" if done is None else "mlp_in_rest", grid=(blocks, s // tm),
        in_specs=[pl.BlockSpec((tm, d), lambda j, i: (i, 0)),
                  pl.BlockSpec((None, d, tn), lambda j, i: ((first + j) // per, 0, (first + j) % per)), ANY] + [ANY] * len(extra),
        out_specs=[out_spec, out_spec],
        out_shape=[jax.ShapeDtypeStruct((s, n * f4), BF16), jax.ShapeDtypeStruct((s, n * f4), BF16)],
        input_output_aliases={3: 0, 4: 1} if extra else {},
        compiler_params=_params("arbitrary", "arbitrary"),
    )(x1b, w1, dep, *extra)


def _mlp_out_ln2_call(hsq, w2, xh1, g1, b1, g2, b2, target):
    s, f = hsq.shape
    d = w2.shape[1]
    tm = min(MM_TILE, s)
    tk = min(LN_MM_K, f)
    nk = f // tk

    def body(h_ref, w_ref, xh1_ref, g1_ref, b1_ref, g2_ref, b2_ref, t_ref,
             dz_ref, dzb_ref, loss_ref, dg_ref, db_ref, acc_s):
        i, k = pl.program_id(0), pl.program_id(1)

        @pl.when((i == 0) & (k == 0))
        def _():
            loss_ref[...] = jnp.zeros_like(loss_ref)
            dg_ref[...] = jnp.zeros_like(dg_ref)
            db_ref[...] = jnp.zeros_like(db_ref)

        @pl.when(k == 0)
        def _():
            acc_s[...] = jnp.zeros_like(acc_s)

        acc_s[...] += jnp.dot(h_ref[...], w_ref[...], preferred_element_type=F32)

        @pl.when(k == nk - 1)
        def _():
            g1, b1, g2, b2 = g1_ref[...], b1_ref[...], g2_ref[...], b2_ref[...]

            def chunk(r0, _):
                rs = pl.ds(r0, SUBLANES)
                x1 = xh1_ref[rs, :] * g1 + b1
                xh2, rstd, x2 = _layer_norm_rows(ALPHA * x1 + acc_s[rs, :], g2, b2)
                diff = x2 - t_ref[rs, :]
                loss_ref[...] += diff * diff
                dx2 = diff * (1.0 / d)
                dg_ref[...] += dx2 * xh2
                db_ref[...] += dx2
                dz = _layer_norm_bwd_rows(dx2, xh2, rstd, g2)
                dz_ref[rs, :] = dz

            _chunk_loop(tm, chunk, unroll=LN_UNROLL)
            dzb_ref[...] = dz_ref[...].astype(BF16)

    row = lambda i, k: (i, 0)
    vec = lambda i, k: (0, 0)
    return pl.pallas_call(
        body, name="mlp_out_ln2", grid=(s // tm, nk),
        in_specs=[pl.BlockSpec((tm, tk), lambda i, k: (i, k)), pl.BlockSpec((tk, d), lambda i, k: (k, 0)),
                  pl.BlockSpec((tm, d), row), pl.BlockSpec((1, d), vec), pl.BlockSpec((1, d), vec),
                  pl.BlockSpec((1, d), vec), pl.BlockSpec((1, d), vec), pl.BlockSpec((tm, d), row)],
        out_specs=[pl.BlockSpec((tm, d), row), pl.BlockSpec((tm, d), row),
                   pl.BlockSpec((SUBLANES, d), vec), pl.BlockSpec((SUBLANES, d), vec), pl.BlockSpec((SUBLANES, d), vec)],
        out_shape=[jax.ShapeDtypeStruct((s, d), F32), jax.ShapeDtypeStruct((s, d), BF16),
                   jax.ShapeDtypeStruct((SUBLANES, d), F32), jax.ShapeDtypeStruct((SUBLANES, d), F32),
                   jax.ShapeDtypeStruct((SUBLANES, d), F32)],
        scratch_shapes=[pltpu.VMEM((tm, d), F32)],
        compiler_params=_params("arbitrary", "arbitrary"),
    )(hsq, w2, xh1, g1, b1, g2, b2, target)


def _half_grad_call(a, b, half, row_sharded, init, name, dep):
    s, m = a.shape
    n = b.shape[1]
    if row_sharded:
        rows, cols = m // (2 * N_CHIPS), n
        tm = min(1024, rows)
        per = rows // tm
        tn = min(1024, cols)
        n_i, n_j = N_CHIPS * per, cols // tn
        a_block = lambda i, h: ((i // per) * 2 + h) * per + i % per
        out_block = lambda i, j: (i // per, i % per, j)
    else:
        rows, cols = m // 2, n // N_CHIPS
        tm = min(1024, rows)
        per = rows // tm
        tn = cols if cols % 1024 else 1024
        per_n = cols // tn
        n_i, n_j = per, N_CHIPS * per_n
        a_block = lambda i, h: h * per + i
        out_block = lambda i, j: (j // per_n, i, j % per_n)
    tk = min(2048, s)
    if row_sharded and tm < 1024 and s * n * 2 <= RESIDENT_OPERAND_BYTES:
        tk, tn, n_j = s, n, 1
    has_init = init is not None

    def body(half_ref, a_ref, b_ref, *rest):
        o_ref = rest[-1]

        @pl.when(pl.program_id(2) == 0)
        def _():
            o_ref[...] = rest[0][...] if has_init else jnp.zeros_like(o_ref)

        o_ref[...] += lax.dot_general(a_ref[...], b_ref[...], (((0,), (0,)), ((), ())), preferred_element_type=F32)

    out_spec = pl.BlockSpec((None, tm, tn), lambda i, j, k, h: out_block(i, j))
    in_specs = [pl.BlockSpec((tk, tm), lambda i, j, k, h: (k, a_block(i, h[0]))),
                pl.BlockSpec((tk, tn), lambda i, j, k, h: (k, j))]
    args = [a, b]
    if has_init:
        in_specs.append(out_spec)
        args.append(init)
    in_specs.append(ANY)
    args.append(dep)
    return pl.pallas_call(
        body, name=name,
        grid_spec=pltpu.PrefetchScalarGridSpec(num_scalar_prefetch=1, grid=(n_i, n_j, s // tk), in_specs=in_specs,
                                               out_specs=out_spec),
        out_shape=jax.ShapeDtypeStruct((N_CHIPS, rows, cols), F32),
        compiler_params=_params("arbitrary", "arbitrary", "arbitrary"),
    )(half, *args)


def _dhsq_call(dzb, w2, r, dep):
    s, d = dzb.shape
    f = w2.shape[0]
    tm = min(MM_TILE, s)
    tn = min(1024, f)

    def body(dz_ref, w_ref, r_ref, dep_ref, o_ref):
        dh = lax.dot_general(dz_ref[...], w_ref[...], (((1,), (1,)), ((), ())), preferred_element_type=F32)
        o_ref[...] = (dh * (2.0 * r_ref[...].astype(F32))).astype(BF16)

    return pl.pallas_call(
        body, name="mlp_dpre", grid=(f // tn, s // tm),
        in_specs=[pl.BlockSpec((tm, d), lambda j, i: (i, 0)), pl.BlockSpec((tn, d), lambda j, i: (j, 0)),
                  pl.BlockSpec((tm, tn), lambda j, i: (i, j)), ANY],
        out_specs=pl.BlockSpec((tm, tn), lambda j, i: (i, j)),
        out_shape=jax.ShapeDtypeStruct((s, f), BF16),
        compiler_params=_params("arbitrary", "arbitrary"),
    )(dzb, w2, r, dep)


def _dx1_ln1_bwd_call(dpre, w1, dz2, xh1, rstd1, g1, dep):
    s, f = dpre.shape
    n, d, f4 = w1.shape
    tm = min(MM_TILE, s)
    tk = min(LN_MM_K, f4)
    per = f4 // tk
    nk = n * per

    def body(dp_ref, w_ref, dz2_ref, xh_ref, rstd_ref, g_ref, dep_ref, dz_ref, dzb_ref, dg_ref, db_ref, acc_s):
        i, k = pl.program_id(0), pl.program_id(1)

        @pl.when((i == 0) & (k == 0))
        def _():
            dg_ref[...] = jnp.zeros_like(dg_ref)
            db_ref[...] = jnp.zeros_like(db_ref)

        @pl.when(k == 0)
        def _():
            acc_s[...] = jnp.zeros_like(acc_s)

        acc_s[...] += lax.dot_general(dp_ref[...], w_ref[...], (((1,), (1,)), ((), ())), preferred_element_type=F32)

        @pl.when(k == nk - 1)
        def _():
            g = g_ref[...]

            def chunk(r0, _):
                rs = pl.ds(r0, SUBLANES)
                dx1 = acc_s[rs, :] + ALPHA * dz2_ref[rs, :]
                xh = xh_ref[rs, :]
                dg_ref[...] += dx1 * xh
                db_ref[...] += dx1
                dz = _layer_norm_bwd_rows(dx1, xh, rstd_ref[rs, :], g)
                dz_ref[rs, :] = dz

            _chunk_loop(tm, chunk, unroll=LN_UNROLL)
            dzb_ref[...] = dz_ref[...].astype(BF16)

    row = lambda i, k: (i, 0)
    vec = lambda i, k: (0, 0)
    return pl.pallas_call(
        body, name="dx1_ln1_bwd", grid=(s // tm, nk),
        in_specs=[pl.BlockSpec((tm, tk), lambda i, k: (i, k)),
                  pl.BlockSpec((None, d, tk), lambda i, k: (k // per, 0, k % per)),
                  pl.BlockSpec((tm, d), row), pl.BlockSpec((tm, d), row), pl.BlockSpec((tm, 1), row),
                  pl.BlockSpec((1, d), vec), ANY],
        out_specs=[pl.BlockSpec((tm, d), row), pl.BlockSpec((tm, d), row),
                   pl.BlockSpec((SUBLANES, d), vec), pl.BlockSpec((SUBLANES, d), vec)],
        out_shape=[jax.ShapeDtypeStruct((s, d), F32), jax.ShapeDtypeStruct((s, d), BF16),
                   jax.ShapeDtypeStruct((SUBLANES, d), F32), jax.ShapeDtypeStruct((SUBLANES, d), F32)],
        scratch_shapes=[pltpu.VMEM((tm, d), F32)],
        compiler_params=_params("arbitrary", "arbitrary"),
    )(dpre, w1, dz2, xh1, rstd1, g1, dep)


def _mixer_bwd_call(dzb, w_out, d_pool, proj, h_f, h_b, w_pool, pool_scale, p, dep):
    s, d = dzb.shape
    c = h_f.shape[1]
    pg = p // N_POOL_GROUPS
    t = min(SEQ_TILE, s)
    n_tiles = s // t

    def body(dz_ref, wo_ref, d_ref, gate_ref, hf_ref, hb_ref, wp_ref, sc_ref, dep_ref,
             e_ref, dh_ref, dgate_ref, dwp_ref, dsc_ref, dd_s, dy_s):
        i = pl.program_id(0)

        @pl.when(i == 0)
        def _():
            dwp_ref[...] = jnp.zeros_like(dwp_ref)
            dsc_ref[...] = jnp.zeros_like(dsc_ref)

        dy_s[...] = lax.dot_general(dz_ref[...], wo_ref[...], (((1,), (1,)), ((), ())), preferred_element_type=F32)

        for g in range(N_POOL_GROUPS):
            cols = pl.ds(g * pg, pg)
            dg = d_ref[:, cols]
            out = jnp.dot(dg, wp_ref[g], preferred_element_type=F32)
            dyp = dy_s[:, cols]
            prod = dyp * out
            dsc_ref[:, cols] += jnp.sum(prod.reshape(t // SUBLANES, SUBLANES, pg), axis=0)
            dout = (dyp * sc_ref[:, cols]).astype(BF16)
            dwp_ref[g] += lax.dot_general(dg, dout, (((0,), (0,)), ((), ())), preferred_element_type=F32)
            dd_s[:, cols] = lax.dot_general(dout, wp_ref[g], (((1,), (1,)), ((), ())), preferred_element_type=F32)

        def chunk(r0, _):
            rs = pl.ds(r0, SUBLANES)
            for g, w in enumerate(POOL_WINDOWS):
                cols = pl.ds(g * pg, pg)
                e_ref[rs, cols] = dd_s[rs, cols] * _window_inverse_counts(r0, i, t, s, w // 2, pg)
            gate, dgate = _gelu_and_grad(gate_ref[rs, :])
            dyr = dy_s[rs, pl.ds(p, c)]
            dh_ref[rs, :] = dyr * gate
            dd_s[rs, :] = dyr * (hf_ref[rs, :] + hb_ref[rs, :]) * dgate

        _chunk_loop(t, chunk)
        dgate_ref[...] = dd_s[...].astype(BF16)

    tile = lambda i: (i, 0)
    return pl.pallas_call(
        body, name="mixer_bwd", grid=(n_tiles,),
        in_specs=[pl.BlockSpec((t, d), tile), pl.BlockSpec((p + c, d), lambda i: (0, 0)), pl.BlockSpec((t, p), tile),
                  pl.BlockSpec((t, c), lambda i: (i, 2)), pl.BlockSpec((t, c), tile), pl.BlockSpec((t, c), tile),
                  pl.BlockSpec((N_POOL_GROUPS, pg, pg), lambda i: (0, 0, 0)), pl.BlockSpec((1, p), lambda i: (0, 0)), ANY],
        out_specs=[pl.BlockSpec((t, p), tile), pl.BlockSpec((t, c), tile), pl.BlockSpec((t, c), tile),
                   pl.BlockSpec((N_POOL_GROUPS, pg, pg), lambda i: (0, 0, 0)), pl.BlockSpec((SUBLANES, p), lambda i: (0, 0))],
        out_shape=[jax.ShapeDtypeStruct((s, p), F32), jax.ShapeDtypeStruct((s, c), F32), jax.ShapeDtypeStruct((s, c), BF16),
                   jax.ShapeDtypeStruct((N_POOL_GROUPS, pg, pg), F32), jax.ShapeDtypeStruct((SUBLANES, p), F32)],
        scratch_shapes=[pltpu.VMEM((t, p), F32), pltpu.VMEM((t, p + c), F32)],
        compiler_params=_params("arbitrary"),
    )(dzb, w_out, d_pool, proj, h_f, h_b, w_pool, pool_scale, dep)


def _scan_bwd_call(xc, dh, h_dir, gates, dxc_prev, wa, wi, lam, reverse, dep):
    s, c = xc.shape
    heads = c // LRU_HEAD
    t = min(SEQ_TILE, s)
    n_tiles = s // t
    per = t // SUBLANES
    last_blk = s // SUBLANES - 1
    tile = (lambda i: (i, 0)) if reverse else (lambda i: (n_tiles - 1 - i, 0))
    if reverse:
        halo = lambda i: (jnp.minimum((i + 1) * per, last_blk), 0)
    else:
        halo = lambda i: (jnp.maximum((n_tiles - 1 - i) * per - 1, 0), 0)
    whole2 = lambda i: (0, 0)
    whole3 = lambda i: (0, 0, 0)
    has_prev = dxc_prev is not None
    n_in = 11 + int(has_prev) + 1

    def body(*refs):
        xc_ref, dh_ref, h_ref, hh_ref, r_ref, ig_ref, a_ref, mult_ref = refs[:8]
        prev_ref = refs[8] if has_prev else None
        wa_ref, wi_ref, lam_ref = refs[n_in - 4:n_in - 1]
        dxc_ref, dwa_ref, dwi_ref, dba_ref, dbi_ref, dsp_ref = refs[n_in:n_in + 6]
        pr_s, pi_s, carry_s = refs[n_in + 6:]
        step = pl.program_id(0)
        tile_idx = step if reverse else n_tiles - 1 - step

        @pl.when(step == 0)
        def _():
            carry_s[...] = jnp.zeros_like(carry_s)
            dwa_ref[...] = jnp.zeros_like(dwa_ref)
            dwi_ref[...] = jnp.zeros_like(dwi_ref)
            dba_ref[...] = jnp.zeros_like(dba_ref)
            dbi_ref[...] = jnp.zeros_like(dbi_ref)
            dsp_ref[...] = jnp.zeros_like(dsp_ref)

        sp = _softplus(-lam_ref[...])
        rows = lax.broadcasted_iota(jnp.int32, (SUBLANES, c), 0)

        def chunk(r0, u_in):
            rs = pl.ds(r0, SUBLANES)
            xcv = xc_ref[rs, :]
            r, ig, a, mult = r_ref[rs, :], ig_ref[rs, :], a_ref[rs, :], mult_ref[rs, :]
            dhv = dh_ref[rs, :]
            u = _scan_chunk(a, a * dhv, u_in, rows, not reverse)
            if reverse:
                gt = dhv + jnp.where(rows >= 1, pltpu.roll(u, 1, 0), u_in)
                u_out = u[SUBLANES - 1:SUBLANES, :]
            else:
                gt = dhv + jnp.where(rows < SUBLANES - 1, pltpu.roll(u, SUBLANES - 1, 0), u_in)
                u_out = u[0:1, :]
            cur = h_ref[rs, :]
            if reverse:
                after = h_ref[pl.ds(pl.multiple_of(jnp.minimum(r0 + SUBLANES, t - SUBLANES), SUBLANES), SUBLANES), :]
                edge = jnp.where(tile_idx == n_tiles - 1, 0.0, hh_ref[...])
                nxt = jnp.where(r0 == t - SUBLANES, edge, after)
                hs = _shift_rows(cur, cur, nxt, 1, rows)
            else:
                before = h_ref[pl.ds(pl.multiple_of(jnp.maximum(r0 - SUBLANES, 0), SUBLANES), SUBLANES), :]
                edge = jnp.where(tile_idx == 0, 0.0, hh_ref[...])
                prv = jnp.where(r0 == 0, edge, before)
                hs = _shift_rows(prv, cur, cur, -1, rows)
            gx = gt * xcv
            dmult = gx * ig
            di = gx * mult
            dlog_a = (gt * hs) * a - dmult * (a * a) / mult
            dr = dlog_a * (-RG_C * sp)
            dsp_ref[...] += dlog_a * (-RG_C * r)
            dpr = dr * r * (1.0 - r)
            dpi = di * ig * (1.0 - ig)
            dba_ref[...] += dpr
            dbi_ref[...] += dpi
            direct = gt * mult * ig
            if has_prev:
                direct = direct + prev_ref[rs, :]
            dxc_ref[rs, :] = direct
            pr_s[rs, :] = dpr
            pi_s[rs, :] = dpi
            return u_out

        carry_s[...] = _chunk_loop(t, chunk, carry_s[...], unroll=2, descending=not reverse)

        for h in range(heads):
            cs = pl.ds(h * LRU_HEAD, LRU_HEAD)
            xb = xc_ref[:, cs].astype(BF16)
            dprb = pr_s[:, cs].astype(BF16)
            dpib = pi_s[:, cs].astype(BF16)
            dwa_ref[h] += lax.dot_general(xb, dprb, (((0,), (0,)), ((), ())), preferred_element_type=F32)
            dwi_ref[h] += lax.dot_general(xb, dpib, (((0,), (0,)), ((), ())), preferred_element_type=F32)
            dxc_ref[:, cs] += (
                lax.dot_general(dprb, wa_ref[h], (((1,), (1,)), ((), ())), preferred_element_type=F32)
                + lax.dot_general(dpib, wi_ref[h], (((1,), (1,)), ((), ())), preferred_element_type=F32))

    tile_spec = pl.BlockSpec((t, c), tile)
    in_specs = [tile_spec, tile_spec, tile_spec, pl.BlockSpec((SUBLANES, c), halo)] + [tile_spec] * 4
    args = [xc, dh, h_dir, h_dir, *gates]
    if has_prev:
        in_specs.append(tile_spec)
        args.append(dxc_prev)
    in_specs += [pl.BlockSpec((heads, LRU_HEAD, LRU_HEAD), whole3), pl.BlockSpec((heads, LRU_HEAD, LRU_HEAD), whole3),
                 pl.BlockSpec((1, c), whole2), ANY]
    args += [wa, wi, lam, dep]
    assert len(args) == n_in
    return pl.pallas_call(
        body, name="scan_bwd_rev" if reverse else "scan_bwd", grid=(n_tiles,),
        in_specs=in_specs,
        out_specs=[tile_spec,
                   pl.BlockSpec((heads, LRU_HEAD, LRU_HEAD), whole3), pl.BlockSpec((heads, LRU_HEAD, LRU_HEAD), whole3),
                   pl.BlockSpec((SUBLANES, c), whole2), pl.BlockSpec((SUBLANES, c), whole2), pl.BlockSpec((SUBLANES, c), whole2)],
        out_shape=[jax.ShapeDtypeStruct((s, c), F32),
                   jax.ShapeDtypeStruct((heads, LRU_HEAD, LRU_HEAD), F32), jax.ShapeDtypeStruct((heads, LRU_HEAD, LRU_HEAD), F32),
                   jax.ShapeDtypeStruct((SUBLANES, c), F32), jax.ShapeDtypeStruct((SUBLANES, c), F32),
                   jax.ShapeDtypeStruct((SUBLANES, c), F32)],
        scratch_shapes=[pltpu.VMEM((t, c), F32), pltpu.VMEM((t, c), F32), pltpu.VMEM((1, c), F32)],
        compiler_params=_params("arbitrary"),
    )(*args)


def _dproj_call(e_pool, dxc, proj, dgate, conv_w, p):
    s, c = dxc.shape
    pg = p // N_POOL_GROUPS
    t = min(SEQ_TILE, s)
    n_tiles = s // t

    def body(e_ref, ep_ref, en_ref, dx_ref, dxp_ref, dxn_ref, u_ref, up_ref, un_ref, dgate_ref, w_ref,
             dproj_ref, dcw_ref, dcb_ref, st_s):
        i = pl.program_id(0)
        first, last = i == 0, i == n_tiles - 1

        @pl.when(first)
        def _():
            dcw_ref[...] = jnp.zeros_like(dcw_ref)
            dcb_ref[...] = jnp.zeros_like(dcb_ref)

        rows_p = lax.broadcasted_iota(jnp.int32, (SUBLANES, pg), 0)
        rows_c = lax.broadcasted_iota(jnp.int32, (SUBLANES, c), 0)
        w = w_ref[...]

        def chunk(r0, _):
            rs = pl.ds(r0, SUBLANES)
            for g, win in enumerate(POOL_WINDOWS):
                cols = pl.ds(g * pg, pg)
                prv, cur, nxt = _neighbour_chunks(e_ref, ep_ref, en_ref, r0, t, cols, first, last)
                tot = cur
                for o in range(-(win // 2) + 1, win // 2 + 1):
                    if o != 0:
                        tot = tot + _shift_rows(prv, cur, nxt, o, rows_p)
                cnt = _window_counts(r0, i, t, s, win // 2, (SUBLANES, pg))
                st_s[rs, cols] = tot - cur * cnt
            prv, cur, nxt = _neighbour_chunks(dx_ref, dxp_ref, dxn_ref, r0, t, slice(None), first, last)
            du = w[1:2] * cur
            du += w[0:1] * _shift_rows(prv, cur, nxt, 1, rows_c)
            du += w[2:3] * _shift_rows(prv, cur, nxt, -1, rows_c)
            du += w[3:4] * _shift_rows(prv, cur, nxt, -2, rows_c)
            st_s[rs, pl.ds(p, c)] = du
            uprv, ucur, unxt = _neighbour_chunks(u_ref, up_ref, un_ref, r0, t, slice(None), first, last)
            dcb_ref[...] += cur
            for j, o in enumerate((-1, 0, 1, 2)):
                dcw_ref[j] += cur * _shift_rows(uprv, ucur, unxt, o, rows_c)

        _chunk_loop(t, chunk)
        dproj_ref[:, pl.ds(0, p + c)] = st_s[...].astype(BF16)
        dproj_ref[:, pl.ds(p + c, c)] = dgate_ref[...]

    return pl.pallas_call(
        body, name="dproj", grid=(n_tiles,),
        in_specs=_halo_specs(t, s, p, 0) + _halo_specs(t, s, c, 0) + _halo_specs(t, s, c, 1) + [
            pl.BlockSpec((t, c), lambda i: (i, 0)), pl.BlockSpec((4, c), lambda i: (0, 0))],
        out_specs=[pl.BlockSpec((t, p + 2 * c), lambda i: (i, 0)),
                   pl.BlockSpec((4, SUBLANES, c), lambda i: (0, 0, 0)), pl.BlockSpec((SUBLANES, c), lambda i: (0, 0))],
        out_shape=[jax.ShapeDtypeStruct((s, p + 2 * c), BF16), jax.ShapeDtypeStruct((4, SUBLANES, c), F32),
                   jax.ShapeDtypeStruct((SUBLANES, c), F32)],
        scratch_shapes=[pltpu.VMEM((t, p + c), F32)],
        compiler_params=_params("arbitrary"),
    )(e_pool, e_pool, e_pool, dxc, dxc, dxc, proj, proj, proj, dgate, conv_w)


def _dx_call(dproj, w_in, dz1, dep):
    s, e = dproj.shape
    n, d, e4 = w_in.shape
    tm = min(MM_TILE, s)

    def body(dp_ref, w_hbm, dz_ref, dep_ref, o_ref, w_s, sems):
        @pl.when(pl.program_id(0) == 0)
        def _():
            copies = [pltpu.make_async_copy(w_hbm.at[k], w_s.at[:, pl.ds(k * e4, e4)], sems.at[k]) for k in range(n)]
            for cp in copies:
                cp.start()
            for cp in copies:
                cp.wait()

        o_ref[...] = ALPHA * dz_ref[...] + lax.dot_general(
            dp_ref[...], w_s[...], (((1,), (1,)), ((), ())), preferred_element_type=F32)

    return pl.pallas_call(
        body, name="grad_x", grid=(s // tm,),
        in_specs=[pl.BlockSpec((tm, e), lambda i: (i, 0)), ANY, pl.BlockSpec((tm, d), lambda i: (i, 0)), ANY],
        out_specs=pl.BlockSpec((tm, d), lambda i: (i, 0)),
        out_shape=jax.ShapeDtypeStruct((s, d), F32),
        scratch_shapes=[pltpu.VMEM((d, e), BF16), pltpu.SemaphoreType.DMA((n,))],
        compiler_params=_params("arbitrary"),
    )(dproj, w_in, dz1, dep)


def _row_tile(rows, cols, n_arrays):
    limit = max(SUBLANES, ELT_BLOCK_BYTES // (4 * cols * max(1, n_arrays // 4)))
    best = SUBLANES
    for cand in range(SUBLANES, min(rows, limit) + 1, SUBLANES):
        if rows % cand == 0:
            best = cand
    return best if rows % SUBLANES == 0 else rows


def _cast_to_slot_call(a, idx, dtype, name, dep):
    rows, cols = a.shape
    tr = _row_tile(rows, cols, 2)
    extra = [] if dep is None else [dep]

    def body(idx_ref, a_ref, *rest):
        rest[-1][...] = a_ref[...].astype(dtype)

    return pl.pallas_call(
        body, name=name,
        grid_spec=pltpu.PrefetchScalarGridSpec(
            num_scalar_prefetch=1, grid=(rows // tr,),
            in_specs=[pl.BlockSpec((tr, cols), lambda i, idx_ref: (i, 0))] + [ANY] * len(extra),
            out_specs=pl.BlockSpec((None, tr, cols), lambda i, idx_ref: (idx_ref[1], i, 0))),
        out_shape=jax.ShapeDtypeStruct((N_CHIPS, rows, cols), dtype),
        compiler_params=_params("arbitrary"),
    )(idx, a, *extra)


def _add_half_call(g, recv, idx, name):
    _, rows, cols = g.shape
    tr = _row_tile(rows, cols, 3)

    def body(idx_ref, g_ref, r_ref, o_ref):
        o_ref[...] = g_ref[...] + r_ref[...]

    return pl.pallas_call(
        body, name=name,
        grid_spec=pltpu.PrefetchScalarGridSpec(
            num_scalar_prefetch=1, grid=(rows // tr,),
            in_specs=[pl.BlockSpec((None, tr, cols), lambda i, idx_ref: (idx_ref[0], i, 0)),
                      pl.BlockSpec((tr, cols), lambda i, idx_ref: (i, 0))],
            out_specs=pl.BlockSpec((None, tr, cols), lambda i, idx_ref: (idx_ref[1], i, 0))),
        out_shape=jax.ShapeDtypeStruct((N_CHIPS, rows, cols), F32),
        compiler_params=_params("arbitrary"),
    )(idx, g, recv)


def _sum_chips_call(own, recv, idx, name):
    _, rows, cols = recv.shape
    tr = _row_tile(rows, cols, 5)
    out_spec = pl.BlockSpec((None, tr, cols), lambda i, idx_ref: (idx_ref[0], i, 0))
    if own is None:
        def body(idx_ref, r_ref, o_ref):
            o_ref[...] = ((r_ref[0] + r_ref[1]) + r_ref[2]) + r_ref[3]
        in_specs = [pl.BlockSpec((N_CHIPS, tr, cols), lambda i, idx_ref: (0, i, 0))]
        args = (recv,)
    else:
        def body(idx_ref, p_ref, r_ref, o_ref):
            o_ref[...] = ((p_ref[...] + r_ref[0]) + r_ref[1]) + r_ref[2]
        in_specs = [pl.BlockSpec((None, tr, cols), lambda i, idx_ref: (idx_ref[1], i, 0)),
                    pl.BlockSpec((N_CHIPS - 1, tr, cols), lambda i, idx_ref: (0, i, 0))]
        args = (own, recv)
    return pl.pallas_call(
        body, name=name,
        grid_spec=pltpu.PrefetchScalarGridSpec(num_scalar_prefetch=1, grid=(rows // tr,), in_specs=in_specs, out_specs=out_spec),
        out_shape=jax.ShapeDtypeStruct((2, rows, cols), F32),
        compiler_params=_params("arbitrary"),
    )(idx, *args)


def _adamw_math(g, w, m, v):
    mn = ADAM_B1 * m + (1.0 - ADAM_B1) * g
    vn = ADAM_B2 * v + (1.0 - ADAM_B2) * (g * g)
    m_hat = mn / (1.0 - ADAM_B1 ** ADAM_STEP)
    v_hat = vn / (1.0 - ADAM_B2 ** ADAM_STEP)
    return -ADAM_LR * (m_hat / (jnp.sqrt(v_hat) + ADAM_EPS) + ADAM_WD * w), mn, vn


def _adamw_call(g, w, m, v, name):
    rows, cols = w.shape
    tr = _row_tile(rows, cols, 4)

    def body(g_ref, w_ref, m_ref, v_ref, go_ref, d_ref, mo_ref, vo_ref):
        gv = g_ref[...]
        go_ref[...] = gv
        d_ref[...], mo_ref[...], vo_ref[...] = _adamw_math(gv, w_ref[...], m_ref[...], v_ref[...])

    spec = pl.BlockSpec((tr, cols), lambda i: (i, 0))
    shape = jax.ShapeDtypeStruct((rows, cols), F32)
    return pl.pallas_call(
        body, name=name, grid=(rows // tr,),
        in_specs=[spec] * 4, out_specs=[spec] * 4, out_shape=[shape] * 4,
        compiler_params=_params("arbitrary"),
    )(g, w, m, v)


def _adamw_small_call(gs, ws, ms, vs):
    n = len(ws)

    def body(*refs):
        ins = [refs[k * n:(k + 1) * n] for k in range(4)]
        outs = [refs[(4 + k) * n:(5 + k) * n] for k in range(3)]
        for a in range(n):
            outs[0][a][...], outs[1][a][...], outs[2][a][...] = _adamw_math(*[ins[k][a][...] for k in range(4)])

    whole = pl.BlockSpec(memory_space=pltpu.VMEM)
    res = pl.pallas_call(
        body, name="adamw_small",
        in_specs=[whole] * (4 * n), out_specs=[whole] * (3 * n),
        out_shape=[jax.ShapeDtypeStruct(w.shape, F32) for w in ws] * 3,
        compiler_params=pltpu.CompilerParams(vmem_limit_bytes=VMEM_LIMIT),
    )(*gs, *ws, *ms, *vs)
    return res[:n], res[n:2 * n], res[2 * n:]


def _mesh_place():
    x, y, c = lax.axis_index("x"), lax.axis_index("y"), lax.axis_index("c")
    chips = [(1 - x, y), (x, 1 - y), (1 - x, 1 - y)]
    return x, y, c, chips


def _remote(src, dst, send_sems, recv_sems, idx, device):
    return pltpu.make_async_remote_copy(src_ref=src, dst_ref=dst, send_sem=send_sems.at[idx], recv_sem=recv_sems.at[idx],
                                        device_id=device, device_id_type=MESH)


HBM_SPEC = pl.BlockSpec(memory_space=pltpu.HBM)
SEM_SPEC = pl.BlockSpec(memory_space=pltpu.SEMAPHORE)
ORDERED_EFFECT = pltpu.SideEffectType.DATAFLOW_SIDE_EFFECTING


def _in_hbm(a):
    return pltpu.with_memory_space_constraint(a, pltpu.HBM)


def _start_copies_call(name, bufs, groups, after=None):
    n, g = len(bufs), len(groups)
    extra = [] if after is None else [after]
    first_out = n + len(extra)

    def body(*refs):
        outs = refs[first_out:first_out + n]
        sems = refs[first_out + n:first_out + n + 2 * g]
        token = refs[first_out + n + 2 * g]
        for i, (which, copies_fn, _) in enumerate(groups):
            for mine, _ in copies_fn([outs[w] for w in which], sems[2 * i], sems[2 * i + 1]):
                mine.start()
        token[...] = jnp.zeros_like(token)

    sem_shapes = [pltpu.SemaphoreType.DMA((cnt,)) for _, _, cnt in groups for _ in range(2)]
    res = pl.pallas_call(
        body, name=name,
        in_specs=[HBM_SPEC] * n + [ANY] * len(extra),
        out_specs=[HBM_SPEC] * n + [SEM_SPEC] * (2 * g) + [pl.BlockSpec(memory_space=pltpu.VMEM)],
        out_shape=[pltpu.HBM(a.shape, a.dtype) for a in bufs] + sem_shapes + [jax.ShapeDtypeStruct((SUBLANES, LANES), F32)],
        input_output_aliases={a: a for a in range(n)},
        compiler_params=pltpu.CompilerParams(has_side_effects=ORDERED_EFFECT),
    )(*[_in_hbm(a) for a in bufs], *extra)
    sems = res[n:n + 2 * g]
    return list(res[:n]), [(sems[2 * i], sems[2 * i + 1]) for i in range(g)], res[n + 2 * g]


def _wait_copies_call(name, bufs, sems, copies_fn, after):
    n = len(bufs)

    def body(*refs):
        ins = refs[:n]
        send_sems, recv_sems = refs[n], refs[n + 1]
        for mine, arriving in copies_fn(list(ins), send_sems, recv_sems):
            arriving.wait_recv()
            mine.wait_send()

    res = pl.pallas_call(
        body, name=name,
        in_specs=[HBM_SPEC] * n + [SEM_SPEC, SEM_SPEC, ANY],
        out_specs=[HBM_SPEC] * n,
        out_shape=[pltpu.HBM(a.shape, a.dtype) for a in bufs],
        input_output_aliases={a: a for a in range(n)},
        compiler_params=pltpu.CompilerParams(has_side_effects=ORDERED_EFFECT),
    )(*bufs, sems[0], sems[1], after)
    return list(res)


def _gather_copies(bufs, send_sems, recv_sems):
    x, y, c, chips = _mesh_place()
    k = 2 * x + y
    out = []
    for a, buf in enumerate(bufs):
        for j, (px, py) in enumerate(chips):
            kj = 2 * px + py
            mine = _remote(buf.at[k, c], buf.at[k, c], send_sems, recv_sems, 3 * a + j, (px, py, c))
            arriving = _remote(buf.at[k, c], buf.at[kj, c], send_sems, recv_sems, 3 * a + j, (px, py, c))
            out.append((mine, arriving))
    return out


def _exchange_copies(n_sharded, n_replicated):
    def copies(bufs, send_sems, recv_sems):
        x, y, c, chips = _mesh_place()
        k = 2 * x + y
        sums, lands = bufs[:n_sharded], bufs[n_sharded:2 * n_sharded]
        repl = bufs[2 * n_sharded:]
        out = []
        for j, (px, py) in enumerate(chips):
            kj = 2 * px + py
            for a in range(n_sharded):
                cp = _remote(sums[a].at[kj], lands[a].at[j], send_sems, recv_sems, 3 * a + j, (px, py, c))
                out.append((cp, cp))
            for a in range(n_replicated):
                idx = 3 * (n_sharded + a) + j
                mine = _remote(repl[a].at[k], repl[a].at[k], send_sems, recv_sems, idx, (px, py, c))
                arriving = _remote(repl[a].at[k], repl[a].at[kj], send_sems, recv_sems, idx, (px, py, c))
                out.append((mine, arriving))
        return out
    return copies


def _sibling_copies(n, halves):
    def copies(bufs, send_sems, recv_sems):
        x, y, c, _ = _mesh_place()
        out = []
        for a in range(n):
            src = bufs[a].at[1 - c] if halves else bufs[a]
            cp = _remote(src, bufs[n + a], send_sems, recv_sems, a, (x, y, 1 - c))
            out.append((cp, cp))
        return out
    return copies


def _forward_copies(bufs, send_sems, recv_sems):
    x, y, c, chips = _mesh_place()
    out = []
    for a, buf in enumerate(bufs):
        for j, (px, py) in enumerate(chips):
            kj = 2 * px + py
            mine = _remote(buf.at[kj, c], buf.at[kj, c], send_sems, recv_sems, 3 * a + j, (x, y, 1 - c))
            arriving = _remote(buf.at[kj, c], buf.at[kj, 1 - c], send_sems, recv_sems, 3 * a + j, (x, y, 1 - c))
            out.append((mine, arriving))
    return out


def _join_copies(bufs, send_sems, recv_sems):
    x, y, c, _ = _mesh_place()
    out = []
    for a, buf in enumerate(bufs):
        mine = _remote(buf.at[c], buf.at[c], send_sems, recv_sems, a, (x, y, 1 - c))
        arriving = _remote(buf.at[c], buf.at[1 - c], send_sems, recv_sems, a, (x, y, 1 - c))
        out.append((mine, arriving))
    return out


def _forward_to_sibling_call(bufs, name):
    n = len(bufs)

    def body(*refs):
        ins, outs = refs[:n], refs[n:2 * n]
        send_sems, recv_sems = refs[2 * n:]
        x, y, c, chips = _mesh_place()
        sibling = (x, y, 1 - c)
        sends = []
        for a in range(n):
            for j, (px, py) in enumerate(chips):
                kj = 2 * px + py
                sends.append(_remote(ins[a].at[kj, c], outs[a].at[kj, c], send_sems, recv_sems, 3 * a + j, sibling))
        for cp in sends:
            cp.start()
        for a in range(n):
            for j, (px, py) in enumerate(chips):
                kj = 2 * px + py
                _remote(ins[a].at[kj, c], outs[a].at[kj, 1 - c], send_sems, recv_sems, 3 * a + j, sibling).wait_recv()
        for cp in sends:
            cp.wait_send()

    return pl.pallas_call(
        body, name=name,
        in_specs=[ANY] * n, out_specs=[ANY] * n,
        out_shape=[jax.ShapeDtypeStruct(a.shape, a.dtype) for a in bufs],
        input_output_aliases={a: a for a in range(n)},
        scratch_shapes=[pltpu.SemaphoreType.DMA((3 * n,)), pltpu.SemaphoreType.DMA((3 * n,))],
    )(*bufs)


def _pack(arrays, rows_multiple):
    flat = jnp.concatenate([a.reshape(-1) for a in arrays])
    per = LANES * rows_multiple
    padded = -(-flat.shape[0] // per) * per
    flat = jnp.pad(flat, (0, padded - flat.shape[0]))
    return flat.reshape(-1, LANES)


def _unpack(packed, shapes):
    flat = packed.reshape(-1)
    out, at = [], 0
    for shp in shapes:
        size = 1
        for dim in shp:
            size *= dim
        out.append(flat[at:at + size].reshape(shp))
        at += size
    return out


def _halves(a):
    return a.reshape((2, a.shape[0] // 2) + a.shape[1:])


def kernel(x, ln_mix_g, ln_mix_b, w_in, w_pool, pool_scale, conv_w, conv_b, w_rg_a, b_rg_a, w_rg_i, b_rg_i, rg_lambda, w_out, ln_ffn_g, ln_ffn_b, w_mlp_in, w_mlp_out, loss_target, m_ln_mix_g, m_ln_mix_b, m_w_in, m_w_pool, m_pool_scale, m_conv_w, m_conv_b, m_w_rg_a, m_b_rg_a, m_w_rg_i, m_b_rg_i, m_rg_lambda, m_w_out, m_ln_ffn_g, m_ln_ffn_b, m_w_mlp_in, m_w_mlp_out, v_ln_mix_g, v_ln_mix_b, v_w_in, v_w_pool, v_pool_scale, v_conv_w, v_conv_b, v_w_rg_a, v_b_rg_a, v_w_rg_i, v_b_rg_i, v_rg_lambda, v_w_out, v_ln_ffn_g, v_ln_ffn_b, v_w_mlp_in, v_w_mlp_out):
    weights = dict(ln_mix_g=ln_mix_g, ln_mix_b=ln_mix_b, w_in=w_in, w_pool=w_pool, pool_scale=pool_scale, conv_w=conv_w,
                   conv_b=conv_b, w_rg_a=w_rg_a, b_rg_a=b_rg_a, w_rg_i=w_rg_i, b_rg_i=b_rg_i, rg_lambda=rg_lambda,
                   w_out=w_out, ln_ffn_g=ln_ffn_g, ln_ffn_b=ln_ffn_b, w_mlp_in=w_mlp_in, w_mlp_out=w_mlp_out)
    m_in = dict(ln_mix_g=m_ln_mix_g, ln_mix_b=m_ln_mix_b, w_in=m_w_in, w_pool=m_w_pool, pool_scale=m_pool_scale,
                conv_w=m_conv_w, conv_b=m_conv_b, w_rg_a=m_w_rg_a, b_rg_a=m_b_rg_a, w_rg_i=m_w_rg_i, b_rg_i=m_b_rg_i,
                rg_lambda=m_rg_lambda, w_out=m_w_out, ln_ffn_g=m_ln_ffn_g, ln_ffn_b=m_ln_ffn_b, w_mlp_in=m_w_mlp_in,
                w_mlp_out=m_w_mlp_out)
    v_in = dict(ln_mix_g=v_ln_mix_g, ln_mix_b=v_ln_mix_b, w_in=v_w_in, w_pool=v_w_pool, pool_scale=v_pool_scale,
                conv_w=v_conv_w, conv_b=v_conv_b, w_rg_a=v_w_rg_a, b_rg_a=v_b_rg_a, w_rg_i=v_w_rg_i, b_rg_i=v_b_rg_i,
                rg_lambda=v_rg_lambda, w_out=v_w_out, ln_ffn_g=v_ln_ffn_g, ln_ffn_b=v_ln_ffn_b, w_mlp_in=v_w_mlp_in,
                w_mlp_out=v_w_mlp_out)
    names = list(weights)

    xs = x[0]
    tgt = loss_target[0]
    s, d = xs.shape
    p = c = d // 2
    pg = p // N_POOL_GROUPS
    core = lax.axis_index("c")
    shard = 2 * lax.axis_index("x") + lax.axis_index("y")

    idx = jnp.stack([core, shard]).astype(jnp.int32)
    small_shard = _pack([conv_w[0], b_rg_a[0], b_rg_i[0], rg_lambda[0]], 2 * SUBLANES)
    to_gather = [(w_in[0], BF16), (w_out[0], BF16), (w_mlp_in[0], BF16), (w_mlp_out[0], BF16),
                 (w_pool[0].reshape(-1, pg), BF16), (small_shard, F32)]

    def slot_view(i, dep):
        a, dt = to_gather[i]
        sl = _cast_to_slot_call(a, idx, dt, f"gather_slot_{i}", dep)
        return sl.reshape(N_CHIPS, 2, sl.shape[1] // 2, sl.shape[2])

    first, later = (0, 4, 5), (1, 2, 3)
    fly_a, sems_a, token_a = _start_copies_call(
        "gather_start_first", [slot_view(i, None) for i in first], [((0, 1, 2), _gather_copies, 3 * len(first))])
    later_views = []
    for i in later:
        later_views.append(slot_view(i, later_views[-1] if later_views else token_a))
    xb = _cast_call(xs, later_views[-1])
    got_first = _wait_copies_call("gather_wait_w_in", fly_a, sems_a[0], _gather_copies, xb)
    fly_b, sems_b, g_token = _start_copies_call(
        "gather_start_later", later_views + got_first,
        [((0,), _gather_copies, 3), ((1,), _gather_copies, 3), ((2,), _gather_copies, 3)])
    got_first = fly_b[len(later):]
    in_flight = dict(zip(later, fly_b))
    g_sems = [None] + list(sems_b)

    def arrive(which, group, after, tag):
        return _wait_copies_call(f"gather_wait_{tag}", [in_flight[w] for w in which], g_sems[group], _gather_copies, after)

    def pass_on(got, tag):
        flying, sems, token = _start_copies_call(
            f"gather_forward_start_{tag}", got, [(tuple(range(len(got))), _forward_copies, 3 * len(got))])
        return (flying, sems[0], tag), token

    def passed_on(state, after):
        flying, sems, tag = state
        return _wait_copies_call(f"gather_forward_wait_{tag}", flying, sems, _forward_copies, after)

    gathered = [None] * len(to_gather)
    gathered[0], gathered[4], gathered[5] = _forward_to_sibling_call(got_first, "gather_forward_w_in")
    w_in_f = gathered[0].reshape((N_CHIPS,) + w_in.shape[1:])
    w_pool_f = gathered[4].reshape(N_CHIPS, N_POOL_GROUPS, pg // N_CHIPS, pg).transpose(1, 0, 2, 3).reshape(N_POOL_GROUPS, pg, pg)
    c4 = c // N_CHIPS
    small_parts = [_unpack(gathered[5][k].reshape(-1, LANES), [(4, c4), (2, c4), (2, c4), (2, c4)]) for k in range(N_CHIPS)]
    conv_w_f = jnp.concatenate([sp_[0] for sp_ in small_parts], axis=1)
    b_a_f = jnp.concatenate([sp_[1] for sp_ in small_parts], axis=1)
    b_i_f = jnp.concatenate([sp_[2] for sp_ in small_parts], axis=1)
    lam_f = jnp.concatenate([sp_[3] for sp_ in small_parts], axis=1)
    wa_b = w_rg_a[0].astype(BF16)
    wi_b = w_rg_i[0].astype(BF16)

    proj = _proj_call(xb, w_in_f)
    xc = _conv_call(proj, conv_w_f, conv_b, c)
    fwd_w_out, token = pass_on(arrive((1,), 1, xc, "w_out"), "w_out")
    h_b, *gates_b = _scan_fwd_call(xc, wa_b[1], wi_b[1], b_a_f[1:2], b_i_f[1:2], lam_f[1:2], True, token)
    h_f, *gates_f = _scan_fwd_call(xc, wa_b[0], wi_b[0], b_a_f[0:1], b_i_f[0:1], lam_f[0:1], False, token)
    y, d_pool = _pool_combine_call(proj, h_f, h_b, w_pool_f, pool_scale, p)
    w_out_f = passed_on(fwd_w_out, y)[0].reshape(d, d)
    fwd_w1, token = pass_on(arrive((2,), 2, y, "w_mlp_in"), "w_mlp_in")
    xh1, x1b, rstd1 = _out_ln1_call(y, w_out_f, xs, ln_mix_g, ln_mix_b, token)
    w1_f = passed_on(fwd_w1, x1b)[0].reshape((N_CHIPS,) + w_mlp_in.shape[1:])
    first_half = _mlp_in_call(x1b, w1_f, g_token, None)
    fwd_w2, token = pass_on(arrive((3,), 3, first_half[0], "w_mlp_out"), "w_mlp_out")
    r_act, hsq = _mlp_in_call(x1b, w1_f, token, first_half)
    w2_f = passed_on(fwd_w2, hsq)[0].reshape(N_CHIPS * w_mlp_out.shape[1], d)
    dz2, dz2b, loss8, dg2, db2 = _mlp_out_ln2_call(hsq, w2_f, xh1, ln_mix_g, ln_mix_b, ln_ffn_g, ln_ffn_b, tgt)

    def start_siblings(grads, halves, tag, after=None):
        lands = [lax.empty(g.shape[1:] if halves else g.shape, g.dtype) for g in grads]
        copies = _sibling_copies(len(grads), halves)
        flying, sems, token = _start_copies_call(
            f"siblings_start_{tag}", list(grads) + lands, [(tuple(range(2 * len(grads))), copies, len(grads))], after)
        return (flying, sems[0], copies, len(grads), tag), token

    def finish_siblings(state, after):
        flying, sems, copies, n, tag = state
        got = _wait_copies_call(f"siblings_wait_{tag}", flying, sems, copies, after)
        return got[:n], got[n:]

    half_own = jnp.reshape(core, (1,)).astype(jnp.int32)
    half_sibling = 1 - half_own

    def chip_sum_of(a, b, row_sharded, tag, dep, overlapped):
        for_sibling = _half_grad_call(a, b, half_sibling, row_sharded, None, f"grad_{tag}_for_sibling", dep)
        state, token = start_siblings([for_sibling], False, tag)
        results = overlapped(token)
        _, (from_sibling,) = finish_siblings(state, results[0])
        return _half_grad_call(a, b, half_own, row_sharded, from_sibling, f"grad_{tag}", token), results

    def start_exchange(sums, n_repl, tag):
        n_sh = len(sums) - n_repl
        lands = [lax.empty((N_CHIPS - 1,) + a.shape[1:], a.dtype) for a in sums[:n_sh]]
        bufs = sums[:n_sh] + lands + sums[n_sh:]
        copies = _exchange_copies(n_sh, n_repl)
        flying, sems, token = _start_copies_call(
            f"reduce_start_{tag}", bufs, [(tuple(range(len(bufs))), copies, 3 * len(sums))])
        return (flying, sems[0], copies, n_sh, tag), token

    def finish_exchange(state, after):
        flying, sems, copies, n_sh, tag = state
        got = _wait_copies_call(f"reduce_wait_{tag}", flying, sems, copies, after)
        halves = []
        for a in range(n_sh):
            own, land = got[a], got[n_sh + a]
            cols = own.shape[-1]
            total = _sum_chips_call(own.reshape(N_CHIPS, -1, cols), land.reshape(N_CHIPS - 1, -1, cols), idx,
                                    f"reduce_sum_{tag}_{a}")
            halves.append(total.reshape((2,) + own.shape[1:]))
        for a, rp in enumerate(got[2 * n_sh:]):
            halves.append(_sum_chips_call(None, rp, idx, f"reduce_sum_{tag}_r{a}"))
        return halves

    def start_join(halves, tag):
        flying, sems, token = _start_copies_call(
            f"join_start_{tag}", halves, [(tuple(range(len(halves))), _join_copies, len(halves))])
        return (flying, sems[0], tag), token

    def finish_join(state, after):
        flying, sems, tag = state
        return _wait_copies_call(f"join_wait_{tag}", flying, sems, _join_copies, after)

    sum_w2, (dpre,) = chip_sum_of(hsq, dz2b, True, "w_mlp_out", g_token,
                                  lambda tok: (_dhsq_call(dz2b, w2_f, r_act, tok),))
    flying_w2, token = start_exchange([sum_w2], 0, "w2")
    sum_w1, (dz1, dz1b, dg1, db1) = chip_sum_of(
        x1b, dpre, False, "w_mlp_in", token,
        lambda tok: _dx1_ln1_bwd_call(dpre, w1_f, dz2, xh1, rstd1, ln_mix_g, tok))
    flying_w1, token = start_exchange([sum_w1], 0, "w1")

    sum_wout, (e_pool, dh, dgate, g_wpool, g_pscale8) = chip_sum_of(
        y, dz1b, True, "w_out", token,
        lambda tok: _mixer_bwd_call(dz1b, w_out_f, d_pool, proj, h_f, h_b, w_pool_f, pool_scale, p, tok))
    flying_wout, token = start_exchange([sum_wout], 0, "w_out")
    dxc0, g_wa0, g_wi0, g_ba0, g_bi0, g_sp0 = _scan_bwd_call(
        xc, dh, h_f, gates_f, None, wa_b[0], wi_b[0], lam_f[0:1], False, token)
    dxc, g_wa1, g_wi1, g_ba1, g_bi1, g_sp1 = _scan_bwd_call(
        xc, dh, h_b, gates_b, dxc0, wa_b[1], wi_b[1], lam_f[1:2], True, token)
    dproj, g_cw8, g_cb8 = _dproj_call(e_pool, dxc, proj, dgate, conv_w_f, p)

    rowsum = lambda a8: jnp.sum(a8, axis=-2)
    g_lam = jnp.stack([rowsum(g_sp0), rowsum(g_sp1)]) * (-_sigmoid(-lam_f))
    small_grads = {
        "ln_mix_g": rowsum(dg1), "ln_mix_b": rowsum(db1), "ln_ffn_g": rowsum(dg2), "ln_ffn_b": rowsum(db2),
        "pool_scale": rowsum(g_pscale8), "conv_b": rowsum(g_cb8),
        "w_rg_a": jnp.stack([g_wa0, g_wa1]), "w_rg_i": jnp.stack([g_wi0, g_wi1]),
        "w_pool": g_wpool, "conv_w": rowsum(g_cw8),
        "b_rg_a": jnp.stack([rowsum(g_ba0), rowsum(g_ba1)]), "b_rg_i": jnp.stack([rowsum(g_bi0), rowsum(g_bi1)]),
        "rg_lambda": g_lam,
    }
    small_names = list(small_grads)
    small_shapes = [small_grads[nm].shape for nm in small_names]
    loss_share = jnp.reshape(jnp.sum(loss8) * (0.5 / d), (1,))
    g_small = _halves(_pack([small_grads[nm] for nm in small_names] + [loss_share], 2 * SUBLANES))
    sib_small, token = start_siblings([g_small], True, "small")
    flying_small = []

    def small_exchange_and_grad_x(tok):
        (mine,), (theirs,) = finish_siblings(sib_small, tok)
        small_sum = _add_half_call(mine, theirs, idx, "reduce_add_small")
        state, tok = start_exchange([small_sum], 1, "small")
        flying_small.append(state)
        return (_dx_call(dproj, w_in_f, dz1, tok),)

    sum_win, (grad_x,) = chip_sum_of(xb, dproj, False, "w_in", token, small_exchange_and_grad_x)
    flying_small = flying_small[0]
    flying_win, token = start_exchange([sum_win], 0, "w_in")

    grad_w, delta_w, new_m, new_v = {}, {}, {}, {}

    def adamw(nm, full):
        w2d = weights[nm][0]
        g2d = full.reshape(w2d.shape)
        go, dl, mn, vn = _adamw_call(g2d, w2d, m_in[nm][0], v_in[nm][0], f"adamw_{nm}")
        grad_w[nm], delta_w[nm], new_m[nm], new_v[nm] = go[None], dl[None], mn[None], vn[None]
        return vn

    join_w2, token = start_join(finish_exchange(flying_w2, token), "w2")
    join_w1, token = start_join(finish_exchange(flying_w1, token), "w1")
    join_wout, token = start_join(finish_exchange(flying_wout, token), "w_out")
    last = adamw("w_mlp_out", finish_join(join_w2, token)[0])
    last = adamw("w_mlp_in", finish_join(join_w1, last)[0])
    last = adamw("w_out", finish_join(join_wout, last)[0])

    join_small, token = start_join(finish_exchange(flying_small, last), "small")
    join_win, token = start_join(finish_exchange(flying_win, token), "w_in")
    small_joined = finish_join(join_small, token)[0]
    *small_sums, loss_sum = _unpack(small_joined.reshape(-1, LANES), small_shapes + [(1,)])
    small_full = dict(zip(small_names, small_sums))
    local = dict(small_full)
    local["w_pool"] = lax.dynamic_slice_in_dim(small_full["w_pool"], shard * (pg // N_CHIPS), pg // N_CHIPS, axis=1)
    for nm in ("conv_w", "b_rg_a", "b_rg_i", "rg_lambda"):
        local[nm] = lax.dynamic_slice_in_dim(small_full[nm], shard * c4, c4, axis=1)
    small_g = [local[nm].reshape(weights[nm].shape) for nm in small_names]
    small_d, small_m, small_v = _adamw_small_call(
        small_g, [weights[nm] for nm in small_names], [m_in[nm] for nm in small_names], [v_in[nm] for nm in small_names])
    for nm, gl, dl, mn, vn in zip(small_names, small_g, small_d, small_m, small_v):
        grad_w[nm], delta_w[nm], new_m[nm], new_v[nm] = gl, dl, mn, vn
    adamw("w_in", finish_join(join_win, small_v[0])[0])

    loss = loss_sum[0]
    return (loss, grad_x[None], *[grad_w[nm] for nm in names], *[delta_w[nm] for nm in names],
            *[new_m[nm] for nm in names], *[new_v[nm] for nm in names])
```

```python
import jax
import jax.numpy as jnp
from jax import lax
from jax.experimental import pallas as pl
from jax.experimental.pallas import tpu as pltpu

F32 = jnp.float32
BF16 = jnp.bfloat16

N_CHIPS = 4
LANES = 128
SUBLANES = 8
LRU_HEAD = 128
N_POOL_GROUPS = 4
POOL_WINDOWS = (2, 4, 8, 16)
RG_C = 8.0
LN_EPS = 1e-5
ALPHA = 2.0 ** 0.25
ADAM_LR, ADAM_B1, ADAM_B2, ADAM_EPS, ADAM_WD, ADAM_STEP = 0.001, 0.9, 0.999, 1e-08, 0.01, 10
VMEM_LIMIT = 56 * 1024 * 1024
SEQ_TILE = 512
MM_TILE = 512
LN_MM_K = 2048
LN_UNROLL = 8
ELT_BLOCK_BYTES = 2 * 1024 * 1024
RESIDENT_OPERAND_BYTES = 16 * 1024 * 1024
MESH = pl.DeviceIdType.MESH
ANY = pl.BlockSpec(memory_space=pl.ANY)


def _params(*sem):
    return pltpu.CompilerParams(dimension_semantics=sem, vmem_limit_bytes=VMEM_LIMIT)


def _sigmoid(z):
    return 1.0 / (1.0 + jnp.exp(-z))


def _neg_expm1(z):
    series = -(z * (1.0 + z * (0.5 + z * (1.0 / 6.0 + z * (1.0 / 24.0)))))
    return jnp.where(z > -0.01, series, 1.0 - jnp.exp(z))


def _softplus(z):
    return jnp.maximum(z, 0.0) + jnp.log1p(jnp.exp(-jnp.abs(z)))


_GELU_K = 0.7978845608028654
_GELU_C = 0.044715


def _gelu_and_grad(u):
    t = jnp.tanh(_GELU_K * (u + _GELU_C * (u * u * u)))
    g = 0.5 * u * (1.0 + t)
    dg = 0.5 * (1.0 + t) + 0.5 * u * (1.0 - t * t) * (_GELU_K * (1.0 + 3.0 * _GELU_C * u * u))
    return g, dg


def _shift_rows(prv, cur, nxt, o, rows):
    if o == 0:
        return cur
    if o == SUBLANES:
        return nxt
    if o == -SUBLANES:
        return prv
    if o > 0:
        return pltpu.roll(jnp.where(rows >= o, cur, nxt), SUBLANES - o, 0)
    p = -o
    return pltpu.roll(jnp.where(rows < SUBLANES - p, cur, prv), p, 0)


def _neighbour_chunks(main_ref, prev_ref, next_ref, r0, t_rows, cols, first_tile, last_tile):
    cur = main_ref[pl.ds(r0, SUBLANES), cols]
    before = main_ref[pl.ds(pl.multiple_of(jnp.maximum(r0 - SUBLANES, 0), SUBLANES), SUBLANES), cols]
    after = main_ref[pl.ds(pl.multiple_of(jnp.minimum(r0 + SUBLANES, t_rows - SUBLANES), SUBLANES), SUBLANES), cols]
    halo_prev = jnp.where(first_tile, 0.0, prev_ref[:, cols])
    halo_next = jnp.where(last_tile, 0.0, next_ref[:, cols])
    prv = jnp.where(r0 == 0, halo_prev, before)
    nxt = jnp.where(r0 == t_rows - SUBLANES, halo_next, after)
    return prv, cur, nxt


def _halo_specs(t_rows, n_rows, width, col_block):
    per = t_rows // SUBLANES
    last = n_rows // SUBLANES - 1
    return [
        pl.BlockSpec((t_rows, width), lambda i: (i, col_block)),
        pl.BlockSpec((SUBLANES, width), lambda i: (jnp.maximum(i * per - 1, 0), col_block)),
        pl.BlockSpec((SUBLANES, width), lambda i: (jnp.minimum((i + 1) * per, last), col_block)),
    ]


def _chunk_loop(t_rows, fn, init=None, unroll=1, descending=False):
    span = SUBLANES * unroll

    def step(ci, carry):
        base = pl.multiple_of(((t_rows // span - 1 - ci) if descending else ci) * span, span)
        for u in range(unroll):
            carry = fn(base + ((unroll - 1 - u) if descending else u) * SUBLANES, carry)
        return carry
    return lax.fori_loop(0, t_rows // span, step, init)


def _scan_chunk(a, b, h_in, rows, reverse):
    for dist in (1, 2, 4):
        if reverse:
            keep = rows < SUBLANES - dist
            shift = SUBLANES - dist
        else:
            keep = rows >= dist
            shift = dist
        b = a * jnp.where(keep, pltpu.roll(b, shift, 0), 0.0) + b
        a = a * jnp.where(keep, pltpu.roll(a, shift, 0), 1.0)
    return a * h_in + b


def _cast_call(x, dep):
    s, d = x.shape
    tm = min(MM_TILE, s)

    def body(x_ref, dep_ref, o_ref):
        o_ref[...] = x_ref[...].astype(BF16)

    return pl.pallas_call(
        body, name="cast_x", grid=(s // tm,),
        in_specs=[pl.BlockSpec((tm, d), lambda i: (i, 0)), ANY],
        out_specs=pl.BlockSpec((tm, d), lambda i: (i, 0)),
        out_shape=jax.ShapeDtypeStruct((s, d), BF16),
        compiler_params=_params("arbitrary"),
    )(x, dep)


def _proj_call(xb, w_in):
    s, d = xb.shape
    n, _, e4 = w_in.shape
    tm = min(MM_TILE, s)

    def body(x_ref, w_hbm, proj_ref, w_s, sems):
        @pl.when(pl.program_id(0) == 0)
        def _():
            copies = [pltpu.make_async_copy(w_hbm.at[k], w_s.at[:, pl.ds(k * e4, e4)], sems.at[k]) for k in range(n)]
            for cp in copies:
                cp.start()
            for cp in copies:
                cp.wait()

        proj_ref[...] = jnp.dot(x_ref[...], w_s[...], preferred_element_type=F32)

    return pl.pallas_call(
        body, name="proj", grid=(s // tm,),
        in_specs=[pl.BlockSpec((tm, d), lambda i: (i, 0)), ANY],
        out_specs=pl.BlockSpec((tm, n * e4), lambda i: (i, 0)),
        out_shape=jax.ShapeDtypeStruct((s, n * e4), F32),
        scratch_shapes=[pltpu.VMEM((d, n * e4), BF16), pltpu.SemaphoreType.DMA((n,))],
        compiler_params=_params("arbitrary"),
    )(xb, w_in)


def _conv_call(proj, conv_w, conv_b, c):
    s = proj.shape[0]
    t = min(SEQ_TILE, s)
    n_tiles = s // t

    def body(u_ref, up_ref, un_ref, w_ref, b_ref, xc_ref):
        i = pl.program_id(0)
        rows = lax.broadcasted_iota(jnp.int32, (SUBLANES, c), 0)
        w = w_ref[...]
        b = b_ref[...]

        def chunk(r0, _):
            prv, cur, nxt = _neighbour_chunks(u_ref, up_ref, un_ref, r0, t, slice(None), i == 0, i == n_tiles - 1)
            acc = b + w[1:2] * cur
            acc += w[0:1] * _shift_rows(prv, cur, nxt, -1, rows)
            acc += w[2:3] * _shift_rows(prv, cur, nxt, 1, rows)
            acc += w[3:4] * _shift_rows(prv, cur, nxt, 2, rows)
            xc_ref[pl.ds(r0, SUBLANES), :] = acc

        _chunk_loop(t, chunk)

    return pl.pallas_call(
        body, name="conv_fwd", grid=(n_tiles,),
        in_specs=_halo_specs(t, s, c, 1) + [pl.BlockSpec((4, c), lambda i: (0, 0)), pl.BlockSpec((1, c), lambda i: (0, 0))],
        out_specs=pl.BlockSpec((t, c), lambda i: (i, 0)),
        out_shape=jax.ShapeDtypeStruct((s, c), F32),
        compiler_params=_params("arbitrary"),
    )(proj, proj, proj, conv_w, conv_b)


def _gate_matmuls(xc_ref, wa_ref, wi_ref, pr_s, pi_s, heads):
    for h in range(heads):
        cs = pl.ds(h * LRU_HEAD, LRU_HEAD)
        xb = xc_ref[:, cs].astype(BF16)
        pr_s[:, cs] = jnp.dot(xb, wa_ref[h], preferred_element_type=F32)
        pi_s[:, cs] = jnp.dot(xb, wi_ref[h], preferred_element_type=F32)


def _rg_gates(pr, pi, ba, bi, sp):
    r = _sigmoid(pr + ba)
    ig = _sigmoid(pi + bi)
    log_a = (-RG_C * r) * sp
    a = jnp.exp(log_a)
    mult = jnp.sqrt(_neg_expm1(2.0 * log_a))
    return r, ig, a, mult


def _scan_fwd_call(xc, wa, wi, ba, bi, lam, reverse, dep):
    s, c = xc.shape
    heads = c // LRU_HEAD
    t = min(SEQ_TILE, s)
    n_tiles = s // t
    tile = (lambda i: (n_tiles - 1 - i, 0)) if reverse else (lambda i: (i, 0))
    whole2 = lambda i: (0, 0)
    whole3 = lambda i: (0, 0, 0)

    def body(xc_ref, wa_ref, wi_ref, ba_ref, bi_ref, lam_ref, dep_ref, h_ref, r_ref, ig_ref, a_ref, mult_ref,
             pr_s, pi_s, carry_s):
        @pl.when(pl.program_id(0) == 0)
        def _():
            carry_s[...] = jnp.zeros_like(carry_s)

        _gate_matmuls(xc_ref, wa_ref, wi_ref, pr_s, pi_s, heads)
        ba_v, bi_v = ba_ref[...], bi_ref[...]
        sp = _softplus(-lam_ref[...])

        rows = lax.broadcasted_iota(jnp.int32, (SUBLANES, c), 0)

        def chunk(r0, h_in):
            rs = pl.ds(r0, SUBLANES)
            r, ig, a, mult = _rg_gates(pr_s[rs, :], pi_s[rs, :], ba_v, bi_v, sp)
            r_ref[rs, :] = r
            ig_ref[rs, :] = ig
            a_ref[rs, :] = a
            mult_ref[rs, :] = mult
            h = _scan_chunk(a, mult * ig * xc_ref[rs, :], h_in, rows, reverse)
            h_ref[rs, :] = h
            return h[0:1, :] if reverse else h[SUBLANES - 1:SUBLANES, :]

        carry_s[...] = _chunk_loop(t, chunk, carry_s[...], unroll=2, descending=reverse)

    return pl.pallas_call(
        body, name="scan_fwd_rev" if reverse else "scan_fwd", grid=(n_tiles,),
        in_specs=[pl.BlockSpec((t, c), tile),
                  pl.BlockSpec((heads, LRU_HEAD, LRU_HEAD), whole3), pl.BlockSpec((heads, LRU_HEAD, LRU_HEAD), whole3),
                  pl.BlockSpec((1, c), whole2), pl.BlockSpec((1, c), whole2), pl.BlockSpec((1, c), whole2), ANY],
        out_specs=[pl.BlockSpec((t, c), tile)] * 5,
        out_shape=[jax.ShapeDtypeStruct((s, c), F32)] * 5,
        scratch_shapes=[pltpu.VMEM((t, c), F32), pltpu.VMEM((t, c), F32), pltpu.VMEM((1, c), F32)],
        compiler_params=_params("arbitrary"),
    )(xc, wa, wi, ba, bi, lam, dep)


def _window_counts(r0, tile_idx, t_rows, n_rows, half, shape):
    pos = tile_idx * t_rows + r0 + lax.broadcasted_iota(jnp.int32, shape, 0)
    hi = jnp.minimum(pos + half, n_rows)
    lo = jnp.maximum(pos - half, 0)
    return (hi - lo).astype(F32)


def _window_inverse_counts(r0, tile_idx, t_rows, n_rows, half, width):
    inv = 1.0 / _window_counts(r0, tile_idx, t_rows, n_rows, half, (SUBLANES, LANES))
    return jnp.tile(inv, (1, width // LANES))


def _pool_combine_call(proj, h_f, h_b, w_pool, pool_scale, p):
    s = proj.shape[0]
    c = h_f.shape[1]
    pg = p // N_POOL_GROUPS
    t = min(SEQ_TILE, s)
    n_tiles = s // t

    def body(u_ref, up_ref, un_ref, gate_ref, hf_ref, hb_ref, wp_ref, sc_ref, y_ref, d_ref, d_s, yr_s):
        i = pl.program_id(0)
        rows = lax.broadcasted_iota(jnp.int32, (SUBLANES, pg), 0)

        def chunk(r0, _):
            rs = pl.ds(r0, SUBLANES)
            for g, w in enumerate(POOL_WINDOWS):
                cols = pl.ds(g * pg, pg)
                prv, cur, nxt = _neighbour_chunks(u_ref, up_ref, un_ref, r0, t, cols, i == 0, i == n_tiles - 1)
                tot = cur
                for o in range(-(w // 2), w // 2):
                    if o != 0:
                        tot = tot + _shift_rows(prv, cur, nxt, o, rows)
                d_s[rs, cols] = tot * _window_inverse_counts(r0, i, t, s, w // 2, pg) - cur
            gate, _ = _gelu_and_grad(gate_ref[rs, :])
            yr_s[rs, :] = (hf_ref[rs, :] + hb_ref[rs, :]) * gate

        _chunk_loop(t, chunk)
        y_ref[:, pl.ds(p, c)] = yr_s[...].astype(BF16)
        d_ref[...] = d_s[...].astype(BF16)
        for g in range(N_POOL_GROUPS):
            cols = pl.ds(g * pg, pg)
            out = jnp.dot(d_s[:, cols].astype(BF16), wp_ref[g], preferred_element_type=F32)
            y_ref[:, cols] = (out * sc_ref[:, cols]).astype(BF16)

    return pl.pallas_call(
        body, name="pool_combine", grid=(n_tiles,),
        in_specs=_halo_specs(t, s, p, 0) + [
            pl.BlockSpec((t, c), lambda i: (i, 2)),
            pl.BlockSpec((t, c), lambda i: (i, 0)), pl.BlockSpec((t, c), lambda i: (i, 0)),
            pl.BlockSpec((N_POOL_GROUPS, pg, pg), lambda i: (0, 0, 0)), pl.BlockSpec((1, p), lambda i: (0, 0))],
        out_specs=[pl.BlockSpec((t, p + c), lambda i: (i, 0)), pl.BlockSpec((t, p), lambda i: (i, 0))],
        out_shape=[jax.ShapeDtypeStruct((s, p + c), BF16), jax.ShapeDtypeStruct((s, p), BF16)],
        scratch_shapes=[pltpu.VMEM((t, p), F32), pltpu.VMEM((t, c), F32)],
        compiler_params=_params("arbitrary"),
    )(proj, proj, proj, proj, h_f, h_b, w_pool, pool_scale)


def _layer_norm_rows(z, g, b):
    mu = jnp.mean(z, axis=-1, keepdims=True)
    zc = z - mu
    var = jnp.mean(zc * zc, axis=-1, keepdims=True)
    rstd = lax.rsqrt(var + LN_EPS)
    xh = zc * rstd
    return xh, rstd, xh * g + b


def _layer_norm_bwd_rows(dx, xh, rstd, g):
    dxh = dx * g
    m1 = jnp.mean(dxh, axis=-1, keepdims=True)
    m2 = jnp.mean(dxh * xh, axis=-1, keepdims=True)
    return rstd * (dxh - m1 - xh * m2)


def _out_ln1_call(y, w_out, x, g1, b1, dep):
    s, d = x.shape
    tm = min(SEQ_TILE, s)

    def body(y_ref, w_ref, x_ref, g_ref, b_ref, dep_ref, xh_ref, x1b_ref, rstd_ref, acc_s, x1_s):
        acc_s[...] = jnp.dot(y_ref[...], w_ref[...], preferred_element_type=F32)
        g, b = g_ref[...], b_ref[...]

        def chunk(r0, _):
            rs = pl.ds(r0, SUBLANES)
            xh, rstd, x1 = _layer_norm_rows(ALPHA * x_ref[rs, :] + acc_s[rs, :], g, b)
            xh_ref[rs, :] = xh
            x1_s[rs, :] = x1
            rstd_ref[rs, :] = rstd

        _chunk_loop(tm, chunk, unroll=LN_UNROLL)
        x1b_ref[...] = x1_s[...].astype(BF16)

    return pl.pallas_call(
        body, name="out_ln1", grid=(s // tm,),
        in_specs=[pl.BlockSpec((tm, d), lambda i: (i, 0)), pl.BlockSpec((d, d), lambda i: (0, 0)),
                  pl.BlockSpec((tm, d), lambda i: (i, 0)),
                  pl.BlockSpec((1, d), lambda i: (0, 0)), pl.BlockSpec((1, d), lambda i: (0, 0)), ANY],
        out_specs=[pl.BlockSpec((tm, d), lambda i: (i, 0)), pl.BlockSpec((tm, d), lambda i: (i, 0)),
                   pl.BlockSpec((tm, 1), lambda i: (i, 0))],
        out_shape=[jax.ShapeDtypeStruct((s, d), F32), jax.ShapeDtypeStruct((s, d), BF16), jax.ShapeDtypeStruct((s, 1), F32)],
        scratch_shapes=[pltpu.VMEM((tm, d), F32), pltpu.VMEM((tm, d), F32)],
        compiler_params=_params("arbitrary"),
    )(y, w_out, x, g1, b1, dep)


def _mlp_in_call(x1b, w1, dep, done):
    s, d = x1b.shape
    n, _, f4 = w1.shape
    tm = min(MM_TILE, s)
    tn = min(1024, f4)
    per = f4 // tn
    blocks = n * per // 2
    first = 0 if done is None else blocks
    extra = [] if done is None else list(done)

    def body(x_ref, w_ref, dep_ref, *rest):
        r_ref, q_ref = rest[-2:]
        r = jnp.maximum(jnp.dot(x_ref[...], w_ref[...], preferred_element_type=F32), 0.0)
        r_ref[...] = r.astype(BF16)
        q_ref[...] = (r * r).astype(BF16)

    out_spec = pl.BlockSpec((tm, tn), lambda j, i: (i, first + j))
    return pl.pallas_call(
        body, name="mlp_in" if done is None else "mlp_in_rest", grid=(blocks, s // tm),
        in_specs=[pl.BlockSpec((tm, d), lambda j, i: (i, 0)),
                  pl.BlockSpec((None, d, tn), lambda j, i: ((first + j) // per, 0, (first + j) % per)), ANY] + [ANY] * len(extra),
        out_specs=[out_spec, out_spec],
        out_shape=[jax.ShapeDtypeStruct((s, n * f4), BF16), jax.ShapeDtypeStruct((s, n * f4), BF16)],
        input_output_aliases={3: 0, 4: 1} if extra else {},
        compiler_params=_params("arbitrary", "arbitrary"),
    )(x1b, w1, dep, *extra)


def _mlp_out_ln2_call(hsq, w2, xh1, g1, b1, g2, b2, target):
    s, f = hsq.shape
    d = w2.shape[1]
    tm = min(MM_TILE, s)
    tk = min(LN_MM_K, f)
    nk = f // tk

    def body(h_ref, w_ref, xh1_ref, g1_ref, b1_ref, g2_ref, b2_ref, t_ref,
             dz_ref, dzb_ref, loss_ref, dg_ref, db_ref, acc_s):
        i, k = pl.program_id(0), pl.program_id(1)

        @pl.when((i == 0) & (k == 0))
        def _():
            loss_ref[...] = jnp.zeros_like(loss_ref)
            dg_ref[...] = jnp.zeros_like(dg_ref)
            db_ref[...] = jnp.zeros_like(db_ref)

        @pl.when(k == 0)
        def _():
            acc_s[...] = jnp.zeros_like(acc_s)

        acc_s[...] += jnp.dot(h_ref[...], w_ref[...], preferred_element_type=F32)

        @pl.when(k == nk - 1)
        def _():
            g1, b1, g2, b2 = g1_ref[...], b1_ref[...], g2_ref[...], b2_ref[...]

            def chunk(r0, _):
                rs = pl.ds(r0, SUBLANES)
                x1 = xh1_ref[rs, :] * g1 + b1
                xh2, rstd, x2 = _layer_norm_rows(ALPHA * x1 + acc_s[rs, :], g2, b2)
                diff = x2 - t_ref[rs, :]
                loss_ref[...] += diff * diff
                dx2 = diff * (1.0 / d)
                dg_ref[...] += dx2 * xh2
                db_ref[...] += dx2
                dz = _layer_norm_bwd_rows(dx2, xh2, rstd, g2)
                dz_ref[rs, :] = dz

            _chunk_loop(tm, chunk, unroll=LN_UNROLL)
            dzb_ref[...] = dz_ref[...].astype(BF16)

    row = lambda i, k: (i, 0)
    vec = lambda i, k: (0, 0)
    return pl.pallas_call(
        body, name="mlp_out_ln2", grid=(s // tm, nk),
        in_specs=[pl.BlockSpec((tm, tk), lambda i, k: (i, k)), pl.BlockSpec((tk, d), lambda i, k: (k, 0)),
                  pl.BlockSpec((tm, d), row), pl.BlockSpec((1, d), vec), pl.BlockSpec((1, d), vec),
                  pl.BlockSpec((1, d), vec), pl.BlockSpec((1, d), vec), pl.BlockSpec((tm, d), row)],
        out_specs=[pl.BlockSpec((tm, d), row), pl.BlockSpec((tm, d), row),
                   pl.BlockSpec((SUBLANES, d), vec), pl.BlockSpec((SUBLANES, d), vec), pl.BlockSpec((SUBLANES, d), vec)],
        out_shape=[jax.ShapeDtypeStruct((s, d), F32), jax.ShapeDtypeStruct((s, d), BF16),
                   jax.ShapeDtypeStruct((SUBLANES, d), F32), jax.ShapeDtypeStruct((SUBLANES, d), F32),
                   jax.ShapeDtypeStruct((SUBLANES, d), F32)],
        scratch_shapes=[pltpu.VMEM((tm, d), F32)],
        compiler_params=_params("arbitrary", "arbitrary"),
    )(hsq, w2, xh1, g1, b1, g2, b2, target)


def _half_grad_call(a, b, half, row_sharded, init, name, dep):
    s, m = a.shape
    n = b.shape[1]
    if row_sharded:
        rows, cols = m // (2 * N_CHIPS), n
        tm = min(1024, rows)
        per = rows // tm
        tn = min(1024, cols)
        n_i, n_j = N_CHIPS * per, cols // tn
        a_block = lambda i, h: ((i // per) * 2 + h) * per + i % per
        out_block = lambda i, j: (i // per, i % per, j)
    else:
        rows, cols = m // 2, n // N_CHIPS
        tm = min(1024, rows)
        per = rows // tm
        tn = cols if cols % 1024 else 1024
        per_n = cols // tn
        n_i, n_j = per, N_CHIPS * per_n
        a_block = lambda i, h: h * per + i
        out_block = lambda i, j: (j // per_n, i, j % per_n)
    tk = min(2048, s)
    if row_sharded and tm < 1024 and s * n * 2 <= RESIDENT_OPERAND_BYTES:
        tk, tn, n_j = s, n, 1
    has_init = init is not None

    def body(half_ref, a_ref, b_ref, *rest):
        o_ref = rest[-1]

        @pl.when(pl.program_id(2) == 0)
        def _():
            o_ref[...] = rest[0][...] if has_init else jnp.zeros_like(o_ref)

        o_ref[...] += lax.dot_general(a_ref[...], b_ref[...], (((0,), (0,)), ((), ())), preferred_element_type=F32)

    out_spec = pl.BlockSpec((None, tm, tn), lambda i, j, k, h: out_block(i, j))
    in_specs = [pl.BlockSpec((tk, tm), lambda i, j, k, h: (k, a_block(i, h[0]))),
                pl.BlockSpec((tk, tn), lambda i, j, k, h: (k, j))]
    args = [a, b]
    if has_init:
        in_specs.append(out_spec)
        args.append(init)
    in_specs.append(ANY)
    args.append(dep)
    return pl.pallas_call(
        body, name=name,
        grid_spec=pltpu.PrefetchScalarGridSpec(num_scalar_prefetch=1, grid=(n_i, n_j, s // tk), in_specs=in_specs,
                                               out_specs=out_spec),
        out_shape=jax.ShapeDtypeStruct((N_CHIPS, rows, cols), F32),
        compiler_params=_params("arbitrary", "arbitrary", "arbitrary"),
    )(half, *args)


def _dhsq_call(dzb, w2, r, dep):
    s, d = dzb.shape
    f = w2.shape[0]
    tm = min(MM_TILE, s)
    tn = min(1024, f)

    def body(dz_ref, w_ref, r_ref, dep_ref, o_ref):
        dh = lax.dot_general(dz_ref[...], w_ref[...], (((1,), (1,)), ((), ())), preferred_element_type=F32)
        o_ref[...] = (dh * (2.0 * r_ref[...].astype(F32))).astype(BF16)

    return pl.pallas_call(
        body, name="mlp_dpre", grid=(f // tn, s // tm),
        in_specs=[pl.BlockSpec((tm, d), lambda j, i: (i, 0)), pl.BlockSpec((tn, d), lambda j, i: (j, 0)),
                  pl.BlockSpec((tm, tn), lambda j, i: (i, j)), ANY],
        out_specs=pl.BlockSpec((tm, tn), lambda j, i: (i, j)),
        out_shape=jax.ShapeDtypeStruct((s, f), BF16),
        compiler_params=_params("arbitrary", "arbitrary"),
    )(dzb, w2, r, dep)


def _dx1_ln1_bwd_call(dpre, w1, dz2, xh1, rstd1, g1, dep):
    s, f = dpre.shape
    n, d, f4 = w1.shape
    tm = min(MM_TILE, s)
    tk = min(LN_MM_K, f4)
    per = f4 // tk
    nk = n * per

    def body(dp_ref, w_ref, dz2_ref, xh_ref, rstd_ref, g_ref, dep_ref, dz_ref, dzb_ref, dg_ref, db_ref, acc_s):
        i, k = pl.program_id(0), pl.program_id(1)

        @pl.when((i == 0) & (k == 0))
        def _():
            dg_ref[...] = jnp.zeros_like(dg_ref)
            db_ref[...] = jnp.zeros_like(db_ref)

        @pl.when(k == 0)
        def _():
            acc_s[...] = jnp.zeros_like(acc_s)

        acc_s[...] += lax.dot_general(dp_ref[...], w_ref[...], (((1,), (1,)), ((), ())), preferred_element_type=F32)

        @pl.when(k == nk - 1)
        def _():
            g = g_ref[...]

            def chunk(r0, _):
                rs = pl.ds(r0, SUBLANES)
                dx1 = acc_s[rs, :] + ALPHA * dz2_ref[rs, :]
                xh = xh_ref[rs, :]
                dg_ref[...] += dx1 * xh
                db_ref[...] += dx1
                dz = _layer_norm_bwd_rows(dx1, xh, rstd_ref[rs, :], g)
                dz_ref[rs, :] = dz

            _chunk_loop(tm, chunk, unroll=LN_UNROLL)
            dzb_ref[...] = dz_ref[...].astype(BF16)

    row = lambda i, k: (i, 0)
    vec = lambda i, k: (0, 0)
    return pl.pallas_call(
        body, name="dx1_ln1_bwd", grid=(s // tm, nk),
        in_specs=[pl.BlockSpec((tm, tk), lambda i, k: (i, k)),
                  pl.BlockSpec((None, d, tk), lambda i, k: (k // per, 0, k % per)),
                  pl.BlockSpec((tm, d), row), pl.BlockSpec((tm, d), row), pl.BlockSpec((tm, 1), row),
                  pl.BlockSpec((1, d), vec), ANY],
        out_specs=[pl.BlockSpec((tm, d), row), pl.BlockSpec((tm, d), row),
                   pl.BlockSpec((SUBLANES, d), vec), pl.BlockSpec((SUBLANES, d), vec)],
        out_shape=[jax.ShapeDtypeStruct((s, d), F32), jax.ShapeDtypeStruct((s, d), BF16),
                   jax.ShapeDtypeStruct((SUBLANES, d), F32), jax.ShapeDtypeStruct((SUBLANES, d), F32)],
        scratch_shapes=[pltpu.VMEM((tm, d), F32)],
        compiler_params=_params("arbitrary", "arbitrary"),
    )(dpre, w1, dz2, xh1, rstd1, g1, dep)


def _mixer_bwd_call(dzb, w_out, d_pool, proj, h_f, h_b, w_pool, pool_scale, p, dep):
    s, d = dzb.shape
    c = h_f.shape[1]
    pg = p // N_POOL_GROUPS
    t = min(SEQ_TILE, s)
    n_tiles = s // t

    def body(dz_ref, wo_ref, d_ref, gate_ref, hf_ref, hb_ref, wp_ref, sc_ref, dep_ref,
             e_ref, dh_ref, dgate_ref, dwp_ref, dsc_ref, dd_s, dy_s):
        i = pl.program_id(0)

        @pl.when(i == 0)
        def _():
            dwp_ref[...] = jnp.zeros_like(dwp_ref)
            dsc_ref[...] = jnp.zeros_like(dsc_ref)

        dy_s[...] = lax.dot_general(dz_ref[...], wo_ref[...], (((1,), (1,)), ((), ())), preferred_element_type=F32)

        for g in range(N_POOL_GROUPS):
            cols = pl.ds(g * pg, pg)
            dg = d_ref[:, cols]
            out = jnp.dot(dg, wp_ref[g], preferred_element_type=F32)
            dyp = dy_s[:, cols]
            prod = dyp * out
            dsc_ref[:, cols] += jnp.sum(prod.reshape(t // SUBLANES, SUBLANES, pg), axis=0)
            dout = (dyp * sc_ref[:, cols]).astype(BF16)
            dwp_ref[g] += lax.dot_general(dg, dout, (((0,), (0,)), ((), ())), preferred_element_type=F32)
            dd_s[:, cols] = lax.dot_general(dout, wp_ref[g], (((1,), (1,)), ((), ())), preferred_element_type=F32)

        def chunk(r0, _):
            rs = pl.ds(r0, SUBLANES)
            for g, w in enumerate(POOL_WINDOWS):
                cols = pl.ds(g * pg, pg)
                e_ref[rs, cols] = dd_s[rs, cols] * _window_inverse_counts(r0, i, t, s, w // 2, pg)
            gate, dgate = _gelu_and_grad(gate_ref[rs, :])
            dyr = dy_s[rs, pl.ds(p, c)]
            dh_ref[rs, :] = dyr * gate
            dd_s[rs, :] = dyr * (hf_ref[rs, :] + hb_ref[rs, :]) * dgate

        _chunk_loop(t, chunk)
        dgate_ref[...] = dd_s[...].astype(BF16)

    tile = lambda i: (i, 0)
    return pl.pallas_call(
        body, name="mixer_bwd", grid=(n_tiles,),
        in_specs=[pl.BlockSpec((t, d), tile), pl.BlockSpec((p + c, d), lambda i: (0, 0)), pl.BlockSpec((t, p), tile),
                  pl.BlockSpec((t, c), lambda i: (i, 2)), pl.BlockSpec((t, c), tile), pl.BlockSpec((t, c), tile),
                  pl.BlockSpec((N_POOL_GROUPS, pg, pg), lambda i: (0, 0, 0)), pl.BlockSpec((1, p), lambda i: (0, 0)), ANY],
        out_specs=[pl.BlockSpec((t, p), tile), pl.BlockSpec((t, c), tile), pl.BlockSpec((t, c), tile),
                   pl.BlockSpec((N_POOL_GROUPS, pg, pg), lambda i: (0, 0, 0)), pl.BlockSpec((SUBLANES, p), lambda i: (0, 0))],
        out_shape=[jax.ShapeDtypeStruct((s, p), F32), jax.ShapeDtypeStruct((s, c), F32), jax.ShapeDtypeStruct((s, c), BF16),
                   jax.ShapeDtypeStruct((N_POOL_GROUPS, pg, pg), F32), jax.ShapeDtypeStruct((SUBLANES, p), F32)],
        scratch_shapes=[pltpu.VMEM((t, p), F32), pltpu.VMEM((t, p + c), F32)],
        compiler_params=_params("arbitrary"),
    )(dzb, w_out, d_pool, proj, h_f, h_b, w_pool, pool_scale, dep)


def _scan_bwd_call(xc, dh, h_dir, gates, dxc_prev, wa, wi, lam, reverse, dep):
    s, c = xc.shape
    heads = c // LRU_HEAD
    t = min(SEQ_TILE, s)
    n_tiles = s // t
    per = t // SUBLANES
    last_blk = s // SUBLANES - 1
    tile = (lambda i: (i, 0)) if reverse else (lambda i: (n_tiles - 1 - i, 0))
    if reverse:
        halo = lambda i: (jnp.minimum((i + 1) * per, last_blk), 0)
    else:
        halo = lambda i: (jnp.maximum((n_tiles - 1 - i) * per - 1, 0), 0)
    whole2 = lambda i: (0, 0)
    whole3 = lambda i: (0, 0, 0)
    has_prev = dxc_prev is not None
    n_in = 11 + int(has_prev) + 1

    def body(*refs):
        xc_ref, dh_ref, h_ref, hh_ref, r_ref, ig_ref, a_ref, mult_ref = refs[:8]
        prev_ref = refs[8] if has_prev else None
        wa_ref, wi_ref, lam_ref = refs[n_in - 4:n_in - 1]
        dxc_ref, dwa_ref, dwi_ref, dba_ref, dbi_ref, dsp_ref = refs[n_in:n_in + 6]
        pr_s, pi_s, carry_s = refs[n_in + 6:]
        step = pl.program_id(0)
        tile_idx = step if reverse else n_tiles - 1 - step

        @pl.when(step == 0)
        def _():
            carry_s[...] = jnp.zeros_like(carry_s)
            dwa_ref[...] = jnp.zeros_like(dwa_ref)
            dwi_ref[...] = jnp.zeros_like(dwi_ref)
            dba_ref[...] = jnp.zeros_like(dba_ref)
            dbi_ref[...] = jnp.zeros_like(dbi_ref)
            dsp_ref[...] = jnp.zeros_like(dsp_ref)

        sp = _softplus(-lam_ref[...])
        rows = lax.broadcasted_iota(jnp.int32, (SUBLANES, c), 0)

        def chunk(r0, u_in):
            rs = pl.ds(r0, SUBLANES)
            xcv = xc_ref[rs, :]
            r, ig, a, mult = r_ref[rs, :], ig_ref[rs, :], a_ref[rs, :], mult_ref[rs, :]
            dhv = dh_ref[rs, :]
            u = _scan_chunk(a, a * dhv, u_in, rows, not reverse)
            if reverse:
                gt = dhv + jnp.where(rows >= 1, pltpu.roll(u, 1, 0), u_in)
                u_out = u[SUBLANES - 1:SUBLANES, :]
            else:
                gt = dhv + jnp.where(rows < SUBLANES - 1, pltpu.roll(u, SUBLANES - 1, 0), u_in)
                u_out = u[0:1, :]
            cur = h_ref[rs, :]
            if reverse:
                after = h_ref[pl.ds(pl.multiple_of(jnp.minimum(r0 + SUBLANES, t - SUBLANES), SUBLANES), SUBLANES), :]
                edge = jnp.where(tile_idx == n_tiles - 1, 0.0, hh_ref[...])
                nxt = jnp.where(r0 == t - SUBLANES, edge, after)
                hs = _shift_rows(cur, cur, nxt, 1, rows)
            else:
                before = h_ref[pl.ds(pl.multiple_of(jnp.maximum(r0 - SUBLANES, 0), SUBLANES), SUBLANES), :]
                edge = jnp.where(tile_idx == 0, 0.0, hh_ref[...])
                prv = jnp.where(r0 == 0, edge, before)
                hs = _shift_rows(prv, cur, cur, -1, rows)
            gx = gt * xcv
            dmult = gx * ig
            di = gx * mult
            dlog_a = (gt * hs) * a - dmult * (a * a) / mult
            dr = dlog_a * (-RG_C * sp)
            dsp_ref[...] += dlog_a * (-RG_C * r)
            dpr = dr * r * (1.0 - r)
            dpi = di * ig * (1.0 - ig)
            dba_ref[...] += dpr
            dbi_ref[...] += dpi
            direct = gt * mult * ig
            if has_prev:
                direct = direct + prev_ref[rs, :]
            dxc_ref[rs, :] = direct
            pr_s[rs, :] = dpr
            pi_s[rs, :] = dpi
            return u_out

        carry_s[...] = _chunk_loop(t, chunk, carry_s[...], unroll=2, descending=not reverse)

        for h in range(heads):
            cs = pl.ds(h * LRU_HEAD, LRU_HEAD)
            xb = xc_ref[:, cs].astype(BF16)
            dprb = pr_s[:, cs].astype(BF16)
            dpib = pi_s[:, cs].astype(BF16)
            dwa_ref[h] += lax.dot_general(xb, dprb, (((0,), (0,)), ((), ())), preferred_element_type=F32)
            dwi_ref[h] += lax.dot_general(xb, dpib, (((0,), (0,)), ((), ())), preferred_element_type=F32)
            dxc_ref[:, cs] += (
                lax.dot_general(dprb, wa_ref[h], (((1,), (1,)), ((), ())), preferred_element_type=F32)
                + lax.dot_general(dpib, wi_ref[h], (((1,), (1,)), ((), ())), preferred_element_type=F32))

    tile_spec = pl.BlockSpec((t, c), tile)
    in_specs = [tile_spec, tile_spec, tile_spec, pl.BlockSpec((SUBLANES, c), halo)] + [tile_spec] * 4
    args = [xc, dh, h_dir, h_dir, *gates]
    if has_prev:
        in_specs.append(tile_spec)
        args.append(dxc_prev)
    in_specs += [pl.BlockSpec((heads, LRU_HEAD, LRU_HEAD), whole3), pl.BlockSpec((heads, LRU_HEAD, LRU_HEAD), whole3),
                 pl.BlockSpec((1, c), whole2), ANY]
    args += [wa, wi, lam, dep]
    assert len(args) == n_in
    return pl.pallas_call(
        body, name="scan_bwd_rev" if reverse else "scan_bwd", grid=(n_tiles,),
        in_specs=in_specs,
        out_specs=[tile_spec,
                   pl.BlockSpec((heads, LRU_HEAD, LRU_HEAD), whole3), pl.BlockSpec((heads, LRU_HEAD, LRU_HEAD), whole3),
                   pl.BlockSpec((SUBLANES, c), whole2), pl.BlockSpec((SUBLANES, c), whole2), pl.BlockSpec((SUBLANES, c), whole2)],
        out_shape=[jax.ShapeDtypeStruct((s, c), F32),
                   jax.ShapeDtypeStruct((heads, LRU_HEAD, LRU_HEAD), F32), jax.ShapeDtypeStruct((heads, LRU_HEAD, LRU_HEAD), F32),
                   jax.ShapeDtypeStruct((SUBLANES, c), F32), jax.ShapeDtypeStruct((SUBLANES, c), F32),
                   jax.ShapeDtypeStruct((SUBLANES, c), F32)],
        scratch_shapes=[pltpu.VMEM((t, c), F32), pltpu.VMEM((t, c), F32), pltpu.VMEM((1, c), F32)],
        compiler_params=_params("arbitrary"),
    )(*args)


def _dproj_call(e_pool, dxc, proj, dgate, conv_w, p):
    s, c = dxc.shape
    pg = p // N_POOL_GROUPS
    t = min(SEQ_TILE, s)
    n_tiles = s // t

    def body(e_ref, ep_ref, en_ref, dx_ref, dxp_ref, dxn_ref, u_ref, up_ref, un_ref, dgate_ref, w_ref,
             dproj_ref, dcw_ref, dcb_ref, st_s):
        i = pl.program_id(0)
        first, last = i == 0, i == n_tiles - 1

        @pl.when(first)
        def _():
            dcw_ref[...] = jnp.zeros_like(dcw_ref)
            dcb_ref[...] = jnp.zeros_like(dcb_ref)

        rows_p = lax.broadcasted_iota(jnp.int32, (SUBLANES, pg), 0)
        rows_c = lax.broadcasted_iota(jnp.int32, (SUBLANES, c), 0)
        w = w_ref[...]

        def chunk(r0, _):
            rs = pl.ds(r0, SUBLANES)
            for g, win in enumerate(POOL_WINDOWS):
                cols = pl.ds(g * pg, pg)
                prv, cur, nxt = _neighbour_chunks(e_ref, ep_ref, en_ref, r0, t, cols, first, last)
                tot = cur
                for o in range(-(win // 2) + 1, win // 2 + 1):
                    if o != 0:
                        tot = tot + _shift_rows(prv, cur, nxt, o, rows_p)
                cnt = _window_counts(r0, i, t, s, win // 2, (SUBLANES, pg))
                st_s[rs, cols] = tot - cur * cnt
            prv, cur, nxt = _neighbour_chunks(dx_ref, dxp_ref, dxn_ref, r0, t, slice(None), first, last)
            du = w[1:2] * cur
            du += w[0:1] * _shift_rows(prv, cur, nxt, 1, rows_c)
            du += w[2:3] * _shift_rows(prv, cur, nxt, -1, rows_c)
            du += w[3:4] * _shift_rows(prv, cur, nxt, -2, rows_c)
            st_s[rs, pl.ds(p, c)] = du
            uprv, ucur, unxt = _neighbour_chunks(u_ref, up_ref, un_ref, r0, t, slice(None), first, last)
            dcb_ref[...] += cur
            for j, o in enumerate((-1, 0, 1, 2)):
                dcw_ref[j] += cur * _shift_rows(uprv, ucur, unxt, o, rows_c)

        _chunk_loop(t, chunk)
        dproj_ref[:, pl.ds(0, p + c)] = st_s[...].astype(BF16)
        dproj_ref[:, pl.ds(p + c, c)] = dgate_ref[...]

    return pl.pallas_call(
        body, name="dproj", grid=(n_tiles,),
        in_specs=_halo_specs(t, s, p, 0) + _halo_specs(t, s, c, 0) + _halo_specs(t, s, c, 1) + [
            pl.BlockSpec((t, c), lambda i: (i, 0)), pl.BlockSpec((4, c), lambda i: (0, 0))],
        out_specs=[pl.BlockSpec((t, p + 2 * c), lambda i: (i, 0)),
                   pl.BlockSpec((4, SUBLANES, c), lambda i: (0, 0, 0)), pl.BlockSpec((SUBLANES, c), lambda i: (0, 0))],
        out_shape=[jax.ShapeDtypeStruct((s, p + 2 * c), BF16), jax.ShapeDtypeStruct((4, SUBLANES, c), F32),
                   jax.ShapeDtypeStruct((SUBLANES, c), F32)],
        scratch_shapes=[pltpu.VMEM((t, p + c), F32)],
        compiler_params=_params("arbitrary"),
    )(e_pool, e_pool, e_pool, dxc, dxc, dxc, proj, proj, proj, dgate, conv_w)


def _dx_call(dproj, w_in, dz1, dep):
    s, e = dproj.shape
    n, d, e4 = w_in.shape
    tm = min(MM_TILE, s)

    def body(dp_ref, w_hbm, dz_ref, dep_ref, o_ref, w_s, sems):
        @pl.when(pl.program_id(0) == 0)
        def _():
            copies = [pltpu.make_async_copy(w_hbm.at[k], w_s.at[:, pl.ds(k * e4, e4)], sems.at[k]) for k in range(n)]
            for cp in copies:
                cp.start()
            for cp in copies:
                cp.wait()

        o_ref[...] = ALPHA * dz_ref[...] + lax.dot_general(
            dp_ref[...], w_s[...], (((1,), (1,)), ((), ())), preferred_element_type=F32)

    return pl.pallas_call(
        body, name="grad_x", grid=(s // tm,),
        in_specs=[pl.BlockSpec((tm, e), lambda i: (i, 0)), ANY, pl.BlockSpec((tm, d), lambda i: (i, 0)), ANY],
        out_specs=pl.BlockSpec((tm, d), lambda i: (i, 0)),
        out_shape=jax.ShapeDtypeStruct((s, d), F32),
        scratch_shapes=[pltpu.VMEM((d, e), BF16), pltpu.SemaphoreType.DMA((n,))],
        compiler_params=_params("arbitrary"),
    )(dproj, w_in, dz1, dep)


def _row_tile(rows, cols, n_arrays):
    limit = max(SUBLANES, ELT_BLOCK_BYTES // (4 * cols * max(1, n_arrays // 4)))
    best = SUBLANES
    for cand in range(SUBLANES, min(rows, limit) + 1, SUBLANES):
        if rows % cand == 0:
            best = cand
    return best if rows % SUBLANES == 0 else rows


def _cast_to_slot_call(a, idx, dtype, name, dep):
    rows, cols = a.shape
    tr = _row_tile(rows, cols, 2)
    extra = [] if dep is None else [dep]

    def body(idx_ref, a_ref, *rest):
        rest[-1][...] = a_ref[...].astype(dtype)

    return pl.pallas_call(
        body, name=name,
        grid_spec=pltpu.PrefetchScalarGridSpec(
            num_scalar_prefetch=1, grid=(rows // tr,),
            in_specs=[pl.BlockSpec((tr, cols), lambda i, idx_ref: (i, 0))] + [ANY] * len(extra),
            out_specs=pl.BlockSpec((None, tr, cols), lambda i, idx_ref: (idx_ref[1], i, 0))),
        out_shape=jax.ShapeDtypeStruct((N_CHIPS, rows, cols), dtype),
        compiler_params=_params("arbitrary"),
    )(idx, a, *extra)


def _add_half_call(g, recv, idx, name):
    _, rows, cols = g.shape
    tr = _row_tile(rows, cols, 3)

    def body(idx_ref, g_ref, r_ref, o_ref):
        o_ref[...] = g_ref[...] + r_ref[...]

    return pl.pallas_call(
        body, name=name,
        grid_spec=pltpu.PrefetchScalarGridSpec(
            num_scalar_prefetch=1, grid=(rows // tr,),
            in_specs=[pl.BlockSpec((None, tr, cols), lambda i, idx_ref: (idx_ref[0], i, 0)),
                      pl.BlockSpec((tr, cols), lambda i, idx_ref: (i, 0))],
            out_specs=pl.BlockSpec((None, tr, cols), lambda i, idx_ref: (idx_ref[1], i, 0))),
        out_shape=jax.ShapeDtypeStruct((N_CHIPS, rows, cols), F32),
        compiler_params=_params("arbitrary"),
    )(idx, g, recv)


def _sum_chips_call(own, recv, idx, name):
    _, rows, cols = recv.shape
    tr = _row_tile(rows, cols, 5)
    out_spec = pl.BlockSpec((None, tr, cols), lambda i, idx_ref: (idx_ref[0], i, 0))
    if own is None:
        def body(idx_ref, r_ref, o_ref):
            o_ref[...] = ((r_ref[0] + r_ref[1]) + r_ref[2]) + r_ref[3]
        in_specs = [pl.BlockSpec((N_CHIPS, tr, cols), lambda i, idx_ref: (0, i, 0))]
        args = (recv,)
    else:
        def body(idx_ref, p_ref, r_ref, o_ref):
            o_ref[...] = ((p_ref[...] + r_ref[0]) + r_ref[1]) + r_ref[2]
        in_specs = [pl.BlockSpec((None, tr, cols), lambda i, idx_ref: (idx_ref[1], i, 0)),
                    pl.BlockSpec((N_CHIPS - 1, tr, cols), lambda i, idx_ref: (0, i, 0))]
        args = (own, recv)
    return pl.pallas_call(
        body, name=name,
        grid_spec=pltpu.PrefetchScalarGridSpec(num_scalar_prefetch=1, grid=(rows // tr,), in_specs=in_specs, out_specs=out_spec),
        out_shape=jax.ShapeDtypeStruct((2, rows, cols), F32),
        compiler_params=_params("arbitrary"),
    )(idx, *args)


def _adamw_math(g, w, m, v):
    mn = ADAM_B1 * m + (1.0 - ADAM_B1) * g
    vn = ADAM_B2 * v + (1.0 - ADAM_B2) * (g * g)
    m_hat = mn / (1.0 - ADAM_B1 ** ADAM_STEP)
    v_hat = vn / (1.0 - ADAM_B2 ** ADAM_STEP)
    return -ADAM_LR * (m_hat / (jnp.sqrt(v_hat) + ADAM_EPS) + ADAM_WD * w), mn, vn


def _adamw_call(g, w, m, v, name):
    rows, cols = w.shape
    tr = _row_tile(rows, cols, 4)

    def body(g_ref, w_ref, m_ref, v_ref, go_ref, d_ref, mo_ref, vo_ref):
        gv = g_ref[...]
        go_ref[...] = gv
        d_ref[...], mo_ref[...], vo_ref[...] = _adamw_math(gv, w_ref[...], m_ref[...], v_ref[...])

    spec = pl.BlockSpec((tr, cols), lambda i: (i, 0))
    shape = jax.ShapeDtypeStruct((rows, cols), F32)
    return pl.pallas_call(
        body, name=name, grid=(rows // tr,),
        in_specs=[spec] * 4, out_specs=[spec] * 4, out_shape=[shape] * 4,
        compiler_params=_params("arbitrary"),
    )(g, w, m, v)


def _adamw_small_call(gs, ws, ms, vs):
    n = len(ws)

    def body(*refs):
        ins = [refs[k * n:(k + 1) * n] for k in range(4)]
        outs = [refs[(4 + k) * n:(5 + k) * n] for k in range(3)]
        for a in range(n):
            outs[0][a][...], outs[1][a][...], outs[2][a][...] = _adamw_math(*[ins[k][a][...] for k in range(4)])

    whole = pl.BlockSpec(memory_space=pltpu.VMEM)
    res = pl.pallas_call(
        body, name="adamw_small",
        in_specs=[whole] * (4 * n), out_specs=[whole] * (3 * n),
        out_shape=[jax.ShapeDtypeStruct(w.shape, F32) for w in ws] * 3,
        compiler_params=pltpu.CompilerParams(vmem_limit_bytes=VMEM_LIMIT),
    )(*gs, *ws, *ms, *vs)
    return res[:n], res[n:2 * n], res[2 * n:]


def _mesh_place():
    x, y, c = lax.axis_index("x"), lax.axis_index("y"), lax.axis_index("c")
    chips = [(1 - x, y), (x, 1 - y), (1 - x, 1 - y)]
    return x, y, c, chips


def _remote(src, dst, send_sems, recv_sems, idx, device):
    return pltpu.make_async_remote_copy(src_ref=src, dst_ref=dst, send_sem=send_sems.at[idx], recv_sem=recv_sems.at[idx],
                                        device_id=device, device_id_type=MESH)


HBM_SPEC = pl.BlockSpec(memory_space=pltpu.HBM)
SEM_SPEC = pl.BlockSpec(memory_space=pltpu.SEMAPHORE)
ORDERED_EFFECT = pltpu.SideEffectType.DATAFLOW_SIDE_EFFECTING


def _in_hbm(a):
    return pltpu.with_memory_space_constraint(a, pltpu.HBM)


def _start_copies_call(name, bufs, groups, after=None):
    n, g = len(bufs), len(groups)
    extra = [] if after is None else [after]
    first_out = n + len(extra)

    def body(*refs):
        outs = refs[first_out:first_out + n]
        sems = refs[first_out + n:first_out + n + 2 * g]
        token = refs[first_out + n + 2 * g]
        for i, (which, copies_fn, _) in enumerate(groups):
            for mine, _ in copies_fn([outs[w] for w in which], sems[2 * i], sems[2 * i + 1]):
                mine.start()
        token[...] = jnp.zeros_like(token)

    sem_shapes = [pltpu.SemaphoreType.DMA((cnt,)) for _, _, cnt in groups for _ in range(2)]
    res = pl.pallas_call(
        body, name=name,
        in_specs=[HBM_SPEC] * n + [ANY] * len(extra),
        out_specs=[HBM_SPEC] * n + [SEM_SPEC] * (2 * g) + [pl.BlockSpec(memory_space=pltpu.VMEM)],
        out_shape=[pltpu.HBM(a.shape, a.dtype) for a in bufs] + sem_shapes + [jax.ShapeDtypeStruct((SUBLANES, LANES), F32)],
        input_output_aliases={a: a for a in range(n)},
        compiler_params=pltpu.CompilerParams(has_side_effects=ORDERED_EFFECT),
    )(*[_in_hbm(a) for a in bufs], *extra)
    sems = res[n:n + 2 * g]
    return list(res[:n]), [(sems[2 * i], sems[2 * i + 1]) for i in range(g)], res[n + 2 * g]


def _wait_copies_call(name, bufs, sems, copies_fn, after):
    n = len(bufs)

    def body(*refs):
        ins = refs[:n]
        send_sems, recv_sems = refs[n], refs[n + 1]
        for mine, arriving in copies_fn(list(ins), send_sems, recv_sems):
            arriving.wait_recv()
            mine.wait_send()

    res = pl.pallas_call(
        body, name=name,
        in_specs=[HBM_SPEC] * n + [SEM_SPEC, SEM_SPEC, ANY],
        out_specs=[HBM_SPEC] * n,
        out_shape=[pltpu.HBM(a.shape, a.dtype) for a in bufs],
        input_output_aliases={a: a for a in range(n)},
        compiler_params=pltpu.CompilerParams(has_side_effects=ORDERED_EFFECT),
    )(*bufs, sems[0], sems[1], after)
    return list(res)


def _gather_copies(bufs, send_sems, recv_sems):
    x, y, c, chips = _mesh_place()
    k = 2 * x + y
    out = []
    for a, buf in enumerate(bufs):
        for j, (px, py) in enumerate(chips):
            kj = 2 * px + py
            mine = _remote(buf.at[k, c], buf.at[k, c], send_sems, recv_sems, 3 * a + j, (px, py, c))
            arriving = _remote(buf.at[k, c], buf.at[kj, c], send_sems, recv_sems, 3 * a + j, (px, py, c))
            out.append((mine, arriving))
    return out


def _exchange_copies(n_sharded, n_replicated):
    def copies(bufs, send_sems, recv_sems):
        x, y, c, chips = _mesh_place()
        k = 2 * x + y
        sums, lands = bufs[:n_sharded], bufs[n_sharded:2 * n_sharded]
        repl = bufs[2 * n_sharded:]
        out = []
        for j, (px, py) in enumerate(chips):
            kj = 2 * px + py
            for a in range(n_sharded):
                cp = _remote(sums[a].at[kj], lands[a].at[j], send_sems, recv_sems, 3 * a + j, (px, py, c))
                out.append((cp, cp))
            for a in range(n_replicated):
                idx = 3 * (n_sharded + a) + j
                mine = _remote(repl[a].at[k], repl[a].at[k], send_sems, recv_sems, idx, (px, py, c))
                arriving = _remote(repl[a].at[k], repl[a].at[kj], send_sems, recv_sems, idx, (px, py, c))
                out.append((mine, arriving))
        return out
    return copies


def _sibling_copies(n, halves):
    def copies(bufs, send_sems, recv_sems):
        x, y, c, _ = _mesh_place()
        out = []
        for a in range(n):
            src = bufs[a].at[1 - c] if halves else bufs[a]
            cp = _remote(src, bufs[n + a], send_sems, recv_sems, a, (x, y, 1 - c))
            out.append((cp, cp))
        return out
    return copies


def _forward_copies(bufs, send_sems, recv_sems):
    x, y, c, chips = _mesh_place()
    out = []
    for a, buf in enumerate(bufs):
        for j, (px, py) in enumerate(chips):
            kj = 2 * px + py
            mine = _remote(buf.at[kj, c], buf.at[kj, c], send_sems, recv_sems, 3 * a + j, (x, y, 1 - c))
            arriving = _remote(buf.at[kj, c], buf.at[kj, 1 - c], send_sems, recv_sems, 3 * a + j, (x, y, 1 - c))
            out.append((mine, arriving))
    return out


def _join_copies(bufs, send_sems, recv_sems):
    x, y, c, _ = _mesh_place()
    out = []
    for a, buf in enumerate(bufs):
        mine = _remote(buf.at[c], buf.at[c], send_sems, recv_sems, a, (x, y, 1 - c))
        arriving = _remote(buf.at[c], buf.at[1 - c], send_sems, recv_sems, a, (x, y, 1 - c))
        out.append((mine, arriving))
    return out


def _forward_to_sibling_call(bufs, name):
    n = len(bufs)

    def body(*refs):
        ins, outs = refs[:n], refs[n:2 * n]
        send_sems, recv_sems = refs[2 * n:]
        x, y, c, chips = _mesh_place()
        sibling = (x, y, 1 - c)
        sends = []
        for a in range(n):
            for j, (px, py) in enumerate(chips):
                kj = 2 * px + py
                sends.append(_remote(ins[a].at[kj, c], outs[a].at[kj, c], send_sems, recv_sems, 3 * a + j, sibling))
        for cp in sends:
            cp.start()
        for a in range(n):
            for j, (px, py) in enumerate(chips):
                kj = 2 * px + py
                _remote(ins[a].at[kj, c], outs[a].at[kj, 1 - c], send_sems, recv_sems, 3 * a + j, sibling).wait_recv()
        for cp in sends:
            cp.wait_send()

    return pl.pallas_call(
        body, name=name,
        in_specs=[ANY] * n, out_specs=[ANY] * n,
        out_shape=[jax.ShapeDtypeStruct(a.shape, a.dtype) for a in bufs],
        input_output_aliases={a: a for a in range(n)},
        scratch_shapes=[pltpu.SemaphoreType.DMA((3 * n,)), pltpu.SemaphoreType.DMA((3 * n,))],
    )(*bufs)


def _pack(arrays, rows_multiple):
    flat = jnp.concatenate([a.reshape(-1) for a in arrays])
    per = LANES * rows_multiple
    padded = -(-flat.shape[0] // per) * per
    flat = jnp.pad(flat, (0, padded - flat.shape[0]))
    return flat.reshape(-1, LANES)


def _unpack(packed, shapes):
    flat = packed.reshape(-1)
    out, at = [], 0
    for shp in shapes:
        size = 1
        for dim in shp:
            size *= dim
        out.append(flat[at:at + size].reshape(shp))
        at += size
    return out


def _halves(a):
    return a.reshape((2, a.shape[0] // 2) + a.shape[1:])


def kernel(x, ln_mix_g, ln_mix_b, w_in, w_pool, pool_scale, conv_w, conv_b, w_rg_a, b_rg_a, w_rg_i, b_rg_i, rg_lambda, w_out, ln_ffn_g, ln_ffn_b, w_mlp_in, w_mlp_out, loss_target, m_ln_mix_g, m_ln_mix_b, m_w_in, m_w_pool, m_pool_scale, m_conv_w, m_conv_b, m_w_rg_a, m_b_rg_a, m_w_rg_i, m_b_rg_i, m_rg_lambda, m_w_out, m_ln_ffn_g, m_ln_ffn_b, m_w_mlp_in, m_w_mlp_out, v_ln_mix_g, v_ln_mix_b, v_w_in, v_w_pool, v_pool_scale, v_conv_w, v_conv_b, v_w_rg_a, v_b_rg_a, v_w_rg_i, v_b_rg_i, v_rg_lambda, v_w_out, v_ln_ffn_g, v_ln_ffn_b, v_w_mlp_in, v_w_mlp_out):
    weights = dict(ln_mix_g=ln_mix_g, ln_mix_b=ln_mix_b, w_in=w_in, w_pool=w_pool, pool_scale=pool_scale, conv_w=conv_w,
                   conv_b=conv_b, w_rg_a=w_rg_a, b_rg_a=b_rg_a, w_rg_i=w_rg_i, b_rg_i=b_rg_i, rg_lambda=rg_lambda,
                   w_out=w_out, ln_ffn_g=ln_ffn_g, ln_ffn_b=ln_ffn_b, w_mlp_in=w_mlp_in, w_mlp_out=w_mlp_out)
    m_in = dict(ln_mix_g=m_ln_mix_g, ln_mix_b=m_ln_mix_b, w_in=m_w_in, w_pool=m_w_pool, pool_scale=m_pool_scale,
                conv_w=m_conv_w, conv_b=m_conv_b, w_rg_a=m_w_rg_a, b_rg_a=m_b_rg_a, w_rg_i=m_w_rg_i, b_rg_i=m_b_rg_i,
                rg_lambda=m_rg_lambda, w_out=m_w_out, ln_ffn_g=m_ln_ffn_g, ln_ffn_b=m_ln_ffn_b, w_mlp_in=m_w_mlp_in,
                w_mlp_out=m_w_mlp_out)
    v_in = dict(ln_mix_g=v_ln_mix_g, ln_mix_b=v_ln_mix_b, w_in=v_w_in, w_pool=v_w_pool, pool_scale=v_pool_scale,
                conv_w=v_conv_w, conv_b=v_conv_b, w_rg_a=v_w_rg_a, b_rg_a=v_b_rg_a, w_rg_i=v_w_rg_i, b_rg_i=v_b_rg_i,
                rg_lambda=v_rg_lambda, w_out=v_w_out, ln_ffn_g=v_ln_ffn_g, ln_ffn_b=v_ln_ffn_b, w_mlp_in=v_w_mlp_in,
                w_mlp_out=v_w_mlp_out)
    names = list(weights)

    xs = x[0]
    tgt = loss_target[0]
    s, d = xs.shape
    p = c = d // 2
    pg = p // N_POOL_GROUPS
    core = lax.axis_index("c")
    shard = 2 * lax.axis_index("x") + lax.axis_index("y")

    idx = jnp.stack([core, shard]).astype(jnp.int32)
    small_shard = _pack([conv_w[0], b_rg_a[0], b_rg_i[0], rg_lambda[0]], 2 * SUBLANES)
    to_gather = [(w_in[0], BF16), (w_out[0], BF16), (w_mlp_in[0], BF16), (w_mlp_out[0], BF16),
                 (w_pool[0].reshape(-1, pg), BF16), (small_shard, F32)]

    def slot_view(i, dep):
        a, dt = to_gather[i]
        sl = _cast_to_slot_call(a, idx, dt, f"gather_slot_{i}", dep)
        return sl.reshape(N_CHIPS, 2, sl.shape[1] // 2, sl.shape[2])

    first, later = (0, 4, 5), (1, 2, 3)
    fly_a, sems_a, token_a = _start_copies_call(
        "gather_start_first", [slot_view(i, None) for i in first], [((0, 1, 2), _gather_copies, 3 * len(first))])
    later_views = []
    for i in later:
        later_views.append(slot_view(i, later_views[-1] if later_views else token_a))
    xb = _cast_call(xs, later_views[-1])
    got_first = _wait_copies_call("gather_wait_w_in", fly_a, sems_a[0], _gather_copies, xb)
    fly_b, sems_b, g_token = _start_copies_call(
        "gather_start_later", later_views + got_first,
        [((0,), _gather_copies, 3), ((1,), _gather_copies, 3), ((2,), _gather_copies, 3)])
    got_first = fly_b[len(later):]
    in_flight = dict(zip(later, fly_b))
    g_sems = [None] + list(sems_b)

    def arrive(which, group, after, tag):
        return _wait_copies_call(f"gather_wait_{tag}", [in_flight[w] for w in which], g_sems[group], _gather_copies, after)

    def pass_on(got, tag):
        flying, sems, token = _start_copies_call(
            f"gather_forward_start_{tag}", got, [(tuple(range(len(got))), _forward_copies, 3 * len(got))])
        return (flying, sems[0], tag), token

    def passed_on(state, after):
        flying, sems, tag = state
        return _wait_copies_call(f"gather_forward_wait_{tag}", flying, sems, _forward_copies, after)

    gathered = [None] * len(to_gather)
    gathered[0], gathered[4], gathered[5] = _forward_to_sibling_call(got_first, "gather_forward_w_in")
    w_in_f = gathered[0].reshape((N_CHIPS,) + w_in.shape[1:])
    w_pool_f = gathered[4].reshape(N_CHIPS, N_POOL_GROUPS, pg // N_CHIPS, pg).transpose(1, 0, 2, 3).reshape(N_POOL_GROUPS, pg, pg)
    c4 = c // N_CHIPS
    small_parts = [_unpack(gathered[5][k].reshape(-1, LANES), [(4, c4), (2, c4), (2, c4), (2, c4)]) for k in range(N_CHIPS)]
    conv_w_f = jnp.concatenate([sp_[0] for sp_ in small_parts], axis=1)
    b_a_f = jnp.concatenate([sp_[1] for sp_ in small_parts], axis=1)
    b_i_f = jnp.concatenate([sp_[2] for sp_ in small_parts], axis=1)
    lam_f = jnp.concatenate([sp_[3] for sp_ in small_parts], axis=1)
    wa_b = w_rg_a[0].astype(BF16)
    wi_b = w_rg_i[0].astype(BF16)

    proj = _proj_call(xb, w_in_f)
    xc = _conv_call(proj, conv_w_f, conv_b, c)
    fwd_w_out, token = pass_on(arrive((1,), 1, xc, "w_out"), "w_out")
    h_b, *gates_b = _scan_fwd_call(xc, wa_b[1], wi_b[1], b_a_f[1:2], b_i_f[1:2], lam_f[1:2], True, token)
    h_f, *gates_f = _scan_fwd_call(xc, wa_b[0], wi_b[0], b_a_f[0:1], b_i_f[0:1], lam_f[0:1], False, token)
    y, d_pool = _pool_combine_call(proj, h_f, h_b, w_pool_f, pool_scale, p)
    w_out_f = passed_on(fwd_w_out, y)[0].reshape(d, d)
    fwd_w1, token = pass_on(arrive((2,), 2, y, "w_mlp_in"), "w_mlp_in")
    xh1, x1b, rstd1 = _out_ln1_call(y, w_out_f, xs, ln_mix_g, ln_mix_b, token)
    w1_f = passed_on(fwd_w1, x1b)[0].reshape((N_CHIPS,) + w_mlp_in.shape[1:])
    first_half = _mlp_in_call(x1b, w1_f, g_token, None)
    fwd_w2, token = pass_on(arrive((3,), 3, first_half[0], "w_mlp_out"), "w_mlp_out")
    r_act, hsq = _mlp_in_call(x1b, w1_f, token, first_half)
    w2_f = passed_on(fwd_w2, hsq)[0].reshape(N_CHIPS * w_mlp_out.shape[1], d)
    dz2, dz2b, loss8, dg2, db2 = _mlp_out_ln2_call(hsq, w2_f, xh1, ln_mix_g, ln_mix_b, ln_ffn_g, ln_ffn_b, tgt)

    def start_siblings(grads, halves, tag, after=None):
        lands = [lax.empty(g.shape[1:] if halves else g.shape, g.dtype) for g in grads]
        copies = _sibling_copies(len(grads), halves)
        flying, sems, token = _start_copies_call(
            f"siblings_start_{tag}", list(grads) + lands, [(tuple(range(2 * len(grads))), copies, len(grads))], after)
        return (flying, sems[0], copies, len(grads), tag), token

    def finish_siblings(state, after):
        flying, sems, copies, n, tag = state
        got = _wait_copies_call(f"siblings_wait_{tag}", flying, sems, copies, after)
        return got[:n], got[n:]

    half_own = jnp.reshape(core, (1,)).astype(jnp.int32)
    half_sibling = 1 - half_own

    def chip_sum_of(a, b, row_sharded, tag, dep, overlapped):
        for_sibling = _half_grad_call(a, b, half_sibling, row_sharded, None, f"grad_{tag}_for_sibling", dep)
        state, token = start_siblings([for_sibling], False, tag)
        results = overlapped(token)
        _, (from_sibling,) = finish_siblings(state, results[0])
        return _half_grad_call(a, b, half_own, row_sharded, from_sibling, f"grad_{tag}", token), results

    def start_exchange(sums, n_repl, tag):
        n_sh = len(sums) - n_repl
        lands = [lax.empty((N_CHIPS - 1,) + a.shape[1:], a.dtype) for a in sums[:n_sh]]
        bufs = sums[:n_sh] + lands + sums[n_sh:]
        copies = _exchange_copies(n_sh, n_repl)
        flying, sems, token = _start_copies_call(
            f"reduce_start_{tag}", bufs, [(tuple(range(len(bufs))), copies, 3 * len(sums))])
        return (flying, sems[0], copies, n_sh, tag), token

    def finish_exchange(state, after):
        flying, sems, copies, n_sh, tag = state
        got = _wait_copies_call(f"reduce_wait_{tag}", flying, sems, copies, after)
        halves = []
        for a in range(n_sh):
            own, land = got[a], got[n_sh + a]
            cols = own.shape[-1]
            total = _sum_chips_call(own.reshape(N_CHIPS, -1, cols), land.reshape(N_CHIPS - 1, -1, cols), idx,
                                    f"reduce_sum_{tag}_{a}")
            halves.append(total.reshape((2,) + own.shape[1:]))
        for a, rp in enumerate(got[2 * n_sh:]):
            halves.append(_sum_chips_call(None, rp, idx, f"reduce_sum_{tag}_r{a}"))
        return halves

    def start_join(halves, tag):
        flying, sems, token = _start_copies_call(
            f"join_start_{tag}", halves, [(tuple(range(len(halves))), _join_copies, len(halves))])
        return (flying, sems[0], tag), token

    def finish_join(state, after):
        flying, sems, tag = state
        return _wait_copies_call(f"join_wait_{tag}", flying, sems, _join_copies, after)

    sum_w2, (dpre,) = chip_sum_of(hsq, dz2b, True, "w_mlp_out", g_token,
                                  lambda tok: (_dhsq_call(dz2b, w2_f, r_act, tok),))
    flying_w2, token = start_exchange([sum_w2], 0, "w2")
    sum_w1, (dz1, dz1b, dg1, db1) = chip_sum_of(
        x1b, dpre, False, "w_mlp_in", token,
        lambda tok: _dx1_ln1_bwd_call(dpre, w1_f, dz2, xh1, rstd1, ln_mix_g, tok))
    flying_w1, token = start_exchange([sum_w1], 0, "w1")

    sum_wout, (e_pool, dh, dgate, g_wpool, g_pscale8) = chip_sum_of(
        y, dz1b, True, "w_out", token,
        lambda tok: _mixer_bwd_call(dz1b, w_out_f, d_pool, proj, h_f, h_b, w_pool_f, pool_scale, p, tok))
    flying_wout, token = start_exchange([sum_wout], 0, "w_out")
    dxc0, g_wa0, g_wi0, g_ba0, g_bi0, g_sp0 = _scan_bwd_call(
        xc, dh, h_f, gates_f, None, wa_b[0], wi_b[0], lam_f[0:1], False, token)
    dxc, g_wa1, g_wi1, g_ba1, g_bi1, g_sp1 = _scan_bwd_call(
        xc, dh, h_b, gates_b, dxc0, wa_b[1], wi_b[1], lam_f[1:2], True, token)
    dproj, g_cw8, g_cb8 = _dproj_call(e_pool, dxc, proj, dgate, conv_w_f, p)

    rowsum = lambda a8: jnp.sum(a8, axis=-2)
    g_lam = jnp.stack([rowsum(g_sp0), rowsum(g_sp1)]) * (-_sigmoid(-lam_f))
    small_grads = {
        "ln_mix_g": rowsum(dg1), "ln_mix_b": rowsum(db1), "ln_ffn_g": rowsum(dg2), "ln_ffn_b": rowsum(db2),
        "pool_scale": rowsum(g_pscale8), "conv_b": rowsum(g_cb8),
        "w_rg_a": jnp.stack([g_wa0, g_wa1]), "w_rg_i": jnp.stack([g_wi0, g_wi1]),
        "w_pool": g_wpool, "conv_w": rowsum(g_cw8),
        "b_rg_a": jnp.stack([rowsum(g_ba0), rowsum(g_ba1)]), "b_rg_i": jnp.stack([rowsum(g_bi0), rowsum(g_bi1)]),
        "rg_lambda": g_lam,
    }
    small_names = list(small_grads)
    small_shapes = [small_grads[nm].shape for nm in small_names]
    loss_share = jnp.reshape(jnp.sum(loss8) * (0.5 / d), (1,))
    g_small = _halves(_pack([small_grads[nm] for nm in small_names] + [loss_share], 2 * SUBLANES))
    sib_small, token = start_siblings([g_small], True, "small")
    flying_small = []

    def small_exchange_and_grad_x(tok):
        (mine,), (theirs,) = finish_siblings(sib_small, tok)
        small_sum = _add_half_call(mine, theirs, idx, "reduce_add_small")
        state, tok = start_exchange([small_sum], 1, "small")
        flying_small.append(state)
        return (_dx_call(dproj, w_in_f, dz1, tok),)

    sum_win, (grad_x,) = chip_sum_of(xb, dproj, False, "w_in", token, small_exchange_and_grad_x)
    flying_small = flying_small[0]
    flying_win, token = start_exchange([sum_win], 0, "w_in")

    grad_w, delta_w, new_m, new_v = {}, {}, {}, {}

    def adamw(nm, full):
        w2d = weights[nm][0]
        g2d = full.reshape(w2d.shape)
        go, dl, mn, vn = _adamw_call(g2d, w2d, m_in[nm][0], v_in[nm][0], f"adamw_{nm}")
        grad_w[nm], delta_w[nm], new_m[nm], new_v[nm] = go[None], dl[None], mn[None], vn[None]
        return vn

    join_w2, token = start_join(finish_exchange(flying_w2, token), "w2")
    join_w1, token = start_join(finish_exchange(flying_w1, token), "w1")
    join_wout, token = start_join(finish_exchange(flying_wout, token), "w_out")
    last = adamw("w_mlp_out", finish_join(join_w2, token)[0])
    last = adamw("w_mlp_in", finish_join(join_w1, last)[0])
    last = adamw("w_out", finish_join(join_wout, last)[0])

    join_small, token = start_join(finish_exchange(flying_small, last), "small")
    join_win, token = start_join(finish_exchange(flying_win, token), "w_in")
    small_joined = finish_join(join_small, token)[0]
    *small_sums, loss_sum = _unpack(small_joined.reshape(-1, LANES), small_shapes + [(1,)])
    small_full = dict(zip(small_names, small_sums))
    local = dict(small_full)
    local["w_pool"] = lax.dynamic_slice_in_dim(small_full["w_pool"], shard * (pg // N_CHIPS), pg // N_CHIPS, axis=1)
    for nm in ("conv_w", "b_rg_a", "b_rg_i", "rg_lambda"):
        local[nm] = lax.dynamic_slice_in_dim(small_full[nm], shard * c4, c4, axis=1)
    small_g = [local[nm].reshape(weights[nm].shape) for nm in small_names]
    small_d, small_m, small_v = _adamw_small_call(
        small_g, [weights[nm] for nm in small_names], [m_in[nm] for nm in small_names], [v_in[nm] for nm in small_names])
    for nm, gl, dl, mn, vn in zip(small_names, small_g, small_d, small_m, small_v):
        grad_w[nm], delta_w[nm], new_m[nm], new_v[nm] = gl, dl, mn, vn
    adamw("w_in", finish_join(join_win, small_v[0])[0])

    loss = loss_sum[0]
    return (loss, grad_x[None], *[grad_w[nm] for nm in names], *[delta_w[nm] for nm in names],
            *[new_m[nm] for nm in names], *[new_v[nm] for nm in names])
```

```python
import jax
import jax.numpy as jnp
from jax import lax
from jax.experimental import pallas as pl
from jax.experimental.pallas import tpu as pltpu

F32 = jnp.float32
BF16 = jnp.bfloat16

N_CHIPS = 4
LANES = 128
SUBLANES = 8
LRU_HEAD = 128
N_POOL_GROUPS = 4
POOL_WINDOWS = (2, 4, 8, 16)
RG_C = 8.0
LN_EPS = 1e-5
ALPHA = 2.0 ** 0.25
ADAM_LR, ADAM_B1, ADAM_B2, ADAM_EPS, ADAM_WD, ADAM_STEP = 0.001, 0.9, 0.999, 1e-08, 0.01, 10
VMEM_LIMIT = 56 * 1024 * 1024
SEQ_TILE = 512
MM_TILE = 512
LN_MM_K = 2048
LN_UNROLL = 8
ELT_BLOCK_BYTES = 2 * 1024 * 1024
RESIDENT_OPERAND_BYTES = 16 * 1024 * 1024
MESH = pl.DeviceIdType.MESH
ANY = pl.BlockSpec(memory_space=pl.ANY)


def _params(*sem):
    return pltpu.CompilerParams(dimension_semantics=sem, vmem_limit_bytes=VMEM_LIMIT)


def _sigmoid(z):
    return 1.0 / (1.0 + jnp.exp(-z))


def _neg_expm1(z):
    series = -(z * (1.0 + z * (0.5 + z * (1.0 / 6.0 + z * (1.0 / 24.0)))))
    return jnp.where(z > -0.01, series, 1.0 - jnp.exp(z))


def _softplus(z):
    return jnp.maximum(z, 0.0) + jnp.log1p(jnp.exp(-jnp.abs(z)))


_GELU_K = 0.7978845608028654
_GELU_C = 0.044715


def _gelu_and_grad(u):
    t = jnp.tanh(_GELU_K * (u + _GELU_C * (u * u * u)))
    g = 0.5 * u * (1.0 + t)
    dg = 0.5 * (1.0 + t) + 0.5 * u * (1.0 - t * t) * (_GELU_K * (1.0 + 3.0 * _GELU_C * u * u))
    return g, dg


def _shift_rows(prv, cur, nxt, o, rows):
    if o == 0:
        return cur
    if o == SUBLANES:
        return nxt
    if o == -SUBLANES:
        return prv
    if o > 0:
        return pltpu.roll(jnp.where(rows >= o, cur, nxt), SUBLANES - o, 0)
    p = -o
    return pltpu.roll(jnp.where(rows < SUBLANES - p, cur, prv), p, 0)


def _neighbour_chunks(main_ref, prev_ref, next_ref, r0, t_rows, cols, first_tile, last_tile):
    cur = main_ref[pl.ds(r0, SUBLANES), cols]
    before = main_ref[pl.ds(pl.multiple_of(jnp.maximum(r0 - SUBLANES, 0), SUBLANES), SUBLANES), cols]
    after = main_ref[pl.ds(pl.multiple_of(jnp.minimum(r0 + SUBLANES, t_rows - SUBLANES), SUBLANES), SUBLANES), cols]
    halo_prev = jnp.where(first_tile, 0.0, prev_ref[:, cols])
    halo_next = jnp.where(last_tile, 0.0, next_ref[:, cols])
    prv = jnp.where(r0 == 0, halo_prev, before)
    nxt = jnp.where(r0 == t_rows - SUBLANES, halo_next, after)
    return prv, cur, nxt


def _halo_specs(t_rows, n_rows, width, col_block):
    per = t_rows // SUBLANES
    last = n_rows // SUBLANES - 1
    return [
        pl.BlockSpec((t_rows, width), lambda i: (i, col_block)),
        pl.BlockSpec((SUBLANES, width), lambda i: (jnp.maximum(i * per - 1, 0), col_block)),
        pl.BlockSpec((SUBLANES, width), lambda i: (jnp.minimum((i + 1) * per, last), col_block)),
    ]


def _chunk_loop(t_rows, fn, init=None, unroll=1, descending=False):
    span = SUBLANES * unroll

    def step(ci, carry):
        base = pl.multiple_of(((t_rows // span - 1 - ci) if descending else ci) * span, span)
        for u in range(unroll):
            carry = fn(base + ((unroll - 1 - u) if descending else u) * SUBLANES, carry)
        return carry
    return lax.fori_loop(0, t_rows // span, step, init)


def _scan_chunk(a, b, h_in, rows, reverse):
    for dist in (1, 2, 4):
        if reverse:
            keep = rows < SUBLANES - dist
            shift = SUBLANES - dist
        else:
            keep = rows >= dist
            shift = dist
        b = a * jnp.where(keep, pltpu.roll(b, shift, 0), 0.0) + b
        a = a * jnp.where(keep, pltpu.roll(a, shift, 0), 1.0)
    return a * h_in + b


def _cast_call(x, dep):
    s, d = x.shape
    tm = min(MM_TILE, s)

    def body(x_ref, dep_ref, o_ref):
        o_ref[...] = x_ref[...].astype(BF16)

    return pl.pallas_call(
        body, name="cast_x", grid=(s // tm,),
        in_specs=[pl.BlockSpec((tm, d), lambda i: (i, 0)), ANY],
        out_specs=pl.BlockSpec((tm, d), lambda i: (i, 0)),
        out_shape=jax.ShapeDtypeStruct((s, d), BF16),
        compiler_params=_params("arbitrary"),
    )(x, dep)


def _proj_call(xb, w_in):
    s, d = xb.shape
    n, _, e4 = w_in.shape
    tm = min(MM_TILE, s)

    def body(x_ref, w_hbm, proj_ref, w_s, sems):
        @pl.when(pl.program_id(0) == 0)
        def _():
            copies = [pltpu.make_async_copy(w_hbm.at[k], w_s.at[:, pl.ds(k * e4, e4)], sems.at[k]) for k in range(n)]
            for cp in copies:
                cp.start()
            for cp in copies:
                cp.wait()

        proj_ref[...] = jnp.dot(x_ref[...], w_s[...], preferred_element_type=F32)

    return pl.pallas_call(
        body, name="proj", grid=(s // tm,),
        in_specs=[pl.BlockSpec((tm, d), lambda i: (i, 0)), ANY],
        out_specs=pl.BlockSpec((tm, n * e4), lambda i: (i, 0)),
        out_shape=jax.ShapeDtypeStruct((s, n * e4), F32),
        scratch_shapes=[pltpu.VMEM((d, n * e4), BF16), pltpu.SemaphoreType.DMA((n,))],
        compiler_params=_params("arbitrary"),
    )(xb, w_in)


def _conv_call(proj, conv_w, conv_b, c):
    s = proj.shape[0]
    t = min(SEQ_TILE, s)
    n_tiles = s // t

    def body(u_ref, up_ref, un_ref, w_ref, b_ref, xc_ref):
        i = pl.program_id(0)
        rows = lax.broadcasted_iota(jnp.int32, (SUBLANES, c), 0)
        w = w_ref[...]
        b = b_ref[...]

        def chunk(r0, _):
            prv, cur, nxt = _neighbour_chunks(u_ref, up_ref, un_ref, r0, t, slice(None), i == 0, i == n_tiles - 1)
            acc = b + w[1:2] * cur
            acc += w[0:1] * _shift_rows(prv, cur, nxt, -1, rows)
            acc += w[2:3] * _shift_rows(prv, cur, nxt, 1, rows)
            acc += w[3:4] * _shift_rows(prv, cur, nxt, 2, rows)
            xc_ref[pl.ds(r0, SUBLANES), :] = acc

        _chunk_loop(t, chunk)

    return pl.pallas_call(
        body, name="conv_fwd", grid=(n_tiles,),
        in_specs=_halo_specs(t, s, c, 1) + [pl.BlockSpec((4, c), lambda i: (0, 0)), pl.BlockSpec((1, c), lambda i: (0, 0))],
        out_specs=pl.BlockSpec((t, c), lambda i: (i, 0)),
        out_shape=jax.ShapeDtypeStruct((s, c), F32),
        compiler_params=_params("arbitrary"),
    )(proj, proj, proj, conv_w, conv_b)


def _gate_matmuls(xc_ref, wa_ref, wi_ref, pr_s, pi_s, heads):
    for h in range(heads):
        cs = pl.ds(h * LRU_HEAD, LRU_HEAD)
        xb = xc_ref[:, cs].astype(BF16)
        pr_s[:, cs] = jnp.dot(xb, wa_ref[h], preferred_element_type=F32)
        pi_s[:, cs] = jnp.dot(xb, wi_ref[h], preferred_element_type=F32)


def _rg_gates(pr, pi, ba, bi, sp):
    r = _sigmoid(pr + ba)
    ig = _sigmoid(pi + bi)
    log_a = (-RG_C * r) * sp
    a = jnp.exp(log_a)
    mult = jnp.sqrt(_neg_expm1(2.0 * log_a))
    return r, ig, a, mult


def _scan_fwd_call(xc, wa, wi, ba, bi, lam, reverse, dep):
    s, c = xc.shape
    heads = c // LRU_HEAD
    t = min(SEQ_TILE, s)
    n_tiles = s // t
    tile = (lambda i: (n_tiles - 1 - i, 0)) if reverse else (lambda i: (i, 0))
    whole2 = lambda i: (0, 0)
    whole3 = lambda i: (0, 0, 0)

    def body(xc_ref, wa_ref, wi_ref, ba_ref, bi_ref, lam_ref, dep_ref, h_ref, r_ref, ig_ref, a_ref, mult_ref,
             pr_s, pi_s, carry_s):
        @pl.when(pl.program_id(0) == 0)
        def _():
            carry_s[...] = jnp.zeros_like(carry_s)

        _gate_matmuls(xc_ref, wa_ref, wi_ref, pr_s, pi_s, heads)
        ba_v, bi_v = ba_ref[...], bi_ref[...]
        sp = _softplus(-lam_ref[...])

        rows = lax.broadcasted_iota(jnp.int32, (SUBLANES, c), 0)

        def chunk(r0, h_in):
            rs = pl.ds(r0, SUBLANES)
            r, ig, a, mult = _rg_gates(pr_s[rs, :], pi_s[rs, :], ba_v, bi_v, sp)
            r_ref[rs, :] = r
            ig_ref[rs, :] = ig
            a_ref[rs, :] = a
            mult_ref[rs, :] = mult
            h = _scan_chunk(a, mult * ig * xc_ref[rs, :], h_in, rows, reverse)
            h_ref[rs, :] = h
            return h[0:1, :] if reverse else h[SUBLANES - 1:SUBLANES, :]

        carry_s[...] = _chunk_loop(t, chunk, carry_s[...], unroll=2, descending=reverse)

    return pl.pallas_call(
        body, name="scan_fwd_rev" if reverse else "scan_fwd", grid=(n_tiles,),
        in_specs=[pl.BlockSpec((t, c), tile),
                  pl.BlockSpec((heads, LRU_HEAD, LRU_HEAD), whole3), pl.BlockSpec((heads, LRU_HEAD, LRU_HEAD), whole3),
                  pl.BlockSpec((1, c), whole2), pl.BlockSpec((1, c), whole2), pl.BlockSpec((1, c), whole2), ANY],
        out_specs=[pl.BlockSpec((t, c), tile)] * 5,
        out_shape=[jax.ShapeDtypeStruct((s, c), F32)] * 5,
        scratch_shapes=[pltpu.VMEM((t, c), F32), pltpu.VMEM((t, c), F32), pltpu.VMEM((1, c), F32)],
        compiler_params=_params("arbitrary"),
    )(xc, wa, wi, ba, bi, lam, dep)


def _window_counts(r0, tile_idx, t_rows, n_rows, half, shape):
    pos = tile_idx * t_rows + r0 + lax.broadcasted_iota(jnp.int32, shape, 0)
    hi = jnp.minimum(pos + half, n_rows)
    lo = jnp.maximum(pos - half, 0)
    return (hi - lo).astype(F32)


def _window_inverse_counts(r0, tile_idx, t_rows, n_rows, half, width):
    inv = 1.0 / _window_counts(r0, tile_idx, t_rows, n_rows, half, (SUBLANES, LANES))
    return jnp.tile(inv, (1, width // LANES))


def _pool_combine_call(proj, h_f, h_b, w_pool, pool_scale, p):
    s = proj.shape[0]
    c = h_f.shape[1]
    pg = p // N_POOL_GROUPS
    t = min(SEQ_TILE, s)
    n_tiles = s // t

    def body(u_ref, up_ref, un_ref, gate_ref, hf_ref, hb_ref, wp_ref, sc_ref, y_ref, d_ref, d_s, yr_s):
        i = pl.program_id(0)
        rows = lax.broadcasted_iota(jnp.int32, (SUBLANES, pg), 0)

        def chunk(r0, _):
            rs = pl.ds(r0, SUBLANES)
            for g, w in enumerate(POOL_WINDOWS):
                cols = pl.ds(g * pg, pg)
                prv, cur, nxt = _neighbour_chunks(u_ref, up_ref, un_ref, r0, t, cols, i == 0, i == n_tiles - 1)
                tot = cur
                for o in range(-(w // 2), w // 2):
                    if o != 0:
                        tot = tot + _shift_rows(prv, cur, nxt, o, rows)
                d_s[rs, cols] = tot * _window_inverse_counts(r0, i, t, s, w // 2, pg) - cur
            gate, _ = _gelu_and_grad(gate_ref[rs, :])
            yr_s[rs, :] = (hf_ref[rs, :] + hb_ref[rs, :]) * gate

        _chunk_loop(t, chunk)
        y_ref[:, pl.ds(p, c)] = yr_s[...].astype(BF16)
        d_ref[...] = d_s[...].astype(BF16)
        for g in range(N_POOL_GROUPS):
            cols = pl.ds(g * pg, pg)
            out = jnp.dot(d_s[:, cols].astype(BF16), wp_ref[g], preferred_element_type=F32)
            y_ref[:, cols] = (out * sc_ref[:, cols]).astype(BF16)

    return pl.pallas_call(
        body, name="pool_combine", grid=(n_tiles,),
        in_specs=_halo_specs(t, s, p, 0) + [
            pl.BlockSpec((t, c), lambda i: (i, 2)),
            pl.BlockSpec((t, c), lambda i: (i, 0)), pl.BlockSpec((t, c), lambda i: (i, 0)),
            pl.BlockSpec((N_POOL_GROUPS, pg, pg), lambda i: (0, 0, 0)), pl.BlockSpec((1, p), lambda i: (0, 0))],
        out_specs=[pl.BlockSpec((t, p + c), lambda i: (i, 0)), pl.BlockSpec((t, p), lambda i: (i, 0))],
        out_shape=[jax.ShapeDtypeStruct((s, p + c), BF16), jax.ShapeDtypeStruct((s, p), BF16)],
        scratch_shapes=[pltpu.VMEM((t, p), F32), pltpu.VMEM((t, c), F32)],
        compiler_params=_params("arbitrary"),
    )(proj, proj, proj, proj, h_f, h_b, w_pool, pool_scale)


def _layer_norm_rows(z, g, b):
    mu = jnp.mean(z, axis=-1, keepdims=True)
    zc = z - mu
    var = jnp.mean(zc * zc, axis=-1, keepdims=True)
    rstd = lax.rsqrt(var + LN_EPS)
    xh = zc * rstd
    return xh, rstd, xh * g + b


def _layer_norm_bwd_rows(dx, xh, rstd, g):
    dxh = dx * g
    m1 = jnp.mean(dxh, axis=-1, keepdims=True)
    m2 = jnp.mean(dxh * xh, axis=-1, keepdims=True)
    return rstd * (dxh - m1 - xh * m2)


def _out_ln1_call(y, w_out, x, g1, b1, dep):
    s, d = x.shape
    tm = min(SEQ_TILE, s)

    def body(y_ref, w_ref, x_ref, g_ref, b_ref, dep_ref, xh_ref, x1b_ref, rstd_ref, acc_s, x1_s):
        acc_s[...] = jnp.dot(y_ref[...], w_ref[...], preferred_element_type=F32)
        g, b = g_ref[...], b_ref[...]

        def chunk(r0, _):
            rs = pl.ds(r0, SUBLANES)
            xh, rstd, x1 = _layer_norm_rows(ALPHA * x_ref[rs, :] + acc_s[rs, :], g, b)
            xh_ref[rs, :] = xh
            x1_s[rs, :] = x1
            rstd_ref[rs, :] = rstd

        _chunk_loop(tm, chunk, unroll=LN_UNROLL)
        x1b_ref[...] = x1_s[...].astype(BF16)

    return pl.pallas_call(
        body, name="out_ln1", grid=(s // tm,),
        in_specs=[pl.BlockSpec((tm, d), lambda i: (i, 0)), pl.BlockSpec((d, d), lambda i: (0, 0)),
                  pl.BlockSpec((tm, d), lambda i: (i, 0)),
                  pl.BlockSpec((1, d), lambda i: (0, 0)), pl.BlockSpec((1, d), lambda i: (0, 0)), ANY],
        out_specs=[pl.BlockSpec((tm, d), lambda i: (i, 0)), pl.BlockSpec((tm, d), lambda i: (i, 0)),
                   pl.BlockSpec((tm, 1), lambda i: (i, 0))],
        out_shape=[jax.ShapeDtypeStruct((s, d), F32), jax.ShapeDtypeStruct((s, d), BF16), jax.ShapeDtypeStruct((s, 1), F32)],
        scratch_shapes=[pltpu.VMEM((tm, d), F32), pltpu.VMEM((tm, d), F32)],
        compiler_params=_params("arbitrary"),
    )(y, w_out, x, g1, b1, dep)


def _mlp_in_call(x1b, w1, dep, done):
    s, d = x1b.shape
    n, _, f4 = w1.shape
    tm = min(2 * MM_TILE, s)
    tn = min(1024, f4)
    per = f4 // tn
    blocks = n * per // 2
    first = 0 if done is None else blocks
    extra = [] if done is None else list(done)

    def body(x_ref, w_ref, dep_ref, *rest):
        r_ref, q_ref = rest[-2:]
        r = jnp.maximum(jnp.dot(x_ref[...], w_ref[...], preferred_element_type=F32), 0.0)
        r_ref[...] = r.astype(BF16)
        q_ref[...] = (r * r).astype(BF16)

    out_spec = pl.BlockSpec((tm, tn), lambda j, i: (i, first + j))
    return pl.pallas_call(
        body, name="mlp_in" if done is None else "mlp_in_rest", grid=(blocks, s // tm),
        in_specs=[pl.BlockSpec((tm, d), lambda j, i: (i, 0)),
                  pl.BlockSpec((None, d, tn), lambda j, i: ((first + j) // per, 0, (first + j) % per)), ANY] + [ANY] * len(extra),
        out_specs=[out_spec, out_spec],
        out_shape=[jax.ShapeDtypeStruct((s, n * f4), BF16), jax.ShapeDtypeStruct((s, n * f4), BF16)],
        input_output_aliases={3: 0, 4: 1} if extra else {},
        compiler_params=_params("arbitrary", "arbitrary"),
    )(x1b, w1, dep, *extra)


def _mlp_out_ln2_call(hsq, w2, xh1, g1, b1, g2, b2, target):
    s, f = hsq.shape
    d = w2.shape[1]
    tm = min(MM_TILE, s)
    tk = min(LN_MM_K, f)
    nk = f // tk

    def body(h_ref, w_ref, xh1_ref, g1_ref, b1_ref, g2_ref, b2_ref, t_ref,
             dz_ref, dzb_ref, loss_ref, dg_ref, db_ref, acc_s):
        i, k = pl.program_id(0), pl.program_id(1)

        @pl.when((i == 0) & (k == 0))
        def _():
            loss_ref[...] = jnp.zeros_like(loss_ref)
            dg_ref[...] = jnp.zeros_like(dg_ref)
            db_ref[...] = jnp.zeros_like(db_ref)

        @pl.when(k == 0)
        def _():
            acc_s[...] = jnp.zeros_like(acc_s)

        acc_s[...] += jnp.dot(h_ref[...], w_ref[...], preferred_element_type=F32)

        @pl.when(k == nk - 1)
        def _():
            g1, b1, g2, b2 = g1_ref[...], b1_ref[...], g2_ref[...], b2_ref[...]

            def chunk(r0, _):
                rs = pl.ds(r0, SUBLANES)
                x1 = xh1_ref[rs, :] * g1 + b1
                xh2, rstd, x2 = _layer_norm_rows(ALPHA * x1 + acc_s[rs, :], g2, b2)
                diff = x2 - t_ref[rs, :]
                loss_ref[...] += diff * diff
                dx2 = diff * (1.0 / d)
                dg_ref[...] += dx2 * xh2
                db_ref[...] += dx2
                dz = _layer_norm_bwd_rows(dx2, xh2, rstd, g2)
                dz_ref[rs, :] = dz

            _chunk_loop(tm, chunk, unroll=LN_UNROLL)
            dzb_ref[...] = dz_ref[...].astype(BF16)

    row = lambda i, k: (i, 0)
    vec = lambda i, k: (0, 0)
    return pl.pallas_call(
        body, name="mlp_out_ln2", grid=(s // tm, nk),
        in_specs=[pl.BlockSpec((tm, tk), lambda i, k: (i, k)), pl.BlockSpec((tk, d), lambda i, k: (k, 0)),
                  pl.BlockSpec((tm, d), row), pl.BlockSpec((1, d), vec), pl.BlockSpec((1, d), vec),
                  pl.BlockSpec((1, d), vec), pl.BlockSpec((1, d), vec), pl.BlockSpec((tm, d), row)],
        out_specs=[pl.BlockSpec((tm, d), row), pl.BlockSpec((tm, d), row),
                   pl.BlockSpec((SUBLANES, d), vec), pl.BlockSpec((SUBLANES, d), vec), pl.BlockSpec((SUBLANES, d), vec)],
        out_shape=[jax.ShapeDtypeStruct((s, d), F32), jax.ShapeDtypeStruct((s, d), BF16),
                   jax.ShapeDtypeStruct((SUBLANES, d), F32), jax.ShapeDtypeStruct((SUBLANES, d), F32),
                   jax.ShapeDtypeStruct((SUBLANES, d), F32)],
        scratch_shapes=[pltpu.VMEM((tm, d), F32)],
        compiler_params=_params("arbitrary", "arbitrary"),
    )(hsq, w2, xh1, g1, b1, g2, b2, target)


def _half_grad_call(a, b, half, row_sharded, init, name, dep):
    s, m = a.shape
    n = b.shape[1]
    if row_sharded:
        rows, cols = m // (2 * N_CHIPS), n
        tm = min(1024, rows)
        per = rows // tm
        tn = min(1024, cols)
        n_i, n_j = N_CHIPS * per, cols // tn
        a_block = lambda i, h: ((i // per) * 2 + h) * per + i % per
        out_block = lambda i, j: (i // per, i % per, j)
    else:
        rows, cols = m // 2, n // N_CHIPS
        tm = min(1024, rows)
        per = rows // tm
        tn = cols if cols % 1024 else 1024
        per_n = cols // tn
        n_i, n_j = per, N_CHIPS * per_n
        a_block = lambda i, h: h * per + i
        out_block = lambda i, j: (j // per_n, i, j % per_n)
    tk = min(2048, s)
    if row_sharded and tm < 1024 and s * n * 2 <= RESIDENT_OPERAND_BYTES:
        tk, tn, n_j = s, n, 1
    has_init = init is not None

    def body(half_ref, a_ref, b_ref, *rest):
        o_ref = rest[-1]

        @pl.when(pl.program_id(2) == 0)
        def _():
            o_ref[...] = rest[0][...] if has_init else jnp.zeros_like(o_ref)

        o_ref[...] += lax.dot_general(a_ref[...], b_ref[...], (((0,), (0,)), ((), ())), preferred_element_type=F32)

    out_spec = pl.BlockSpec((None, tm, tn), lambda i, j, k, h: out_block(i, j))
    in_specs = [pl.BlockSpec((tk, tm), lambda i, j, k, h: (k, a_block(i, h[0]))),
                pl.BlockSpec((tk, tn), lambda i, j, k, h: (k, j))]
    args = [a, b]
    if has_init:
        in_specs.append(out_spec)
        args.append(init)
    in_specs.append(ANY)
    args.append(dep)
    return pl.pallas_call(
        body, name=name,
        grid_spec=pltpu.PrefetchScalarGridSpec(num_scalar_prefetch=1, grid=(n_i, n_j, s // tk), in_specs=in_specs,
                                               out_specs=out_spec),
        out_shape=jax.ShapeDtypeStruct((N_CHIPS, rows, cols), F32),
        compiler_params=_params("arbitrary", "arbitrary", "arbitrary"),
    )(half, *args)


def _dhsq_call(dzb, w2, r, dep):
    s, d = dzb.shape
    f = w2.shape[0]
    tm = min(2 * MM_TILE, s)
    tn = min(1024, f)

    def body(dz_ref, w_ref, r_ref, dep_ref, o_ref):
        dh = lax.dot_general(dz_ref[...], w_ref[...], (((1,), (1,)), ((), ())), preferred_element_type=F32)
        o_ref[...] = (dh * (2.0 * r_ref[...].astype(F32))).astype(BF16)

    return pl.pallas_call(
        body, name="mlp_dpre", grid=(f // tn, s // tm),
        in_specs=[pl.BlockSpec((tm, d), lambda j, i: (i, 0)), pl.BlockSpec((tn, d), lambda j, i: (j, 0)),
                  pl.BlockSpec((tm, tn), lambda j, i: (i, j)), ANY],
        out_specs=pl.BlockSpec((tm, tn), lambda j, i: (i, j)),
        out_shape=jax.ShapeDtypeStruct((s, f), BF16),
        compiler_params=_params("arbitrary", "arbitrary"),
    )(dzb, w2, r, dep)


def _dx1_ln1_bwd_call(dpre, w1, dz2, xh1, rstd1, g1, dep):
    s, f = dpre.shape
    n, d, f4 = w1.shape
    tm = min(MM_TILE, s)
    tk = min(LN_MM_K, f4)
    per = f4 // tk
    nk = n * per

    def body(dp_ref, w_ref, dz2_ref, xh_ref, rstd_ref, g_ref, dep_ref, dz_ref, dzb_ref, dg_ref, db_ref, acc_s):
        i, k = pl.program_id(0), pl.program_id(1)

        @pl.when((i == 0) & (k == 0))
        def _():
            dg_ref[...] = jnp.zeros_like(dg_ref)
            db_ref[...] = jnp.zeros_like(db_ref)

        @pl.when(k == 0)
        def _():
            acc_s[...] = jnp.zeros_like(acc_s)

        acc_s[...] += lax.dot_general(dp_ref[...], w_ref[...], (((1,), (1,)), ((), ())), preferred_element_type=F32)

        @pl.when(k == nk - 1)
        def _():
            g = g_ref[...]

            def chunk(r0, _):
                rs = pl.ds(r0, SUBLANES)
                dx1 = acc_s[rs, :] + ALPHA * dz2_ref[rs, :]
                xh = xh_ref[rs, :]
                dg_ref[...] += dx1 * xh
                db_ref[...] += dx1
                dz = _layer_norm_bwd_rows(dx1, xh, rstd_ref[rs, :], g)
                dz_ref[rs, :] = dz

            _chunk_loop(tm, chunk, unroll=LN_UNROLL)
            dzb_ref[...] = dz_ref[...].astype(BF16)

    row = lambda i, k: (i, 0)
    vec = lambda i, k: (0, 0)
    return pl.pallas_call(
        body, name="dx1_ln1_bwd", grid=(s // tm, nk),
        in_specs=[pl.BlockSpec((tm, tk), lambda i, k: (i, k)),
                  pl.BlockSpec((None, d, tk), lambda i, k: (k // per, 0, k % per)),
                  pl.BlockSpec((tm, d), row), pl.BlockSpec((tm, d), row), pl.BlockSpec((tm, 1), row),
                  pl.BlockSpec((1, d), vec), ANY],
        out_specs=[pl.BlockSpec((tm, d), row), pl.BlockSpec((tm, d), row),
                   pl.BlockSpec((SUBLANES, d), vec), pl.BlockSpec((SUBLANES, d), vec)],
        out_shape=[jax.ShapeDtypeStruct((s, d), F32), jax.ShapeDtypeStruct((s, d), BF16),
                   jax.ShapeDtypeStruct((SUBLANES, d), F32), jax.ShapeDtypeStruct((SUBLANES, d), F32)],
        scratch_shapes=[pltpu.VMEM((tm, d), F32)],
        compiler_params=_params("arbitrary", "arbitrary"),
    )(dpre, w1, dz2, xh1, rstd1, g1, dep)


def _mixer_bwd_call(dzb, w_out, d_pool, proj, h_f, h_b, w_pool, pool_scale, p, dep):
    s, d = dzb.shape
    c = h_f.shape[1]
    pg = p // N_POOL_GROUPS
    t = min(SEQ_TILE, s)
    n_tiles = s // t

    def body(dz_ref, wo_ref, d_ref, gate_ref, hf_ref, hb_ref, wp_ref, sc_ref, dep_ref,
             e_ref, dh_ref, dgate_ref, dwp_ref, dsc_ref, dd_s, dy_s):
        i = pl.program_id(0)

        @pl.when(i == 0)
        def _():
            dwp_ref[...] = jnp.zeros_like(dwp_ref)
            dsc_ref[...] = jnp.zeros_like(dsc_ref)

        dy_s[...] = lax.dot_general(dz_ref[...], wo_ref[...], (((1,), (1,)), ((), ())), preferred_element_type=F32)

        for g in range(N_POOL_GROUPS):
            cols = pl.ds(g * pg, pg)
            dg = d_ref[:, cols]
            out = jnp.dot(dg, wp_ref[g], preferred_element_type=F32)
            dyp = dy_s[:, cols]
            prod = dyp * out
            dsc_ref[:, cols] += jnp.sum(prod.reshape(t // SUBLANES, SUBLANES, pg), axis=0)
            dout = (dyp * sc_ref[:, cols]).astype(BF16)
            dwp_ref[g] += lax.dot_general(dg, dout, (((0,), (0,)), ((), ())), preferred_element_type=F32)
            dd_s[:, cols] = lax.dot_general(dout, wp_ref[g], (((1,), (1,)), ((), ())), preferred_element_type=F32)

        def chunk(r0, _):
            rs = pl.ds(r0, SUBLANES)
            for g, w in enumerate(POOL_WINDOWS):
                cols = pl.ds(g * pg, pg)
                e_ref[rs, cols] = dd_s[rs, cols] * _window_inverse_counts(r0, i, t, s, w // 2, pg)
            gate, dgate = _gelu_and_grad(gate_ref[rs, :])
            dyr = dy_s[rs, pl.ds(p, c)]
            dh_ref[rs, :] = dyr * gate
            dd_s[rs, :] = dyr * (hf_ref[rs, :] + hb_ref[rs, :]) * dgate

        _chunk_loop(t, chunk)
        dgate_ref[...] = dd_s[...].astype(BF16)

    tile = lambda i: (i, 0)
    return pl.pallas_call(
        body, name="mixer_bwd", grid=(n_tiles,),
        in_specs=[pl.BlockSpec((t, d), tile), pl.BlockSpec((p + c, d), lambda i: (0, 0)), pl.BlockSpec((t, p), tile),
                  pl.BlockSpec((t, c), lambda i: (i, 2)), pl.BlockSpec((t, c), tile), pl.BlockSpec((t, c), tile),
                  pl.BlockSpec((N_POOL_GROUPS, pg, pg), lambda i: (0, 0, 0)), pl.BlockSpec((1, p), lambda i: (0, 0)), ANY],
        out_specs=[pl.BlockSpec((t, p), tile), pl.BlockSpec((t, c), tile), pl.BlockSpec((t, c), tile),
                   pl.BlockSpec((N_POOL_GROUPS, pg, pg), lambda i: (0, 0, 0)), pl.BlockSpec((SUBLANES, p), lambda i: (0, 0))],
        out_shape=[jax.ShapeDtypeStruct((s, p), F32), jax.ShapeDtypeStruct((s, c), F32), jax.ShapeDtypeStruct((s, c), BF16),
                   jax.ShapeDtypeStruct((N_POOL_GROUPS, pg, pg), F32), jax.ShapeDtypeStruct((SUBLANES, p), F32)],
        scratch_shapes=[pltpu.VMEM((t, p), F32), pltpu.VMEM((t, p + c), F32)],
        compiler_params=_params("arbitrary"),
    )(dzb, w_out, d_pool, proj, h_f, h_b, w_pool, pool_scale, dep)


def _scan_bwd_call(xc, dh, h_dir, gates, dxc_prev, wa, wi, lam, reverse, dep):
    s, c = xc.shape
    heads = c // LRU_HEAD
    t = min(SEQ_TILE, s)
    n_tiles = s // t
    per = t // SUBLANES
    last_blk = s // SUBLANES - 1
    tile = (lambda i: (i, 0)) if reverse else (lambda i: (n_tiles - 1 - i, 0))
    if reverse:
        halo = lambda i: (jnp.minimum((i + 1) * per, last_blk), 0)
    else:
        halo = lambda i: (jnp.maximum((n_tiles - 1 - i) * per - 1, 0), 0)
    whole2 = lambda i: (0, 0)
    whole3 = lambda i: (0, 0, 0)
    has_prev = dxc_prev is not None
    n_in = 11 + int(has_prev) + 1

    def body(*refs):
        xc_ref, dh_ref, h_ref, hh_ref, r_ref, ig_ref, a_ref, mult_ref = refs[:8]
        prev_ref = refs[8] if has_prev else None
        wa_ref, wi_ref, lam_ref = refs[n_in - 4:n_in - 1]
        dxc_ref, dwa_ref, dwi_ref, dba_ref, dbi_ref, dsp_ref = refs[n_in:n_in + 6]
        pr_s, pi_s, carry_s = refs[n_in + 6:]
        step = pl.program_id(0)
        tile_idx = step if reverse else n_tiles - 1 - step

        @pl.when(step == 0)
        def _():
            carry_s[...] = jnp.zeros_like(carry_s)
            dwa_ref[...] = jnp.zeros_like(dwa_ref)
            dwi_ref[...] = jnp.zeros_like(dwi_ref)
            dba_ref[...] = jnp.zeros_like(dba_ref)
            dbi_ref[...] = jnp.zeros_like(dbi_ref)
            dsp_ref[...] = jnp.zeros_like(dsp_ref)

        sp = _softplus(-lam_ref[...])
        rows = lax.broadcasted_iota(jnp.int32, (SUBLANES, c), 0)

        def chunk(r0, u_in):
            rs = pl.ds(r0, SUBLANES)
            xcv = xc_ref[rs, :]
            r, ig, a, mult = r_ref[rs, :], ig_ref[rs, :], a_ref[rs, :], mult_ref[rs, :]
            dhv = dh_ref[rs, :]
            u = _scan_chunk(a, a * dhv, u_in, rows, not reverse)
            if reverse:
                gt = dhv + jnp.where(rows >= 1, pltpu.roll(u, 1, 0), u_in)
                u_out = u[SUBLANES - 1:SUBLANES, :]
            else:
                gt = dhv + jnp.where(rows < SUBLANES - 1, pltpu.roll(u, SUBLANES - 1, 0), u_in)
                u_out = u[0:1, :]
            cur = h_ref[rs, :]
            if reverse:
                after = h_ref[pl.ds(pl.multiple_of(jnp.minimum(r0 + SUBLANES, t - SUBLANES), SUBLANES), SUBLANES), :]
                edge = jnp.where(tile_idx == n_tiles - 1, 0.0, hh_ref[...])
                nxt = jnp.where(r0 == t - SUBLANES, edge, after)
                hs = _shift_rows(cur, cur, nxt, 1, rows)
            else:
                before = h_ref[pl.ds(pl.multiple_of(jnp.maximum(r0 - SUBLANES, 0), SUBLANES), SUBLANES), :]
                edge = jnp.where(tile_idx == 0, 0.0, hh_ref[...])
                prv = jnp.where(r0 == 0, edge, before)
                hs = _shift_rows(prv, cur, cur, -1, rows)
            gx = gt * xcv
            dmult = gx * ig
            di = gx * mult
            dlog_a = (gt * hs) * a - dmult * (a * a) / mult
            dr = dlog_a * (-RG_C * sp)
            dsp_ref[...] += dlog_a * (-RG_C * r)
            dpr = dr * r * (1.0 - r)
            dpi = di * ig * (1.0 - ig)
            dba_ref[...] += dpr
            dbi_ref[...] += dpi
            direct = gt * mult * ig
            if has_prev:
                direct = direct + prev_ref[rs, :]
            dxc_ref[rs, :] = direct
            pr_s[rs, :] = dpr
            pi_s[rs, :] = dpi
            return u_out

        carry_s[...] = _chunk_loop(t, chunk, carry_s[...], unroll=2, descending=not reverse)

        for h in range(heads):
            cs = pl.ds(h * LRU_HEAD, LRU_HEAD)
            xb = xc_ref[:, cs].astype(BF16)
            dprb = pr_s[:, cs].astype(BF16)
            dpib = pi_s[:, cs].astype(BF16)
            dwa_ref[h] += lax.dot_general(xb, dprb, (((0,), (0,)), ((), ())), preferred_element_type=F32)
            dwi_ref[h] += lax.dot_general(xb, dpib, (((0,), (0,)), ((), ())), preferred_element_type=F32)
            dxc_ref[:, cs] += (
                lax.dot_general(dprb, wa_ref[h], (((1,), (1,)), ((), ())), preferred_element_type=F32)
                + lax.dot_general(dpib, wi_ref[h], (((1,), (1,)), ((), ())), preferred_element_type=F32))

    tile_spec = pl.BlockSpec((t, c), tile)
    in_specs = [tile_spec, tile_spec, tile_spec, pl.BlockSpec((SUBLANES, c), halo)] + [tile_spec] * 4
    args = [xc, dh, h_dir, h_dir, *gates]
    if has_prev:
        in_specs.append(tile_spec)
        args.append(dxc_prev)
    in_specs += [pl.BlockSpec((heads, LRU_HEAD, LRU_HEAD), whole3), pl.BlockSpec((heads, LRU_HEAD, LRU_HEAD), whole3),
                 pl.BlockSpec((1, c), whole2), ANY]
    args += [wa, wi, lam, dep]
    assert len(args) == n_in
    return pl.pallas_call(
        body, name="scan_bwd_rev" if reverse else "scan_bwd", grid=(n_tiles,),
        in_specs=in_specs,
        out_specs=[tile_spec,
                   pl.BlockSpec((heads, LRU_HEAD, LRU_HEAD), whole3), pl.BlockSpec((heads, LRU_HEAD, LRU_HEAD), whole3),
                   pl.BlockSpec((SUBLANES, c), whole2), pl.BlockSpec((SUBLANES, c), whole2), pl.BlockSpec((SUBLANES, c), whole2)],
        out_shape=[jax.ShapeDtypeStruct((s, c), F32),
                   jax.ShapeDtypeStruct((heads, LRU_HEAD, LRU_HEAD), F32), jax.ShapeDtypeStruct((heads, LRU_HEAD, LRU_HEAD), F32),
                   jax.ShapeDtypeStruct((SUBLANES, c), F32), jax.ShapeDtypeStruct((SUBLANES, c), F32),
                   jax.ShapeDtypeStruct((SUBLANES, c), F32)],
        scratch_shapes=[pltpu.VMEM((t, c), F32), pltpu.VMEM((t, c), F32), pltpu.VMEM((1, c), F32)],
        compiler_params=_params("arbitrary"),
    )(*args)


def _dproj_call(e_pool, dxc, proj, dgate, conv_w, p):
    s, c = dxc.shape
    pg = p // N_POOL_GROUPS
    t = min(SEQ_TILE, s)
    n_tiles = s // t

    def body(e_ref, ep_ref, en_ref, dx_ref, dxp_ref, dxn_ref, u_ref, up_ref, un_ref, dgate_ref, w_ref,
             dproj_ref, dcw_ref, dcb_ref, st_s):
        i = pl.program_id(0)
        first, last = i == 0, i == n_tiles - 1

        @pl.when(first)
        def _():
            dcw_ref[...] = jnp.zeros_like(dcw_ref)
            dcb_ref[...] = jnp.zeros_like(dcb_ref)

        rows_p = lax.broadcasted_iota(jnp.int32, (SUBLANES, pg), 0)
        rows_c = lax.broadcasted_iota(jnp.int32, (SUBLANES, c), 0)
        w = w_ref[...]

        def chunk(r0, _):
            rs = pl.ds(r0, SUBLANES)
            for g, win in enumerate(POOL_WINDOWS):
                cols = pl.ds(g * pg, pg)
                prv, cur, nxt = _neighbour_chunks(e_ref, ep_ref, en_ref, r0, t, cols, first, last)
                tot = cur
                for o in range(-(win // 2) + 1, win // 2 + 1):
                    if o != 0:
                        tot = tot + _shift_rows(prv, cur, nxt, o, rows_p)
                cnt = _window_counts(r0, i, t, s, win // 2, (SUBLANES, pg))
                st_s[rs, cols] = tot - cur * cnt
            prv, cur, nxt = _neighbour_chunks(dx_ref, dxp_ref, dxn_ref, r0, t, slice(None), first, last)
            du = w[1:2] * cur
            du += w[0:1] * _shift_rows(prv, cur, nxt, 1, rows_c)
            du += w[2:3] * _shift_rows(prv, cur, nxt, -1, rows_c)
            du += w[3:4] * _shift_rows(prv, cur, nxt, -2, rows_c)
            st_s[rs, pl.ds(p, c)] = du
            uprv, ucur, unxt = _neighbour_chunks(u_ref, up_ref, un_ref, r0, t, slice(None), first, last)
            dcb_ref[...] += cur
            for j, o in enumerate((-1, 0, 1, 2)):
                dcw_ref[j] += cur * _shift_rows(uprv, ucur, unxt, o, rows_c)

        _chunk_loop(t, chunk)
        dproj_ref[:, pl.ds(0, p + c)] = st_s[...].astype(BF16)
        dproj_ref[:, pl.ds(p + c, c)] = dgate_ref[...]

    return pl.pallas_call(
        body, name="dproj", grid=(n_tiles,),
        in_specs=_halo_specs(t, s, p, 0) + _halo_specs(t, s, c, 0) + _halo_specs(t, s, c, 1) + [
            pl.BlockSpec((t, c), lambda i: (i, 0)), pl.BlockSpec((4, c), lambda i: (0, 0))],
        out_specs=[pl.BlockSpec((t, p + 2 * c), lambda i: (i, 0)),
                   pl.BlockSpec((4, SUBLANES, c), lambda i: (0, 0, 0)), pl.BlockSpec((SUBLANES, c), lambda i: (0, 0))],
        out_shape=[jax.ShapeDtypeStruct((s, p + 2 * c), BF16), jax.ShapeDtypeStruct((4, SUBLANES, c), F32),
                   jax.ShapeDtypeStruct((SUBLANES, c), F32)],
        scratch_shapes=[pltpu.VMEM((t, p + c), F32)],
        compiler_params=_params("arbitrary"),
    )(e_pool, e_pool, e_pool, dxc, dxc, dxc, proj, proj, proj, dgate, conv_w)


def _dx_call(dproj, w_in, dz1, dep):
    s, e = dproj.shape
    n, d, e4 = w_in.shape
    tm = min(MM_TILE, s)

    def body(dp_ref, w_hbm, dz_ref, dep_ref, o_ref, w_s, sems):
        @pl.when(pl.program_id(0) == 0)
        def _():
            copies = [pltpu.make_async_copy(w_hbm.at[k], w_s.at[:, pl.ds(k * e4, e4)], sems.at[k]) for k in range(n)]
            for cp in copies:
                cp.start()
            for cp in copies:
                cp.wait()

        o_ref[...] = ALPHA * dz_ref[...] + lax.dot_general(
            dp_ref[...], w_s[...], (((1,), (1,)), ((), ())), preferred_element_type=F32)

    return pl.pallas_call(
        body, name="grad_x", grid=(s // tm,),
        in_specs=[pl.BlockSpec((tm, e), lambda i: (i, 0)), ANY, pl.BlockSpec((tm, d), lambda i: (i, 0)), ANY],
        out_specs=pl.BlockSpec((tm, d), lambda i: (i, 0)),
        out_shape=jax.ShapeDtypeStruct((s, d), F32),
        scratch_shapes=[pltpu.VMEM((d, e), BF16), pltpu.SemaphoreType.DMA((n,))],
        compiler_params=_params("arbitrary"),
    )(dproj, w_in, dz1, dep)


def _row_tile(rows, cols, n_arrays):
    limit = max(SUBLANES, ELT_BLOCK_BYTES // (4 * cols * max(1, n_arrays // 4)))
    best = SUBLANES
    for cand in range(SUBLANES, min(rows, limit) + 1, SUBLANES):
        if rows % cand == 0:
            best = cand
    return best if rows % SUBLANES == 0 else rows


def _cast_to_slot_call(a, idx, dtype, name, dep):
    rows, cols = a.shape
    tr = _row_tile(rows, cols, 2)
    extra = [] if dep is None else [dep]

    def body(idx_ref, a_ref, *rest):
        rest[-1][...] = a_ref[...].astype(dtype)

    return pl.pallas_call(
        body, name=name,
        grid_spec=pltpu.PrefetchScalarGridSpec(
            num_scalar_prefetch=1, grid=(rows // tr,),
            in_specs=[pl.BlockSpec((tr, cols), lambda i, idx_ref: (i, 0))] + [ANY] * len(extra),
            out_specs=pl.BlockSpec((None, tr, cols), lambda i, idx_ref: (idx_ref[1], i, 0))),
        out_shape=jax.ShapeDtypeStruct((N_CHIPS, rows, cols), dtype),
        compiler_params=_params("arbitrary"),
    )(idx, a, *extra)


def _add_half_call(g, recv, idx, name):
    _, rows, cols = g.shape
    tr = _row_tile(rows, cols, 3)

    def body(idx_ref, g_ref, r_ref, o_ref):
        o_ref[...] = g_ref[...] + r_ref[...]

    return pl.pallas_call(
        body, name=name,
        grid_spec=pltpu.PrefetchScalarGridSpec(
            num_scalar_prefetch=1, grid=(rows // tr,),
            in_specs=[pl.BlockSpec((None, tr, cols), lambda i, idx_ref: (idx_ref[0], i, 0)),
                      pl.BlockSpec((tr, cols), lambda i, idx_ref: (i, 0))],
            out_specs=pl.BlockSpec((None, tr, cols), lambda i, idx_ref: (idx_ref[1], i, 0))),
        out_shape=jax.ShapeDtypeStruct((N_CHIPS, rows, cols), F32),
        compiler_params=_params("arbitrary"),
    )(idx, g, recv)


def _sum_chips_call(own, recv, idx, name):
    _, rows, cols = recv.shape
    tr = _row_tile(rows, cols, 5)
    out_spec = pl.BlockSpec((None, tr, cols), lambda i, idx_ref: (idx_ref[0], i, 0))
    if own is None:
        def body(idx_ref, r_ref, o_ref):
            o_ref[...] = ((r_ref[0] + r_ref[1]) + r_ref[2]) + r_ref[3]
        in_specs = [pl.BlockSpec((N_CHIPS, tr, cols), lambda i, idx_ref: (0, i, 0))]
        args = (recv,)
    else:
        def body(idx_ref, p_ref, r_ref, o_ref):
            o_ref[...] = ((p_ref[...] + r_ref[0]) + r_ref[1]) + r_ref[2]
        in_specs = [pl.BlockSpec((None, tr, cols), lambda i, idx_ref: (idx_ref[1], i, 0)),
                    pl.BlockSpec((N_CHIPS - 1, tr, cols), lambda i, idx_ref: (0, i, 0))]
        args = (own, recv)
    return pl.pallas_call(
        body, name=name,
        grid_spec=pltpu.PrefetchScalarGridSpec(num_scalar_prefetch=1, grid=(rows // tr,), in_specs=in_specs, out_specs=out_spec),
        out_shape=jax.ShapeDtypeStruct((2, rows, cols), F32),
        compiler_params=_params("arbitrary"),
    )(idx, *args)


def _adamw_math(g, w, m, v):
    mn = ADAM_B1 * m + (1.0 - ADAM_B1) * g
    vn = ADAM_B2 * v + (1.0 - ADAM_B2) * (g * g)
    m_hat = mn / (1.0 - ADAM_B1 ** ADAM_STEP)
    v_hat = vn / (1.0 - ADAM_B2 ** ADAM_STEP)
    return -ADAM_LR * (m_hat / (jnp.sqrt(v_hat) + ADAM_EPS) + ADAM_WD * w), mn, vn


def _adamw_call(g, w, m, v, name):
    rows, cols = w.shape
    tr = _row_tile(rows, cols, 4)

    def body(g_ref, w_ref, m_ref, v_ref, go_ref, d_ref, mo_ref, vo_ref):
        gv = g_ref[...]
        go_ref[...] = gv
        d_ref[...], mo_ref[...], vo_ref[...] = _adamw_math(gv, w_ref[...], m_ref[...], v_ref[...])

    spec = pl.BlockSpec((tr, cols), lambda i: (i, 0))
    shape = jax.ShapeDtypeStruct((rows, cols), F32)
    return pl.pallas_call(
        body, name=name, grid=(rows // tr,),
        in_specs=[spec] * 4, out_specs=[spec] * 4, out_shape=[shape] * 4,
        compiler_params=_params("arbitrary"),
    )(g, w, m, v)


def _adamw_small_call(gs, ws, ms, vs):
    n = len(ws)

    def body(*refs):
        ins = [refs[k * n:(k + 1) * n] for k in range(4)]
        outs = [refs[(4 + k) * n:(5 + k) * n] for k in range(3)]
        for a in range(n):
            outs[0][a][...], outs[1][a][...], outs[2][a][...] = _adamw_math(*[ins[k][a][...] for k in range(4)])

    whole = pl.BlockSpec(memory_space=pltpu.VMEM)
    res = pl.pallas_call(
        body, name="adamw_small",
        in_specs=[whole] * (4 * n), out_specs=[whole] * (3 * n),
        out_shape=[jax.ShapeDtypeStruct(w.shape, F32) for w in ws] * 3,
        compiler_params=pltpu.CompilerParams(vmem_limit_bytes=VMEM_LIMIT),
    )(*gs, *ws, *ms, *vs)
    return res[:n], res[n:2 * n], res[2 * n:]


def _mesh_place():
    x, y, c = lax.axis_index("x"), lax.axis_index("y"), lax.axis_index("c")
    chips = [(1 - x, y), (x, 1 - y), (1 - x, 1 - y)]
    return x, y, c, chips


def _remote(src, dst, send_sems, recv_sems, idx, device):
    return pltpu.make_async_remote_copy(src_ref=src, dst_ref=dst, send_sem=send_sems.at[idx], recv_sem=recv_sems.at[idx],
                                        device_id=device, device_id_type=MESH)


HBM_SPEC = pl.BlockSpec(memory_space=pltpu.HBM)
SEM_SPEC = pl.BlockSpec(memory_space=pltpu.SEMAPHORE)
ORDERED_EFFECT = pltpu.SideEffectType.DATAFLOW_SIDE_EFFECTING


def _in_hbm(a):
    return pltpu.with_memory_space_constraint(a, pltpu.HBM)


def _start_copies_call(name, bufs, groups, after=None):
    n, g = len(bufs), len(groups)
    extra = [] if after is None else [after]
    first_out = n + len(extra)

    def body(*refs):
        outs = refs[first_out:first_out + n]
        sems = refs[first_out + n:first_out + n + 2 * g]
        token = refs[first_out + n + 2 * g]
        for i, (which, copies_fn, _) in enumerate(groups):
            for mine, _ in copies_fn([outs[w] for w in which], sems[2 * i], sems[2 * i + 1]):
                mine.start()
        token[...] = jnp.zeros_like(token)

    sem_shapes = [pltpu.SemaphoreType.DMA((cnt,)) for _, _, cnt in groups for _ in range(2)]
    res = pl.pallas_call(
        body, name=name,
        in_specs=[HBM_SPEC] * n + [ANY] * len(extra),
        out_specs=[HBM_SPEC] * n + [SEM_SPEC] * (2 * g) + [pl.BlockSpec(memory_space=pltpu.VMEM)],
        out_shape=[pltpu.HBM(a.shape, a.dtype) for a in bufs] + sem_shapes + [jax.ShapeDtypeStruct((SUBLANES, LANES), F32)],
        input_output_aliases={a: a for a in range(n)},
        compiler_params=pltpu.CompilerParams(has_side_effects=ORDERED_EFFECT),
    )(*[_in_hbm(a) for a in bufs], *extra)
    sems = res[n:n + 2 * g]
    return list(res[:n]), [(sems[2 * i], sems[2 * i + 1]) for i in range(g)], res[n + 2 * g]


def _wait_copies_call(name, bufs, sems, copies_fn, after):
    n = len(bufs)

    def body(*refs):
        ins = refs[:n]
        send_sems, recv_sems = refs[n], refs[n + 1]
        for mine, arriving in copies_fn(list(ins), send_sems, recv_sems):
            arriving.wait_recv()
            mine.wait_send()

    res = pl.pallas_call(
        body, name=name,
        in_specs=[HBM_SPEC] * n + [SEM_SPEC, SEM_SPEC, ANY],
        out_specs=[HBM_SPEC] * n,
        out_shape=[pltpu.HBM(a.shape, a.dtype) for a in bufs],
        input_output_aliases={a: a for a in range(n)},
        compiler_params=pltpu.CompilerParams(has_side_effects=ORDERED_EFFECT),
    )(*bufs, sems[0], sems[1], after)
    return list(res)


def _gather_copies(bufs, send_sems, recv_sems):
    x, y, c, chips = _mesh_place()
    k = 2 * x + y
    out = []
    for a, buf in enumerate(bufs):
        for j, (px, py) in enumerate(chips):
            kj = 2 * px + py
            mine = _remote(buf.at[k, c], buf.at[k, c], send_sems, recv_sems, 3 * a + j, (px, py, c))
            arriving = _remote(buf.at[k, c], buf.at[kj, c], send_sems, recv_sems, 3 * a + j, (px, py, c))
            out.append((mine, arriving))
    return out


def _exchange_copies(n_sharded, n_replicated):
    def copies(bufs, send_sems, recv_sems):
        x, y, c, chips = _mesh_place()
        k = 2 * x + y
        sums, lands = bufs[:n_sharded], bufs[n_sharded:2 * n_sharded]
        repl = bufs[2 * n_sharded:]
        out = []
        for j, (px, py) in enumerate(chips):
            kj = 2 * px + py
            for a in range(n_sharded):
                cp = _remote(sums[a].at[kj], lands[a].at[j], send_sems, recv_sems, 3 * a + j, (px, py, c))
                out.append((cp, cp))
            for a in range(n_replicated):
                idx = 3 * (n_sharded + a) + j
                mine = _remote(repl[a].at[k], repl[a].at[k], send_sems, recv_sems, idx, (px, py, c))
                arriving = _remote(repl[a].at[k], repl[a].at[kj], send_sems, recv_sems, idx, (px, py, c))
                out.append((mine, arriving))
        return out
    return copies


def _sibling_copies(n, halves):
    def copies(bufs, send_sems, recv_sems):
        x, y, c, _ = _mesh_place()
        out = []
        for a in range(n):
            src = bufs[a].at[1 - c] if halves else bufs[a]
            cp = _remote(src, bufs[n + a], send_sems, recv_sems, a, (x, y, 1 - c))
            out.append((cp, cp))
        return out
    return copies


def _forward_copies(bufs, send_sems, recv_sems):
    x, y, c, chips = _mesh_place()
    out = []
    for a, buf in enumerate(bufs):
        for j, (px, py) in enumerate(chips):
            kj = 2 * px + py
            mine = _remote(buf.at[kj, c], buf.at[kj, c], send_sems, recv_sems, 3 * a + j, (x, y, 1 - c))
            arriving = _remote(buf.at[kj, c], buf.at[kj, 1 - c], send_sems, recv_sems, 3 * a + j, (x, y, 1 - c))
            out.append((mine, arriving))
    return out


def _join_copies(bufs, send_sems, recv_sems):
    x, y, c, _ = _mesh_place()
    out = []
    for a, buf in enumerate(bufs):
        mine = _remote(buf.at[c], buf.at[c], send_sems, recv_sems, a, (x, y, 1 - c))
        arriving = _remote(buf.at[c], buf.at[1 - c], send_sems, recv_sems, a, (x, y, 1 - c))
        out.append((mine, arriving))
    return out


def _forward_to_sibling_call(bufs, name):
    n = len(bufs)

    def body(*refs):
        ins, outs = refs[:n], refs[n:2 * n]
        send_sems, recv_sems = refs[2 * n:]
        x, y, c, chips = _mesh_place()
        sibling = (x, y, 1 - c)
        sends = []
        for a in range(n):
            for j, (px, py) in enumerate(chips):
                kj = 2 * px + py
                sends.append(_remote(ins[a].at[kj, c], outs[a].at[kj, c], send_sems, recv_sems, 3 * a + j, sibling))
        for cp in sends:
            cp.start()
        for a in range(n):
            for j, (px, py) in enumerate(chips):
                kj = 2 * px + py
                _remote(ins[a].at[kj, c], outs[a].at[kj, 1 - c], send_sems, recv_sems, 3 * a + j, sibling).wait_recv()
        for cp in sends:
            cp.wait_send()

    return pl.pallas_call(
        body, name=name,
        in_specs=[ANY] * n, out_specs=[ANY] * n,
        out_shape=[jax.ShapeDtypeStruct(a.shape, a.dtype) for a in bufs],
        input_output_aliases={a: a for a in range(n)},
        scratch_shapes=[pltpu.SemaphoreType.DMA((3 * n,)), pltpu.SemaphoreType.DMA((3 * n,))],
    )(*bufs)


def _pack(arrays, rows_multiple):
    flat = jnp.concatenate([a.reshape(-1) for a in arrays])
    per = LANES * rows_multiple
    padded = -(-flat.shape[0] // per) * per
    flat = jnp.pad(flat, (0, padded - flat.shape[0]))
    return flat.reshape(-1, LANES)


def _unpack(packed, shapes):
    flat = packed.reshape(-1)
    out, at = [], 0
    for shp in shapes:
        size = 1
        for dim in shp:
            size *= dim
        out.append(flat[at:at + size].reshape(shp))
        at += size
    return out


def _halves(a):
    return a.reshape((2, a.shape[0] // 2) + a.shape[1:])


def kernel(x, ln_mix_g, ln_mix_b, w_in, w_pool, pool_scale, conv_w, conv_b, w_rg_a, b_rg_a, w_rg_i, b_rg_i, rg_lambda, w_out, ln_ffn_g, ln_ffn_b, w_mlp_in, w_mlp_out, loss_target, m_ln_mix_g, m_ln_mix_b, m_w_in, m_w_pool, m_pool_scale, m_conv_w, m_conv_b, m_w_rg_a, m_b_rg_a, m_w_rg_i, m_b_rg_i, m_rg_lambda, m_w_out, m_ln_ffn_g, m_ln_ffn_b, m_w_mlp_in, m_w_mlp_out, v_ln_mix_g, v_ln_mix_b, v_w_in, v_w_pool, v_pool_scale, v_conv_w, v_conv_b, v_w_rg_a, v_b_rg_a, v_w_rg_i, v_b_rg_i, v_rg_lambda, v_w_out, v_ln_ffn_g, v_ln_ffn_b, v_w_mlp_in, v_w_mlp_out):
    weights = dict(ln_mix_g=ln_mix_g, ln_mix_b=ln_mix_b, w_in=w_in, w_pool=w_pool, pool_scale=pool_scale, conv_w=conv_w,
                   conv_b=conv_b, w_rg_a=w_rg_a, b_rg_a=b_rg_a, w_rg_i=w_rg_i, b_rg_i=b_rg_i, rg_lambda=rg_lambda,
                   w_out=w_out, ln_ffn_g=ln_ffn_g, ln_ffn_b=ln_ffn_b, w_mlp_in=w_mlp_in, w_mlp_out=w_mlp_out)
    m_in = dict(ln_mix_g=m_ln_mix_g, ln_mix_b=m_ln_mix_b, w_in=m_w_in, w_pool=m_w_pool, pool_scale=m_pool_scale,
                conv_w=m_conv_w, conv_b=m_conv_b, w_rg_a=m_w_rg_a, b_rg_a=m_b_rg_a, w_rg_i=m_w_rg_i, b_rg_i=m_b_rg_i,
                rg_lambda=m_rg_lambda, w_out=m_w_out, ln_ffn_g=m_ln_ffn_g, ln_ffn_b=m_ln_ffn_b, w_mlp_in=m_w_mlp_in,
                w_mlp_out=m_w_mlp_out)
    v_in = dict(ln_mix_g=v_ln_mix_g, ln_mix_b=v_ln_mix_b, w_in=v_w_in, w_pool=v_w_pool, pool_scale=v_pool_scale,
                conv_w=v_conv_w, conv_b=v_conv_b, w_rg_a=v_w_rg_a, b_rg_a=v_b_rg_a, w_rg_i=v_w_rg_i, b_rg_i=v_b_rg_i,
                rg_lambda=v_rg_lambda, w_out=v_w_out, ln_ffn_g=v_ln_ffn_g, ln_ffn_b=v_ln_ffn_b, w_mlp_in=v_w_mlp_in,
                w_mlp_out=v_w_mlp_out)
    names = list(weights)

    xs = x[0]
    tgt = loss_target[0]
    s, d = xs.shape
    p = c = d // 2
    pg = p // N_POOL_GROUPS
    core = lax.axis_index("c")
    shard = 2 * lax.axis_index("x") + lax.axis_index("y")

    idx = jnp.stack([core, shard]).astype(jnp.int32)
    small_shard = _pack([conv_w[0], b_rg_a[0], b_rg_i[0], rg_lambda[0]], 2 * SUBLANES)
    to_gather = [(w_in[0], BF16), (w_out[0], BF16), (w_mlp_in[0], BF16), (w_mlp_out[0], BF16),
                 (w_pool[0].reshape(-1, pg), BF16), (small_shard, F32)]

    def slot_view(i, dep):
        a, dt = to_gather[i]
        sl = _cast_to_slot_call(a, idx, dt, f"gather_slot_{i}", dep)
        return sl.reshape(N_CHIPS, 2, sl.shape[1] // 2, sl.shape[2])

    first, later = (0, 4, 5), (1, 2, 3)
    fly_a, sems_a, token_a = _start_copies_call(
        "gather_start_first", [slot_view(i, None) for i in first], [((0, 1, 2), _gather_copies, 3 * len(first))])
    later_views = []
    for i in later:
        later_views.append(slot_view(i, later_views[-1] if later_views else token_a))
    xb = _cast_call(xs, later_views[-1])
    got_first = _wait_copies_call("gather_wait_w_in", fly_a, sems_a[0], _gather_copies, xb)
    fly_b, sems_b, g_token = _start_copies_call(
        "gather_start_later", later_views + got_first,
        [((0,), _gather_copies, 3), ((1,), _gather_copies, 3), ((2,), _gather_copies, 3)])
    got_first = fly_b[len(later):]
    in_flight = dict(zip(later, fly_b))
    g_sems = [None] + list(sems_b)

    def arrive(which, group, after, tag):
        return _wait_copies_call(f"gather_wait_{tag}", [in_flight[w] for w in which], g_sems[group], _gather_copies, after)

    def pass_on(got, tag):
        flying, sems, token = _start_copies_call(
            f"gather_forward_start_{tag}", got, [(tuple(range(len(got))), _forward_copies, 3 * len(got))])
        return (flying, sems[0], tag), token

    def passed_on(state, after):
        flying, sems, tag = state
        return _wait_copies_call(f"gather_forward_wait_{tag}", flying, sems, _forward_copies, after)

    gathered = [None] * len(to_gather)
    gathered[0], gathered[4], gathered[5] = _forward_to_sibling_call(got_first, "gather_forward_w_in")
    w_in_f = gathered[0].reshape((N_CHIPS,) + w_in.shape[1:])
    w_pool_f = gathered[4].reshape(N_CHIPS, N_POOL_GROUPS, pg // N_CHIPS, pg).transpose(1, 0, 2, 3).reshape(N_POOL_GROUPS, pg, pg)
    c4 = c // N_CHIPS
    small_parts = [_unpack(gathered[5][k].reshape(-1, LANES), [(4, c4), (2, c4), (2, c4), (2, c4)]) for k in range(N_CHIPS)]
    conv_w_f = jnp.concatenate([sp_[0] for sp_ in small_parts], axis=1)
    b_a_f = jnp.concatenate([sp_[1] for sp_ in small_parts], axis=1)
    b_i_f = jnp.concatenate([sp_[2] for sp_ in small_parts], axis=1)
    lam_f = jnp.concatenate([sp_[3] for sp_ in small_parts], axis=1)
    wa_b = w_rg_a[0].astype(BF16)
    wi_b = w_rg_i[0].astype(BF16)

    proj = _proj_call(xb, w_in_f)
    xc = _conv_call(proj, conv_w_f, conv_b, c)
    fwd_w_out, token = pass_on(arrive((1,), 1, xc, "w_out"), "w_out")
    h_b, *gates_b = _scan_fwd_call(xc, wa_b[1], wi_b[1], b_a_f[1:2], b_i_f[1:2], lam_f[1:2], True, token)
    h_f, *gates_f = _scan_fwd_call(xc, wa_b[0], wi_b[0], b_a_f[0:1], b_i_f[0:1], lam_f[0:1], False, token)
    y, d_pool = _pool_combine_call(proj, h_f, h_b, w_pool_f, pool_scale, p)
    w_out_f = passed_on(fwd_w_out, y)[0].reshape(d, d)
    fwd_w1, token = pass_on(arrive((2,), 2, y, "w_mlp_in"), "w_mlp_in")
    xh1, x1b, rstd1 = _out_ln1_call(y, w_out_f, xs, ln_mix_g, ln_mix_b, token)
    w1_f = passed_on(fwd_w1, x1b)[0].reshape((N_CHIPS,) + w_mlp_in.shape[1:])
    first_half = _mlp_in_call(x1b, w1_f, g_token, None)
    fwd_w2, token = pass_on(arrive((3,), 3, first_half[0], "w_mlp_out"), "w_mlp_out")
    r_act, hsq = _mlp_in_call(x1b, w1_f, token, first_half)
    w2_f = passed_on(fwd_w2, hsq)[0].reshape(N_CHIPS * w_mlp_out.shape[1], d)
    dz2, dz2b, loss8, dg2, db2 = _mlp_out_ln2_call(hsq, w2_f, xh1, ln_mix_g, ln_mix_b, ln_ffn_g, ln_ffn_b, tgt)

    def start_siblings(grads, halves, tag, after=None):
        lands = [lax.empty(g.shape[1:] if halves else g.shape, g.dtype) for g in grads]
        copies = _sibling_copies(len(grads), halves)
        flying, sems, token = _start_copies_call(
            f"siblings_start_{tag}", list(grads) + lands, [(tuple(range(2 * len(grads))), copies, len(grads))], after)
        return (flying, sems[0], copies, len(grads), tag), token

    def finish_siblings(state, after):
        flying, sems, copies, n, tag = state
        got = _wait_copies_call(f"siblings_wait_{tag}", flying, sems, copies, after)
        return got[:n], got[n:]

    half_own = jnp.reshape(core, (1,)).astype(jnp.int32)
    half_sibling = 1 - half_own

    def chip_sum_of(a, b, row_sharded, tag, dep, overlapped):
        for_sibling = _half_grad_call(a, b, half_sibling, row_sharded, None, f"grad_{tag}_for_sibling", dep)
        state, token = start_siblings([for_sibling], False, tag)
        results = overlapped(token)
        _, (from_sibling,) = finish_siblings(state, results[0])
        return _half_grad_call(a, b, half_own, row_sharded, from_sibling, f"grad_{tag}", token), results

    def start_exchange(sums, n_repl, tag):
        n_sh = len(sums) - n_repl
        lands = [lax.empty((N_CHIPS - 1,) + a.shape[1:], a.dtype) for a in sums[:n_sh]]
        bufs = sums[:n_sh] + lands + sums[n_sh:]
        copies = _exchange_copies(n_sh, n_repl)
        flying, sems, token = _start_copies_call(
            f"reduce_start_{tag}", bufs, [(tuple(range(len(bufs))), copies, 3 * len(sums))])
        return (flying, sems[0], copies, n_sh, tag), token

    def finish_exchange(state, after):
        flying, sems, copies, n_sh, tag = state
        got = _wait_copies_call(f"reduce_wait_{tag}", flying, sems, copies, after)
        halves = []
        for a in range(n_sh):
            own, land = got[a], got[n_sh + a]
            cols = own.shape[-1]
            total = _sum_chips_call(own.reshape(N_CHIPS, -1, cols), land.reshape(N_CHIPS - 1, -1, cols), idx,
                                    f"reduce_sum_{tag}_{a}")
            halves.append(total.reshape((2,) + own.shape[1:]))
        for a, rp in enumerate(got[2 * n_sh:]):
            halves.append(_sum_chips_call(None, rp, idx, f"reduce_sum_{tag}_r{a}"))
        return halves

    def start_join(halves, tag):
        flying, sems, token = _start_copies_call(
            f"join_start_{tag}", halves, [(tuple(range(len(halves))), _join_copies, len(halves))])
        return (flying, sems[0], tag), token

    def finish_join(state, after):
        flying, sems, tag = state
        return _wait_copies_call(f"join_wait_{tag}", flying, sems, _join_copies, after)

    sum_w2, (dpre,) = chip_sum_of(hsq, dz2b, True, "w_mlp_out", g_token,
                                  lambda tok: (_dhsq_call(dz2b, w2_f, r_act, tok),))
    flying_w2, token = start_exchange([sum_w2], 0, "w2")
    sum_w1, (dz1, dz1b, dg1, db1) = chip_sum_of(
        x1b, dpre, False, "w_mlp_in", token,
        lambda tok: _dx1_ln1_bwd_call(dpre, w1_f, dz2, xh1, rstd1, ln_mix_g, tok))
    flying_w1, token = start_exchange([sum_w1], 0, "w1")

    sum_wout, (e_pool, dh, dgate, g_wpool, g_pscale8) = chip_sum_of(
        y, dz1b, True, "w_out", token,
        lambda tok: _mixer_bwd_call(dz1b, w_out_f, d_pool, proj, h_f, h_b, w_pool_f, pool_scale, p, tok))
    flying_wout, token = start_exchange([sum_wout], 0, "w_out")
    dxc0, g_wa0, g_wi0, g_ba0, g_bi0, g_sp0 = _scan_bwd_call(
        xc, dh, h_f, gates_f, None, wa_b[0], wi_b[0], lam_f[0:1], False, token)
    dxc, g_wa1, g_wi1, g_ba1, g_bi1, g_sp1 = _scan_bwd_call(
        xc, dh, h_b, gates_b, dxc0, wa_b[1], wi_b[1], lam_f[1:2], True, token)
    dproj, g_cw8, g_cb8 = _dproj_call(e_pool, dxc, proj, dgate, conv_w_f, p)

    rowsum = lambda a8: jnp.sum(a8, axis=-2)
    g_lam = jnp.stack([rowsum(g_sp0), rowsum(g_sp1)]) * (-_sigmoid(-lam_f))
    small_grads = {
        "ln_mix_g": rowsum(dg1), "ln_mix_b": rowsum(db1), "ln_ffn_g": rowsum(dg2), "ln_ffn_b": rowsum(db2),
        "pool_scale": rowsum(g_pscale8), "conv_b": rowsum(g_cb8),
        "w_rg_a": jnp.stack([g_wa0, g_wa1]), "w_rg_i": jnp.stack([g_wi0, g_wi1]),
        "w_pool": g_wpool, "conv_w": rowsum(g_cw8),
        "b_rg_a": jnp.stack([rowsum(g_ba0), rowsum(g_ba1)]), "b_rg_i": jnp.stack([rowsum(g_bi0), rowsum(g_bi1)]),
        "rg_lambda": g_lam,
    }
    small_names = list(small_grads)
    small_shapes = [small_grads[nm].shape for nm in small_names]
    loss_share = jnp.reshape(jnp.sum(loss8) * (0.5 / d), (1,))
    g_small = _halves(_pack([small_grads[nm] for nm in small_names] + [loss_share], 2 * SUBLANES))
    sib_small, token = start_siblings([g_small], True, "small")
    flying_small = []

    def small_exchange_and_grad_x(tok):
        (mine,), (theirs,) = finish_siblings(sib_small, tok)
        small_sum = _add_half_call(mine, theirs, idx, "reduce_add_small")
        state, tok = start_exchange([small_sum], 1, "small")
        flying_small.append(state)
        return (_dx_call(dproj, w_in_f, dz1, tok),)

    sum_win, (grad_x,) = chip_sum_of(xb, dproj, False, "w_in", token, small_exchange_and_grad_x)
    flying_small = flying_small[0]
    flying_win, token = start_exchange([sum_win], 0, "w_in")

    grad_w, delta_w, new_m, new_v = {}, {}, {}, {}

    def adamw(nm, full):
        w2d = weights[nm][0]
        g2d = full.reshape(w2d.shape)
        go, dl, mn, vn = _adamw_call(g2d, w2d, m_in[nm][0], v_in[nm][0], f"adamw_{nm}")
        grad_w[nm], delta_w[nm], new_m[nm], new_v[nm] = go[None], dl[None], mn[None], vn[None]
        return vn

    join_w2, token = start_join(finish_exchange(flying_w2, token), "w2")
    join_w1, token = start_join(finish_exchange(flying_w1, token), "w1")
    join_wout, token = start_join(finish_exchange(flying_wout, token), "w_out")
    last = adamw("w_mlp_out", finish_join(join_w2, token)[0])
    last = adamw("w_mlp_in", finish_join(join_w1, last)[0])
    last = adamw("w_out", finish_join(join_wout, last)[0])

    join_small, token = start_join(finish_exchange(flying_small, last), "small")
    join_win, token = start_join(finish_exchange(flying_win, token), "w_in")
    small_joined = finish_join(join_small, token)[0]
    *small_sums, loss_sum = _unpack(small_joined.reshape(-1, LANES), small_shapes + [(1,)])
    small_full = dict(zip(small_names, small_sums))
    local = dict(small_full)
    local["w_pool"] = lax.dynamic_slice_in_dim(small_full["w_pool"], shard * (pg // N_CHIPS), pg // N_CHIPS, axis=1)
    for nm in ("conv_w", "b_rg_a", "b_rg_i", "rg_lambda"):
        local[nm] = lax.dynamic_slice_in_dim(small_full[nm], shard * c4, c4, axis=1)
    small_g = [local[nm].reshape(weights[nm].shape) for nm in small_names]
    small_d, small_m, small_v = _adamw_small_call(
        small_g, [weights[nm] for nm in small_names], [m_in[nm] for nm in small_names], [v_in[nm] for nm in small_names])
    for nm, gl, dl, mn, vn in zip(small_names, small_g, small_d, small_m, small_v):
        grad_w[nm], delta_w[nm], new_m[nm], new_v[nm] = gl, dl, mn, vn
    adamw("w_in", finish_join(join_win, small_v[0])[0])

    loss = loss_sum[0]
    return (loss, grad_x[None], *[grad_w[nm] for nm in names], *[delta_w[nm] for nm in names],
            *[new_m[nm] for nm in names], *[new_v[nm] for nm in names])
```

```python
import jax
import jax.numpy as jnp
from jax import lax
from jax.experimental import pallas as pl
from jax.experimental.pallas import tpu as pltpu

F32 = jnp.float32
BF16 = jnp.bfloat16

N_CHIPS = 4
LANES = 128
SUBLANES = 8
LRU_HEAD = 128
N_POOL_GROUPS = 4
POOL_WINDOWS = (2, 4, 8, 16)
RG_C = 8.0
LN_EPS = 1e-5
ALPHA = 2.0 ** 0.25
ADAM_LR, ADAM_B1, ADAM_B2, ADAM_EPS, ADAM_WD, ADAM_STEP = 0.001, 0.9, 0.999, 1e-08, 0.01, 10
VMEM_LIMIT = 56 * 1024 * 1024
SEQ_TILE = 512
MM_TILE = 512
LN_MM_K = 2048
LN_UNROLL = 8
ELT_BLOCK_BYTES = 2 * 1024 * 1024
RESIDENT_OPERAND_BYTES = 16 * 1024 * 1024
MESH = pl.DeviceIdType.MESH
ANY = pl.BlockSpec(memory_space=pl.ANY)


def _params(*sem):
    return pltpu.CompilerParams(dimension_semantics=sem, vmem_limit_bytes=VMEM_LIMIT)


def _sigmoid(z):
    return 1.0 / (1.0 + jnp.exp(-z))


def _neg_expm1(z):
    series = -(z * (1.0 + z * (0.5 + z * (1.0 / 6.0 + z * (1.0 / 24.0)))))
    return jnp.where(z > -0.01, series, 1.0 - jnp.exp(z))


def _softplus(z):
    return jnp.maximum(z, 0.0) + jnp.log1p(jnp.exp(-jnp.abs(z)))


_GELU_K = 0.7978845608028654
_GELU_C = 0.044715


def _gelu_and_grad(u):
    t = jnp.tanh(_GELU_K * (u + _GELU_C * (u * u * u)))
    g = 0.5 * u * (1.0 + t)
    dg = 0.5 * (1.0 + t) + 0.5 * u * (1.0 - t * t) * (_GELU_K * (1.0 + 3.0 * _GELU_C * u * u))
    return g, dg


def _shift_rows(prv, cur, nxt, o, rows):
    if o == 0:
        return cur
    if o == SUBLANES:
        return nxt
    if o == -SUBLANES:
        return prv
    if o > 0:
        return pltpu.roll(jnp.where(rows >= o, cur, nxt), SUBLANES - o, 0)
    p = -o
    return pltpu.roll(jnp.where(rows < SUBLANES - p, cur, prv), p, 0)


def _neighbour_chunks(main_ref, prev_ref, next_ref, r0, t_rows, cols, first_tile, last_tile):
    cur = main_ref[pl.ds(r0, SUBLANES), cols]
    before = main_ref[pl.ds(pl.multiple_of(jnp.maximum(r0 - SUBLANES, 0), SUBLANES), SUBLANES), cols]
    after = main_ref[pl.ds(pl.multiple_of(jnp.minimum(r0 + SUBLANES, t_rows - SUBLANES), SUBLANES), SUBLANES), cols]
    halo_prev = jnp.where(first_tile, 0.0, prev_ref[:, cols])
    halo_next = jnp.where(last_tile, 0.0, next_ref[:, cols])
    prv = jnp.where(r0 == 0, halo_prev, before)
    nxt = jnp.where(r0 == t_rows - SUBLANES, halo_next, after)
    return prv, cur, nxt


def _halo_specs(t_rows, n_rows, width, col_block):
    per = t_rows // SUBLANES
    last = n_rows // SUBLANES - 1
    return [
        pl.BlockSpec((t_rows, width), lambda i: (i, col_block)),
        pl.BlockSpec((SUBLANES, width), lambda i: (jnp.maximum(i * per - 1, 0), col_block)),
        pl.BlockSpec((SUBLANES, width), lambda i: (jnp.minimum((i + 1) * per, last), col_block)),
    ]


def _chunk_loop(t_rows, fn, init=None, unroll=1, descending=False):
    span = SUBLANES * unroll

    def step(ci, carry):
        base = pl.multiple_of(((t_rows // span - 1 - ci) if descending else ci) * span, span)
        for u in range(unroll):
            carry = fn(base + ((unroll - 1 - u) if descending else u) * SUBLANES, carry)
        return carry
    return lax.fori_loop(0, t_rows // span, step, init)


def _scan_chunk(a, b, h_in, rows, reverse):
    for dist in (1, 2, 4):
        if reverse:
            keep = rows < SUBLANES - dist
            shift = SUBLANES - dist
        else:
            keep = rows >= dist
            shift = dist
        b = a * jnp.where(keep, pltpu.roll(b, shift, 0), 0.0) + b
        a = a * jnp.where(keep, pltpu.roll(a, shift, 0), 1.0)
    return a * h_in + b


def _cast_call(x, dep):
    s, d = x.shape
    tm = min(MM_TILE, s)

    def body(x_ref, dep_ref, o_ref):
        o_ref[...] = x_ref[...].astype(BF16)

    return pl.pallas_call(
        body, name="cast_x", grid=(s // tm,),
        in_specs=[pl.BlockSpec((tm, d), lambda i: (i, 0)), ANY],
        out_specs=pl.BlockSpec((tm, d), lambda i: (i, 0)),
        out_shape=jax.ShapeDtypeStruct((s, d), BF16),
        compiler_params=_params("arbitrary"),
    )(x, dep)


def _proj_call(xb, w_in):
    s, d = xb.shape
    n, _, e4 = w_in.shape
    tm = min(2 * MM_TILE, s)

    def body(x_ref, w_hbm, proj_ref, w_s, sems):
        @pl.when(pl.program_id(0) == 0)
        def _():
            copies = [pltpu.make_async_copy(w_hbm.at[k], w_s.at[:, pl.ds(k * e4, e4)], sems.at[k]) for k in range(n)]
            for cp in copies:
                cp.start()
            for cp in copies:
                cp.wait()

        proj_ref[...] = jnp.dot(x_ref[...], w_s[...], preferred_element_type=F32)

    return pl.pallas_call(
        body, name="proj", grid=(s // tm,),
        in_specs=[pl.BlockSpec((tm, d), lambda i: (i, 0)), ANY],
        out_specs=pl.BlockSpec((tm, n * e4), lambda i: (i, 0)),
        out_shape=jax.ShapeDtypeStruct((s, n * e4), F32),
        scratch_shapes=[pltpu.VMEM((d, n * e4), BF16), pltpu.SemaphoreType.DMA((n,))],
        compiler_params=_params("arbitrary"),
    )(xb, w_in)


def _conv_call(proj, conv_w, conv_b, c):
    s = proj.shape[0]
    t = min(SEQ_TILE, s)
    n_tiles = s // t

    def body(u_ref, up_ref, un_ref, w_ref, b_ref, xc_ref):
        i = pl.program_id(0)
        rows = lax.broadcasted_iota(jnp.int32, (SUBLANES, c), 0)
        w = w_ref[...]
        b = b_ref[...]

        def chunk(r0, _):
            prv, cur, nxt = _neighbour_chunks(u_ref, up_ref, un_ref, r0, t, slice(None), i == 0, i == n_tiles - 1)
            acc = b + w[1:2] * cur
            acc += w[0:1] * _shift_rows(prv, cur, nxt, -1, rows)
            acc += w[2:3] * _shift_rows(prv, cur, nxt, 1, rows)
            acc += w[3:4] * _shift_rows(prv, cur, nxt, 2, rows)
            xc_ref[pl.ds(r0, SUBLANES), :] = acc

        _chunk_loop(t, chunk)

    return pl.pallas_call(
        body, name="conv_fwd", grid=(n_tiles,),
        in_specs=_halo_specs(t, s, c, 1) + [pl.BlockSpec((4, c), lambda i: (0, 0)), pl.BlockSpec((1, c), lambda i: (0, 0))],
        out_specs=pl.BlockSpec((t, c), lambda i: (i, 0)),
        out_shape=jax.ShapeDtypeStruct((s, c), F32),
        compiler_params=_params("arbitrary"),
    )(proj, proj, proj, conv_w, conv_b)


def _gate_matmuls(xc_ref, wa_ref, wi_ref, pr_s, pi_s, heads):
    for h in range(heads):
        cs = pl.ds(h * LRU_HEAD, LRU_HEAD)
        xb = xc_ref[:, cs].astype(BF16)
        pr_s[:, cs] = jnp.dot(xb, wa_ref[h], preferred_element_type=F32)
        pi_s[:, cs] = jnp.dot(xb, wi_ref[h], preferred_element_type=F32)


def _rg_gates(pr, pi, ba, bi, sp):
    r = _sigmoid(pr + ba)
    ig = _sigmoid(pi + bi)
    log_a = (-RG_C * r) * sp
    a = jnp.exp(log_a)
    mult = jnp.sqrt(_neg_expm1(2.0 * log_a))
    return r, ig, a, mult


def _scan_fwd_call(xc, wa, wi, ba, bi, lam, reverse, dep):
    s, c = xc.shape
    heads = c // LRU_HEAD
    t = min(SEQ_TILE, s)
    n_tiles = s // t
    tile = (lambda i: (n_tiles - 1 - i, 0)) if reverse else (lambda i: (i, 0))
    whole2 = lambda i: (0, 0)
    whole3 = lambda i: (0, 0, 0)

    def body(xc_ref, wa_ref, wi_ref, ba_ref, bi_ref, lam_ref, dep_ref, h_ref, r_ref, ig_ref, a_ref, mult_ref,
             pr_s, pi_s, carry_s):
        @pl.when(pl.program_id(0) == 0)
        def _():
            carry_s[...] = jnp.zeros_like(carry_s)

        _gate_matmuls(xc_ref, wa_ref, wi_ref, pr_s, pi_s, heads)
        ba_v, bi_v = ba_ref[...], bi_ref[...]
        sp = _softplus(-lam_ref[...])

        rows = lax.broadcasted_iota(jnp.int32, (SUBLANES, c), 0)

        def chunk(r0, h_in):
            rs = pl.ds(r0, SUBLANES)
            r, ig, a, mult = _rg_gates(pr_s[rs, :], pi_s[rs, :], ba_v, bi_v, sp)
            r_ref[rs, :] = r
            ig_ref[rs, :] = ig
            a_ref[rs, :] = a
            mult_ref[rs, :] = mult
            h = _scan_chunk(a, mult * ig * xc_ref[rs, :], h_in, rows, reverse)
            h_ref[rs, :] = h
            return h[0:1, :] if reverse else h[SUBLANES - 1:SUBLANES, :]

        carry_s[...] = _chunk_loop(t, chunk, carry_s[...], unroll=2, descending=reverse)

    return pl.pallas_call(
        body, name="scan_fwd_rev" if reverse else "scan_fwd", grid=(n_tiles,),
        in_specs=[pl.BlockSpec((t, c), tile),
                  pl.BlockSpec((heads, LRU_HEAD, LRU_HEAD), whole3), pl.BlockSpec((heads, LRU_HEAD, LRU_HEAD), whole3),
                  pl.BlockSpec((1, c), whole2), pl.BlockSpec((1, c), whole2), pl.BlockSpec((1, c), whole2), ANY],
        out_specs=[pl.BlockSpec((t, c), tile)] * 5,
        out_shape=[jax.ShapeDtypeStruct((s, c), F32)] * 5,
        scratch_shapes=[pltpu.VMEM((t, c), F32), pltpu.VMEM((t, c), F32), pltpu.VMEM((1, c), F32)],
        compiler_params=_params("arbitrary"),
    )(xc, wa, wi, ba, bi, lam, dep)


def _window_counts(r0, tile_idx, t_rows, n_rows, half, shape):
    pos = tile_idx * t_rows + r0 + lax.broadcasted_iota(jnp.int32, shape, 0)
    hi = jnp.minimum(pos + half, n_rows)
    lo = jnp.maximum(pos - half, 0)
    return (hi - lo).astype(F32)


def _window_inverse_counts(r0, tile_idx, t_rows, n_rows, half, width):
    inv = 1.0 / _window_counts(r0, tile_idx, t_rows, n_rows, half, (SUBLANES, LANES))
    return jnp.tile(inv, (1, width // LANES))


def _pool_combine_call(proj, h_f, h_b, w_pool, pool_scale, p):
    s = proj.shape[0]
    c = h_f.shape[1]
    pg = p // N_POOL_GROUPS
    t = min(SEQ_TILE, s)
    n_tiles = s // t

    def body(u_ref, up_ref, un_ref, gate_ref, hf_ref, hb_ref, wp_ref, sc_ref, y_ref, d_ref, d_s, yr_s):
        i = pl.program_id(0)
        rows = lax.broadcasted_iota(jnp.int32, (SUBLANES, pg), 0)

        def chunk(r0, _):
            rs = pl.ds(r0, SUBLANES)
            for g, w in enumerate(POOL_WINDOWS):
                cols = pl.ds(g * pg, pg)
                prv, cur, nxt = _neighbour_chunks(u_ref, up_ref, un_ref, r0, t, cols, i == 0, i == n_tiles - 1)
                tot = cur
                for o in range(-(w // 2), w // 2):
                    if o != 0:
                        tot = tot + _shift_rows(prv, cur, nxt, o, rows)
                d_s[rs, cols] = tot * _window_inverse_counts(r0, i, t, s, w // 2, pg) - cur
            gate, _ = _gelu_and_grad(gate_ref[rs, :])
            yr_s[rs, :] = (hf_ref[rs, :] + hb_ref[rs, :]) * gate

        _chunk_loop(t, chunk)
        y_ref[:, pl.ds(p, c)] = yr_s[...].astype(BF16)
        d_ref[...] = d_s[...].astype(BF16)
        for g in range(N_POOL_GROUPS):
            cols = pl.ds(g * pg, pg)
            out = jnp.dot(d_s[:, cols].astype(BF16), wp_ref[g], preferred_element_type=F32)
            y_ref[:, cols] = (out * sc_ref[:, cols]).astype(BF16)

    return pl.pallas_call(
        body, name="pool_combine", grid=(n_tiles,),
        in_specs=_halo_specs(t, s, p, 0) + [
            pl.BlockSpec((t, c), lambda i: (i, 2)),
            pl.BlockSpec((t, c), lambda i: (i, 0)), pl.BlockSpec((t, c), lambda i: (i, 0)),
            pl.BlockSpec((N_POOL_GROUPS, pg, pg), lambda i: (0, 0, 0)), pl.BlockSpec((1, p), lambda i: (0, 0))],
        out_specs=[pl.BlockSpec((t, p + c), lambda i: (i, 0)), pl.BlockSpec((t, p), lambda i: (i, 0))],
        out_shape=[jax.ShapeDtypeStruct((s, p + c), BF16), jax.ShapeDtypeStruct((s, p), BF16)],
        scratch_shapes=[pltpu.VMEM((t, p), F32), pltpu.VMEM((t, c), F32)],
        compiler_params=_params("arbitrary"),
    )(proj, proj, proj, proj, h_f, h_b, w_pool, pool_scale)


def _layer_norm_rows(z, g, b):
    mu = jnp.mean(z, axis=-1, keepdims=True)
    zc = z - mu
    var = jnp.mean(zc * zc, axis=-1, keepdims=True)
    rstd = lax.rsqrt(var + LN_EPS)
    xh = zc * rstd
    return xh, rstd, xh * g + b


def _layer_norm_bwd_rows(dx, xh, rstd, g):
    dxh = dx * g
    m1 = jnp.mean(dxh, axis=-1, keepdims=True)
    m2 = jnp.mean(dxh * xh, axis=-1, keepdims=True)
    return rstd * (dxh - m1 - xh * m2)


def _out_ln1_call(y, w_out, x, g1, b1, dep):
    s, d = x.shape
    tm = min(SEQ_TILE, s)

    def body(y_ref, w_ref, x_ref, g_ref, b_ref, dep_ref, xh_ref, x1b_ref, rstd_ref, acc_s, x1_s):
        acc_s[...] = jnp.dot(y_ref[...], w_ref[...], preferred_element_type=F32)
        g, b = g_ref[...], b_ref[...]

        def chunk(r0, _):
            rs = pl.ds(r0, SUBLANES)
            xh, rstd, x1 = _layer_norm_rows(ALPHA * x_ref[rs, :] + acc_s[rs, :], g, b)
            xh_ref[rs, :] = xh
            x1_s[rs, :] = x1
            rstd_ref[rs, :] = rstd

        _chunk_loop(tm, chunk, unroll=LN_UNROLL)
        x1b_ref[...] = x1_s[...].astype(BF16)

    return pl.pallas_call(
        body, name="out_ln1", grid=(s // tm,),
        in_specs=[pl.BlockSpec((tm, d), lambda i: (i, 0)), pl.BlockSpec((d, d), lambda i: (0, 0)),
                  pl.BlockSpec((tm, d), lambda i: (i, 0)),
                  pl.BlockSpec((1, d), lambda i: (0, 0)), pl.BlockSpec((1, d), lambda i: (0, 0)), ANY],
        out_specs=[pl.BlockSpec((tm, d), lambda i: (i, 0)), pl.BlockSpec((tm, d), lambda i: (i, 0)),
                   pl.BlockSpec((tm, 1), lambda i: (i, 0))],
        out_shape=[jax.ShapeDtypeStruct((s, d), F32), jax.ShapeDtypeStruct((s, d), BF16), jax.ShapeDtypeStruct((s, 1), F32)],
        scratch_shapes=[pltpu.VMEM((tm, d), F32), pltpu.VMEM((tm, d), F32)],
        compiler_params=_params("arbitrary"),
    )(y, w_out, x, g1, b1, dep)


def _mlp_in_call(x1b, w1, dep, done):
    s, d = x1b.shape
    n, _, f4 = w1.shape
    tm = min(2 * MM_TILE, s)
    tn = min(1024, f4)
    per = f4 // tn
    blocks = n * per // 2
    first = 0 if done is None else blocks
    extra = [] if done is None else list(done)

    def body(x_ref, w_ref, dep_ref, *rest):
        r_ref, q_ref = rest[-2:]
        r = jnp.maximum(jnp.dot(x_ref[...], w_ref[...], preferred_element_type=F32), 0.0)
        r_ref[...] = r.astype(BF16)
        q_ref[...] = (r * r).astype(BF16)

    out_spec = pl.BlockSpec((tm, tn), lambda j, i: (i, first + j))
    return pl.pallas_call(
        body, name="mlp_in" if done is None else "mlp_in_rest", grid=(blocks, s // tm),
        in_specs=[pl.BlockSpec((tm, d), lambda j, i: (i, 0)),
                  pl.BlockSpec((None, d, tn), lambda j, i: ((first + j) // per, 0, (first + j) % per)), ANY] + [ANY] * len(extra),
        out_specs=[out_spec, out_spec],
        out_shape=[jax.ShapeDtypeStruct((s, n * f4), BF16), jax.ShapeDtypeStruct((s, n * f4), BF16)],
        input_output_aliases={3: 0, 4: 1} if extra else {},
        compiler_params=_params("arbitrary", "arbitrary"),
    )(x1b, w1, dep, *extra)


def _mlp_out_ln2_call(hsq, w2, xh1, g1, b1, g2, b2, target):
    s, f = hsq.shape
    d = w2.shape[1]
    tm = min(MM_TILE, s)
    tk = min(LN_MM_K, f)
    nk = f // tk

    def body(h_ref, w_ref, xh1_ref, g1_ref, b1_ref, g2_ref, b2_ref, t_ref,
             dz_ref, dzb_ref, loss_ref, dg_ref, db_ref, acc_s):
        i, k = pl.program_id(0), pl.program_id(1)

        @pl.when((i == 0) & (k == 0))
        def _():
            loss_ref[...] = jnp.zeros_like(loss_ref)
            dg_ref[...] = jnp.zeros_like(dg_ref)
            db_ref[...] = jnp.zeros_like(db_ref)

        @pl.when(k == 0)
        def _():
            acc_s[...] = jnp.zeros_like(acc_s)

        acc_s[...] += jnp.dot(h_ref[...], w_ref[...], preferred_element_type=F32)

        @pl.when(k == nk - 1)
        def _():
            g1, b1, g2, b2 = g1_ref[...], b1_ref[...], g2_ref[...], b2_ref[...]

            def chunk(r0, _):
                rs = pl.ds(r0, SUBLANES)
                x1 = xh1_ref[rs, :] * g1 + b1
                xh2, rstd, x2 = _layer_norm_rows(ALPHA * x1 + acc_s[rs, :], g2, b2)
                diff = x2 - t_ref[rs, :]
                loss_ref[...] += diff * diff
                dx2 = diff * (1.0 / d)
                dg_ref[...] += dx2 * xh2
                db_ref[...] += dx2
                dz = _layer_norm_bwd_rows(dx2, xh2, rstd, g2)
                dz_ref[rs, :] = dz

            _chunk_loop(tm, chunk, unroll=LN_UNROLL)
            dzb_ref[...] = dz_ref[...].astype(BF16)

    row = lambda i, k: (i, 0)
    vec = lambda i, k: (0, 0)
    return pl.pallas_call(
        body, name="mlp_out_ln2", grid=(s // tm, nk),
        in_specs=[pl.BlockSpec((tm, tk), lambda i, k: (i, k)), pl.BlockSpec((tk, d), lambda i, k: (k, 0)),
                  pl.BlockSpec((tm, d), row), pl.BlockSpec((1, d), vec), pl.BlockSpec((1, d), vec),
                  pl.BlockSpec((1, d), vec), pl.BlockSpec((1, d), vec), pl.BlockSpec((tm, d), row)],
        out_specs=[pl.BlockSpec((tm, d), row), pl.BlockSpec((tm, d), row),
                   pl.BlockSpec((SUBLANES, d), vec), pl.BlockSpec((SUBLANES, d), vec), pl.BlockSpec((SUBLANES, d), vec)],
        out_shape=[jax.ShapeDtypeStruct((s, d), F32), jax.ShapeDtypeStruct((s, d), BF16),
                   jax.ShapeDtypeStruct((SUBLANES, d), F32), jax.ShapeDtypeStruct((SUBLANES, d), F32),
                   jax.ShapeDtypeStruct((SUBLANES, d), F32)],
        scratch_shapes=[pltpu.VMEM((tm, d), F32)],
        compiler_params=_params("arbitrary", "arbitrary"),
    )(hsq, w2, xh1, g1, b1, g2, b2, target)


def _half_grad_call(a, b, half, row_sharded, init, name, dep):
    s, m = a.shape
    n = b.shape[1]
    if row_sharded:
        rows, cols = m // (2 * N_CHIPS), n
        tm = min(1024, rows)
        per = rows // tm
        tn = min(1024, cols)
        n_i, n_j = N_CHIPS * per, cols // tn
        a_block = lambda i, h: ((i // per) * 2 + h) * per + i % per
        out_block = lambda i, j: (i // per, i % per, j)
    else:
        rows, cols = m // 2, n // N_CHIPS
        tm = min(1024, rows)
        per = rows // tm
        tn = cols if cols % 1024 else 1024
        per_n = cols // tn
        n_i, n_j = per, N_CHIPS * per_n
        a_block = lambda i, h: h * per + i
        out_block = lambda i, j: (j // per_n, i, j % per_n)
    tk = min(2048, s)
    if row_sharded and tm < 1024 and s * n * 2 <= RESIDENT_OPERAND_BYTES:
        tk, tn, n_j = s, n, 1
    has_init = init is not None

    def body(half_ref, a_ref, b_ref, *rest):
        o_ref = rest[-1]

        @pl.when(pl.program_id(2) == 0)
        def _():
            o_ref[...] = rest[0][...] if has_init else jnp.zeros_like(o_ref)

        o_ref[...] += lax.dot_general(a_ref[...], b_ref[...], (((0,), (0,)), ((), ())), preferred_element_type=F32)

    out_spec = pl.BlockSpec((None, tm, tn), lambda i, j, k, h: out_block(i, j))
    in_specs = [pl.BlockSpec((tk, tm), lambda i, j, k, h: (k, a_block(i, h[0]))),
                pl.BlockSpec((tk, tn), lambda i, j, k, h: (k, j))]
    args = [a, b]
    if has_init:
        in_specs.append(out_spec)
        args.append(init)
    in_specs.append(ANY)
    args.append(dep)
    return pl.pallas_call(
        body, name=name,
        grid_spec=pltpu.PrefetchScalarGridSpec(num_scalar_prefetch=1, grid=(n_i, n_j, s // tk), in_specs=in_specs,
                                               out_specs=out_spec),
        out_shape=jax.ShapeDtypeStruct((N_CHIPS, rows, cols), F32),
        compiler_params=_params("arbitrary", "arbitrary", "arbitrary"),
    )(half, *args)


def _dhsq_call(dzb, w2, r, dep):
    s, d = dzb.shape
    f = w2.shape[0]
    tm = min(2 * MM_TILE, s)
    tn = min(1024, f)

    def body(dz_ref, w_ref, r_ref, dep_ref, o_ref):
        dh = lax.dot_general(dz_ref[...], w_ref[...], (((1,), (1,)), ((), ())), preferred_element_type=F32)
        o_ref[...] = (dh * (2.0 * r_ref[...].astype(F32))).astype(BF16)

    return pl.pallas_call(
        body, name="mlp_dpre", grid=(f // tn, s // tm),
        in_specs=[pl.BlockSpec((tm, d), lambda j, i: (i, 0)), pl.BlockSpec((tn, d), lambda j, i: (j, 0)),
                  pl.BlockSpec((tm, tn), lambda j, i: (i, j)), ANY],
        out_specs=pl.BlockSpec((tm, tn), lambda j, i: (i, j)),
        out_shape=jax.ShapeDtypeStruct((s, f), BF16),
        compiler_params=_params("arbitrary", "arbitrary"),
    )(dzb, w2, r, dep)


def _dx1_ln1_bwd_call(dpre, w1, dz2, xh1, rstd1, g1, dep):
    s, f = dpre.shape
    n, d, f4 = w1.shape
    tm = min(MM_TILE, s)
    tk = min(LN_MM_K, f4)
    per = f4 // tk
    nk = n * per

    def body(dp_ref, w_ref, dz2_ref, xh_ref, rstd_ref, g_ref, dep_ref, dz_ref, dzb_ref, dg_ref, db_ref, acc_s):
        i, k = pl.program_id(0), pl.program_id(1)

        @pl.when((i == 0) & (k == 0))
        def _():
            dg_ref[...] = jnp.zeros_like(dg_ref)
            db_ref[...] = jnp.zeros_like(db_ref)

        @pl.when(k == 0)
        def _():
            acc_s[...] = jnp.zeros_like(acc_s)

        acc_s[...] += lax.dot_general(dp_ref[...], w_ref[...], (((1,), (1,)), ((), ())), preferred_element_type=F32)

        @pl.when(k == nk - 1)
        def _():
            g = g_ref[...]

            def chunk(r0, _):
                rs = pl.ds(r0, SUBLANES)
                dx1 = acc_s[rs, :] + ALPHA * dz2_ref[rs, :]
                xh = xh_ref[rs, :]
                dg_ref[...] += dx1 * xh
                db_ref[...] += dx1
                dz = _layer_norm_bwd_rows(dx1, xh, rstd_ref[rs, :], g)
                dz_ref[rs, :] = dz

            _chunk_loop(tm, chunk, unroll=LN_UNROLL)
            dzb_ref[...] = dz_ref[...].astype(BF16)

    row = lambda i, k: (i, 0)
    vec = lambda i, k: (0, 0)
    return pl.pallas_call(
        body, name="dx1_ln1_bwd", grid=(s // tm, nk),
        in_specs=[pl.BlockSpec((tm, tk), lambda i, k: (i, k)),
                  pl.BlockSpec((None, d, tk), lambda i, k: (k // per, 0, k % per)),
                  pl.BlockSpec((tm, d), row), pl.BlockSpec((tm, d), row), pl.BlockSpec((tm, 1), row),
                  pl.BlockSpec((1, d), vec), ANY],
        out_specs=[pl.BlockSpec((tm, d), row), pl.BlockSpec((tm, d), row),
                   pl.BlockSpec((SUBLANES, d), vec), pl.BlockSpec((SUBLANES, d), vec)],
        out_shape=[jax.ShapeDtypeStruct((s, d), F32), jax.ShapeDtypeStruct((s, d), BF16),
                   jax.ShapeDtypeStruct((SUBLANES, d), F32), jax.ShapeDtypeStruct((SUBLANES, d), F32)],
        scratch_shapes=[pltpu.VMEM((tm, d), F32)],
        compiler_params=_params("arbitrary", "arbitrary"),
    )(dpre, w1, dz2, xh1, rstd1, g1, dep)


def _mixer_bwd_call(dzb, w_out, d_pool, proj, h_f, h_b, w_pool, pool_scale, p, dep):
    s, d = dzb.shape
    c = h_f.shape[1]
    pg = p // N_POOL_GROUPS
    t = min(SEQ_TILE, s)
    n_tiles = s // t

    def body(dz_ref, wo_ref, d_ref, gate_ref, hf_ref, hb_ref, wp_ref, sc_ref, dep_ref,
             e_ref, dh_ref, dgate_ref, dwp_ref, dsc_ref, dd_s, dy_s):
        i = pl.program_id(0)

        @pl.when(i == 0)
        def _():
            dwp_ref[...] = jnp.zeros_like(dwp_ref)
            dsc_ref[...] = jnp.zeros_like(dsc_ref)

        dy_s[...] = lax.dot_general(dz_ref[...], wo_ref[...], (((1,), (1,)), ((), ())), preferred_element_type=F32)

        for g in range(N_POOL_GROUPS):
            cols = pl.ds(g * pg, pg)
            dg = d_ref[:, cols]
            out = jnp.dot(dg, wp_ref[g], preferred_element_type=F32)
            dyp = dy_s[:, cols]
            prod = dyp * out
            dsc_ref[:, cols] += jnp.sum(prod.reshape(t // SUBLANES, SUBLANES, pg), axis=0)
            dout = (dyp * sc_ref[:, cols]).astype(BF16)
            dwp_ref[g] += lax.dot_general(dg, dout, (((0,), (0,)), ((), ())), preferred_element_type=F32)
            dd_s[:, cols] = lax.dot_general(dout, wp_ref[g], (((1,), (1,)), ((), ())), preferred_element_type=F32)

        def chunk(r0, _):
            rs = pl.ds(r0, SUBLANES)
            for g, w in enumerate(POOL_WINDOWS):
                cols = pl.ds(g * pg, pg)
                e_ref[rs, cols] = dd_s[rs, cols] * _window_inverse_counts(r0, i, t, s, w // 2, pg)
            gate, dgate = _gelu_and_grad(gate_ref[rs, :])
            dyr = dy_s[rs, pl.ds(p, c)]
            dh_ref[rs, :] = dyr * gate
            dd_s[rs, :] = dyr * (hf_ref[rs, :] + hb_ref[rs, :]) * dgate

        _chunk_loop(t, chunk)
        dgate_ref[...] = dd_s[...].astype(BF16)

    tile = lambda i: (i, 0)
    return pl.pallas_call(
        body, name="mixer_bwd", grid=(n_tiles,),
        in_specs=[pl.BlockSpec((t, d), tile), pl.BlockSpec((p + c, d), lambda i: (0, 0)), pl.BlockSpec((t, p), tile),
                  pl.BlockSpec((t, c), lambda i: (i, 2)), pl.BlockSpec((t, c), tile), pl.BlockSpec((t, c), tile),
                  pl.BlockSpec((N_POOL_GROUPS, pg, pg), lambda i: (0, 0, 0)), pl.BlockSpec((1, p), lambda i: (0, 0)), ANY],
        out_specs=[pl.BlockSpec((t, p), tile), pl.BlockSpec((t, c), tile), pl.BlockSpec((t, c), tile),
                   pl.BlockSpec((N_POOL_GROUPS, pg, pg), lambda i: (0, 0, 0)), pl.BlockSpec((SUBLANES, p), lambda i: (0, 0))],
        out_shape=[jax.ShapeDtypeStruct((s, p), F32), jax.ShapeDtypeStruct((s, c), F32), jax.ShapeDtypeStruct((s, c), BF16),
                   jax.ShapeDtypeStruct((N_POOL_GROUPS, pg, pg), F32), jax.ShapeDtypeStruct((SUBLANES, p), F32)],
        scratch_shapes=[pltpu.VMEM((t, p), F32), pltpu.VMEM((t, p + c), F32)],
        compiler_params=_params("arbitrary"),
    )(dzb, w_out, d_pool, proj, h_f, h_b, w_pool, pool_scale, dep)


def _scan_bwd_call(xc, dh, h_dir, gates, dxc_prev, wa, wi, lam, reverse, dep):
    s, c = xc.shape
    heads = c // LRU_HEAD
    t = min(SEQ_TILE, s)
    n_tiles = s // t
    per = t // SUBLANES
    last_blk = s // SUBLANES - 1
    tile = (lambda i: (i, 0)) if reverse else (lambda i: (n_tiles - 1 - i, 0))
    if reverse:
        halo = lambda i: (jnp.minimum((i + 1) * per, last_blk), 0)
    else:
        halo = lambda i: (jnp.maximum((n_tiles - 1 - i) * per - 1, 0), 0)
    whole2 = lambda i: (0, 0)
    whole3 = lambda i: (0, 0, 0)
    has_prev = dxc_prev is not None
    n_in = 11 + int(has_prev) + 1

    def body(*refs):
        xc_ref, dh_ref, h_ref, hh_ref, r_ref, ig_ref, a_ref, mult_ref = refs[:8]
        prev_ref = refs[8] if has_prev else None
        wa_ref, wi_ref, lam_ref = refs[n_in - 4:n_in - 1]
        dxc_ref, dwa_ref, dwi_ref, dba_ref, dbi_ref, dsp_ref = refs[n_in:n_in + 6]
        pr_s, pi_s, carry_s = refs[n_in + 6:]
        step = pl.program_id(0)
        tile_idx = step if reverse else n_tiles - 1 - step

        @pl.when(step == 0)
        def _():
            carry_s[...] = jnp.zeros_like(carry_s)
            dwa_ref[...] = jnp.zeros_like(dwa_ref)
            dwi_ref[...] = jnp.zeros_like(dwi_ref)
            dba_ref[...] = jnp.zeros_like(dba_ref)
            dbi_ref[...] = jnp.zeros_like(dbi_ref)
            dsp_ref[...] = jnp.zeros_like(dsp_ref)

        sp = _softplus(-lam_ref[...])
        rows = lax.broadcasted_iota(jnp.int32, (SUBLANES, c), 0)

        def chunk(r0, u_in):
            rs = pl.ds(r0, SUBLANES)
            xcv = xc_ref[rs, :]
            r, ig, a, mult = r_ref[rs, :], ig_ref[rs, :], a_ref[rs, :], mult_ref[rs, :]
            dhv = dh_ref[rs, :]
            u = _scan_chunk(a, a * dhv, u_in, rows, not reverse)
            if reverse:
                gt = dhv + jnp.where(rows >= 1, pltpu.roll(u, 1, 0), u_in)
                u_out = u[SUBLANES - 1:SUBLANES, :]
            else:
                gt = dhv + jnp.where(rows < SUBLANES - 1, pltpu.roll(u, SUBLANES - 1, 0), u_in)
                u_out = u[0:1, :]
            cur = h_ref[rs, :]
            if reverse:
                after = h_ref[pl.ds(pl.multiple_of(jnp.minimum(r0 + SUBLANES, t - SUBLANES), SUBLANES), SUBLANES), :]
                edge = jnp.where(tile_idx == n_tiles - 1, 0.0, hh_ref[...])
                nxt = jnp.where(r0 == t - SUBLANES, edge, after)
                hs = _shift_rows(cur, cur, nxt, 1, rows)
            else:
                before = h_ref[pl.ds(pl.multiple_of(jnp.maximum(r0 - SUBLANES, 0), SUBLANES), SUBLANES), :]
                edge = jnp.where(tile_idx == 0, 0.0, hh_ref[...])
                prv = jnp.where(r0 == 0, edge, before)
                hs = _shift_rows(prv, cur, cur, -1, rows)
            gx = gt * xcv
            dmult = gx * ig
            di = gx * mult
            dlog_a = (gt * hs) * a - dmult * (a * a) / mult
            dr = dlog_a * (-RG_C * sp)
            dsp_ref[...] += dlog_a * (-RG_C * r)
            dpr = dr * r * (1.0 - r)
            dpi = di * ig * (1.0 - ig)
            dba_ref[...] += dpr
            dbi_ref[...] += dpi
            direct = gt * mult * ig
            if has_prev:
                direct = direct + prev_ref[rs, :]
            dxc_ref[rs, :] = direct
            pr_s[rs, :] = dpr
            pi_s[rs, :] = dpi
            return u_out

        carry_s[...] = _chunk_loop(t, chunk, carry_s[...], unroll=2, descending=not reverse)

        for h in range(heads):
            cs = pl.ds(h * LRU_HEAD, LRU_HEAD)
            xb = xc_ref[:, cs].astype(BF16)
            dprb = pr_s[:, cs].astype(BF16)
            dpib = pi_s[:, cs].astype(BF16)
            dwa_ref[h] += lax.dot_general(xb, dprb, (((0,), (0,)), ((), ())), preferred_element_type=F32)
            dwi_ref[h] += lax.dot_general(xb, dpib, (((0,), (0,)), ((), ())), preferred_element_type=F32)
            dxc_ref[:, cs] += (
                lax.dot_general(dprb, wa_ref[h], (((1,), (1,)), ((), ())), preferred_element_type=F32)
                + lax.dot_general(dpib, wi_ref[h], (((1,), (1,)), ((), ())), preferred_element_type=F32))

    tile_spec = pl.BlockSpec((t, c), tile)
    in_specs = [tile_spec, tile_spec, tile_spec, pl.BlockSpec((SUBLANES, c), halo)] + [tile_spec] * 4
    args = [xc, dh, h_dir, h_dir, *gates]
    if has_prev:
        in_specs.append(tile_spec)
        args.append(dxc_prev)
    in_specs += [pl.BlockSpec((heads, LRU_HEAD, LRU_HEAD), whole3), pl.BlockSpec((heads, LRU_HEAD, LRU_HEAD), whole3),
                 pl.BlockSpec((1, c), whole2), ANY]
    args += [wa, wi, lam, dep]
    assert len(args) == n_in
    return pl.pallas_call(
        body, name="scan_bwd_rev" if reverse else "scan_bwd", grid=(n_tiles,),
        in_specs=in_specs,
        out_specs=[tile_spec,
                   pl.BlockSpec((heads, LRU_HEAD, LRU_HEAD), whole3), pl.BlockSpec((heads, LRU_HEAD, LRU_HEAD), whole3),
                   pl.BlockSpec((SUBLANES, c), whole2), pl.BlockSpec((SUBLANES, c), whole2), pl.BlockSpec((SUBLANES, c), whole2)],
        out_shape=[jax.ShapeDtypeStruct((s, c), F32),
                   jax.ShapeDtypeStruct((heads, LRU_HEAD, LRU_HEAD), F32), jax.ShapeDtypeStruct((heads, LRU_HEAD, LRU_HEAD), F32),
                   jax.ShapeDtypeStruct((SUBLANES, c), F32), jax.ShapeDtypeStruct((SUBLANES, c), F32),
                   jax.ShapeDtypeStruct((SUBLANES, c), F32)],
        scratch_shapes=[pltpu.VMEM((t, c), F32), pltpu.VMEM((t, c), F32), pltpu.VMEM((1, c), F32)],
        compiler_params=_params("arbitrary"),
    )(*args)


def _dproj_call(e_pool, dxc, proj, dgate, conv_w, p):
    s, c = dxc.shape
    pg = p // N_POOL_GROUPS
    t = min(SEQ_TILE, s)
    n_tiles = s // t

    def body(e_ref, ep_ref, en_ref, dx_ref, dxp_ref, dxn_ref, u_ref, up_ref, un_ref, dgate_ref, w_ref,
             dproj_ref, dcw_ref, dcb_ref, st_s):
        i = pl.program_id(0)
        first, last = i == 0, i == n_tiles - 1

        @pl.when(first)
        def _():
            dcw_ref[...] = jnp.zeros_like(dcw_ref)
            dcb_ref[...] = jnp.zeros_like(dcb_ref)

        rows_p = lax.broadcasted_iota(jnp.int32, (SUBLANES, pg), 0)
        rows_c = lax.broadcasted_iota(jnp.int32, (SUBLANES, c), 0)
        w = w_ref[...]

        def chunk(r0, _):
            rs = pl.ds(r0, SUBLANES)
            for g, win in enumerate(POOL_WINDOWS):
                cols = pl.ds(g * pg, pg)
                prv, cur, nxt = _neighbour_chunks(e_ref, ep_ref, en_ref, r0, t, cols, first, last)
                tot = cur
                for o in range(-(win // 2) + 1, win // 2 + 1):
                    if o != 0:
                        tot = tot + _shift_rows(prv, cur, nxt, o, rows_p)
                cnt = _window_counts(r0, i, t, s, win // 2, (SUBLANES, pg))
                st_s[rs, cols] = tot - cur * cnt
            prv, cur, nxt = _neighbour_chunks(dx_ref, dxp_ref, dxn_ref, r0, t, slice(None), first, last)
            du = w[1:2] * cur
            du += w[0:1] * _shift_rows(prv, cur, nxt, 1, rows_c)
            du += w[2:3] * _shift_rows(prv, cur, nxt, -1, rows_c)
            du += w[3:4] * _shift_rows(prv, cur, nxt, -2, rows_c)
            st_s[rs, pl.ds(p, c)] = du
            uprv, ucur, unxt = _neighbour_chunks(u_ref, up_ref, un_ref, r0, t, slice(None), first, last)
            dcb_ref[...] += cur
            for j, o in enumerate((-1, 0, 1, 2)):
                dcw_ref[j] += cur * _shift_rows(uprv, ucur, unxt, o, rows_c)

        _chunk_loop(t, chunk)
        dproj_ref[:, pl.ds(0, p + c)] = st_s[...].astype(BF16)
        dproj_ref[:, pl.ds(p + c, c)] = dgate_ref[...]

    return pl.pallas_call(
        body, name="dproj", grid=(n_tiles,),
        in_specs=_halo_specs(t, s, p, 0) + _halo_specs(t, s, c, 0) + _halo_specs(t, s, c, 1) + [
            pl.BlockSpec((t, c), lambda i: (i, 0)), pl.BlockSpec((4, c), lambda i: (0, 0))],
        out_specs=[pl.BlockSpec((t, p + 2 * c), lambda i: (i, 0)),
                   pl.BlockSpec((4, SUBLANES, c), lambda i: (0, 0, 0)), pl.BlockSpec((SUBLANES, c), lambda i: (0, 0))],
        out_shape=[jax.ShapeDtypeStruct((s, p + 2 * c), BF16), jax.ShapeDtypeStruct((4, SUBLANES, c), F32),
                   jax.ShapeDtypeStruct((SUBLANES, c), F32)],
        scratch_shapes=[pltpu.VMEM((t, p + c), F32)],
        compiler_params=_params("arbitrary"),
    )(e_pool, e_pool, e_pool, dxc, dxc, dxc, proj, proj, proj, dgate, conv_w)


def _dx_call(dproj, w_in, dz1, dep):
    s, e = dproj.shape
    n, d, e4 = w_in.shape
    tm = min(MM_TILE, s)

    def body(dp_ref, w_hbm, dz_ref, dep_ref, o_ref, w_s, sems):
        @pl.when(pl.program_id(0) == 0)
        def _():
            copies = [pltpu.make_async_copy(w_hbm.at[k], w_s.at[:, pl.ds(k * e4, e4)], sems.at[k]) for k in range(n)]
            for cp in copies:
                cp.start()
            for cp in copies:
                cp.wait()

        o_ref[...] = ALPHA * dz_ref[...] + lax.dot_general(
            dp_ref[...], w_s[...], (((1,), (1,)), ((), ())), preferred_element_type=F32)

    return pl.pallas_call(
        body, name="grad_x", grid=(s // tm,),
        in_specs=[pl.BlockSpec((tm, e), lambda i: (i, 0)), ANY, pl.BlockSpec((tm, d), lambda i: (i, 0)), ANY],
        out_specs=pl.BlockSpec((tm, d), lambda i: (i, 0)),
        out_shape=jax.ShapeDtypeStruct((s, d), F32),
        scratch_shapes=[pltpu.VMEM((d, e), BF16), pltpu.SemaphoreType.DMA((n,))],
        compiler_params=_params("arbitrary"),
    )(dproj, w_in, dz1, dep)


def _row_tile(rows, cols, n_arrays):
    limit = max(SUBLANES, ELT_BLOCK_BYTES // (4 * cols * max(1, n_arrays // 4)))
    best = SUBLANES
    for cand in range(SUBLANES, min(rows, limit) + 1, SUBLANES):
        if rows % cand == 0:
            best = cand
    return best if rows % SUBLANES == 0 else rows


def _cast_to_slot_call(a, idx, dtype, name, dep):
    rows, cols = a.shape
    tr = _row_tile(rows, cols, 2)
    extra = [] if dep is None else [dep]

    def body(idx_ref, a_ref, *rest):
        rest[-1][...] = a_ref[...].astype(dtype)

    return pl.pallas_call(
        body, name=name,
        grid_spec=pltpu.PrefetchScalarGridSpec(
            num_scalar_prefetch=1, grid=(rows // tr,),
            in_specs=[pl.BlockSpec((tr, cols), lambda i, idx_ref: (i, 0))] + [ANY] * len(extra),
            out_specs=pl.BlockSpec((None, tr, cols), lambda i, idx_ref: (idx_ref[1], i, 0))),
        out_shape=jax.ShapeDtypeStruct((N_CHIPS, rows, cols), dtype),
        compiler_params=_params("arbitrary"),
    )(idx, a, *extra)


def _add_half_call(g, recv, idx, name):
    _, rows, cols = g.shape
    tr = _row_tile(rows, cols, 3)

    def body(idx_ref, g_ref, r_ref, o_ref):
        o_ref[...] = g_ref[...] + r_ref[...]

    return pl.pallas_call(
        body, name=name,
        grid_spec=pltpu.PrefetchScalarGridSpec(
            num_scalar_prefetch=1, grid=(rows // tr,),
            in_specs=[pl.BlockSpec((None, tr, cols), lambda i, idx_ref: (idx_ref[0], i, 0)),
                      pl.BlockSpec((tr, cols), lambda i, idx_ref: (i, 0))],
            out_specs=pl.BlockSpec((None, tr, cols), lambda i, idx_ref: (idx_ref[1], i, 0))),
        out_shape=jax.ShapeDtypeStruct((N_CHIPS, rows, cols), F32),
        compiler_params=_params("arbitrary"),
    )(idx, g, recv)


def _sum_chips_call(own, recv, idx, name):
    _, rows, cols = recv.shape
    tr = _row_tile(rows, cols, 5)
    out_spec = pl.BlockSpec((None, tr, cols), lambda i, idx_ref: (idx_ref[0], i, 0))
    if own is None:
        def body(idx_ref, r_ref, o_ref):
            o_ref[...] = ((r_ref[0] + r_ref[1]) + r_ref[2]) + r_ref[3]
        in_specs = [pl.BlockSpec((N_CHIPS, tr, cols), lambda i, idx_ref: (0, i, 0))]
        args = (recv,)
    else:
        def body(idx_ref, p_ref, r_ref, o_ref):
            o_ref[...] = ((p_ref[...] + r_ref[0]) + r_ref[1]) + r_ref[2]
        in_specs = [pl.BlockSpec((None, tr, cols), lambda i, idx_ref: (idx_ref[1], i, 0)),
                    pl.BlockSpec((N_CHIPS - 1, tr, cols), lambda i, idx_ref: (0, i, 0))]
        args = (own, recv)
    return pl.pallas_call(
        body, name=name,
        grid_spec=pltpu.PrefetchScalarGridSpec(num_scalar_prefetch=1, grid=(rows // tr,), in_specs=in_specs, out_specs=out_spec),
        out_shape=jax.ShapeDtypeStruct((2, rows, cols), F32),
        compiler_params=_params("arbitrary"),
    )(idx, *args)


def _adamw_math(g, w, m, v):
    mn = ADAM_B1 * m + (1.0 - ADAM_B1) * g
    vn = ADAM_B2 * v + (1.0 - ADAM_B2) * (g * g)
    m_hat = mn / (1.0 - ADAM_B1 ** ADAM_STEP)
    v_hat = vn / (1.0 - ADAM_B2 ** ADAM_STEP)
    return -ADAM_LR * (m_hat / (jnp.sqrt(v_hat) + ADAM_EPS) + ADAM_WD * w), mn, vn


def _adamw_call(g, w, m, v, name):
    rows, cols = w.shape
    tr = _row_tile(rows, cols, 4)

    def body(g_ref, w_ref, m_ref, v_ref, go_ref, d_ref, mo_ref, vo_ref):
        gv = g_ref[...]
        go_ref[...] = gv
        d_ref[...], mo_ref[...], vo_ref[...] = _adamw_math(gv, w_ref[...], m_ref[...], v_ref[...])

    spec = pl.BlockSpec((tr, cols), lambda i: (i, 0))
    shape = jax.ShapeDtypeStruct((rows, cols), F32)
    return pl.pallas_call(
        body, name=name, grid=(rows // tr,),
        in_specs=[spec] * 4, out_specs=[spec] * 4, out_shape=[shape] * 4,
        compiler_params=_params("arbitrary"),
    )(g, w, m, v)


def _adamw_small_call(gs, ws, ms, vs):
    n = len(ws)

    def body(*refs):
        ins = [refs[k * n:(k + 1) * n] for k in range(4)]
        outs = [refs[(4 + k) * n:(5 + k) * n] for k in range(3)]
        for a in range(n):
            outs[0][a][...], outs[1][a][...], outs[2][a][...] = _adamw_math(*[ins[k][a][...] for k in range(4)])

    whole = pl.BlockSpec(memory_space=pltpu.VMEM)
    res = pl.pallas_call(
        body, name="adamw_small",
        in_specs=[whole] * (4 * n), out_specs=[whole] * (3 * n),
        out_shape=[jax.ShapeDtypeStruct(w.shape, F32) for w in ws] * 3,
        compiler_params=pltpu.CompilerParams(vmem_limit_bytes=VMEM_LIMIT),
    )(*gs, *ws, *ms, *vs)
    return res[:n], res[n:2 * n], res[2 * n:]


def _mesh_place():
    x, y, c = lax.axis_index("x"), lax.axis_index("y"), lax.axis_index("c")
    chips = [(1 - x, y), (x, 1 - y), (1 - x, 1 - y)]
    return x, y, c, chips


def _remote(src, dst, send_sems, recv_sems, idx, device):
    return pltpu.make_async_remote_copy(src_ref=src, dst_ref=dst, send_sem=send_sems.at[idx], recv_sem=recv_sems.at[idx],
                                        device_id=device, device_id_type=MESH)


HBM_SPEC = pl.BlockSpec(memory_space=pltpu.HBM)
SEM_SPEC = pl.BlockSpec(memory_space=pltpu.SEMAPHORE)
ORDERED_EFFECT = pltpu.SideEffectType.DATAFLOW_SIDE_EFFECTING


def _in_hbm(a):
    return pltpu.with_memory_space_constraint(a, pltpu.HBM)


def _start_copies_call(name, bufs, groups, after=None):
    n, g = len(bufs), len(groups)
    extra = [] if after is None else [after]
    first_out = n + len(extra)

    def body(*refs):
        outs = refs[first_out:first_out + n]
        sems = refs[first_out + n:first_out + n + 2 * g]
        token = refs[first_out + n + 2 * g]
        for i, (which, copies_fn, _) in enumerate(groups):
            for mine, _ in copies_fn([outs[w] for w in which], sems[2 * i], sems[2 * i + 1]):
                mine.start()
        token[...] = jnp.zeros_like(token)

    sem_shapes = [pltpu.SemaphoreType.DMA((cnt,)) for _, _, cnt in groups for _ in range(2)]
    res = pl.pallas_call(
        body, name=name,
        in_specs=[HBM_SPEC] * n + [ANY] * len(extra),
        out_specs=[HBM_SPEC] * n + [SEM_SPEC] * (2 * g) + [pl.BlockSpec(memory_space=pltpu.VMEM)],
        out_shape=[pltpu.HBM(a.shape, a.dtype) for a in bufs] + sem_shapes + [jax.ShapeDtypeStruct((SUBLANES, LANES), F32)],
        input_output_aliases={a: a for a in range(n)},
        compiler_params=pltpu.CompilerParams(has_side_effects=ORDERED_EFFECT),
    )(*[_in_hbm(a) for a in bufs], *extra)
    sems = res[n:n + 2 * g]
    return list(res[:n]), [(sems[2 * i], sems[2 * i + 1]) for i in range(g)], res[n + 2 * g]


def _wait_copies_call(name, bufs, sems, copies_fn, after):
    n = len(bufs)

    def body(*refs):
        ins = refs[:n]
        send_sems, recv_sems = refs[n], refs[n + 1]
        for mine, arriving in copies_fn(list(ins), send_sems, recv_sems):
            arriving.wait_recv()
            mine.wait_send()

    res = pl.pallas_call(
        body, name=name,
        in_specs=[HBM_SPEC] * n + [SEM_SPEC, SEM_SPEC, ANY],
        out_specs=[HBM_SPEC] * n,
        out_shape=[pltpu.HBM(a.shape, a.dtype) for a in bufs],
        input_output_aliases={a: a for a in range(n)},
        compiler_params=pltpu.CompilerParams(has_side_effects=ORDERED_EFFECT),
    )(*bufs, sems[0], sems[1], after)
    return list(res)


def _gather_copies(bufs, send_sems, recv_sems):
    x, y, c, chips = _mesh_place()
    k = 2 * x + y
    out = []
    for a, buf in enumerate(bufs):
        for j, (px, py) in enumerate(chips):
            kj = 2 * px + py
            mine = _remote(buf.at[k, c], buf.at[k, c], send_sems, recv_sems, 3 * a + j, (px, py, c))
            arriving = _remote(buf.at[k, c], buf.at[kj, c], send_sems, recv_sems, 3 * a + j, (px, py, c))
            out.append((mine, arriving))
    return out


def _exchange_copies(n_sharded, n_replicated):
    def copies(bufs, send_sems, recv_sems):
        x, y, c, chips = _mesh_place()
        k = 2 * x + y
        sums, lands = bufs[:n_sharded], bufs[n_sharded:2 * n_sharded]
        repl = bufs[2 * n_sharded:]
        out = []
        for j, (px, py) in enumerate(chips):
            kj = 2 * px + py
            for a in range(n_sharded):
                cp = _remote(sums[a].at[kj], lands[a].at[j], send_sems, recv_sems, 3 * a + j, (px, py, c))
                out.append((cp, cp))
            for a in range(n_replicated):
                idx = 3 * (n_sharded + a) + j
                mine = _remote(repl[a].at[k], repl[a].at[k], send_sems, recv_sems, idx, (px, py, c))
                arriving = _remote(repl[a].at[k], repl[a].at[kj], send_sems, recv_sems, idx, (px, py, c))
                out.append((mine, arriving))
        return out
    return copies


def _sibling_copies(n, halves):
    def copies(bufs, send_sems, recv_sems):
        x, y, c, _ = _mesh_place()
        out = []
        for a in range(n):
            src = bufs[a].at[1 - c] if halves else bufs[a]
            cp = _remote(src, bufs[n + a], send_sems, recv_sems, a, (x, y, 1 - c))
            out.append((cp, cp))
        return out
    return copies


def _forward_copies(bufs, send_sems, recv_sems):
    x, y, c, chips = _mesh_place()
    out = []
    for a, buf in enumerate(bufs):
        for j, (px, py) in enumerate(chips):
            kj = 2 * px + py
            mine = _remote(buf.at[kj, c], buf.at[kj, c], send_sems, recv_sems, 3 * a + j, (x, y, 1 - c))
            arriving = _remote(buf.at[kj, c], buf.at[kj, 1 - c], send_sems, recv_sems, 3 * a + j, (x, y, 1 - c))
            out.append((mine, arriving))
    return out


def _join_copies(bufs, send_sems, recv_sems):
    x, y, c, _ = _mesh_place()
    out = []
    for a, buf in enumerate(bufs):
        mine = _remote(buf.at[c], buf.at[c], send_sems, recv_sems, a, (x, y, 1 - c))
        arriving = _remote(buf.at[c], buf.at[1 - c], send_sems, recv_sems, a, (x, y, 1 - c))
        out.append((mine, arriving))
    return out


def _forward_to_sibling_call(bufs, name):
    n = len(bufs)

    def body(*refs):
        ins, outs = refs[:n], refs[n:2 * n]
        send_sems, recv_sems = refs[2 * n:]
        x, y, c, chips = _mesh_place()
        sibling = (x, y, 1 - c)
        sends = []
        for a in range(n):
            for j, (px, py) in enumerate(chips):
                kj = 2 * px + py
                sends.append(_remote(ins[a].at[kj, c], outs[a].at[kj, c], send_sems, recv_sems, 3 * a + j, sibling))
        for cp in sends:
            cp.start()
        for a in range(n):
            for j, (px, py) in enumerate(chips):
                kj = 2 * px + py
                _remote(ins[a].at[kj, c], outs[a].at[kj, 1 - c], send_sems, recv_sems, 3 * a + j, sibling).wait_recv()
        for cp in sends:
            cp.wait_send()

    return pl.pallas_call(
        body, name=name,
        in_specs=[ANY] * n, out_specs=[ANY] * n,
        out_shape=[jax.ShapeDtypeStruct(a.shape, a.dtype) for a in bufs],
        input_output_aliases={a: a for a in range(n)},
        scratch_shapes=[pltpu.SemaphoreType.DMA((3 * n,)), pltpu.SemaphoreType.DMA((3 * n,))],
    )(*bufs)


def _pack(arrays, rows_multiple):
    flat = jnp.concatenate([a.reshape(-1) for a in arrays])
    per = LANES * rows_multiple
    padded = -(-flat.shape[0] // per) * per
    flat = jnp.pad(flat, (0, padded - flat.shape[0]))
    return flat.reshape(-1, LANES)


def _unpack(packed, shapes):
    flat = packed.reshape(-1)
    out, at = [], 0
    for shp in shapes:
        size = 1
        for dim in shp:
            size *= dim
        out.append(flat[at:at + size].reshape(shp))
        at += size
    return out


def _halves(a):
    return a.reshape((2, a.shape[0] // 2) + a.shape[1:])


def kernel(x, ln_mix_g, ln_mix_b, w_in, w_pool, pool_scale, conv_w, conv_b, w_rg_a, b_rg_a, w_rg_i, b_rg_i, rg_lambda, w_out, ln_ffn_g, ln_ffn_b, w_mlp_in, w_mlp_out, loss_target, m_ln_mix_g, m_ln_mix_b, m_w_in, m_w_pool, m_pool_scale, m_conv_w, m_conv_b, m_w_rg_a, m_b_rg_a, m_w_rg_i, m_b_rg_i, m_rg_lambda, m_w_out, m_ln_ffn_g, m_ln_ffn_b, m_w_mlp_in, m_w_mlp_out, v_ln_mix_g, v_ln_mix_b, v_w_in, v_w_pool, v_pool_scale, v_conv_w, v_conv_b, v_w_rg_a, v_b_rg_a, v_w_rg_i, v_b_rg_i, v_rg_lambda, v_w_out, v_ln_ffn_g, v_ln_ffn_b, v_w_mlp_in, v_w_mlp_out):
    weights = dict(ln_mix_g=ln_mix_g, ln_mix_b=ln_mix_b, w_in=w_in, w_pool=w_pool, pool_scale=pool_scale, conv_w=conv_w,
                   conv_b=conv_b, w_rg_a=w_rg_a, b_rg_a=b_rg_a, w_rg_i=w_rg_i, b_rg_i=b_rg_i, rg_lambda=rg_lambda,
                   w_out=w_out, ln_ffn_g=ln_ffn_g, ln_ffn_b=ln_ffn_b, w_mlp_in=w_mlp_in, w_mlp_out=w_mlp_out)
    m_in = dict(ln_mix_g=m_ln_mix_g, ln_mix_b=m_ln_mix_b, w_in=m_w_in, w_pool=m_w_pool, pool_scale=m_pool_scale,
                conv_w=m_conv_w, conv_b=m_conv_b, w_rg_a=m_w_rg_a, b_rg_a=m_b_rg_a, w_rg_i=m_w_rg_i, b_rg_i=m_b_rg_i,
                rg_lambda=m_rg_lambda, w_out=m_w_out, ln_ffn_g=m_ln_ffn_g, ln_ffn_b=m_ln_ffn_b, w_mlp_in=m_w_mlp_in,
                w_mlp_out=m_w_mlp_out)
    v_in = dict(ln_mix_g=v_ln_mix_g, ln_mix_b=v_ln_mix_b, w_in=v_w_in, w_pool=v_w_pool, pool_scale=v_pool_scale,
                conv_w=v_conv_w, conv_b=v_conv_b, w_rg_a=v_w_rg_a, b_rg_a=v_b_rg_a, w_rg_i=v_w_rg_i, b_rg_i=v_b_rg_i,
                rg_lambda=v_rg_lambda, w_out=v_w_out, ln_ffn_g=v_ln_ffn_g, ln_ffn_b=v_ln_ffn_b, w_mlp_in=v_w_mlp_in,
                w_mlp_out=v_w_mlp_out)
    names = list(weights)

    xs = x[0]
    tgt = loss_target[0]
    s, d = xs.shape
    p = c = d // 2
    pg = p // N_POOL_GROUPS
    core = lax.axis_index("c")
    shard = 2 * lax.axis_index("x") + lax.axis_index("y")

    idx = jnp.stack([core, shard]).astype(jnp.int32)
    small_shard = _pack([conv_w[0], b_rg_a[0], b_rg_i[0], rg_lambda[0]], 2 * SUBLANES)
    to_gather = [(w_in[0], BF16), (w_out[0], BF16), (w_mlp_in[0], BF16), (w_mlp_out[0], BF16),
                 (w_pool[0].reshape(-1, pg), BF16), (small_shard, F32)]

    def slot_view(i, dep):
        a, dt = to_gather[i]
        sl = _cast_to_slot_call(a, idx, dt, f"gather_slot_{i}", dep)
        return sl.reshape(N_CHIPS, 2, sl.shape[1] // 2, sl.shape[2])

    first, later = (0, 4, 5), (1, 2, 3)
    fly_a, sems_a, token_a = _start_copies_call(
        "gather_start_first", [slot_view(i, None) for i in first], [((0, 1, 2), _gather_copies, 3 * len(first))])
    later_views = []
    for i in later:
        later_views.append(slot_view(i, later_views[-1] if later_views else token_a))
    xb = _cast_call(xs, later_views[-1])
    got_first = _wait_copies_call("gather_wait_w_in", fly_a, sems_a[0], _gather_copies, xb)
    fly_b, sems_b, g_token = _start_copies_call(
        "gather_start_later", later_views + got_first,
        [((0,), _gather_copies, 3), ((1,), _gather_copies, 3), ((2,), _gather_copies, 3)])
    got_first = fly_b[len(later):]
    in_flight = dict(zip(later, fly_b))
    g_sems = [None] + list(sems_b)

    def arrive(which, group, after, tag):
        return _wait_copies_call(f"gather_wait_{tag}", [in_flight[w] for w in which], g_sems[group], _gather_copies, after)

    def pass_on(got, tag):
        flying, sems, token = _start_copies_call(
            f"gather_forward_start_{tag}", got, [(tuple(range(len(got))), _forward_copies, 3 * len(got))])
        return (flying, sems[0], tag), token

    def passed_on(state, after):
        flying, sems, tag = state
        return _wait_copies_call(f"gather_forward_wait_{tag}", flying, sems, _forward_copies, after)

    gathered = [None] * len(to_gather)
    gathered[0], gathered[4], gathered[5] = _forward_to_sibling_call(got_first, "gather_forward_w_in")
    w_in_f = gathered[0].reshape((N_CHIPS,) + w_in.shape[1:])
    w_pool_f = gathered[4].reshape(N_CHIPS, N_POOL_GROUPS, pg // N_CHIPS, pg).transpose(1, 0, 2, 3).reshape(N_POOL_GROUPS, pg, pg)
    c4 = c // N_CHIPS
    small_parts = [_unpack(gathered[5][k].reshape(-1, LANES), [(4, c4), (2, c4), (2, c4), (2, c4)]) for k in range(N_CHIPS)]
    conv_w_f = jnp.concatenate([sp_[0] for sp_ in small_parts], axis=1)
    b_a_f = jnp.concatenate([sp_[1] for sp_ in small_parts], axis=1)
    b_i_f = jnp.concatenate([sp_[2] for sp_ in small_parts], axis=1)
    lam_f = jnp.concatenate([sp_[3] for sp_ in small_parts], axis=1)
    wa_b = w_rg_a[0].astype(BF16)
    wi_b = w_rg_i[0].astype(BF16)

    proj = _proj_call(xb, w_in_f)
    xc = _conv_call(proj, conv_w_f, conv_b, c)
    fwd_w_out, token = pass_on(arrive((1,), 1, xc, "w_out"), "w_out")
    h_b, *gates_b = _scan_fwd_call(xc, wa_b[1], wi_b[1], b_a_f[1:2], b_i_f[1:2], lam_f[1:2], True, token)
    h_f, *gates_f = _scan_fwd_call(xc, wa_b[0], wi_b[0], b_a_f[0:1], b_i_f[0:1], lam_f[0:1], False, token)
    y, d_pool = _pool_combine_call(proj, h_f, h_b, w_pool_f, pool_scale, p)
    w_out_f = passed_on(fwd_w_out, y)[0].reshape(d, d)
    fwd_w1, token = pass_on(arrive((2,), 2, y, "w_mlp_in"), "w_mlp_in")
    xh1, x1b, rstd1 = _out_ln1_call(y, w_out_f, xs, ln_mix_g, ln_mix_b, token)
    w1_f = passed_on(fwd_w1, x1b)[0].reshape((N_CHIPS,) + w_mlp_in.shape[1:])
    first_half = _mlp_in_call(x1b, w1_f, g_token, None)
    fwd_w2, token = pass_on(arrive((3,), 3, first_half[0], "w_mlp_out"), "w_mlp_out")
    r_act, hsq = _mlp_in_call(x1b, w1_f, token, first_half)
    w2_f = passed_on(fwd_w2, hsq)[0].reshape(N_CHIPS * w_mlp_out.shape[1], d)
    dz2, dz2b, loss8, dg2, db2 = _mlp_out_ln2_call(hsq, w2_f, xh1, ln_mix_g, ln_mix_b, ln_ffn_g, ln_ffn_b, tgt)

    def start_siblings(grads, halves, tag, after=None):
        lands = [lax.empty(g.shape[1:] if halves else g.shape, g.dtype) for g in grads]
        copies = _sibling_copies(len(grads), halves)
        flying, sems, token = _start_copies_call(
            f"siblings_start_{tag}", list(grads) + lands, [(tuple(range(2 * len(grads))), copies, len(grads))], after)
        return (flying, sems[0], copies, len(grads), tag), token

    def finish_siblings(state, after):
        flying, sems, copies, n, tag = state
        got = _wait_copies_call(f"siblings_wait_{tag}", flying, sems, copies, after)
        return got[:n], got[n:]

    half_own = jnp.reshape(core, (1,)).astype(jnp.int32)
    half_sibling = 1 - half_own

    def chip_sum_of(a, b, row_sharded, tag, dep, overlapped):
        for_sibling = _half_grad_call(a, b, half_sibling, row_sharded, None, f"grad_{tag}_for_sibling", dep)
        state, token = start_siblings([for_sibling], False, tag)
        results = overlapped(token)
        _, (from_sibling,) = finish_siblings(state, results[0])
        return _half_grad_call(a, b, half_own, row_sharded, from_sibling, f"grad_{tag}", token), results

    def start_exchange(sums, n_repl, tag):
        n_sh = len(sums) - n_repl
        lands = [lax.empty((N_CHIPS - 1,) + a.shape[1:], a.dtype) for a in sums[:n_sh]]
        bufs = sums[:n_sh] + lands + sums[n_sh:]
        copies = _exchange_copies(n_sh, n_repl)
        flying, sems, token = _start_copies_call(
            f"reduce_start_{tag}", bufs, [(tuple(range(len(bufs))), copies, 3 * len(sums))])
        return (flying, sems[0], copies, n_sh, tag), token

    def finish_exchange(state, after):
        flying, sems, copies, n_sh, tag = state
        got = _wait_copies_call(f"reduce_wait_{tag}", flying, sems, copies, after)
        halves = []
        for a in range(n_sh):
            own, land = got[a], got[n_sh + a]
            cols = own.shape[-1]
            total = _sum_chips_call(own.reshape(N_CHIPS, -1, cols), land.reshape(N_CHIPS - 1, -1, cols), idx,
                                    f"reduce_sum_{tag}_{a}")
            halves.append(total.reshape((2,) + own.shape[1:]))
        for a, rp in enumerate(got[2 * n_sh:]):
            halves.append(_sum_chips_call(None, rp, idx, f"reduce_sum_{tag}_r{a}"))
        return halves

    def start_join(halves, tag):
        flying, sems, token = _start_copies_call(
            f"join_start_{tag}", halves, [(tuple(range(len(halves))), _join_copies, len(halves))])
        return (flying, sems[0], tag), token

    def finish_join(state, after):
        flying, sems, tag = state
        return _wait_copies_call(f"join_wait_{tag}", flying, sems, _join_copies, after)

    sum_w2, (dpre,) = chip_sum_of(hsq, dz2b, True, "w_mlp_out", g_token,
                                  lambda tok: (_dhsq_call(dz2b, w2_f, r_act, tok),))
    flying_w2, token = start_exchange([sum_w2], 0, "w2")
    sum_w1, (dz1, dz1b, dg1, db1) = chip_sum_of(
        x1b, dpre, False, "w_mlp_in", token,
        lambda tok: _dx1_ln1_bwd_call(dpre, w1_f, dz2, xh1, rstd1, ln_mix_g, tok))
    flying_w1, token = start_exchange([sum_w1], 0, "w1")

    sum_wout, (e_pool, dh, dgate, g_wpool, g_pscale8) = chip_sum_of(
        y, dz1b, True, "w_out", token,
        lambda tok: _mixer_bwd_call(dz1b, w_out_f, d_pool, proj, h_f, h_b, w_pool_f, pool_scale, p, tok))
    flying_wout, token = start_exchange([sum_wout], 0, "w_out")
    dxc0, g_wa0, g_wi0, g_ba0, g_bi0, g_sp0 = _scan_bwd_call(
        xc, dh, h_f, gates_f, None, wa_b[0], wi_b[0], lam_f[0:1], False, token)
    dxc, g_wa1, g_wi1, g_ba1, g_bi1, g_sp1 = _scan_bwd_call(
        xc, dh, h_b, gates_b, dxc0, wa_b[1], wi_b[1], lam_f[1:2], True, token)
    dproj, g_cw8, g_cb8 = _dproj_call(e_pool, dxc, proj, dgate, conv_w_f, p)

    rowsum = lambda a8: jnp.sum(a8, axis=-2)
    g_lam = jnp.stack([rowsum(g_sp0), rowsum(g_sp1)]) * (-_sigmoid(-lam_f))
    small_grads = {
        "ln_mix_g": rowsum(dg1), "ln_mix_b": rowsum(db1), "ln_ffn_g": rowsum(dg2), "ln_ffn_b": rowsum(db2),
        "pool_scale": rowsum(g_pscale8), "conv_b": rowsum(g_cb8),
        "w_rg_a": jnp.stack([g_wa0, g_wa1]), "w_rg_i": jnp.stack([g_wi0, g_wi1]),
        "w_pool": g_wpool, "conv_w": rowsum(g_cw8),
        "b_rg_a": jnp.stack([rowsum(g_ba0), rowsum(g_ba1)]), "b_rg_i": jnp.stack([rowsum(g_bi0), rowsum(g_bi1)]),
        "rg_lambda": g_lam,
    }
    small_names = list(small_grads)
    small_shapes = [small_grads[nm].shape for nm in small_names]
    loss_share = jnp.reshape(jnp.sum(loss8) * (0.5 / d), (1,))
    g_small = _halves(_pack([small_grads[nm] for nm in small_names] + [loss_share], 2 * SUBLANES))
    sib_small, token = start_siblings([g_small], True, "small")
    flying_small = []

    def small_exchange_and_grad_x(tok):
        (mine,), (theirs,) = finish_siblings(sib_small, tok)
        small_sum = _add_half_call(mine, theirs, idx, "reduce_add_small")
        state, tok = start_exchange([small_sum], 1, "small")
        flying_small.append(state)
        return (_dx_call(dproj, w_in_f, dz1, tok),)

    sum_win, (grad_x,) = chip_sum_of(xb, dproj, False, "w_in", token, small_exchange_and_grad_x)
    flying_small = flying_small[0]
    flying_win, token = start_exchange([sum_win], 0, "w_in")

    grad_w, delta_w, new_m, new_v = {}, {}, {}, {}

    def adamw(nm, full):
        w2d = weights[nm][0]
        g2d = full.reshape(w2d.shape)
        go, dl, mn, vn = _adamw_call(g2d, w2d, m_in[nm][0], v_in[nm][0], f"adamw_{nm}")
        grad_w[nm], delta_w[nm], new_m[nm], new_v[nm] = go[None], dl[None], mn[None], vn[None]
        return vn

    join_w2, token = start_join(finish_exchange(flying_w2, token), "w2")
    join_w1, token = start_join(finish_exchange(flying_w1, token), "w1")
    join_wout, token = start_join(finish_exchange(flying_wout, token), "w_out")
    last = adamw("w_mlp_out", finish_join(join_w2, token)[0])
    last = adamw("w_mlp_in", finish_join(join_w1, last)[0])
    last = adamw("w_out", finish_join(join_wout, last)[0])

    join_small, token = start_join(finish_exchange(flying_small, last), "small")
    join_win, token = start_join(finish_exchange(flying_win, token), "w_in")
    small_joined = finish_join(join_small, token)[0]
    *small_sums, loss_sum = _unpack(small_joined.reshape(-1, LANES), small_shapes + [(1,)])
    small_full = dict(zip(small_names, small_sums))
    local = dict(small_full)
    local["w_pool"] = lax.dynamic_slice_in_dim(small_full["w_pool"], shard * (pg // N_CHIPS), pg // N_CHIPS, axis=1)
    for nm in ("conv_w", "b_rg_a", "b_rg_i", "rg_lambda"):
        local[nm] = lax.dynamic_slice_in_dim(small_full[nm], shard * c4, c4, axis=1)
    small_g = [local[nm].reshape(weights[nm].shape) for nm in small_names]
    small_d, small_m, small_v = _adamw_small_call(
        small_g, [weights[nm] for nm in small_names], [m_in[nm] for nm in small_names], [v_in[nm] for nm in small_names])
    for nm, gl, dl, mn, vn in zip(small_names, small_g, small_d, small_m, small_v):
        grad_w[nm], delta_w[nm], new_m[nm], new_v[nm] = gl, dl, mn, vn
    adamw("w_in", finish_join(join_win, small_v[0])[0])

    loss = loss_sum[0]
    return (loss, grad_x[None], *[grad_w[nm] for nm in names], *[delta_w[nm] for nm in names],
            *[new_m[nm] for nm in names], *[new_v[nm] for nm in names])
```

```python
import jax
import jax.numpy as jnp
from jax import lax
from jax.experimental import pallas as pl
from jax.experimental.pallas import tpu as pltpu

F32 = jnp.float32
BF16 = jnp.bfloat16

N_CHIPS = 4
LANES = 128
SUBLANES = 8
LRU_HEAD = 128
N_POOL_GROUPS = 4
POOL_WINDOWS = (2, 4, 8, 16)
RG_C = 8.0
LN_EPS = 1e-5
ALPHA = 2.0 ** 0.25
ADAM_LR, ADAM_B1, ADAM_B2, ADAM_EPS, ADAM_WD, ADAM_STEP = 0.001, 0.9, 0.999, 1e-08, 0.01, 10
VMEM_LIMIT = 56 * 1024 * 1024
SEQ_TILE = 512
MM_TILE = 512
LN_MM_K = 2048
LN_UNROLL = 8
ELT_BLOCK_BYTES = 2 * 1024 * 1024
RESIDENT_OPERAND_BYTES = 16 * 1024 * 1024
MESH = pl.DeviceIdType.MESH
ANY = pl.BlockSpec(memory_space=pl.ANY)


def _params(*sem):
    return pltpu.CompilerParams(dimension_semantics=sem, vmem_limit_bytes=VMEM_LIMIT)


def _sigmoid(z):
    return 1.0 / (1.0 + jnp.exp(-z))


def _neg_expm1(z):
    series = -(z * (1.0 + z * (0.5 + z * (1.0 / 6.0 + z * (1.0 / 24.0)))))
    return jnp.where(z > -0.01, series, 1.0 - jnp.exp(z))


def _softplus(z):
    return jnp.maximum(z, 0.0) + jnp.log1p(jnp.exp(-jnp.abs(z)))


_GELU_K = 0.7978845608028654
_GELU_C = 0.044715


def _gelu_and_grad(u):
    t = jnp.tanh(_GELU_K * (u + _GELU_C * (u * u * u)))
    g = 0.5 * u * (1.0 + t)
    dg = 0.5 * (1.0 + t) + 0.5 * u * (1.0 - t * t) * (_GELU_K * (1.0 + 3.0 * _GELU_C * u * u))
    return g, dg


def _shift_rows(prv, cur, nxt, o, rows):
    if o == 0:
        return cur
    if o == SUBLANES:
        return nxt
    if o == -SUBLANES:
        return prv
    if o > 0:
        return pltpu.roll(jnp.where(rows >= o, cur, nxt), SUBLANES - o, 0)
    p = -o
    return pltpu.roll(jnp.where(rows < SUBLANES - p, cur, prv), p, 0)


def _neighbour_chunks(main_ref, prev_ref, next_ref, r0, t_rows, cols, first_tile, last_tile):
    cur = main_ref[pl.ds(r0, SUBLANES), cols]
    before = main_ref[pl.ds(pl.multiple_of(jnp.maximum(r0 - SUBLANES, 0), SUBLANES), SUBLANES), cols]
    after = main_ref[pl.ds(pl.multiple_of(jnp.minimum(r0 + SUBLANES, t_rows - SUBLANES), SUBLANES), SUBLANES), cols]
    halo_prev = jnp.where(first_tile, 0.0, prev_ref[:, cols])
    halo_next = jnp.where(last_tile, 0.0, next_ref[:, cols])
    prv = jnp.where(r0 == 0, halo_prev, before)
    nxt = jnp.where(r0 == t_rows - SUBLANES, halo_next, after)
    return prv, cur, nxt


def _halo_specs(t_rows, n_rows, width, col_block):
    per = t_rows // SUBLANES
    last = n_rows // SUBLANES - 1
    return [
        pl.BlockSpec((t_rows, width), lambda i: (i, col_block)),
        pl.BlockSpec((SUBLANES, width), lambda i: (jnp.maximum(i * per - 1, 0), col_block)),
        pl.BlockSpec((SUBLANES, width), lambda i: (jnp.minimum((i + 1) * per, last), col_block)),
    ]


def _chunk_loop(t_rows, fn, init=None, unroll=1, descending=False):
    span = SUBLANES * unroll

    def step(ci, carry):
        base = pl.multiple_of(((t_rows // span - 1 - ci) if descending else ci) * span, span)
        for u in range(unroll):
            carry = fn(base + ((unroll - 1 - u) if descending else u) * SUBLANES, carry)
        return carry
    return lax.fori_loop(0, t_rows // span, step, init)


def _scan_chunk(a, b, h_in, rows, reverse):
    for dist in (1, 2, 4):
        if reverse:
            keep = rows < SUBLANES - dist
            shift = SUBLANES - dist
        else:
            keep = rows >= dist
            shift = dist
        b = a * jnp.where(keep, pltpu.roll(b, shift, 0), 0.0) + b
        a = a * jnp.where(keep, pltpu.roll(a, shift, 0), 1.0)
    return a * h_in + b


def _cast_call(x, dep):
    s, d = x.shape
    tm = min(MM_TILE, s)

    def body(x_ref, dep_ref, o_ref):
        o_ref[...] = x_ref[...].astype(BF16)

    return pl.pallas_call(
        body, name="cast_x", grid=(s // tm,),
        in_specs=[pl.BlockSpec((tm, d), lambda i: (i, 0)), ANY],
        out_specs=pl.BlockSpec((tm, d), lambda i: (i, 0)),
        out_shape=jax.ShapeDtypeStruct((s, d), BF16),
        compiler_params=_params("arbitrary"),
    )(x, dep)


def _proj_call(xb, w_in):
    s, d = xb.shape
    n, _, e4 = w_in.shape
    tm = min(MM_TILE, s)

    def body(x_ref, w_hbm, proj_ref, w_s, sems):
        @pl.when(pl.program_id(0) == 0)
        def _():
            copies = [pltpu.make_async_copy(w_hbm.at[k], w_s.at[:, pl.ds(k * e4, e4)], sems.at[k]) for k in range(n)]
            for cp in copies:
                cp.start()
            for cp in copies:
                cp.wait()

        proj_ref[...] = jnp.dot(x_ref[...], w_s[...], preferred_element_type=F32)

    return pl.pallas_call(
        body, name="proj", grid=(s // tm,),
        in_specs=[pl.BlockSpec((tm, d), lambda i: (i, 0)), ANY],
        out_specs=pl.BlockSpec((tm, n * e4), lambda i: (i, 0)),
        out_shape=jax.ShapeDtypeStruct((s, n * e4), F32),
        scratch_shapes=[pltpu.VMEM((d, n * e4), BF16), pltpu.SemaphoreType.DMA((n,))],
        compiler_params=_params("arbitrary"),
    )(xb, w_in)


def _conv_call(proj, conv_w, conv_b, c):
    s = proj.shape[0]
    t = min(SEQ_TILE, s)
    n_tiles = s // t

    def body(u_ref, up_ref, un_ref, w_ref, b_ref, xc_ref):
        i = pl.program_id(0)
        rows = lax.broadcasted_iota(jnp.int32, (SUBLANES, c), 0)
        w = w_ref[...]
        b = b_ref[...]

        def chunk(r0, _):
            prv, cur, nxt = _neighbour_chunks(u_ref, up_ref, un_ref, r0, t, slice(None), i == 0, i == n_tiles - 1)
            acc = b + w[1:2] * cur
            acc += w[0:1] * _shift_rows(prv, cur, nxt, -1, rows)
            acc += w[2:3] * _shift_rows(prv, cur, nxt, 1, rows)
            acc += w[3:4] * _shift_rows(prv, cur, nxt, 2, rows)
            xc_ref[pl.ds(r0, SUBLANES), :] = acc

        _chunk_loop(t, chunk)

    return pl.pallas_call(
        body, name="conv_fwd", grid=(n_tiles,),
        in_specs=_halo_specs(t, s, c, 1) + [pl.BlockSpec((4, c), lambda i: (0, 0)), pl.BlockSpec((1, c), lambda i: (0, 0))],
        out_specs=pl.BlockSpec((t, c), lambda i: (i, 0)),
        out_shape=jax.ShapeDtypeStruct((s, c), F32),
        compiler_params=_params("arbitrary"),
    )(proj, proj, proj, conv_w, conv_b)


def _gate_matmuls(xc_ref, wa_ref, wi_ref, pr_s, pi_s, heads):
    for h in range(heads):
        cs = pl.ds(h * LRU_HEAD, LRU_HEAD)
        xb = xc_ref[:, cs].astype(BF16)
        pr_s[:, cs] = jnp.dot(xb, wa_ref[h], preferred_element_type=F32)
        pi_s[:, cs] = jnp.dot(xb, wi_ref[h], preferred_element_type=F32)


def _rg_gates(pr, pi, ba, bi, sp):
    r = _sigmoid(pr + ba)
    ig = _sigmoid(pi + bi)
    log_a = (-RG_C * r) * sp
    a = jnp.exp(log_a)
    mult = jnp.sqrt(_neg_expm1(2.0 * log_a))
    return r, ig, a, mult


def _scan_fwd_call(xc, wa, wi, ba, bi, lam, reverse, dep):
    s, c = xc.shape
    heads = c // LRU_HEAD
    t = min(SEQ_TILE, s)
    n_tiles = s // t
    tile = (lambda i: (n_tiles - 1 - i, 0)) if reverse else (lambda i: (i, 0))
    whole2 = lambda i: (0, 0)
    whole3 = lambda i: (0, 0, 0)

    def body(xc_ref, wa_ref, wi_ref, ba_ref, bi_ref, lam_ref, dep_ref, h_ref, r_ref, ig_ref, a_ref, mult_ref,
             pr_s, pi_s, carry_s):
        @pl.when(pl.program_id(0) == 0)
        def _():
            carry_s[...] = jnp.zeros_like(carry_s)

        _gate_matmuls(xc_ref, wa_ref, wi_ref, pr_s, pi_s, heads)
        ba_v, bi_v = ba_ref[...], bi_ref[...]
        sp = _softplus(-lam_ref[...])

        rows = lax.broadcasted_iota(jnp.int32, (SUBLANES, c), 0)

        def chunk(r0, h_in):
            rs = pl.ds(r0, SUBLANES)
            r, ig, a, mult = _rg_gates(pr_s[rs, :], pi_s[rs, :], ba_v, bi_v, sp)
            r_ref[rs, :] = r
            ig_ref[rs, :] = ig
            a_ref[rs, :] = a
            mult_ref[rs, :] = mult
            h = _scan_chunk(a, mult * ig * xc_ref[rs, :], h_in, rows, reverse)
            h_ref[rs, :] = h
            return h[0:1, :] if reverse else h[SUBLANES - 1:SUBLANES, :]

        carry_s[...] = _chunk_loop(t, chunk, carry_s[...], unroll=2, descending=reverse)

    return pl.pallas_call(
        body, name="scan_fwd_rev" if reverse else "scan_fwd", grid=(n_tiles,),
        in_specs=[pl.BlockSpec((t, c), tile),
                  pl.BlockSpec((heads, LRU_HEAD, LRU_HEAD), whole3), pl.BlockSpec((heads, LRU_HEAD, LRU_HEAD), whole3),
                  pl.BlockSpec((1, c), whole2), pl.BlockSpec((1, c), whole2), pl.BlockSpec((1, c), whole2), ANY],
        out_specs=[pl.BlockSpec((t, c), tile)] * 5,
        out_shape=[jax.ShapeDtypeStruct((s, c), F32)] * 5,
        scratch_shapes=[pltpu.VMEM((t, c), F32), pltpu.VMEM((t, c), F32), pltpu.VMEM((1, c), F32)],
        compiler_params=_params("arbitrary"),
    )(xc, wa, wi, ba, bi, lam, dep)


def _window_counts(r0, tile_idx, t_rows, n_rows, half, shape):
    pos = tile_idx * t_rows + r0 + lax.broadcasted_iota(jnp.int32, shape, 0)
    hi = jnp.minimum(pos + half, n_rows)
    lo = jnp.maximum(pos - half, 0)
    return (hi - lo).astype(F32)


def _window_inverse_counts(r0, tile_idx, t_rows, n_rows, half, width):
    inv = 1.0 / _window_counts(r0, tile_idx, t_rows, n_rows, half, (SUBLANES, LANES))
    return jnp.tile(inv, (1, width // LANES))


def _pool_combine_call(proj, h_f, h_b, w_pool, pool_scale, p):
    s = proj.shape[0]
    c = h_f.shape[1]
    pg = p // N_POOL_GROUPS
    t = min(SEQ_TILE, s)
    n_tiles = s // t

    def body(u_ref, up_ref, un_ref, gate_ref, hf_ref, hb_ref, wp_ref, sc_ref, y_ref, d_ref, d_s, yr_s):
        i = pl.program_id(0)
        rows = lax.broadcasted_iota(jnp.int32, (SUBLANES, pg), 0)

        def chunk(r0, _):
            rs = pl.ds(r0, SUBLANES)
            for g, w in enumerate(POOL_WINDOWS):
                cols = pl.ds(g * pg, pg)
                prv, cur, nxt = _neighbour_chunks(u_ref, up_ref, un_ref, r0, t, cols, i == 0, i == n_tiles - 1)
                tot = cur
                for o in range(-(w // 2), w // 2):
                    if o != 0:
                        tot = tot + _shift_rows(prv, cur, nxt, o, rows)
                d_s[rs, cols] = tot * _window_inverse_counts(r0, i, t, s, w // 2, pg) - cur
            gate, _ = _gelu_and_grad(gate_ref[rs, :])
            yr_s[rs, :] = (hf_ref[rs, :] + hb_ref[rs, :]) * gate

        _chunk_loop(t, chunk)
        y_ref[:, pl.ds(p, c)] = yr_s[...].astype(BF16)
        d_ref[...] = d_s[...].astype(BF16)
        for g in range(N_POOL_GROUPS):
            cols = pl.ds(g * pg, pg)
            out = jnp.dot(d_s[:, cols].astype(BF16), wp_ref[g], preferred_element_type=F32)
            y_ref[:, cols] = (out * sc_ref[:, cols]).astype(BF16)

    return pl.pallas_call(
        body, name="pool_combine", grid=(n_tiles,),
        in_specs=_halo_specs(t, s, p, 0) + [
            pl.BlockSpec((t, c), lambda i: (i, 2)),
            pl.BlockSpec((t, c), lambda i: (i, 0)), pl.BlockSpec((t, c), lambda i: (i, 0)),
            pl.BlockSpec((N_POOL_GROUPS, pg, pg), lambda i: (0, 0, 0)), pl.BlockSpec((1, p), lambda i: (0, 0))],
        out_specs=[pl.BlockSpec((t, p + c), lambda i: (i, 0)), pl.BlockSpec((t, p), lambda i: (i, 0))],
        out_shape=[jax.ShapeDtypeStruct((s, p + c), BF16), jax.ShapeDtypeStruct((s, p), BF16)],
        scratch_shapes=[pltpu.VMEM((t, p), F32), pltpu.VMEM((t, c), F32)],
        compiler_params=_params("arbitrary"),
    )(proj, proj, proj, proj, h_f, h_b, w_pool, pool_scale)


def _layer_norm_rows(z, g, b):
    mu = jnp.mean(z, axis=-1, keepdims=True)
    zc = z - mu
    var = jnp.mean(zc * zc, axis=-1, keepdims=True)
    rstd = lax.rsqrt(var + LN_EPS)
    xh = zc * rstd
    return xh, rstd, xh * g + b


def _layer_norm_bwd_rows(dx, xh, rstd, g):
    dxh = dx * g
    m1 = jnp.mean(dxh, axis=-1, keepdims=True)
    m2 = jnp.mean(dxh * xh, axis=-1, keepdims=True)
    return rstd * (dxh - m1 - xh * m2)


def _out_ln1_call(y, w_out, x, g1, b1, dep):
    s, d = x.shape
    tm = min(SEQ_TILE, s)

    def body(y_ref, w_ref, x_ref, g_ref, b_ref, dep_ref, xh_ref, x1b_ref, rstd_ref, acc_s, x1_s):
        acc_s[...] = jnp.dot(y_ref[...], w_ref[...], preferred_element_type=F32)
        g, b = g_ref[...], b_ref[...]

        def chunk(r0, _):
            rs = pl.ds(r0, SUBLANES)
            xh, rstd, x1 = _layer_norm_rows(ALPHA * x_ref[rs, :] + acc_s[rs, :], g, b)
            xh_ref[rs, :] = xh
            x1_s[rs, :] = x1
            rstd_ref[rs, :] = rstd

        _chunk_loop(tm, chunk, unroll=LN_UNROLL)
        x1b_ref[...] = x1_s[...].astype(BF16)

    return pl.pallas_call(
        body, name="out_ln1", grid=(s // tm,),
        in_specs=[pl.BlockSpec((tm, d), lambda i: (i, 0)), pl.BlockSpec((d, d), lambda i: (0, 0)),
                  pl.BlockSpec((tm, d), lambda i: (i, 0)),
                  pl.BlockSpec((1, d), lambda i: (0, 0)), pl.BlockSpec((1, d), lambda i: (0, 0)), ANY],
        out_specs=[pl.BlockSpec((tm, d), lambda i: (i, 0)), pl.BlockSpec((tm, d), lambda i: (i, 0)),
                   pl.BlockSpec((tm, 1), lambda i: (i, 0))],
        out_shape=[jax.ShapeDtypeStruct((s, d), F32), jax.ShapeDtypeStruct((s, d), BF16), jax.ShapeDtypeStruct((s, 1), F32)],
        scratch_shapes=[pltpu.VMEM((tm, d), F32), pltpu.VMEM((tm, d), F32)],
        compiler_params=_params("arbitrary"),
    )(y, w_out, x, g1, b1, dep)


def _mlp_in_call(x1b, w1, dep, done):
    s, d = x1b.shape
    n, _, f4 = w1.shape
    tm = min(2 * MM_TILE, s)
    tn = min(1024, f4)
    per = f4 // tn
    blocks = n * per // 2
    first = 0 if done is None else blocks
    extra = [] if done is None else list(done)

    def body(x_ref, w_ref, dep_ref, *rest):
        r_ref, q_ref = rest[-2:]
        r = jnp.maximum(jnp.dot(x_ref[...], w_ref[...], preferred_element_type=F32), 0.0)
        r_ref[...] = r.astype(BF16)
        q_ref[...] = (r * r).astype(BF16)

    out_spec = pl.BlockSpec((tm, tn), lambda j, i: (i, first + j))
    return pl.pallas_call(
        body, name="mlp_in" if done is None else "mlp_in_rest", grid=(blocks, s // tm),
        in_specs=[pl.BlockSpec((tm, d), lambda j, i: (i, 0)),
                  pl.BlockSpec((None, d, tn), lambda j, i: ((first + j) // per, 0, (first + j) % per)), ANY] + [ANY] * len(extra),
        out_specs=[out_spec, out_spec],
        out_shape=[jax.ShapeDtypeStruct((s, n * f4), BF16), jax.ShapeDtypeStruct((s, n * f4), BF16)],
        input_output_aliases={3: 0, 4: 1} if extra else {},
        compiler_params=_params("arbitrary", "arbitrary"),
    )(x1b, w1, dep, *extra)


def _mlp_out_ln2_call(hsq, w2, xh1, g1, b1, g2, b2, target):
    s, f = hsq.shape
    d = w2.shape[1]
    tm = min(MM_TILE, s)
    tk = min(LN_MM_K, f)
    nk = f // tk

    def body(h_ref, w_ref, xh1_ref, g1_ref, b1_ref, g2_ref, b2_ref, t_ref,
             dz_ref, dzb_ref, loss_ref, dg_ref, db_ref, acc_s):
        i, k = pl.program_id(0), pl.program_id(1)

        @pl.when((i == 0) & (k == 0))
        def _():
            loss_ref[...] = jnp.zeros_like(loss_ref)
            dg_ref[...] = jnp.zeros_like(dg_ref)
            db_ref[...] = jnp.zeros_like(db_ref)

        @pl.when(k == 0)
        def _():
            acc_s[...] = jnp.zeros_like(acc_s)

        acc_s[...] += jnp.dot(h_ref[...], w_ref[...], preferred_element_type=F32)

        @pl.when(k == nk - 1)
        def _():
            g1, b1, g2, b2 = g1_ref[...], b1_ref[...], g2_ref[...], b2_ref[...]

            def chunk(r0, _):
                rs = pl.ds(r0, SUBLANES)
                x1 = xh1_ref[rs, :] * g1 + b1
                xh2, rstd, x2 = _layer_norm_rows(ALPHA * x1 + acc_s[rs, :], g2, b2)
                diff = x2 - t_ref[rs, :]
                loss_ref[...] += diff * diff
                dx2 = diff * (1.0 / d)
                dg_ref[...] += dx2 * xh2
                db_ref[...] += dx2
                dz = _layer_norm_bwd_rows(dx2, xh2, rstd, g2)
                dz_ref[rs, :] = dz

            _chunk_loop(tm, chunk, unroll=LN_UNROLL)
            dzb_ref[...] = dz_ref[...].astype(BF16)

    row = lambda i, k: (i, 0)
    vec = lambda i, k: (0, 0)
    return pl.pallas_call(
        body, name="mlp_out_ln2", grid=(s // tm, nk),
        in_specs=[pl.BlockSpec((tm, tk), lambda i, k: (i, k)), pl.BlockSpec((tk, d), lambda i, k: (k, 0)),
                  pl.BlockSpec((tm, d), row), pl.BlockSpec((1, d), vec), pl.BlockSpec((1, d), vec),
                  pl.BlockSpec((1, d), vec), pl.BlockSpec((1, d), vec), pl.BlockSpec((tm, d), row)],
        out_specs=[pl.BlockSpec((tm, d), row), pl.BlockSpec((tm, d), row),
                   pl.BlockSpec((SUBLANES, d), vec), pl.BlockSpec((SUBLANES, d), vec), pl.BlockSpec((SUBLANES, d), vec)],
        out_shape=[jax.ShapeDtypeStruct((s, d), F32), jax.ShapeDtypeStruct((s, d), BF16),
                   jax.ShapeDtypeStruct((SUBLANES, d), F32), jax.ShapeDtypeStruct((SUBLANES, d), F32),
                   jax.ShapeDtypeStruct((SUBLANES, d), F32)],
        scratch_shapes=[pltpu.VMEM((tm, d), F32)],
        compiler_params=_params("arbitrary", "arbitrary"),
    )(hsq, w2, xh1, g1, b1, g2, b2, target)


def _half_grad_call(a, b, half, row_sharded, init, name, dep):
    s, m = a.shape
    n = b.shape[1]
    if row_sharded:
        rows, cols = m // (2 * N_CHIPS), n
        tm = min(1024, rows)
        per = rows // tm
        tn = min(1024, cols)
        n_i, n_j = N_CHIPS * per, cols // tn
        a_block = lambda i, h: ((i // per) * 2 + h) * per + i % per
        out_block = lambda i, j: (i // per, i % per, j)
    else:
        rows, cols = m // 2, n // N_CHIPS
        tm = min(1024, rows)
        per = rows // tm
        tn = cols if cols % 1024 else 1024
        per_n = cols // tn
        n_i, n_j = per, N_CHIPS * per_n
        a_block = lambda i, h: h * per + i
        out_block = lambda i, j: (j // per_n, i, j % per_n)
    tk = min(4096, s)
    if row_sharded and tm < 1024 and s * n * 2 <= RESIDENT_OPERAND_BYTES:
        tk, tn, n_j = s, n, 1
    has_init = init is not None

    def body(half_ref, a_ref, b_ref, *rest):
        o_ref = rest[-1]

        @pl.when(pl.program_id(2) == 0)
        def _():
            o_ref[...] = rest[0][...] if has_init else jnp.zeros_like(o_ref)

        o_ref[...] += lax.dot_general(a_ref[...], b_ref[...], (((0,), (0,)), ((), ())), preferred_element_type=F32)

    out_spec = pl.BlockSpec((None, tm, tn), lambda i, j, k, h: out_block(i, j))
    in_specs = [pl.BlockSpec((tk, tm), lambda i, j, k, h: (k, a_block(i, h[0]))),
                pl.BlockSpec((tk, tn), lambda i, j, k, h: (k, j))]
    args = [a, b]
    if has_init:
        in_specs.append(out_spec)
        args.append(init)
    in_specs.append(ANY)
    args.append(dep)
    return pl.pallas_call(
        body, name=name,
        grid_spec=pltpu.PrefetchScalarGridSpec(num_scalar_prefetch=1, grid=(n_i, n_j, s // tk), in_specs=in_specs,
                                               out_specs=out_spec),
        out_shape=jax.ShapeDtypeStruct((N_CHIPS, rows, cols), F32),
        compiler_params=_params("arbitrary", "arbitrary", "arbitrary"),
    )(half, *args)


def _dhsq_call(dzb, w2, r, dep):
    s, d = dzb.shape
    f = w2.shape[0]
    tm = min(2 * MM_TILE, s)
    tn = min(1024, f)

    def body(dz_ref, w_ref, r_ref, dep_ref, o_ref):
        dh = lax.dot_general(dz_ref[...], w_ref[...], (((1,), (1,)), ((), ())), preferred_element_type=F32)
        o_ref[...] = (dh * (2.0 * r_ref[...].astype(F32))).astype(BF16)

    return pl.pallas_call(
        body, name="mlp_dpre", grid=(f // tn, s // tm),
        in_specs=[pl.BlockSpec((tm, d), lambda j, i: (i, 0)), pl.BlockSpec((tn, d), lambda j, i: (j, 0)),
                  pl.BlockSpec((tm, tn), lambda j, i: (i, j)), ANY],
        out_specs=pl.BlockSpec((tm, tn), lambda j, i: (i, j)),
        out_shape=jax.ShapeDtypeStruct((s, f), BF16),
        compiler_params=_params("arbitrary", "arbitrary"),
    )(dzb, w2, r, dep)


def _dx1_ln1_bwd_call(dpre, w1, dz2, xh1, rstd1, g1, dep):
    s, f = dpre.shape
    n, d, f4 = w1.shape
    tm = min(MM_TILE, s)
    tk = min(LN_MM_K, f4)
    per = f4 // tk
    nk = n * per

    def body(dp_ref, w_ref, dz2_ref, xh_ref, rstd_ref, g_ref, dep_ref, dz_ref, dzb_ref, dg_ref, db_ref, acc_s):
        i, k = pl.program_id(0), pl.program_id(1)

        @pl.when((i == 0) & (k == 0))
        def _():
            dg_ref[...] = jnp.zeros_like(dg_ref)
            db_ref[...] = jnp.zeros_like(db_ref)

        @pl.when(k == 0)
        def _():
            acc_s[...] = jnp.zeros_like(acc_s)

        acc_s[...] += lax.dot_general(dp_ref[...], w_ref[...], (((1,), (1,)), ((), ())), preferred_element_type=F32)

        @pl.when(k == nk - 1)
        def _():
            g = g_ref[...]

            def chunk(r0, _):
                rs = pl.ds(r0, SUBLANES)
                dx1 = acc_s[rs, :] + ALPHA * dz2_ref[rs, :]
                xh = xh_ref[rs, :]
                dg_ref[...] += dx1 * xh
                db_ref[...] += dx1
                dz = _layer_norm_bwd_rows(dx1, xh, rstd_ref[rs, :], g)
                dz_ref[rs, :] = dz

            _chunk_loop(tm, chunk, unroll=LN_UNROLL)
            dzb_ref[...] = dz_ref[...].astype(BF16)

    row = lambda i, k: (i, 0)
    vec = lambda i, k: (0, 0)
    return pl.pallas_call(
        body, name="dx1_ln1_bwd", grid=(s // tm, nk),
        in_specs=[pl.BlockSpec((tm, tk), lambda i, k: (i, k)),
                  pl.BlockSpec((None, d, tk), lambda i, k: (k // per, 0, k % per)),
                  pl.BlockSpec((tm, d), row), pl.BlockSpec((tm, d), row), pl.BlockSpec((tm, 1), row),
                  pl.BlockSpec((1, d), vec), ANY],
        out_specs=[pl.BlockSpec((tm, d), row), pl.BlockSpec((tm, d), row),
                   pl.BlockSpec((SUBLANES, d), vec), pl.BlockSpec((SUBLANES, d), vec)],
        out_shape=[jax.ShapeDtypeStruct((s, d), F32), jax.ShapeDtypeStruct((s, d), BF16),
                   jax.ShapeDtypeStruct((SUBLANES, d), F32), jax.ShapeDtypeStruct((SUBLANES, d), F32)],
        scratch_shapes=[pltpu.VMEM((tm, d), F32)],
        compiler_params=_params("arbitrary", "arbitrary"),
    )(dpre, w1, dz2, xh1, rstd1, g1, dep)


def _mixer_bwd_call(dzb, w_out, d_pool, proj, h_f, h_b, w_pool, pool_scale, p, dep):
    s, d = dzb.shape
    c = h_f.shape[1]
    pg = p // N_POOL_GROUPS
    t = min(SEQ_TILE, s)
    n_tiles = s // t

    def body(dz_ref, wo_ref, d_ref, gate_ref, hf_ref, hb_ref, wp_ref, sc_ref, dep_ref,
             e_ref, dh_ref, dgate_ref, dwp_ref, dsc_ref, dd_s, dy_s):
        i = pl.program_id(0)

        @pl.when(i == 0)
        def _():
            dwp_ref[...] = jnp.zeros_like(dwp_ref)
            dsc_ref[...] = jnp.zeros_like(dsc_ref)

        dy_s[...] = lax.dot_general(dz_ref[...], wo_ref[...], (((1,), (1,)), ((), ())), preferred_element_type=F32)

        for g in range(N_POOL_GROUPS):
            cols = pl.ds(g * pg, pg)
            dg = d_ref[:, cols]
            out = jnp.dot(dg, wp_ref[g], preferred_element_type=F32)
            dyp = dy_s[:, cols]
            prod = dyp * out
            dsc_ref[:, cols] += jnp.sum(prod.reshape(t // SUBLANES, SUBLANES, pg), axis=0)
            dout = (dyp * sc_ref[:, cols]).astype(BF16)
            dwp_ref[g] += lax.dot_general(dg, dout, (((0,), (0,)), ((), ())), preferred_element_type=F32)
            dd_s[:, cols] = lax.dot_general(dout, wp_ref[g], (((1,), (1,)), ((), ())), preferred_element_type=F32)

        def chunk(r0, _):
            rs = pl.ds(r0, SUBLANES)
            for g, w in enumerate(POOL_WINDOWS):
                cols = pl.ds(g * pg, pg)
                e_ref[rs, cols] = dd_s[rs, cols] * _window_inverse_counts(r0, i, t, s, w // 2, pg)
            gate, dgate = _gelu_and_grad(gate_ref[rs, :])
            dyr = dy_s[rs, pl.ds(p, c)]
            dh_ref[rs, :] = dyr * gate
            dd_s[rs, :] = dyr * (hf_ref[rs, :] + hb_ref[rs, :]) * dgate

        _chunk_loop(t, chunk)
        dgate_ref[...] = dd_s[...].astype(BF16)

    tile = lambda i: (i, 0)
    return pl.pallas_call(
        body, name="mixer_bwd", grid=(n_tiles,),
        in_specs=[pl.BlockSpec((t, d), tile), pl.BlockSpec((p + c, d), lambda i: (0, 0)), pl.BlockSpec((t, p), tile),
                  pl.BlockSpec((t, c), lambda i: (i, 2)), pl.BlockSpec((t, c), tile), pl.BlockSpec((t, c), tile),
                  pl.BlockSpec((N_POOL_GROUPS, pg, pg), lambda i: (0, 0, 0)), pl.BlockSpec((1, p), lambda i: (0, 0)), ANY],
        out_specs=[pl.BlockSpec((t, p), tile), pl.BlockSpec((t, c), tile), pl.BlockSpec((t, c), tile),
                   pl.BlockSpec((N_POOL_GROUPS, pg, pg), lambda i: (0, 0, 0)), pl.BlockSpec((SUBLANES, p), lambda i: (0, 0))],
        out_shape=[jax.ShapeDtypeStruct((s, p), F32), jax.ShapeDtypeStruct((s, c), F32), jax.ShapeDtypeStruct((s, c), BF16),
                   jax.ShapeDtypeStruct((N_POOL_GROUPS, pg, pg), F32), jax.ShapeDtypeStruct((SUBLANES, p), F32)],
        scratch_shapes=[pltpu.VMEM((t, p), F32), pltpu.VMEM((t, p + c), F32)],
        compiler_params=_params("arbitrary"),
    )(dzb, w_out, d_pool, proj, h_f, h_b, w_pool, pool_scale, dep)


def _scan_bwd_call(xc, dh, h_dir, gates, dxc_prev, wa, wi, lam, reverse, dep):
    s, c = xc.shape
    heads = c // LRU_HEAD
    t = min(SEQ_TILE, s)
    n_tiles = s // t
    per = t // SUBLANES
    last_blk = s // SUBLANES - 1
    tile = (lambda i: (i, 0)) if reverse else (lambda i: (n_tiles - 1 - i, 0))
    if reverse:
        halo = lambda i: (jnp.minimum((i + 1) * per, last_blk), 0)
    else:
        halo = lambda i: (jnp.maximum((n_tiles - 1 - i) * per - 1, 0), 0)
    whole2 = lambda i: (0, 0)
    whole3 = lambda i: (0, 0, 0)
    has_prev = dxc_prev is not None
    n_in = 11 + int(has_prev) + 1

    def body(*refs):
        xc_ref, dh_ref, h_ref, hh_ref, r_ref, ig_ref, a_ref, mult_ref = refs[:8]
        prev_ref = refs[8] if has_prev else None
        wa_ref, wi_ref, lam_ref = refs[n_in - 4:n_in - 1]
        dxc_ref, dwa_ref, dwi_ref, dba_ref, dbi_ref, dsp_ref = refs[n_in:n_in + 6]
        pr_s, pi_s, carry_s = refs[n_in + 6:]
        step = pl.program_id(0)
        tile_idx = step if reverse else n_tiles - 1 - step

        @pl.when(step == 0)
        def _():
            carry_s[...] = jnp.zeros_like(carry_s)
            dwa_ref[...] = jnp.zeros_like(dwa_ref)
            dwi_ref[...] = jnp.zeros_like(dwi_ref)
            dba_ref[...] = jnp.zeros_like(dba_ref)
            dbi_ref[...] = jnp.zeros_like(dbi_ref)
            dsp_ref[...] = jnp.zeros_like(dsp_ref)

        sp = _softplus(-lam_ref[...])
        rows = lax.broadcasted_iota(jnp.int32, (SUBLANES, c), 0)

        def chunk(r0, u_in):
            rs = pl.ds(r0, SUBLANES)
            xcv = xc_ref[rs, :]
            r, ig, a, mult = r_ref[rs, :], ig_ref[rs, :], a_ref[rs, :], mult_ref[rs, :]
            dhv = dh_ref[rs, :]
            u = _scan_chunk(a, a * dhv, u_in, rows, not reverse)
            if reverse:
                gt = dhv + jnp.where(rows >= 1, pltpu.roll(u, 1, 0), u_in)
                u_out = u[SUBLANES - 1:SUBLANES, :]
            else:
                gt = dhv + jnp.where(rows < SUBLANES - 1, pltpu.roll(u, SUBLANES - 1, 0), u_in)
                u_out = u[0:1, :]
            cur = h_ref[rs, :]
            if reverse:
                after = h_ref[pl.ds(pl.multiple_of(jnp.minimum(r0 + SUBLANES, t - SUBLANES), SUBLANES), SUBLANES), :]
                edge = jnp.where(tile_idx == n_tiles - 1, 0.0, hh_ref[...])
                nxt = jnp.where(r0 == t - SUBLANES, edge, after)
                hs = _shift_rows(cur, cur, nxt, 1, rows)
            else:
                before = h_ref[pl.ds(pl.multiple_of(jnp.maximum(r0 - SUBLANES, 0), SUBLANES), SUBLANES), :]
                edge = jnp.where(tile_idx == 0, 0.0, hh_ref[...])
                prv = jnp.where(r0 == 0, edge, before)
                hs = _shift_rows(prv, cur, cur, -1, rows)
            gx = gt * xcv
            dmult = gx * ig
            di = gx * mult
            dlog_a = (gt * hs) * a - dmult * (a * a) / mult
            dr = dlog_a * (-RG_C * sp)
            dsp_ref[...] += dlog_a * (-RG_C * r)
            dpr = dr * r * (1.0 - r)
            dpi = di * ig * (1.0 - ig)
            dba_ref[...] += dpr
            dbi_ref[...] += dpi
            direct = gt * mult * ig
            if has_prev:
                direct = direct + prev_ref[rs, :]
            dxc_ref[rs, :] = direct
            pr_s[rs, :] = dpr
            pi_s[rs, :] = dpi
            return u_out

        carry_s[...] = _chunk_loop(t, chunk, carry_s[...], unroll=2, descending=not reverse)

        for h in range(heads):
            cs = pl.ds(h * LRU_HEAD, LRU_HEAD)
            xb = xc_ref[:, cs].astype(BF16)
            dprb = pr_s[:, cs].astype(BF16)
            dpib = pi_s[:, cs].astype(BF16)
            dwa_ref[h] += lax.dot_general(xb, dprb, (((0,), (0,)), ((), ())), preferred_element_type=F32)
            dwi_ref[h] += lax.dot_general(xb, dpib, (((0,), (0,)), ((), ())), preferred_element_type=F32)
            dxc_ref[:, cs] += (
                lax.dot_general(dprb, wa_ref[h], (((1,), (1,)), ((), ())), preferred_element_type=F32)
                + lax.dot_general(dpib, wi_ref[h], (((1,), (1,)), ((), ())), preferred_element_type=F32))

    tile_spec = pl.BlockSpec((t, c), tile)
    in_specs = [tile_spec, tile_spec, tile_spec, pl.BlockSpec((SUBLANES, c), halo)] + [tile_spec] * 4
    args = [xc, dh, h_dir, h_dir, *gates]
    if has_prev:
        in_specs.append(tile_spec)
        args.append(dxc_prev)
    in_specs += [pl.BlockSpec((heads, LRU_HEAD, LRU_HEAD), whole3), pl.BlockSpec((heads, LRU_HEAD, LRU_HEAD), whole3),
                 pl.BlockSpec((1, c), whole2), ANY]
    args += [wa, wi, lam, dep]
    assert len(args) == n_in
    return pl.pallas_call(
        body, name="scan_bwd_rev" if reverse else "scan_bwd", grid=(n_tiles,),
        in_specs=in_specs,
        out_specs=[tile_spec,
                   pl.BlockSpec((heads, LRU_HEAD, LRU_HEAD), whole3), pl.BlockSpec((heads, LRU_HEAD, LRU_HEAD), whole3),
                   pl.BlockSpec((SUBLANES, c), whole2), pl.BlockSpec((SUBLANES, c), whole2), pl.BlockSpec((SUBLANES, c), whole2)],
        out_shape=[jax.ShapeDtypeStruct((s, c), F32),
                   jax.ShapeDtypeStruct((heads, LRU_HEAD, LRU_HEAD), F32), jax.ShapeDtypeStruct((heads, LRU_HEAD, LRU_HEAD), F32),
                   jax.ShapeDtypeStruct((SUBLANES, c), F32), jax.ShapeDtypeStruct((SUBLANES, c), F32),
                   jax.ShapeDtypeStruct((SUBLANES, c), F32)],
        scratch_shapes=[pltpu.VMEM((t, c), F32), pltpu.VMEM((t, c), F32), pltpu.VMEM((1, c), F32)],
        compiler_params=_params("arbitrary"),
    )(*args)


def _dproj_call(e_pool, dxc, proj, dgate, conv_w, p):
    s, c = dxc.shape
    pg = p // N_POOL_GROUPS
    t = min(SEQ_TILE, s)
    n_tiles = s // t

    def body(e_ref, ep_ref, en_ref, dx_ref, dxp_ref, dxn_ref, u_ref, up_ref, un_ref, dgate_ref, w_ref,
             dproj_ref, dcw_ref, dcb_ref, st_s):
        i = pl.program_id(0)
        first, last = i == 0, i == n_tiles - 1

        @pl.when(first)
        def _():
            dcw_ref[...] = jnp.zeros_like(dcw_ref)
            dcb_ref[...] = jnp.zeros_like(dcb_ref)

        rows_p = lax.broadcasted_iota(jnp.int32, (SUBLANES, pg), 0)
        rows_c = lax.broadcasted_iota(jnp.int32, (SUBLANES, c), 0)
        w = w_ref[...]

        def chunk(r0, _):
            rs = pl.ds(r0, SUBLANES)
            for g, win in enumerate(POOL_WINDOWS):
                cols = pl.ds(g * pg, pg)
                prv, cur, nxt = _neighbour_chunks(e_ref, ep_ref, en_ref, r0, t, cols, first, last)
                tot = cur
                for o in range(-(win // 2) + 1, win // 2 + 1):
                    if o != 0:
                        tot = tot + _shift_rows(prv, cur, nxt, o, rows_p)
                cnt = _window_counts(r0, i, t, s, win // 2, (SUBLANES, pg))
                st_s[rs, cols] = tot - cur * cnt
            prv, cur, nxt = _neighbour_chunks(dx_ref, dxp_ref, dxn_ref, r0, t, slice(None), first, last)
            du = w[1:2] * cur
            du += w[0:1] * _shift_rows(prv, cur, nxt, 1, rows_c)
            du += w[2:3] * _shift_rows(prv, cur, nxt, -1, rows_c)
            du += w[3:4] * _shift_rows(prv, cur, nxt, -2, rows_c)
            st_s[rs, pl.ds(p, c)] = du
            uprv, ucur, unxt = _neighbour_chunks(u_ref, up_ref, un_ref, r0, t, slice(None), first, last)
            dcb_ref[...] += cur
            for j, o in enumerate((-1, 0, 1, 2)):
                dcw_ref[j] += cur * _shift_rows(uprv, ucur, unxt, o, rows_c)

        _chunk_loop(t, chunk)
        dproj_ref[:, pl.ds(0, p + c)] = st_s[...].astype(BF16)
        dproj_ref[:, pl.ds(p + c, c)] = dgate_ref[...]

    return pl.pallas_call(
        body, name="dproj", grid=(n_tiles,),
        in_specs=_halo_specs(t, s, p, 0) + _halo_specs(t, s, c, 0) + _halo_specs(t, s, c, 1) + [
            pl.BlockSpec((t, c), lambda i: (i, 0)), pl.BlockSpec((4, c), lambda i: (0, 0))],
        out_specs=[pl.BlockSpec((t, p + 2 * c), lambda i: (i, 0)),
                   pl.BlockSpec((4, SUBLANES, c), lambda i: (0, 0, 0)), pl.BlockSpec((SUBLANES, c), lambda i: (0, 0))],
        out_shape=[jax.ShapeDtypeStruct((s, p + 2 * c), BF16), jax.ShapeDtypeStruct((4, SUBLANES, c), F32),
                   jax.ShapeDtypeStruct((SUBLANES, c), F32)],
        scratch_shapes=[pltpu.VMEM((t, p + c), F32)],
        compiler_params=_params("arbitrary"),
    )(e_pool, e_pool, e_pool, dxc, dxc, dxc, proj, proj, proj, dgate, conv_w)


def _dx_call(dproj, w_in, dz1, dep):
    s, e = dproj.shape
    n, d, e4 = w_in.shape
    tm = min(MM_TILE, s)

    def body(dp_ref, w_hbm, dz_ref, dep_ref, o_ref, w_s, sems):
        @pl.when(pl.program_id(0) == 0)
        def _():
            copies = [pltpu.make_async_copy(w_hbm.at[k], w_s.at[:, pl.ds(k * e4, e4)], sems.at[k]) for k in range(n)]
            for cp in copies:
                cp.start()
            for cp in copies:
                cp.wait()

        o_ref[...] = ALPHA * dz_ref[...] + lax.dot_general(
            dp_ref[...], w_s[...], (((1,), (1,)), ((), ())), preferred_element_type=F32)

    return pl.pallas_call(
        body, name="grad_x", grid=(s // tm,),
        in_specs=[pl.BlockSpec((tm, e), lambda i: (i, 0)), ANY, pl.BlockSpec((tm, d), lambda i: (i, 0)), ANY],
        out_specs=pl.BlockSpec((tm, d), lambda i: (i, 0)),
        out_shape=jax.ShapeDtypeStruct((s, d), F32),
        scratch_shapes=[pltpu.VMEM((d, e), BF16), pltpu.SemaphoreType.DMA((n,))],
        compiler_params=_params("arbitrary"),
    )(dproj, w_in, dz1, dep)


def _row_tile(rows, cols, n_arrays):
    limit = max(SUBLANES, ELT_BLOCK_BYTES // (4 * cols * max(1, n_arrays // 4)))
    best = SUBLANES
    for cand in range(SUBLANES, min(rows, limit) + 1, SUBLANES):
        if rows % cand == 0:
            best = cand
    return best if rows % SUBLANES == 0 else rows


def _cast_to_slot_call(a, idx, dtype, name, dep):
    rows, cols = a.shape
    tr = _row_tile(rows, cols, 2)
    extra = [] if dep is None else [dep]

    def body(idx_ref, a_ref, *rest):
        rest[-1][...] = a_ref[...].astype(dtype)

    return pl.pallas_call(
        body, name=name,
        grid_spec=pltpu.PrefetchScalarGridSpec(
            num_scalar_prefetch=1, grid=(rows // tr,),
            in_specs=[pl.BlockSpec((tr, cols), lambda i, idx_ref: (i, 0))] + [ANY] * len(extra),
            out_specs=pl.BlockSpec((None, tr, cols), lambda i, idx_ref: (idx_ref[1], i, 0))),
        out_shape=jax.ShapeDtypeStruct((N_CHIPS, rows, cols), dtype),
        compiler_params=_params("arbitrary"),
    )(idx, a, *extra)


def _add_half_call(g, recv, idx, name):
    _, rows, cols = g.shape
    tr = _row_tile(rows, cols, 3)

    def body(idx_ref, g_ref, r_ref, o_ref):
        o_ref[...] = g_ref[...] + r_ref[...]

    return pl.pallas_call(
        body, name=name,
        grid_spec=pltpu.PrefetchScalarGridSpec(
            num_scalar_prefetch=1, grid=(rows // tr,),
            in_specs=[pl.BlockSpec((None, tr, cols), lambda i, idx_ref: (idx_ref[0], i, 0)),
                      pl.BlockSpec((tr, cols), lambda i, idx_ref: (i, 0))],
            out_specs=pl.BlockSpec((None, tr, cols), lambda i, idx_ref: (idx_ref[1], i, 0))),
        out_shape=jax.ShapeDtypeStruct((N_CHIPS, rows, cols), F32),
        compiler_params=_params("arbitrary"),
    )(idx, g, recv)


def _sum_chips_call(own, recv, idx, name):
    _, rows, cols = recv.shape
    tr = _row_tile(rows, cols, 5)
    out_spec = pl.BlockSpec((None, tr, cols), lambda i, idx_ref: (idx_ref[0], i, 0))
    if own is None:
        def body(idx_ref, r_ref, o_ref):
            o_ref[...] = ((r_ref[0] + r_ref[1]) + r_ref[2]) + r_ref[3]
        in_specs = [pl.BlockSpec((N_CHIPS, tr, cols), lambda i, idx_ref: (0, i, 0))]
        args = (recv,)
    else:
        def body(idx_ref, p_ref, r_ref, o_ref):
            o_ref[...] = ((p_ref[...] + r_ref[0]) + r_ref[1]) + r_ref[2]
        in_specs = [pl.BlockSpec((None, tr, cols), lambda i, idx_ref: (idx_ref[1], i, 0)),
                    pl.BlockSpec((N_CHIPS - 1, tr, cols), lambda i, idx_ref: (0, i, 0))]
        args = (own, recv)
    return pl.pallas_call(
        body, name=name,
        grid_spec=pltpu.PrefetchScalarGridSpec(num_scalar_prefetch=1, grid=(rows // tr,), in_specs=in_specs, out_specs=out_spec),
        out_shape=jax.ShapeDtypeStruct((2, rows, cols), F32),
        compiler_params=_params("arbitrary"),
    )(idx, *args)


def _adamw_math(g, w, m, v):
    mn = ADAM_B1 * m + (1.0 - ADAM_B1) * g
    vn = ADAM_B2 * v + (1.0 - ADAM_B2) * (g * g)
    m_hat = mn / (1.0 - ADAM_B1 ** ADAM_STEP)
    v_hat = vn / (1.0 - ADAM_B2 ** ADAM_STEP)
    return -ADAM_LR * (m_hat / (jnp.sqrt(v_hat) + ADAM_EPS) + ADAM_WD * w), mn, vn


def _adamw_call(g, w, m, v, name):
    rows, cols = w.shape
    tr = _row_tile(rows, cols, 4)

    def body(g_ref, w_ref, m_ref, v_ref, go_ref, d_ref, mo_ref, vo_ref):
        gv = g_ref[...]
        go_ref[...] = gv
        d_ref[...], mo_ref[...], vo_ref[...] = _adamw_math(gv, w_ref[...], m_ref[...], v_ref[...])

    spec = pl.BlockSpec((tr, cols), lambda i: (i, 0))
    shape = jax.ShapeDtypeStruct((rows, cols), F32)
    return pl.pallas_call(
        body, name=name, grid=(rows // tr,),
        in_specs=[spec] * 4, out_specs=[spec] * 4, out_shape=[shape] * 4,
        compiler_params=_params("arbitrary"),
    )(g, w, m, v)


def _adamw_small_call(gs, ws, ms, vs):
    n = len(ws)

    def body(*refs):
        ins = [refs[k * n:(k + 1) * n] for k in range(4)]
        outs = [refs[(4 + k) * n:(5 + k) * n] for k in range(3)]
        for a in range(n):
            outs[0][a][...], outs[1][a][...], outs[2][a][...] = _adamw_math(*[ins[k][a][...] for k in range(4)])

    whole = pl.BlockSpec(memory_space=pltpu.VMEM)
    res = pl.pallas_call(
        body, name="adamw_small",
        in_specs=[whole] * (4 * n), out_specs=[whole] * (3 * n),
        out_shape=[jax.ShapeDtypeStruct(w.shape, F32) for w in ws] * 3,
        compiler_params=pltpu.CompilerParams(vmem_limit_bytes=VMEM_LIMIT),
    )(*gs, *ws, *ms, *vs)
    return res[:n], res[n:2 * n], res[2 * n:]


def _mesh_place():
    x, y, c = lax.axis_index("x"), lax.axis_index("y"), lax.axis_index("c")
    chips = [(1 - x, y), (x, 1 - y), (1 - x, 1 - y)]
    return x, y, c, chips


def _remote(src, dst, send_sems, recv_sems, idx, device):
    return pltpu.make_async_remote_copy(src_ref=src, dst_ref=dst, send_sem=send_sems.at[idx], recv_sem=recv_sems.at[idx],
                                        device_id=device, device_id_type=MESH)


HBM_SPEC = pl.BlockSpec(memory_space=pltpu.HBM)
SEM_SPEC = pl.BlockSpec(memory_space=pltpu.SEMAPHORE)
ORDERED_EFFECT = pltpu.SideEffectType.DATAFLOW_SIDE_EFFECTING


def _in_hbm(a):
    return pltpu.with_memory_space_constraint(a, pltpu.HBM)


def _start_copies_call(name, bufs, groups, after=None):
    n, g = len(bufs), len(groups)
    extra = [] if after is None else [after]
    first_out = n + len(extra)

    def body(*refs):
        outs = refs[first_out:first_out + n]
        sems = refs[first_out + n:first_out + n + 2 * g]
        token = refs[first_out + n + 2 * g]
        for i, (which, copies_fn, _) in enumerate(groups):
            for mine, _ in copies_fn([outs[w] for w in which], sems[2 * i], sems[2 * i + 1]):
                mine.start()
        token[...] = jnp.zeros_like(token)

    sem_shapes = [pltpu.SemaphoreType.DMA((cnt,)) for _, _, cnt in groups for _ in range(2)]
    res = pl.pallas_call(
        body, name=name,
        in_specs=[HBM_SPEC] * n + [ANY] * len(extra),
        out_specs=[HBM_SPEC] * n + [SEM_SPEC] * (2 * g) + [pl.BlockSpec(memory_space=pltpu.VMEM)],
        out_shape=[pltpu.HBM(a.shape, a.dtype) for a in bufs] + sem_shapes + [jax.ShapeDtypeStruct((SUBLANES, LANES), F32)],
        input_output_aliases={a: a for a in range(n)},
        compiler_params=pltpu.CompilerParams(has_side_effects=ORDERED_EFFECT),
    )(*[_in_hbm(a) for a in bufs], *extra)
    sems = res[n:n + 2 * g]
    return list(res[:n]), [(sems[2 * i], sems[2 * i + 1]) for i in range(g)], res[n + 2 * g]


def _wait_copies_call(name, bufs, sems, copies_fn, after):
    n = len(bufs)

    def body(*refs):
        ins = refs[:n]
        send_sems, recv_sems = refs[n], refs[n + 1]
        for mine, arriving in copies_fn(list(ins), send_sems, recv_sems):
            arriving.wait_recv()
            mine.wait_send()

    res = pl.pallas_call(
        body, name=name,
        in_specs=[HBM_SPEC] * n + [SEM_SPEC, SEM_SPEC, ANY],
        out_specs=[HBM_SPEC] * n,
        out_shape=[pltpu.HBM(a.shape, a.dtype) for a in bufs],
        input_output_aliases={a: a for a in range(n)},
        compiler_params=pltpu.CompilerParams(has_side_effects=ORDERED_EFFECT),
    )(*bufs, sems[0], sems[1], after)
    return list(res)


def _gather_copies(bufs, send_sems, recv_sems):
    x, y, c, chips = _mesh_place()
    k = 2 * x + y
    out = []
    for a, buf in enumerate(bufs):
        for j, (px, py) in enumerate(chips):
            kj = 2 * px + py
            mine = _remote(buf.at[k, c], buf.at[k, c], send_sems, recv_sems, 3 * a + j, (px, py, c))
            arriving = _remote(buf.at[k, c], buf.at[kj, c], send_sems, recv_sems, 3 * a + j, (px, py, c))
            out.append((mine, arriving))
    return out


def _exchange_copies(n_sharded, n_replicated):
    def copies(bufs, send_sems, recv_sems):
        x, y, c, chips = _mesh_place()
        k = 2 * x + y
        sums, lands = bufs[:n_sharded], bufs[n_sharded:2 * n_sharded]
        repl = bufs[2 * n_sharded:]
        out = []
        for j, (px, py) in enumerate(chips):
            kj = 2 * px + py
            for a in range(n_sharded):
                cp = _remote(sums[a].at[kj], lands[a].at[j], send_sems, recv_sems, 3 * a + j, (px, py, c))
                out.append((cp, cp))
            for a in range(n_replicated):
                idx = 3 * (n_sharded + a) + j
                mine = _remote(repl[a].at[k], repl[a].at[k], send_sems, recv_sems, idx, (px, py, c))
                arriving = _remote(repl[a].at[k], repl[a].at[kj], send_sems, recv_sems, idx, (px, py, c))
                out.append((mine, arriving))
        return out
    return copies


def _sibling_copies(n, halves):
    def copies(bufs, send_sems, recv_sems):
        x, y, c, _ = _mesh_place()
        out = []
        for a in range(n):
            src = bufs[a].at[1 - c] if halves else bufs[a]
            cp = _remote(src, bufs[n + a], send_sems, recv_sems, a, (x, y, 1 - c))
            out.append((cp, cp))
        return out
    return copies


def _forward_copies(bufs, send_sems, recv_sems):
    x, y, c, chips = _mesh_place()
    out = []
    for a, buf in enumerate(bufs):
        for j, (px, py) in enumerate(chips):
            kj = 2 * px + py
            mine = _remote(buf.at[kj, c], buf.at[kj, c], send_sems, recv_sems, 3 * a + j, (x, y, 1 - c))
            arriving = _remote(buf.at[kj, c], buf.at[kj, 1 - c], send_sems, recv_sems, 3 * a + j, (x, y, 1 - c))
            out.append((mine, arriving))
    return out


def _join_copies(bufs, send_sems, recv_sems):
    x, y, c, _ = _mesh_place()
    out = []
    for a, buf in enumerate(bufs):
        mine = _remote(buf.at[c], buf.at[c], send_sems, recv_sems, a, (x, y, 1 - c))
        arriving = _remote(buf.at[c], buf.at[1 - c], send_sems, recv_sems, a, (x, y, 1 - c))
        out.append((mine, arriving))
    return out


def _forward_to_sibling_call(bufs, name):
    n = len(bufs)

    def body(*refs):
        ins, outs = refs[:n], refs[n:2 * n]
        send_sems, recv_sems = refs[2 * n:]
        x, y, c, chips = _mesh_place()
        sibling = (x, y, 1 - c)
        sends = []
        for a in range(n):
            for j, (px, py) in enumerate(chips):
                kj = 2 * px + py
                sends.append(_remote(ins[a].at[kj, c], outs[a].at[kj, c], send_sems, recv_sems, 3 * a + j, sibling))
        for cp in sends:
            cp.start()
        for a in range(n):
            for j, (px, py) in enumerate(chips):
                kj = 2 * px + py
                _remote(ins[a].at[kj, c], outs[a].at[kj, 1 - c], send_sems, recv_sems, 3 * a + j, sibling).wait_recv()
        for cp in sends:
            cp.wait_send()

    return pl.pallas_call(
        body, name=name,
        in_specs=[ANY] * n, out_specs=[ANY] * n,
        out_shape=[jax.ShapeDtypeStruct(a.shape, a.dtype) for a in bufs],
        input_output_aliases={a: a for a in range(n)},
        scratch_shapes=[pltpu.SemaphoreType.DMA((3 * n,)), pltpu.SemaphoreType.DMA((3 * n,))],
    )(*bufs)


def _pack(arrays, rows_multiple):
    flat = jnp.concatenate([a.reshape(-1) for a in arrays])
    per = LANES * rows_multiple
    padded = -(-flat.shape[0] // per) * per
    flat = jnp.pad(flat, (0, padded - flat.shape[0]))
    return flat.reshape(-1, LANES)


def _unpack(packed, shapes):
    flat = packed.reshape(-1)
    out, at = [], 0
    for shp in shapes:
        size = 1
        for dim in shp:
            size *= dim
        out.append(flat[at:at + size].reshape(shp))
        at += size
    return out


def _halves(a):
    return a.reshape((2, a.shape[0] // 2) + a.shape[1:])


def kernel(x, ln_mix_g, ln_mix_b, w_in, w_pool, pool_scale, conv_w, conv_b, w_rg_a, b_rg_a, w_rg_i, b_rg_i, rg_lambda, w_out, ln_ffn_g, ln_ffn_b, w_mlp_in, w_mlp_out, loss_target, m_ln_mix_g, m_ln_mix_b, m_w_in, m_w_pool, m_pool_scale, m_conv_w, m_conv_b, m_w_rg_a, m_b_rg_a, m_w_rg_i, m_b_rg_i, m_rg_lambda, m_w_out, m_ln_ffn_g, m_ln_ffn_b, m_w_mlp_in, m_w_mlp_out, v_ln_mix_g, v_ln_mix_b, v_w_in, v_w_pool, v_pool_scale, v_conv_w, v_conv_b, v_w_rg_a, v_b_rg_a, v_w_rg_i, v_b_rg_i, v_rg_lambda, v_w_out, v_ln_ffn_g, v_ln_ffn_b, v_w_mlp_in, v_w_mlp_out):
    weights = dict(ln_mix_g=ln_mix_g, ln_mix_b=ln_mix_b, w_in=w_in, w_pool=w_pool, pool_scale=pool_scale, conv_w=conv_w,
                   conv_b=conv_b, w_rg_a=w_rg_a, b_rg_a=b_rg_a, w_rg_i=w_rg_i, b_rg_i=b_rg_i, rg_lambda=rg_lambda,
                   w_out=w_out, ln_ffn_g=ln_ffn_g, ln_ffn_b=ln_ffn_b, w_mlp_in=w_mlp_in, w_mlp_out=w_mlp_out)
    m_in = dict(ln_mix_g=m_ln_mix_g, ln_mix_b=m_ln_mix_b, w_in=m_w_in, w_pool=m_w_pool, pool_scale=m_pool_scale,
                conv_w=m_conv_w, conv_b=m_conv_b, w_rg_a=m_w_rg_a, b_rg_a=m_b_rg_a, w_rg_i=m_w_rg_i, b_rg_i=m_b_rg_i,
                rg_lambda=m_rg_lambda, w_out=m_w_out, ln_ffn_g=m_ln_ffn_g, ln_ffn_b=m_ln_ffn_b, w_mlp_in=m_w_mlp_in,
                w_mlp_out=m_w_mlp_out)
    v_in = dict(ln_mix_g=v_ln_mix_g, ln_mix_b=v_ln_mix_b, w_in=v_w_in, w_pool=v_w_pool, pool_scale=v_pool_scale,
                conv_w=v_conv_w, conv_b=v_conv_b, w_rg_a=v_w_rg_a, b_rg_a=v_b_rg_a, w_rg_i=v_w_rg_i, b_rg_i=v_b_rg_i,
                rg_lambda=v_rg_lambda, w_out=v_w_out, ln_ffn_g=v_ln_ffn_g, ln_ffn_b=v_ln_ffn_b, w_mlp_in=v_w_mlp_in,
                w_mlp_out=v_w_mlp_out)
    names = list(weights)

    xs = x[0]
    tgt = loss_target[0]
    s, d = xs.shape
    p = c = d // 2
    pg = p // N_POOL_GROUPS
    core = lax.axis_index("c")
    shard = 2 * lax.axis_index("x") + lax.axis_index("y")

    idx = jnp.stack([core, shard]).astype(jnp.int32)
    small_shard = _pack([conv_w[0], b_rg_a[0], b_rg_i[0], rg_lambda[0]], 2 * SUBLANES)
    to_gather = [(w_in[0], BF16), (w_out[0], BF16), (w_mlp_in[0], BF16), (w_mlp_out[0], BF16),
                 (w_pool[0].reshape(-1, pg), BF16), (small_shard, F32)]

    def slot_view(i, dep):
        a, dt = to_gather[i]
        sl = _cast_to_slot_call(a, idx, dt, f"gather_slot_{i}", dep)
        return sl.reshape(N_CHIPS, 2, sl.shape[1] // 2, sl.shape[2])

    first, later = (0, 4, 5), (1, 2, 3)
    fly_a, sems_a, token_a = _start_copies_call(
        "gather_start_first", [slot_view(i, None) for i in first], [((0, 1, 2), _gather_copies, 3 * len(first))])
    later_views = []
    for i in later:
        later_views.append(slot_view(i, later_views[-1] if later_views else token_a))
    xb = _cast_call(xs, later_views[-1])
    got_first = _wait_copies_call("gather_wait_w_in", fly_a, sems_a[0], _gather_copies, xb)
    fly_b, sems_b, g_token = _start_copies_call(
        "gather_start_later", later_views + got_first,
        [((0,), _gather_copies, 3), ((1,), _gather_copies, 3), ((2,), _gather_copies, 3)])
    got_first = fly_b[len(later):]
    in_flight = dict(zip(later, fly_b))
    g_sems = [None] + list(sems_b)

    def arrive(which, group, after, tag):
        return _wait_copies_call(f"gather_wait_{tag}", [in_flight[w] for w in which], g_sems[group], _gather_copies, after)

    def pass_on(got, tag):
        flying, sems, token = _start_copies_call(
            f"gather_forward_start_{tag}", got, [(tuple(range(len(got))), _forward_copies, 3 * len(got))])
        return (flying, sems[0], tag), token

    def passed_on(state, after):
        flying, sems, tag = state
        return _wait_copies_call(f"gather_forward_wait_{tag}", flying, sems, _forward_copies, after)

    gathered = [None] * len(to_gather)
    gathered[0], gathered[4], gathered[5] = _forward_to_sibling_call(got_first, "gather_forward_w_in")
    w_in_f = gathered[0].reshape((N_CHIPS,) + w_in.shape[1:])
    w_pool_f = gathered[4].reshape(N_CHIPS, N_POOL_GROUPS, pg // N_CHIPS, pg).transpose(1, 0, 2, 3).reshape(N_POOL_GROUPS, pg, pg)
    c4 = c // N_CHIPS
    small_parts = [_unpack(gathered[5][k].reshape(-1, LANES), [(4, c4), (2, c4), (2, c4), (2, c4)]) for k in range(N_CHIPS)]
    conv_w_f = jnp.concatenate([sp_[0] for sp_ in small_parts], axis=1)
    b_a_f = jnp.concatenate([sp_[1] for sp_ in small_parts], axis=1)
    b_i_f = jnp.concatenate([sp_[2] for sp_ in small_parts], axis=1)
    lam_f = jnp.concatenate([sp_[3] for sp_ in small_parts], axis=1)
    wa_b = w_rg_a[0].astype(BF16)
    wi_b = w_rg_i[0].astype(BF16)

    proj = _proj_call(xb, w_in_f)
    xc = _conv_call(proj, conv_w_f, conv_b, c)
    fwd_w_out, token = pass_on(arrive((1,), 1, xc, "w_out"), "w_out")
    h_b, *gates_b = _scan_fwd_call(xc, wa_b[1], wi_b[1], b_a_f[1:2], b_i_f[1:2], lam_f[1:2], True, token)
    h_f, *gates_f = _scan_fwd_call(xc, wa_b[0], wi_b[0], b_a_f[0:1], b_i_f[0:1], lam_f[0:1], False, token)
    y, d_pool = _pool_combine_call(proj, h_f, h_b, w_pool_f, pool_scale, p)
    w_out_f = passed_on(fwd_w_out, y)[0].reshape(d, d)
    fwd_w1, token = pass_on(arrive((2,), 2, y, "w_mlp_in"), "w_mlp_in")
    xh1, x1b, rstd1 = _out_ln1_call(y, w_out_f, xs, ln_mix_g, ln_mix_b, token)
    w1_f = passed_on(fwd_w1, x1b)[0].reshape((N_CHIPS,) + w_mlp_in.shape[1:])
    first_half = _mlp_in_call(x1b, w1_f, g_token, None)
    fwd_w2, token = pass_on(arrive((3,), 3, first_half[0], "w_mlp_out"), "w_mlp_out")
    r_act, hsq = _mlp_in_call(x1b, w1_f, token, first_half)
    w2_f = passed_on(fwd_w2, hsq)[0].reshape(N_CHIPS * w_mlp_out.shape[1], d)
    dz2, dz2b, loss8, dg2, db2 = _mlp_out_ln2_call(hsq, w2_f, xh1, ln_mix_g, ln_mix_b, ln_ffn_g, ln_ffn_b, tgt)

    def start_siblings(grads, halves, tag, after=None):
        lands = [lax.empty(g.shape[1:] if halves else g.shape, g.dtype) for g in grads]
        copies = _sibling_copies(len(grads), halves)
        flying, sems, token = _start_copies_call(
            f"siblings_start_{tag}", list(grads) + lands, [(tuple(range(2 * len(grads))), copies, len(grads))], after)
        return (flying, sems[0], copies, len(grads), tag), token

    def finish_siblings(state, after):
        flying, sems, copies, n, tag = state
        got = _wait_copies_call(f"siblings_wait_{tag}", flying, sems, copies, after)
        return got[:n], got[n:]

    half_own = jnp.reshape(core, (1,)).astype(jnp.int32)
    half_sibling = 1 - half_own

    def chip_sum_of(a, b, row_sharded, tag, dep, overlapped):
        for_sibling = _half_grad_call(a, b, half_sibling, row_sharded, None, f"grad_{tag}_for_sibling", dep)
        state, token = start_siblings([for_sibling], False, tag)
        results = overlapped(token)
        _, (from_sibling,) = finish_siblings(state, results[0])
        return _half_grad_call(a, b, half_own, row_sharded, from_sibling, f"grad_{tag}", token), results

    def start_exchange(sums, n_repl, tag):
        n_sh = len(sums) - n_repl
        lands = [lax.empty((N_CHIPS - 1,) + a.shape[1:], a.dtype) for a in sums[:n_sh]]
        bufs = sums[:n_sh] + lands + sums[n_sh:]
        copies = _exchange_copies(n_sh, n_repl)
        flying, sems, token = _start_copies_call(
            f"reduce_start_{tag}", bufs, [(tuple(range(len(bufs))), copies, 3 * len(sums))])
        return (flying, sems[0], copies, n_sh, tag), token

    def finish_exchange(state, after):
        flying, sems, copies, n_sh, tag = state
        got = _wait_copies_call(f"reduce_wait_{tag}", flying, sems, copies, after)
        halves = []
        for a in range(n_sh):
            own, land = got[a], got[n_sh + a]
            cols = own.shape[-1]
            total = _sum_chips_call(own.reshape(N_CHIPS, -1, cols), land.reshape(N_CHIPS - 1, -1, cols), idx,
                                    f"reduce_sum_{tag}_{a}")
            halves.append(total.reshape((2,) + own.shape[1:]))
        for a, rp in enumerate(got[2 * n_sh:]):
            halves.append(_sum_chips_call(None, rp, idx, f"reduce_sum_{tag}_r{a}"))
        return halves

    def start_join(halves, tag):
        flying, sems, token = _start_copies_call(
            f"join_start_{tag}", halves, [(tuple(range(len(halves))), _join_copies, len(halves))])
        return (flying, sems[0], tag), token

    def finish_join(state, after):
        flying, sems, tag = state
        return _wait_copies_call(f"join_wait_{tag}", flying, sems, _join_copies, after)

    sum_w2, (dpre,) = chip_sum_of(hsq, dz2b, True, "w_mlp_out", g_token,
                                  lambda tok: (_dhsq_call(dz2b, w2_f, r_act, tok),))
    flying_w2, token = start_exchange([sum_w2], 0, "w2")
    sum_w1, (dz1, dz1b, dg1, db1) = chip_sum_of(
        x1b, dpre, False, "w_mlp_in", token,
        lambda tok: _dx1_ln1_bwd_call(dpre, w1_f, dz2, xh1, rstd1, ln_mix_g, tok))
    flying_w1, token = start_exchange([sum_w1], 0, "w1")

    sum_wout, (e_pool, dh, dgate, g_wpool, g_pscale8) = chip_sum_of(
        y, dz1b, True, "w_out", token,
        lambda tok: _mixer_bwd_call(dz1b, w_out_f, d_pool, proj, h_f, h_b, w_pool_f, pool_scale, p, tok))
    flying_wout, token = start_exchange([sum_wout], 0, "w_out")
    dxc0, g_wa0, g_wi0, g_ba0, g_bi0, g_sp0 = _scan_bwd_call(
        xc, dh, h_f, gates_f, None, wa_b[0], wi_b[0], lam_f[0:1], False, token)
    dxc, g_wa1, g_wi1, g_ba1, g_bi1, g_sp1 = _scan_bwd_call(
        xc, dh, h_b, gates_b, dxc0, wa_b[1], wi_b[1], lam_f[1:2], True, token)
    dproj, g_cw8, g_cb8 = _dproj_call(e_pool, dxc, proj, dgate, conv_w_f, p)

    rowsum = lambda a8: jnp.sum(a8, axis=-2)
    g_lam = jnp.stack([rowsum(g_sp0), rowsum(g_sp1)]) * (-_sigmoid(-lam_f))
    small_grads = {
        "ln_mix_g": rowsum(dg1), "ln_mix_b": rowsum(db1), "ln_ffn_g": rowsum(dg2), "ln_ffn_b": rowsum(db2),
        "pool_scale": rowsum(g_pscale8), "conv_b": rowsum(g_cb8),
        "w_rg_a": jnp.stack([g_wa0, g_wa1]), "w_rg_i": jnp.stack([g_wi0, g_wi1]),
        "w_pool": g_wpool, "conv_w": rowsum(g_cw8),
        "b_rg_a": jnp.stack([rowsum(g_ba0), rowsum(g_ba1)]), "b_rg_i": jnp.stack([rowsum(g_bi0), rowsum(g_bi1)]),
        "rg_lambda": g_lam,
    }
    small_names = list(small_grads)
    small_shapes = [small_grads[nm].shape for nm in small_names]
    loss_share = jnp.reshape(jnp.sum(loss8) * (0.5 / d), (1,))
    g_small = _halves(_pack([small_grads[nm] for nm in small_names] + [loss_share], 2 * SUBLANES))
    sib_small, token = start_siblings([g_small], True, "small")
    flying_small = []

    def small_exchange_and_grad_x(tok):
        (mine,), (theirs,) = finish_siblings(sib_small, tok)
        small_sum = _add_half_call(mine, theirs, idx, "reduce_add_small")
        state, tok = start_exchange([small_sum], 1, "small")
        flying_small.append(state)
        return (_dx_call(dproj, w_in_f, dz1, tok),)

    sum_win, (grad_x,) = chip_sum_of(xb, dproj, False, "w_in", token, small_exchange_and_grad_x)
    flying_small = flying_small[0]
    flying_win, token = start_exchange([sum_win], 0, "w_in")

    grad_w, delta_w, new_m, new_v = {}, {}, {}, {}

    def adamw(nm, full):
        w2d = weights[nm][0]
        g2d = full.reshape(w2d.shape)
        go, dl, mn, vn = _adamw_call(g2d, w2d, m_in[nm][0], v_in[nm][0], f"adamw_{nm}")
        grad_w[nm], delta_w[nm], new_m[nm], new_v[nm] = go[None], dl[None], mn[None], vn[None]
        return vn

    join_w2, token = start_join(finish_exchange(flying_w2, token), "w2")
    join_w1, token = start_join(finish_exchange(flying_w1, token), "w1")
    join_wout, token = start_join(finish_exchange(flying_wout, token), "w_out")
    last = adamw("w_mlp_out", finish_join(join_w2, token)[0])
    last = adamw("w_mlp_in", finish_join(join_w1, last)[0])
    last = adamw("w_out", finish_join(join_wout, last)[0])

    join_small, token = start_join(finish_exchange(flying_small, last), "small")
    join_win, token = start_join(finish_exchange(flying_win, token), "w_in")
    small_joined = finish_join(join_small, token)[0]
    *small_sums, loss_sum = _unpack(small_joined.reshape(-1, LANES), small_shapes + [(1,)])
    small_full = dict(zip(small_names, small_sums))
    local = dict(small_full)
    local["w_pool"] = lax.dynamic_slice_in_dim(small_full["w_pool"], shard * (pg // N_CHIPS), pg // N_CHIPS, axis=1)
    for nm in ("conv_w", "b_rg_a", "b_rg_i", "rg_lambda"):
        local[nm] = lax.dynamic_slice_in_dim(small_full[nm], shard * c4, c4, axis=1)
    small_g = [local[nm].reshape(weights[nm].shape) for nm in small_names]
    small_d, small_m, small_v = _adamw_small_call(
        small_g, [weights[nm] for nm in small_names], [m_in[nm] for nm in small_names], [v_in[nm] for nm in small_names])
    for nm, gl, dl, mn, vn in zip(small_names, small_g, small_d, small_m, small_v):
        grad_w[nm], delta_w[nm], new_m[nm], new_v[nm] = gl, dl, mn, vn
    adamw("w_in", finish_join(join_win, small_v[0])[0])

    loss = loss_sum[0]
    return (loss, grad_x[None], *[grad_w[nm] for nm in names], *[delta_w[nm] for nm in names],
            *[new_m[nm] for nm in names], *[new_v[nm] for nm in names])
```

```python
import jax
import jax.numpy as jnp
from jax import lax
from jax.experimental import pallas as pl
from jax.experimental.pallas import tpu as pltpu

F32 = jnp.float32
BF16 = jnp.bfloat16

N_CHIPS = 4
LANES = 128
SUBLANES = 8
LRU_HEAD = 128
N_POOL_GROUPS = 4
POOL_WINDOWS = (2, 4, 8, 16)
RG_C = 8.0
LN_EPS = 1e-5
ALPHA = 2.0 ** 0.25
ADAM_LR, ADAM_B1, ADAM_B2, ADAM_EPS, ADAM_WD, ADAM_STEP = 0.001, 0.9, 0.999, 1e-08, 0.01, 10
VMEM_LIMIT = 56 * 1024 * 1024
SEQ_TILE = 512
MM_TILE = 512
LN_MM_K = 2048
LN_UNROLL = 8
ELT_BLOCK_BYTES = 2 * 1024 * 1024
RESIDENT_OPERAND_BYTES = 16 * 1024 * 1024
MESH = pl.DeviceIdType.MESH
ANY = pl.BlockSpec(memory_space=pl.ANY)


def _params(*sem):
    return pltpu.CompilerParams(dimension_semantics=sem, vmem_limit_bytes=VMEM_LIMIT)


def _sigmoid(z):
    return 1.0 / (1.0 + jnp.exp(-z))


def _neg_expm1(z):
    series = -(z * (1.0 + z * (0.5 + z * (1.0 / 6.0 + z * (1.0 / 24.0)))))
    return jnp.where(z > -0.01, series, 1.0 - jnp.exp(z))


def _softplus(z):
    return jnp.maximum(z, 0.0) + jnp.log1p(jnp.exp(-jnp.abs(z)))


_GELU_K = 0.7978845608028654
_GELU_C = 0.044715


def _gelu_and_grad(u):
    t = jnp.tanh(_GELU_K * (u + _GELU_C * (u * u * u)))
    g = 0.5 * u * (1.0 + t)
    dg = 0.5 * (1.0 + t) + 0.5 * u * (1.0 - t * t) * (_GELU_K * (1.0 + 3.0 * _GELU_C * u * u))
    return g, dg


def _shift_rows(prv, cur, nxt, o, rows):
    if o == 0:
        return cur
    if o == SUBLANES:
        return nxt
    if o == -SUBLANES:
        return prv
    if o > 0:
        return pltpu.roll(jnp.where(rows >= o, cur, nxt), SUBLANES - o, 0)
    p = -o
    return pltpu.roll(jnp.where(rows < SUBLANES - p, cur, prv), p, 0)


def _neighbour_chunks(main_ref, prev_ref, next_ref, r0, t_rows, cols, first_tile, last_tile):
    cur = main_ref[pl.ds(r0, SUBLANES), cols]
    before = main_ref[pl.ds(pl.multiple_of(jnp.maximum(r0 - SUBLANES, 0), SUBLANES), SUBLANES), cols]
    after = main_ref[pl.ds(pl.multiple_of(jnp.minimum(r0 + SUBLANES, t_rows - SUBLANES), SUBLANES), SUBLANES), cols]
    halo_prev = jnp.where(first_tile, 0.0, prev_ref[:, cols])
    halo_next = jnp.where(last_tile, 0.0, next_ref[:, cols])
    prv = jnp.where(r0 == 0, halo_prev, before)
    nxt = jnp.where(r0 == t_rows - SUBLANES, halo_next, after)
    return prv, cur, nxt


def _halo_specs(t_rows, n_rows, width, col_block):
    per = t_rows // SUBLANES
    last = n_rows // SUBLANES - 1
    return [
        pl.BlockSpec((t_rows, width), lambda i: (i, col_block)),
        pl.BlockSpec((SUBLANES, width), lambda i: (jnp.maximum(i * per - 1, 0), col_block)),
        pl.BlockSpec((SUBLANES, width), lambda i: (jnp.minimum((i + 1) * per, last), col_block)),
    ]


def _chunk_loop(t_rows, fn, init=None, unroll=1, descending=False):
    span = SUBLANES * unroll

    def step(ci, carry):
        base = pl.multiple_of(((t_rows // span - 1 - ci) if descending else ci) * span, span)
        for u in range(unroll):
            carry = fn(base + ((unroll - 1 - u) if descending else u) * SUBLANES, carry)
        return carry
    return lax.fori_loop(0, t_rows // span, step, init)


def _scan_chunk(a, b, h_in, rows, reverse):
    for dist in (1, 2, 4):
        if reverse:
            keep = rows < SUBLANES - dist
            shift = SUBLANES - dist
        else:
            keep = rows >= dist
            shift = dist
        b = a * jnp.where(keep, pltpu.roll(b, shift, 0), 0.0) + b
        a = a * jnp.where(keep, pltpu.roll(a, shift, 0), 1.0)
    return a * h_in + b


def _cast_call(x, dep):
    s, d = x.shape
    tm = min(MM_TILE, s)

    def body(x_ref, dep_ref, o_ref):
        o_ref[...] = x_ref[...].astype(BF16)

    return pl.pallas_call(
        body, name="cast_x", grid=(s // tm,),
        in_specs=[pl.BlockSpec((tm, d), lambda i: (i, 0)), ANY],
        out_specs=pl.BlockSpec((tm, d), lambda i: (i, 0)),
        out_shape=jax.ShapeDtypeStruct((s, d), BF16),
        compiler_params=_params("arbitrary"),
    )(x, dep)


def _proj_call(xb, w_in):
    s, d = xb.shape
    n, _, e4 = w_in.shape
    tm = min(MM_TILE, s)

    def body(x_ref, w_hbm, proj_ref, w_s, sems):
        @pl.when(pl.program_id(0) == 0)
        def _():
            copies = [pltpu.make_async_copy(w_hbm.at[k], w_s.at[:, pl.ds(k * e4, e4)], sems.at[k]) for k in range(n)]
            for cp in copies:
                cp.start()
            for cp in copies:
                cp.wait()

        proj_ref[...] = jnp.dot(x_ref[...], w_s[...], preferred_element_type=F32)

    return pl.pallas_call(
        body, name="proj", grid=(s // tm,),
        in_specs=[pl.BlockSpec((tm, d), lambda i: (i, 0)), ANY],
        out_specs=pl.BlockSpec((tm, n * e4), lambda i: (i, 0)),
        out_shape=jax.ShapeDtypeStruct((s, n * e4), F32),
        scratch_shapes=[pltpu.VMEM((d, n * e4), BF16), pltpu.SemaphoreType.DMA((n,))],
        compiler_params=_params("arbitrary"),
    )(xb, w_in)


def _conv_call(proj, conv_w, conv_b, c):
    s = proj.shape[0]
    t = min(SEQ_TILE, s)
    n_tiles = s // t

    def body(u_ref, up_ref, un_ref, w_ref, b_ref, xc_ref):
        i = pl.program_id(0)
        rows = lax.broadcasted_iota(jnp.int32, (SUBLANES, c), 0)
        w = w_ref[...]
        b = b_ref[...]

        def chunk(r0, _):
            prv, cur, nxt = _neighbour_chunks(u_ref, up_ref, un_ref, r0, t, slice(None), i == 0, i == n_tiles - 1)
            acc = b + w[1:2] * cur
            acc += w[0:1] * _shift_rows(prv, cur, nxt, -1, rows)
            acc += w[2:3] * _shift_rows(prv, cur, nxt, 1, rows)
            acc += w[3:4] * _shift_rows(prv, cur, nxt, 2, rows)
            xc_ref[pl.ds(r0, SUBLANES), :] = acc

        _chunk_loop(t, chunk)

    return pl.pallas_call(
        body, name="conv_fwd", grid=(n_tiles,),
        in_specs=_halo_specs(t, s, c, 1) + [pl.BlockSpec((4, c), lambda i: (0, 0)), pl.BlockSpec((1, c), lambda i: (0, 0))],
        out_specs=pl.BlockSpec((t, c), lambda i: (i, 0)),
        out_shape=jax.ShapeDtypeStruct((s, c), F32),
        compiler_params=_params("arbitrary"),
    )(proj, proj, proj, conv_w, conv_b)


def _gate_matmuls(xc_ref, wa_ref, wi_ref, pr_s, pi_s, heads):
    for h in range(heads):
        cs = pl.ds(h * LRU_HEAD, LRU_HEAD)
        xb = xc_ref[:, cs].astype(BF16)
        pr_s[:, cs] = jnp.dot(xb, wa_ref[h], preferred_element_type=F32)
        pi_s[:, cs] = jnp.dot(xb, wi_ref[h], preferred_element_type=F32)


def _rg_gates(pr, pi, ba, bi, sp):
    r = _sigmoid(pr + ba)
    ig = _sigmoid(pi + bi)
    log_a = (-RG_C * r) * sp
    a = jnp.exp(log_a)
    mult = jnp.sqrt(_neg_expm1(2.0 * log_a))
    return r, ig, a, mult


def _scan_fwd_call(xc, wa, wi, ba, bi, lam, reverse, dep, other):
    has_other = other is not None
    s, c = xc.shape
    heads = c // LRU_HEAD
    t = min(SEQ_TILE, s)
    n_tiles = s // t
    tile = (lambda i: (n_tiles - 1 - i, 0)) if reverse else (lambda i: (i, 0))
    whole2 = lambda i: (0, 0)
    whole3 = lambda i: (0, 0, 0)

    def body(xc_ref, wa_ref, wi_ref, ba_ref, bi_ref, lam_ref, dep_ref, *rest):
        other_ref = rest[0] if has_other else None
        h_ref, r_ref, ig_ref, a_ref, mult_ref = rest[int(has_other):int(has_other) + 5]
        sum_ref = rest[int(has_other) + 5] if has_other else None
        pr_s, pi_s, carry_s = rest[-3:]

        @pl.when(pl.program_id(0) == 0)
        def _():
            carry_s[...] = jnp.zeros_like(carry_s)

        _gate_matmuls(xc_ref, wa_ref, wi_ref, pr_s, pi_s, heads)
        ba_v, bi_v = ba_ref[...], bi_ref[...]
        sp = _softplus(-lam_ref[...])

        rows = lax.broadcasted_iota(jnp.int32, (SUBLANES, c), 0)

        def chunk(r0, h_in):
            rs = pl.ds(r0, SUBLANES)
            r, ig, a, mult = _rg_gates(pr_s[rs, :], pi_s[rs, :], ba_v, bi_v, sp)
            r_ref[rs, :] = r
            ig_ref[rs, :] = ig
            a_ref[rs, :] = a
            mult_ref[rs, :] = mult
            h = _scan_chunk(a, mult * ig * xc_ref[rs, :], h_in, rows, reverse)
            h_ref[rs, :] = h
            if has_other:
                sum_ref[rs, :] = h + other_ref[rs, :]
            return h[0:1, :] if reverse else h[SUBLANES - 1:SUBLANES, :]

        carry_s[...] = _chunk_loop(t, chunk, carry_s[...], unroll=2, descending=reverse)

    return pl.pallas_call(
        body, name="scan_fwd_rev" if reverse else "scan_fwd", grid=(n_tiles,),
        in_specs=[pl.BlockSpec((t, c), tile),
                  pl.BlockSpec((heads, LRU_HEAD, LRU_HEAD), whole3), pl.BlockSpec((heads, LRU_HEAD, LRU_HEAD), whole3),
                  pl.BlockSpec((1, c), whole2), pl.BlockSpec((1, c), whole2), pl.BlockSpec((1, c), whole2), ANY]
        + [pl.BlockSpec((t, c), tile)] * int(has_other),
        out_specs=[pl.BlockSpec((t, c), tile)] * (5 + int(has_other)),
        out_shape=[jax.ShapeDtypeStruct((s, c), F32)] * (5 + int(has_other)),
        scratch_shapes=[pltpu.VMEM((t, c), F32), pltpu.VMEM((t, c), F32), pltpu.VMEM((1, c), F32)],
        compiler_params=_params("arbitrary"),
    )(xc, wa, wi, ba, bi, lam, dep, *([other] if has_other else []))


def _window_counts(r0, tile_idx, t_rows, n_rows, half, shape):
    pos = tile_idx * t_rows + r0 + lax.broadcasted_iota(jnp.int32, shape, 0)
    hi = jnp.minimum(pos + half, n_rows)
    lo = jnp.maximum(pos - half, 0)
    return (hi - lo).astype(F32)


def _window_inverse_counts(r0, tile_idx, t_rows, n_rows, half, width):
    inv = 1.0 / _window_counts(r0, tile_idx, t_rows, n_rows, half, (SUBLANES, LANES))
    return jnp.tile(inv, (1, width // LANES))


def _pool_combine_call(proj, h_sum, w_pool, pool_scale, p):
    s = proj.shape[0]
    c = h_sum.shape[1]
    pg = p // N_POOL_GROUPS
    t = min(SEQ_TILE, s)
    n_tiles = s // t

    def body(u_ref, up_ref, un_ref, gate_ref, h_ref, wp_ref, sc_ref, y_ref, d_ref, d_s, yr_s):
        i = pl.program_id(0)
        rows = lax.broadcasted_iota(jnp.int32, (SUBLANES, pg), 0)

        def chunk(r0, _):
            rs = pl.ds(r0, SUBLANES)
            for g, w in enumerate(POOL_WINDOWS):
                cols = pl.ds(g * pg, pg)
                prv, cur, nxt = _neighbour_chunks(u_ref, up_ref, un_ref, r0, t, cols, i == 0, i == n_tiles - 1)
                tot = cur
                for o in range(-(w // 2), w // 2):
                    if o != 0:
                        tot = tot + _shift_rows(prv, cur, nxt, o, rows)
                d_s[rs, cols] = tot * _window_inverse_counts(r0, i, t, s, w // 2, pg) - cur
            gate, _ = _gelu_and_grad(gate_ref[rs, :])
            yr_s[rs, :] = h_ref[rs, :] * gate

        _chunk_loop(t, chunk)
        y_ref[:, pl.ds(p, c)] = yr_s[...].astype(BF16)
        d_ref[...] = d_s[...].astype(BF16)
        for g in range(N_POOL_GROUPS):
            cols = pl.ds(g * pg, pg)
            out = jnp.dot(d_s[:, cols].astype(BF16), wp_ref[g], preferred_element_type=F32)
            y_ref[:, cols] = (out * sc_ref[:, cols]).astype(BF16)

    return pl.pallas_call(
        body, name="pool_combine", grid=(n_tiles,),
        in_specs=_halo_specs(t, s, p, 0) + [
            pl.BlockSpec((t, c), lambda i: (i, 2)),
            pl.BlockSpec((t, c), lambda i: (i, 0)),
            pl.BlockSpec((N_POOL_GROUPS, pg, pg), lambda i: (0, 0, 0)), pl.BlockSpec((1, p), lambda i: (0, 0))],
        out_specs=[pl.BlockSpec((t, p + c), lambda i: (i, 0)), pl.BlockSpec((t, p), lambda i: (i, 0))],
        out_shape=[jax.ShapeDtypeStruct((s, p + c), BF16), jax.ShapeDtypeStruct((s, p), BF16)],
        scratch_shapes=[pltpu.VMEM((t, p), F32), pltpu.VMEM((t, c), F32)],
        compiler_params=_params("arbitrary"),
    )(proj, proj, proj, proj, h_sum, w_pool, pool_scale)


def _layer_norm_rows(z, g, b):
    mu = jnp.mean(z, axis=-1, keepdims=True)
    zc = z - mu
    var = jnp.mean(zc * zc, axis=-1, keepdims=True)
    rstd = lax.rsqrt(var + LN_EPS)
    xh = zc * rstd
    return xh, rstd, xh * g + b


def _layer_norm_bwd_rows(dx, xh, rstd, g):
    dxh = dx * g
    m1 = jnp.mean(dxh, axis=-1, keepdims=True)
    m2 = jnp.mean(dxh * xh, axis=-1, keepdims=True)
    return rstd * (dxh - m1 - xh * m2)


def _out_ln1_call(y, w_out, x, g1, b1, dep):
    s, d = x.shape
    tm = min(SEQ_TILE, s)

    def body(y_ref, w_ref, x_ref, g_ref, b_ref, dep_ref, xh_ref, x1b_ref, rstd_ref, acc_s, x1_s):
        acc_s[...] = jnp.dot(y_ref[...], w_ref[...], preferred_element_type=F32)
        g, b = g_ref[...], b_ref[...]

        def chunk(r0, _):
            rs = pl.ds(r0, SUBLANES)
            xh, rstd, x1 = _layer_norm_rows(ALPHA * x_ref[rs, :] + acc_s[rs, :], g, b)
            xh_ref[rs, :] = xh
            x1_s[rs, :] = x1
            rstd_ref[rs, :] = rstd

        _chunk_loop(tm, chunk, unroll=LN_UNROLL)
        x1b_ref[...] = x1_s[...].astype(BF16)

    return pl.pallas_call(
        body, name="out_ln1", grid=(s // tm,),
        in_specs=[pl.BlockSpec((tm, d), lambda i: (i, 0)), pl.BlockSpec((d, d), lambda i: (0, 0)),
                  pl.BlockSpec((tm, d), lambda i: (i, 0)),
                  pl.BlockSpec((1, d), lambda i: (0, 0)), pl.BlockSpec((1, d), lambda i: (0, 0)), ANY],
        out_specs=[pl.BlockSpec((tm, d), lambda i: (i, 0)), pl.BlockSpec((tm, d), lambda i: (i, 0)),
                   pl.BlockSpec((tm, 1), lambda i: (i, 0))],
        out_shape=[jax.ShapeDtypeStruct((s, d), F32), jax.ShapeDtypeStruct((s, d), BF16), jax.ShapeDtypeStruct((s, 1), F32)],
        scratch_shapes=[pltpu.VMEM((tm, d), F32), pltpu.VMEM((tm, d), F32)],
        compiler_params=_params("arbitrary"),
    )(y, w_out, x, g1, b1, dep)


def _mlp_in_call(x1b, w1, dep, done):
    s, d = x1b.shape
    n, _, f4 = w1.shape
    tm = min(2 * MM_TILE, s)
    tn = min(1024, f4)
    per = f4 // tn
    blocks = n * per // 2
    first = 0 if done is None else blocks
    extra = [] if done is None else list(done)

    def body(x_ref, w_ref, dep_ref, *rest):
        r_ref, q_ref = rest[-2:]
        r = jnp.maximum(jnp.dot(x_ref[...], w_ref[...], preferred_element_type=F32), 0.0)
        r_ref[...] = r.astype(BF16)
        q_ref[...] = (r * r).astype(BF16)

    out_spec = pl.BlockSpec((tm, tn), lambda j, i: (i, first + j))
    return pl.pallas_call(
        body, name="mlp_in" if done is None else "mlp_in_rest", grid=(blocks, s // tm),
        in_specs=[pl.BlockSpec((tm, d), lambda j, i: (i, 0)),
                  pl.BlockSpec((None, d, tn), lambda j, i: ((first + j) // per, 0, (first + j) % per)), ANY] + [ANY] * len(extra),
        out_specs=[out_spec, out_spec],
        out_shape=[jax.ShapeDtypeStruct((s, n * f4), BF16), jax.ShapeDtypeStruct((s, n * f4), BF16)],
        input_output_aliases={3: 0, 4: 1} if extra else {},
        compiler_params=_params("arbitrary", "arbitrary"),
    )(x1b, w1, dep, *extra)


def _mlp_out_ln2_call(hsq, w2, xh1, g1, b1, g2, b2, target):
    s, f = hsq.shape
    d = w2.shape[1]
    tm = min(MM_TILE, s)
    tk = min(LN_MM_K, f)
    nk = f // tk

    def body(h_ref, w_ref, xh1_ref, g1_ref, b1_ref, g2_ref, b2_ref, t_ref,
             dz_ref, dzb_ref, loss_ref, dg_ref, db_ref, acc_s):
        i, k = pl.program_id(0), pl.program_id(1)

        @pl.when((i == 0) & (k == 0))
        def _():
            loss_ref[...] = jnp.zeros_like(loss_ref)
            dg_ref[...] = jnp.zeros_like(dg_ref)
            db_ref[...] = jnp.zeros_like(db_ref)

        @pl.when(k == 0)
        def _():
            acc_s[...] = jnp.zeros_like(acc_s)

        acc_s[...] += jnp.dot(h_ref[...], w_ref[...], preferred_element_type=F32)

        @pl.when(k == nk - 1)
        def _():
            g1, b1, g2, b2 = g1_ref[...], b1_ref[...], g2_ref[...], b2_ref[...]

            def chunk(r0, _):
                rs = pl.ds(r0, SUBLANES)
                x1 = xh1_ref[rs, :] * g1 + b1
                xh2, rstd, x2 = _layer_norm_rows(ALPHA * x1 + acc_s[rs, :], g2, b2)
                diff = x2 - t_ref[rs, :]
                loss_ref[...] += diff * diff
                dx2 = diff * (1.0 / d)
                dg_ref[...] += dx2 * xh2
                db_ref[...] += dx2
                dz = _layer_norm_bwd_rows(dx2, xh2, rstd, g2)
                dz_ref[rs, :] = dz

            _chunk_loop(tm, chunk, unroll=LN_UNROLL)
            dzb_ref[...] = dz_ref[...].astype(BF16)

    row = lambda i, k: (i, 0)
    vec = lambda i, k: (0, 0)
    return pl.pallas_call(
        body, name="mlp_out_ln2", grid=(s // tm, nk),
        in_specs=[pl.BlockSpec((tm, tk), lambda i, k: (i, k)), pl.BlockSpec((tk, d), lambda i, k: (k, 0)),
                  pl.BlockSpec((tm, d), row), pl.BlockSpec((1, d), vec), pl.BlockSpec((1, d), vec),
                  pl.BlockSpec((1, d), vec), pl.BlockSpec((1, d), vec), pl.BlockSpec((tm, d), row)],
        out_specs=[pl.BlockSpec((tm, d), row), pl.BlockSpec((tm, d), row),
                   pl.BlockSpec((SUBLANES, d), vec), pl.BlockSpec((SUBLANES, d), vec), pl.BlockSpec((SUBLANES, d), vec)],
        out_shape=[jax.ShapeDtypeStruct((s, d), F32), jax.ShapeDtypeStruct((s, d), BF16),
                   jax.ShapeDtypeStruct((SUBLANES, d), F32), jax.ShapeDtypeStruct((SUBLANES, d), F32),
                   jax.ShapeDtypeStruct((SUBLANES, d), F32)],
        scratch_shapes=[pltpu.VMEM((tm, d), F32)],
        compiler_params=_params("arbitrary", "arbitrary"),
    )(hsq, w2, xh1, g1, b1, g2, b2, target)


def _half_grad_call(a, b, half, row_sharded, init, name, dep):
    s, m = a.shape
    n = b.shape[1]
    if row_sharded:
        rows, cols = m // (2 * N_CHIPS), n
        tm = min(1024, rows)
        per = rows // tm
        tn = min(1024, cols)
        n_i, n_j = N_CHIPS * per, cols // tn
        a_block = lambda i, h: ((i // per) * 2 + h) * per + i % per
        out_block = lambda i, j: (i // per, i % per, j)
    else:
        rows, cols = m // 2, n // N_CHIPS
        tm = min(1024, rows)
        per = rows // tm
        tn = cols if cols % 1024 else 1024
        per_n = cols // tn
        n_i, n_j = per, N_CHIPS * per_n
        a_block = lambda i, h: h * per + i
        out_block = lambda i, j: (j // per_n, i, j % per_n)
    tk = min(4096, s)
    if row_sharded and tm < 1024 and s * n * 2 <= RESIDENT_OPERAND_BYTES:
        tk, tn, n_j = s, n, 1
    has_init = init is not None

    def body(half_ref, a_ref, b_ref, *rest):
        o_ref = rest[-1]

        @pl.when(pl.program_id(2) == 0)
        def _():
            o_ref[...] = rest[0][...] if has_init else jnp.zeros_like(o_ref)

        o_ref[...] += lax.dot_general(a_ref[...], b_ref[...], (((0,), (0,)), ((), ())), preferred_element_type=F32)

    out_spec = pl.BlockSpec((None, tm, tn), lambda i, j, k, h: out_block(i, j))
    in_specs = [pl.BlockSpec((tk, tm), lambda i, j, k, h: (k, a_block(i, h[0]))),
                pl.BlockSpec((tk, tn), lambda i, j, k, h: (k, j))]
    args = [a, b]
    if has_init:
        in_specs.append(out_spec)
        args.append(init)
    in_specs.append(ANY)
    args.append(dep)
    return pl.pallas_call(
        body, name=name,
        grid_spec=pltpu.PrefetchScalarGridSpec(num_scalar_prefetch=1, grid=(n_i, n_j, s // tk), in_specs=in_specs,
                                               out_specs=out_spec),
        out_shape=jax.ShapeDtypeStruct((N_CHIPS, rows, cols), F32),
        compiler_params=_params("arbitrary", "arbitrary", "arbitrary"),
    )(half, *args)


def _dhsq_call(dzb, w2, r, dep):
    s, d = dzb.shape
    f = w2.shape[0]
    tm = min(2 * MM_TILE, s)
    tn = min(1024, f)

    def body(dz_ref, w_ref, r_ref, dep_ref, o_ref):
        dh = lax.dot_general(dz_ref[...], w_ref[...], (((1,), (1,)), ((), ())), preferred_element_type=F32)
        o_ref[...] = (dh * (2.0 * r_ref[...].astype(F32))).astype(BF16)

    return pl.pallas_call(
        body, name="mlp_dpre", grid=(f // tn, s // tm),
        in_specs=[pl.BlockSpec((tm, d), lambda j, i: (i, 0)), pl.BlockSpec((tn, d), lambda j, i: (j, 0)),
                  pl.BlockSpec((tm, tn), lambda j, i: (i, j)), ANY],
        out_specs=pl.BlockSpec((tm, tn), lambda j, i: (i, j)),
        out_shape=jax.ShapeDtypeStruct((s, f), BF16),
        compiler_params=_params("arbitrary", "arbitrary"),
    )(dzb, w2, r, dep)


def _dx1_ln1_bwd_call(dpre, w1, dz2, xh1, rstd1, g1, dep):
    s, f = dpre.shape
    n, d, f4 = w1.shape
    tm = min(MM_TILE, s)
    tk = min(LN_MM_K, f4)
    per = f4 // tk
    nk = n * per

    def body(dp_ref, w_ref, dz2_ref, xh_ref, rstd_ref, g_ref, dep_ref, dz_ref, dzb_ref, dg_ref, db_ref, acc_s):
        i, k = pl.program_id(0), pl.program_id(1)

        @pl.when((i == 0) & (k == 0))
        def _():
            dg_ref[...] = jnp.zeros_like(dg_ref)
            db_ref[...] = jnp.zeros_like(db_ref)

        @pl.when(k == 0)
        def _():
            acc_s[...] = jnp.zeros_like(acc_s)

        acc_s[...] += lax.dot_general(dp_ref[...], w_ref[...], (((1,), (1,)), ((), ())), preferred_element_type=F32)

        @pl.when(k == nk - 1)
        def _():
            g = g_ref[...]

            def chunk(r0, _):
                rs = pl.ds(r0, SUBLANES)
                dx1 = acc_s[rs, :] + ALPHA * dz2_ref[rs, :]
                xh = xh_ref[rs, :]
                dg_ref[...] += dx1 * xh
                db_ref[...] += dx1
                dz = _layer_norm_bwd_rows(dx1, xh, rstd_ref[rs, :], g)
                dz_ref[rs, :] = dz

            _chunk_loop(tm, chunk, unroll=LN_UNROLL)
            dzb_ref[...] = dz_ref[...].astype(BF16)

    row = lambda i, k: (i, 0)
    vec = lambda i, k: (0, 0)
    return pl.pallas_call(
        body, name="dx1_ln1_bwd", grid=(s // tm, nk),
        in_specs=[pl.BlockSpec((tm, tk), lambda i, k: (i, k)),
                  pl.BlockSpec((None, d, tk), lambda i, k: (k // per, 0, k % per)),
                  pl.BlockSpec((tm, d), row), pl.BlockSpec((tm, d), row), pl.BlockSpec((tm, 1), row),
                  pl.BlockSpec((1, d), vec), ANY],
        out_specs=[pl.BlockSpec((tm, d), row), pl.BlockSpec((tm, d), row),
                   pl.BlockSpec((SUBLANES, d), vec), pl.BlockSpec((SUBLANES, d), vec)],
        out_shape=[jax.ShapeDtypeStruct((s, d), F32), jax.ShapeDtypeStruct((s, d), BF16),
                   jax.ShapeDtypeStruct((SUBLANES, d), F32), jax.ShapeDtypeStruct((SUBLANES, d), F32)],
        scratch_shapes=[pltpu.VMEM((tm, d), F32)],
        compiler_params=_params("arbitrary", "arbitrary"),
    )(dpre, w1, dz2, xh1, rstd1, g1, dep)


def _mixer_bwd_call(dzb, w_out, d_pool, proj, h_sum, w_pool, pool_scale, p, dep):
    s, d = dzb.shape
    c = h_sum.shape[1]
    pg = p // N_POOL_GROUPS
    t = min(SEQ_TILE, s)
    n_tiles = s // t

    def body(dz_ref, wo_ref, d_ref, gate_ref, h_ref, wp_ref, sc_ref, dep_ref,
             e_ref, dh_ref, dgate_ref, dwp_ref, dsc_ref, dd_s, dy_s):
        i = pl.program_id(0)

        @pl.when(i == 0)
        def _():
            dwp_ref[...] = jnp.zeros_like(dwp_ref)
            dsc_ref[...] = jnp.zeros_like(dsc_ref)

        dy_s[...] = lax.dot_general(dz_ref[...], wo_ref[...], (((1,), (1,)), ((), ())), preferred_element_type=F32)

        for g in range(N_POOL_GROUPS):
            cols = pl.ds(g * pg, pg)
            dg = d_ref[:, cols]
            out = jnp.dot(dg, wp_ref[g], preferred_element_type=F32)
            dyp = dy_s[:, cols]
            prod = dyp * out
            dsc_ref[:, cols] += jnp.sum(prod.reshape(t // SUBLANES, SUBLANES, pg), axis=0)
            dout = (dyp * sc_ref[:, cols]).astype(BF16)
            dwp_ref[g] += lax.dot_general(dg, dout, (((0,), (0,)), ((), ())), preferred_element_type=F32)
            dd_s[:, cols] = lax.dot_general(dout, wp_ref[g], (((1,), (1,)), ((), ())), preferred_element_type=F32)

        def chunk(r0, _):
            rs = pl.ds(r0, SUBLANES)
            for g, w in enumerate(POOL_WINDOWS):
                cols = pl.ds(g * pg, pg)
                e_ref[rs, cols] = dd_s[rs, cols] * _window_inverse_counts(r0, i, t, s, w // 2, pg)
            gate, dgate = _gelu_and_grad(gate_ref[rs, :])
            dyr = dy_s[rs, pl.ds(p, c)]
            dh_ref[rs, :] = dyr * gate
            dd_s[rs, :] = dyr * h_ref[rs, :] * dgate

        _chunk_loop(t, chunk)
        dgate_ref[...] = dd_s[...].astype(BF16)

    tile = lambda i: (i, 0)
    return pl.pallas_call(
        body, name="mixer_bwd", grid=(n_tiles,),
        in_specs=[pl.BlockSpec((t, d), tile), pl.BlockSpec((p + c, d), lambda i: (0, 0)), pl.BlockSpec((t, p), tile),
                  pl.BlockSpec((t, c), lambda i: (i, 2)), pl.BlockSpec((t, c), tile),
                  pl.BlockSpec((N_POOL_GROUPS, pg, pg), lambda i: (0, 0, 0)), pl.BlockSpec((1, p), lambda i: (0, 0)), ANY],
        out_specs=[pl.BlockSpec((t, p), tile), pl.BlockSpec((t, c), tile), pl.BlockSpec((t, c), tile),
                   pl.BlockSpec((N_POOL_GROUPS, pg, pg), lambda i: (0, 0, 0)), pl.BlockSpec((SUBLANES, p), lambda i: (0, 0))],
        out_shape=[jax.ShapeDtypeStruct((s, p), F32), jax.ShapeDtypeStruct((s, c), F32), jax.ShapeDtypeStruct((s, c), BF16),
                   jax.ShapeDtypeStruct((N_POOL_GROUPS, pg, pg), F32), jax.ShapeDtypeStruct((SUBLANES, p), F32)],
        scratch_shapes=[pltpu.VMEM((t, p), F32), pltpu.VMEM((t, p + c), F32)],
        compiler_params=_params("arbitrary"),
    )(dzb, w_out, d_pool, proj, h_sum, w_pool, pool_scale, dep)


def _scan_bwd_call(xc, dh, h_dir, gates, dxc_prev, wa, wi, lam, reverse, dep):
    s, c = xc.shape
    heads = c // LRU_HEAD
    t = min(SEQ_TILE, s)
    n_tiles = s // t
    per = t // SUBLANES
    last_blk = s // SUBLANES - 1
    tile = (lambda i: (i, 0)) if reverse else (lambda i: (n_tiles - 1 - i, 0))
    if reverse:
        halo = lambda i: (jnp.minimum((i + 1) * per, last_blk), 0)
    else:
        halo = lambda i: (jnp.maximum((n_tiles - 1 - i) * per - 1, 0), 0)
    whole2 = lambda i: (0, 0)
    whole3 = lambda i: (0, 0, 0)
    has_prev = dxc_prev is not None
    n_in = 11 + int(has_prev) + 1

    def body(*refs):
        xc_ref, dh_ref, h_ref, hh_ref, r_ref, ig_ref, a_ref, mult_ref = refs[:8]
        prev_ref = refs[8] if has_prev else None
        wa_ref, wi_ref, lam_ref = refs[n_in - 4:n_in - 1]
        dxc_ref, dwa_ref, dwi_ref, dba_ref, dbi_ref, dsp_ref = refs[n_in:n_in + 6]
        pr_s, pi_s, carry_s = refs[n_in + 6:]
        step = pl.program_id(0)
        tile_idx = step if reverse else n_tiles - 1 - step

        @pl.when(step == 0)
        def _():
            carry_s[...] = jnp.zeros_like(carry_s)
            dwa_ref[...] = jnp.zeros_like(dwa_ref)
            dwi_ref[...] = jnp.zeros_like(dwi_ref)
            dba_ref[...] = jnp.zeros_like(dba_ref)
            dbi_ref[...] = jnp.zeros_like(dbi_ref)
            dsp_ref[...] = jnp.zeros_like(dsp_ref)

        sp = _softplus(-lam_ref[...])
        rows = lax.broadcasted_iota(jnp.int32, (SUBLANES, c), 0)

        def chunk(r0, u_in):
            rs = pl.ds(r0, SUBLANES)
            xcv = xc_ref[rs, :]
            r, ig, a, mult = r_ref[rs, :], ig_ref[rs, :], a_ref[rs, :], mult_ref[rs, :]
            dhv = dh_ref[rs, :]
            u = _scan_chunk(a, a * dhv, u_in, rows, not reverse)
            if reverse:
                gt = dhv + jnp.where(rows >= 1, pltpu.roll(u, 1, 0), u_in)
                u_out = u[SUBLANES - 1:SUBLANES, :]
            else:
                gt = dhv + jnp.where(rows < SUBLANES - 1, pltpu.roll(u, SUBLANES - 1, 0), u_in)
                u_out = u[0:1, :]
            cur = h_ref[rs, :]
            if reverse:
                after = h_ref[pl.ds(pl.multiple_of(jnp.minimum(r0 + SUBLANES, t - SUBLANES), SUBLANES), SUBLANES), :]
                edge = jnp.where(tile_idx == n_tiles - 1, 0.0, hh_ref[...])
                nxt = jnp.where(r0 == t - SUBLANES, edge, after)
                hs = _shift_rows(cur, cur, nxt, 1, rows)
            else:
                before = h_ref[pl.ds(pl.multiple_of(jnp.maximum(r0 - SUBLANES, 0), SUBLANES), SUBLANES), :]
                edge = jnp.where(tile_idx == 0, 0.0, hh_ref[...])
                prv = jnp.where(r0 == 0, edge, before)
                hs = _shift_rows(prv, cur, cur, -1, rows)
            gx = gt * xcv
            dmult = gx * ig
            di = gx * mult
            dlog_a = (gt * hs) * a - dmult * (a * a) / mult
            dr = dlog_a * (-RG_C * sp)
            dsp_ref[...] += dlog_a * (-RG_C * r)
            dpr = dr * r * (1.0 - r)
            dpi = di * ig * (1.0 - ig)
            dba_ref[...] += dpr
            dbi_ref[...] += dpi
            direct = gt * mult * ig
            if has_prev:
                direct = direct + prev_ref[rs, :]
            dxc_ref[rs, :] = direct
            pr_s[rs, :] = dpr
            pi_s[rs, :] = dpi
            return u_out

        carry_s[...] = _chunk_loop(t, chunk, carry_s[...], unroll=2, descending=not reverse)

        for h in range(heads):
            cs = pl.ds(h * LRU_HEAD, LRU_HEAD)
            xb = xc_ref[:, cs].astype(BF16)
            dprb = pr_s[:, cs].astype(BF16)
            dpib = pi_s[:, cs].astype(BF16)
            dwa_ref[h] += lax.dot_general(xb, dprb, (((0,), (0,)), ((), ())), preferred_element_type=F32)
            dwi_ref[h] += lax.dot_general(xb, dpib, (((0,), (0,)), ((), ())), preferred_element_type=F32)
            dxc_ref[:, cs] += (
                lax.dot_general(dprb, wa_ref[h], (((1,), (1,)), ((), ())), preferred_element_type=F32)
                + lax.dot_general(dpib, wi_ref[h], (((1,), (1,)), ((), ())), preferred_element_type=F32))

    tile_spec = pl.BlockSpec((t, c), tile)
    in_specs = [tile_spec, tile_spec, tile_spec, pl.BlockSpec((SUBLANES, c), halo)] + [tile_spec] * 4
    args = [xc, dh, h_dir, h_dir, *gates]
    if has_prev:
        in_specs.append(tile_spec)
        args.append(dxc_prev)
    in_specs += [pl.BlockSpec((heads, LRU_HEAD, LRU_HEAD), whole3), pl.BlockSpec((heads, LRU_HEAD, LRU_HEAD), whole3),
                 pl.BlockSpec((1, c), whole2), ANY]
    args += [wa, wi, lam, dep]
    assert len(args) == n_in
    return pl.pallas_call(
        body, name="scan_bwd_rev" if reverse else "scan_bwd", grid=(n_tiles,),
        in_specs=in_specs,
        out_specs=[tile_spec,
                   pl.BlockSpec((heads, LRU_HEAD, LRU_HEAD), whole3), pl.BlockSpec((heads, LRU_HEAD, LRU_HEAD), whole3),
                   pl.BlockSpec((SUBLANES, c), whole2), pl.BlockSpec((SUBLANES, c), whole2), pl.BlockSpec((SUBLANES, c), whole2)],
        out_shape=[jax.ShapeDtypeStruct((s, c), F32),
                   jax.ShapeDtypeStruct((heads, LRU_HEAD, LRU_HEAD), F32), jax.ShapeDtypeStruct((heads, LRU_HEAD, LRU_HEAD), F32),
                   jax.ShapeDtypeStruct((SUBLANES, c), F32), jax.ShapeDtypeStruct((SUBLANES, c), F32),
                   jax.ShapeDtypeStruct((SUBLANES, c), F32)],
        scratch_shapes=[pltpu.VMEM((t, c), F32), pltpu.VMEM((t, c), F32), pltpu.VMEM((1, c), F32)],
        compiler_params=_params("arbitrary"),
    )(*args)


def _dproj_call(e_pool, dxc, proj, dgate, conv_w, p):
    s, c = dxc.shape
    pg = p // N_POOL_GROUPS
    t = min(SEQ_TILE, s)
    n_tiles = s // t

    def body(e_ref, ep_ref, en_ref, dx_ref, dxp_ref, dxn_ref, u_ref, up_ref, un_ref, dgate_ref, w_ref,
             dproj_ref, dcw_ref, dcb_ref, st_s):
        i = pl.program_id(0)
        first, last = i == 0, i == n_tiles - 1

        @pl.when(first)
        def _():
            dcw_ref[...] = jnp.zeros_like(dcw_ref)
            dcb_ref[...] = jnp.zeros_like(dcb_ref)

        rows_p = lax.broadcasted_iota(jnp.int32, (SUBLANES, pg), 0)
        rows_c = lax.broadcasted_iota(jnp.int32, (SUBLANES, c), 0)
        w = w_ref[...]

        def chunk(r0, _):
            rs = pl.ds(r0, SUBLANES)
            for g, win in enumerate(POOL_WINDOWS):
                cols = pl.ds(g * pg, pg)
                prv, cur, nxt = _neighbour_chunks(e_ref, ep_ref, en_ref, r0, t, cols, first, last)
                tot = cur
                for o in range(-(win // 2) + 1, win // 2 + 1):
                    if o != 0:
                        tot = tot + _shift_rows(prv, cur, nxt, o, rows_p)
                cnt = _window_counts(r0, i, t, s, win // 2, (SUBLANES, pg))
                st_s[rs, cols] = tot - cur * cnt
            prv, cur, nxt = _neighbour_chunks(dx_ref, dxp_ref, dxn_ref, r0, t, slice(None), first, last)
            du = w[1:2] * cur
            du += w[0:1] * _shift_rows(prv, cur, nxt, 1, rows_c)
            du += w[2:3] * _shift_rows(prv, cur, nxt, -1, rows_c)
            du += w[3:4] * _shift_rows(prv, cur, nxt, -2, rows_c)
            st_s[rs, pl.ds(p, c)] = du
            uprv, ucur, unxt = _neighbour_chunks(u_ref, up_ref, un_ref, r0, t, slice(None), first, last)
            dcb_ref[...] += cur
            for j, o in enumerate((-1, 0, 1, 2)):
                dcw_ref[j] += cur * _shift_rows(uprv, ucur, unxt, o, rows_c)

        _chunk_loop(t, chunk)
        dproj_ref[:, pl.ds(0, p + c)] = st_s[...].astype(BF16)
        dproj_ref[:, pl.ds(p + c, c)] = dgate_ref[...]

    return pl.pallas_call(
        body, name="dproj", grid=(n_tiles,),
        in_specs=_halo_specs(t, s, p, 0) + _halo_specs(t, s, c, 0) + _halo_specs(t, s, c, 1) + [
            pl.BlockSpec((t, c), lambda i: (i, 0)), pl.BlockSpec((4, c), lambda i: (0, 0))],
        out_specs=[pl.BlockSpec((t, p + 2 * c), lambda i: (i, 0)),
                   pl.BlockSpec((4, SUBLANES, c), lambda i: (0, 0, 0)), pl.BlockSpec((SUBLANES, c), lambda i: (0, 0))],
        out_shape=[jax.ShapeDtypeStruct((s, p + 2 * c), BF16), jax.ShapeDtypeStruct((4, SUBLANES, c), F32),
                   jax.ShapeDtypeStruct((SUBLANES, c), F32)],
        scratch_shapes=[pltpu.VMEM((t, p + c), F32)],
        compiler_params=_params("arbitrary"),
    )(e_pool, e_pool, e_pool, dxc, dxc, dxc, proj, proj, proj, dgate, conv_w)


def _dx_call(dproj, w_in, dz1, dep):
    s, e = dproj.shape
    n, d, e4 = w_in.shape
    tm = min(MM_TILE, s)

    def body(dp_ref, w_hbm, dz_ref, dep_ref, o_ref, w_s, sems):
        @pl.when(pl.program_id(0) == 0)
        def _():
            copies = [pltpu.make_async_copy(w_hbm.at[k], w_s.at[:, pl.ds(k * e4, e4)], sems.at[k]) for k in range(n)]
            for cp in copies:
                cp.start()
            for cp in copies:
                cp.wait()

        o_ref[...] = ALPHA * dz_ref[...] + lax.dot_general(
            dp_ref[...], w_s[...], (((1,), (1,)), ((), ())), preferred_element_type=F32)

    return pl.pallas_call(
        body, name="grad_x", grid=(s // tm,),
        in_specs=[pl.BlockSpec((tm, e), lambda i: (i, 0)), ANY, pl.BlockSpec((tm, d), lambda i: (i, 0)), ANY],
        out_specs=pl.BlockSpec((tm, d), lambda i: (i, 0)),
        out_shape=jax.ShapeDtypeStruct((s, d), F32),
        scratch_shapes=[pltpu.VMEM((d, e), BF16), pltpu.SemaphoreType.DMA((n,))],
        compiler_params=_params("arbitrary"),
    )(dproj, w_in, dz1, dep)


def _row_tile(rows, cols, n_arrays):
    limit = max(SUBLANES, ELT_BLOCK_BYTES // (4 * cols * max(1, n_arrays // 4)))
    best = SUBLANES
    for cand in range(SUBLANES, min(rows, limit) + 1, SUBLANES):
        if rows % cand == 0:
            best = cand
    return best if rows % SUBLANES == 0 else rows


def _cast_to_slot_call(a, idx, dtype, name, dep):
    rows, cols = a.shape
    tr = _row_tile(rows, cols, 2)
    extra = [] if dep is None else [dep]

    def body(idx_ref, a_ref, *rest):
        rest[-1][...] = a_ref[...].astype(dtype)

    return pl.pallas_call(
        body, name=name,
        grid_spec=pltpu.PrefetchScalarGridSpec(
            num_scalar_prefetch=1, grid=(rows // tr,),
            in_specs=[pl.BlockSpec((tr, cols), lambda i, idx_ref: (i, 0))] + [ANY] * len(extra),
            out_specs=pl.BlockSpec((None, tr, cols), lambda i, idx_ref: (idx_ref[1], i, 0))),
        out_shape=jax.ShapeDtypeStruct((N_CHIPS, rows, cols), dtype),
        compiler_params=_params("arbitrary"),
    )(idx, a, *extra)


def _add_half_call(g, recv, idx, name):
    _, rows, cols = g.shape
    tr = _row_tile(rows, cols, 3)

    def body(idx_ref, g_ref, r_ref, o_ref):
        o_ref[...] = g_ref[...] + r_ref[...]

    return pl.pallas_call(
        body, name=name,
        grid_spec=pltpu.PrefetchScalarGridSpec(
            num_scalar_prefetch=1, grid=(rows // tr,),
            in_specs=[pl.BlockSpec((None, tr, cols), lambda i, idx_ref: (idx_ref[0], i, 0)),
                      pl.BlockSpec((tr, cols), lambda i, idx_ref: (i, 0))],
            out_specs=pl.BlockSpec((None, tr, cols), lambda i, idx_ref: (idx_ref[1], i, 0))),
        out_shape=jax.ShapeDtypeStruct((N_CHIPS, rows, cols), F32),
        compiler_params=_params("arbitrary"),
    )(idx, g, recv)


def _sum_chips_call(own, recv, idx, name):
    _, rows, cols = recv.shape
    tr = _row_tile(rows, cols, 5)
    out_spec = pl.BlockSpec((None, tr, cols), lambda i, idx_ref: (idx_ref[0], i, 0))
    if own is None:
        def body(idx_ref, r_ref, o_ref):
            o_ref[...] = ((r_ref[0] + r_ref[1]) + r_ref[2]) + r_ref[3]
        in_specs = [pl.BlockSpec((N_CHIPS, tr, cols), lambda i, idx_ref: (0, i, 0))]
        args = (recv,)
    else:
        def body(idx_ref, p_ref, r_ref, o_ref):
            o_ref[...] = ((p_ref[...] + r_ref[0]) + r_ref[1]) + r_ref[2]
        in_specs = [pl.BlockSpec((None, tr, cols), lambda i, idx_ref: (idx_ref[1], i, 0)),
                    pl.BlockSpec((N_CHIPS - 1, tr, cols), lambda i, idx_ref: (0, i, 0))]
        args = (own, recv)
    return pl.pallas_call(
        body, name=name,
        grid_spec=pltpu.PrefetchScalarGridSpec(num_scalar_prefetch=1, grid=(rows // tr,), in_specs=in_specs, out_specs=out_spec),
        out_shape=jax.ShapeDtypeStruct((2, rows, cols), F32),
        compiler_params=_params("arbitrary"),
    )(idx, *args)


def _adamw_math(g, w, m, v):
    mn = ADAM_B1 * m + (1.0 - ADAM_B1) * g
    vn = ADAM_B2 * v + (1.0 - ADAM_B2) * (g * g)
    m_hat = mn / (1.0 - ADAM_B1 ** ADAM_STEP)
    v_hat = vn / (1.0 - ADAM_B2 ** ADAM_STEP)
    return -ADAM_LR * (m_hat / (jnp.sqrt(v_hat) + ADAM_EPS) + ADAM_WD * w), mn, vn


def _adamw_call(g, w, m, v, name):
    rows, cols = w.shape
    tr = _row_tile(rows, cols, 4)

    def body(g_ref, w_ref, m_ref, v_ref, go_ref, d_ref, mo_ref, vo_ref):
        gv = g_ref[...]
        go_ref[...] = gv
        d_ref[...], mo_ref[...], vo_ref[...] = _adamw_math(gv, w_ref[...], m_ref[...], v_ref[...])

    spec = pl.BlockSpec((tr, cols), lambda i: (i, 0))
    shape = jax.ShapeDtypeStruct((rows, cols), F32)
    return pl.pallas_call(
        body, name=name, grid=(rows // tr,),
        in_specs=[spec] * 4, out_specs=[spec] * 4, out_shape=[shape] * 4,
        compiler_params=_params("arbitrary"),
    )(g, w, m, v)


def _adamw_small_call(gs, ws, ms, vs):
    n = len(ws)

    def body(*refs):
        ins = [refs[k * n:(k + 1) * n] for k in range(4)]
        outs = [refs[(4 + k) * n:(5 + k) * n] for k in range(3)]
        for a in range(n):
            outs[0][a][...], outs[1][a][...], outs[2][a][...] = _adamw_math(*[ins[k][a][...] for k in range(4)])

    whole = pl.BlockSpec(memory_space=pltpu.VMEM)
    res = pl.pallas_call(
        body, name="adamw_small",
        in_specs=[whole] * (4 * n), out_specs=[whole] * (3 * n),
        out_shape=[jax.ShapeDtypeStruct(w.shape, F32) for w in ws] * 3,
        compiler_params=pltpu.CompilerParams(vmem_limit_bytes=VMEM_LIMIT),
    )(*gs, *ws, *ms, *vs)
    return res[:n], res[n:2 * n], res[2 * n:]


def _mesh_place():
    x, y, c = lax.axis_index("x"), lax.axis_index("y"), lax.axis_index("c")
    chips = [(1 - x, y), (x, 1 - y), (1 - x, 1 - y)]
    return x, y, c, chips


def _remote(src, dst, send_sems, recv_sems, idx, device):
    return pltpu.make_async_remote_copy(src_ref=src, dst_ref=dst, send_sem=send_sems.at[idx], recv_sem=recv_sems.at[idx],
                                        device_id=device, device_id_type=MESH)


HBM_SPEC = pl.BlockSpec(memory_space=pltpu.HBM)
SEM_SPEC = pl.BlockSpec(memory_space=pltpu.SEMAPHORE)
ORDERED_EFFECT = pltpu.SideEffectType.DATAFLOW_SIDE_EFFECTING


def _in_hbm(a):
    return pltpu.with_memory_space_constraint(a, pltpu.HBM)


def _start_copies_call(name, bufs, groups, after=None):
    n, g = len(bufs), len(groups)
    extra = [] if after is None else [after]
    first_out = n + len(extra)

    def body(*refs):
        outs = refs[first_out:first_out + n]
        sems = refs[first_out + n:first_out + n + 2 * g]
        token = refs[first_out + n + 2 * g]
        for i, (which, copies_fn, _) in enumerate(groups):
            for mine, _ in copies_fn([outs[w] for w in which], sems[2 * i], sems[2 * i + 1]):
                mine.start()
        token[...] = jnp.zeros_like(token)

    sem_shapes = [pltpu.SemaphoreType.DMA((cnt,)) for _, _, cnt in groups for _ in range(2)]
    res = pl.pallas_call(
        body, name=name,
        in_specs=[HBM_SPEC] * n + [ANY] * len(extra),
        out_specs=[HBM_SPEC] * n + [SEM_SPEC] * (2 * g) + [pl.BlockSpec(memory_space=pltpu.VMEM)],
        out_shape=[pltpu.HBM(a.shape, a.dtype) for a in bufs] + sem_shapes + [jax.ShapeDtypeStruct((SUBLANES, LANES), F32)],
        input_output_aliases={a: a for a in range(n)},
        compiler_params=pltpu.CompilerParams(has_side_effects=ORDERED_EFFECT),
    )(*[_in_hbm(a) for a in bufs], *extra)
    sems = res[n:n + 2 * g]
    return list(res[:n]), [(sems[2 * i], sems[2 * i + 1]) for i in range(g)], res[n + 2 * g]


def _wait_copies_call(name, bufs, sems, copies_fn, after):
    n = len(bufs)

    def body(*refs):
        ins = refs[:n]
        send_sems, recv_sems = refs[n], refs[n + 1]
        for mine, arriving in copies_fn(list(ins), send_sems, recv_sems):
            arriving.wait_recv()
            mine.wait_send()

    res = pl.pallas_call(
        body, name=name,
        in_specs=[HBM_SPEC] * n + [SEM_SPEC, SEM_SPEC, ANY],
        out_specs=[HBM_SPEC] * n,
        out_shape=[pltpu.HBM(a.shape, a.dtype) for a in bufs],
        input_output_aliases={a: a for a in range(n)},
        compiler_params=pltpu.CompilerParams(has_side_effects=ORDERED_EFFECT),
    )(*bufs, sems[0], sems[1], after)
    return list(res)


def _gather_copies(bufs, send_sems, recv_sems):
    x, y, c, chips = _mesh_place()
    k = 2 * x + y
    out = []
    for a, buf in enumerate(bufs):
        for j, (px, py) in enumerate(chips):
            kj = 2 * px + py
            mine = _remote(buf.at[k, c], buf.at[k, c], send_sems, recv_sems, 3 * a + j, (px, py, c))
            arriving = _remote(buf.at[k, c], buf.at[kj, c], send_sems, recv_sems, 3 * a + j, (px, py, c))
            out.append((mine, arriving))
    return out


def _exchange_copies(n_sharded, n_replicated):
    def copies(bufs, send_sems, recv_sems):
        x, y, c, chips = _mesh_place()
        k = 2 * x + y
        sums, lands = bufs[:n_sharded], bufs[n_sharded:2 * n_sharded]
        repl = bufs[2 * n_sharded:]
        out = []
        for j, (px, py) in enumerate(chips):
            kj = 2 * px + py
            for a in range(n_sharded):
                cp = _remote(sums[a].at[kj], lands[a].at[j], send_sems, recv_sems, 3 * a + j, (px, py, c))
                out.append((cp, cp))
            for a in range(n_replicated):
                idx = 3 * (n_sharded + a) + j
                mine = _remote(repl[a].at[k], repl[a].at[k], send_sems, recv_sems, idx, (px, py, c))
                arriving = _remote(repl[a].at[k], repl[a].at[kj], send_sems, recv_sems, idx, (px, py, c))
                out.append((mine, arriving))
        return out
    return copies


def _sibling_copies(n, halves):
    def copies(bufs, send_sems, recv_sems):
        x, y, c, _ = _mesh_place()
        out = []
        for a in range(n):
            src = bufs[a].at[1 - c] if halves else bufs[a]
            cp = _remote(src, bufs[n + a], send_sems, recv_sems, a, (x, y, 1 - c))
            out.append((cp, cp))
        return out
    return copies


def _forward_copies(bufs, send_sems, recv_sems):
    x, y, c, chips = _mesh_place()
    out = []
    for a, buf in enumerate(bufs):
        for j, (px, py) in enumerate(chips):
            kj = 2 * px + py
            mine = _remote(buf.at[kj, c], buf.at[kj, c], send_sems, recv_sems, 3 * a + j, (x, y, 1 - c))
            arriving = _remote(buf.at[kj, c], buf.at[kj, 1 - c], send_sems, recv_sems, 3 * a + j, (x, y, 1 - c))
            out.append((mine, arriving))
    return out


def _join_copies(bufs, send_sems, recv_sems):
    x, y, c, _ = _mesh_place()
    out = []
    for a, buf in enumerate(bufs):
        mine = _remote(buf.at[c], buf.at[c], send_sems, recv_sems, a, (x, y, 1 - c))
        arriving = _remote(buf.at[c], buf.at[1 - c], send_sems, recv_sems, a, (x, y, 1 - c))
        out.append((mine, arriving))
    return out


def _forward_to_sibling_call(bufs, name):
    n = len(bufs)

    def body(*refs):
        ins, outs = refs[:n], refs[n:2 * n]
        send_sems, recv_sems = refs[2 * n:]
        x, y, c, chips = _mesh_place()
        sibling = (x, y, 1 - c)
        sends = []
        for a in range(n):
            for j, (px, py) in enumerate(chips):
                kj = 2 * px + py
                sends.append(_remote(ins[a].at[kj, c], outs[a].at[kj, c], send_sems, recv_sems, 3 * a + j, sibling))
        for cp in sends:
            cp.start()
        for a in range(n):
            for j, (px, py) in enumerate(chips):
                kj = 2 * px + py
                _remote(ins[a].at[kj, c], outs[a].at[kj, 1 - c], send_sems, recv_sems, 3 * a + j, sibling).wait_recv()
        for cp in sends:
            cp.wait_send()

    return pl.pallas_call(
        body, name=name,
        in_specs=[ANY] * n, out_specs=[ANY] * n,
        out_shape=[jax.ShapeDtypeStruct(a.shape, a.dtype) for a in bufs],
        input_output_aliases={a: a for a in range(n)},
        scratch_shapes=[pltpu.SemaphoreType.DMA((3 * n,)), pltpu.SemaphoreType.DMA((3 * n,))],
    )(*bufs)


def _pack(arrays, rows_multiple):
    flat = jnp.concatenate([a.reshape(-1) for a in arrays])
    per = LANES * rows_multiple
    padded = -(-flat.shape[0] // per) * per
    flat = jnp.pad(flat, (0, padded - flat.shape[0]))
    return flat.reshape(-1, LANES)


def _unpack(packed, shapes):
    flat = packed.reshape(-1)
    out, at = [], 0
    for shp in shapes:
        size = 1
        for dim in shp:
            size *= dim
        out.append(flat[at:at + size].reshape(shp))
        at += size
    return out


def _halves(a):
    return a.reshape((2, a.shape[0] // 2) + a.shape[1:])


def kernel(x, ln_mix_g, ln_mix_b, w_in, w_pool, pool_scale, conv_w, conv_b, w_rg_a, b_rg_a, w_rg_i, b_rg_i, rg_lambda, w_out, ln_ffn_g, ln_ffn_b, w_mlp_in, w_mlp_out, loss_target, m_ln_mix_g, m_ln_mix_b, m_w_in, m_w_pool, m_pool_scale, m_conv_w, m_conv_b, m_w_rg_a, m_b_rg_a, m_w_rg_i, m_b_rg_i, m_rg_lambda, m_w_out, m_ln_ffn_g, m_ln_ffn_b, m_w_mlp_in, m_w_mlp_out, v_ln_mix_g, v_ln_mix_b, v_w_in, v_w_pool, v_pool_scale, v_conv_w, v_conv_b, v_w_rg_a, v_b_rg_a, v_w_rg_i, v_b_rg_i, v_rg_lambda, v_w_out, v_ln_ffn_g, v_ln_ffn_b, v_w_mlp_in, v_w_mlp_out):
    weights = dict(ln_mix_g=ln_mix_g, ln_mix_b=ln_mix_b, w_in=w_in, w_pool=w_pool, pool_scale=pool_scale, conv_w=conv_w,
                   conv_b=conv_b, w_rg_a=w_rg_a, b_rg_a=b_rg_a, w_rg_i=w_rg_i, b_rg_i=b_rg_i, rg_lambda=rg_lambda,
                   w_out=w_out, ln_ffn_g=ln_ffn_g, ln_ffn_b=ln_ffn_b, w_mlp_in=w_mlp_in, w_mlp_out=w_mlp_out)
    m_in = dict(ln_mix_g=m_ln_mix_g, ln_mix_b=m_ln_mix_b, w_in=m_w_in, w_pool=m_w_pool, pool_scale=m_pool_scale,
                conv_w=m_conv_w, conv_b=m_conv_b, w_rg_a=m_w_rg_a, b_rg_a=m_b_rg_a, w_rg_i=m_w_rg_i, b_rg_i=m_b_rg_i,
                rg_lambda=m_rg_lambda, w_out=m_w_out, ln_ffn_g=m_ln_ffn_g, ln_ffn_b=m_ln_ffn_b, w_mlp_in=m_w_mlp_in,
                w_mlp_out=m_w_mlp_out)
    v_in = dict(ln_mix_g=v_ln_mix_g, ln_mix_b=v_ln_mix_b, w_in=v_w_in, w_pool=v_w_pool, pool_scale=v_pool_scale,
                conv_w=v_conv_w, conv_b=v_conv_b, w_rg_a=v_w_rg_a, b_rg_a=v_b_rg_a, w_rg_i=v_w_rg_i, b_rg_i=v_b_rg_i,
                rg_lambda=v_rg_lambda, w_out=v_w_out, ln_ffn_g=v_ln_ffn_g, ln_ffn_b=v_ln_ffn_b, w_mlp_in=v_w_mlp_in,
                w_mlp_out=v_w_mlp_out)
    names = list(weights)

    xs = x[0]
    tgt = loss_target[0]
    s, d = xs.shape
    p = c = d // 2
    pg = p // N_POOL_GROUPS
    core = lax.axis_index("c")
    shard = 2 * lax.axis_index("x") + lax.axis_index("y")

    idx = jnp.stack([core, shard]).astype(jnp.int32)
    small_shard = _pack([conv_w[0], b_rg_a[0], b_rg_i[0], rg_lambda[0]], 2 * SUBLANES)
    to_gather = [(w_in[0], BF16), (w_out[0], BF16), (w_mlp_in[0], BF16), (w_mlp_out[0], BF16),
                 (w_pool[0].reshape(-1, pg), BF16), (small_shard, F32)]

    def slot_view(i, dep):
        a, dt = to_gather[i]
        sl = _cast_to_slot_call(a, idx, dt, f"gather_slot_{i}", dep)
        return sl.reshape(N_CHIPS, 2, sl.shape[1] // 2, sl.shape[2])

    first, later = (0, 4, 5), (1, 2, 3)
    fly_a, sems_a, token_a = _start_copies_call(
        "gather_start_first", [slot_view(i, None) for i in first], [((0, 1, 2), _gather_copies, 3 * len(first))])
    later_views = []
    for i in later:
        later_views.append(slot_view(i, later_views[-1] if later_views else token_a))
    xb = _cast_call(xs, later_views[-1])
    got_first = _wait_copies_call("gather_wait_w_in", fly_a, sems_a[0], _gather_copies, xb)
    fly_b, sems_b, g_token = _start_copies_call(
        "gather_start_later", later_views + got_first,
        [((0,), _gather_copies, 3), ((1,), _gather_copies, 3), ((2,), _gather_copies, 3)])
    got_first = fly_b[len(later):]
    in_flight = dict(zip(later, fly_b))
    g_sems = [None] + list(sems_b)

    def arrive(which, group, after, tag):
        return _wait_copies_call(f"gather_wait_{tag}", [in_flight[w] for w in which], g_sems[group], _gather_copies, after)

    def pass_on(got, tag):
        flying, sems, token = _start_copies_call(
            f"gather_forward_start_{tag}", got, [(tuple(range(len(got))), _forward_copies, 3 * len(got))])
        return (flying, sems[0], tag), token

    def passed_on(state, after):
        flying, sems, tag = state
        return _wait_copies_call(f"gather_forward_wait_{tag}", flying, sems, _forward_copies, after)

    gathered = [None] * len(to_gather)
    gathered[0], gathered[4], gathered[5] = _forward_to_sibling_call(got_first, "gather_forward_w_in")
    w_in_f = gathered[0].reshape((N_CHIPS,) + w_in.shape[1:])
    w_pool_f = gathered[4].reshape(N_CHIPS, N_POOL_GROUPS, pg // N_CHIPS, pg).transpose(1, 0, 2, 3).reshape(N_POOL_GROUPS, pg, pg)
    c4 = c // N_CHIPS
    small_parts = [_unpack(gathered[5][k].reshape(-1, LANES), [(4, c4), (2, c4), (2, c4), (2, c4)]) for k in range(N_CHIPS)]
    conv_w_f = jnp.concatenate([sp_[0] for sp_ in small_parts], axis=1)
    b_a_f = jnp.concatenate([sp_[1] for sp_ in small_parts], axis=1)
    b_i_f = jnp.concatenate([sp_[2] for sp_ in small_parts], axis=1)
    lam_f = jnp.concatenate([sp_[3] for sp_ in small_parts], axis=1)
    wa_b = w_rg_a[0].astype(BF16)
    wi_b = w_rg_i[0].astype(BF16)

    proj = _proj_call(xb, w_in_f)
    xc = _conv_call(proj, conv_w_f, conv_b, c)
    fwd_w_out, token = pass_on(arrive((1,), 1, xc, "w_out"), "w_out")
    h_b, *gates_b = _scan_fwd_call(xc, wa_b[1], wi_b[1], b_a_f[1:2], b_i_f[1:2], lam_f[1:2], True, token, None)
    h_f, *gates_f, h_sum = _scan_fwd_call(xc, wa_b[0], wi_b[0], b_a_f[0:1], b_i_f[0:1], lam_f[0:1], False, token, h_b)
    y, d_pool = _pool_combine_call(proj, h_sum, w_pool_f, pool_scale, p)
    w_out_f = passed_on(fwd_w_out, y)[0].reshape(d, d)
    fwd_w1, token = pass_on(arrive((2,), 2, y, "w_mlp_in"), "w_mlp_in")
    xh1, x1b, rstd1 = _out_ln1_call(y, w_out_f, xs, ln_mix_g, ln_mix_b, token)
    w1_f = passed_on(fwd_w1, x1b)[0].reshape((N_CHIPS,) + w_mlp_in.shape[1:])
    first_half = _mlp_in_call(x1b, w1_f, g_token, None)
    fwd_w2, token = pass_on(arrive((3,), 3, first_half[0], "w_mlp_out"), "w_mlp_out")
    r_act, hsq = _mlp_in_call(x1b, w1_f, token, first_half)
    w2_f = passed_on(fwd_w2, hsq)[0].reshape(N_CHIPS * w_mlp_out.shape[1], d)
    dz2, dz2b, loss8, dg2, db2 = _mlp_out_ln2_call(hsq, w2_f, xh1, ln_mix_g, ln_mix_b, ln_ffn_g, ln_ffn_b, tgt)

    def start_siblings(grads, halves, tag, after=None):
        lands = [lax.empty(g.shape[1:] if halves else g.shape, g.dtype) for g in grads]
        copies = _sibling_copies(len(grads), halves)
        flying, sems, token = _start_copies_call(
            f"siblings_start_{tag}", list(grads) + lands, [(tuple(range(2 * len(grads))), copies, len(grads))], after)
        return (flying, sems[0], copies, len(grads), tag), token

    def finish_siblings(state, after):
        flying, sems, copies, n, tag = state
        got = _wait_copies_call(f"siblings_wait_{tag}", flying, sems, copies, after)
        return got[:n], got[n:]

    half_own = jnp.reshape(core, (1,)).astype(jnp.int32)
    half_sibling = 1 - half_own

    def chip_sum_of(a, b, row_sharded, tag, dep, overlapped):
        for_sibling = _half_grad_call(a, b, half_sibling, row_sharded, None, f"grad_{tag}_for_sibling", dep)
        state, token = start_siblings([for_sibling], False, tag)
        results = overlapped(token)
        _, (from_sibling,) = finish_siblings(state, results[0])
        return _half_grad_call(a, b, half_own, row_sharded, from_sibling, f"grad_{tag}", token), results

    def start_exchange(sums, n_repl, tag):
        n_sh = len(sums) - n_repl
        lands = [lax.empty((N_CHIPS - 1,) + a.shape[1:], a.dtype) for a in sums[:n_sh]]
        bufs = sums[:n_sh] + lands + sums[n_sh:]
        copies = _exchange_copies(n_sh, n_repl)
        flying, sems, token = _start_copies_call(
            f"reduce_start_{tag}", bufs, [(tuple(range(len(bufs))), copies, 3 * len(sums))])
        return (flying, sems[0], copies, n_sh, tag), token

    def finish_exchange(state, after):
        flying, sems, copies, n_sh, tag = state
        got = _wait_copies_call(f"reduce_wait_{tag}", flying, sems, copies, after)
        halves = []
        for a in range(n_sh):
            own, land = got[a], got[n_sh + a]
            cols = own.shape[-1]
            total = _sum_chips_call(own.reshape(N_CHIPS, -1, cols), land.reshape(N_CHIPS - 1, -1, cols), idx,
                                    f"reduce_sum_{tag}_{a}")
            halves.append(total.reshape((2,) + own.shape[1:]))
        for a, rp in enumerate(got[2 * n_sh:]):
            halves.append(_sum_chips_call(None, rp, idx, f"reduce_sum_{tag}_r{a}"))
        return halves

    def start_join(halves, tag):
        flying, sems, token = _start_copies_call(
            f"join_start_{tag}", halves, [(tuple(range(len(halves))), _join_copies, len(halves))])
        return (flying, sems[0], tag), token

    def finish_join(state, after):
        flying, sems, tag = state
        return _wait_copies_call(f"join_wait_{tag}", flying, sems, _join_copies, after)

    sum_w2, (dpre,) = chip_sum_of(hsq, dz2b, True, "w_mlp_out", g_token,
                                  lambda tok: (_dhsq_call(dz2b, w2_f, r_act, tok),))
    flying_w2, token = start_exchange([sum_w2], 0, "w2")
    sum_w1, (dz1, dz1b, dg1, db1) = chip_sum_of(
        x1b, dpre, False, "w_mlp_in", token,
        lambda tok: _dx1_ln1_bwd_call(dpre, w1_f, dz2, xh1, rstd1, ln_mix_g, tok))
    flying_w1, token = start_exchange([sum_w1], 0, "w1")

    sum_wout, (e_pool, dh, dgate, g_wpool, g_pscale8) = chip_sum_of(
        y, dz1b, True, "w_out", token,
        lambda tok: _mixer_bwd_call(dz1b, w_out_f, d_pool, proj, h_sum, w_pool_f, pool_scale, p, tok))
    flying_wout, token = start_exchange([sum_wout], 0, "w_out")
    dxc0, g_wa0, g_wi0, g_ba0, g_bi0, g_sp0 = _scan_bwd_call(
        xc, dh, h_f, gates_f, None, wa_b[0], wi_b[0], lam_f[0:1], False, token)
    dxc, g_wa1, g_wi1, g_ba1, g_bi1, g_sp1 = _scan_bwd_call(
        xc, dh, h_b, gates_b, dxc0, wa_b[1], wi_b[1], lam_f[1:2], True, token)
    dproj, g_cw8, g_cb8 = _dproj_call(e_pool, dxc, proj, dgate, conv_w_f, p)

    rowsum = lambda a8: jnp.sum(a8, axis=-2)
    g_lam = jnp.stack([rowsum(g_sp0), rowsum(g_sp1)]) * (-_sigmoid(-lam_f))
    small_grads = {
        "ln_mix_g": rowsum(dg1), "ln_mix_b": rowsum(db1), "ln_ffn_g": rowsum(dg2), "ln_ffn_b": rowsum(db2),
        "pool_scale": rowsum(g_pscale8), "conv_b": rowsum(g_cb8),
        "w_rg_a": jnp.stack([g_wa0, g_wa1]), "w_rg_i": jnp.stack([g_wi0, g_wi1]),
        "w_pool": g_wpool, "conv_w": rowsum(g_cw8),
        "b_rg_a": jnp.stack([rowsum(g_ba0), rowsum(g_ba1)]), "b_rg_i": jnp.stack([rowsum(g_bi0), rowsum(g_bi1)]),
        "rg_lambda": g_lam,
    }
    small_names = list(small_grads)
    small_shapes = [small_grads[nm].shape for nm in small_names]
    loss_share = jnp.reshape(jnp.sum(loss8) * (0.5 / d), (1,))
    g_small = _halves(_pack([small_grads[nm] for nm in small_names] + [loss_share], 2 * SUBLANES))
    sib_small, token = start_siblings([g_small], True, "small")
    flying_small = []

    def small_exchange_and_grad_x(tok):
        (mine,), (theirs,) = finish_siblings(sib_small, tok)
        small_sum = _add_half_call(mine, theirs, idx, "reduce_add_small")
        state, tok = start_exchange([small_sum], 1, "small")
        flying_small.append(state)
        return (_dx_call(dproj, w_in_f, dz1, tok),)

    sum_win, (grad_x,) = chip_sum_of(xb, dproj, False, "w_in", token, small_exchange_and_grad_x)
    flying_small = flying_small[0]
    flying_win, token = start_exchange([sum_win], 0, "w_in")

    grad_w, delta_w, new_m, new_v = {}, {}, {}, {}

    def adamw(nm, full):
        w2d = weights[nm][0]
        g2d = full.reshape(w2d.shape)
        go, dl, mn, vn = _adamw_call(g2d, w2d, m_in[nm][0], v_in[nm][0], f"adamw_{nm}")
        grad_w[nm], delta_w[nm], new_m[nm], new_v[nm] = go[None], dl[None], mn[None], vn[None]
        return vn

    join_w2, token = start_join(finish_exchange(flying_w2, token), "w2")
    join_w1, token = start_join(finish_exchange(flying_w1, token), "w1")
    join_wout, token = start_join(finish_exchange(flying_wout, token), "w_out")
    last = adamw("w_mlp_out", finish_join(join_w2, token)[0])
    last = adamw("w_mlp_in", finish_join(join_w1, last)[0])
    last = adamw("w_out", finish_join(join_wout, last)[0])

    join_small, token = start_join(finish_exchange(flying_small, last), "small")
    join_win, token = start_join(finish_exchange(flying_win, token), "w_in")
    small_joined = finish_join(join_small, token)[0]
    *small_sums, loss_sum = _unpack(small_joined.reshape(-1, LANES), small_shapes + [(1,)])
    small_full = dict(zip(small_names, small_sums))
    local = dict(small_full)
    local["w_pool"] = lax.dynamic_slice_in_dim(small_full["w_pool"], shard * (pg // N_CHIPS), pg // N_CHIPS, axis=1)
    for nm in ("conv_w", "b_rg_a", "b_rg_i", "rg_lambda"):
        local[nm] = lax.dynamic_slice_in_dim(small_full[nm], shard * c4, c4, axis=1)
    small_g = [local[nm].reshape(weights[nm].shape) for nm in small_names]
    small_d, small_m, small_v = _adamw_small_call(
        small_g, [weights[nm] for nm in small_names], [m_in[nm] for nm in small_names], [v_in[nm] for nm in small_names])
    for nm, gl, dl, mn, vn in zip(small_names, small_g, small_d, small_m, small_v):
        grad_w[nm], delta_w[nm], new_m[nm], new_v[nm] = gl, dl, mn, vn
    adamw("w_in", finish_join(join_win, small_v[0])[0])

    loss = loss_sum[0]
    return (loss, grad_x[None], *[grad_w[nm] for nm in names], *[delta_w[nm] for nm in names],
            *[new_m[nm] for nm in names], *[new_v[nm] for nm in names])
```
